```python
import jax
import jax.numpy as jnp
from jax import lax
import numpy as np

D_MODEL = 1024
BATCH = 8
SEQ = 4096
DEPTH = 2

N_META = 16
D_MIX = D_MODEL
POOL_WIDTH = D_MIX // 2
POOL_WINDOWS = (2, 4, 8, 16)
N_POOL_GROUPS = len(POOL_WINDOWS)
POOL_GROUP_DIM = POOL_WIDTH // N_POOL_GROUPS
MLA_HEADS = 8
QK_NOPE_DIM = 64
QK_ROPE_DIM = 32
QK_HEAD_DIM = QK_NOPE_DIM + QK_ROPE_DIM
V_HEAD_DIM = (D_MIX - POOL_WIDTH) // MLA_HEADS
Q_LORA_RANK = 384
KV_LORA_RANK = 256
IN_WIDTH = POOL_WIDTH + Q_LORA_RANK + KV_LORA_RANK + QK_ROPE_DIM
D_FF = -(-8 * D_MODEL // (3 * 256)) * 256
ROPE_THETA = 10000.0
RMS_EPS = 1e-6
Q_BLOCK = 128

kernel_name = "hymba_pool_mla_hybrid"


def rms_norm(x, g):
    xf = x.astype(jnp.float32)
    y = xf * lax.rsqrt(jnp.mean(xf * xf, axis=-1, keepdims=True) + RMS_EPS)
    return (y * g.astype(jnp.float32)).astype(x.dtype)


def rope_tables(length):
    inv = 1.0 / (ROPE_THETA ** (jnp.arange(0, QK_ROPE_DIM, 2, dtype=jnp.float32) / QK_ROPE_DIM))
    ang = jnp.arange(length, dtype=jnp.float32)[:, None] * inv[None, :]
    return jnp.cos(ang), jnp.sin(ang)


def apply_rope(x, cos, sin):
    x1, x2 = jnp.split(x, 2, axis=-1)
    c = cos.astype(x.dtype)
    s = sin.astype(x.dtype)
    return jnp.concatenate([x1 * c - x2 * s, x1 * s + x2 * c], axis=-1)


def multiscale_pool(u, w_pool, pool_scale):
    B, L, _ = u.shape
    uf = u.astype(jnp.float32)
    cs = jnp.concatenate([jnp.zeros((B, 1, POOL_WIDTH), jnp.float32), lax.cumsum(uf, axis=1)], axis=1)
    t = jnp.arange(L)
    outs = []
    for g, w in enumerate(POOL_WINDOWS):
        sl = slice(g * POOL_GROUP_DIM, (g + 1) * POOL_GROUP_DIM)
        csg = cs[:, :, sl]
        start = jnp.maximum(t + 1 - w, 0)
        count = (t + 1 - start).astype(jnp.float32)
        mean = (csg[:, 1:] - csg[:, start]) / count[None, :, None]
        outs.append(mean - uf[:, :, sl])
    p = jnp.stack(outs, axis=2).astype(u.dtype)
    y = jnp.einsum('blgc,gcd->blgd', p, w_pool).reshape(B, L, POOL_WIDTH)
    return y * pool_scale.astype(y.dtype)


def causal_mla_attention(q_nope, q_rope, k_nope, k_rope, v):
    B, L, H, _ = q_nope.shape
    n_blocks = -(-L // Q_BLOCK)
    Lp = n_blocks * Q_BLOCK
    pad = Lp - L

    def padl(a):
        return jnp.pad(a, [(0, 0), (0, pad)] + [(0, 0)] * (a.ndim - 2))

    q_nope, q_rope, k_nope, k_rope, v = (padl(a) for a in (q_nope, q_rope, k_nope, k_rope, v))

    def blocks(a):
        return jnp.moveaxis(a.reshape(B, n_blocks, Q_BLOCK, *a.shape[2:]), 1, 0)

    qn_b, qr_b = blocks(q_nope), blocks(q_rope)
    key_pos = jnp.arange(Lp)
    scale = QK_HEAD_DIM ** -0.5

    def one_block(args):
        i, qn, qr = args
        s = (jnp.einsum('bqhd,bkhd->bhqk', qn, k_nope)
             + jnp.einsum('bqhr,bkr->bhqk', qr, k_rope)).astype(jnp.float32) * scale
        q_pos = i * Q_BLOCK + jnp.arange(Q_BLOCK)
        mask = key_pos[None, :] <= q_pos[:, None]
        s = jnp.where(mask[None, None], s, -jnp.inf)
        p = jax.nn.softmax(s, axis=-1).astype(v.dtype)
        return jnp.einsum('bhqk,bkhd->bqhd', p, v)

    out = lax.map(one_block, (jnp.arange(n_blocks), qn_b, qr_b))
    return jnp.moveaxis(out, 0, 1).reshape(B, Lp, H, V_HEAD_DIM)[:, :L]


def mla_group(c_q, c_kv, k_rope_raw, cos, sin, q_a_norm_g, w_q_b, kv_a_norm_g, w_kv_b, q_norm_g, k_norm_g):
    B, L, _ = c_q.shape
    q = (rms_norm(c_q, q_a_norm_g) @ w_q_b).reshape(B, L, MLA_HEADS, QK_HEAD_DIM)
    kv = (rms_norm(c_kv, kv_a_norm_g) @ w_kv_b).reshape(B, L, MLA_HEADS, QK_NOPE_DIM + V_HEAD_DIM)
    q_nope, q_rope = q[..., :QK_NOPE_DIM], q[..., QK_NOPE_DIM:]
    k_nope, v = kv[..., :QK_NOPE_DIM], kv[..., QK_NOPE_DIM:]
    q_nope = rms_norm(q_nope, q_norm_g[:QK_NOPE_DIM])
    q_rope = rms_norm(q_rope, q_norm_g[QK_NOPE_DIM:])
    k_nope = rms_norm(k_nope, k_norm_g[:QK_NOPE_DIM])
    k_rope = rms_norm(k_rope_raw, k_norm_g[QK_NOPE_DIM:])
    q_rope = apply_rope(q_rope, cos[:, None, :], sin[:, None, :])
    k_rope = apply_rope(k_rope, cos, sin)
    o = causal_mla_attention(q_nope, q_rope, k_nope, k_rope, v)
    return o.reshape(B, L, MLA_HEADS * V_HEAD_DIM)


def _fwd_setup_inputs(seed: int = 0) -> dict:
    key = jax.random.key(seed)
    ks = jax.random.split(key, 18)
    f32 = jnp.float32
    nl = DEPTH

    def nrm(k, shape, scale):
        return jax.random.normal(k, shape, f32) * scale

    def gain(k, shape):
        return 1.0 + 0.05 * jax.random.normal(k, shape, f32)

    return {
        'x': nrm(ks[0], (BATCH, SEQ, D_MODEL), 1.0),
        'meta_tokens': nrm(ks[1], (N_META, D_MODEL), 1.0),
        'attn_norm_g': gain(ks[2], (nl, D_MODEL)),
        'w_in': nrm(ks[3], (nl, D_MODEL, IN_WIDTH), D_MODEL ** -0.5),
        'w_pool': nrm(ks[4], (nl, N_POOL_GROUPS, POOL_GROUP_DIM, POOL_GROUP_DIM), POOL_GROUP_DIM ** -0.5),
        'pool_scale': gain(ks[5], (nl, POOL_WIDTH)),
        'q_a_norm_g': gain(ks[6], (nl, Q_LORA_RANK)),
        'w_q_b': nrm(ks[7], (nl, Q_LORA_RANK, MLA_HEADS * QK_HEAD_DIM), Q_LORA_RANK ** -0.5),
        'kv_a_norm_g': gain(ks[8], (nl, KV_LORA_RANK)),
        'w_kv_b': nrm(ks[9], (nl, KV_LORA_RANK, MLA_HEADS * (QK_NOPE_DIM + V_HEAD_DIM)), KV_LORA_RANK ** -0.5),
        'q_norm_g': gain(ks[10], (nl, QK_HEAD_DIM)),
        'k_norm_g': gain(ks[11], (nl, QK_HEAD_DIM)),
        'w_out': nrm(ks[12], (nl, D_MIX, D_MODEL), D_MIX ** -0.5),
        'ffn_norm_g': gain(ks[13], (nl, D_MODEL)),
        'w_gate': nrm(ks[14], (nl, D_MODEL, D_FF), D_MODEL ** -0.5),
        'w_up': nrm(ks[15], (nl, D_MODEL, D_FF), D_MODEL ** -0.5),
        'w_down': nrm(ks[16], (nl, D_FF, D_MODEL), D_FF ** -0.5),
    }


def _fwd_reference(x, meta_tokens, attn_norm_g, w_in, w_pool, pool_scale, q_a_norm_g, w_q_b,
              kv_a_norm_g, w_kv_b, q_norm_g, k_norm_g, w_out, ffn_norm_g, w_gate, w_up, w_down):
    B = x.shape[0]
    meta = jnp.broadcast_to(meta_tokens.astype(x.dtype)[None], (B, N_META, D_MODEL))
    h = jnp.concatenate([meta, x], axis=1)
    L = h.shape[1]
    cos, sin = rope_tables(L)
    s1 = POOL_WIDTH
    s2 = s1 + Q_LORA_RANK
    s3 = s2 + KV_LORA_RANK
    for l in range(DEPTH):
        z = rms_norm(h, attn_norm_g[l]) @ w_in[l]
        u_pool, c_q, c_kv, k_rope = jnp.split(z, [s1, s2, s3], axis=-1)
        y_pool = multiscale_pool(u_pool, w_pool[l], pool_scale[l])
        y_mla = mla_group(c_q, c_kv, k_rope, cos, sin, q_a_norm_g[l], w_q_b[l],
                          kv_a_norm_g[l], w_kv_b[l], q_norm_g[l], k_norm_g[l])
        h = h + jnp.concatenate([y_pool, y_mla], axis=-1) @ w_out[l]
        g = rms_norm(h, ffn_norm_g[l])
        h = h + (jax.nn.silu(g @ w_gate[l]) * (g @ w_up[l])) @ w_down[l]
    return h[:, N_META:]


import jax as _jax
import jax.numpy as _jnp

TWIN_FORMAT = 'train_step'
FWD_PARAMS = ['x', 'meta_tokens', 'attn_norm_g', 'w_in', 'w_pool', 'pool_scale', 'q_a_norm_g', 'w_q_b', 'kv_a_norm_g', 'w_kv_b', 'q_norm_g', 'k_norm_g', 'w_out', 'ffn_norm_g', 'w_gate', 'w_up', 'w_down']
TWIN_WEIGHTS = ['meta_tokens', 'attn_norm_g', 'w_in', 'w_pool', 'pool_scale', 'q_a_norm_g', 'w_q_b', 'kv_a_norm_g', 'w_kv_b', 'q_norm_g', 'k_norm_g', 'w_out', 'ffn_norm_g', 'w_gate', 'w_up', 'w_down']
TWIN_DIFF_INPUT = 'x'
TWIN_INPUTS = ['x', 'meta_tokens', 'attn_norm_g', 'w_in', 'w_pool', 'pool_scale', 'q_a_norm_g', 'w_q_b', 'kv_a_norm_g', 'w_kv_b', 'q_norm_g', 'k_norm_g', 'w_out', 'ffn_norm_g', 'w_gate', 'w_up', 'w_down', 'loss_target', 'm_meta_tokens', 'm_attn_norm_g', 'm_w_in', 'm_w_pool', 'm_pool_scale', 'm_q_a_norm_g', 'm_w_q_b', 'm_kv_a_norm_g', 'm_w_kv_b', 'm_q_norm_g', 'm_k_norm_g', 'm_w_out', 'm_ffn_norm_g', 'm_w_gate', 'm_w_up', 'm_w_down', 'v_meta_tokens', 'v_attn_norm_g', 'v_w_in', 'v_w_pool', 'v_pool_scale', 'v_q_a_norm_g', 'v_w_q_b', 'v_kv_a_norm_g', 'v_w_kv_b', 'v_q_norm_g', 'v_k_norm_g', 'v_w_out', 'v_ffn_norm_g', 'v_w_gate', 'v_w_up', 'v_w_down']
TWIN_OUTPUTS = ['loss', 'grad_x', 'grad_meta_tokens', 'grad_attn_norm_g', 'grad_w_in', 'grad_w_pool', 'grad_pool_scale', 'grad_q_a_norm_g', 'grad_w_q_b', 'grad_kv_a_norm_g', 'grad_w_kv_b', 'grad_q_norm_g', 'grad_k_norm_g', 'grad_w_out', 'grad_ffn_norm_g', 'grad_w_gate', 'grad_w_up', 'grad_w_down', 'delta_meta_tokens', 'delta_attn_norm_g', 'delta_w_in', 'delta_w_pool', 'delta_pool_scale', 'delta_q_a_norm_g', 'delta_w_q_b', 'delta_kv_a_norm_g', 'delta_w_kv_b', 'delta_q_norm_g', 'delta_k_norm_g', 'delta_w_out', 'delta_ffn_norm_g', 'delta_w_gate', 'delta_w_up', 'delta_w_down', 'new_m_meta_tokens', 'new_m_attn_norm_g', 'new_m_w_in', 'new_m_w_pool', 'new_m_pool_scale', 'new_m_q_a_norm_g', 'new_m_w_q_b', 'new_m_kv_a_norm_g', 'new_m_w_kv_b', 'new_m_q_norm_g', 'new_m_k_norm_g', 'new_m_w_out', 'new_m_ffn_norm_g', 'new_m_w_gate', 'new_m_w_up', 'new_m_w_down', 'new_v_meta_tokens', 'new_v_attn_norm_g', 'new_v_w_in', 'new_v_w_pool', 'new_v_pool_scale', 'new_v_q_a_norm_g', 'new_v_w_q_b', 'new_v_kv_a_norm_g', 'new_v_w_kv_b', 'new_v_q_norm_g', 'new_v_k_norm_g', 'new_v_w_out', 'new_v_ffn_norm_g', 'new_v_w_gate', 'new_v_w_up', 'new_v_w_down']
TWIN_LEAF_KINDS = {'loss': 'loss', 'grad_x': 'grad_x', 'grad_meta_tokens': 'grad_w', 'grad_attn_norm_g': 'grad_w', 'grad_w_in': 'grad_w', 'grad_w_pool': 'grad_w', 'grad_pool_scale': 'grad_w', 'grad_q_a_norm_g': 'grad_w', 'grad_w_q_b': 'grad_w', 'grad_kv_a_norm_g': 'grad_w', 'grad_w_kv_b': 'grad_w', 'grad_q_norm_g': 'grad_w', 'grad_k_norm_g': 'grad_w', 'grad_w_out': 'grad_w', 'grad_ffn_norm_g': 'grad_w', 'grad_w_gate': 'grad_w', 'grad_w_up': 'grad_w', 'grad_w_down': 'grad_w', 'delta_meta_tokens': 'delta_w', 'delta_attn_norm_g': 'delta_w', 'delta_w_in': 'delta_w', 'delta_w_pool': 'delta_w', 'delta_pool_scale': 'delta_w', 'delta_q_a_norm_g': 'delta_w', 'delta_w_q_b': 'delta_w', 'delta_kv_a_norm_g': 'delta_w', 'delta_w_kv_b': 'delta_w', 'delta_q_norm_g': 'delta_w', 'delta_k_norm_g': 'delta_w', 'delta_w_out': 'delta_w', 'delta_ffn_norm_g': 'delta_w', 'delta_w_gate': 'delta_w', 'delta_w_up': 'delta_w', 'delta_w_down': 'delta_w', 'new_m_meta_tokens': 'new_m', 'new_m_attn_norm_g': 'new_m', 'new_m_w_in': 'new_m', 'new_m_w_pool': 'new_m', 'new_m_pool_scale': 'new_m', 'new_m_q_a_norm_g': 'new_m', 'new_m_w_q_b': 'new_m', 'new_m_kv_a_norm_g': 'new_m', 'new_m_w_kv_b': 'new_m', 'new_m_q_norm_g': 'new_m', 'new_m_k_norm_g': 'new_m', 'new_m_w_out': 'new_m', 'new_m_ffn_norm_g': 'new_m', 'new_m_w_gate': 'new_m', 'new_m_w_up': 'new_m', 'new_m_w_down': 'new_m', 'new_v_meta_tokens': 'new_v', 'new_v_attn_norm_g': 'new_v', 'new_v_w_in': 'new_v', 'new_v_w_pool': 'new_v', 'new_v_pool_scale': 'new_v', 'new_v_q_a_norm_g': 'new_v', 'new_v_w_q_b': 'new_v', 'new_v_kv_a_norm_g': 'new_v', 'new_v_w_kv_b': 'new_v', 'new_v_q_norm_g': 'new_v', 'new_v_k_norm_g': 'new_v', 'new_v_w_out': 'new_v', 'new_v_ffn_norm_g': 'new_v', 'new_v_w_gate': 'new_v', 'new_v_w_up': 'new_v', 'new_v_w_down': 'new_v'}


def _forward(args):
    return _fwd_reference(*[args[k] for k in FWD_PARAMS])


def _output_shape():
    def fwd():
        inp = _fwd_setup_inputs(0)
        return _fwd_reference(*[inp[k] for k in FWD_PARAMS])
    out = _jax.eval_shape(fwd)
    return out.shape, out.dtype

N_MICROBATCH = 1
ADAM_LR = 0.001
ADAM_B1 = 0.9
ADAM_B2 = 0.999
ADAM_EPS = 1e-08
ADAM_WD = 0.01
ADAM_STEP = 10
PER_EXAMPLE_BATCH_AXIS = {'x': 0, 'loss_target': 0}
SHARED_INPUTS = []
_WEIGHT_DTYPES = {'meta_tokens': _jnp.float32, 'attn_norm_g': _jnp.float32, 'w_in': _jnp.float32, 'w_pool': _jnp.float32, 'pool_scale': _jnp.float32, 'q_a_norm_g': _jnp.float32, 'w_q_b': _jnp.float32, 'kv_a_norm_g': _jnp.float32, 'w_kv_b': _jnp.float32, 'q_norm_g': _jnp.float32, 'k_norm_g': _jnp.float32, 'w_out': _jnp.float32, 'ffn_norm_g': _jnp.float32, 'w_gate': _jnp.float32, 'w_up': _jnp.float32, 'w_down': _jnp.float32}
MOMENT_SCALE = {'meta_tokens': 4.115123e-02, 'attn_norm_g': 1.254678e+01, 'w_in': 1.081392e+00, 'w_pool': 2.781664e+00, 'pool_scale': 2.573082e+01, 'q_a_norm_g': 1.015295e-01, 'w_q_b': 7.358152e-02, 'kv_a_norm_g': 6.342335e-01, 'w_kv_b': 1.348215e-01, 'q_norm_g': 1.017721e+00, 'k_norm_g': 1.020734e+00, 'w_out': 1.240808e+00, 'ffn_norm_g': 2.490757e+01, 'w_gate': 2.247953e-01, 'w_up': 2.830057e-01, 'w_down': 4.451565e-01}


def _to_microbatches(a, axis):
    t = _jnp.moveaxis(a, axis, 0)
    t = t.reshape((N_MICROBATCH, t.shape[0] // N_MICROBATCH) + t.shape[1:])
    return _jnp.moveaxis(t, 1, axis + 1)


def setup_inputs(seed: int = 0) -> dict:
    inp = _fwd_setup_inputs(seed)
    key = _jax.random.fold_in(_jax.random.key(seed), 7919)
    shape, _ = _output_shape()
    out = dict(inp)
    out["loss_target"] = _jax.random.normal(_jax.random.fold_in(key, 0), shape, _jnp.float32)
    for i, name in enumerate(TWIN_WEIGHTS):
        w = inp[name].astype(_jnp.float32)
        if MOMENT_SCALE is None:
            s = _jnp.sqrt(_jnp.mean(_jnp.square(w)) + 1e-30)
        else:
            s = MOMENT_SCALE[name]
        km, kv = _jax.random.split(_jax.random.fold_in(key, i + 1))
        out[name] = w
        out["m_" + name] = s * _jax.random.normal(km, w.shape, _jnp.float32)
        out["v_" + name] = (s * s) * _jax.random.uniform(kv, w.shape, _jnp.float32, 0.5, 1.5)
    if N_MICROBATCH > 1:
        for name, axis in PER_EXAMPLE_BATCH_AXIS.items():
            out[name] = _to_microbatches(out[name], axis)
    return {'x': out['x'], 'meta_tokens': out['meta_tokens'], 'attn_norm_g': out['attn_norm_g'], 'w_in': out['w_in'], 'w_pool': out['w_pool'], 'pool_scale': out['pool_scale'], 'q_a_norm_g': out['q_a_norm_g'], 'w_q_b': out['w_q_b'], 'kv_a_norm_g': out['kv_a_norm_g'], 'w_kv_b': out['w_kv_b'], 'q_norm_g': out['q_norm_g'], 'k_norm_g': out['k_norm_g'], 'w_out': out['w_out'], 'ffn_norm_g': out['ffn_norm_g'], 'w_gate': out['w_gate'], 'w_up': out['w_up'], 'w_down': out['w_down'], 'loss_target': out['loss_target'], 'm_meta_tokens': out['m_meta_tokens'], 'm_attn_norm_g': out['m_attn_norm_g'], 'm_w_in': out['m_w_in'], 'm_w_pool': out['m_w_pool'], 'm_pool_scale': out['m_pool_scale'], 'm_q_a_norm_g': out['m_q_a_norm_g'], 'm_w_q_b': out['m_w_q_b'], 'm_kv_a_norm_g': out['m_kv_a_norm_g'], 'm_w_kv_b': out['m_w_kv_b'], 'm_q_norm_g': out['m_q_norm_g'], 'm_k_norm_g': out['m_k_norm_g'], 'm_w_out': out['m_w_out'], 'm_ffn_norm_g': out['m_ffn_norm_g'], 'm_w_gate': out['m_w_gate'], 'm_w_up': out['m_w_up'], 'm_w_down': out['m_w_down'], 'v_meta_tokens': out['v_meta_tokens'], 'v_attn_norm_g': out['v_attn_norm_g'], 'v_w_in': out['v_w_in'], 'v_w_pool': out['v_w_pool'], 'v_pool_scale': out['v_pool_scale'], 'v_q_a_norm_g': out['v_q_a_norm_g'], 'v_w_q_b': out['v_w_q_b'], 'v_kv_a_norm_g': out['v_kv_a_norm_g'], 'v_w_kv_b': out['v_w_kv_b'], 'v_q_norm_g': out['v_q_norm_g'], 'v_k_norm_g': out['v_k_norm_g'], 'v_w_out': out['v_w_out'], 'v_ffn_norm_g': out['v_ffn_norm_g'], 'v_w_gate': out['v_w_gate'], 'v_w_up': out['v_w_up'], 'v_w_down': out['v_w_down']}


def _loss(weights, diff, rest, loss_target):
    with _jax.named_scope("forward"):
        args = {**rest, TWIN_DIFF_INPUT: diff, **{k: w.astype(_WEIGHT_DTYPES[k]) for k, w in weights.items()}}
        y = _forward(args)
    with _jax.named_scope("loss_head"):
        err = _jnp.square(y.astype(_jnp.float32) - loss_target)
        return 0.5 * _jnp.sum(_jnp.mean(err, axis=-1)) if err.ndim else 0.5 * err


def _adamw(w, g, m, v):
    m = ADAM_B1 * m + (1.0 - ADAM_B1) * g
    v = ADAM_B2 * v + (1.0 - ADAM_B2) * _jnp.square(g)
    m_hat = m / (1.0 - ADAM_B1 ** ADAM_STEP)
    v_hat = v / (1.0 - ADAM_B2 ** ADAM_STEP)
    delta = -ADAM_LR * (m_hat / (_jnp.sqrt(v_hat) + ADAM_EPS) + ADAM_WD * w)
    return delta, m, v


def reference(x, meta_tokens, attn_norm_g, w_in, w_pool, pool_scale, q_a_norm_g, w_q_b, kv_a_norm_g, w_kv_b, q_norm_g, k_norm_g, w_out, ffn_norm_g, w_gate, w_up, w_down, loss_target, m_meta_tokens, m_attn_norm_g, m_w_in, m_w_pool, m_pool_scale, m_q_a_norm_g, m_w_q_b, m_kv_a_norm_g, m_w_kv_b, m_q_norm_g, m_k_norm_g, m_w_out, m_ffn_norm_g, m_w_gate, m_w_up, m_w_down, v_meta_tokens, v_attn_norm_g, v_w_in, v_w_pool, v_pool_scale, v_q_a_norm_g, v_w_q_b, v_kv_a_norm_g, v_w_kv_b, v_q_norm_g, v_k_norm_g, v_w_out, v_ffn_norm_g, v_w_gate, v_w_up, v_w_down):
    given = dict(x=x, meta_tokens=meta_tokens, attn_norm_g=attn_norm_g, w_in=w_in, w_pool=w_pool, pool_scale=pool_scale, q_a_norm_g=q_a_norm_g, w_q_b=w_q_b, kv_a_norm_g=kv_a_norm_g, w_kv_b=w_kv_b, q_norm_g=q_norm_g, k_norm_g=k_norm_g, w_out=w_out, ffn_norm_g=ffn_norm_g, w_gate=w_gate, w_up=w_up, w_down=w_down, loss_target=loss_target, m_meta_tokens=m_meta_tokens, m_attn_norm_g=m_attn_norm_g, m_w_in=m_w_in, m_w_pool=m_w_pool, m_pool_scale=m_pool_scale, m_q_a_norm_g=m_q_a_norm_g, m_w_q_b=m_w_q_b, m_kv_a_norm_g=m_kv_a_norm_g, m_w_kv_b=m_w_kv_b, m_q_norm_g=m_q_norm_g, m_k_norm_g=m_k_norm_g, m_w_out=m_w_out, m_ffn_norm_g=m_ffn_norm_g, m_w_gate=m_w_gate, m_w_up=m_w_up, m_w_down=m_w_down, v_meta_tokens=v_meta_tokens, v_attn_norm_g=v_attn_norm_g, v_w_in=v_w_in, v_w_pool=v_w_pool, v_pool_scale=v_pool_scale, v_q_a_norm_g=v_q_a_norm_g, v_w_q_b=v_w_q_b, v_kv_a_norm_g=v_kv_a_norm_g, v_w_kv_b=v_w_kv_b, v_q_norm_g=v_q_norm_g, v_k_norm_g=v_k_norm_g, v_w_out=v_w_out, v_ffn_norm_g=v_ffn_norm_g, v_w_gate=v_w_gate, v_w_up=v_w_up, v_w_down=v_w_down)
    weights = {n: given[n] for n in TWIN_WEIGHTS}
    shared = {n: given[n] for n in SHARED_INPUTS}
    per_example = {n: given[n] for n in ['x']}
    grad_fn = _jax.value_and_grad(_loss, argnums=(0, 1))

    def one_microbatch(ex, loss_target):
        ex = dict(ex)
        diff = ex.pop(TWIN_DIFF_INPUT)
        return grad_fn(weights, diff, {**shared, **ex}, loss_target)

    if N_MICROBATCH == 1:
        loss, (grad_w, grad_x) = one_microbatch(per_example, given["loss_target"])
    else:
        def body(carry, xs):
            loss_sum, grad_sum = carry
            l_k, (gw_k, gx_k) = one_microbatch(xs[0], xs[1])
            with _jax.named_scope("update"):
                return (loss_sum + l_k, _jax.tree.map(_jnp.add, grad_sum, gw_k)), gx_k

        init = (_jnp.zeros((), _jnp.float32), _jax.tree.map(_jnp.zeros_like, weights))
        (loss, grad_w), grad_x = _jax.lax.scan(body, init, (per_example, given["loss_target"]))
    with _jax.named_scope("update"):
        delta_w, new_m, new_v = {}, {}, {}
        for n in TWIN_WEIGHTS:
            delta_w[n], new_m[n], new_v[n] = _adamw(weights[n], grad_w[n], given["m_" + n], given["v_" + n])
    return (loss, grad_x, *[grad_w[n] for n in TWIN_WEIGHTS], *[delta_w[n] for n in TWIN_WEIGHTS],
            *[new_m[n] for n in TWIN_WEIGHTS], *[new_v[n] for n in TWIN_WEIGHTS])
```

```python
import functools

import jax
import jax.numpy as jnp
from jax import lax
from jax.experimental import pallas as pl
from jax.experimental.pallas import tpu as pltpu

F32 = jnp.float32
BF16 = jnp.bfloat16

D_MODEL = 1024
N_META = 16
POOL_WIDTH = 512
POOL_WINDOWS = (2, 4, 8, 16)
POOL_GROUP_DIM = 128
POOL_HALO = 16
MLA_HEADS = 8
QK_NOPE_DIM = 64
QK_ROPE_DIM = 32
QK_HEAD_DIM = 96
V_HEAD_DIM = 64
HEAD_PAD = 128
Q_LORA_RANK = 384
KV_LORA_RANK = 256
IN_WIDTH = 1184
IN_WIDTH_PAD = 1280
D_FF = 2816
ROPE_THETA = 10000.0
RMS_EPS = 1e-6
ATTN_SCALE = QK_HEAD_DIM ** -0.5

ADAM_LR = 0.001
ADAM_B1 = 0.9
ADAM_B2 = 0.999
ADAM_EPS = 1e-08
ADAM_WD = 0.01
ADAM_STEP = 10

N_DEV = 8
LANES = 128
ROW_TILE = 384
LONG_TILE = 1056
FF_TILE = 1408
VMEM_LIMIT = 56 * 1024 * 1024


def _params(*sem):
    return pltpu.CompilerParams(dimension_semantics=sem, vmem_limit_bytes=VMEM_LIMIT)


def _row_spec(tile, width):
    return pl.BlockSpec((tile, width), lambda i: (i, 0))


def _const_spec(shape):
    return pl.BlockSpec(shape, lambda i: tuple(0 for _ in shape))


def _nt(a, b):
    return lax.dot_general(a, b, (((1,), (1,)), ((), ())), preferred_element_type=F32)


def _tn(a, b):
    return lax.dot_general(a, b, (((0,), (0,)), ((), ())), preferred_element_type=F32)


def _nn(a, b):
    return jnp.dot(a, b, preferred_element_type=F32)


def _silu(g):
    return g * (1.0 / (1.0 + jnp.exp(-g)))


def norm_mm(x, gamma, w, splits, dtypes, name):
    L, K = x.shape
    N = w.shape[1]
    tm = ROW_TILE

    def body(x_ref, g_ref, w_ref, a_ref, *z_refs):
        xf = x_ref[...]
        r = lax.rsqrt(jnp.mean(xf * xf, axis=-1, keepdims=True) + RMS_EPS)
        a = ((xf * r) * g_ref[...]).astype(BF16)
        a_ref[...] = a
        z = _nn(a, w_ref[...])
        for (s, n), zr in zip(splits, z_refs):
            zr[...] = z[:, s:s + n].astype(zr.dtype)

    widths = [n for _, n in splits]
    return pl.pallas_call(
        body,
        grid=(L // tm,),
        in_specs=[_row_spec(tm, K), _const_spec((1, K)), _const_spec((K, N))],
        out_specs=[_row_spec(tm, K)] + [_row_spec(tm, n) for n in widths],
        out_shape=[jax.ShapeDtypeStruct((L, K), BF16)]
        + [jax.ShapeDtypeStruct((L, n), dt) for n, dt in zip(widths, dtypes)],
        compiler_params=_params("arbitrary"),
        name=name,
    )(x, gamma, w)


def ffn_up(g, w_gate, w_up, name):
    L, K = g.shape
    F = w_gate.shape[1]
    tm, tf = ROW_TILE, FF_TILE

    def body(a_ref, wg_ref, wu_ref, gate_ref, up_ref, act_ref):
        a = a_ref[...]
        gate = _nn(a, wg_ref[...])
        up = _nn(a, wu_ref[...])
        gate_ref[...] = gate
        up_ref[...] = up
        act_ref[...] = (_silu(gate) * up).astype(BF16)

    tile = pl.BlockSpec((tm, tf), lambda j, i: (i, j))
    return pl.pallas_call(
        body,
        grid=(F // tf, L // tm),
        in_specs=[
            pl.BlockSpec((tm, K), lambda j, i: (i, 0)),
            pl.BlockSpec((K, tf), lambda j, i: (0, j)),
            pl.BlockSpec((K, tf), lambda j, i: (0, j)),
        ],
        out_specs=[tile, tile, tile],
        out_shape=[
            jax.ShapeDtypeStruct((L, F), F32),
            jax.ShapeDtypeStruct((L, F), F32),
            jax.ShapeDtypeStruct((L, F), BF16),
        ],
        compiler_params=_params("arbitrary", "arbitrary"),
        name=name,
    )(g, w_gate, w_up)


def mm_res(a, w, res, name, gamma=None):
    L, K = a.shape
    N = w.shape[1]
    tm = ROW_TILE
    normed = gamma is not None

    def body(a_ref, w_ref, r_ref, *rest):
        out = r_ref[...] + _nn(a_ref[...], w_ref[...])
        if normed:
            g_ref, o_ref, n_ref = rest
            r = lax.rsqrt(jnp.mean(out * out, axis=-1, keepdims=True) + RMS_EPS)
            n_ref[...] = ((out * r) * g_ref[...]).astype(BF16)
        else:
            (o_ref,) = rest
        o_ref[...] = out

    in_specs = [_row_spec(tm, K), _const_spec((K, N)), _row_spec(tm, N)]
    out_specs = [_row_spec(tm, N)]
    out_shape = [jax.ShapeDtypeStruct((L, N), F32)]
    args = [a, w, res]
    if normed:
        in_specs.append(_const_spec((1, N)))
        out_specs.append(_row_spec(tm, N))
        out_shape.append(jax.ShapeDtypeStruct((L, N), BF16))
        args.append(gamma)
    return pl.pallas_call(
        body,
        grid=(L // tm,),
        in_specs=in_specs,
        out_specs=out_specs,
        out_shape=out_shape,
        compiler_params=_params("arbitrary"),
        name=name,
    )(*args)


def mm_nt(dz, w, splits, name):
    L, N = dz.shape
    K = w.shape[0]
    tm = ROW_TILE

    def body(dz_ref, w_ref, *o_refs):
        d = _nt(dz_ref[...].astype(BF16), w_ref[...])
        for (s, n), o in zip(splits, o_refs):
            o[...] = d[:, s:s + n]

    return pl.pallas_call(
        body,
        grid=(L // tm,),
        in_specs=[_row_spec(tm, N), _const_spec((K, N))],
        out_specs=[_row_spec(tm, n) for _, n in splits],
        out_shape=[jax.ShapeDtypeStruct((L, n), F32) for _, n in splits],
        compiler_params=_params("arbitrary"),
        name=name,
    )(dz, w)


def mm_nt_normbwd(pairs, x, gamma, dres, name):
    L, K = x.shape
    tm = ROW_TILE
    n_pairs = len(pairs)
    has_res = dres is not None

    def body(*refs):
        dz_refs = refs[:n_pairs]
        w_refs = refs[n_pairs:2 * n_pairs]
        x_ref, g_ref = refs[2 * n_pairs], refs[2 * n_pairs + 1]
        pos = 2 * n_pairs + 2
        r_ref = refs[pos] if has_res else None
        dx_ref, dg_ref = refs[pos + has_res], refs[pos + has_res + 1]
        da = None
        for dz_ref, w_ref in zip(dz_refs, w_refs):
            t = _nt(dz_ref[...].astype(BF16), w_ref[...])
            da = t if da is None else da + t
        xf = x_ref[...]
        r = lax.rsqrt(jnp.mean(xf * xf, axis=-1, keepdims=True) + RMS_EPS)
        xh = xf * r

        @pl.when(pl.program_id(0) == 0)
        def _():
            dg_ref[...] = jnp.zeros_like(dg_ref)

        dg_ref[...] += jnp.sum(da * xh, axis=0, keepdims=True)
        dxh = da * g_ref[...]
        dx = r * (dxh - xh * jnp.mean(dxh * xh, axis=-1, keepdims=True))
        if has_res:
            dx = dx + r_ref[...]
        dx_ref[...] = dx

    in_specs = [_row_spec(tm, dz.shape[1]) for dz, _ in pairs]
    in_specs += [_const_spec(w.shape) for _, w in pairs]
    in_specs += [_row_spec(tm, K), _const_spec((1, K))]
    args = [dz for dz, _ in pairs] + [w for _, w in pairs] + [x, gamma]
    if has_res:
        in_specs.append(_row_spec(tm, K))
        args.append(dres)
    return pl.pallas_call(
        body,
        grid=(L // tm,),
        in_specs=in_specs,
        out_specs=[_row_spec(tm, K), _const_spec((1, K))],
        out_shape=[jax.ShapeDtypeStruct((L, K), F32), jax.ShapeDtypeStruct((1, K), F32)],
        compiler_params=_params("arbitrary"),
        name=name,
    )(*args)


def mm_tn(a, b, name):
    L, K = a.shape
    N = b.shape[1]
    tl = LONG_TILE if L % LONG_TILE == 0 else ROW_TILE
    tk = FF_TILE if K % FF_TILE == 0 and K > FF_TILE else K
    tn = FF_TILE if N % FF_TILE == 0 and N > FF_TILE else N

    def body(a_ref, b_ref, o_ref):
        @pl.when(pl.program_id(2) == 0)
        def _():
            o_ref[...] = jnp.zeros_like(o_ref)

        o_ref[...] += _tn(a_ref[...].astype(BF16), b_ref[...].astype(BF16))

    return pl.pallas_call(
        body,
        grid=(K // tk, N // tn, L // tl),
        in_specs=[
            pl.BlockSpec((tl, tk), lambda k, n, l: (l, k)),
            pl.BlockSpec((tl, tn), lambda k, n, l: (l, n)),
        ],
        out_specs=pl.BlockSpec((tk, tn), lambda k, n, l: (k, n)),
        out_shape=jax.ShapeDtypeStruct((K, N), F32),
        compiler_params=_params("arbitrary", "arbitrary", "arbitrary"),
        name=name,
    )(a, b)


def ffn_bwd_act(dh, w_down, gate, up, name):
    L, K = dh.shape
    F = w_down.shape[0]
    tm, tf = ROW_TILE, FF_TILE

    def body(dh_ref, w_ref, gate_ref, up_ref, dgate_ref, dup_ref):
        dact = _nt(dh_ref[...].astype(BF16), w_ref[...])
        g = gate_ref[...]
        sig = 1.0 / (1.0 + jnp.exp(-g))
        dup_ref[...] = (dact * (g * sig)).astype(BF16)
        dgate_ref[...] = (dact * up_ref[...] * (sig * (1.0 + g * (1.0 - sig)))).astype(BF16)

    return pl.pallas_call(
        body,
        grid=(F // tf, L // tm),
        in_specs=[
            pl.BlockSpec((tm, K), lambda j, i: (i, 0)),
            pl.BlockSpec((tf, K), lambda j, i: (j, 0)),
            pl.BlockSpec((tm, tf), lambda j, i: (i, j)),
            pl.BlockSpec((tm, tf), lambda j, i: (i, j)),
        ],
        out_specs=[
            pl.BlockSpec((tm, tf), lambda j, i: (i, j)),
            pl.BlockSpec((tm, tf), lambda j, i: (i, j)),
        ],
        out_shape=[jax.ShapeDtypeStruct((L, F), BF16), jax.ShapeDtypeStruct((L, F), BF16)],
        compiler_params=_params("arbitrary", "arbitrary"),
        name=name,
    )(dh, w_down, gate, up)


def _pool_residual(scr, lo, tm, g, w, t):
    cols = slice(g * POOL_GROUP_DIM, (g + 1) * POOL_GROUP_DIM)
    cur = scr[lo:lo + tm, cols]
    s = cur
    for k in range(1, w):
        s = s + scr[lo - k:lo - k + tm, cols]
    cnt = jnp.minimum(t + 1, w).astype(F32)
    return s / cnt - cur


def pool_fwd(u, w_pool, scale, name):
    L, C = u.shape
    tm, halo = ROW_TILE, POOL_HALO

    def body(u_ref, halo_ref, w_ref, s_ref, y_ref, scr):
        i = pl.program_id(0)
        scr[0:halo, :] = jnp.where(i > 0, halo_ref[...], 0.0)
        scr[halo:halo + tm, :] = u_ref[...]
        t = i * tm + lax.broadcasted_iota(jnp.int32, (tm, POOL_GROUP_DIM), 0)
        for g, w in enumerate(POOL_WINDOWS):
            cols = slice(g * POOL_GROUP_DIM, (g + 1) * POOL_GROUP_DIM)
            p = _pool_residual(scr, halo, tm, g, w, t)
            y = _nn(p.astype(BF16), w_ref[g]) * s_ref[:, cols]
            y_ref[:, cols] = y.astype(y_ref.dtype)

    return pl.pallas_call(
        body,
        grid=(L // tm,),
        in_specs=[
            _row_spec(tm, C),
            pl.BlockSpec((halo, C), lambda i: (jnp.maximum(i * (tm // halo) - 1, 0), 0)),
            _const_spec(w_pool.shape),
            _const_spec((1, C)),
        ],
        out_specs=_row_spec(tm, C),
        out_shape=jax.ShapeDtypeStruct((L, C), BF16),
        scratch_shapes=[pltpu.VMEM((tm + halo, C), F32)],
        compiler_params=_params("arbitrary"),
        name=name,
    )(u, u, w_pool, scale)


def pool_bwd(u, dy, w_pool, scale, name):
    L, C = u.shape
    tm, halo = ROW_TILE, POOL_HALO
    n_tiles = L // tm
    last_halo = L // halo - 1

    def body(u_ref, uh_ref, dy_ref, dyh_ref, w_ref, s_ref, du_ref, dw_ref, ds_ref, scr_u, scr_q):
        i = pl.program_id(0)

        @pl.when(i == 0)
        def _():
            dw_ref[...] = jnp.zeros_like(dw_ref)
            ds_ref[...] = jnp.zeros_like(ds_ref)

        scr_u[0:halo, :] = jnp.where(i > 0, uh_ref[...], 0.0)
        scr_u[halo:halo + tm, :] = u_ref[...]
        t = i * tm + lax.broadcasted_iota(jnp.int32, (tm, POOL_GROUP_DIM), 0)
        th = (i + 1) * tm + lax.broadcasted_iota(jnp.int32, (halo, POOL_GROUP_DIM), 0)
        for g, w in enumerate(POOL_WINDOWS):
            cols = slice(g * POOL_GROUP_DIM, (g + 1) * POOL_GROUP_DIM)
            p = _pool_residual(scr_u, halo, tm, g, w, t).astype(BF16)
            wg = w_ref[g]
            sc = s_ref[:, cols]
            dy = dy_ref[:, cols]
            ds_ref[:, cols] += jnp.sum(dy * _nn(p, wg), axis=0, keepdims=True)
            dys = (dy * sc).astype(BF16)
            dw_ref[g] += _tn(p, dys)
            dp = _nt(dys, wg)
            dyh = jnp.where(i < n_tiles - 1, dyh_ref[:, cols], 0.0)
            dph = _nt((dyh * sc).astype(BF16), wg)
            scr_q[0:tm, cols] = dp / jnp.minimum(t + 1, w).astype(F32)
            scr_q[tm:tm + halo, cols] = dph / jnp.minimum(th + 1, w).astype(F32)
            acc = scr_q[0:tm, cols]
            for k in range(1, w):
                acc = acc + scr_q[k:k + tm, cols]
            du_ref[:, cols] = acc - dp

    return pl.pallas_call(
        body,
        grid=(n_tiles,),
        in_specs=[
            _row_spec(tm, C),
            pl.BlockSpec((halo, C), lambda i: (jnp.maximum(i * (tm // halo) - 1, 0), 0)),
            _row_spec(tm, C),
            pl.BlockSpec((halo, C), lambda i: (jnp.minimum((i + 1) * (tm // halo), last_halo), 0)),
            _const_spec(w_pool.shape),
            _const_spec((1, C)),
        ],
        out_specs=[_row_spec(tm, C), _const_spec(w_pool.shape), _const_spec((1, C))],
        out_shape=[
            jax.ShapeDtypeStruct((L, C), F32),
            jax.ShapeDtypeStruct(w_pool.shape, F32),
            jax.ShapeDtypeStruct((1, C), F32),
        ],
        scratch_shapes=[pltpu.VMEM((tm + halo, C), F32), pltpu.VMEM((tm + halo, C), F32)],
        compiler_params=_params("arbitrary"),
        name=name,
    )(u, u, dy, dy, w_pool, scale)


def _head_masks(rows):
    lane = lax.broadcasted_iota(jnp.int32, (rows, HEAD_PAD), 1)
    return lane, lane < QK_NOPE_DIM, (lane >= QK_NOPE_DIM) & (lane < QK_HEAD_DIM)


def _rope_swap(x, lane):
    half = QK_ROPE_DIM // 2
    swapped = jnp.where(lane < QK_NOPE_DIM + half, pltpu.roll(x, HEAD_PAD - half, 1), pltpu.roll(x, half, 1))
    return jnp.where((lane >= QK_NOPE_DIM) & (lane < QK_HEAD_DIM), swapped, 0.0)


def _seg_mean(v, m_n, m_r):
    mn = jnp.sum(jnp.where(m_n, v, 0.0), axis=1, keepdims=True) * (1.0 / QK_NOPE_DIM)
    mr = jnp.sum(jnp.where(m_r, v, 0.0), axis=1, keepdims=True) * (1.0 / QK_ROPE_DIM)
    return jnp.where(m_n, mn, mr)


def qk_prep_fwd(q, kn, kr, cosf, sins, gq, gkn, gkr, name):
    L = q.shape[0]
    tm = ROW_TILE
    W = MLA_HEADS * HEAD_PAD

    def body(q_ref, kn_ref, kr_ref, c_ref, s_ref, gq_ref, gkn_ref, gkr_ref, qo_ref, ko_ref):
        lane, m_n, m_r = _head_masks(tm)
        cosf_, sins_ = c_ref[...], s_ref[...]
        kr_ = kr_ref[...]
        rk = lax.rsqrt(_seg_mean(kr_ * kr_, m_n, m_r) + RMS_EPS)
        krn = kr_ * rk * gkr_ref[...]
        krf = krn * cosf_ + _rope_swap(krn, lane) * sins_
        for h in range(MLA_HEADS):
            cols = slice(h * HEAD_PAD, (h + 1) * HEAD_PAD)
            qh = q_ref[:, cols]
            qn = qh * lax.rsqrt(_seg_mean(qh * qh, m_n, m_r) + RMS_EPS) * gq_ref[...]
            qo_ref[:, cols] = (qn * cosf_ + _rope_swap(qn, lane) * sins_).astype(BF16)
            kh = kn_ref[:, cols]
            rn = lax.rsqrt(jnp.sum(kh * kh, axis=1, keepdims=True) * (1.0 / QK_NOPE_DIM) + RMS_EPS)
            ko_ref[:, cols] = (kh * rn * gkn_ref[...] + krf).astype(BF16)

    vec = _const_spec((1, HEAD_PAD))
    return pl.pallas_call(
        body,
        grid=(L // tm,),
        in_specs=[_row_spec(tm, W), _row_spec(tm, W), _row_spec(tm, HEAD_PAD), _row_spec(tm, HEAD_PAD),
                  _row_spec(tm, HEAD_PAD), vec, vec, vec],
        out_specs=[_row_spec(tm, W), _row_spec(tm, W)],
        out_shape=[jax.ShapeDtypeStruct((L, W), BF16), jax.ShapeDtypeStruct((L, W), BF16)],
        compiler_params=_params("arbitrary"),
        name=name,
    )(q, kn, kr, cosf, sins, gq, gkn, gkr)


def qk_prep_bwd(dqo, dko, q, kn, kr, cosf, sins, gq, gkn, gkr, name):
    L = q.shape[0]
    tm = ROW_TILE
    W = MLA_HEADS * HEAD_PAD

    def body(dqo_ref, dko_ref, q_ref, kn_ref, kr_ref, c_ref, s_ref, gq_ref, gkn_ref, gkr_ref,
             dq_ref, dkn_ref, dkr_ref, dgq_ref, dgkn_ref, dgkr_ref):
        @pl.when(pl.program_id(0) == 0)
        def _():
            dgq_ref[...] = jnp.zeros_like(dgq_ref)
            dgkn_ref[...] = jnp.zeros_like(dgkn_ref)
            dgkr_ref[...] = jnp.zeros_like(dgkr_ref)

        lane, m_n, m_r = _head_masks(tm)
        cosf_, sins_ = c_ref[...], s_ref[...]
        dgq = jnp.zeros((1, HEAD_PAD), F32)
        dgkn = jnp.zeros((1, HEAD_PAD), F32)
        dkrf = jnp.zeros((tm, HEAD_PAD), F32)
        for h in range(MLA_HEADS):
            cols = slice(h * HEAD_PAD, (h + 1) * HEAD_PAD)
            dy = dqo_ref[:, cols]
            dqn = dy * cosf_ + _rope_swap(dy * sins_, lane)
            qh = q_ref[:, cols]
            rinv = lax.rsqrt(_seg_mean(qh * qh, m_n, m_r) + RMS_EPS)
            xh = qh * rinv
            dgq = dgq + jnp.sum(dqn * xh, axis=0, keepdims=True)
            dxh = dqn * gq_ref[...]
            dq_ref[:, cols] = rinv * (dxh - xh * _seg_mean(dxh * xh, m_n, m_r))
            dk = dko_ref[:, cols]
            dkrf = dkrf + dk
            kh = kn_ref[:, cols]
            rn = lax.rsqrt(jnp.sum(kh * kh, axis=1, keepdims=True) * (1.0 / QK_NOPE_DIM) + RMS_EPS)
            xk = kh * rn
            dgkn = dgkn + jnp.sum(dk * xk, axis=0, keepdims=True)
            dxk = dk * gkn_ref[...]
            dkn_ref[:, cols] = rn * (dxk - xk * (jnp.sum(dxk * xk, axis=1, keepdims=True) * (1.0 / QK_NOPE_DIM)))
        kr_ = kr_ref[...]
        rk = lax.rsqrt(_seg_mean(kr_ * kr_, m_n, m_r) + RMS_EPS)
        xr = kr_ * rk
        dkrn = dkrf * cosf_ + _rope_swap(dkrf * sins_, lane)
        dgkr_ref[...] += jnp.sum(dkrn * xr, axis=0, keepdims=True)
        dxr = dkrn * gkr_ref[...]
        dkr_ref[...] = rk * (dxr - xr * _seg_mean(dxr * xr, m_n, m_r))
        dgq_ref[...] += dgq
        dgkn_ref[...] += dgkn

    vec = _const_spec((1, HEAD_PAD))
    return pl.pallas_call(
        body,
        grid=(L // tm,),
        in_specs=[_row_spec(tm, W), _row_spec(tm, W), _row_spec(tm, W), _row_spec(tm, W),
                  _row_spec(tm, HEAD_PAD), _row_spec(tm, HEAD_PAD), _row_spec(tm, HEAD_PAD), vec, vec, vec],
        out_specs=[_row_spec(tm, W), _row_spec(tm, W), _row_spec(tm, HEAD_PAD), vec, vec, vec],
        out_shape=[jax.ShapeDtypeStruct((L, W), F32), jax.ShapeDtypeStruct((L, W), F32),
                   jax.ShapeDtypeStruct((L, HEAD_PAD), F32)] + [jax.ShapeDtypeStruct((1, HEAD_PAD), F32)] * 3,
        compiler_params=_params("arbitrary"),
        name=name,
    )(dqo, dko, q, kn, kr, cosf, sins, gq, gkn, gkr)


def attn_fwd(qp, kp, v, name):
    L = qp.shape[0]
    tq = ROW_TILE
    pair_w = 2 * HEAD_PAD

    def body(q_ref, k_ref, v_ref, o_ref, lse_ref):
        i = pl.program_id(1)
        left = lax.broadcasted_iota(jnp.int32, (tq, LANES), 1) < V_HEAD_DIM
        row = lax.broadcasted_iota(jnp.int32, (tq, tq), 0)
        col = lax.broadcasted_iota(jnp.int32, (tq, tq), 1)
        res = []
        for hh in range(2):
            cols = slice(hh * HEAD_PAD, (hh + 1) * HEAD_PAD)
            q = q_ref[:, cols]

            def step(j, carry, masked, q=q, cols=cols):
                m, l, acc = carry
                rows = pl.ds(pl.multiple_of(j * tq, tq), tq)
                s = _nt(q, k_ref[rows, cols]) * ATTN_SCALE
                if masked:
                    s = jnp.where(col <= row, s, -jnp.inf)
                m_new = jnp.maximum(m, jnp.max(s, axis=1, keepdims=True))
                alpha = jnp.exp(m - m_new)
                p = jnp.exp(s - m_new)
                l = alpha * l + jnp.sum(p, axis=1, keepdims=True)
                acc = alpha * acc + _nn(p.astype(BF16), v_ref[rows, :])
                return m_new, l, acc

            init = (jnp.full((tq, 1), -jnp.inf, F32), jnp.zeros((tq, 1), F32), jnp.zeros((tq, LANES), F32))
            carry = lax.fori_loop(0, i, functools.partial(step, masked=False), init)
            m, l, acc = step(i, carry, True)
            res.append((acc / l, m + jnp.log(l)))
        o_ref[...] = jnp.where(left, res[0][0], res[1][0])
        lse_ref[...] = jnp.where(left, res[0][1], res[1][1])

    n_pairs = MLA_HEADS // 2
    return pl.pallas_call(
        body,
        grid=(n_pairs, L // tq),
        in_specs=[
            pl.BlockSpec((tq, pair_w), lambda p, i: (i, p)),
            pl.BlockSpec((L, pair_w), lambda p, i: (0, p)),
            pl.BlockSpec((L, LANES), lambda p, i: (0, p)),
        ],
        out_specs=[
            pl.BlockSpec((tq, LANES), lambda p, i: (i, p)),
            pl.BlockSpec((tq, LANES), lambda p, i: (i, p)),
        ],
        out_shape=[jax.ShapeDtypeStruct((L, n_pairs * LANES), F32), jax.ShapeDtypeStruct((L, n_pairs * LANES), F32)],
        compiler_params=_params("arbitrary", "arbitrary"),
        name=name,
    )(qp, kp, v)


def attn_delta(do, o, name):
    L, C = o.shape
    tm = ROW_TILE

    def body(do_ref, o_ref, d_ref):
        left = lax.broadcasted_iota(jnp.int32, (tm, LANES), 1) < V_HEAD_DIM
        for p in range(C // LANES):
            cols = slice(p * LANES, (p + 1) * LANES)
            prod = do_ref[:, cols] * o_ref[:, cols]
            d0 = jnp.sum(jnp.where(left, prod, 0.0), axis=1, keepdims=True)
            d1 = jnp.sum(jnp.where(left, 0.0, prod), axis=1, keepdims=True)
            d_ref[:, cols] = jnp.where(left, d0, d1)

    return pl.pallas_call(
        body,
        grid=(L // tm,),
        in_specs=[_row_spec(tm, C), _row_spec(tm, C)],
        out_specs=_row_spec(tm, C),
        out_shape=jax.ShapeDtypeStruct((L, C), F32),
        compiler_params=_params("arbitrary"),
        name=name,
    )(do, o)


def attn_bwd(qp, kp, v, do, lse, delta, name):
    L = qp.shape[0]
    tq = ROW_TILE
    n_q = L // tq
    pair_w = 2 * HEAD_PAD

    def body(q_ref, k_ref, v_ref, do_ref, lse_ref, dl_ref, dq_ref, dk_ref, dv_ref):
        j = pl.program_id(1)

        @pl.when(j == 0)
        def _():
            dq_ref[...] = jnp.zeros_like(dq_ref)

        left = lax.broadcasted_iota(jnp.int32, (tq, LANES), 1) < V_HEAD_DIM
        row = lax.broadcasted_iota(jnp.int32, (tq, tq), 0)
        col = lax.broadcasted_iota(jnp.int32, (tq, tq), 1)
        vv = v_ref[...]
        dv_acc = jnp.zeros((tq, LANES), F32)
        for hh in range(2):
            cols = slice(hh * HEAD_PAD, (hh + 1) * HEAD_PAD)
            k = k_ref[:, cols]
            mine = left if hh == 0 else jnp.logical_not(left)
            stat = slice(hh * V_HEAD_DIM, hh * V_HEAD_DIM + 1)

            def step(i, carry, masked, k=k, cols=cols, mine=mine, stat=stat):
                dk_acc, dv_acc = carry
                rows = pl.ds(pl.multiple_of(i * tq, tq), tq)
                q = q_ref[rows, cols]
                dom = jnp.where(mine, do_ref[rows, :], 0.0).astype(BF16)
                s = _nt(q, k) * ATTN_SCALE
                p = jnp.exp(s - lse_ref[rows, stat])
                if masked:
                    p = jnp.where(col <= row, p, 0.0)
                dp = _nt(dom, vv)
                ds = (p * (dp - dl_ref[rows, stat]) * ATTN_SCALE).astype(BF16)
                dq_ref[rows, cols] += _nn(ds, k)
                dk_acc = dk_acc + _tn(ds, q)
                dv_acc = dv_acc + _tn(p.astype(BF16), dom)
                return dk_acc, dv_acc

            carry = step(j, (jnp.zeros((tq, HEAD_PAD), F32), dv_acc), True)
            dk_acc, dv_acc = lax.fori_loop(j + 1, n_q, functools.partial(step, masked=False), carry)
            dk_ref[:, cols] = dk_acc
        dv_ref[...] = dv_acc

    n_pairs = MLA_HEADS // 2
    return pl.pallas_call(
        body,
        grid=(n_pairs, n_q),
        in_specs=[
            pl.BlockSpec((L, pair_w), lambda p, j: (0, p)),
            pl.BlockSpec((tq, pair_w), lambda p, j: (j, p)),
            pl.BlockSpec((tq, LANES), lambda p, j: (j, p)),
            pl.BlockSpec((L, LANES), lambda p, j: (0, p)),
            pl.BlockSpec((L, LANES), lambda p, j: (0, p)),
            pl.BlockSpec((L, LANES), lambda p, j: (0, p)),
        ],
        out_specs=[
            pl.BlockSpec((L, pair_w), lambda p, j: (0, p)),
            pl.BlockSpec((tq, pair_w), lambda p, j: (j, p)),
            pl.BlockSpec((tq, LANES), lambda p, j: (j, p)),
        ],
        out_shape=[
            jax.ShapeDtypeStruct((L, n_pairs * pair_w), F32),
            jax.ShapeDtypeStruct((L, n_pairs * pair_w), F32),
            jax.ShapeDtypeStruct((L, n_pairs * LANES), F32),
        ],
        compiler_params=_params("arbitrary", "arbitrary"),
        name=name,
    )(qp, kp, v, do, lse, delta)


def loss_head(h, target, n_real, name):
    L, D = h.shape
    tm = ROW_TILE

    def body(h_ref, t_ref, dh_ref, sq_ref):
        i = pl.program_id(0)

        @pl.when(i == 0)
        def _():
            sq_ref[...] = jnp.zeros_like(sq_ref)

        t = i * tm + lax.broadcasted_iota(jnp.int32, (tm, D), 0)
        real = (t >= N_META) & (t < N_META + n_real)
        diff = jnp.where(real, h_ref[...] - t_ref[...], 0.0)
        dh_ref[...] = diff * (1.0 / D)
        sq_ref[...] += jnp.sum(diff * diff, axis=0, keepdims=True)

    return pl.pallas_call(
        body,
        grid=(L // tm,),
        in_specs=[_row_spec(tm, D), _row_spec(tm, D)],
        out_specs=[_row_spec(tm, D), _const_spec((1, D))],
        out_shape=[jax.ShapeDtypeStruct((L, D), F32), jax.ShapeDtypeStruct((1, D), F32)],
        compiler_params=_params("arbitrary"),
        name=name,
    )(h, target)


def _mesh_position():
    x, y, c = lax.axis_index("x"), lax.axis_index("y"), lax.axis_index("c")
    return x, y, c, 4 * x + 2 * y + c


def _flip(x, y, c, k):
    px = 1 - x if k & 4 else x
    py = 1 - y if k & 2 else y
    pc = 1 - c if k & 1 else c
    return (px, py, pc), 4 * px + 2 * py + pc


def all_gather(big, small, name):
    n_peers = N_DEV - 1

    def body(big_ref, small_ref, gbig_ref, gsmall_ref, send_sems, recv_sems, local_sems):
        x, y, c, me = _mesh_position()
        own = [pltpu.make_async_copy(big_ref, gbig_ref.at[me], local_sems.at[0]),
               pltpu.make_async_copy(small_ref, gsmall_ref.at[me], local_sems.at[1])]
        for cp in own:
            cp.start()
        sends, recvs = [], []
        for k in range(1, N_DEV):
            peer, peer_idx = _flip(x, y, c, k)
            for b, (src, dst) in enumerate(((big_ref, gbig_ref), (small_ref, gsmall_ref))):
                n = 2 * (k - 1) + b
                sends.append(pltpu.make_async_remote_copy(
                    src_ref=src, dst_ref=dst.at[me], send_sem=send_sems.at[n], recv_sem=recv_sems.at[n],
                    device_id=peer, device_id_type=pl.DeviceIdType.MESH))
                recvs.append(pltpu.make_async_remote_copy(
                    src_ref=src, dst_ref=dst.at[peer_idx], send_sem=send_sems.at[n], recv_sem=recv_sems.at[n],
                    device_id=peer, device_id_type=pl.DeviceIdType.MESH))
        for cp in sends:
            cp.start()
        for cp in recvs:
            cp.wait_recv()
        for cp in sends:
            cp.wait_send()
        for cp in own:
            cp.wait()

    any_spec = pl.BlockSpec(memory_space=pl.ANY)
    return pl.pallas_call(
        body,
        in_specs=[any_spec, any_spec],
        out_specs=[any_spec, any_spec],
        out_shape=[jax.ShapeDtypeStruct((N_DEV,) + big.shape, big.dtype),
                   jax.ShapeDtypeStruct((N_DEV,) + small.shape, small.dtype)],
        scratch_shapes=[pltpu.SemaphoreType.DMA((2 * n_peers,)), pltpu.SemaphoreType.DMA((2 * n_peers,)),
                        pltpu.SemaphoreType.DMA((2,))],
        name=name,
    )(big, small)


def all_to_all(big, small, name):
    n_peers = N_DEV - 1

    def body(big_ref, small_ref, gbig_ref, gsmall_ref, send_sems, recv_sems, local_sems):
        x, y, c, me = _mesh_position()
        own = [pltpu.make_async_copy(big_ref.at[me], gbig_ref.at[me], local_sems.at[0]),
               pltpu.make_async_copy(small_ref.at[me], gsmall_ref.at[me], local_sems.at[1])]
        for cp in own:
            cp.start()
        sends, recvs = [], []
        for k in range(1, N_DEV):
            peer, peer_idx = _flip(x, y, c, k)
            for b, (src, dst) in enumerate(((big_ref, gbig_ref), (small_ref, gsmall_ref))):
                n = 2 * (k - 1) + b
                sends.append(pltpu.make_async_remote_copy(
                    src_ref=src.at[peer_idx], dst_ref=dst.at[me], send_sem=send_sems.at[n], recv_sem=recv_sems.at[n],
                    device_id=peer, device_id_type=pl.DeviceIdType.MESH))
                recvs.append(pltpu.make_async_remote_copy(
                    src_ref=src.at[peer_idx], dst_ref=dst.at[peer_idx], send_sem=send_sems.at[n],
                    recv_sem=recv_sems.at[n], device_id=peer, device_id_type=pl.DeviceIdType.MESH))
        for cp in sends:
            cp.start()
        for cp in recvs:
            cp.wait_recv()
        for cp in sends:
            cp.wait_send()
        for cp in own:
            cp.wait()

    any_spec = pl.BlockSpec(memory_space=pl.ANY)
    return pl.pallas_call(
        body,
        in_specs=[any_spec, any_spec],
        out_specs=[any_spec, any_spec],
        out_shape=[jax.ShapeDtypeStruct(big.shape, big.dtype), jax.ShapeDtypeStruct(small.shape, small.dtype)],
        scratch_shapes=[pltpu.SemaphoreType.DMA((2 * n_peers,)), pltpu.SemaphoreType.DMA((2 * n_peers,)),
                        pltpu.SemaphoreType.DMA((2,))],
        name=name,
    )(big, small)


def adamw(parts, w, m, v, tile, name):
    R = w.shape[0]

    def body(p_ref, w_ref, m_ref, v_ref, g_ref, d_ref, mo_ref, vo_ref):
        g = p_ref[0].astype(F32)
        for s in range(1, N_DEV):
            g = g + p_ref[s].astype(F32)
        m_new = ADAM_B1 * m_ref[...] + (1.0 - ADAM_B1) * g
        v_new = ADAM_B2 * v_ref[...] + (1.0 - ADAM_B2) * (g * g)
        m_hat = m_new / (1.0 - ADAM_B1 ** ADAM_STEP)
        v_hat = v_new / (1.0 - ADAM_B2 ** ADAM_STEP)
        g_ref[...] = g
        d_ref[...] = -ADAM_LR * (m_hat / (jnp.sqrt(v_hat) + ADAM_EPS) + ADAM_WD * w_ref[...])
        mo_ref[...] = m_new
        vo_ref[...] = v_new

    flat = _row_spec(tile, LANES)
    return pl.pallas_call(
        body,
        grid=(R // tile,),
        in_specs=[pl.BlockSpec((N_DEV, tile, LANES), lambda i: (0, i, 0)), flat, flat, flat],
        out_specs=[flat, flat, flat, flat],
        out_shape=[jax.ShapeDtypeStruct((R, LANES), F32)] * 4,
        compiler_params=_params("arbitrary"),
        name=name,
    )(parts, w, m, v)


SHARDED = ("w_in", "w_q_b", "w_kv_b", "w_out", "w_gate", "w_up", "w_down")
COLUMN_SHARDED = ("w_in", "w_q_b", "w_kv_b", "w_gate", "w_up")
REPLICATED = ("attn_norm_g", "w_pool", "pool_scale", "q_a_norm_g", "kv_a_norm_g", "q_norm_g", "k_norm_g",
              "ffn_norm_g")
ADAM_TILE = 896


def _pack_rows(arrays):
    return jnp.concatenate([a.reshape(-1, LANES) for a in arrays], axis=0)


def _pack_flat(arrays, rows):
    flat = jnp.concatenate([a.reshape(-1) for a in arrays])
    return jnp.pad(flat, (0, rows * LANES - flat.shape[0])).reshape(rows, LANES)


def _unpack_rows(packed, shapes):
    out, at = [], 0
    for shp in shapes:
        n = 1
        for d in shp:
            n *= d
        out.append(packed[at:at + n // LANES].reshape(shp))
        at += n // LANES
    return out


def _unpack_flat(packed, shapes):
    flat, out, at = packed.reshape(-1), [], 0
    for shp in shapes:
        n = 1
        for d in shp:
            n *= d
        out.append(flat[at:at + n].reshape(shp))
        at += n
    return out


def _gathered_full(slots, name):
    if name in COLUMN_SHARDED:
        s = jnp.transpose(slots, (1, 2, 0, 3))
        return s.reshape(s.shape[0], s.shape[1], -1)
    s = jnp.transpose(slots, (1, 0, 2, 3))
    return s.reshape(s.shape[0], -1, s.shape[3])


def _to_slots(full, name):
    d, a, b = full.shape
    if name in COLUMN_SHARDED:
        return jnp.transpose(full.reshape(d, a, N_DEV, b // N_DEV), (2, 0, 1, 3))
    return jnp.transpose(full.reshape(d, N_DEV, a // N_DEV, b), (1, 0, 2, 3))


def _rope_lane_tables(length):
    inv = 1.0 / (ROPE_THETA ** (jnp.arange(0, QK_ROPE_DIM, 2, dtype=F32) / QK_ROPE_DIM))
    ang = jnp.arange(length, dtype=F32)[:, None] * inv[None, :]
    cos, sin = jnp.cos(ang), jnp.sin(ang)
    ones = jnp.ones((length, QK_NOPE_DIM), F32)
    zeros = jnp.zeros((length, QK_NOPE_DIM), F32)
    tail = HEAD_PAD - QK_HEAD_DIM
    cosf = jnp.concatenate([ones, cos, cos, ones[:, :tail]], axis=1)
    sins = jnp.concatenate([zeros, -sin, sin, zeros[:, :tail]], axis=1)
    return cosf, sins


def _pad_lanes(vec, at, width=HEAD_PAD):
    return jnp.pad(vec, (at, width - at - vec.shape[0])).reshape(1, width)


def kernel(x, meta_tokens, attn_norm_g, w_in, w_pool, pool_scale, q_a_norm_g, w_q_b, kv_a_norm_g, w_kv_b, q_norm_g, k_norm_g, w_out, ffn_norm_g, w_gate, w_up, w_down, loss_target, m_meta_tokens, m_attn_norm_g, m_w_in, m_w_pool, m_pool_scale, m_q_a_norm_g, m_w_q_b, m_kv_a_norm_g, m_w_kv_b, m_q_norm_g, m_k_norm_g, m_w_out, m_ffn_norm_g, m_w_gate, m_w_up, m_w_down, v_meta_tokens, v_attn_norm_g, v_w_in, v_w_pool, v_pool_scale, v_q_a_norm_g, v_w_q_b, v_kv_a_norm_g, v_w_kv_b, v_q_norm_g, v_k_norm_g, v_w_out, v_ffn_norm_g, v_w_gate, v_w_up, v_w_down):
    weights = dict(meta_tokens=meta_tokens, attn_norm_g=attn_norm_g, w_in=w_in, w_pool=w_pool, pool_scale=pool_scale,
                   q_a_norm_g=q_a_norm_g, w_q_b=w_q_b, kv_a_norm_g=kv_a_norm_g, w_kv_b=w_kv_b, q_norm_g=q_norm_g,
                   k_norm_g=k_norm_g, w_out=w_out, ffn_norm_g=ffn_norm_g, w_gate=w_gate, w_up=w_up, w_down=w_down)
    mom1 = dict(meta_tokens=m_meta_tokens, attn_norm_g=m_attn_norm_g, w_in=m_w_in, w_pool=m_w_pool,
                pool_scale=m_pool_scale, q_a_norm_g=m_q_a_norm_g, w_q_b=m_w_q_b, kv_a_norm_g=m_kv_a_norm_g,
                w_kv_b=m_w_kv_b, q_norm_g=m_q_norm_g, k_norm_g=m_k_norm_g, w_out=m_w_out, ffn_norm_g=m_ffn_norm_g,
                w_gate=m_w_gate, w_up=m_w_up, w_down=m_w_down)
    mom2 = dict(meta_tokens=v_meta_tokens, attn_norm_g=v_attn_norm_g, w_in=v_w_in, w_pool=v_w_pool,
                pool_scale=v_pool_scale, q_a_norm_g=v_q_a_norm_g, w_q_b=v_w_q_b, kv_a_norm_g=v_kv_a_norm_g,
                w_kv_b=v_w_kv_b, q_norm_g=v_q_norm_g, k_norm_g=v_k_norm_g, w_out=v_w_out, ffn_norm_g=v_ffn_norm_g,
                w_gate=v_w_gate, w_up=v_w_up, w_down=v_w_down)
    order = ("meta_tokens", "attn_norm_g", "w_in", "w_pool", "pool_scale", "q_a_norm_g", "w_q_b", "kv_a_norm_g",
             "w_kv_b", "q_norm_g", "k_norm_g", "w_out", "ffn_norm_g", "w_gate", "w_up", "w_down")
    depth = w_in.shape[0]
    seq = x.shape[1]
    length = N_META + seq
    lp = -(-length // ROW_TILE) * ROW_TILE

    shard_shapes = [weights[n].shape for n in SHARDED]
    packed_w = _pack_rows([weights[n] for n in SHARDED]).astype(BF16)
    g_w, g_meta = all_gather(packed_w, meta_tokens, "all_gather")
    full, at = {}, 0
    for n, shp in zip(SHARDED, shard_shapes):
        rows = shp[0] * shp[1] * shp[2] // LANES
        full[n] = _gathered_full(g_w[:, at:at + rows].reshape((N_DEV,) + shp), n)
        at += rows
    meta_full = jnp.transpose(g_meta, (1, 0, 2)).reshape(N_META, D_MODEL)

    s1, s2, s3 = POOL_WIDTH, POOL_WIDTH + Q_LORA_RANK, POOL_WIDTH + Q_LORA_RANK + KV_LORA_RANK
    zpad = lambda r, n: jnp.zeros((depth, r, n), BF16)
    w_in_p = jnp.concatenate([full["w_in"][:, :, :s3], zpad(D_MODEL, QK_NOPE_DIM), full["w_in"][:, :, s3:],
                              zpad(D_MODEL, HEAD_PAD - QK_HEAD_DIM)], axis=2)
    wq = full["w_q_b"].reshape(depth, Q_LORA_RANK, MLA_HEADS, QK_HEAD_DIM)
    w_q_p = jnp.pad(wq, ((0, 0), (0, 0), (0, 0), (0, HEAD_PAD - QK_HEAD_DIM))).reshape(depth, Q_LORA_RANK, -1)
    wkv = full["w_kv_b"].reshape(depth, KV_LORA_RANK, MLA_HEADS, QK_NOPE_DIM + V_HEAD_DIM)
    w_k_p = jnp.pad(wkv[..., :QK_NOPE_DIM], ((0, 0), (0, 0), (0, 0), (0, HEAD_PAD - QK_NOPE_DIM)))
    w_k_p = w_k_p.reshape(depth, KV_LORA_RANK, -1)
    w_v = wkv[..., QK_NOPE_DIM:].reshape(depth, KV_LORA_RANK, -1)
    w_kv_p = jnp.concatenate([w_k_p, w_v], axis=2)
    w_pool_b = w_pool.astype(BF16)
    kw = MLA_HEADS * HEAD_PAD
    vw = MLA_HEADS * V_HEAD_DIM

    cosf, sins = _rope_lane_tables(lp)
    row = lambda a, l: a[l].reshape(1, -1)

    h = jnp.concatenate([meta_full, x[0], jnp.zeros((lp - length, D_MODEL), F32)], axis=0)
    target = jnp.pad(loss_target[0], ((N_META, lp - length), (0, 0)))
    saved = []
    for l in range(depth):
        gq = _pad_lanes(q_norm_g[l], 0)
        gkn = _pad_lanes(k_norm_g[l, :QK_NOPE_DIM], 0)
        gkr = _pad_lanes(k_norm_g[l, QK_NOPE_DIM:], QK_NOPE_DIM)
        a, u, c_q, c_kv, kr = norm_mm(
            h, row(attn_norm_g, l), w_in_p[l],
            [(0, s1), (s1, Q_LORA_RANK), (s2, KV_LORA_RANK), (s3, HEAD_PAD)], [F32] * 4, "in_proj")
        y_pool = pool_fwd(u, w_pool_b[l], row(pool_scale, l), "pool_fwd")
        qn, q = norm_mm(c_q, row(q_a_norm_g, l), w_q_p[l], [(0, kw)], [F32], "q_proj")
        kvn, kn, v = norm_mm(c_kv, row(kv_a_norm_g, l), w_kv_p[l], [(0, kw), (kw, vw)], [F32, BF16], "kv_proj")
        qp, kp = qk_prep_fwd(q, kn, kr, cosf, sins, gq, gkn, gkr, "qk_prep_fwd")
        o, lse = attn_fwd(qp, kp, v, "attn_fwd")
        cat = jnp.concatenate([y_pool, o.astype(BF16)], axis=1)
        h_mid, g = mm_res(cat, full["w_out"][l], h, "out_proj", gamma=row(ffn_norm_g, l))
        gate, up, act = ffn_up(g, full["w_gate"][l], full["w_up"][l], "ffn_up")
        h_next = mm_res(act, full["w_down"][l], h_mid, "ffn_down")[0]
        saved.append(dict(h=h, a=a, u=u, c_q=c_q, c_kv=c_kv, kr=kr, qn=qn, q=q, kvn=kvn, kn=kn, v=v, qp=qp, kp=kp,
                          o=o, lse=lse, cat=cat, h_mid=h_mid, g=g, gate=gate, up=up, act=act,
                          gq=gq, gkn=gkn, gkr=gkr))
        h = h_next

    dh, sq = loss_head(h, target, seq, "loss_head")
    loss = lax.psum(0.5 / D_MODEL * jnp.sum(sq), ("x", "y", "c"))

    grads = {n: [None] * depth for n in order if n != "meta_tokens"}
    for l in reversed(range(depth)):
        s = saved[l]
        dgate, dup = ffn_bwd_act(dh, full["w_down"][l], s["gate"], s["up"], "ffn_bwd_act")
        grads["w_down"][l] = mm_tn(s["act"], dh, "dw_down")
        grads["w_gate"][l] = mm_tn(s["g"], dgate, "dw_gate")
        grads["w_up"][l] = mm_tn(s["g"], dup, "dw_up")
        dh_mid, dg_ffn = mm_nt_normbwd([(dgate, full["w_gate"][l]), (dup, full["w_up"][l])], s["h_mid"],
                                       row(ffn_norm_g, l), dh, "ffn_bwd_in")
        grads["ffn_norm_g"][l] = dg_ffn[0]
        grads["w_out"][l] = mm_tn(s["cat"], dh_mid, "dw_out")
        dy_pool, do = mm_nt(dh_mid, full["w_out"][l], [(0, POOL_WIDTH), (POOL_WIDTH, vw)], "out_proj_bwd")
        du, dw_pool, dscale = pool_bwd(s["u"], dy_pool, w_pool_b[l], row(pool_scale, l), "pool_bwd")
        grads["w_pool"][l] = dw_pool
        grads["pool_scale"][l] = dscale[0]
        delta = attn_delta(do, s["o"], "attn_delta")
        dqp, dkp, dv = attn_bwd(s["qp"], s["kp"], s["v"], do, s["lse"], delta, "attn_bwd")
        dq, dkn, dkr, dgq, dgkn, dgkr = qk_prep_bwd(dqp, dkp, s["q"], s["kn"], s["kr"], cosf, sins,
                                                    s["gq"], s["gkn"], s["gkr"], "qk_prep_bwd")
        grads["q_norm_g"][l] = dgq[0, :QK_HEAD_DIM]
        grads["k_norm_g"][l] = jnp.concatenate([dgkn[0, :QK_NOPE_DIM], dgkr[0, QK_NOPE_DIM:QK_HEAD_DIM]])
        dwq = mm_tn(s["qn"], dq, "dw_q")
        grads["w_q_b"][l] = dwq.reshape(Q_LORA_RANK, MLA_HEADS, HEAD_PAD)[:, :, :QK_HEAD_DIM].reshape(Q_LORA_RANK, -1)
        dc_q, dg_qa = mm_nt_normbwd([(dq, w_q_p[l])], s["c_q"], row(q_a_norm_g, l), None, "q_proj_bwd")
        grads["q_a_norm_g"][l] = dg_qa[0]
        dwk = mm_tn(s["kvn"], dkn, "dw_k").reshape(KV_LORA_RANK, MLA_HEADS, HEAD_PAD)[:, :, :QK_NOPE_DIM]
        dwv = mm_tn(s["kvn"], dv, "dw_v").reshape(KV_LORA_RANK, MLA_HEADS, V_HEAD_DIM)
        grads["w_kv_b"][l] = jnp.concatenate([dwk, dwv], axis=2).reshape(KV_LORA_RANK, -1)
        dc_kv, dg_kva = mm_nt_normbwd([(dkn, w_k_p[l]), (dv, w_v[l])], s["c_kv"], row(kv_a_norm_g, l), None,
                                      "kv_proj_bwd")
        grads["kv_a_norm_g"][l] = dg_kva[0]
        dw_in_parts = [mm_tn(s["a"], du, "dw_in_pool"), mm_tn(s["a"], dc_q, "dw_in_q"),
                       mm_tn(s["a"], dc_kv, "dw_in_kv"),
                       mm_tn(s["a"], dkr, "dw_in_rope")[:, QK_NOPE_DIM:QK_HEAD_DIM]]
        grads["w_in"][l] = jnp.concatenate(dw_in_parts, axis=1)
        w_in_l = w_in_p[l]
        dh, dg_attn = mm_nt_normbwd(
            [(du, w_in_l[:, :s1]), (dc_q, w_in_l[:, s1:s2]), (dc_kv, w_in_l[:, s2:s3]), (dkr, w_in_l[:, s3:])],
            s["h"], row(attn_norm_g, l), dh_mid, "in_proj_bwd")
        grads["attn_norm_g"][l] = dg_attn[0]

    grad_x = dh[N_META:length][None]
    d_meta = dh[:N_META]
    grads = {n: jnp.stack(g, axis=0) for n, g in grads.items()}

    sharded_slots = jnp.concatenate(
        [_to_slots(grads[n], n).reshape(N_DEV, -1, LANES) for n in SHARDED], axis=1).astype(BF16)
    rep_shapes = [weights[n].shape for n in REPLICATED]
    rep_count = sum(int(jnp.size(weights[n])) for n in REPLICATED)
    rep_rows = -(-rep_count // (8 * LANES)) * 8
    rep_packed = _pack_flat([grads[n] for n in REPLICATED], rep_rows)
    meta_slots = jnp.transpose(d_meta.reshape(N_META, N_DEV, LANES), (1, 0, 2))
    small_slots = jnp.concatenate([meta_slots, jnp.broadcast_to(rep_packed[None], (N_DEV, rep_rows, LANES))], axis=1)
    got_big, got_small = all_to_all(sharded_slots, small_slots, "grad_exchange")

    pk = lambda src: _pack_rows([src[n] for n in SHARDED])
    big_out = adamw(got_big, pk(weights), pk(mom1), pk(mom2), ADAM_TILE, "adamw_sharded")
    ps = lambda src: jnp.concatenate(
        [src["meta_tokens"], _pack_flat([src[n] for n in REPLICATED], rep_rows)], axis=0)
    small_out = adamw(got_small, ps(weights), ps(mom1), ps(mom2), N_META + rep_rows, "adamw_small")

    results = []
    for big_arr, small_arr in zip(big_out, small_out):
        per = dict(zip(SHARDED, _unpack_rows(big_arr, shard_shapes)))
        per["meta_tokens"] = small_arr[:N_META]
        per.update(zip(REPLICATED, _unpack_flat(small_arr[N_META:], rep_shapes)))
        results.append([per[n] for n in order])
    return (loss, grad_x, *results[0], *results[1], *results[2], *results[3])
```

```python
import functools

import jax
import jax.numpy as jnp
from jax import lax
from jax.experimental import pallas as pl
from jax.experimental.pallas import tpu as pltpu

F32 = jnp.float32
BF16 = jnp.bfloat16

D_MODEL = 1024
N_META = 16
POOL_WIDTH = 512
POOL_WINDOWS = (2, 4, 8, 16)
POOL_GROUP_DIM = 128
POOL_HALO = 16
MLA_HEADS = 8
QK_NOPE_DIM = 64
QK_ROPE_DIM = 32
QK_HEAD_DIM = 96
V_HEAD_DIM = 64
HEAD_PAD = 128
Q_LORA_RANK = 384
KV_LORA_RANK = 256
IN_WIDTH = 1184
IN_WIDTH_PAD = 1280
D_FF = 2816
ROPE_THETA = 10000.0
RMS_EPS = 1e-6
ATTN_SCALE = QK_HEAD_DIM ** -0.5

ADAM_LR = 0.001
ADAM_B1 = 0.9
ADAM_B2 = 0.999
ADAM_EPS = 1e-08
ADAM_WD = 0.01
ADAM_STEP = 10

N_DEV = 8
LANES = 128
ROW_TILE = 384
LONG_TILE = 1056
FF_TILE = 1408
VMEM_LIMIT = 56 * 1024 * 1024


def _params(*sem):
    return pltpu.CompilerParams(dimension_semantics=sem, vmem_limit_bytes=VMEM_LIMIT)


def _row_spec(tile, width):
    return pl.BlockSpec((tile, width), lambda i: (i, 0))


def _const_spec(shape):
    return pl.BlockSpec(shape, lambda i: tuple(0 for _ in shape))


def _weight_spec(w, layer):
    if w.ndim == 2:
        return pl.BlockSpec(w.shape, lambda *_: (0, 0))
    return pl.BlockSpec((None,) + w.shape[1:], lambda *_: (layer, 0, 0))


def _nt(a, b):
    return lax.dot_general(a, b, (((1,), (1,)), ((), ())), preferred_element_type=F32)


def _tn(a, b):
    return lax.dot_general(a, b, (((0,), (0,)), ((), ())), preferred_element_type=F32)


def _nn(a, b):
    return jnp.dot(a, b, preferred_element_type=F32)


def _silu(g):
    return g * (1.0 / (1.0 + jnp.exp(-g)))


def norm_mm(x, gamma, w, splits, dtypes, name, layer=0):
    L, K = x.shape
    N = w.shape[-1]
    tm = ROW_TILE

    def body(x_ref, g_ref, w_ref, a_ref, *z_refs):
        xf = x_ref[...]
        r = lax.rsqrt(jnp.mean(xf * xf, axis=-1, keepdims=True) + RMS_EPS)
        a = ((xf * r) * g_ref[...]).astype(BF16)
        a_ref[...] = a
        z = _nn(a, w_ref[...])
        for (s, n), zr in zip(splits, z_refs):
            zr[...] = z[:, s:s + n].astype(zr.dtype)

    widths = [n for _, n in splits]
    return pl.pallas_call(
        body,
        grid=(L // tm,),
        in_specs=[_row_spec(tm, K), _const_spec((1, K)), _weight_spec(w, layer)],
        out_specs=[_row_spec(tm, K)] + [_row_spec(tm, n) for n in widths],
        out_shape=[jax.ShapeDtypeStruct((L, K), BF16)]
        + [jax.ShapeDtypeStruct((L, n), dt) for n, dt in zip(widths, dtypes)],
        compiler_params=_params("arbitrary"),
        name=name,
    )(x, gamma, w)


def ffn_up(g, w_gate, w_up, layer, name):
    L, K = g.shape
    F = w_gate.shape[2]
    tm, tf = ROW_TILE, FF_TILE

    def body(a_ref, wg_ref, wu_ref, gate_ref, up_ref, act_ref):
        a = a_ref[...]
        gate = _nn(a, wg_ref[...])
        up = _nn(a, wu_ref[...])
        gate_ref[...] = gate
        up_ref[...] = up
        act_ref[...] = (_silu(gate) * up).astype(BF16)

    tile = pl.BlockSpec((tm, tf), lambda j, i: (i, j))
    return pl.pallas_call(
        body,
        grid=(F // tf, L // tm),
        in_specs=[
            pl.BlockSpec((tm, K), lambda j, i: (i, 0)),
            pl.BlockSpec((None, K, tf), lambda j, i: (layer, 0, j)),
            pl.BlockSpec((None, K, tf), lambda j, i: (layer, 0, j)),
        ],
        out_specs=[tile, tile, tile],
        out_shape=[
            jax.ShapeDtypeStruct((L, F), F32),
            jax.ShapeDtypeStruct((L, F), F32),
            jax.ShapeDtypeStruct((L, F), BF16),
        ],
        compiler_params=_params("arbitrary", "arbitrary"),
        name=name,
    )(g, w_gate, w_up)


def mm_res(a, w, res, name, gamma=None, layer=0):
    L, K = a.shape
    N = w.shape[-1]
    tm = ROW_TILE
    normed = gamma is not None

    def body(a_ref, w_ref, r_ref, *rest):
        out = r_ref[...] + _nn(a_ref[...], w_ref[...])
        if normed:
            g_ref, o_ref, n_ref = rest
            r = lax.rsqrt(jnp.mean(out * out, axis=-1, keepdims=True) + RMS_EPS)
            n_ref[...] = ((out * r) * g_ref[...]).astype(BF16)
        else:
            (o_ref,) = rest
        o_ref[...] = out

    in_specs = [_row_spec(tm, K), _weight_spec(w, layer), _row_spec(tm, N)]
    out_specs = [_row_spec(tm, N)]
    out_shape = [jax.ShapeDtypeStruct((L, N), F32)]
    args = [a, w, res]
    if normed:
        in_specs.append(_const_spec((1, N)))
        out_specs.append(_row_spec(tm, N))
        out_shape.append(jax.ShapeDtypeStruct((L, N), BF16))
        args.append(gamma)
    return pl.pallas_call(
        body,
        grid=(L // tm,),
        in_specs=in_specs,
        out_specs=out_specs,
        out_shape=out_shape,
        compiler_params=_params("arbitrary"),
        name=name,
    )(*args)


def mm_nt(dz, w, splits, name, layer=0):
    L, N = dz.shape
    K = w.shape[-2]
    tm = ROW_TILE

    def body(dz_ref, w_ref, *o_refs):
        d = _nt(dz_ref[...].astype(BF16), w_ref[...])
        for (s, n), o in zip(splits, o_refs):
            o[...] = d[:, s:s + n]

    return pl.pallas_call(
        body,
        grid=(L // tm,),
        in_specs=[_row_spec(tm, N), _weight_spec(w, layer)],
        out_specs=[_row_spec(tm, n) for _, n in splits],
        out_shape=[jax.ShapeDtypeStruct((L, n), F32) for _, n in splits],
        compiler_params=_params("arbitrary"),
        name=name,
    )(dz, w)


def mm_nt_normbwd(pairs, x, gamma, dres, name, layer=0):
    L, K = x.shape
    tm = ROW_TILE
    n_pairs = len(pairs)
    has_res = dres is not None
    weights = []
    for _, w, _ in pairs:
        if not any(w is u for u in weights):
            weights.append(w)
    which = [[w is u for u in weights].index(True) for _, w, _ in pairs]
    n_in = n_pairs + len(weights)

    def body(*refs):
        dz_refs = refs[:n_pairs]
        w_refs = [refs[n_pairs + n] for n in which]
        x_ref, g_ref = refs[n_in], refs[n_in + 1]
        pos = n_in + 2
        r_ref = refs[pos] if has_res else None
        dx_ref, dg_ref = refs[pos + has_res], refs[pos + has_res + 1]
        da = None
        for dz_ref, w_ref, (dz, _, at) in zip(dz_refs, w_refs, pairs):
            t = _nt(dz_ref[...].astype(BF16), w_ref[:, at:at + dz.shape[1]])
            da = t if da is None else da + t
        xf = x_ref[...]
        r = lax.rsqrt(jnp.mean(xf * xf, axis=-1, keepdims=True) + RMS_EPS)
        xh = xf * r

        @pl.when(pl.program_id(0) == 0)
        def _():
            dg_ref[...] = jnp.zeros_like(dg_ref)

        dg_ref[...] += jnp.sum(da * xh, axis=0, keepdims=True)
        dxh = da * g_ref[...]
        dx = r * (dxh - xh * jnp.mean(dxh * xh, axis=-1, keepdims=True))
        if has_res:
            dx = dx + r_ref[...]
        dx_ref[...] = dx

    in_specs = [_row_spec(tm, dz.shape[1]) for dz, _, _ in pairs]
    in_specs += [_weight_spec(w, layer) for w in weights]
    in_specs += [_row_spec(tm, K), _const_spec((1, K))]
    args = [dz for dz, _, _ in pairs] + weights + [x, gamma]
    if has_res:
        in_specs.append(_row_spec(tm, K))
        args.append(dres)
    return pl.pallas_call(
        body,
        grid=(L // tm,),
        in_specs=in_specs,
        out_specs=[_row_spec(tm, K), _const_spec((1, K))],
        out_shape=[jax.ShapeDtypeStruct((L, K), F32), jax.ShapeDtypeStruct((1, K), F32)],
        compiler_params=_params("arbitrary"),
        name=name,
    )(*args)


def mm_tn(a, b, name):
    L, K = a.shape
    N = b.shape[1]
    tl = LONG_TILE if L % LONG_TILE == 0 else ROW_TILE
    tk = FF_TILE if K % FF_TILE == 0 and K > FF_TILE else K
    tn = FF_TILE if N % FF_TILE == 0 and N > FF_TILE else N

    def body(a_ref, b_ref, o_ref):
        @pl.when(pl.program_id(2) == 0)
        def _():
            o_ref[...] = jnp.zeros_like(o_ref)

        o_ref[...] += _tn(a_ref[...].astype(BF16), b_ref[...].astype(BF16))

    return pl.pallas_call(
        body,
        grid=(K // tk, N // tn, L // tl),
        in_specs=[
            pl.BlockSpec((tl, tk), lambda k, n, l: (l, k)),
            pl.BlockSpec((tl, tn), lambda k, n, l: (l, n)),
        ],
        out_specs=pl.BlockSpec((tk, tn), lambda k, n, l: (k, n)),
        out_shape=jax.ShapeDtypeStruct((K, N), F32),
        compiler_params=_params("arbitrary", "arbitrary", "arbitrary"),
        name=name,
    )(a, b)


def ffn_bwd_act(dh, w_down, layer, gate, up, name):
    L, K = dh.shape
    F = w_down.shape[1]
    tm, tf = ROW_TILE, FF_TILE

    def body(dh_ref, w_ref, gate_ref, up_ref, dgate_ref, dup_ref):
        dact = _nt(dh_ref[...].astype(BF16), w_ref[...])
        g = gate_ref[...]
        sig = 1.0 / (1.0 + jnp.exp(-g))
        dup_ref[...] = (dact * (g * sig)).astype(BF16)
        dgate_ref[...] = (dact * up_ref[...] * (sig * (1.0 + g * (1.0 - sig)))).astype(BF16)

    return pl.pallas_call(
        body,
        grid=(F // tf, L // tm),
        in_specs=[
            pl.BlockSpec((tm, K), lambda j, i: (i, 0)),
            pl.BlockSpec((None, tf, K), lambda j, i: (layer, j, 0)),
            pl.BlockSpec((tm, tf), lambda j, i: (i, j)),
            pl.BlockSpec((tm, tf), lambda j, i: (i, j)),
        ],
        out_specs=[
            pl.BlockSpec((tm, tf), lambda j, i: (i, j)),
            pl.BlockSpec((tm, tf), lambda j, i: (i, j)),
        ],
        out_shape=[jax.ShapeDtypeStruct((L, F), BF16), jax.ShapeDtypeStruct((L, F), BF16)],
        compiler_params=_params("arbitrary", "arbitrary"),
        name=name,
    )(dh, w_down, gate, up)


def _pool_residual(scr, lo, tm, g, w, t):
    cols = slice(g * POOL_GROUP_DIM, (g + 1) * POOL_GROUP_DIM)
    cur = scr[lo:lo + tm, cols]
    s = cur
    for k in range(1, w):
        s = s + scr[lo - k:lo - k + tm, cols]
    cnt = jnp.minimum(t + 1, w).astype(F32)
    return s / cnt - cur


def pool_fwd(u, w_pool, scale, name):
    L, C = u.shape
    tm, halo = ROW_TILE, POOL_HALO

    def body(u_ref, halo_ref, w_ref, s_ref, y_ref, scr):
        i = pl.program_id(0)
        scr[0:halo, :] = jnp.where(i > 0, halo_ref[...], 0.0)
        scr[halo:halo + tm, :] = u_ref[...]
        t = i * tm + lax.broadcasted_iota(jnp.int32, (tm, POOL_GROUP_DIM), 0)
        for g, w in enumerate(POOL_WINDOWS):
            cols = slice(g * POOL_GROUP_DIM, (g + 1) * POOL_GROUP_DIM)
            p = _pool_residual(scr, halo, tm, g, w, t)
            y = _nn(p.astype(BF16), w_ref[g]) * s_ref[:, cols]
            y_ref[:, cols] = y.astype(y_ref.dtype)

    return pl.pallas_call(
        body,
        grid=(L // tm,),
        in_specs=[
            _row_spec(tm, C),
            pl.BlockSpec((halo, C), lambda i: (jnp.maximum(i * (tm // halo) - 1, 0), 0)),
            _const_spec(w_pool.shape),
            _const_spec((1, C)),
        ],
        out_specs=_row_spec(tm, C),
        out_shape=jax.ShapeDtypeStruct((L, 2 * C), BF16),
        scratch_shapes=[pltpu.VMEM((tm + halo, C), F32)],
        compiler_params=_params("arbitrary"),
        name=name,
    )(u, u, w_pool, scale)


def pool_bwd(u, dy, w_pool, scale, name):
    L, C = u.shape
    tm, halo = ROW_TILE, POOL_HALO
    n_tiles = L // tm
    last_halo = L // halo - 1

    def body(u_ref, uh_ref, dy_ref, dyh_ref, w_ref, s_ref, du_ref, dw_ref, ds_ref, scr_u, scr_q):
        i = pl.program_id(0)

        @pl.when(i == 0)
        def _():
            dw_ref[...] = jnp.zeros_like(dw_ref)
            ds_ref[...] = jnp.zeros_like(ds_ref)

        scr_u[0:halo, :] = jnp.where(i > 0, uh_ref[...], 0.0)
        scr_u[halo:halo + tm, :] = u_ref[...]
        t = i * tm + lax.broadcasted_iota(jnp.int32, (tm, POOL_GROUP_DIM), 0)
        th = (i + 1) * tm + lax.broadcasted_iota(jnp.int32, (halo, POOL_GROUP_DIM), 0)
        for g, w in enumerate(POOL_WINDOWS):
            cols = slice(g * POOL_GROUP_DIM, (g + 1) * POOL_GROUP_DIM)
            p = _pool_residual(scr_u, halo, tm, g, w, t).astype(BF16)
            wg = w_ref[g]
            sc = s_ref[:, cols]
            dy = dy_ref[:, cols]
            ds_ref[:, cols] += jnp.sum(dy * _nn(p, wg), axis=0, keepdims=True)
            dys = (dy * sc).astype(BF16)
            dw_ref[g] += _tn(p, dys)
            dp = _nt(dys, wg)
            dyh = jnp.where(i < n_tiles - 1, dyh_ref[:, cols], 0.0)
            dph = _nt((dyh * sc).astype(BF16), wg)
            scr_q[0:tm, cols] = dp / jnp.minimum(t + 1, w).astype(F32)
            scr_q[tm:tm + halo, cols] = dph / jnp.minimum(th + 1, w).astype(F32)
            acc = scr_q[0:tm, cols]
            for k in range(1, w):
                acc = acc + scr_q[k:k + tm, cols]
            du_ref[:, cols] = acc - dp

    return pl.pallas_call(
        body,
        grid=(n_tiles,),
        in_specs=[
            _row_spec(tm, C),
            pl.BlockSpec((halo, C), lambda i: (jnp.maximum(i * (tm // halo) - 1, 0), 0)),
            _row_spec(tm, C),
            pl.BlockSpec((halo, C), lambda i: (jnp.minimum((i + 1) * (tm // halo), last_halo), 0)),
            _const_spec(w_pool.shape),
            _const_spec((1, C)),
        ],
        out_specs=[_row_spec(tm, C), _const_spec(w_pool.shape), _const_spec((1, C))],
        out_shape=[
            jax.ShapeDtypeStruct((L, C), F32),
            jax.ShapeDtypeStruct(w_pool.shape, F32),
            jax.ShapeDtypeStruct((1, C), F32),
        ],
        scratch_shapes=[pltpu.VMEM((tm + halo, C), F32), pltpu.VMEM((tm + halo, C), F32)],
        compiler_params=_params("arbitrary"),
        name=name,
    )(u, u, dy, dy, w_pool, scale)


def _head_masks(rows):
    lane = lax.broadcasted_iota(jnp.int32, (rows, HEAD_PAD), 1)
    return lane, lane < QK_NOPE_DIM, (lane >= QK_NOPE_DIM) & (lane < QK_HEAD_DIM)


def _rope_swap(x, lane):
    half = QK_ROPE_DIM // 2
    swapped = jnp.where(lane < QK_NOPE_DIM + half, pltpu.roll(x, HEAD_PAD - half, 1), pltpu.roll(x, half, 1))
    return jnp.where((lane >= QK_NOPE_DIM) & (lane < QK_HEAD_DIM), swapped, 0.0)


def _seg_mean(v, m_n, m_r):
    mn = jnp.sum(jnp.where(m_n, v, 0.0), axis=1, keepdims=True) * (1.0 / QK_NOPE_DIM)
    mr = jnp.sum(jnp.where(m_r, v, 0.0), axis=1, keepdims=True) * (1.0 / QK_ROPE_DIM)
    return jnp.where(m_n, mn, mr)


def qk_prep_fwd(q, kn, kr, cosf, sins, gq, gkn, gkr, name):
    L = q.shape[0]
    tm = ROW_TILE
    W = MLA_HEADS * HEAD_PAD

    def body(q_ref, kn_ref, kr_ref, c_ref, s_ref, gq_ref, gkn_ref, gkr_ref, qo_ref, ko_ref):
        lane, m_n, m_r = _head_masks(tm)
        cosf_, sins_ = c_ref[...], s_ref[...]
        kr_ = kr_ref[...]
        rk = lax.rsqrt(_seg_mean(kr_ * kr_, m_n, m_r) + RMS_EPS)
        krn = kr_ * rk * gkr_ref[...]
        krf = krn * cosf_ + _rope_swap(krn, lane) * sins_
        for h in range(MLA_HEADS):
            cols = slice(h * HEAD_PAD, (h + 1) * HEAD_PAD)
            qh = q_ref[:, cols]
            qn = qh * lax.rsqrt(_seg_mean(qh * qh, m_n, m_r) + RMS_EPS) * gq_ref[...]
            qo_ref[:, cols] = (qn * cosf_ + _rope_swap(qn, lane) * sins_).astype(BF16)
            kh = kn_ref[:, cols]
            rn = lax.rsqrt(jnp.sum(kh * kh, axis=1, keepdims=True) * (1.0 / QK_NOPE_DIM) + RMS_EPS)
            ko_ref[:, cols] = (kh * rn * gkn_ref[...] + krf).astype(BF16)

    vec = _const_spec((1, HEAD_PAD))
    return pl.pallas_call(
        body,
        grid=(L // tm,),
        in_specs=[_row_spec(tm, W), _row_spec(tm, W), _row_spec(tm, HEAD_PAD), _row_spec(tm, HEAD_PAD),
                  _row_spec(tm, HEAD_PAD), vec, vec, vec],
        out_specs=[_row_spec(tm, W), _row_spec(tm, W)],
        out_shape=[jax.ShapeDtypeStruct((L, W), BF16), jax.ShapeDtypeStruct((L, W), BF16)],
        compiler_params=_params("arbitrary"),
        name=name,
    )(q, kn, kr, cosf, sins, gq, gkn, gkr)


def qk_prep_bwd(dqo, dko, q, kn, kr, cosf, sins, gq, gkn, gkr, name):
    L = q.shape[0]
    tm = ROW_TILE
    W = MLA_HEADS * HEAD_PAD

    def body(dqo_ref, dko_ref, q_ref, kn_ref, kr_ref, c_ref, s_ref, gq_ref, gkn_ref, gkr_ref,
             dq_ref, dkn_ref, dkr_ref, dgq_ref, dgkn_ref, dgkr_ref):
        @pl.when(pl.program_id(0) == 0)
        def _():
            dgq_ref[...] = jnp.zeros_like(dgq_ref)
            dgkn_ref[...] = jnp.zeros_like(dgkn_ref)
            dgkr_ref[...] = jnp.zeros_like(dgkr_ref)

        lane, m_n, m_r = _head_masks(tm)
        cosf_, sins_ = c_ref[...], s_ref[...]
        dgq = jnp.zeros((1, HEAD_PAD), F32)
        dgkn = jnp.zeros((1, HEAD_PAD), F32)
        dkrf = jnp.zeros((tm, HEAD_PAD), F32)
        for h in range(MLA_HEADS):
            cols = slice(h * HEAD_PAD, (h + 1) * HEAD_PAD)
            dy = dqo_ref[:, cols]
            dqn = dy * cosf_ + _rope_swap(dy * sins_, lane)
            qh = q_ref[:, cols]
            rinv = lax.rsqrt(_seg_mean(qh * qh, m_n, m_r) + RMS_EPS)
            xh = qh * rinv
            dgq = dgq + jnp.sum(dqn * xh, axis=0, keepdims=True)
            dxh = dqn * gq_ref[...]
            dq_ref[:, cols] = rinv * (dxh - xh * _seg_mean(dxh * xh, m_n, m_r))
            dk = dko_ref[:, cols]
            dkrf = dkrf + dk
            kh = kn_ref[:, cols]
            rn = lax.rsqrt(jnp.sum(kh * kh, axis=1, keepdims=True) * (1.0 / QK_NOPE_DIM) + RMS_EPS)
            xk = kh * rn
            dgkn = dgkn + jnp.sum(dk * xk, axis=0, keepdims=True)
            dxk = dk * gkn_ref[...]
            dkn_ref[:, cols] = rn * (dxk - xk * (jnp.sum(dxk * xk, axis=1, keepdims=True) * (1.0 / QK_NOPE_DIM)))
        kr_ = kr_ref[...]
        rk = lax.rsqrt(_seg_mean(kr_ * kr_, m_n, m_r) + RMS_EPS)
        xr = kr_ * rk
        dkrn = dkrf * cosf_ + _rope_swap(dkrf * sins_, lane)
        dgkr_ref[...] += jnp.sum(dkrn * xr, axis=0, keepdims=True)
        dxr = dkrn * gkr_ref[...]
        dkr_ref[...] = rk * (dxr - xr * _seg_mean(dxr * xr, m_n, m_r))
        dgq_ref[...] += dgq
        dgkn_ref[...] += dgkn

    vec = _const_spec((1, HEAD_PAD))
    return pl.pallas_call(
        body,
        grid=(L // tm,),
        in_specs=[_row_spec(tm, W), _row_spec(tm, W), _row_spec(tm, W), _row_spec(tm, W),
                  _row_spec(tm, HEAD_PAD), _row_spec(tm, HEAD_PAD), _row_spec(tm, HEAD_PAD), vec, vec, vec],
        out_specs=[_row_spec(tm, W), _row_spec(tm, W), _row_spec(tm, HEAD_PAD), vec, vec, vec],
        out_shape=[jax.ShapeDtypeStruct((L, W), F32), jax.ShapeDtypeStruct((L, W), F32),
                   jax.ShapeDtypeStruct((L, HEAD_PAD), F32)] + [jax.ShapeDtypeStruct((1, HEAD_PAD), F32)] * 3,
        compiler_params=_params("arbitrary"),
        name=name,
    )(dqo, dko, q, kn, kr, cosf, sins, gq, gkn, gkr)


def attn_fwd(qp, kp, v, cat, name):
    L = qp.shape[0]
    tq = ROW_TILE
    pair_w = 2 * HEAD_PAD
    n_pairs = MLA_HEADS // 2

    def body(q_ref, k_ref, v_ref, cat_in, o_ref, lse_ref, cat_ref):
        del cat_in
        i = pl.program_id(1)
        left = lax.broadcasted_iota(jnp.int32, (tq, LANES), 1) < V_HEAD_DIM
        row = lax.broadcasted_iota(jnp.int32, (tq, tq), 0)
        col = lax.broadcasted_iota(jnp.int32, (tq, tq), 1)

        def step(j, carry, masked):
            rows = pl.ds(pl.multiple_of(j * tq, tq), tq)
            vv = v_ref[rows, :]
            out = []
            for hh in range(2):
                cols = slice(hh * HEAD_PAD, (hh + 1) * HEAD_PAD)
                m, l, acc = carry[hh]
                s = _nt(q_ref[:, cols], k_ref[rows, cols]) * ATTN_SCALE
                if masked:
                    s = jnp.where(col <= row, s, -jnp.inf)
                m_new = jnp.maximum(m, jnp.max(s, axis=1, keepdims=True))
                alpha = jnp.exp(m - m_new)
                p = jnp.exp(s - m_new)
                l = alpha * l + jnp.sum(p, axis=1, keepdims=True)
                acc = alpha * acc + _nn(p.astype(BF16), vv)
                out.append((m_new, l, acc))
            return tuple(out)

        one = (jnp.full((tq, 1), -jnp.inf, F32), jnp.zeros((tq, 1), F32), jnp.zeros((tq, LANES), F32))
        carry = lax.fori_loop(0, i, functools.partial(step, masked=False), (one, one))
        (m0, l0, a0), (m1, l1, a1) = step(i, carry, True)
        o = jnp.where(left, a0 / l0, a1 / l1)
        o_ref[...] = o
        cat_ref[...] = o.astype(BF16)
        lse_ref[...] = jnp.where(left, m0 + jnp.log(l0), m1 + jnp.log(l1))

    return pl.pallas_call(
        body,
        grid=(n_pairs, L // tq),
        in_specs=[
            pl.BlockSpec((tq, pair_w), lambda p, i: (i, p)),
            pl.BlockSpec((L, pair_w), lambda p, i: (0, p)),
            pl.BlockSpec((L, LANES), lambda p, i: (0, p)),
            pl.BlockSpec(memory_space=pl.ANY),
        ],
        out_specs=[
            pl.BlockSpec((tq, LANES), lambda p, i: (i, p)),
            pl.BlockSpec((tq, LANES), lambda p, i: (i, p)),
            pl.BlockSpec((tq, LANES), lambda p, i: (i, n_pairs + p)),
        ],
        out_shape=[jax.ShapeDtypeStruct((L, n_pairs * LANES), F32), jax.ShapeDtypeStruct((L, n_pairs * LANES), F32),
                   jax.ShapeDtypeStruct(cat.shape, cat.dtype)],
        input_output_aliases={3: 2},
        compiler_params=_params("arbitrary", "arbitrary"),
        name=name,
    )(qp, kp, v, cat)


def attn_delta(do, o, name):
    L, C = o.shape
    tm = ROW_TILE

    def body(do_ref, o_ref, d_ref):
        left = lax.broadcasted_iota(jnp.int32, (tm, LANES), 1) < V_HEAD_DIM
        for p in range(C // LANES):
            cols = slice(p * LANES, (p + 1) * LANES)
            prod = do_ref[:, cols] * o_ref[:, cols]
            d0 = jnp.sum(jnp.where(left, prod, 0.0), axis=1, keepdims=True)
            d1 = jnp.sum(jnp.where(left, 0.0, prod), axis=1, keepdims=True)
            d_ref[:, cols] = jnp.where(left, d0, d1)

    return pl.pallas_call(
        body,
        grid=(L // tm,),
        in_specs=[_row_spec(tm, C), _row_spec(tm, C)],
        out_specs=_row_spec(tm, C),
        out_shape=jax.ShapeDtypeStruct((L, C), F32),
        compiler_params=_params("arbitrary"),
        name=name,
    )(do, o)


def attn_bwd(qp, kp, v, do, lse, delta, name):
    L = qp.shape[0]
    tq = ROW_TILE
    n_q = L // tq
    pair_w = 2 * HEAD_PAD

    def body(q_ref, k_ref, v_ref, do_ref, lse_ref, dl_ref, dq_ref, dk_ref, dv_ref):
        j = pl.program_id(1)

        @pl.when(j == 0)
        def _():
            dq_ref[...] = jnp.zeros_like(dq_ref)

        dk_ref[...] = jnp.zeros_like(dk_ref)
        dv_ref[...] = jnp.zeros_like(dv_ref)
        left = lax.broadcasted_iota(jnp.int32, (tq, LANES), 1) < V_HEAD_DIM
        row = lax.broadcasted_iota(jnp.int32, (tq, tq), 0)
        col = lax.broadcasted_iota(jnp.int32, (tq, tq), 1)

        def step(i, carry, masked):
            rows = pl.ds(pl.multiple_of(i * tq, tq), tq)
            do = do_ref[rows, :]
            vv = v_ref[...]
            dv = None
            for hh in range(2):
                cols = slice(hh * HEAD_PAD, (hh + 1) * HEAD_PAD)
                stat = slice(hh * V_HEAD_DIM, hh * V_HEAD_DIM + 1)
                q = q_ref[rows, cols]
                k = k_ref[:, cols]
                dom = jnp.where(left if hh == 0 else jnp.logical_not(left), do, 0.0).astype(BF16)
                s = _nt(q, k) * ATTN_SCALE
                p = jnp.exp(s - lse_ref[rows, stat])
                if masked:
                    p = jnp.where(col <= row, p, 0.0)
                dp = _nt(dom, vv)
                ds = (p * (dp - dl_ref[rows, stat]) * ATTN_SCALE).astype(BF16)
                dq_ref[rows, cols] += _nn(ds, k)
                dk_ref[:, cols] += _tn(ds, q)
                t = _tn(p.astype(BF16), dom)
                dv = t if dv is None else dv + t
            dv_ref[...] += dv
            return carry

        step(j, 0, True)
        lax.fori_loop(j + 1, n_q, functools.partial(step, masked=False), 0)

    n_pairs = MLA_HEADS // 2
    return pl.pallas_call(
        body,
        grid=(n_pairs, n_q),
        in_specs=[
            pl.BlockSpec((L, pair_w), lambda p, j: (0, p)),
            pl.BlockSpec((tq, pair_w), lambda p, j: (j, p)),
            pl.BlockSpec((tq, LANES), lambda p, j: (j, p)),
            pl.BlockSpec((L, LANES), lambda p, j: (0, p)),
            pl.BlockSpec((L, LANES), lambda p, j: (0, p)),
            pl.BlockSpec((L, LANES), lambda p, j: (0, p)),
        ],
        out_specs=[
            pl.BlockSpec((L, pair_w), lambda p, j: (0, p)),
            pl.BlockSpec((tq, pair_w), lambda p, j: (j, p)),
            pl.BlockSpec((tq, LANES), lambda p, j: (j, p)),
        ],
        out_shape=[
            jax.ShapeDtypeStruct((L, n_pairs * pair_w), F32),
            jax.ShapeDtypeStruct((L, n_pairs * pair_w), F32),
            jax.ShapeDtypeStruct((L, n_pairs * LANES), F32),
        ],
        compiler_params=_params("arbitrary", "arbitrary"),
        name=name,
    )(qp, kp, v, do, lse, delta)


def loss_head(h, target, n_real, name):
    L, D = h.shape
    tm = ROW_TILE

    def body(h_ref, t_ref, dh_ref, sq_ref):
        i = pl.program_id(0)

        @pl.when(i == 0)
        def _():
            sq_ref[...] = jnp.zeros_like(sq_ref)

        t = i * tm + lax.broadcasted_iota(jnp.int32, (tm, D), 0)
        real = (t >= N_META) & (t < N_META + n_real)
        diff = jnp.where(real, h_ref[...] - t_ref[...], 0.0)
        dh_ref[...] = diff * (1.0 / D)
        sq_ref[...] += jnp.sum(diff * diff, axis=0, keepdims=True)

    return pl.pallas_call(
        body,
        grid=(L // tm,),
        in_specs=[_row_spec(tm, D), _row_spec(tm, D)],
        out_specs=[_row_spec(tm, D), _const_spec((1, D))],
        out_shape=[jax.ShapeDtypeStruct((L, D), F32), jax.ShapeDtypeStruct((1, D), F32)],
        compiler_params=_params("arbitrary"),
        name=name,
    )(h, target)


def _mesh_position():
    x, y, c = lax.axis_index("x"), lax.axis_index("y"), lax.axis_index("c")
    return x, y, c, 4 * x + 2 * y + c


def _flip(x, y, c, k):
    px = 1 - x if k & 4 else x
    py = 1 - y if k & 2 else y
    pc = 1 - c if k & 1 else c
    return (px, py, pc), 4 * px + 2 * py + pc


def _other_chips(x, y):
    return [(1 - x, y), (x, 1 - y), (1 - x, 1 - y)]


def _dev_index(px, py, pc):
    return 4 * px + 2 * py + pc


def all_gather(big, small, name):
    n_copies = N_DEV - 1

    def body(big_ref, small_ref, gbig_ref, gsmall_ref, send_sems, recv_sems, local_sems):
        x, y, c, me = _mesh_position()
        sibling = (x, y, 1 - c)
        chips = _other_chips(x, y)
        pay = ((big_ref, gbig_ref), (small_ref, gsmall_ref))
        own = [pltpu.make_async_copy(src, dst.at[me], local_sems.at[b]) for b, (src, dst) in enumerate(pay)]
        for cp in own:
            cp.start()

        def copy(k, b, block, to, src=None):
            dst = pay[b][1].at[block]
            return pltpu.make_async_remote_copy(
                src_ref=dst if src is None else src, dst_ref=dst, send_sem=send_sems.at[2 * k + b],
                recv_sem=recv_sems.at[2 * k + b], device_id=to, device_id_type=pl.DeviceIdType.MESH)

        first = []
        for b in range(2):
            first.append(copy(0, b, me, sibling, src=pay[b][0]))
            first += [copy(1 + n, b, me, (*chip, c), src=pay[b][0]) for n, chip in enumerate(chips)]
        for cp in first:
            cp.start()
        passed = []
        for n, chip in enumerate(chips):
            for b in range(2):
                block = _dev_index(*chip, c)
                copy(1 + n, b, block, sibling).wait_recv()
                passed.append(copy(4 + n, b, block, sibling))
                passed[-1].start()
        for b in range(2):
            copy(0, b, _dev_index(x, y, 1 - c), sibling).wait_recv()
            for n, chip in enumerate(chips):
                copy(4 + n, b, _dev_index(*chip, 1 - c), sibling).wait_recv()
        for cp in first + passed:
            cp.wait_send()
        for cp in own:
            cp.wait()

    any_spec = pl.BlockSpec(memory_space=pl.ANY)
    return pl.pallas_call(
        body,
        in_specs=[any_spec, any_spec],
        out_specs=[any_spec, any_spec],
        out_shape=[jax.ShapeDtypeStruct((N_DEV,) + big.shape, big.dtype),
                   jax.ShapeDtypeStruct((N_DEV,) + small.shape, small.dtype)],
        scratch_shapes=[pltpu.SemaphoreType.DMA((2 * n_copies,)), pltpu.SemaphoreType.DMA((2 * n_copies,)),
                        pltpu.SemaphoreType.DMA((2,))],
        name=name,
    )(big, small)


N_CHIPS = 4


def exchange_pair(big, small, name):
    n_peers = N_DEV - 1

    def body(big_ref, small_ref, sib_ref, gsmall_ref, send_sems, recv_sems, local_sem):
        x, y, c, me = _mesh_position()
        sibling = (x, y, 1 - c)
        own = pltpu.make_async_copy(small_ref.at[me], gsmall_ref.at[me], local_sem.at[0])
        own.start()
        copies = []
        for q in range(N_CHIPS):
            copies.append(pltpu.make_async_remote_copy(
                src_ref=big_ref.at[_dev_index(q // 2, q % 2, 1 - c)], dst_ref=sib_ref.at[q],
                send_sem=send_sems.at[q], recv_sem=recv_sems.at[q],
                device_id=sibling, device_id_type=pl.DeviceIdType.MESH))
        waits = list(copies)
        for k in range(1, N_DEV):
            peer, peer_idx = _flip(x, y, c, k)
            n = N_CHIPS + k - 1
            copies.append(pltpu.make_async_remote_copy(
                src_ref=small_ref.at[peer_idx], dst_ref=gsmall_ref.at[me], send_sem=send_sems.at[n],
                recv_sem=recv_sems.at[n], device_id=peer, device_id_type=pl.DeviceIdType.MESH))
            waits.append(pltpu.make_async_remote_copy(
                src_ref=small_ref.at[peer_idx], dst_ref=gsmall_ref.at[peer_idx], send_sem=send_sems.at[n],
                recv_sem=recv_sems.at[n], device_id=peer, device_id_type=pl.DeviceIdType.MESH))
        for cp in copies:
            cp.start()
        for cp in waits:
            cp.wait_recv()
        for cp in copies:
            cp.wait_send()
        own.wait()

    any_spec = pl.BlockSpec(memory_space=pl.ANY)
    n_sems = N_CHIPS + n_peers
    return pl.pallas_call(
        body,
        in_specs=[any_spec, any_spec],
        out_specs=[any_spec, any_spec],
        out_shape=[jax.ShapeDtypeStruct((N_CHIPS,) + big.shape[1:], big.dtype),
                   jax.ShapeDtypeStruct(small.shape, small.dtype)],
        scratch_shapes=[pltpu.SemaphoreType.DMA((n_sems,)), pltpu.SemaphoreType.DMA((n_sems,)),
                        pltpu.SemaphoreType.DMA((1,))],
        name=name,
    )(big, small)


def pair_sum(big, from_sibling, core, tile, name):
    R = big.shape[1]

    def body(core_ref, mine_ref, sib_ref, o_ref):
        del core_ref
        o_ref[...] = (mine_ref[...].astype(F32) + sib_ref[...].astype(F32)).astype(o_ref.dtype)

    grid_spec = pltpu.PrefetchScalarGridSpec(
        num_scalar_prefetch=1,
        grid=(N_CHIPS, R // tile),
        in_specs=[
            pl.BlockSpec((None, tile, LANES), lambda q, i, core_ref: (2 * q + core_ref[0], i, 0)),
            pl.BlockSpec((None, tile, LANES), lambda q, i, core_ref: (q, i, 0)),
        ],
        out_specs=pl.BlockSpec((None, tile, LANES), lambda q, i, core_ref: (q, i, 0)),
    )
    return pl.pallas_call(
        body,
        grid_spec=grid_spec,
        out_shape=jax.ShapeDtypeStruct((N_CHIPS, R, LANES), big.dtype),
        compiler_params=_params("arbitrary", "arbitrary"),
        name=name,
    )(core, big, from_sibling)


def exchange_chips(partial, name):
    def body(part_ref, got_ref, send_sems, recv_sems, local_sem):
        x, y, c, me = _mesh_position()
        mine = 2 * x + y
        own = pltpu.make_async_copy(part_ref.at[mine], got_ref.at[mine], local_sem.at[0])
        own.start()
        sends, recvs = [], []
        for n, (px, py) in enumerate(_other_chips(x, y)):
            theirs = 2 * px + py
            sends.append(pltpu.make_async_remote_copy(
                src_ref=part_ref.at[theirs], dst_ref=got_ref.at[mine], send_sem=send_sems.at[n],
                recv_sem=recv_sems.at[n], device_id=(px, py, c), device_id_type=pl.DeviceIdType.MESH))
            recvs.append(pltpu.make_async_remote_copy(
                src_ref=part_ref.at[theirs], dst_ref=got_ref.at[theirs], send_sem=send_sems.at[n],
                recv_sem=recv_sems.at[n], device_id=(px, py, c), device_id_type=pl.DeviceIdType.MESH))
        for cp in sends:
            cp.start()
        for cp in recvs:
            cp.wait_recv()
        for cp in sends:
            cp.wait_send()
        own.wait()

    any_spec = pl.BlockSpec(memory_space=pl.ANY)
    return pl.pallas_call(
        body,
        in_specs=[any_spec],
        out_specs=any_spec,
        out_shape=jax.ShapeDtypeStruct(partial.shape, partial.dtype),
        scratch_shapes=[pltpu.SemaphoreType.DMA((N_CHIPS - 1,)), pltpu.SemaphoreType.DMA((N_CHIPS - 1,)),
                        pltpu.SemaphoreType.DMA((1,))],
        name=name,
    )(partial)


def adamw(parts, w, m, v, tile, name):
    R = w.shape[0]
    n_slots = parts.shape[0]

    def body(p_ref, w_ref, m_ref, v_ref, g_ref, d_ref, mo_ref, vo_ref):
        g = p_ref[0].astype(F32)
        for s in range(1, n_slots):
            g = g + p_ref[s].astype(F32)
        m_new = ADAM_B1 * m_ref[...] + (1.0 - ADAM_B1) * g
        v_new = ADAM_B2 * v_ref[...] + (1.0 - ADAM_B2) * (g * g)
        m_hat = m_new / (1.0 - ADAM_B1 ** ADAM_STEP)
        v_hat = v_new / (1.0 - ADAM_B2 ** ADAM_STEP)
        g_ref[...] = g
        d_ref[...] = -ADAM_LR * (m_hat / (jnp.sqrt(v_hat) + ADAM_EPS) + ADAM_WD * w_ref[...])
        mo_ref[...] = m_new
        vo_ref[...] = v_new

    flat = _row_spec(tile, LANES)
    return pl.pallas_call(
        body,
        grid=(R // tile,),
        in_specs=[pl.BlockSpec((n_slots, tile, LANES), lambda i: (0, i, 0)), flat, flat, flat],
        out_specs=[flat, flat, flat, flat],
        out_shape=[jax.ShapeDtypeStruct((R, LANES), F32)] * 4,
        compiler_params=_params("arbitrary"),
        name=name,
    )(parts, w, m, v)


SHARDED = ("w_in", "w_q_b", "w_kv_b", "w_out", "w_gate", "w_up", "w_down")
COLUMN_SHARDED = ("w_in", "w_q_b", "w_kv_b", "w_gate", "w_up")
REPLICATED = ("attn_norm_g", "w_pool", "pool_scale", "q_a_norm_g", "kv_a_norm_g", "q_norm_g", "k_norm_g",
              "ffn_norm_g")
ADAM_TILE = 896


def _pack_rows(arrays):
    return jnp.concatenate([a.reshape(-1, LANES) for a in arrays], axis=0)


def _pack_flat(arrays, rows):
    flat = jnp.concatenate([a.reshape(-1) for a in arrays])
    return jnp.pad(flat, (0, rows * LANES - flat.shape[0])).reshape(rows, LANES)


def _unpack_rows(packed, shapes):
    out, at = [], 0
    for shp in shapes:
        n = 1
        for d in shp:
            n *= d
        out.append(packed[at:at + n // LANES].reshape(shp))
        at += n // LANES
    return out


def _unpack_flat(packed, shapes):
    flat, out, at = packed.reshape(-1), [], 0
    for shp in shapes:
        n = 1
        for d in shp:
            n *= d
        out.append(flat[at:at + n].reshape(shp))
        at += n
    return out


def _gathered_full(slots, name):
    if name in COLUMN_SHARDED:
        s = jnp.transpose(slots, (1, 2, 0, 3))
        return s.reshape(s.shape[0], s.shape[1], -1)
    s = jnp.transpose(slots, (1, 0, 2, 3))
    return s.reshape(s.shape[0], -1, s.shape[3])


def _to_slots(full, name):
    d, a, b = full.shape
    if name in COLUMN_SHARDED:
        return jnp.transpose(full.reshape(d, a, N_DEV, b // N_DEV), (2, 0, 1, 3))
    return jnp.transpose(full.reshape(d, N_DEV, a // N_DEV, b), (1, 0, 2, 3))


def _rope_lane_tables(length):
    inv = 1.0 / (ROPE_THETA ** (jnp.arange(0, QK_ROPE_DIM, 2, dtype=F32) / QK_ROPE_DIM))
    ang = jnp.arange(length, dtype=F32)[:, None] * inv[None, :]
    cos, sin = jnp.cos(ang), jnp.sin(ang)
    ones = jnp.ones((length, QK_NOPE_DIM), F32)
    zeros = jnp.zeros((length, QK_NOPE_DIM), F32)
    tail = HEAD_PAD - QK_HEAD_DIM
    cosf = jnp.concatenate([ones, cos, cos, ones[:, :tail]], axis=1)
    sins = jnp.concatenate([zeros, -sin, sin, zeros[:, :tail]], axis=1)
    return cosf, sins


def _pad_lanes(vec, at, width=HEAD_PAD):
    return jnp.pad(vec, (at, width - at - vec.shape[0])).reshape(1, width)


def kernel(x, meta_tokens, attn_norm_g, w_in, w_pool, pool_scale, q_a_norm_g, w_q_b, kv_a_norm_g, w_kv_b, q_norm_g, k_norm_g, w_out, ffn_norm_g, w_gate, w_up, w_down, loss_target, m_meta_tokens, m_attn_norm_g, m_w_in, m_w_pool, m_pool_scale, m_q_a_norm_g, m_w_q_b, m_kv_a_norm_g, m_w_kv_b, m_q_norm_g, m_k_norm_g, m_w_out, m_ffn_norm_g, m_w_gate, m_w_up, m_w_down, v_meta_tokens, v_attn_norm_g, v_w_in, v_w_pool, v_pool_scale, v_q_a_norm_g, v_w_q_b, v_kv_a_norm_g, v_w_kv_b, v_q_norm_g, v_k_norm_g, v_w_out, v_ffn_norm_g, v_w_gate, v_w_up, v_w_down):
    weights = dict(meta_tokens=meta_tokens, attn_norm_g=attn_norm_g, w_in=w_in, w_pool=w_pool, pool_scale=pool_scale,
                   q_a_norm_g=q_a_norm_g, w_q_b=w_q_b, kv_a_norm_g=kv_a_norm_g, w_kv_b=w_kv_b, q_norm_g=q_norm_g,
                   k_norm_g=k_norm_g, w_out=w_out, ffn_norm_g=ffn_norm_g, w_gate=w_gate, w_up=w_up, w_down=w_down)
    mom1 = dict(meta_tokens=m_meta_tokens, attn_norm_g=m_attn_norm_g, w_in=m_w_in, w_pool=m_w_pool,
                pool_scale=m_pool_scale, q_a_norm_g=m_q_a_norm_g, w_q_b=m_w_q_b, kv_a_norm_g=m_kv_a_norm_g,
                w_kv_b=m_w_kv_b, q_norm_g=m_q_norm_g, k_norm_g=m_k_norm_g, w_out=m_w_out, ffn_norm_g=m_ffn_norm_g,
                w_gate=m_w_gate, w_up=m_w_up, w_down=m_w_down)
    mom2 = dict(meta_tokens=v_meta_tokens, attn_norm_g=v_attn_norm_g, w_in=v_w_in, w_pool=v_w_pool,
                pool_scale=v_pool_scale, q_a_norm_g=v_q_a_norm_g, w_q_b=v_w_q_b, kv_a_norm_g=v_kv_a_norm_g,
                w_kv_b=v_w_kv_b, q_norm_g=v_q_norm_g, k_norm_g=v_k_norm_g, w_out=v_w_out, ffn_norm_g=v_ffn_norm_g,
                w_gate=v_w_gate, w_up=v_w_up, w_down=v_w_down)
    order = ("meta_tokens", "attn_norm_g", "w_in", "w_pool", "pool_scale", "q_a_norm_g", "w_q_b", "kv_a_norm_g",
             "w_kv_b", "q_norm_g", "k_norm_g", "w_out", "ffn_norm_g", "w_gate", "w_up", "w_down")
    depth = w_in.shape[0]
    seq = x.shape[1]
    length = N_META + seq
    lp = -(-length // ROW_TILE) * ROW_TILE

    shard_shapes = [weights[n].shape for n in SHARDED]
    packed_w = _pack_rows([weights[n] for n in SHARDED]).astype(BF16)
    g_w, g_meta = all_gather(packed_w, meta_tokens, "all_gather")
    full, at = {}, 0
    for n, shp in zip(SHARDED, shard_shapes):
        rows = shp[0] * shp[1] * shp[2] // LANES
        full[n] = _gathered_full(g_w[:, at:at + rows].reshape((N_DEV,) + shp), n)
        at += rows
    meta_full = jnp.transpose(g_meta, (1, 0, 2)).reshape(N_META, D_MODEL)

    s1, s2, s3 = POOL_WIDTH, POOL_WIDTH + Q_LORA_RANK, POOL_WIDTH + Q_LORA_RANK + KV_LORA_RANK
    zpad = lambda r, n: jnp.zeros((depth, r, n), BF16)
    w_in_p = jnp.concatenate([full["w_in"][:, :, :s3], zpad(D_MODEL, QK_NOPE_DIM), full["w_in"][:, :, s3:],
                              zpad(D_MODEL, HEAD_PAD - QK_HEAD_DIM)], axis=2)
    wq = full["w_q_b"].reshape(depth, Q_LORA_RANK, MLA_HEADS, QK_HEAD_DIM)
    w_q_p = jnp.pad(wq, ((0, 0), (0, 0), (0, 0), (0, HEAD_PAD - QK_HEAD_DIM))).reshape(depth, Q_LORA_RANK, -1)
    wkv = full["w_kv_b"].reshape(depth, KV_LORA_RANK, MLA_HEADS, QK_NOPE_DIM + V_HEAD_DIM)
    w_k_p = jnp.pad(wkv[..., :QK_NOPE_DIM], ((0, 0), (0, 0), (0, 0), (0, HEAD_PAD - QK_NOPE_DIM)))
    w_k_p = w_k_p.reshape(depth, KV_LORA_RANK, -1)
    w_v = wkv[..., QK_NOPE_DIM:].reshape(depth, KV_LORA_RANK, -1)
    w_kv_p = jnp.concatenate([w_k_p, w_v], axis=2)
    w_pool_b = w_pool.astype(BF16)
    kw = MLA_HEADS * HEAD_PAD
    vw = MLA_HEADS * V_HEAD_DIM

    cosf, sins = _rope_lane_tables(lp)
    row = lambda a, l: a[l].reshape(1, -1)

    h = jnp.concatenate([meta_full, x[0], jnp.zeros((lp - length, D_MODEL), F32)], axis=0)
    target = jnp.pad(loss_target[0], ((N_META, lp - length), (0, 0)))
    saved = []
    for l in range(depth):
        gq = _pad_lanes(q_norm_g[l], 0)
        gkn = _pad_lanes(k_norm_g[l, :QK_NOPE_DIM], 0)
        gkr = _pad_lanes(k_norm_g[l, QK_NOPE_DIM:], QK_NOPE_DIM)
        a, u, c_q, c_kv, kr = norm_mm(
            h, row(attn_norm_g, l), w_in_p,
            [(0, s1), (s1, Q_LORA_RANK), (s2, KV_LORA_RANK), (s3, HEAD_PAD)], [F32] * 4, "in_proj", layer=l)
        cat = pool_fwd(u, w_pool_b[l], row(pool_scale, l), "pool_fwd")
        qn, q = norm_mm(c_q, row(q_a_norm_g, l), w_q_p, [(0, kw)], [F32], "q_proj", layer=l)
        kvn, kn, v = norm_mm(c_kv, row(kv_a_norm_g, l), w_kv_p, [(0, kw), (kw, vw)], [F32, BF16], "kv_proj",
                             layer=l)
        qp, kp = qk_prep_fwd(q, kn, kr, cosf, sins, gq, gkn, gkr, "qk_prep_fwd")
        o, lse, cat = attn_fwd(qp, kp, v, cat, "attn_fwd")
        h_mid, g = mm_res(cat, full["w_out"], h, "out_proj", gamma=row(ffn_norm_g, l), layer=l)
        gate, up, act = ffn_up(g, full["w_gate"], full["w_up"], l, "ffn_up")
        h_next = mm_res(act, full["w_down"], h_mid, "ffn_down", layer=l)[0]
        saved.append(dict(h=h, a=a, u=u, c_q=c_q, c_kv=c_kv, kr=kr, qn=qn, q=q, kvn=kvn, kn=kn, v=v, qp=qp, kp=kp,
                          o=o, lse=lse, cat=cat, h_mid=h_mid, g=g, gate=gate, up=up, act=act,
                          gq=gq, gkn=gkn, gkr=gkr))
        h = h_next

    dh, sq = loss_head(h, target, seq, "loss_head")
    loss = lax.psum(0.5 / D_MODEL * jnp.sum(sq), ("x", "y", "c"))

    grads = {n: [None] * depth for n in order if n != "meta_tokens"}
    for l in reversed(range(depth)):
        s = saved[l]
        dgate, dup = ffn_bwd_act(dh, full["w_down"], l, s["gate"], s["up"], "ffn_bwd_act")
        grads["w_down"][l] = mm_tn(s["act"], dh, "dw_down")
        grads["w_gate"][l] = mm_tn(s["g"], dgate, "dw_gate")
        grads["w_up"][l] = mm_tn(s["g"], dup, "dw_up")
        dh_mid, dg_ffn = mm_nt_normbwd([(dgate, full["w_gate"], 0), (dup, full["w_up"], 0)], s["h_mid"],
                                       row(ffn_norm_g, l), dh, "ffn_bwd_in", layer=l)
        grads["ffn_norm_g"][l] = dg_ffn[0]
        grads["w_out"][l] = mm_tn(s["cat"], dh_mid, "dw_out")
        dy_pool, do = mm_nt(dh_mid, full["w_out"], [(0, POOL_WIDTH), (POOL_WIDTH, vw)], "out_proj_bwd", layer=l)
        du, dw_pool, dscale = pool_bwd(s["u"], dy_pool, w_pool_b[l], row(pool_scale, l), "pool_bwd")
        grads["w_pool"][l] = dw_pool
        grads["pool_scale"][l] = dscale[0]
        delta = attn_delta(do, s["o"], "attn_delta")
        dqp, dkp, dv = attn_bwd(s["qp"], s["kp"], s["v"], do, s["lse"], delta, "attn_bwd")
        dq, dkn, dkr, dgq, dgkn, dgkr = qk_prep_bwd(dqp, dkp, s["q"], s["kn"], s["kr"], cosf, sins,
                                                    s["gq"], s["gkn"], s["gkr"], "qk_prep_bwd")
        grads["q_norm_g"][l] = dgq[0, :QK_HEAD_DIM]
        grads["k_norm_g"][l] = jnp.concatenate([dgkn[0, :QK_NOPE_DIM], dgkr[0, QK_NOPE_DIM:QK_HEAD_DIM]])
        dwq = mm_tn(s["qn"], dq, "dw_q")
        grads["w_q_b"][l] = dwq.reshape(Q_LORA_RANK, MLA_HEADS, HEAD_PAD)[:, :, :QK_HEAD_DIM].reshape(Q_LORA_RANK, -1)
        dc_q, dg_qa = mm_nt_normbwd([(dq, w_q_p, 0)], s["c_q"], row(q_a_norm_g, l), None, "q_proj_bwd", layer=l)
        grads["q_a_norm_g"][l] = dg_qa[0]
        dwk = mm_tn(s["kvn"], dkn, "dw_k").reshape(KV_LORA_RANK, MLA_HEADS, HEAD_PAD)[:, :, :QK_NOPE_DIM]
        dwv = mm_tn(s["kvn"], dv, "dw_v").reshape(KV_LORA_RANK, MLA_HEADS, V_HEAD_DIM)
        grads["w_kv_b"][l] = jnp.concatenate([dwk, dwv], axis=2).reshape(KV_LORA_RANK, -1)
        dc_kv, dg_kva = mm_nt_normbwd([(dkn, w_kv_p, 0), (dv, w_kv_p, kw)], s["c_kv"], row(kv_a_norm_g, l), None,
                                      "kv_proj_bwd", layer=l)
        grads["kv_a_norm_g"][l] = dg_kva[0]
        dw_in_parts = [mm_tn(s["a"], du, "dw_in_pool"), mm_tn(s["a"], dc_q, "dw_in_q"),
                       mm_tn(s["a"], dc_kv, "dw_in_kv"),
                       mm_tn(s["a"], dkr, "dw_in_rope")[:, QK_NOPE_DIM:QK_HEAD_DIM]]
        grads["w_in"][l] = jnp.concatenate(dw_in_parts, axis=1)
        dh, dg_attn = mm_nt_normbwd(
            [(du, w_in_p, 0), (dc_q, w_in_p, s1), (dc_kv, w_in_p, s2), (dkr, w_in_p, s3)],
            s["h"], row(attn_norm_g, l), dh_mid, "in_proj_bwd", layer=l)
        grads["attn_norm_g"][l] = dg_attn[0]

    grad_x = dh[N_META:length][None]
    d_meta = dh[:N_META]
    grads = {n: jnp.stack(g, axis=0) for n, g in grads.items()}

    sharded_slots = jnp.concatenate(
        [_to_slots(grads[n], n).reshape(N_DEV, -1, LANES) for n in SHARDED], axis=1).astype(BF16)
    rep_shapes = [weights[n].shape for n in REPLICATED]
    rep_count = sum(int(jnp.size(weights[n])) for n in REPLICATED)
    rep_rows = -(-rep_count // (8 * LANES)) * 8
    rep_packed = _pack_flat([grads[n] for n in REPLICATED], rep_rows)
    meta_slots = jnp.transpose(d_meta.reshape(N_META, N_DEV, LANES), (1, 0, 2))
    small_slots = jnp.concatenate([meta_slots, jnp.broadcast_to(rep_packed[None], (N_DEV, rep_rows, LANES))], axis=1)
    from_sibling, got_small = exchange_pair(sharded_slots, small_slots, "grad_exchange_pair")
    core = lax.axis_index("c").astype(jnp.int32).reshape(1)
    chip_partial = pair_sum(sharded_slots, from_sibling, core, ADAM_TILE, "grad_pair_sum")
    got_big = exchange_chips(chip_partial, "grad_exchange_chips")

    pk = lambda src: _pack_rows([src[n] for n in SHARDED])
    big_out = adamw(got_big, pk(weights), pk(mom1), pk(mom2), ADAM_TILE, "adamw_sharded")
    ps = lambda src: jnp.concatenate(
        [src["meta_tokens"], _pack_flat([src[n] for n in REPLICATED], rep_rows)], axis=0)
    small_out = adamw(got_small, ps(weights), ps(mom1), ps(mom2), N_META + rep_rows, "adamw_small")

    results = []
    for big_arr, small_arr in zip(big_out, small_out):
        per = dict(zip(SHARDED, _unpack_rows(big_arr, shard_shapes)))
        per["meta_tokens"] = small_arr[:N_META]
        per.update(zip(REPLICATED, _unpack_flat(small_arr[N_META:], rep_shapes)))
        results.append([per[n] for n in order])
    return (loss, grad_x, *results[0], *results[1], *results[2], *results[3])
```

```python
import functools

import jax
import jax.numpy as jnp
from jax import lax
from jax.experimental import pallas as pl
from jax.experimental.pallas import tpu as pltpu

F32 = jnp.float32
BF16 = jnp.bfloat16

D_MODEL = 1024
N_META = 16
POOL_WIDTH = 512
POOL_WINDOWS = (2, 4, 8, 16)
POOL_GROUP_DIM = 128
POOL_HALO = 16
MLA_HEADS = 8
QK_NOPE_DIM = 64
QK_ROPE_DIM = 32
QK_HEAD_DIM = 96
V_HEAD_DIM = 64
HEAD_PAD = 128
Q_LORA_RANK = 384
KV_LORA_RANK = 256
ROPE_THETA = 10000.0
RMS_EPS = 1e-6
ATTN_SCALE = QK_HEAD_DIM ** -0.5

ADAM_LR = 0.001
ADAM_B1 = 0.9
ADAM_B2 = 0.999
ADAM_EPS = 1e-08
ADAM_WD = 0.01
ADAM_STEP = 10

N_DEV = 8
N_CHIPS = 4
LANES = 128
ROW_TILE = 384
LONG_TILE = 1056
VMEM_LIMIT = 56 * 1024 * 1024


def _params(*sem):
    return pltpu.CompilerParams(dimension_semantics=sem, vmem_limit_bytes=VMEM_LIMIT)


def _row_spec(tile, width):
    return pl.BlockSpec((tile, width), lambda i: (i, 0))


def _const_spec(shape):
    return pl.BlockSpec(shape, lambda i: tuple(0 for _ in shape))


def _layer_spec(w, layer):
    return pl.BlockSpec((None,) + w.shape[1:], lambda *_: (layer, 0, 0))


def _gathered_spec(w8, layer):
    return pl.BlockSpec((N_DEV, None) + w8.shape[2:], lambda *_: (0, layer, 0, 0))


def _nt(a, b):
    return lax.dot_general(a, b, (((1,), (1,)), ((), ())), preferred_element_type=F32)


def _tn(a, b):
    return lax.dot_general(a, b, (((0,), (0,)), ((), ())), preferred_element_type=F32)


def _nn(a, b):
    return jnp.dot(a, b, preferred_element_type=F32)


def _silu(g):
    return g * (1.0 / (1.0 + jnp.exp(-g)))


def _rms(xf):
    return lax.rsqrt(jnp.mean(xf * xf, axis=-1, keepdims=True) + RMS_EPS)


def norm_mm(x, gamma, w, layer, splits, dtypes, name):
    L, K = x.shape
    tm = ROW_TILE

    def body(x_ref, g_ref, w_ref, a_ref, *z_refs):
        xf = x_ref[...]
        a = ((xf * _rms(xf)) * g_ref[...]).astype(BF16)
        a_ref[...] = a
        z = _nn(a, w_ref[...])
        for (s, n), zr in zip(splits, z_refs):
            zr[...] = z[:, s:s + n].astype(zr.dtype)

    widths = [n for _, n in splits]
    return pl.pallas_call(
        body,
        grid=(L // tm,),
        in_specs=[_row_spec(tm, K), _const_spec((1, K)), _layer_spec(w, layer)],
        out_specs=[_row_spec(tm, K)] + [_row_spec(tm, n) for n in widths],
        out_shape=[jax.ShapeDtypeStruct((L, K), BF16)]
        + [jax.ShapeDtypeStruct((L, n), dt) for n, dt in zip(widths, dtypes)],
        compiler_params=_params("arbitrary"),
        name=name,
    )(x, gamma, w)


def norm_mm_heads(x, gamma, w8, layer, name):
    L, K = x.shape
    hw = w8.shape[-1]
    tm = ROW_TILE

    def body(x_ref, g_ref, w_ref, a_ref, z_ref):
        xf = x_ref[...]
        a = ((xf * _rms(xf)) * g_ref[...]).astype(BF16)
        a_ref[...] = a
        if hw < HEAD_PAD:
            z_ref[...] = jnp.zeros_like(z_ref)
        for j in range(MLA_HEADS):
            z_ref[:, j * HEAD_PAD:j * HEAD_PAD + hw] = _nn(a, w_ref[j])

    return pl.pallas_call(
        body,
        grid=(L // tm,),
        in_specs=[_row_spec(tm, K), _const_spec((1, K)), _gathered_spec(w8, layer)],
        out_specs=[_row_spec(tm, K), _row_spec(tm, MLA_HEADS * HEAD_PAD)],
        out_shape=[jax.ShapeDtypeStruct((L, K), BF16), jax.ShapeDtypeStruct((L, MLA_HEADS * HEAD_PAD), F32)],
        compiler_params=_params("arbitrary"),
        name=name,
    )(x, gamma, w8)


def ffn_up(g, w_gate8, w_up8, layer, name):
    L, K = g.shape
    fb = w_gate8.shape[-1]
    tm = ROW_TILE

    def body(a_ref, wg_ref, wu_ref, gate_ref, up_ref, act_ref):
        a = a_ref[...]
        gate = _nn(a, wg_ref[...])
        up = _nn(a, wu_ref[...])
        gate_ref[...] = gate
        up_ref[...] = up
        act_ref[...] = (_silu(gate) * up).astype(BF16)

    w_spec = pl.BlockSpec((None, None, K, fb), lambda j, i: (j, layer, 0, 0))
    tile = pl.BlockSpec((None, tm, fb), lambda j, i: (j, i, 0))
    return pl.pallas_call(
        body,
        grid=(N_DEV, L // tm),
        in_specs=[pl.BlockSpec((tm, K), lambda j, i: (i, 0)), w_spec, w_spec],
        out_specs=[tile, tile, tile],
        out_shape=[
            jax.ShapeDtypeStruct((N_DEV, L, fb), F32),
            jax.ShapeDtypeStruct((N_DEV, L, fb), F32),
            jax.ShapeDtypeStruct((N_DEV, L, fb), BF16),
        ],
        compiler_params=_params("arbitrary", "arbitrary"),
        name=name,
    )(g, w_gate8, w_up8)


def mm_res(a, w8, res, layer, name, gamma=None):
    blocked = a.ndim == 3
    L = a.shape[-2]
    kb, N = w8.shape[-2:]
    tm = ROW_TILE
    normed = gamma is not None

    def body(a_ref, w_ref, r_ref, *rest):
        if blocked:
            acc = _nn(a_ref[0], w_ref[0])
            for j in range(1, N_DEV):
                acc = acc + _nn(a_ref[j], w_ref[j])
        else:
            acc = _nn(a_ref[...], w_ref[...].reshape(N_DEV * kb, N))
        out = r_ref[...] + acc
        if normed:
            g_ref, o_ref, n_ref = rest
            n_ref[...] = ((out * _rms(out)) * g_ref[...]).astype(BF16)
        else:
            (o_ref,) = rest
        o_ref[...] = out

    a_spec = pl.BlockSpec((N_DEV, tm, kb), lambda i: (0, i, 0)) if blocked else _row_spec(tm, N_DEV * kb)
    in_specs = [a_spec, _gathered_spec(w8, layer), _row_spec(tm, N)]
    out_specs = [_row_spec(tm, N)]
    out_shape = [jax.ShapeDtypeStruct((L, N), F32)]
    args = [a, w8, res]
    if normed:
        in_specs.append(_const_spec((1, N)))
        out_specs.append(_row_spec(tm, N))
        out_shape.append(jax.ShapeDtypeStruct((L, N), BF16))
        args.append(gamma)
    return pl.pallas_call(
        body,
        grid=(L // tm,),
        in_specs=in_specs,
        out_specs=out_specs,
        out_shape=out_shape,
        compiler_params=_params("arbitrary"),
        name=name,
    )(*args)


def mm_nt(dz, w8, layer, splits, name):
    L, N = dz.shape
    kb = w8.shape[-2]
    tm = ROW_TILE

    def body(dz_ref, w_ref, *o_refs):
        d = _nt(dz_ref[...].astype(BF16), w_ref[...].reshape(N_DEV * kb, N))
        for (s, n), o in zip(splits, o_refs):
            o[...] = d[:, s:s + n]

    return pl.pallas_call(
        body,
        grid=(L // tm,),
        in_specs=[_row_spec(tm, N), _gathered_spec(w8, layer)],
        out_specs=[_row_spec(tm, n) for _, n in splits],
        out_shape=[jax.ShapeDtypeStruct((L, n), F32) for _, n in splits],
        compiler_params=_params("arbitrary"),
        name=name,
    )(dz, w8)


def mm_nt_normbwd(terms, x, gamma, dres, layer, name):
    L, K = x.shape
    tm = ROW_TILE
    n_terms = len(terms)
    has_res = dres is not None
    weights = []
    for t in terms:
        if not any(t[2] is u for u in weights):
            weights.append(t[2])
    which = [[t[2] is u for u in weights].index(True) for t in terms]
    n_in = n_terms + len(weights)

    def body(*refs):
        dz_refs = refs[:n_terms]
        w_refs = [refs[n_terms + n] for n in which]
        x_ref, g_ref = refs[n_in], refs[n_in + 1]
        pos = n_in + 2
        r_ref = refs[pos] if has_res else None
        dx_ref, dg_ref = refs[pos + has_res], refs[pos + has_res + 1]
        da = None
        for t, dz_ref, w_ref in zip(terms, dz_refs, w_refs):
            if t[0] == "cols":
                n, at = t[1].shape[1], t[3]
                parts = [_nt(dz_ref[...].astype(BF16), w_ref[:, at:at + n])]
            elif t[0] == "heads":
                hw = t[2].shape[-1]
                parts = [_nt(dz_ref[:, j * HEAD_PAD:j * HEAD_PAD + hw].astype(BF16), w_ref[j])
                         for j in range(MLA_HEADS)]
            else:
                parts = [_nt(dz_ref[j].astype(BF16), w_ref[j]) for j in range(N_DEV)]
            for p in parts:
                da = p if da is None else da + p
        xf = x_ref[...]
        r = _rms(xf)
        xh = xf * r

        @pl.when(pl.program_id(0) == 0)
        def _():
            dg_ref[...] = jnp.zeros_like(dg_ref)

        dg_ref[...] += jnp.sum(da * xh, axis=0, keepdims=True)
        dxh = da * g_ref[...]
        dx = r * (dxh - xh * jnp.mean(dxh * xh, axis=-1, keepdims=True))
        if has_res:
            dx = dx + r_ref[...]
        dx_ref[...] = dx

    in_specs = []
    for t in terms:
        dz = t[1]
        if t[0] == "blocks":
            in_specs.append(pl.BlockSpec((N_DEV, tm, dz.shape[2]), lambda i: (0, i, 0)))
        else:
            in_specs.append(_row_spec(tm, dz.shape[1]))
    for w in weights:
        in_specs.append(_layer_spec(w, layer) if w.ndim == 3 else _gathered_spec(w, layer))
    in_specs += [_row_spec(tm, K), _const_spec((1, K))]
    args = [t[1] for t in terms] + weights + [x, gamma]
    if has_res:
        in_specs.append(_row_spec(tm, K))
        args.append(dres)
    return pl.pallas_call(
        body,
        grid=(L // tm,),
        in_specs=in_specs,
        out_specs=[_row_spec(tm, K), _const_spec((1, K))],
        out_shape=[jax.ShapeDtypeStruct((L, K), F32), jax.ShapeDtypeStruct((1, K), F32)],
        compiler_params=_params("arbitrary"),
        name=name,
    )(*args)


def mm_tn(a, b, name, out_dtype=F32, b_head=None):
    a_blocked, b_blocked = a.ndim == 3, b.ndim == 3
    L = a.shape[-2]
    blocks = N_DEV if (a_blocked or b_blocked or b_head) else 1
    K = a.shape[-1]
    N = b_head or b.shape[-1]
    tl = LONG_TILE if L % LONG_TILE == 0 else ROW_TILE
    n_l = L // tl

    def body(a_ref, b_ref, o_ref, acc):
        l = pl.program_id(1)

        @pl.when(l == 0)
        def _():
            acc[...] = jnp.zeros_like(acc)

        bt = b_ref[...] if not b_head else b_ref[:, :N]
        acc[...] += _tn(a_ref[...].astype(BF16), bt.astype(BF16))

        @pl.when(l == n_l - 1)
        def _():
            o_ref[...] = acc[...].astype(o_ref.dtype)

    a_spec = (pl.BlockSpec((None, tl, K), lambda j, l: (j, l, 0)) if a_blocked
              else pl.BlockSpec((tl, K), lambda j, l: (l, 0)))
    if b_blocked:
        b_spec = pl.BlockSpec((None, tl, N), lambda j, l: (j, l, 0))
    elif b_head:
        b_spec = pl.BlockSpec((tl, HEAD_PAD), lambda j, l: (l, j))
    else:
        b_spec = pl.BlockSpec((tl, N), lambda j, l: (l, 0))
    if blocks > 1:
        out_spec = pl.BlockSpec((None, K, N), lambda j, l: (j, 0, 0))
        out_shape = jax.ShapeDtypeStruct((blocks, K, N), out_dtype)
    else:
        out_spec = pl.BlockSpec((K, N), lambda j, l: (0, 0))
        out_shape = jax.ShapeDtypeStruct((K, N), out_dtype)
    return pl.pallas_call(
        body,
        grid=(blocks, n_l),
        in_specs=[a_spec, b_spec],
        out_specs=out_spec,
        out_shape=out_shape,
        scratch_shapes=[pltpu.VMEM((K, N), F32)],
        compiler_params=_params("arbitrary", "arbitrary"),
        name=name,
    )(a, b)


def ffn_bwd_act(dh, w_down8, layer, gate, up, name):
    L, K = dh.shape
    fb = w_down8.shape[-2]
    tm = ROW_TILE

    def body(dh_ref, w_ref, gate_ref, up_ref, dgate_ref, dup_ref):
        dact = _nt(dh_ref[...].astype(BF16), w_ref[...])
        g = gate_ref[...]
        sig = 1.0 / (1.0 + jnp.exp(-g))
        dup_ref[...] = (dact * (g * sig)).astype(BF16)
        dgate_ref[...] = (dact * up_ref[...] * (sig * (1.0 + g * (1.0 - sig)))).astype(BF16)

    tile = pl.BlockSpec((None, tm, fb), lambda j, i: (j, i, 0))
    return pl.pallas_call(
        body,
        grid=(N_DEV, L // tm),
        in_specs=[
            pl.BlockSpec((tm, K), lambda j, i: (i, 0)),
            pl.BlockSpec((None, None, fb, K), lambda j, i: (j, layer, 0, 0)),
            tile,
            tile,
        ],
        out_specs=[tile, tile],
        out_shape=[jax.ShapeDtypeStruct((N_DEV, L, fb), BF16), jax.ShapeDtypeStruct((N_DEV, L, fb), BF16)],
        compiler_params=_params("arbitrary", "arbitrary"),
        name=name,
    )(dh, w_down8, gate, up)


def _pool_residual(scr, lo, tm, g, w, t):
    cols = slice(g * POOL_GROUP_DIM, (g + 1) * POOL_GROUP_DIM)
    cur = scr[lo:lo + tm, cols]
    s = cur
    for k in range(1, w):
        s = s + scr[lo - k:lo - k + tm, cols]
    cnt = jnp.minimum(t + 1, w).astype(F32)
    return s / cnt - cur


def pool_fwd(u, w_pool, scale, name):
    L, C = u.shape
    tm, halo = ROW_TILE, POOL_HALO

    def body(u_ref, halo_ref, w_ref, s_ref, y_ref, scr):
        i = pl.program_id(0)
        scr[0:halo, :] = jnp.where(i > 0, halo_ref[...], 0.0)
        scr[halo:halo + tm, :] = u_ref[...]
        t = i * tm + lax.broadcasted_iota(jnp.int32, (tm, POOL_GROUP_DIM), 0)
        for g, w in enumerate(POOL_WINDOWS):
            cols = slice(g * POOL_GROUP_DIM, (g + 1) * POOL_GROUP_DIM)
            p = _pool_residual(scr, halo, tm, g, w, t)
            y = _nn(p.astype(BF16), w_ref[g]) * s_ref[:, cols]
            y_ref[:, cols] = y.astype(y_ref.dtype)

    return pl.pallas_call(
        body,
        grid=(L // tm,),
        in_specs=[
            _row_spec(tm, C),
            pl.BlockSpec((halo, C), lambda i: (jnp.maximum(i * (tm // halo) - 1, 0), 0)),
            _const_spec(w_pool.shape),
            _const_spec((1, C)),
        ],
        out_specs=_row_spec(tm, C),
        out_shape=jax.ShapeDtypeStruct((L, 2 * C), BF16),
        scratch_shapes=[pltpu.VMEM((tm + halo, C), F32)],
        compiler_params=_params("arbitrary"),
        name=name,
    )(u, u, w_pool, scale)


def pool_bwd(u, dy, w_pool, scale, name):
    L, C = u.shape
    tm, halo = ROW_TILE, POOL_HALO
    n_tiles = L // tm
    last_halo = L // halo - 1

    def body(u_ref, uh_ref, dy_ref, dyh_ref, w_ref, s_ref, du_ref, dw_ref, ds_ref, scr_u, scr_q):
        i = pl.program_id(0)

        @pl.when(i == 0)
        def _():
            dw_ref[...] = jnp.zeros_like(dw_ref)
            ds_ref[...] = jnp.zeros_like(ds_ref)

        scr_u[0:halo, :] = jnp.where(i > 0, uh_ref[...], 0.0)
        scr_u[halo:halo + tm, :] = u_ref[...]
        t = i * tm + lax.broadcasted_iota(jnp.int32, (tm, POOL_GROUP_DIM), 0)
        th = (i + 1) * tm + lax.broadcasted_iota(jnp.int32, (halo, POOL_GROUP_DIM), 0)
        for g, w in enumerate(POOL_WINDOWS):
            cols = slice(g * POOL_GROUP_DIM, (g + 1) * POOL_GROUP_DIM)
            p = _pool_residual(scr_u, halo, tm, g, w, t).astype(BF16)
            wg = w_ref[g]
            sc = s_ref[:, cols]
            dy = dy_ref[:, cols]
            ds_ref[:, cols] += jnp.sum(dy * _nn(p, wg), axis=0, keepdims=True)
            dys = (dy * sc).astype(BF16)
            dw_ref[g] += _tn(p, dys)
            dp = _nt(dys, wg)
            dyh = jnp.where(i < n_tiles - 1, dyh_ref[:, cols], 0.0)
            dph = _nt((dyh * sc).astype(BF16), wg)
            scr_q[0:tm, cols] = dp / jnp.minimum(t + 1, w).astype(F32)
            scr_q[tm:tm + halo, cols] = dph / jnp.minimum(th + 1, w).astype(F32)
            acc = scr_q[0:tm, cols]
            for k in range(1, w):
                acc = acc + scr_q[k:k + tm, cols]
            du_ref[:, cols] = acc - dp

    return pl.pallas_call(
        body,
        grid=(n_tiles,),
        in_specs=[
            _row_spec(tm, C),
            pl.BlockSpec((halo, C), lambda i: (jnp.maximum(i * (tm // halo) - 1, 0), 0)),
            _row_spec(tm, C),
            pl.BlockSpec((halo, C), lambda i: (jnp.minimum((i + 1) * (tm // halo), last_halo), 0)),
            _const_spec(w_pool.shape),
            _const_spec((1, C)),
        ],
        out_specs=[_row_spec(tm, C), _const_spec(w_pool.shape), _const_spec((1, C))],
        out_shape=[
            jax.ShapeDtypeStruct((L, C), F32),
            jax.ShapeDtypeStruct(w_pool.shape, F32),
            jax.ShapeDtypeStruct((1, C), F32),
        ],
        scratch_shapes=[pltpu.VMEM((tm + halo, C), F32), pltpu.VMEM((tm + halo, C), F32)],
        compiler_params=_params("arbitrary"),
        name=name,
    )(u, u, dy, dy, w_pool, scale)


def _head_masks(rows):
    lane = lax.broadcasted_iota(jnp.int32, (rows, HEAD_PAD), 1)
    return lane, lane < QK_NOPE_DIM, (lane >= QK_NOPE_DIM) & (lane < QK_HEAD_DIM)


def _rope_swap(x, lane):
    half = QK_ROPE_DIM // 2
    swapped = jnp.where(lane < QK_NOPE_DIM + half, pltpu.roll(x, HEAD_PAD - half, 1), pltpu.roll(x, half, 1))
    return jnp.where((lane >= QK_NOPE_DIM) & (lane < QK_HEAD_DIM), swapped, 0.0)


def _seg_mean(v, m_n, m_r):
    mn = jnp.sum(jnp.where(m_n, v, 0.0), axis=1, keepdims=True) * (1.0 / QK_NOPE_DIM)
    mr = jnp.sum(jnp.where(m_r, v, 0.0), axis=1, keepdims=True) * (1.0 / QK_ROPE_DIM)
    return jnp.where(m_n, mn, mr)


def qk_prep_fwd(q, kv, kr, cosf, sins, gq, gkn, gkr, name):
    L = q.shape[0]
    tm = ROW_TILE
    W = MLA_HEADS * HEAD_PAD

    def body(q_ref, kv_ref, kr_ref, c_ref, s_ref, gq_ref, gkn_ref, gkr_ref, qo_ref, ko_ref, vo_ref):
        lane, m_n, m_r = _head_masks(tm)
        cosf_, sins_ = c_ref[...], s_ref[...]
        kr_ = kr_ref[...]
        rk = lax.rsqrt(_seg_mean(kr_ * kr_, m_n, m_r) + RMS_EPS)
        krn = kr_ * rk * gkr_ref[...]
        krf = krn * cosf_ + _rope_swap(krn, lane) * sins_
        for h in range(MLA_HEADS):
            cols = slice(h * HEAD_PAD, (h + 1) * HEAD_PAD)
            qh = q_ref[:, cols]
            qn = qh * lax.rsqrt(_seg_mean(qh * qh, m_n, m_r) + RMS_EPS) * gq_ref[...]
            qo_ref[:, cols] = (qn * cosf_ + _rope_swap(qn, lane) * sins_).astype(BF16)
            kh = jnp.where(m_n, kv_ref[:, cols], 0.0)
            rn = lax.rsqrt(jnp.sum(kh * kh, axis=1, keepdims=True) * (1.0 / QK_NOPE_DIM) + RMS_EPS)
            ko_ref[:, cols] = (kh * rn * gkn_ref[...] + krf).astype(BF16)
        for p in range(MLA_HEADS // 2):
            even = kv_ref[:, 2 * p * HEAD_PAD:(2 * p + 1) * HEAD_PAD]
            odd = kv_ref[:, (2 * p + 1) * HEAD_PAD:(2 * p + 2) * HEAD_PAD]
            pair = jnp.where(m_n, pltpu.roll(even, V_HEAD_DIM, 1), odd)
            vo_ref[:, p * LANES:(p + 1) * LANES] = pair.astype(BF16)

    vec = _const_spec((1, HEAD_PAD))
    return pl.pallas_call(
        body,
        grid=(L // tm,),
        in_specs=[_row_spec(tm, W), _row_spec(tm, W), _row_spec(tm, HEAD_PAD), _row_spec(tm, HEAD_PAD),
                  _row_spec(tm, HEAD_PAD), vec, vec, vec],
        out_specs=[_row_spec(tm, W), _row_spec(tm, W), _row_spec(tm, W // 2)],
        out_shape=[jax.ShapeDtypeStruct((L, W), BF16), jax.ShapeDtypeStruct((L, W), BF16),
                   jax.ShapeDtypeStruct((L, W // 2), BF16)],
        compiler_params=_params("arbitrary"),
        name=name,
    )(q, kv, kr, cosf, sins, gq, gkn, gkr)


def qk_prep_bwd(dqo, dko, dv, q, kv, kr, cosf, sins, gq, gkn, gkr, name):
    L = q.shape[0]
    tm = ROW_TILE
    W = MLA_HEADS * HEAD_PAD

    def body(dqo_ref, dko_ref, dv_ref, q_ref, kv_ref, kr_ref, c_ref, s_ref, gq_ref, gkn_ref, gkr_ref,
             dq_ref, dkv_ref, dkr_ref, dgq_ref, dgkn_ref, dgkr_ref):
        @pl.when(pl.program_id(0) == 0)
        def _():
            dgq_ref[...] = jnp.zeros_like(dgq_ref)
            dgkn_ref[...] = jnp.zeros_like(dgkn_ref)
            dgkr_ref[...] = jnp.zeros_like(dgkr_ref)

        lane, m_n, m_r = _head_masks(tm)
        cosf_, sins_ = c_ref[...], s_ref[...]
        dgq = jnp.zeros((1, HEAD_PAD), F32)
        dgkn = jnp.zeros((1, HEAD_PAD), F32)
        dkrf = jnp.zeros((tm, HEAD_PAD), F32)
        for h in range(MLA_HEADS):
            cols = slice(h * HEAD_PAD, (h + 1) * HEAD_PAD)
            dy = dqo_ref[:, cols]
            dqn = dy * cosf_ + _rope_swap(dy * sins_, lane)
            qh = q_ref[:, cols]
            rinv = lax.rsqrt(_seg_mean(qh * qh, m_n, m_r) + RMS_EPS)
            xh = qh * rinv
            dgq = dgq + jnp.sum(dqn * xh, axis=0, keepdims=True)
            dxh = dqn * gq_ref[...]
            dq_ref[:, cols] = rinv * (dxh - xh * _seg_mean(dxh * xh, m_n, m_r))
            dk = dko_ref[:, cols]
            dkrf = dkrf + dk
            kh = jnp.where(m_n, kv_ref[:, cols], 0.0)
            rn = lax.rsqrt(jnp.sum(kh * kh, axis=1, keepdims=True) * (1.0 / QK_NOPE_DIM) + RMS_EPS)
            xk = kh * rn
            dgkn = dgkn + jnp.sum(dk * xk, axis=0, keepdims=True)
            dxk = dk * gkn_ref[...]
            dkn = rn * (dxk - xk * (jnp.sum(dxk * xk, axis=1, keepdims=True) * (1.0 / QK_NOPE_DIM)))
            dvp = dv_ref[:, (h // 2) * LANES:(h // 2 + 1) * LANES]
            dvh = pltpu.roll(dvp, V_HEAD_DIM, 1) if h % 2 == 0 else dvp
            dkv_ref[:, cols] = jnp.where(m_n, dkn, dvh)
        kr_ = kr_ref[...]
        rk = lax.rsqrt(_seg_mean(kr_ * kr_, m_n, m_r) + RMS_EPS)
        xr = kr_ * rk
        dkrn = dkrf * cosf_ + _rope_swap(dkrf * sins_, lane)
        dgkr_ref[...] += jnp.sum(dkrn * xr, axis=0, keepdims=True)
        dxr = dkrn * gkr_ref[...]
        dkr_ref[...] = rk * (dxr - xr * _seg_mean(dxr * xr, m_n, m_r))
        dgq_ref[...] += dgq
        dgkn_ref[...] += dgkn

    vec = _const_spec((1, HEAD_PAD))
    return pl.pallas_call(
        body,
        grid=(L // tm,),
        in_specs=[_row_spec(tm, W), _row_spec(tm, W), _row_spec(tm, W // 2), _row_spec(tm, W), _row_spec(tm, W),
                  _row_spec(tm, HEAD_PAD), _row_spec(tm, HEAD_PAD), _row_spec(tm, HEAD_PAD), vec, vec, vec],
        out_specs=[_row_spec(tm, W), _row_spec(tm, W), _row_spec(tm, HEAD_PAD), vec, vec, vec],
        out_shape=[jax.ShapeDtypeStruct((L, W), F32), jax.ShapeDtypeStruct((L, W), F32),
                   jax.ShapeDtypeStruct((L, HEAD_PAD), F32)] + [jax.ShapeDtypeStruct((1, HEAD_PAD), F32)] * 3,
        compiler_params=_params("arbitrary"),
        name=name,
    )(dqo, dko, dv, q, kv, kr, cosf, sins, gq, gkn, gkr)


def attn_fwd(qp, kp, v, cat, name):
    L = qp.shape[0]
    tq = ROW_TILE
    pair_w = 2 * HEAD_PAD
    n_pairs = MLA_HEADS // 2

    def body(q_ref, k_ref, v_ref, cat_in, o_ref, lse_ref, cat_ref):
        del cat_in
        i = pl.program_id(1)
        left = lax.broadcasted_iota(jnp.int32, (tq, LANES), 1) < V_HEAD_DIM
        row = lax.broadcasted_iota(jnp.int32, (tq, tq), 0)
        col = lax.broadcasted_iota(jnp.int32, (tq, tq), 1)

        def step(j, carry, masked):
            rows = pl.ds(pl.multiple_of(j * tq, tq), tq)
            vv = v_ref[rows, :]
            out = []
            for hh in range(2):
                cols = slice(hh * HEAD_PAD, (hh + 1) * HEAD_PAD)
                m, l, acc = carry[hh]
                s = _nt(q_ref[:, cols], k_ref[rows, cols]) * ATTN_SCALE
                if masked:
                    s = jnp.where(col <= row, s, -jnp.inf)
                m_new = jnp.maximum(m, jnp.max(s, axis=1, keepdims=True))
                alpha = jnp.exp(m - m_new)
                p = jnp.exp(s - m_new)
                l = alpha * l + jnp.sum(p, axis=1, keepdims=True)
                acc = alpha * acc + _nn(p.astype(BF16), vv)
                out.append((m_new, l, acc))
            return tuple(out)

        one = (jnp.full((tq, 1), -jnp.inf, F32), jnp.zeros((tq, 1), F32), jnp.zeros((tq, LANES), F32))
        carry = lax.fori_loop(0, i, functools.partial(step, masked=False), (one, one))
        (m0, l0, a0), (m1, l1, a1) = step(i, carry, True)
        o = jnp.where(left, a0 / l0, a1 / l1)
        o_ref[...] = o
        cat_ref[...] = o.astype(BF16)
        lse_ref[...] = jnp.where(left, m0 + jnp.log(l0), m1 + jnp.log(l1))

    return pl.pallas_call(
        body,
        grid=(n_pairs, L // tq),
        in_specs=[
            pl.BlockSpec((tq, pair_w), lambda p, i: (i, p)),
            pl.BlockSpec((L, pair_w), lambda p, i: (0, p)),
            pl.BlockSpec((L, LANES), lambda p, i: (0, p)),
            pl.BlockSpec(memory_space=pl.ANY),
        ],
        out_specs=[
            pl.BlockSpec((tq, LANES), lambda p, i: (i, p)),
            pl.BlockSpec((tq, LANES), lambda p, i: (i, p)),
            pl.BlockSpec((tq, LANES), lambda p, i: (i, n_pairs + p)),
        ],
        out_shape=[jax.ShapeDtypeStruct((L, n_pairs * LANES), F32), jax.ShapeDtypeStruct((L, n_pairs * LANES), F32),
                   jax.ShapeDtypeStruct(cat.shape, cat.dtype)],
        input_output_aliases={3: 2},
        compiler_params=_params("arbitrary", "arbitrary"),
        name=name,
    )(qp, kp, v, cat)


def attn_delta(do, o, name):
    L, C = o.shape
    tm = ROW_TILE

    def body(do_ref, o_ref, d_ref):
        left = lax.broadcasted_iota(jnp.int32, (tm, LANES), 1) < V_HEAD_DIM
        for p in range(C // LANES):
            cols = slice(p * LANES, (p + 1) * LANES)
            prod = do_ref[:, cols] * o_ref[:, cols]
            d0 = jnp.sum(jnp.where(left, prod, 0.0), axis=1, keepdims=True)
            d1 = jnp.sum(jnp.where(left, 0.0, prod), axis=1, keepdims=True)
            d_ref[:, cols] = jnp.where(left, d0, d1)

    return pl.pallas_call(
        body,
        grid=(L // tm,),
        in_specs=[_row_spec(tm, C), _row_spec(tm, C)],
        out_specs=_row_spec(tm, C),
        out_shape=jax.ShapeDtypeStruct((L, C), F32),
        compiler_params=_params("arbitrary"),
        name=name,
    )(do, o)


def attn_bwd(qp, kp, v, do, lse, delta, name):
    L = qp.shape[0]
    tq = ROW_TILE
    n_q = L // tq
    pair_w = 2 * HEAD_PAD

    def body(q_ref, k_ref, v_ref, do_ref, lse_ref, dl_ref, dq_ref, dk_ref, dv_ref):
        j = pl.program_id(1)

        @pl.when(j == 0)
        def _():
            dq_ref[...] = jnp.zeros_like(dq_ref)

        dk_ref[...] = jnp.zeros_like(dk_ref)
        dv_ref[...] = jnp.zeros_like(dv_ref)
        left = lax.broadcasted_iota(jnp.int32, (tq, LANES), 1) < V_HEAD_DIM
        row = lax.broadcasted_iota(jnp.int32, (tq, tq), 0)
        col = lax.broadcasted_iota(jnp.int32, (tq, tq), 1)

        def step(i, carry, masked):
            rows = pl.ds(pl.multiple_of(i * tq, tq), tq)
            do = do_ref[rows, :]
            vv = v_ref[...]
            dv = None
            for hh in range(2):
                cols = slice(hh * HEAD_PAD, (hh + 1) * HEAD_PAD)
                stat = slice(hh * V_HEAD_DIM, hh * V_HEAD_DIM + 1)
                q = q_ref[rows, cols]
                k = k_ref[:, cols]
                dom = jnp.where(left if hh == 0 else jnp.logical_not(left), do, 0.0).astype(BF16)
                s = _nt(q, k) * ATTN_SCALE
                p = jnp.exp(s - lse_ref[rows, stat])
                if masked:
                    p = jnp.where(col <= row, p, 0.0)
                dp = _nt(dom, vv)
                ds = (p * (dp - dl_ref[rows, stat]) * ATTN_SCALE).astype(BF16)
                dq_ref[rows, cols] += _nn(ds, k)
                dk_ref[:, cols] += _tn(ds, q)
                t = _tn(p.astype(BF16), dom)
                dv = t if dv is None else dv + t
            dv_ref[...] += dv
            return carry

        step(j, 0, True)
        lax.fori_loop(j + 1, n_q, functools.partial(step, masked=False), 0)

    n_pairs = MLA_HEADS // 2
    return pl.pallas_call(
        body,
        grid=(n_pairs, n_q),
        in_specs=[
            pl.BlockSpec((L, pair_w), lambda p, j: (0, p)),
            pl.BlockSpec((tq, pair_w), lambda p, j: (j, p)),
            pl.BlockSpec((tq, LANES), lambda p, j: (j, p)),
            pl.BlockSpec((L, LANES), lambda p, j: (0, p)),
            pl.BlockSpec((L, LANES), lambda p, j: (0, p)),
            pl.BlockSpec((L, LANES), lambda p, j: (0, p)),
        ],
        out_specs=[
            pl.BlockSpec((L, pair_w), lambda p, j: (0, p)),
            pl.BlockSpec((tq, pair_w), lambda p, j: (j, p)),
            pl.BlockSpec((tq, LANES), lambda p, j: (j, p)),
        ],
        out_shape=[
            jax.ShapeDtypeStruct((L, n_pairs * pair_w), F32),
            jax.ShapeDtypeStruct((L, n_pairs * pair_w), F32),
            jax.ShapeDtypeStruct((L, n_pairs * LANES), F32),
        ],
        compiler_params=_params("arbitrary", "arbitrary"),
        name=name,
    )(qp, kp, v, do, lse, delta)


def loss_head(h, target, n_real, name):
    L, D = h.shape
    tm = ROW_TILE

    def body(h_ref, t_ref, dh_ref, sq_ref):
        i = pl.program_id(0)

        @pl.when(i == 0)
        def _():
            sq_ref[...] = jnp.zeros_like(sq_ref)

        t = i * tm + lax.broadcasted_iota(jnp.int32, (tm, D), 0)
        real = (t >= N_META) & (t < N_META + n_real)
        diff = jnp.where(real, h_ref[...] - t_ref[...], 0.0)
        dh_ref[...] = diff * (1.0 / D)
        sq_ref[...] += jnp.sum(diff * diff, axis=0, keepdims=True)

    return pl.pallas_call(
        body,
        grid=(L // tm,),
        in_specs=[_row_spec(tm, D), _row_spec(tm, D)],
        out_specs=[_row_spec(tm, D), _const_spec((1, D))],
        out_shape=[jax.ShapeDtypeStruct((L, D), F32), jax.ShapeDtypeStruct((1, D), F32)],
        compiler_params=_params("arbitrary"),
        name=name,
    )(h, target)


def _mesh_position():
    x, y, c = lax.axis_index("x"), lax.axis_index("y"), lax.axis_index("c")
    return x, y, c, 4 * x + 2 * y + c


def _flip(x, y, c, k):
    px = 1 - x if k & 4 else x
    py = 1 - y if k & 2 else y
    pc = 1 - c if k & 1 else c
    return (px, py, pc), 4 * px + 2 * py + pc


def _other_chips(x, y):
    return [(1 - x, y), (x, 1 - y), (1 - x, 1 - y)]


def _dev_index(px, py, pc):
    return 4 * px + 2 * py + pc


_ANY = pl.BlockSpec(memory_space=pl.ANY)


def cast_bf16(arrays, name):
    n = len(arrays)

    def body(*refs):
        for src, dst in zip(refs[:n], refs[n:]):
            dst[...] = src[...].astype(BF16)

    return pl.pallas_call(
        body,
        out_shape=[jax.ShapeDtypeStruct(a.shape, BF16) for a in arrays],
        compiler_params=pltpu.CompilerParams(vmem_limit_bytes=VMEM_LIMIT),
        name=name,
    )(*arrays)


def all_gather(payloads, name):
    n = len(payloads)
    n_copies = N_DEV - 1

    def body(*refs):
        srcs, dsts = refs[:n], refs[n:2 * n]
        send_sems, recv_sems, local_sems = refs[2 * n:]
        x, y, c, me = _mesh_position()
        sibling = (x, y, 1 - c)
        chips = _other_chips(x, y)
        own = [pltpu.make_async_copy(srcs[b], dsts[b].at[me], local_sems.at[b]) for b in range(n)]
        for cp in own:
            cp.start()

        def copy(k, b, block, to, from_src=False):
            dst = dsts[b].at[block]
            return pltpu.make_async_remote_copy(
                src_ref=srcs[b] if from_src else dst, dst_ref=dst, send_sem=send_sems.at[k * n + b],
                recv_sem=recv_sems.at[k * n + b], device_id=to, device_id_type=pl.DeviceIdType.MESH)

        first = []
        for b in range(n):
            first.append(copy(0, b, me, sibling, from_src=True))
            first += [copy(1 + q, b, me, (*chip, c), from_src=True) for q, chip in enumerate(chips)]
        for cp in first:
            cp.start()
        passed = []
        for q, chip in enumerate(chips):
            block = _dev_index(*chip, c)
            for b in range(n):
                copy(1 + q, b, block, sibling).wait_recv()
                passed.append(copy(4 + q, b, block, sibling))
                passed[-1].start()
        for b in range(n):
            copy(0, b, _dev_index(x, y, 1 - c), sibling).wait_recv()
            for q, chip in enumerate(chips):
                copy(4 + q, b, _dev_index(*chip, 1 - c), sibling).wait_recv()
        for cp in first + passed:
            cp.wait_send()
        for cp in own:
            cp.wait()

    return pl.pallas_call(
        body,
        in_specs=[_ANY] * n,
        out_specs=[_ANY] * n,
        out_shape=[jax.ShapeDtypeStruct((N_DEV,) + p.shape, p.dtype) for p in payloads],
        scratch_shapes=[pltpu.SemaphoreType.DMA((n_copies * n,)), pltpu.SemaphoreType.DMA((n_copies * n,)),
                        pltpu.SemaphoreType.DMA((n,))],
        name=name,
    )(*payloads)


def exchange_pair(bigs, small, name):
    n = len(bigs)
    has_small = small is not None
    n_big = N_CHIPS * n
    n_sems = n_big + (N_DEV - 1 if has_small else 0)

    def body(*refs):
        big_refs = refs[:n]
        pos = n + has_small
        sib_refs = refs[pos:pos + n]
        send_sems, recv_sems, local_sem = refs[-3:]
        x, y, c, me = _mesh_position()
        sibling = (x, y, 1 - c)
        copies = []
        for b in range(n):
            for q in range(N_CHIPS):
                copies.append(pltpu.make_async_remote_copy(
                    src_ref=big_refs[b].at[_dev_index(q // 2, q % 2, 1 - c)], dst_ref=sib_refs[b].at[q],
                    send_sem=send_sems.at[b * N_CHIPS + q], recv_sem=recv_sems.at[b * N_CHIPS + q],
                    device_id=sibling, device_id_type=pl.DeviceIdType.MESH))
        waits = list(copies)
        if has_small:
            small_ref, gsmall_ref = refs[n], refs[pos + n]
            own = pltpu.make_async_copy(small_ref.at[me], gsmall_ref.at[me], local_sem.at[0])
            own.start()
            for k in range(1, N_DEV):
                peer, peer_idx = _flip(x, y, c, k)
                s = n_big + k - 1
                copies.append(pltpu.make_async_remote_copy(
                    src_ref=small_ref.at[peer_idx], dst_ref=gsmall_ref.at[me], send_sem=send_sems.at[s],
                    recv_sem=recv_sems.at[s], device_id=peer, device_id_type=pl.DeviceIdType.MESH))
                waits.append(pltpu.make_async_remote_copy(
                    src_ref=small_ref.at[peer_idx], dst_ref=gsmall_ref.at[peer_idx], send_sem=send_sems.at[s],
                    recv_sem=recv_sems.at[s], device_id=peer, device_id_type=pl.DeviceIdType.MESH))
        for cp in copies:
            cp.start()
        for cp in waits:
            cp.wait_recv()
        for cp in copies:
            cp.wait_send()
        if has_small:
            own.wait()

    out_shape = [jax.ShapeDtypeStruct((N_CHIPS,) + b.shape[1:], b.dtype) for b in bigs]
    args = list(bigs)
    if has_small:
        out_shape.append(jax.ShapeDtypeStruct(small.shape, small.dtype))
        args.append(small)
    return pl.pallas_call(
        body,
        in_specs=[_ANY] * len(args),
        out_specs=[_ANY] * len(out_shape),
        out_shape=out_shape,
        scratch_shapes=[pltpu.SemaphoreType.DMA((n_sems,)), pltpu.SemaphoreType.DMA((n_sems,)),
                        pltpu.SemaphoreType.DMA((1,))],
        name=name,
    )(*args)


def pair_sum(bigs, from_sibling, core, name):
    n = len(bigs)

    def body(core_ref, *refs):
        del core_ref
        for mine, sib, out in zip(refs[:n], refs[n:2 * n], refs[2 * n:]):
            out[...] = (mine[...].astype(F32) + sib[...].astype(F32)).astype(out.dtype)

    def slot(shape, picked):
        zeros = (0,) * (len(shape) - 1)
        if picked:
            return pl.BlockSpec((None,) + shape[1:], lambda q, core_ref: (2 * q + core_ref[0],) + zeros)
        return pl.BlockSpec((None,) + shape[1:], lambda q, core_ref: (q,) + zeros)

    grid_spec = pltpu.PrefetchScalarGridSpec(
        num_scalar_prefetch=1,
        grid=(N_CHIPS,),
        in_specs=[slot(b.shape, True) for b in bigs] + [slot(s.shape, False) for s in from_sibling],
        out_specs=[slot(s.shape, False) for s in from_sibling],
    )
    return pl.pallas_call(
        body,
        grid_spec=grid_spec,
        out_shape=[jax.ShapeDtypeStruct(s.shape, s.dtype) for s in from_sibling],
        compiler_params=_params("arbitrary"),
        name=name,
    )(core, *bigs, *from_sibling)


def exchange_chips(partials, name):
    n = len(partials)

    def body(*refs):
        part_refs, got_refs = refs[:n], refs[n:2 * n]
        send_sems, recv_sems, local_sems = refs[2 * n:]
        x, y, c, me = _mesh_position()
        mine = 2 * x + y
        own = [pltpu.make_async_copy(part_refs[b].at[mine], got_refs[b].at[mine], local_sems.at[b])
               for b in range(n)]
        for cp in own:
            cp.start()
        sends, recvs = [], []
        for q, (px, py) in enumerate(_other_chips(x, y)):
            theirs = 2 * px + py
            for b in range(n):
                s = q * n + b
                sends.append(pltpu.make_async_remote_copy(
                    src_ref=part_refs[b].at[theirs], dst_ref=got_refs[b].at[mine], send_sem=send_sems.at[s],
                    recv_sem=recv_sems.at[s], device_id=(px, py, c), device_id_type=pl.DeviceIdType.MESH))
                recvs.append(pltpu.make_async_remote_copy(
                    src_ref=part_refs[b].at[theirs], dst_ref=got_refs[b].at[theirs], send_sem=send_sems.at[s],
                    recv_sem=recv_sems.at[s], device_id=(px, py, c), device_id_type=pl.DeviceIdType.MESH))
        for cp in sends:
            cp.start()
        for cp in recvs:
            cp.wait_recv()
        for cp in sends:
            cp.wait_send()
        for cp in own:
            cp.wait()

    n_sems = (N_CHIPS - 1) * n
    return pl.pallas_call(
        body,
        in_specs=[_ANY] * n,
        out_specs=[_ANY] * n,
        out_shape=[jax.ShapeDtypeStruct(p.shape, p.dtype) for p in partials],
        scratch_shapes=[pltpu.SemaphoreType.DMA((n_sems,)), pltpu.SemaphoreType.DMA((n_sems,)),
                        pltpu.SemaphoreType.DMA((n,))],
        name=name,
    )(*partials)


def _adamw_math(g, w, m, v):
    m_new = ADAM_B1 * m + (1.0 - ADAM_B1) * g
    v_new = ADAM_B2 * v + (1.0 - ADAM_B2) * (g * g)
    m_hat = m_new / (1.0 - ADAM_B1 ** ADAM_STEP)
    v_hat = v_new / (1.0 - ADAM_B2 ** ADAM_STEP)
    delta = -ADAM_LR * (m_hat / (jnp.sqrt(v_hat) + ADAM_EPS) + ADAM_WD * w)
    return delta, m_new, v_new


def adamw_shard(parts, w, m, v, name):
    depth = len(parts)

    def body(*refs):
        p_refs = refs[:depth]
        w_ref, m_ref, v_ref, g_ref, d_ref, mo_ref, vo_ref = refs[depth:]
        for l in range(depth):
            g = p_refs[l][0].astype(F32)
            for q in range(1, N_CHIPS):
                g = g + p_refs[l][q].astype(F32)
            delta, m_new, v_new = _adamw_math(g, w_ref[l], m_ref[l], v_ref[l])
            g_ref[l] = g
            d_ref[l] = delta
            mo_ref[l] = m_new
            vo_ref[l] = v_new

    return pl.pallas_call(
        body,
        out_shape=[jax.ShapeDtypeStruct(w.shape, F32)] * 4,
        compiler_params=pltpu.CompilerParams(vmem_limit_bytes=VMEM_LIMIT),
        name=name,
    )(*parts, w, m, v)


def adamw_packed(parts, w, m, v, name):
    n_slots = parts.shape[0]

    def body(p_ref, w_ref, m_ref, v_ref, g_ref, d_ref, mo_ref, vo_ref):
        g = p_ref[0]
        for s in range(1, n_slots):
            g = g + p_ref[s]
        delta, m_new, v_new = _adamw_math(g, w_ref[...], m_ref[...], v_ref[...])
        g_ref[...] = g
        d_ref[...] = delta
        mo_ref[...] = m_new
        vo_ref[...] = v_new

    return pl.pallas_call(
        body,
        out_shape=[jax.ShapeDtypeStruct(w.shape, F32)] * 4,
        compiler_params=pltpu.CompilerParams(vmem_limit_bytes=VMEM_LIMIT),
        name=name,
    )(parts, w, m, v)


SHARDED = ("w_in", "w_q_b", "w_kv_b", "w_out", "w_gate", "w_up", "w_down")
REPLICATED = ("attn_norm_g", "w_pool", "pool_scale", "q_a_norm_g", "kv_a_norm_g", "q_norm_g", "k_norm_g",
              "ffn_norm_g")


def _pack_flat(arrays, rows):
    flat = jnp.concatenate([a.reshape(-1) for a in arrays])
    return jnp.pad(flat, (0, rows * LANES - flat.shape[0])).reshape(rows, LANES)


def _unpack_flat(packed, shapes):
    flat, out, at = packed.reshape(-1), [], 0
    for shp in shapes:
        n = 1
        for d in shp:
            n *= d
        out.append(flat[at:at + n].reshape(shp))
        at += n
    return out


def _rope_lane_tables(length):
    inv = 1.0 / (ROPE_THETA ** (jnp.arange(0, QK_ROPE_DIM, 2, dtype=F32) / QK_ROPE_DIM))
    ang = jnp.arange(length, dtype=F32)[:, None] * inv[None, :]
    cos, sin = jnp.cos(ang), jnp.sin(ang)
    ones = jnp.ones((length, QK_NOPE_DIM), F32)
    zeros = jnp.zeros((length, QK_NOPE_DIM), F32)
    tail = HEAD_PAD - QK_HEAD_DIM
    cosf = jnp.concatenate([ones, cos, cos, ones[:, :tail]], axis=1)
    sins = jnp.concatenate([zeros, -sin, sin, zeros[:, :tail]], axis=1)
    return cosf, sins


def _pad_lanes(vec, at, width=HEAD_PAD):
    return jnp.pad(vec, (at, width - at - vec.shape[0])).reshape(1, width)


def kernel(x, meta_tokens, attn_norm_g, w_in, w_pool, pool_scale, q_a_norm_g, w_q_b, kv_a_norm_g, w_kv_b, q_norm_g, k_norm_g, w_out, ffn_norm_g, w_gate, w_up, w_down, loss_target, m_meta_tokens, m_attn_norm_g, m_w_in, m_w_pool, m_pool_scale, m_q_a_norm_g, m_w_q_b, m_kv_a_norm_g, m_w_kv_b, m_q_norm_g, m_k_norm_g, m_w_out, m_ffn_norm_g, m_w_gate, m_w_up, m_w_down, v_meta_tokens, v_attn_norm_g, v_w_in, v_w_pool, v_pool_scale, v_q_a_norm_g, v_w_q_b, v_kv_a_norm_g, v_w_kv_b, v_q_norm_g, v_k_norm_g, v_w_out, v_ffn_norm_g, v_w_gate, v_w_up, v_w_down):
    weights = dict(meta_tokens=meta_tokens, attn_norm_g=attn_norm_g, w_in=w_in, w_pool=w_pool, pool_scale=pool_scale,
                   q_a_norm_g=q_a_norm_g, w_q_b=w_q_b, kv_a_norm_g=kv_a_norm_g, w_kv_b=w_kv_b, q_norm_g=q_norm_g,
                   k_norm_g=k_norm_g, w_out=w_out, ffn_norm_g=ffn_norm_g, w_gate=w_gate, w_up=w_up, w_down=w_down)
    mom1 = dict(meta_tokens=m_meta_tokens, attn_norm_g=m_attn_norm_g, w_in=m_w_in, w_pool=m_w_pool,
                pool_scale=m_pool_scale, q_a_norm_g=m_q_a_norm_g, w_q_b=m_w_q_b, kv_a_norm_g=m_kv_a_norm_g,
                w_kv_b=m_w_kv_b, q_norm_g=m_q_norm_g, k_norm_g=m_k_norm_g, w_out=m_w_out, ffn_norm_g=m_ffn_norm_g,
                w_gate=m_w_gate, w_up=m_w_up, w_down=m_w_down)
    mom2 = dict(meta_tokens=v_meta_tokens, attn_norm_g=v_attn_norm_g, w_in=v_w_in, w_pool=v_w_pool,
                pool_scale=v_pool_scale, q_a_norm_g=v_q_a_norm_g, w_q_b=v_w_q_b, kv_a_norm_g=v_kv_a_norm_g,
                w_kv_b=v_w_kv_b, q_norm_g=v_q_norm_g, k_norm_g=v_k_norm_g, w_out=v_w_out, ffn_norm_g=v_ffn_norm_g,
                w_gate=v_w_gate, w_up=v_w_up, w_down=v_w_down)
    order = ("meta_tokens", "attn_norm_g", "w_in", "w_pool", "pool_scale", "q_a_norm_g", "w_q_b", "kv_a_norm_g",
             "w_kv_b", "q_norm_g", "k_norm_g", "w_out", "ffn_norm_g", "w_gate", "w_up", "w_down")
    depth = w_in.shape[0]
    seq = x.shape[1]
    length = N_META + seq
    lp = -(-length // ROW_TILE) * ROW_TILE
    in_cols = w_in.shape[2]

    shards = cast_bf16([weights[n] for n in SHARDED], "cast_weights")
    gathered = all_gather(list(shards) + [meta_tokens], "all_gather")
    g8 = dict(zip(SHARDED, gathered[:-1]))
    meta_full = jnp.transpose(gathered[-1], (1, 0, 2)).reshape(N_META, D_MODEL)

    s1, s2, s3 = POOL_WIDTH, POOL_WIDTH + Q_LORA_RANK, POOL_WIDTH + Q_LORA_RANK + KV_LORA_RANK
    w_in_full = jnp.transpose(g8["w_in"], (1, 2, 0, 3)).reshape(depth, D_MODEL, N_DEV * in_cols)
    zpad = lambda n: jnp.zeros((depth, D_MODEL, n), BF16)
    w_in_p = jnp.concatenate([w_in_full[:, :, :s3], zpad(QK_NOPE_DIM), w_in_full[:, :, s3:],
                              zpad(HEAD_PAD - QK_HEAD_DIM)], axis=2)
    w_pool_b = w_pool.astype(BF16)
    vw = MLA_HEADS * V_HEAD_DIM

    cosf, sins = _rope_lane_tables(lp)
    row = lambda a, l: a[l].reshape(1, -1)

    h = jnp.concatenate([meta_full, x[0], jnp.zeros((lp - length, D_MODEL), F32)], axis=0)
    target = jnp.pad(loss_target[0], ((N_META, lp - length), (0, 0)))
    saved = []
    for l in range(depth):
        gq = _pad_lanes(q_norm_g[l], 0)
        gkn = _pad_lanes(k_norm_g[l, :QK_NOPE_DIM], 0)
        gkr = _pad_lanes(k_norm_g[l, QK_NOPE_DIM:], QK_NOPE_DIM)
        a, u, c_q, c_kv, kr = norm_mm(
            h, row(attn_norm_g, l), w_in_p, l,
            [(0, s1), (s1, Q_LORA_RANK), (s2, KV_LORA_RANK), (s3, HEAD_PAD)], [F32] * 4, "in_proj")
        cat = pool_fwd(u, w_pool_b[l], row(pool_scale, l), "pool_fwd")
        qn, q = norm_mm_heads(c_q, row(q_a_norm_g, l), g8["w_q_b"], l, "q_proj")
        kvn, kv = norm_mm_heads(c_kv, row(kv_a_norm_g, l), g8["w_kv_b"], l, "kv_proj")
        qp, kp, v = qk_prep_fwd(q, kv, kr, cosf, sins, gq, gkn, gkr, "qk_prep_fwd")
        o, lse, cat = attn_fwd(qp, kp, v, cat, "attn_fwd")
        h_mid, g = mm_res(cat, g8["w_out"], h, l, "out_proj", gamma=row(ffn_norm_g, l))
        gate, up, act = ffn_up(g, g8["w_gate"], g8["w_up"], l, "ffn_up")
        h_next = mm_res(act, g8["w_down"], h_mid, l, "ffn_down")[0]
        saved.append(dict(h=h, a=a, u=u, c_q=c_q, c_kv=c_kv, kr=kr, qn=qn, q=q, kvn=kvn, kv=kv, v=v, qp=qp, kp=kp,
                          o=o, lse=lse, cat=cat, h_mid=h_mid, g=g, gate=gate, up=up, act=act,
                          gq=gq, gkn=gkn, gkr=gkr))
        h = h_next

    dh, sq = loss_head(h, target, seq, "loss_head")
    loss = lax.psum(0.5 / D_MODEL * jnp.sum(sq), ("x", "y", "c"))

    core = lax.axis_index("c").astype(jnp.int32).reshape(1)
    small_grads = {n: [None] * depth for n in REPLICATED}
    got = [None] * depth
    for l in reversed(range(depth)):
        s = saved[l]
        slots = {}
        dgate, dup = ffn_bwd_act(dh, g8["w_down"], l, s["gate"], s["up"], "ffn_bwd_act")
        slots["w_down"] = mm_tn(s["act"], dh, "dw_down", out_dtype=BF16)
        slots["w_gate"] = mm_tn(s["g"], dgate, "dw_gate", out_dtype=BF16)
        slots["w_up"] = mm_tn(s["g"], dup, "dw_up", out_dtype=BF16)
        dh_mid, dg_ffn = mm_nt_normbwd([("blocks", dgate, g8["w_gate"]), ("blocks", dup, g8["w_up"])], s["h_mid"],
                                       row(ffn_norm_g, l), dh, l, "ffn_bwd_in")
        small_grads["ffn_norm_g"][l] = dg_ffn[0]
        slots["w_out"] = mm_tn(s["cat"], dh_mid, "dw_out", out_dtype=BF16).reshape(g8["w_out"].shape[:1]
                                                                                  + g8["w_out"].shape[2:])
        dy_pool, do = mm_nt(dh_mid, g8["w_out"], l, [(0, POOL_WIDTH), (POOL_WIDTH, vw)], "out_proj_bwd")
        du, dw_pool, dscale = pool_bwd(s["u"], dy_pool, w_pool_b[l], row(pool_scale, l), "pool_bwd")
        small_grads["w_pool"][l] = dw_pool
        small_grads["pool_scale"][l] = dscale[0]
        delta = attn_delta(do, s["o"], "attn_delta")
        dqp, dkp, dv = attn_bwd(s["qp"], s["kp"], s["v"], do, s["lse"], delta, "attn_bwd")
        dq, dkv, dkr, dgq, dgkn, dgkr = qk_prep_bwd(dqp, dkp, dv, s["q"], s["kv"], s["kr"], cosf, sins,
                                                    s["gq"], s["gkn"], s["gkr"], "qk_prep_bwd")
        small_grads["q_norm_g"][l] = dgq[0, :QK_HEAD_DIM]
        small_grads["k_norm_g"][l] = jnp.concatenate([dgkn[0, :QK_NOPE_DIM], dgkr[0, QK_NOPE_DIM:QK_HEAD_DIM]])
        slots["w_q_b"] = mm_tn(s["qn"], dq, "dw_q", out_dtype=BF16, b_head=QK_HEAD_DIM)
        dc_q, dg_qa = mm_nt_normbwd([("heads", dq, g8["w_q_b"])], s["c_q"], row(q_a_norm_g, l), None, l,
                                    "q_proj_bwd")
        small_grads["q_a_norm_g"][l] = dg_qa[0]
        slots["w_kv_b"] = mm_tn(s["kvn"], dkv, "dw_kv", out_dtype=BF16, b_head=HEAD_PAD)
        dc_kv, dg_kva = mm_nt_normbwd([("heads", dkv, g8["w_kv_b"])], s["c_kv"], row(kv_a_norm_g, l), None, l,
                                      "kv_proj_bwd")
        small_grads["kv_a_norm_g"][l] = dg_kva[0]
        dw_in = jnp.concatenate(
            [mm_tn(s["a"], du, "dw_in_pool"), mm_tn(s["a"], dc_q, "dw_in_q"), mm_tn(s["a"], dc_kv, "dw_in_kv"),
             mm_tn(s["a"], dkr, "dw_in_rope")[:, QK_NOPE_DIM:QK_HEAD_DIM]], axis=1)
        slots["w_in"] = jnp.transpose(dw_in.reshape(D_MODEL, N_DEV, in_cols), (1, 0, 2)).astype(BF16)
        dh, dg_attn = mm_nt_normbwd(
            [("cols", du, w_in_p, 0), ("cols", dc_q, w_in_p, s1), ("cols", dc_kv, w_in_p, s2),
             ("cols", dkr, w_in_p, s3)],
            s["h"], row(attn_norm_g, l), dh_mid, l, "in_proj_bwd")
        small_grads["attn_norm_g"][l] = dg_attn[0]

        bigs = [slots[n] for n in SHARDED]
        if l > 0:
            from_sibling = exchange_pair(bigs, None, "grad_exchange_pair")
        else:
            rep_shapes = [weights[n].shape for n in REPLICATED]
            rep_count = sum(int(jnp.size(weights[n])) for n in REPLICATED)
            rep_rows = -(-rep_count // (8 * LANES)) * 8
            rep_packed = _pack_flat([jnp.stack(small_grads[n], axis=0) for n in REPLICATED], rep_rows)
            meta_slots = jnp.transpose(dh[:N_META].reshape(N_META, N_DEV, LANES), (1, 0, 2))
            small_slots = jnp.concatenate(
                [meta_slots, jnp.broadcast_to(rep_packed[None], (N_DEV, rep_rows, LANES))], axis=1)
            *from_sibling, got_small = exchange_pair(bigs, small_slots, "grad_exchange_pair_small")
        partial = pair_sum(bigs, list(from_sibling), core, "grad_pair_sum")
        got[l] = dict(zip(SHARDED, exchange_chips(list(partial), "grad_exchange_chips")))

    grad_x = dh[N_META:length][None]

    per = [{} for _ in range(4)]
    for n in SHARDED:
        outs = adamw_shard([got[l][n] for l in range(depth)], weights[n], mom1[n], mom2[n], "adamw_" + n)
        for k in range(4):
            per[k][n] = outs[k]
    ps = lambda src: jnp.concatenate(
        [src["meta_tokens"], _pack_flat([src[n] for n in REPLICATED], rep_rows)], axis=0)
    small_out = adamw_packed(got_small, ps(weights), ps(mom1), ps(mom2), "adamw_small")
    for k in range(4):
        per[k]["meta_tokens"] = small_out[k][:N_META]
        per[k].update(zip(REPLICATED, _unpack_flat(small_out[k][N_META:], rep_shapes)))
    return (loss, grad_x, *[per[0][n] for n in order], *[per[1][n] for n in order],
            *[per[2][n] for n in order], *[per[3][n] for n in order])
```

```python
import functools

import jax
import jax.numpy as jnp
from jax import lax
from jax.experimental import pallas as pl
from jax.experimental.pallas import tpu as pltpu

F32 = jnp.float32
BF16 = jnp.bfloat16

D_MODEL = 1024
N_META = 16
POOL_WIDTH = 512
POOL_WINDOWS = (2, 4, 8, 16)
POOL_GROUP_DIM = 128
POOL_HALO = 16
MLA_HEADS = 8
QK_NOPE_DIM = 64
QK_ROPE_DIM = 32
QK_HEAD_DIM = 96
V_HEAD_DIM = 64
HEAD_PAD = 128
Q_LORA_RANK = 384
KV_LORA_RANK = 256
ROPE_THETA = 10000.0
RMS_EPS = 1e-6
ATTN_SCALE = QK_HEAD_DIM ** -0.5

ADAM_LR = 0.001
ADAM_B1 = 0.9
ADAM_B2 = 0.999
ADAM_EPS = 1e-08
ADAM_WD = 0.01
ADAM_STEP = 10

N_DEV = 8
N_CHIPS = 4
LANES = 128
ROW_TILE = 384
LONG_TILE = 1056
VMEM_LIMIT = 56 * 1024 * 1024


def _params(*sem):
    return pltpu.CompilerParams(dimension_semantics=sem, vmem_limit_bytes=VMEM_LIMIT)


def _row_spec(tile, width):
    return pl.BlockSpec((tile, width), lambda i: (i, 0))


def _const_spec(shape):
    return pl.BlockSpec(shape, lambda i: tuple(0 for _ in shape))


def _layer_spec(w, layer):
    return pl.BlockSpec((None,) + w.shape[1:], lambda *_: (layer, 0, 0))


def _gathered_spec(w8, layer):
    return pl.BlockSpec((N_DEV, None) + w8.shape[2:], lambda *_: (0, layer, 0, 0))


def _nt(a, b):
    return lax.dot_general(a, b, (((1,), (1,)), ((), ())), preferred_element_type=F32)


def _tn(a, b):
    return lax.dot_general(a, b, (((0,), (0,)), ((), ())), preferred_element_type=F32)


def _nn(a, b):
    return jnp.dot(a, b, preferred_element_type=F32)


def _silu(g):
    return g * (1.0 / (1.0 + jnp.exp(-g)))


def _rms(xf):
    return lax.rsqrt(jnp.mean(xf * xf, axis=-1, keepdims=True) + RMS_EPS)


def norm_mm(x, gamma, wt, layer, splits, dtypes, name):
    L, K = x.shape
    tm = ROW_TILE

    def body(x_ref, g_ref, w_ref, a_ref, *z_refs):
        xf = x_ref[...]
        a = ((xf * _rms(xf)) * g_ref[...]).astype(BF16)
        a_ref[...] = a
        z = _nt(a, w_ref[...])
        for (s, n), zr in zip(splits, z_refs):
            zr[...] = z[:, s:s + n].astype(zr.dtype)

    widths = [n for _, n in splits]
    return pl.pallas_call(
        body,
        grid=(L // tm,),
        in_specs=[_row_spec(tm, K), _const_spec((1, K)), _layer_spec(wt, layer)],
        out_specs=[_row_spec(tm, K)] + [_row_spec(tm, n) for n in widths],
        out_shape=[jax.ShapeDtypeStruct((L, K), BF16)]
        + [jax.ShapeDtypeStruct((L, n), dt) for n, dt in zip(widths, dtypes)],
        compiler_params=_params("arbitrary"),
        name=name,
    )(x, gamma, wt)


def norm_mm_heads(x, gamma, w8, layer, name):
    L, K = x.shape
    hw = w8.shape[-1]
    tm = ROW_TILE

    def body(x_ref, g_ref, w_ref, a_ref, z_ref):
        xf = x_ref[...]
        a = ((xf * _rms(xf)) * g_ref[...]).astype(BF16)
        a_ref[...] = a
        if hw < HEAD_PAD:
            z_ref[...] = jnp.zeros_like(z_ref)
        for j in range(MLA_HEADS):
            z_ref[:, j * HEAD_PAD:j * HEAD_PAD + hw] = _nn(a, w_ref[j])

    return pl.pallas_call(
        body,
        grid=(L // tm,),
        in_specs=[_row_spec(tm, K), _const_spec((1, K)), _gathered_spec(w8, layer)],
        out_specs=[_row_spec(tm, K), _row_spec(tm, MLA_HEADS * HEAD_PAD)],
        out_shape=[jax.ShapeDtypeStruct((L, K), BF16), jax.ShapeDtypeStruct((L, MLA_HEADS * HEAD_PAD), F32)],
        compiler_params=_params("arbitrary"),
        name=name,
    )(x, gamma, w8)


FF_GROUP = 4


def _ff_spec(w8, layer):
    return pl.BlockSpec((FF_GROUP, None) + w8.shape[2:], lambda j, i: (j, layer, 0, 0))


def ffn_up(g, w_gate8, w_up8, layer, name):
    L, K = g.shape
    fb = w_gate8.shape[-2]
    tm, tf = ROW_TILE, FF_GROUP * fb

    def body(a_ref, wg_ref, wu_ref, gate_ref, up_ref, act_ref):
        a = a_ref[...]
        gate = _nt(a, wg_ref[...].reshape(tf, K))
        up = _nt(a, wu_ref[...].reshape(tf, K))
        gate_ref[...] = gate
        up_ref[...] = up
        act_ref[...] = (_silu(gate) * up).astype(BF16)

    tile = pl.BlockSpec((tm, tf), lambda j, i: (i, j))
    F = N_DEV * fb
    return pl.pallas_call(
        body,
        grid=(N_DEV // FF_GROUP, L // tm),
        in_specs=[pl.BlockSpec((tm, K), lambda j, i: (i, 0)), _ff_spec(w_gate8, layer), _ff_spec(w_up8, layer)],
        out_specs=[tile, tile, tile],
        out_shape=[
            jax.ShapeDtypeStruct((L, F), F32),
            jax.ShapeDtypeStruct((L, F), F32),
            jax.ShapeDtypeStruct((L, F), BF16),
        ],
        compiler_params=_params("arbitrary", "arbitrary"),
        name=name,
    )(g, w_gate8, w_up8)


def mm_res(a, w8, res, layer, name, gamma=None):
    L = a.shape[0]
    kb, N = w8.shape[-2:]
    tm = ROW_TILE
    normed = gamma is not None

    def body(a_ref, w_ref, r_ref, *rest):
        out = r_ref[...] + _nn(a_ref[...], w_ref[...].reshape(N_DEV * kb, N))
        if normed:
            g_ref, o_ref, n_ref = rest
            n_ref[...] = ((out * _rms(out)) * g_ref[...]).astype(BF16)
        else:
            (o_ref,) = rest
        o_ref[...] = out

    in_specs = [_row_spec(tm, N_DEV * kb), _gathered_spec(w8, layer), _row_spec(tm, N)]
    out_specs = [_row_spec(tm, N)]
    out_shape = [jax.ShapeDtypeStruct((L, N), F32)]
    args = [a, w8, res]
    if normed:
        in_specs.append(_const_spec((1, N)))
        out_specs.append(_row_spec(tm, N))
        out_shape.append(jax.ShapeDtypeStruct((L, N), BF16))
        args.append(gamma)
    return pl.pallas_call(
        body,
        grid=(L // tm,),
        in_specs=in_specs,
        out_specs=out_specs,
        out_shape=out_shape,
        compiler_params=_params("arbitrary"),
        name=name,
    )(*args)


def mm_nt(dz, w8, layer, splits, name):
    L, N = dz.shape
    kb = w8.shape[-2]
    tm = ROW_TILE

    def body(dz_ref, w_ref, *o_refs):
        d = _nt(dz_ref[...].astype(BF16), w_ref[...].reshape(N_DEV * kb, N))
        for (s, n), o in zip(splits, o_refs):
            o[...] = d[:, s:s + n]

    return pl.pallas_call(
        body,
        grid=(L // tm,),
        in_specs=[_row_spec(tm, N), _gathered_spec(w8, layer)],
        out_specs=[_row_spec(tm, n) for _, n in splits],
        out_shape=[jax.ShapeDtypeStruct((L, n), F32) for _, n in splits],
        compiler_params=_params("arbitrary"),
        name=name,
    )(dz, w8)


def mm_nt_normbwd(terms, x, gamma, dres, layer, name):
    L, K = x.shape
    tm = ROW_TILE
    n_terms = len(terms)
    has_res = dres is not None
    weights = []
    for t in terms:
        if not any(t[2] is u for u in weights):
            weights.append(t[2])
    which = [[t[2] is u for u in weights].index(True) for t in terms]
    n_in = n_terms + len(weights)

    def body(*refs):
        dz_refs = refs[:n_terms]
        w_refs = [refs[n_terms + n] for n in which]
        x_ref, g_ref = refs[n_in], refs[n_in + 1]
        pos = n_in + 2
        r_ref = refs[pos] if has_res else None
        dx_ref, dg_ref = refs[pos + has_res], refs[pos + has_res + 1]
        da = None
        for t, dz_ref, w_ref in zip(terms, dz_refs, w_refs):
            if t[0] == "rows":
                n, at = t[1].shape[1], t[3]
                parts = [_nn(dz_ref[...].astype(BF16), w_ref[at:at + n, :])]
            elif t[0] == "heads":
                hw = t[2].shape[-1]
                parts = [_nt(dz_ref[:, j * HEAD_PAD:j * HEAD_PAD + hw].astype(BF16), w_ref[j])
                         for j in range(MLA_HEADS)]
            else:
                nb = t[2].shape[-2]
                parts = [_nn(dz_ref[...].astype(BF16), w_ref[...].reshape(N_DEV * nb, K))]
            for p in parts:
                da = p if da is None else da + p
        xf = x_ref[...]
        r = _rms(xf)
        xh = xf * r

        @pl.when(pl.program_id(0) == 0)
        def _():
            dg_ref[...] = jnp.zeros_like(dg_ref)

        dg_ref[...] += jnp.sum(da * xh, axis=0, keepdims=True)
        dxh = da * g_ref[...]
        dx = r * (dxh - xh * jnp.mean(dxh * xh, axis=-1, keepdims=True))
        if has_res:
            dx = dx + r_ref[...]
        dx_ref[...] = dx

    in_specs = [_row_spec(tm, t[1].shape[1]) for t in terms]
    for w in weights:
        in_specs.append(_layer_spec(w, layer) if w.ndim == 3 else _gathered_spec(w, layer))
    in_specs += [_row_spec(tm, K), _const_spec((1, K))]
    args = [t[1] for t in terms] + weights + [x, gamma]
    if has_res:
        in_specs.append(_row_spec(tm, K))
        args.append(dres)
    return pl.pallas_call(
        body,
        grid=(L // tm,),
        in_specs=in_specs,
        out_specs=[_row_spec(tm, K), _const_spec((1, K))],
        out_shape=[jax.ShapeDtypeStruct((L, K), F32), jax.ShapeDtypeStruct((1, K), F32)],
        compiler_params=_params("arbitrary"),
        name=name,
    )(*args)


MAX_OUT_ROWS = 1408


def mm_tn(a, b, name, out_dtype=F32, b_head=None):
    L, K = a.shape
    N = b_head or b.shape[-1]
    tk = MAX_OUT_ROWS if (K > MAX_OUT_ROWS and K % MAX_OUT_ROWS == 0) else K
    blocks = N_DEV if b_head else K // tk
    tl = LONG_TILE if L % LONG_TILE == 0 else ROW_TILE
    n_l = L // tl

    def body(a_ref, b_ref, o_ref, acc):
        l = pl.program_id(1)

        @pl.when(l == 0)
        def _():
            acc[...] = jnp.zeros_like(acc)

        bt = b_ref[...] if not b_head else b_ref[:, :N]
        acc[...] += _tn(a_ref[...].astype(BF16), bt.astype(BF16))

        @pl.when(l == n_l - 1)
        def _():
            o_ref[...] = acc[...].astype(o_ref.dtype)

    if b_head:
        a_spec = pl.BlockSpec((tl, K), lambda j, l: (l, 0))
        b_spec = pl.BlockSpec((tl, HEAD_PAD), lambda j, l: (l, j))
        out_spec = pl.BlockSpec((None, K, N), lambda j, l: (j, 0, 0))
        out_shape = jax.ShapeDtypeStruct((N_DEV, K, N), out_dtype)
    else:
        a_spec = pl.BlockSpec((tl, tk), lambda j, l: (l, j))
        b_spec = pl.BlockSpec((tl, N), lambda j, l: (l, 0))
        out_spec = pl.BlockSpec((tk, N), lambda j, l: (j, 0))
        out_shape = jax.ShapeDtypeStruct((K, N), out_dtype)
    return pl.pallas_call(
        body,
        grid=(blocks, n_l),
        in_specs=[a_spec, b_spec],
        out_specs=out_spec,
        out_shape=out_shape,
        scratch_shapes=[pltpu.VMEM((tk, N), F32)],
        compiler_params=_params("arbitrary", "arbitrary"),
        name=name,
    )(a, b)


def ffn_bwd_act(dh, w_down8, layer, gate, up, name):
    L, K = dh.shape
    fb = w_down8.shape[-2]
    tm, tf = ROW_TILE, FF_GROUP * fb

    def body(dh_ref, w_ref, gate_ref, up_ref, dgate_ref, dup_ref):
        dact = _nt(dh_ref[...].astype(BF16), w_ref[...].reshape(tf, K))
        g = gate_ref[...]
        sig = 1.0 / (1.0 + jnp.exp(-g))
        dup_ref[...] = (dact * (g * sig)).astype(BF16)
        dgate_ref[...] = (dact * up_ref[...] * (sig * (1.0 + g * (1.0 - sig)))).astype(BF16)

    tile = pl.BlockSpec((tm, tf), lambda j, i: (i, j))
    F = N_DEV * fb
    return pl.pallas_call(
        body,
        grid=(N_DEV // FF_GROUP, L // tm),
        in_specs=[pl.BlockSpec((tm, K), lambda j, i: (i, 0)), _ff_spec(w_down8, layer), tile, tile],
        out_specs=[tile, tile],
        out_shape=[jax.ShapeDtypeStruct((L, F), BF16), jax.ShapeDtypeStruct((L, F), BF16)],
        compiler_params=_params("arbitrary", "arbitrary"),
        name=name,
    )(dh, w_down8, gate, up)


def _pool_residual(scr, lo, tm, g, w, t):
    cols = slice(g * POOL_GROUP_DIM, (g + 1) * POOL_GROUP_DIM)
    cur = scr[lo:lo + tm, cols]
    s = cur
    for k in range(1, w):
        s = s + scr[lo - k:lo - k + tm, cols]
    cnt = jnp.minimum(t + 1, w).astype(F32)
    return s / cnt - cur


def pool_fwd(u, w_pool, scale, name):
    L, C = u.shape
    tm, halo = ROW_TILE, POOL_HALO

    def body(u_ref, halo_ref, w_ref, s_ref, y_ref, scr):
        i = pl.program_id(0)
        scr[0:halo, :] = jnp.where(i > 0, halo_ref[...], 0.0)
        scr[halo:halo + tm, :] = u_ref[...]
        t = i * tm + lax.broadcasted_iota(jnp.int32, (tm, POOL_GROUP_DIM), 0)
        for g, w in enumerate(POOL_WINDOWS):
            cols = slice(g * POOL_GROUP_DIM, (g + 1) * POOL_GROUP_DIM)
            p = _pool_residual(scr, halo, tm, g, w, t)
            y = _nn(p.astype(BF16), w_ref[g]) * s_ref[:, cols]
            y_ref[:, cols] = y.astype(y_ref.dtype)

    return pl.pallas_call(
        body,
        grid=(L // tm,),
        in_specs=[
            _row_spec(tm, C),
            pl.BlockSpec((halo, C), lambda i: (jnp.maximum(i * (tm // halo) - 1, 0), 0)),
            _const_spec(w_pool.shape),
            _const_spec((1, C)),
        ],
        out_specs=_row_spec(tm, C),
        out_shape=jax.ShapeDtypeStruct((L, 2 * C), BF16),
        scratch_shapes=[pltpu.VMEM((tm + halo, C), F32)],
        compiler_params=_params("arbitrary"),
        name=name,
    )(u, u, w_pool, scale)


def pool_bwd(u, dy, w_pool, scale, name):
    L, C = u.shape
    tm, halo = ROW_TILE, POOL_HALO
    n_tiles = L // tm
    last_halo = L // halo - 1

    def body(u_ref, uh_ref, dy_ref, dyh_ref, w_ref, s_ref, du_ref, dw_ref, ds_ref, scr_u, scr_q):
        i = pl.program_id(0)

        @pl.when(i == 0)
        def _():
            dw_ref[...] = jnp.zeros_like(dw_ref)
            ds_ref[...] = jnp.zeros_like(ds_ref)

        scr_u[0:halo, :] = jnp.where(i > 0, uh_ref[...], 0.0)
        scr_u[halo:halo + tm, :] = u_ref[...]
        t = i * tm + lax.broadcasted_iota(jnp.int32, (tm, POOL_GROUP_DIM), 0)
        th = (i + 1) * tm + lax.broadcasted_iota(jnp.int32, (halo, POOL_GROUP_DIM), 0)
        for g, w in enumerate(POOL_WINDOWS):
            cols = slice(g * POOL_GROUP_DIM, (g + 1) * POOL_GROUP_DIM)
            p = _pool_residual(scr_u, halo, tm, g, w, t).astype(BF16)
            wg = w_ref[g]
            sc = s_ref[:, cols]
            dy = dy_ref[:, cols]
            ds_ref[:, cols] += jnp.sum(dy * _nn(p, wg), axis=0, keepdims=True)
            dys = (dy * sc).astype(BF16)
            dw_ref[g] += _tn(p, dys)
            dp = _nt(dys, wg)
            dyh = jnp.where(i < n_tiles - 1, dyh_ref[:, cols], 0.0)
            dph = _nt((dyh * sc).astype(BF16), wg)
            scr_q[0:tm, cols] = dp / jnp.minimum(t + 1, w).astype(F32)
            scr_q[tm:tm + halo, cols] = dph / jnp.minimum(th + 1, w).astype(F32)
            acc = scr_q[0:tm, cols]
            for k in range(1, w):
                acc = acc + scr_q[k:k + tm, cols]
            du_ref[:, cols] = acc - dp

    return pl.pallas_call(
        body,
        grid=(n_tiles,),
        in_specs=[
            _row_spec(tm, C),
            pl.BlockSpec((halo, C), lambda i: (jnp.maximum(i * (tm // halo) - 1, 0), 0)),
            _row_spec(tm, C),
            pl.BlockSpec((halo, C), lambda i: (jnp.minimum((i + 1) * (tm // halo), last_halo), 0)),
            _const_spec(w_pool.shape),
            _const_spec((1, C)),
        ],
        out_specs=[_row_spec(tm, C), _const_spec(w_pool.shape), _const_spec((1, C))],
        out_shape=[
            jax.ShapeDtypeStruct((L, C), F32),
            jax.ShapeDtypeStruct(w_pool.shape, F32),
            jax.ShapeDtypeStruct((1, C), F32),
        ],
        scratch_shapes=[pltpu.VMEM((tm + halo, C), F32), pltpu.VMEM((tm + halo, C), F32)],
        compiler_params=_params("arbitrary"),
        name=name,
    )(u, u, dy, dy, w_pool, scale)


def _head_masks(rows):
    lane = lax.broadcasted_iota(jnp.int32, (rows, HEAD_PAD), 1)
    return lane, lane < QK_NOPE_DIM, (lane >= QK_NOPE_DIM) & (lane < QK_HEAD_DIM)


def _rope_swap(x, lane):
    half = QK_ROPE_DIM // 2
    swapped = jnp.where(lane < QK_NOPE_DIM + half, pltpu.roll(x, HEAD_PAD - half, 1), pltpu.roll(x, half, 1))
    return jnp.where((lane >= QK_NOPE_DIM) & (lane < QK_HEAD_DIM), swapped, 0.0)


def _seg_mean(v, m_n, m_r):
    mn = jnp.sum(jnp.where(m_n, v, 0.0), axis=1, keepdims=True) * (1.0 / QK_NOPE_DIM)
    mr = jnp.sum(jnp.where(m_r, v, 0.0), axis=1, keepdims=True) * (1.0 / QK_ROPE_DIM)
    return jnp.where(m_n, mn, mr)


def qk_prep_fwd(q, kv, kr, cosf, sins, gq, gkn, gkr, name):
    L = q.shape[0]
    tm = ROW_TILE
    W = MLA_HEADS * HEAD_PAD

    def body(q_ref, kv_ref, kr_ref, c_ref, s_ref, gq_ref, gkn_ref, gkr_ref, qo_ref, ko_ref, vo_ref):
        lane, m_n, m_r = _head_masks(tm)
        cosf_, sins_ = c_ref[...], s_ref[...]
        kr_ = kr_ref[...]
        rk = lax.rsqrt(_seg_mean(kr_ * kr_, m_n, m_r) + RMS_EPS)
        krn = kr_ * rk * gkr_ref[...]
        krf = krn * cosf_ + _rope_swap(krn, lane) * sins_
        for h in range(MLA_HEADS):
            cols = slice(h * HEAD_PAD, (h + 1) * HEAD_PAD)
            qh = q_ref[:, cols]
            qn = qh * lax.rsqrt(_seg_mean(qh * qh, m_n, m_r) + RMS_EPS) * gq_ref[...]
            qo_ref[:, cols] = (qn * cosf_ + _rope_swap(qn, lane) * sins_).astype(BF16)
            kh = jnp.where(m_n, kv_ref[:, cols], 0.0)
            rn = lax.rsqrt(jnp.sum(kh * kh, axis=1, keepdims=True) * (1.0 / QK_NOPE_DIM) + RMS_EPS)
            ko_ref[:, cols] = (kh * rn * gkn_ref[...] + krf).astype(BF16)
        for p in range(MLA_HEADS // 2):
            even = kv_ref[:, 2 * p * HEAD_PAD:(2 * p + 1) * HEAD_PAD]
            odd = kv_ref[:, (2 * p + 1) * HEAD_PAD:(2 * p + 2) * HEAD_PAD]
            pair = jnp.where(m_n, pltpu.roll(even, V_HEAD_DIM, 1), odd)
            vo_ref[:, p * LANES:(p + 1) * LANES] = pair.astype(BF16)

    vec = _const_spec((1, HEAD_PAD))
    return pl.pallas_call(
        body,
        grid=(L // tm,),
        in_specs=[_row_spec(tm, W), _row_spec(tm, W), _row_spec(tm, HEAD_PAD), _row_spec(tm, HEAD_PAD),
                  _row_spec(tm, HEAD_PAD), vec, vec, vec],
        out_specs=[_row_spec(tm, W), _row_spec(tm, W), _row_spec(tm, W // 2)],
        out_shape=[jax.ShapeDtypeStruct((L, W), BF16), jax.ShapeDtypeStruct((L, W), BF16),
                   jax.ShapeDtypeStruct((L, W // 2), BF16)],
        compiler_params=_params("arbitrary"),
        name=name,
    )(q, kv, kr, cosf, sins, gq, gkn, gkr)


def qk_prep_bwd(dqo, dko, dv, q, kv, kr, cosf, sins, gq, gkn, gkr, name):
    L = q.shape[0]
    tm = ROW_TILE
    W = MLA_HEADS * HEAD_PAD

    def body(dqo_ref, dko_ref, dv_ref, q_ref, kv_ref, kr_ref, c_ref, s_ref, gq_ref, gkn_ref, gkr_ref,
             dq_ref, dkv_ref, dkr_ref, dgq_ref, dgkn_ref, dgkr_ref):
        @pl.when(pl.program_id(0) == 0)
        def _():
            dgq_ref[...] = jnp.zeros_like(dgq_ref)
            dgkn_ref[...] = jnp.zeros_like(dgkn_ref)
            dgkr_ref[...] = jnp.zeros_like(dgkr_ref)

        lane, m_n, m_r = _head_masks(tm)
        cosf_, sins_ = c_ref[...], s_ref[...]
        dgq = jnp.zeros((1, HEAD_PAD), F32)
        dgkn = jnp.zeros((1, HEAD_PAD), F32)
        dkrf = jnp.zeros((tm, HEAD_PAD), F32)
        for h in range(MLA_HEADS):
            cols = slice(h * HEAD_PAD, (h + 1) * HEAD_PAD)
            dy = dqo_ref[:, cols]
            dqn = dy * cosf_ + _rope_swap(dy * sins_, lane)
            qh = q_ref[:, cols]
            rinv = lax.rsqrt(_seg_mean(qh * qh, m_n, m_r) + RMS_EPS)
            xh = qh * rinv
            dgq = dgq + jnp.sum(dqn * xh, axis=0, keepdims=True)
            dxh = dqn * gq_ref[...]
            dq_ref[:, cols] = rinv * (dxh - xh * _seg_mean(dxh * xh, m_n, m_r))
            dk = dko_ref[:, cols]
            dkrf = dkrf + dk
            kh = jnp.where(m_n, kv_ref[:, cols], 0.0)
            rn = lax.rsqrt(jnp.sum(kh * kh, axis=1, keepdims=True) * (1.0 / QK_NOPE_DIM) + RMS_EPS)
            xk = kh * rn
            dgkn = dgkn + jnp.sum(dk * xk, axis=0, keepdims=True)
            dxk = dk * gkn_ref[...]
            dkn = rn * (dxk - xk * (jnp.sum(dxk * xk, axis=1, keepdims=True) * (1.0 / QK_NOPE_DIM)))
            dvp = dv_ref[:, (h // 2) * LANES:(h // 2 + 1) * LANES]
            dvh = pltpu.roll(dvp, V_HEAD_DIM, 1) if h % 2 == 0 else dvp
            dkv_ref[:, cols] = jnp.where(m_n, dkn, dvh)
        kr_ = kr_ref[...]
        rk = lax.rsqrt(_seg_mean(kr_ * kr_, m_n, m_r) + RMS_EPS)
        xr = kr_ * rk
        dkrn = dkrf * cosf_ + _rope_swap(dkrf * sins_, lane)
        dgkr_ref[...] += jnp.sum(dkrn * xr, axis=0, keepdims=True)
        dxr = dkrn * gkr_ref[...]
        dkr_ref[...] = rk * (dxr - xr * _seg_mean(dxr * xr, m_n, m_r))
        dgq_ref[...] += dgq
        dgkn_ref[...] += dgkn

    vec = _const_spec((1, HEAD_PAD))
    return pl.pallas_call(
        body,
        grid=(L // tm,),
        in_specs=[_row_spec(tm, W), _row_spec(tm, W), _row_spec(tm, W // 2), _row_spec(tm, W), _row_spec(tm, W),
                  _row_spec(tm, HEAD_PAD), _row_spec(tm, HEAD_PAD), _row_spec(tm, HEAD_PAD), vec, vec, vec],
        out_specs=[_row_spec(tm, W), _row_spec(tm, W), _row_spec(tm, HEAD_PAD), vec, vec, vec],
        out_shape=[jax.ShapeDtypeStruct((L, W), F32), jax.ShapeDtypeStruct((L, W), F32),
                   jax.ShapeDtypeStruct((L, HEAD_PAD), F32)] + [jax.ShapeDtypeStruct((1, HEAD_PAD), F32)] * 3,
        compiler_params=_params("arbitrary"),
        name=name,
    )(dqo, dko, dv, q, kv, kr, cosf, sins, gq, gkn, gkr)


def attn_fwd(qp, kp, v, cat, name):
    L = qp.shape[0]
    tq = ROW_TILE
    pair_w = 2 * HEAD_PAD
    n_pairs = MLA_HEADS // 2

    def body(q_ref, k_ref, v_ref, cat_in, o_ref, lse_ref, cat_ref):
        del cat_in
        i = pl.program_id(1)
        left = lax.broadcasted_iota(jnp.int32, (tq, LANES), 1) < V_HEAD_DIM
        row = lax.broadcasted_iota(jnp.int32, (tq, tq), 0)
        col = lax.broadcasted_iota(jnp.int32, (tq, tq), 1)

        def step(j, carry, masked):
            rows = pl.ds(pl.multiple_of(j * tq, tq), tq)
            vv = v_ref[rows, :]
            out = []
            for hh in range(2):
                cols = slice(hh * HEAD_PAD, (hh + 1) * HEAD_PAD)
                m, l, acc = carry[hh]
                s = _nt(q_ref[:, cols], k_ref[rows, cols]) * ATTN_SCALE
                if masked:
                    s = jnp.where(col <= row, s, -jnp.inf)
                m_new = jnp.maximum(m, jnp.max(s, axis=1, keepdims=True))
                alpha = jnp.exp(m - m_new)
                p = jnp.exp(s - m_new)
                l = alpha * l + jnp.sum(p, axis=1, keepdims=True)
                acc = alpha * acc + _nn(p.astype(BF16), vv)
                out.append((m_new, l, acc))
            return tuple(out)

        one = (jnp.full((tq, 1), -jnp.inf, F32), jnp.zeros((tq, 1), F32), jnp.zeros((tq, LANES), F32))
        carry = lax.fori_loop(0, i, functools.partial(step, masked=False), (one, one))
        (m0, l0, a0), (m1, l1, a1) = step(i, carry, True)
        o = jnp.where(left, a0 / l0, a1 / l1)
        o_ref[...] = o
        cat_ref[...] = o.astype(BF16)
        lse_ref[...] = jnp.where(left, m0 + jnp.log(l0), m1 + jnp.log(l1))

    return pl.pallas_call(
        body,
        grid=(n_pairs, L // tq),
        in_specs=[
            pl.BlockSpec((tq, pair_w), lambda p, i: (i, p)),
            pl.BlockSpec((L, pair_w), lambda p, i: (0, p)),
            pl.BlockSpec((L, LANES), lambda p, i: (0, p)),
            pl.BlockSpec(memory_space=pl.ANY),
        ],
        out_specs=[
            pl.BlockSpec((tq, LANES), lambda p, i: (i, p)),
            pl.BlockSpec((tq, LANES), lambda p, i: (i, p)),
            pl.BlockSpec((tq, LANES), lambda p, i: (i, n_pairs + p)),
        ],
        out_shape=[jax.ShapeDtypeStruct((L, n_pairs * LANES), F32), jax.ShapeDtypeStruct((L, n_pairs * LANES), F32),
                   jax.ShapeDtypeStruct(cat.shape, cat.dtype)],
        input_output_aliases={3: 2},
        compiler_params=_params("arbitrary", "arbitrary"),
        name=name,
    )(qp, kp, v, cat)


def attn_delta(do, o, name):
    L, C = o.shape
    tm = ROW_TILE

    def body(do_ref, o_ref, d_ref):
        left = lax.broadcasted_iota(jnp.int32, (tm, LANES), 1) < V_HEAD_DIM
        for p in range(C // LANES):
            cols = slice(p * LANES, (p + 1) * LANES)
            prod = do_ref[:, cols] * o_ref[:, cols]
            d0 = jnp.sum(jnp.where(left, prod, 0.0), axis=1, keepdims=True)
            d1 = jnp.sum(jnp.where(left, 0.0, prod), axis=1, keepdims=True)
            d_ref[:, cols] = jnp.where(left, d0, d1)

    return pl.pallas_call(
        body,
        grid=(L // tm,),
        in_specs=[_row_spec(tm, C), _row_spec(tm, C)],
        out_specs=_row_spec(tm, C),
        out_shape=jax.ShapeDtypeStruct((L, C), F32),
        compiler_params=_params("arbitrary"),
        name=name,
    )(do, o)


def attn_bwd(qp, kp, v, do, lse, delta, name):
    L = qp.shape[0]
    tq = ROW_TILE
    n_q = L // tq
    pair_w = 2 * HEAD_PAD

    def body(q_ref, k_ref, v_ref, do_ref, lse_ref, dl_ref, dq_ref, dk_ref, dv_ref):
        j = pl.program_id(1)

        @pl.when(j == 0)
        def _():
            dq_ref[...] = jnp.zeros_like(dq_ref)

        dk_ref[...] = jnp.zeros_like(dk_ref)
        dv_ref[...] = jnp.zeros_like(dv_ref)
        left = lax.broadcasted_iota(jnp.int32, (tq, LANES), 1) < V_HEAD_DIM
        row = lax.broadcasted_iota(jnp.int32, (tq, tq), 0)
        col = lax.broadcasted_iota(jnp.int32, (tq, tq), 1)

        def step(i, carry, masked):
            rows = pl.ds(pl.multiple_of(i * tq, tq), tq)
            do = do_ref[rows, :]
            vv = v_ref[...]
            dv = None
            for hh in range(2):
                cols = slice(hh * HEAD_PAD, (hh + 1) * HEAD_PAD)
                stat = slice(hh * V_HEAD_DIM, hh * V_HEAD_DIM + 1)
                q = q_ref[rows, cols]
                k = k_ref[:, cols]
                dom = jnp.where(left if hh == 0 else jnp.logical_not(left), do, 0.0).astype(BF16)
                s = _nt(q, k) * ATTN_SCALE
                p = jnp.exp(s - lse_ref[rows, stat])
                if masked:
                    p = jnp.where(col <= row, p, 0.0)
                dp = _nt(dom, vv)
                ds = (p * (dp - dl_ref[rows, stat]) * ATTN_SCALE).astype(BF16)
                dq_ref[rows, cols] += _nn(ds, k)
                dk_ref[:, cols] += _tn(ds, q)
                t = _tn(p.astype(BF16), dom)
                dv = t if dv is None else dv + t
            dv_ref[...] += dv
            return carry

        step(j, 0, True)
        lax.fori_loop(j + 1, n_q, functools.partial(step, masked=False), 0)

    n_pairs = MLA_HEADS // 2
    return pl.pallas_call(
        body,
        grid=(n_pairs, n_q),
        in_specs=[
            pl.BlockSpec((L, pair_w), lambda p, j: (0, p)),
            pl.BlockSpec((tq, pair_w), lambda p, j: (j, p)),
            pl.BlockSpec((tq, LANES), lambda p, j: (j, p)),
            pl.BlockSpec((L, LANES), lambda p, j: (0, p)),
            pl.BlockSpec((L, LANES), lambda p, j: (0, p)),
            pl.BlockSpec((L, LANES), lambda p, j: (0, p)),
        ],
        out_specs=[
            pl.BlockSpec((L, pair_w), lambda p, j: (0, p)),
            pl.BlockSpec((tq, pair_w), lambda p, j: (j, p)),
            pl.BlockSpec((tq, LANES), lambda p, j: (j, p)),
        ],
        out_shape=[
            jax.ShapeDtypeStruct((L, n_pairs * pair_w), F32),
            jax.ShapeDtypeStruct((L, n_pairs * pair_w), F32),
            jax.ShapeDtypeStruct((L, n_pairs * LANES), F32),
        ],
        compiler_params=_params("arbitrary", "arbitrary"),
        name=name,
    )(qp, kp, v, do, lse, delta)


def loss_head(h, target, n_real, name):
    L, D = h.shape
    tm = ROW_TILE

    def body(h_ref, t_ref, dh_ref, sq_ref):
        i = pl.program_id(0)

        @pl.when(i == 0)
        def _():
            sq_ref[...] = jnp.zeros_like(sq_ref)

        t = i * tm + lax.broadcasted_iota(jnp.int32, (tm, D), 0)
        real = (t >= N_META) & (t < N_META + n_real)
        diff = jnp.where(real, h_ref[...] - t_ref[...], 0.0)
        dh_ref[...] = diff * (1.0 / D)
        sq_ref[...] += jnp.sum(diff * diff, axis=0, keepdims=True)

    return pl.pallas_call(
        body,
        grid=(L // tm,),
        in_specs=[_row_spec(tm, D), _row_spec(tm, D)],
        out_specs=[_row_spec(tm, D), _const_spec((1, D))],
        out_shape=[jax.ShapeDtypeStruct((L, D), F32), jax.ShapeDtypeStruct((1, D), F32)],
        compiler_params=_params("arbitrary"),
        name=name,
    )(h, target)


def _mesh_position():
    x, y, c = lax.axis_index("x"), lax.axis_index("y"), lax.axis_index("c")
    return x, y, c, 4 * x + 2 * y + c


def _flip(x, y, c, k):
    px = 1 - x if k & 4 else x
    py = 1 - y if k & 2 else y
    pc = 1 - c if k & 1 else c
    return (px, py, pc), 4 * px + 2 * py + pc


def _other_chips(x, y):
    return [(1 - x, y), (x, 1 - y), (1 - x, 1 - y)]


def _dev_index(px, py, pc):
    return 4 * px + 2 * py + pc


_ANY = pl.BlockSpec(memory_space=pl.ANY)


def cast_bf16(arrays, name):
    n = len(arrays)

    def body(*refs):
        for src, dst in zip(refs[:n], refs[n:]):
            dst[...] = src[...].astype(BF16)

    return pl.pallas_call(
        body,
        out_shape=[jax.ShapeDtypeStruct(a.shape, BF16) for a in arrays],
        compiler_params=pltpu.CompilerParams(vmem_limit_bytes=VMEM_LIMIT),
        name=name,
    )(*arrays)


def all_gather(payloads, name):
    n = len(payloads)
    n_copies = N_DEV - 1

    def body(*refs):
        srcs, dsts = refs[:n], refs[n:2 * n]
        send_sems, recv_sems, local_sems = refs[2 * n:]
        x, y, c, me = _mesh_position()
        sibling = (x, y, 1 - c)
        chips = _other_chips(x, y)
        own = [pltpu.make_async_copy(srcs[b], dsts[b].at[me], local_sems.at[b]) for b in range(n)]
        for cp in own:
            cp.start()

        def copy(k, b, block, to, from_src=False):
            dst = dsts[b].at[block]
            return pltpu.make_async_remote_copy(
                src_ref=srcs[b] if from_src else dst, dst_ref=dst, send_sem=send_sems.at[k * n + b],
                recv_sem=recv_sems.at[k * n + b], device_id=to, device_id_type=pl.DeviceIdType.MESH)

        first = []
        for b in range(n):
            first.append(copy(0, b, me, sibling, from_src=True))
            first += [copy(1 + q, b, me, (*chip, c), from_src=True) for q, chip in enumerate(chips)]
        for cp in first:
            cp.start()
        passed = []
        for q, chip in enumerate(chips):
            block = _dev_index(*chip, c)
            for b in range(n):
                copy(1 + q, b, block, sibling).wait_recv()
                passed.append(copy(4 + q, b, block, sibling))
                passed[-1].start()
        for b in range(n):
            copy(0, b, _dev_index(x, y, 1 - c), sibling).wait_recv()
            for q, chip in enumerate(chips):
                copy(4 + q, b, _dev_index(*chip, 1 - c), sibling).wait_recv()
        for cp in first + passed:
            cp.wait_send()
        for cp in own:
            cp.wait()

    return pl.pallas_call(
        body,
        in_specs=[_ANY] * n,
        out_specs=[_ANY] * n,
        out_shape=[jax.ShapeDtypeStruct((N_DEV,) + p.shape, p.dtype) for p in payloads],
        scratch_shapes=[pltpu.SemaphoreType.DMA((n_copies * n,)), pltpu.SemaphoreType.DMA((n_copies * n,)),
                        pltpu.SemaphoreType.DMA((n,))],
        name=name,
    )(*payloads)


def exchange_pair(bigs, small, name):
    n = len(bigs)
    has_small = small is not None
    n_big = N_CHIPS * n
    n_sems = n_big + (N_DEV - 1 if has_small else 0)

    def body(*refs):
        big_refs = refs[:n]
        pos = n + has_small
        sib_refs = refs[pos:pos + n]
        send_sems, recv_sems, local_sem = refs[-3:]
        x, y, c, me = _mesh_position()
        sibling = (x, y, 1 - c)
        copies = []
        for b in range(n):
            for q in range(N_CHIPS):
                copies.append(pltpu.make_async_remote_copy(
                    src_ref=big_refs[b].at[_dev_index(q // 2, q % 2, 1 - c)], dst_ref=sib_refs[b].at[q],
                    send_sem=send_sems.at[b * N_CHIPS + q], recv_sem=recv_sems.at[b * N_CHIPS + q],
                    device_id=sibling, device_id_type=pl.DeviceIdType.MESH))
        waits = list(copies)
        if has_small:
            small_ref, gsmall_ref = refs[n], refs[pos + n]
            own = pltpu.make_async_copy(small_ref.at[me], gsmall_ref.at[me], local_sem.at[0])
            own.start()
            for k in range(1, N_DEV):
                peer, peer_idx = _flip(x, y, c, k)
                s = n_big + k - 1
                copies.append(pltpu.make_async_remote_copy(
                    src_ref=small_ref.at[peer_idx], dst_ref=gsmall_ref.at[me], send_sem=send_sems.at[s],
                    recv_sem=recv_sems.at[s], device_id=peer, device_id_type=pl.DeviceIdType.MESH))
                waits.append(pltpu.make_async_remote_copy(
                    src_ref=small_ref.at[peer_idx], dst_ref=gsmall_ref.at[peer_idx], send_sem=send_sems.at[s],
                    recv_sem=recv_sems.at[s], device_id=peer, device_id_type=pl.DeviceIdType.MESH))
        for cp in copies:
            cp.start()
        for cp in waits:
            cp.wait_recv()
        for cp in copies:
            cp.wait_send()
        if has_small:
            own.wait()

    out_shape = [jax.ShapeDtypeStruct((N_CHIPS,) + b.shape[1:], b.dtype) for b in bigs]
    args = list(bigs)
    if has_small:
        out_shape.append(jax.ShapeDtypeStruct(small.shape, small.dtype))
        args.append(small)
    return pl.pallas_call(
        body,
        in_specs=[_ANY] * len(args),
        out_specs=[_ANY] * len(out_shape),
        out_shape=out_shape,
        scratch_shapes=[pltpu.SemaphoreType.DMA((n_sems,)), pltpu.SemaphoreType.DMA((n_sems,)),
                        pltpu.SemaphoreType.DMA((1,))],
        name=name,
    )(*args)


def pair_sum(bigs, from_sibling, core, name):
    n = len(bigs)

    def body(core_ref, *refs):
        del core_ref
        for mine, sib, out in zip(refs[:n], refs[n:2 * n], refs[2 * n:]):
            out[...] = (mine[...].astype(F32) + sib[...].astype(F32)).astype(out.dtype)

    def slot(shape, picked):
        zeros = (0,) * (len(shape) - 1)
        if picked:
            return pl.BlockSpec((None,) + shape[1:], lambda q, core_ref: (2 * q + core_ref[0],) + zeros)
        return pl.BlockSpec((None,) + shape[1:], lambda q, core_ref: (q,) + zeros)

    grid_spec = pltpu.PrefetchScalarGridSpec(
        num_scalar_prefetch=1,
        grid=(N_CHIPS,),
        in_specs=[slot(b.shape, True) for b in bigs] + [slot(s.shape, False) for s in from_sibling],
        out_specs=[slot(s.shape, False) for s in from_sibling],
    )
    return pl.pallas_call(
        body,
        grid_spec=grid_spec,
        out_shape=[jax.ShapeDtypeStruct(s.shape, s.dtype) for s in from_sibling],
        compiler_params=_params("arbitrary"),
        name=name,
    )(core, *bigs, *from_sibling)


def exchange_chips(partials, name):
    n = len(partials)

    def body(*refs):
        part_refs, got_refs = refs[:n], refs[n:2 * n]
        send_sems, recv_sems, local_sems = refs[2 * n:]
        x, y, c, me = _mesh_position()
        mine = 2 * x + y
        own = [pltpu.make_async_copy(part_refs[b].at[mine], got_refs[b].at[mine], local_sems.at[b])
               for b in range(n)]
        for cp in own:
            cp.start()
        sends, recvs = [], []
        for q, (px, py) in enumerate(_other_chips(x, y)):
            theirs = 2 * px + py
            for b in range(n):
                s = q * n + b
                sends.append(pltpu.make_async_remote_copy(
                    src_ref=part_refs[b].at[theirs], dst_ref=got_refs[b].at[mine], send_sem=send_sems.at[s],
                    recv_sem=recv_sems.at[s], device_id=(px, py, c), device_id_type=pl.DeviceIdType.MESH))
                recvs.append(pltpu.make_async_remote_copy(
                    src_ref=part_refs[b].at[theirs], dst_ref=got_refs[b].at[theirs], send_sem=send_sems.at[s],
                    recv_sem=recv_sems.at[s], device_id=(px, py, c), device_id_type=pl.DeviceIdType.MESH))
        for cp in sends:
            cp.start()
        for cp in recvs:
            cp.wait_recv()
        for cp in sends:
            cp.wait_send()
        for cp in own:
            cp.wait()

    n_sems = (N_CHIPS - 1) * n
    return pl.pallas_call(
        body,
        in_specs=[_ANY] * n,
        out_specs=[_ANY] * n,
        out_shape=[jax.ShapeDtypeStruct(p.shape, p.dtype) for p in partials],
        scratch_shapes=[pltpu.SemaphoreType.DMA((n_sems,)), pltpu.SemaphoreType.DMA((n_sems,)),
                        pltpu.SemaphoreType.DMA((n,))],
        name=name,
    )(*partials)


def _adamw_math(g, w, m, v):
    m_new = ADAM_B1 * m + (1.0 - ADAM_B1) * g
    v_new = ADAM_B2 * v + (1.0 - ADAM_B2) * (g * g)
    m_hat = m_new / (1.0 - ADAM_B1 ** ADAM_STEP)
    v_hat = v_new / (1.0 - ADAM_B2 ** ADAM_STEP)
    delta = -ADAM_LR * (m_hat / (jnp.sqrt(v_hat) + ADAM_EPS) + ADAM_WD * w)
    return delta, m_new, v_new


def _chip_total(p_ref):
    g = p_ref[0].astype(F32)
    for q in range(1, N_CHIPS):
        g = g + p_ref[q].astype(F32)
    return g


def chip_sum(parts, name):
    depth = len(parts)

    def body(*refs):
        for l in range(depth):
            refs[depth][l] = _chip_total(refs[l])

    return pl.pallas_call(
        body,
        out_shape=jax.ShapeDtypeStruct((depth,) + parts[0].shape[1:], F32),
        compiler_params=pltpu.CompilerParams(vmem_limit_bytes=VMEM_LIMIT),
        name=name,
    )(*parts)


def adamw_shard(parts, w, m, v, name, grad=None):
    depth = w.shape[0]
    n_in = 1 if grad is not None else depth

    def body(*refs):
        w_ref, m_ref, v_ref, g_ref, d_ref, mo_ref, vo_ref = refs[n_in:]
        for l in range(depth):
            g = refs[0][l] if grad is not None else _chip_total(refs[l])
            delta, m_new, v_new = _adamw_math(g, w_ref[l], m_ref[l], v_ref[l])
            g_ref[l] = g
            d_ref[l] = delta
            mo_ref[l] = m_new
            vo_ref[l] = v_new

    return pl.pallas_call(
        body,
        out_shape=[jax.ShapeDtypeStruct(w.shape, F32)] * 4,
        compiler_params=pltpu.CompilerParams(vmem_limit_bytes=VMEM_LIMIT),
        name=name,
    )(*([grad] if grad is not None else parts), w, m, v)


def adamw_packed(parts, w, m, v, name):
    n_slots = parts.shape[0]

    def body(p_ref, w_ref, m_ref, v_ref, g_ref, d_ref, mo_ref, vo_ref):
        g = p_ref[0]
        for s in range(1, n_slots):
            g = g + p_ref[s]
        delta, m_new, v_new = _adamw_math(g, w_ref[...], m_ref[...], v_ref[...])
        g_ref[...] = g
        d_ref[...] = delta
        mo_ref[...] = m_new
        vo_ref[...] = v_new

    return pl.pallas_call(
        body,
        out_shape=[jax.ShapeDtypeStruct(w.shape, F32)] * 4,
        compiler_params=pltpu.CompilerParams(vmem_limit_bytes=VMEM_LIMIT),
        name=name,
    )(parts, w, m, v)


SHARDED = ("w_in", "w_q_b", "w_kv_b", "w_out", "w_gate", "w_up", "w_down")
TRANSPOSED = ("w_in", "w_gate", "w_up")
REPLICATED = ("attn_norm_g", "w_pool", "pool_scale", "q_a_norm_g", "kv_a_norm_g", "q_norm_g", "k_norm_g",
              "ffn_norm_g")


def _pack_flat(arrays, rows):
    flat = jnp.concatenate([a.reshape(-1) for a in arrays])
    return jnp.pad(flat, (0, rows * LANES - flat.shape[0])).reshape(rows, LANES)


def _unpack_flat(packed, shapes):
    flat, out, at = packed.reshape(-1), [], 0
    for shp in shapes:
        n = 1
        for d in shp:
            n *= d
        out.append(flat[at:at + n].reshape(shp))
        at += n
    return out


def _rope_lane_tables(length):
    inv = 1.0 / (ROPE_THETA ** (jnp.arange(0, QK_ROPE_DIM, 2, dtype=F32) / QK_ROPE_DIM))
    ang = jnp.arange(length, dtype=F32)[:, None] * inv[None, :]
    cos, sin = jnp.cos(ang), jnp.sin(ang)
    ones = jnp.ones((length, QK_NOPE_DIM), F32)
    zeros = jnp.zeros((length, QK_NOPE_DIM), F32)
    tail = HEAD_PAD - QK_HEAD_DIM
    cosf = jnp.concatenate([ones, cos, cos, ones[:, :tail]], axis=1)
    sins = jnp.concatenate([zeros, -sin, sin, zeros[:, :tail]], axis=1)
    return cosf, sins


def _pad_lanes(vec, at, width=HEAD_PAD):
    return jnp.pad(vec, (at, width - at - vec.shape[0])).reshape(1, width)


def kernel(x, meta_tokens, attn_norm_g, w_in, w_pool, pool_scale, q_a_norm_g, w_q_b, kv_a_norm_g, w_kv_b, q_norm_g, k_norm_g, w_out, ffn_norm_g, w_gate, w_up, w_down, loss_target, m_meta_tokens, m_attn_norm_g, m_w_in, m_w_pool, m_pool_scale, m_q_a_norm_g, m_w_q_b, m_kv_a_norm_g, m_w_kv_b, m_q_norm_g, m_k_norm_g, m_w_out, m_ffn_norm_g, m_w_gate, m_w_up, m_w_down, v_meta_tokens, v_attn_norm_g, v_w_in, v_w_pool, v_pool_scale, v_q_a_norm_g, v_w_q_b, v_kv_a_norm_g, v_w_kv_b, v_q_norm_g, v_k_norm_g, v_w_out, v_ffn_norm_g, v_w_gate, v_w_up, v_w_down):
    weights = dict(meta_tokens=meta_tokens, attn_norm_g=attn_norm_g, w_in=w_in, w_pool=w_pool, pool_scale=pool_scale,
                   q_a_norm_g=q_a_norm_g, w_q_b=w_q_b, kv_a_norm_g=kv_a_norm_g, w_kv_b=w_kv_b, q_norm_g=q_norm_g,
                   k_norm_g=k_norm_g, w_out=w_out, ffn_norm_g=ffn_norm_g, w_gate=w_gate, w_up=w_up, w_down=w_down)
    mom1 = dict(meta_tokens=m_meta_tokens, attn_norm_g=m_attn_norm_g, w_in=m_w_in, w_pool=m_w_pool,
                pool_scale=m_pool_scale, q_a_norm_g=m_q_a_norm_g, w_q_b=m_w_q_b, kv_a_norm_g=m_kv_a_norm_g,
                w_kv_b=m_w_kv_b, q_norm_g=m_q_norm_g, k_norm_g=m_k_norm_g, w_out=m_w_out, ffn_norm_g=m_ffn_norm_g,
                w_gate=m_w_gate, w_up=m_w_up, w_down=m_w_down)
    mom2 = dict(meta_tokens=v_meta_tokens, attn_norm_g=v_attn_norm_g, w_in=v_w_in, w_pool=v_w_pool,
                pool_scale=v_pool_scale, q_a_norm_g=v_q_a_norm_g, w_q_b=v_w_q_b, kv_a_norm_g=v_kv_a_norm_g,
                w_kv_b=v_w_kv_b, q_norm_g=v_q_norm_g, k_norm_g=v_k_norm_g, w_out=v_w_out, ffn_norm_g=v_ffn_norm_g,
                w_gate=v_w_gate, w_up=v_w_up, w_down=v_w_down)
    order = ("meta_tokens", "attn_norm_g", "w_in", "w_pool", "pool_scale", "q_a_norm_g", "w_q_b", "kv_a_norm_g",
             "w_kv_b", "q_norm_g", "k_norm_g", "w_out", "ffn_norm_g", "w_gate", "w_up", "w_down")
    depth = w_in.shape[0]
    seq = x.shape[1]
    length = N_META + seq
    lp = -(-length // ROW_TILE) * ROW_TILE
    in_cols = w_in.shape[2]

    local = {n: (jnp.swapaxes(weights[n], 1, 2) if n in TRANSPOSED else weights[n]) for n in SHARDED}
    shards = cast_bf16([local[n] for n in SHARDED], "cast_weights")
    gathered = all_gather(list(shards) + [meta_tokens], "all_gather")
    g8 = dict(zip(SHARDED, gathered[:-1]))
    meta_full = jnp.transpose(gathered[-1], (1, 0, 2)).reshape(N_META, D_MODEL)

    s1, s2, s3 = POOL_WIDTH, POOL_WIDTH + Q_LORA_RANK, POOL_WIDTH + Q_LORA_RANK + KV_LORA_RANK
    w_in_t = jnp.swapaxes(g8["w_in"], 0, 1).reshape(depth, N_DEV * in_cols, D_MODEL)
    zpad = lambda n: jnp.zeros((depth, n, D_MODEL), BF16)
    w_in_p = jnp.concatenate([w_in_t[:, :s3], zpad(QK_NOPE_DIM), w_in_t[:, s3:],
                              zpad(HEAD_PAD - QK_HEAD_DIM)], axis=1)
    w_pool_b = w_pool.astype(BF16)
    vw = MLA_HEADS * V_HEAD_DIM

    cosf, sins = _rope_lane_tables(lp)
    row = lambda a, l: a[l].reshape(1, -1)

    h = jnp.concatenate([meta_full, x[0], jnp.zeros((lp - length, D_MODEL), F32)], axis=0)
    target = jnp.pad(loss_target[0], ((N_META, lp - length), (0, 0)))
    saved = []
    for l in range(depth):
        gq = _pad_lanes(q_norm_g[l], 0)
        gkn = _pad_lanes(k_norm_g[l, :QK_NOPE_DIM], 0)
        gkr = _pad_lanes(k_norm_g[l, QK_NOPE_DIM:], QK_NOPE_DIM)
        a, u, c_q, c_kv, kr = norm_mm(
            h, row(attn_norm_g, l), w_in_p, l,
            [(0, s1), (s1, Q_LORA_RANK), (s2, KV_LORA_RANK), (s3, HEAD_PAD)], [F32] * 4, "in_proj")
        cat = pool_fwd(u, w_pool_b[l], row(pool_scale, l), "pool_fwd")
        qn, q = norm_mm_heads(c_q, row(q_a_norm_g, l), g8["w_q_b"], l, "q_proj")
        kvn, kv = norm_mm_heads(c_kv, row(kv_a_norm_g, l), g8["w_kv_b"], l, "kv_proj")
        qp, kp, v = qk_prep_fwd(q, kv, kr, cosf, sins, gq, gkn, gkr, "qk_prep_fwd")
        o, lse, cat = attn_fwd(qp, kp, v, cat, "attn_fwd")
        h_mid, g = mm_res(cat, g8["w_out"], h, l, "out_proj", gamma=row(ffn_norm_g, l))
        gate, up, act = ffn_up(g, g8["w_gate"], g8["w_up"], l, "ffn_up")
        h_next = mm_res(act, g8["w_down"], h_mid, l, "ffn_down")[0]
        saved.append(dict(h=h, a=a, u=u, c_q=c_q, c_kv=c_kv, kr=kr, qn=qn, q=q, kvn=kvn, kv=kv, v=v, qp=qp, kp=kp,
                          o=o, lse=lse, cat=cat, h_mid=h_mid, g=g, gate=gate, up=up, act=act,
                          gq=gq, gkn=gkn, gkr=gkr))
        h = h_next

    dh, sq = loss_head(h, target, seq, "loss_head")
    loss = lax.psum(0.5 / D_MODEL * jnp.sum(sq), ("x", "y", "c"))

    core = lax.axis_index("c").astype(jnp.int32).reshape(1)
    small_grads = {n: [None] * depth for n in REPLICATED}
    got = [None] * depth
    for l in reversed(range(depth)):
        s = saved[l]
        slots = {}
        dgate, dup = ffn_bwd_act(dh, g8["w_down"], l, s["gate"], s["up"], "ffn_bwd_act")
        slot_of = lambda n, full: full.reshape(g8[n].shape[:1] + g8[n].shape[2:])
        slots["w_down"] = slot_of("w_down", mm_tn(s["act"], dh, "dw_down", out_dtype=BF16))
        slots["w_gate"] = slot_of("w_gate", mm_tn(dgate, s["g"], "dw_gate", out_dtype=BF16))
        slots["w_up"] = slot_of("w_up", mm_tn(dup, s["g"], "dw_up", out_dtype=BF16))
        dh_mid, dg_ffn = mm_nt_normbwd([("gathered", dgate, g8["w_gate"]), ("gathered", dup, g8["w_up"])],
                                       s["h_mid"], row(ffn_norm_g, l), dh, l, "ffn_bwd_in")
        small_grads["ffn_norm_g"][l] = dg_ffn[0]
        slots["w_out"] = slot_of("w_out", mm_tn(s["cat"], dh_mid, "dw_out", out_dtype=BF16))
        dy_pool, do = mm_nt(dh_mid, g8["w_out"], l, [(0, POOL_WIDTH), (POOL_WIDTH, vw)], "out_proj_bwd")
        du, dw_pool, dscale = pool_bwd(s["u"], dy_pool, w_pool_b[l], row(pool_scale, l), "pool_bwd")
        small_grads["w_pool"][l] = dw_pool
        small_grads["pool_scale"][l] = dscale[0]
        delta = attn_delta(do, s["o"], "attn_delta")
        dqp, dkp, dv = attn_bwd(s["qp"], s["kp"], s["v"], do, s["lse"], delta, "attn_bwd")
        dq, dkv, dkr, dgq, dgkn, dgkr = qk_prep_bwd(dqp, dkp, dv, s["q"], s["kv"], s["kr"], cosf, sins,
                                                    s["gq"], s["gkn"], s["gkr"], "qk_prep_bwd")
        small_grads["q_norm_g"][l] = dgq[0, :QK_HEAD_DIM]
        small_grads["k_norm_g"][l] = jnp.concatenate([dgkn[0, :QK_NOPE_DIM], dgkr[0, QK_NOPE_DIM:QK_HEAD_DIM]])
        slots["w_q_b"] = mm_tn(s["qn"], dq, "dw_q", out_dtype=BF16, b_head=QK_HEAD_DIM)
        dc_q, dg_qa = mm_nt_normbwd([("heads", dq, g8["w_q_b"])], s["c_q"], row(q_a_norm_g, l), None, l,
                                    "q_proj_bwd")
        small_grads["q_a_norm_g"][l] = dg_qa[0]
        slots["w_kv_b"] = mm_tn(s["kvn"], dkv, "dw_kv", out_dtype=BF16, b_head=HEAD_PAD)
        dc_kv, dg_kva = mm_nt_normbwd([("heads", dkv, g8["w_kv_b"])], s["c_kv"], row(kv_a_norm_g, l), None, l,
                                      "kv_proj_bwd")
        small_grads["kv_a_norm_g"][l] = dg_kva[0]
        dw_in_t = jnp.concatenate(
            [mm_tn(du, s["a"], "dw_in_pool", out_dtype=BF16), mm_tn(dc_q, s["a"], "dw_in_q", out_dtype=BF16),
             mm_tn(dc_kv, s["a"], "dw_in_kv", out_dtype=BF16),
             mm_tn(dkr, s["a"], "dw_in_rope", out_dtype=BF16)[QK_NOPE_DIM:QK_HEAD_DIM]], axis=0)
        slots["w_in"] = slot_of("w_in", dw_in_t)
        dh, dg_attn = mm_nt_normbwd(
            [("rows", du, w_in_p, 0), ("rows", dc_q, w_in_p, s1), ("rows", dc_kv, w_in_p, s2),
             ("rows", dkr, w_in_p, s3)],
            s["h"], row(attn_norm_g, l), dh_mid, l, "in_proj_bwd")
        small_grads["attn_norm_g"][l] = dg_attn[0]

        bigs = [slots[n] for n in SHARDED]
        if l > 0:
            from_sibling = exchange_pair(bigs, None, "grad_exchange_pair")
        else:
            rep_shapes = [weights[n].shape for n in REPLICATED]
            rep_count = sum(int(jnp.size(weights[n])) for n in REPLICATED)
            rep_rows = -(-rep_count // (8 * LANES)) * 8
            rep_packed = _pack_flat([jnp.stack(small_grads[n], axis=0) for n in REPLICATED], rep_rows)
            meta_slots = jnp.transpose(dh[:N_META].reshape(N_META, N_DEV, LANES), (1, 0, 2))
            small_slots = jnp.concatenate(
                [meta_slots, jnp.broadcast_to(rep_packed[None], (N_DEV, rep_rows, LANES))], axis=1)
            *from_sibling, got_small = exchange_pair(bigs, small_slots, "grad_exchange_pair_small")
        partial = pair_sum(bigs, list(from_sibling), core, "grad_pair_sum")
        got[l] = dict(zip(SHARDED, exchange_chips(list(partial), "grad_exchange_chips")))

    grad_x = dh[N_META:length][None]

    per = [{} for _ in range(4)]
    for n in SHARDED:
        parts = [got[l][n] for l in range(depth)]
        if n in TRANSPOSED:
            grad = jnp.swapaxes(chip_sum(parts, "chip_sum_" + n), 1, 2)
            outs = adamw_shard(None, weights[n], mom1[n], mom2[n], "adamw_" + n, grad=grad)
        else:
            outs = adamw_shard(parts, weights[n], mom1[n], mom2[n], "adamw_" + n)
        for k in range(4):
            per[k][n] = outs[k]
    ps = lambda src: jnp.concatenate(
        [src["meta_tokens"], _pack_flat([src[n] for n in REPLICATED], rep_rows)], axis=0)
    small_out = adamw_packed(got_small, ps(weights), ps(mom1), ps(mom2), "adamw_small")
    for k in range(4):
        per[k]["meta_tokens"] = small_out[k][:N_META]
        per[k].update(zip(REPLICATED, _unpack_flat(small_out[k][N_META:], rep_shapes)))
    return (loss, grad_x, *[per[0][n] for n in order], *[per[1][n] for n in order],
            *[per[2][n] for n in order], *[per[3][n] for n in order])
```

```python
import functools

import jax
import jax.numpy as jnp
from jax import lax
from jax.experimental import pallas as pl
from jax.experimental.pallas import tpu as pltpu

F32 = jnp.float32
BF16 = jnp.bfloat16

D_MODEL = 1024
N_META = 16
POOL_WIDTH = 512
POOL_WINDOWS = (2, 4, 8, 16)
POOL_GROUP_DIM = 128
POOL_HALO = 16
MLA_HEADS = 8
QK_NOPE_DIM = 64
QK_ROPE_DIM = 32
QK_HEAD_DIM = 96
V_HEAD_DIM = 64
HEAD_PAD = 128
Q_LORA_RANK = 384
KV_LORA_RANK = 256
ROPE_THETA = 10000.0
RMS_EPS = 1e-6
ATTN_SCALE = QK_HEAD_DIM ** -0.5

ADAM_LR = 0.001
ADAM_B1 = 0.9
ADAM_B2 = 0.999
ADAM_EPS = 1e-08
ADAM_WD = 0.01
ADAM_STEP = 10

N_DEV = 8
N_CHIPS = 4
LANES = 128
ROW_TILE = 384
LONG_TILE = 1056
VMEM_LIMIT = 56 * 1024 * 1024


def _params(*sem):
    return pltpu.CompilerParams(dimension_semantics=sem, vmem_limit_bytes=VMEM_LIMIT)


def _row_spec(tile, width):
    return pl.BlockSpec((tile, width), lambda i: (i, 0))


def _const_spec(shape):
    return pl.BlockSpec(shape, lambda i: tuple(0 for _ in shape))


def _layer_spec(w, layer):
    return pl.BlockSpec((None,) + w.shape[1:], lambda *_: (layer, 0, 0))


def _gathered_spec(w8, layer):
    return pl.BlockSpec((N_DEV, None) + w8.shape[2:], lambda *_: (0, layer, 0, 0))


def _nt(a, b):
    return lax.dot_general(a, b, (((1,), (1,)), ((), ())), preferred_element_type=F32)


def _tn(a, b):
    return lax.dot_general(a, b, (((0,), (0,)), ((), ())), preferred_element_type=F32)


def _nn(a, b):
    return jnp.dot(a, b, preferred_element_type=F32)


def _silu(g):
    return g * (1.0 / (1.0 + jnp.exp(-g)))


def _rms(xf):
    return lax.rsqrt(jnp.mean(xf * xf, axis=-1, keepdims=True) + RMS_EPS)


def norm_mm(x, gamma, wt, layer, splits, dtypes, name):
    L, K = x.shape
    tm = ROW_TILE

    def body(x_ref, g_ref, w_ref, a_ref, *z_refs):
        xf = x_ref[...]
        a = ((xf * _rms(xf)) * g_ref[...]).astype(BF16)
        a_ref[...] = a
        z = _nt(a, w_ref[...])
        for (s, n), zr in zip(splits, z_refs):
            zr[...] = z[:, s:s + n].astype(zr.dtype)

    widths = [n for _, n in splits]
    return pl.pallas_call(
        body,
        grid=(L // tm,),
        in_specs=[_row_spec(tm, K), _const_spec((1, K)), _layer_spec(wt, layer)],
        out_specs=[_row_spec(tm, K)] + [_row_spec(tm, n) for n in widths],
        out_shape=[jax.ShapeDtypeStruct((L, K), BF16)]
        + [jax.ShapeDtypeStruct((L, n), dt) for n, dt in zip(widths, dtypes)],
        compiler_params=_params("arbitrary"),
        name=name,
    )(x, gamma, wt)


def norm_mm_heads(x, gamma, w8, layer, name):
    L, K = x.shape
    hw = w8.shape[-1]
    tm = ROW_TILE

    def body(x_ref, g_ref, w_ref, a_ref, z_ref):
        xf = x_ref[...]
        a = ((xf * _rms(xf)) * g_ref[...]).astype(BF16)
        a_ref[...] = a
        if hw < HEAD_PAD:
            z_ref[...] = jnp.zeros_like(z_ref)
        for j in range(MLA_HEADS):
            z_ref[:, j * HEAD_PAD:j * HEAD_PAD + hw] = _nn(a, w_ref[j])

    return pl.pallas_call(
        body,
        grid=(L // tm,),
        in_specs=[_row_spec(tm, K), _const_spec((1, K)), _gathered_spec(w8, layer)],
        out_specs=[_row_spec(tm, K), _row_spec(tm, MLA_HEADS * HEAD_PAD)],
        out_shape=[jax.ShapeDtypeStruct((L, K), BF16), jax.ShapeDtypeStruct((L, MLA_HEADS * HEAD_PAD), F32)],
        compiler_params=_params("arbitrary"),
        name=name,
    )(x, gamma, w8)


FF_GROUP = 4


def _ff_spec(w8, layer):
    return pl.BlockSpec((FF_GROUP, None) + w8.shape[2:], lambda j, i: (j, layer, 0, 0))


def ffn_up(g, w_gate8, w_up8, layer, name):
    L, K = g.shape
    fb = w_gate8.shape[-2]
    tm, tf = ROW_TILE, FF_GROUP * fb

    def body(a_ref, wg_ref, wu_ref, gate_ref, up_ref, act_ref):
        a = a_ref[...]
        gate = _nt(a, wg_ref[...].reshape(tf, K))
        up = _nt(a, wu_ref[...].reshape(tf, K))
        gate_ref[...] = gate
        up_ref[...] = up
        act_ref[...] = (_silu(gate) * up).astype(BF16)

    tile = pl.BlockSpec((tm, tf), lambda j, i: (i, j))
    F = N_DEV * fb
    return pl.pallas_call(
        body,
        grid=(N_DEV // FF_GROUP, L // tm),
        in_specs=[pl.BlockSpec((tm, K), lambda j, i: (i, 0)), _ff_spec(w_gate8, layer), _ff_spec(w_up8, layer)],
        out_specs=[tile, tile, tile],
        out_shape=[
            jax.ShapeDtypeStruct((L, F), F32),
            jax.ShapeDtypeStruct((L, F), F32),
            jax.ShapeDtypeStruct((L, F), BF16),
        ],
        compiler_params=_params("arbitrary", "arbitrary"),
        name=name,
    )(g, w_gate8, w_up8)


def mm_res(a, w8, res, layer, name, gamma=None):
    L = a.shape[0]
    kb, N = w8.shape[-2:]
    tm = ROW_TILE
    normed = gamma is not None

    def body(a_ref, w_ref, r_ref, *rest):
        out = r_ref[...] + _nn(a_ref[...], w_ref[...].reshape(N_DEV * kb, N))
        if normed:
            g_ref, o_ref, n_ref = rest
            n_ref[...] = ((out * _rms(out)) * g_ref[...]).astype(BF16)
        else:
            (o_ref,) = rest
        o_ref[...] = out

    in_specs = [_row_spec(tm, N_DEV * kb), _gathered_spec(w8, layer), _row_spec(tm, N)]
    out_specs = [_row_spec(tm, N)]
    out_shape = [jax.ShapeDtypeStruct((L, N), F32)]
    args = [a, w8, res]
    if normed:
        in_specs.append(_const_spec((1, N)))
        out_specs.append(_row_spec(tm, N))
        out_shape.append(jax.ShapeDtypeStruct((L, N), BF16))
        args.append(gamma)
    return pl.pallas_call(
        body,
        grid=(L // tm,),
        in_specs=in_specs,
        out_specs=out_specs,
        out_shape=out_shape,
        compiler_params=_params("arbitrary"),
        name=name,
    )(*args)


def mm_nt(dz, w8, layer, splits, name):
    L, N = dz.shape
    kb = w8.shape[-2]
    tm = ROW_TILE

    def body(dz_ref, w_ref, *o_refs):
        d = _nt(dz_ref[...].astype(BF16), w_ref[...].reshape(N_DEV * kb, N))
        for (s, n), o in zip(splits, o_refs):
            o[...] = d[:, s:s + n]

    return pl.pallas_call(
        body,
        grid=(L // tm,),
        in_specs=[_row_spec(tm, N), _gathered_spec(w8, layer)],
        out_specs=[_row_spec(tm, n) for _, n in splits],
        out_shape=[jax.ShapeDtypeStruct((L, n), F32) for _, n in splits],
        compiler_params=_params("arbitrary"),
        name=name,
    )(dz, w8)


def mm_nt_normbwd(terms, x, gamma, dres, layer, name):
    L, K = x.shape
    tm = ROW_TILE
    n_terms = len(terms)
    has_res = dres is not None
    weights = []
    for t in terms:
        if not any(t[2] is u for u in weights):
            weights.append(t[2])
    which = [[t[2] is u for u in weights].index(True) for t in terms]
    n_in = n_terms + len(weights)

    def body(*refs):
        dz_refs = refs[:n_terms]
        w_refs = [refs[n_terms + n] for n in which]
        x_ref, g_ref = refs[n_in], refs[n_in + 1]
        pos = n_in + 2
        r_ref = refs[pos] if has_res else None
        dx_ref, dg_ref = refs[pos + has_res], refs[pos + has_res + 1]
        da = None
        for t, dz_ref, w_ref in zip(terms, dz_refs, w_refs):
            if t[0] == "rows":
                n, at = t[1].shape[1], t[3]
                parts = [_nn(dz_ref[...].astype(BF16), w_ref[at:at + n, :])]
            elif t[0] == "heads":
                hw = t[2].shape[-1]
                parts = [_nt(dz_ref[:, j * HEAD_PAD:j * HEAD_PAD + hw].astype(BF16), w_ref[j])
                         for j in range(MLA_HEADS)]
            else:
                nb = t[2].shape[-2]
                parts = [_nn(dz_ref[...].astype(BF16), w_ref[...].reshape(N_DEV * nb, K))]
            for p in parts:
                da = p if da is None else da + p
        xf = x_ref[...]
        r = _rms(xf)
        xh = xf * r

        @pl.when(pl.program_id(0) == 0)
        def _():
            dg_ref[...] = jnp.zeros_like(dg_ref)

        dg_ref[...] += jnp.sum(da * xh, axis=0, keepdims=True)
        dxh = da * g_ref[...]
        dx = r * (dxh - xh * jnp.mean(dxh * xh, axis=-1, keepdims=True))
        if has_res:
            dx = dx + r_ref[...]
        dx_ref[...] = dx

    in_specs = [_row_spec(tm, t[1].shape[1]) for t in terms]
    for w in weights:
        in_specs.append(_layer_spec(w, layer) if w.ndim == 3 else _gathered_spec(w, layer))
    in_specs += [_row_spec(tm, K), _const_spec((1, K))]
    args = [t[1] for t in terms] + weights + [x, gamma]
    if has_res:
        in_specs.append(_row_spec(tm, K))
        args.append(dres)
    return pl.pallas_call(
        body,
        grid=(L // tm,),
        in_specs=in_specs,
        out_specs=[_row_spec(tm, K), _const_spec((1, K))],
        out_shape=[jax.ShapeDtypeStruct((L, K), F32), jax.ShapeDtypeStruct((1, K), F32)],
        compiler_params=_params("arbitrary"),
        name=name,
    )(*args)


MAX_OUT_ROWS = 1408


def mm_tn(a, b, name, out_dtype=F32):
    L, K = a.shape
    N = b.shape[-1]
    tk = MAX_OUT_ROWS if (K > MAX_OUT_ROWS and K % MAX_OUT_ROWS == 0) else K
    tl = LONG_TILE if L % LONG_TILE == 0 else ROW_TILE
    n_l = L // tl

    def body(a_ref, b_ref, o_ref, acc):
        l = pl.program_id(1)

        @pl.when(l == 0)
        def _():
            acc[...] = jnp.zeros_like(acc)

        acc[...] += _tn(a_ref[...].astype(BF16), b_ref[...].astype(BF16))

        @pl.when(l == n_l - 1)
        def _():
            o_ref[...] = acc[...].astype(o_ref.dtype)

    return pl.pallas_call(
        body,
        grid=(K // tk, n_l),
        in_specs=[pl.BlockSpec((tl, tk), lambda j, l: (l, j)), pl.BlockSpec((tl, N), lambda j, l: (l, 0))],
        out_specs=pl.BlockSpec((tk, N), lambda j, l: (j, 0)),
        out_shape=jax.ShapeDtypeStruct((K, N), out_dtype),
        scratch_shapes=[pltpu.VMEM((tk, N), F32)],
        compiler_params=_params("arbitrary", "arbitrary"),
        name=name,
    )(a, b)


def ffn_bwd_act(dh, w_down8, layer, gate, up, name):
    L, K = dh.shape
    fb = w_down8.shape[-2]
    tm, tf = ROW_TILE, FF_GROUP * fb

    def body(dh_ref, w_ref, gate_ref, up_ref, dgate_ref, dup_ref):
        dact = _nt(dh_ref[...].astype(BF16), w_ref[...].reshape(tf, K))
        g = gate_ref[...]
        sig = 1.0 / (1.0 + jnp.exp(-g))
        dup_ref[...] = (dact * (g * sig)).astype(BF16)
        dgate_ref[...] = (dact * up_ref[...] * (sig * (1.0 + g * (1.0 - sig)))).astype(BF16)

    tile = pl.BlockSpec((tm, tf), lambda j, i: (i, j))
    F = N_DEV * fb
    return pl.pallas_call(
        body,
        grid=(N_DEV // FF_GROUP, L // tm),
        in_specs=[pl.BlockSpec((tm, K), lambda j, i: (i, 0)), _ff_spec(w_down8, layer), tile, tile],
        out_specs=[tile, tile],
        out_shape=[jax.ShapeDtypeStruct((L, F), BF16), jax.ShapeDtypeStruct((L, F), BF16)],
        compiler_params=_params("arbitrary", "arbitrary"),
        name=name,
    )(dh, w_down8, gate, up)


def _pool_residual(scr, lo, tm, g, w, t):
    cols = slice(g * POOL_GROUP_DIM, (g + 1) * POOL_GROUP_DIM)
    cur = scr[lo:lo + tm, cols]
    s = cur
    for k in range(1, w):
        s = s + scr[lo - k:lo - k + tm, cols]
    cnt = jnp.minimum(t + 1, w).astype(F32)
    return s / cnt - cur


def pool_fwd(u, w_pool, scale, name):
    L, C = u.shape
    tm, halo = ROW_TILE, POOL_HALO

    def body(u_ref, halo_ref, w_ref, s_ref, y_ref, scr):
        i = pl.program_id(0)
        scr[0:halo, :] = jnp.where(i > 0, halo_ref[...], 0.0)
        scr[halo:halo + tm, :] = u_ref[...]
        t = i * tm + lax.broadcasted_iota(jnp.int32, (tm, POOL_GROUP_DIM), 0)
        for g, w in enumerate(POOL_WINDOWS):
            cols = slice(g * POOL_GROUP_DIM, (g + 1) * POOL_GROUP_DIM)
            p = _pool_residual(scr, halo, tm, g, w, t)
            y = _nn(p.astype(BF16), w_ref[g]) * s_ref[:, cols]
            y_ref[:, cols] = y.astype(y_ref.dtype)

    return pl.pallas_call(
        body,
        grid=(L // tm,),
        in_specs=[
            _row_spec(tm, C),
            pl.BlockSpec((halo, C), lambda i: (jnp.maximum(i * (tm // halo) - 1, 0), 0)),
            _const_spec(w_pool.shape),
            _const_spec((1, C)),
        ],
        out_specs=_row_spec(tm, C),
        out_shape=jax.ShapeDtypeStruct((L, 2 * C), BF16),
        scratch_shapes=[pltpu.VMEM((tm + halo, C), F32)],
        compiler_params=_params("arbitrary"),
        name=name,
    )(u, u, w_pool, scale)


def pool_bwd(u, dy, w_pool, scale, name):
    L, C = u.shape
    tm, halo = ROW_TILE, POOL_HALO
    n_tiles = L // tm
    last_halo = L // halo - 1

    def body(u_ref, uh_ref, dy_ref, dyh_ref, w_ref, s_ref, du_ref, dw_ref, ds_ref, scr_u, scr_q):
        i = pl.program_id(0)

        @pl.when(i == 0)
        def _():
            dw_ref[...] = jnp.zeros_like(dw_ref)
            ds_ref[...] = jnp.zeros_like(ds_ref)

        scr_u[0:halo, :] = jnp.where(i > 0, uh_ref[...], 0.0)
        scr_u[halo:halo + tm, :] = u_ref[...]
        t = i * tm + lax.broadcasted_iota(jnp.int32, (tm, POOL_GROUP_DIM), 0)
        th = (i + 1) * tm + lax.broadcasted_iota(jnp.int32, (halo, POOL_GROUP_DIM), 0)
        for g, w in enumerate(POOL_WINDOWS):
            cols = slice(g * POOL_GROUP_DIM, (g + 1) * POOL_GROUP_DIM)
            p = _pool_residual(scr_u, halo, tm, g, w, t).astype(BF16)
            wg = w_ref[g]
            sc = s_ref[:, cols]
            dy = dy_ref[:, cols]
            ds_ref[:, cols] += jnp.sum(dy * _nn(p, wg), axis=0, keepdims=True)
            dys = (dy * sc).astype(BF16)
            dw_ref[g] += _tn(p, dys)
            dp = _nt(dys, wg)
            dyh = jnp.where(i < n_tiles - 1, dyh_ref[:, cols], 0.0)
            dph = _nt((dyh * sc).astype(BF16), wg)
            scr_q[0:tm, cols] = dp / jnp.minimum(t + 1, w).astype(F32)
            scr_q[tm:tm + halo, cols] = dph / jnp.minimum(th + 1, w).astype(F32)
            acc = scr_q[0:tm, cols]
            for k in range(1, w):
                acc = acc + scr_q[k:k + tm, cols]
            du_ref[:, cols] = acc - dp

    return pl.pallas_call(
        body,
        grid=(n_tiles,),
        in_specs=[
            _row_spec(tm, C),
            pl.BlockSpec((halo, C), lambda i: (jnp.maximum(i * (tm // halo) - 1, 0), 0)),
            _row_spec(tm, C),
            pl.BlockSpec((halo, C), lambda i: (jnp.minimum((i + 1) * (tm // halo), last_halo), 0)),
            _const_spec(w_pool.shape),
            _const_spec((1, C)),
        ],
        out_specs=[_row_spec(tm, C), _const_spec(w_pool.shape), _const_spec((1, C))],
        out_shape=[
            jax.ShapeDtypeStruct((L, C), F32),
            jax.ShapeDtypeStruct(w_pool.shape, F32),
            jax.ShapeDtypeStruct((1, C), F32),
        ],
        scratch_shapes=[pltpu.VMEM((tm + halo, C), F32), pltpu.VMEM((tm + halo, C), F32)],
        compiler_params=_params("arbitrary"),
        name=name,
    )(u, u, dy, dy, w_pool, scale)


def _head_masks(rows):
    lane = lax.broadcasted_iota(jnp.int32, (rows, HEAD_PAD), 1)
    return lane, lane < QK_NOPE_DIM, (lane >= QK_NOPE_DIM) & (lane < QK_HEAD_DIM)


def _rope_swap(x, lane):
    half = QK_ROPE_DIM // 2
    swapped = jnp.where(lane < QK_NOPE_DIM + half, pltpu.roll(x, HEAD_PAD - half, 1), pltpu.roll(x, half, 1))
    return jnp.where((lane >= QK_NOPE_DIM) & (lane < QK_HEAD_DIM), swapped, 0.0)


def _seg_mean(v, m_n, m_r):
    mn = jnp.sum(jnp.where(m_n, v, 0.0), axis=1, keepdims=True) * (1.0 / QK_NOPE_DIM)
    mr = jnp.sum(jnp.where(m_r, v, 0.0), axis=1, keepdims=True) * (1.0 / QK_ROPE_DIM)
    return jnp.where(m_n, mn, mr)


def qk_prep_fwd(q, kv, kr, cosf, sins, gq, gkn, gkr, name):
    L = q.shape[0]
    tm = ROW_TILE
    W = MLA_HEADS * HEAD_PAD

    def body(q_ref, kv_ref, kr_ref, c_ref, s_ref, gq_ref, gkn_ref, gkr_ref, qo_ref, ko_ref, vo_ref):
        lane, m_n, m_r = _head_masks(tm)
        cosf_, sins_ = c_ref[...], s_ref[...]
        kr_ = kr_ref[...]
        rk = lax.rsqrt(_seg_mean(kr_ * kr_, m_n, m_r) + RMS_EPS)
        krn = kr_ * rk * gkr_ref[...]
        krf = krn * cosf_ + _rope_swap(krn, lane) * sins_
        for h in range(MLA_HEADS):
            cols = slice(h * HEAD_PAD, (h + 1) * HEAD_PAD)
            qh = q_ref[:, cols]
            qn = qh * lax.rsqrt(_seg_mean(qh * qh, m_n, m_r) + RMS_EPS) * gq_ref[...]
            qo_ref[:, cols] = (qn * cosf_ + _rope_swap(qn, lane) * sins_).astype(BF16)
            kh = jnp.where(m_n, kv_ref[:, cols], 0.0)
            rn = lax.rsqrt(jnp.sum(kh * kh, axis=1, keepdims=True) * (1.0 / QK_NOPE_DIM) + RMS_EPS)
            ko_ref[:, cols] = (kh * rn * gkn_ref[...] + krf).astype(BF16)
        for p in range(MLA_HEADS // 2):
            even = kv_ref[:, 2 * p * HEAD_PAD:(2 * p + 1) * HEAD_PAD]
            odd = kv_ref[:, (2 * p + 1) * HEAD_PAD:(2 * p + 2) * HEAD_PAD]
            pair = jnp.where(m_n, pltpu.roll(even, V_HEAD_DIM, 1), odd)
            vo_ref[:, p * LANES:(p + 1) * LANES] = pair.astype(BF16)

    vec = _const_spec((1, HEAD_PAD))
    return pl.pallas_call(
        body,
        grid=(L // tm,),
        in_specs=[_row_spec(tm, W), _row_spec(tm, W), _row_spec(tm, HEAD_PAD), _row_spec(tm, HEAD_PAD),
                  _row_spec(tm, HEAD_PAD), vec, vec, vec],
        out_specs=[_row_spec(tm, W), _row_spec(tm, W), _row_spec(tm, W // 2)],
        out_shape=[jax.ShapeDtypeStruct((L, W), BF16), jax.ShapeDtypeStruct((L, W), BF16),
                   jax.ShapeDtypeStruct((L, W // 2), BF16)],
        compiler_params=_params("arbitrary"),
        name=name,
    )(q, kv, kr, cosf, sins, gq, gkn, gkr)


def qk_prep_bwd(dqo, dko, dv, q, kv, kr, cosf, sins, gq, gkn, gkr, name):
    L = q.shape[0]
    tm = ROW_TILE
    W = MLA_HEADS * HEAD_PAD

    def body(dqo_ref, dko_ref, dv_ref, q_ref, kv_ref, kr_ref, c_ref, s_ref, gq_ref, gkn_ref, gkr_ref,
             dq_ref, dkv_ref, dkr_ref, dgq_ref, dgkn_ref, dgkr_ref):
        @pl.when(pl.program_id(0) == 0)
        def _():
            dgq_ref[...] = jnp.zeros_like(dgq_ref)
            dgkn_ref[...] = jnp.zeros_like(dgkn_ref)
            dgkr_ref[...] = jnp.zeros_like(dgkr_ref)

        lane, m_n, m_r = _head_masks(tm)
        cosf_, sins_ = c_ref[...], s_ref[...]
        dgq = jnp.zeros((1, HEAD_PAD), F32)
        dgkn = jnp.zeros((1, HEAD_PAD), F32)
        dkrf = jnp.zeros((tm, HEAD_PAD), F32)
        for h in range(MLA_HEADS):
            cols = slice(h * HEAD_PAD, (h + 1) * HEAD_PAD)
            dy = dqo_ref[:, cols]
            dqn = dy * cosf_ + _rope_swap(dy * sins_, lane)
            qh = q_ref[:, cols]
            rinv = lax.rsqrt(_seg_mean(qh * qh, m_n, m_r) + RMS_EPS)
            xh = qh * rinv
            dgq = dgq + jnp.sum(dqn * xh, axis=0, keepdims=True)
            dxh = dqn * gq_ref[...]
            dq_ref[:, cols] = rinv * (dxh - xh * _seg_mean(dxh * xh, m_n, m_r))
            dk = dko_ref[:, cols]
            dkrf = dkrf + dk
            kh = jnp.where(m_n, kv_ref[:, cols], 0.0)
            rn = lax.rsqrt(jnp.sum(kh * kh, axis=1, keepdims=True) * (1.0 / QK_NOPE_DIM) + RMS_EPS)
            xk = kh * rn
            dgkn = dgkn + jnp.sum(dk * xk, axis=0, keepdims=True)
            dxk = dk * gkn_ref[...]
            dkn = rn * (dxk - xk * (jnp.sum(dxk * xk, axis=1, keepdims=True) * (1.0 / QK_NOPE_DIM)))
            dvp = dv_ref[:, (h // 2) * LANES:(h // 2 + 1) * LANES]
            dvh = pltpu.roll(dvp, V_HEAD_DIM, 1) if h % 2 == 0 else dvp
            dkv_ref[:, cols] = jnp.where(m_n, dkn, dvh)
        kr_ = kr_ref[...]
        rk = lax.rsqrt(_seg_mean(kr_ * kr_, m_n, m_r) + RMS_EPS)
        xr = kr_ * rk
        dkrn = dkrf * cosf_ + _rope_swap(dkrf * sins_, lane)
        dgkr_ref[...] += jnp.sum(dkrn * xr, axis=0, keepdims=True)
        dxr = dkrn * gkr_ref[...]
        dkr_ref[...] = rk * (dxr - xr * _seg_mean(dxr * xr, m_n, m_r))
        dgq_ref[...] += dgq
        dgkn_ref[...] += dgkn

    vec = _const_spec((1, HEAD_PAD))
    return pl.pallas_call(
        body,
        grid=(L // tm,),
        in_specs=[_row_spec(tm, W), _row_spec(tm, W), _row_spec(tm, W // 2), _row_spec(tm, W), _row_spec(tm, W),
                  _row_spec(tm, HEAD_PAD), _row_spec(tm, HEAD_PAD), _row_spec(tm, HEAD_PAD), vec, vec, vec],
        out_specs=[_row_spec(tm, W), _row_spec(tm, W), _row_spec(tm, HEAD_PAD), vec, vec, vec],
        out_shape=[jax.ShapeDtypeStruct((L, W), F32), jax.ShapeDtypeStruct((L, W), F32),
                   jax.ShapeDtypeStruct((L, HEAD_PAD), F32)] + [jax.ShapeDtypeStruct((1, HEAD_PAD), F32)] * 3,
        compiler_params=_params("arbitrary"),
        name=name,
    )(dqo, dko, dv, q, kv, kr, cosf, sins, gq, gkn, gkr)


def attn_fwd(qp, kp, v, cat, name, gather=()):
    L = qp.shape[0]
    tq = ROW_TILE
    n_i = L // tq
    pair_w = 2 * HEAD_PAD
    n_pairs = MLA_HEADS // 2
    n_g = len(gather)

    def body(q_ref, k_ref, v_ref, cat_in, *rest):
        del cat_in
        o_ref, lse_ref, cat_ref = rest[n_g:n_g + 3]
        i = pl.program_id(1)
        if n_g:
            pair = pl.program_id(0)
            start, forward, finish = _gather_steps(rest[:n_g], rest[n_g + 3:2 * n_g + 3], *rest[2 * n_g + 3:])
            pl.when((pair == 0) & (i == 0))(start)
            pl.when((pair == n_pairs // 2) & (i == 0))(forward)
        left = lax.broadcasted_iota(jnp.int32, (tq, LANES), 1) < V_HEAD_DIM
        row = lax.broadcasted_iota(jnp.int32, (tq, tq), 0)
        col = lax.broadcasted_iota(jnp.int32, (tq, tq), 1)

        def step(j, carry, masked):
            rows = pl.ds(pl.multiple_of(j * tq, tq), tq)
            vv = v_ref[rows, :]
            out = []
            for hh in range(2):
                cols = slice(hh * HEAD_PAD, (hh + 1) * HEAD_PAD)
                m, l, acc = carry[hh]
                s = _nt(q_ref[:, cols], k_ref[rows, cols]) * ATTN_SCALE
                if masked:
                    s = jnp.where(col <= row, s, -jnp.inf)
                m_new = jnp.maximum(m, jnp.max(s, axis=1, keepdims=True))
                alpha = jnp.exp(m - m_new)
                p = jnp.exp(s - m_new)
                l = alpha * l + jnp.sum(p, axis=1, keepdims=True)
                acc = alpha * acc + _nn(p.astype(BF16), vv)
                out.append((m_new, l, acc))
            return tuple(out)

        one = (jnp.full((tq, 1), -jnp.inf, F32), jnp.zeros((tq, 1), F32), jnp.zeros((tq, LANES), F32))
        carry = lax.fori_loop(0, i, functools.partial(step, masked=False), (one, one))
        (m0, l0, a0), (m1, l1, a1) = step(i, carry, True)
        o = jnp.where(left, a0 / l0, a1 / l1)
        o_ref[...] = o
        cat_ref[...] = o.astype(BF16)
        lse_ref[...] = jnp.where(left, m0 + jnp.log(l0), m1 + jnp.log(l1))
        if n_g:
            pl.when((pair == n_pairs - 1) & (i == n_i - 1))(finish)

    return pl.pallas_call(
        body,
        grid=(n_pairs, n_i),
        in_specs=[
            pl.BlockSpec((tq, pair_w), lambda p, i: (i, p)),
            pl.BlockSpec((L, pair_w), lambda p, i: (0, p)),
            pl.BlockSpec((L, LANES), lambda p, i: (0, p)),
            _ANY,
        ] + [_ANY] * n_g,
        out_specs=[
            pl.BlockSpec((tq, LANES), lambda p, i: (i, p)),
            pl.BlockSpec((tq, LANES), lambda p, i: (i, p)),
            pl.BlockSpec((tq, LANES), lambda p, i: (i, n_pairs + p)),
        ] + [_ANY] * n_g,
        out_shape=[jax.ShapeDtypeStruct((L, n_pairs * LANES), F32), jax.ShapeDtypeStruct((L, n_pairs * LANES), F32),
                   jax.ShapeDtypeStruct(cat.shape, cat.dtype)]
        + [jax.ShapeDtypeStruct((N_DEV,) + g.shape, g.dtype) for g in gather],
        scratch_shapes=_gather_scratch(n_g) if n_g else [],
        input_output_aliases={3: 2},
        compiler_params=_params("arbitrary", "arbitrary"),
        name=name,
    )(qp, kp, v, cat, *gather)


def attn_delta(do, o, name):
    L, C = o.shape
    tm = ROW_TILE

    def body(do_ref, o_ref, d_ref):
        left = lax.broadcasted_iota(jnp.int32, (tm, LANES), 1) < V_HEAD_DIM
        for p in range(C // LANES):
            cols = slice(p * LANES, (p + 1) * LANES)
            prod = do_ref[:, cols] * o_ref[:, cols]
            d0 = jnp.sum(jnp.where(left, prod, 0.0), axis=1, keepdims=True)
            d1 = jnp.sum(jnp.where(left, 0.0, prod), axis=1, keepdims=True)
            d_ref[:, cols] = jnp.where(left, d0, d1)

    return pl.pallas_call(
        body,
        grid=(L // tm,),
        in_specs=[_row_spec(tm, C), _row_spec(tm, C)],
        out_specs=_row_spec(tm, C),
        out_shape=jax.ShapeDtypeStruct((L, C), F32),
        compiler_params=_params("arbitrary"),
        name=name,
    )(do, o)


def attn_bwd(qp, kp, v, do, lse, delta, name, exchange=()):
    L = qp.shape[0]
    tq = ROW_TILE
    n_q = L // tq
    pair_w = 2 * HEAD_PAD
    n_pairs = MLA_HEADS // 2
    n_x = len(exchange)

    def body(q_ref, k_ref, v_ref, do_ref, lse_ref, dl_ref, *rest):
        dq_ref, dk_ref, dv_ref = rest[n_x:n_x + 3]
        j = pl.program_id(1)
        if n_x:
            pair = pl.program_id(0)
            start, finish = _chip_exchange_steps(rest[:n_x], rest[n_x + 3:2 * n_x + 3], *rest[2 * n_x + 3:])
            pl.when((pair == 0) & (j == 0))(start)

        @pl.when(j == 0)
        def _():
            dq_ref[...] = jnp.zeros_like(dq_ref)

        dk_ref[...] = jnp.zeros_like(dk_ref)
        dv_ref[...] = jnp.zeros_like(dv_ref)
        left = lax.broadcasted_iota(jnp.int32, (tq, LANES), 1) < V_HEAD_DIM
        row = lax.broadcasted_iota(jnp.int32, (tq, tq), 0)
        col = lax.broadcasted_iota(jnp.int32, (tq, tq), 1)

        def step(i, carry, masked):
            rows = pl.ds(pl.multiple_of(i * tq, tq), tq)
            do = do_ref[rows, :]
            vv = v_ref[...]
            dv = None
            for hh in range(2):
                cols = slice(hh * HEAD_PAD, (hh + 1) * HEAD_PAD)
                stat = slice(hh * V_HEAD_DIM, hh * V_HEAD_DIM + 1)
                q = q_ref[rows, cols]
                k = k_ref[:, cols]
                dom = jnp.where(left if hh == 0 else jnp.logical_not(left), do, 0.0).astype(BF16)
                s = _nt(q, k) * ATTN_SCALE
                p = jnp.exp(s - lse_ref[rows, stat])
                if masked:
                    p = jnp.where(col <= row, p, 0.0)
                dp = _nt(dom, vv)
                ds = (p * (dp - dl_ref[rows, stat]) * ATTN_SCALE).astype(BF16)
                dq_ref[rows, cols] += _nn(ds, k)
                dk_ref[:, cols] += _tn(ds, q)
                t = _tn(p.astype(BF16), dom)
                dv = t if dv is None else dv + t
            dv_ref[...] += dv
            return carry

        step(j, 0, True)
        lax.fori_loop(j + 1, n_q, functools.partial(step, masked=False), 0)
        if n_x:
            pl.when((pair == n_pairs - 1) & (j == n_q - 1))(finish)

    return pl.pallas_call(
        body,
        grid=(n_pairs, n_q),
        in_specs=[
            pl.BlockSpec((L, pair_w), lambda p, j: (0, p)),
            pl.BlockSpec((tq, pair_w), lambda p, j: (j, p)),
            pl.BlockSpec((tq, LANES), lambda p, j: (j, p)),
            pl.BlockSpec((L, LANES), lambda p, j: (0, p)),
            pl.BlockSpec((L, LANES), lambda p, j: (0, p)),
            pl.BlockSpec((L, LANES), lambda p, j: (0, p)),
        ] + [_ANY] * n_x,
        out_specs=[
            pl.BlockSpec((L, pair_w), lambda p, j: (0, p)),
            pl.BlockSpec((tq, pair_w), lambda p, j: (j, p)),
            pl.BlockSpec((tq, LANES), lambda p, j: (j, p)),
        ] + [_ANY] * n_x,
        out_shape=[
            jax.ShapeDtypeStruct((L, n_pairs * pair_w), F32),
            jax.ShapeDtypeStruct((L, n_pairs * pair_w), F32),
            jax.ShapeDtypeStruct((L, n_pairs * LANES), F32),
        ] + [jax.ShapeDtypeStruct(e.shape, e.dtype) for e in exchange],
        scratch_shapes=_chip_exchange_scratch(n_x) if n_x else [],
        compiler_params=_params("arbitrary", "arbitrary"),
        name=name,
    )(qp, kp, v, do, lse, delta, *exchange)


def loss_head(h, target, n_real, name):
    L, D = h.shape
    tm = ROW_TILE

    def body(h_ref, t_ref, dh_ref, sq_ref):
        i = pl.program_id(0)

        @pl.when(i == 0)
        def _():
            sq_ref[...] = jnp.zeros_like(sq_ref)

        t = i * tm + lax.broadcasted_iota(jnp.int32, (tm, D), 0)
        real = (t >= N_META) & (t < N_META + n_real)
        diff = jnp.where(real, h_ref[...] - t_ref[...], 0.0)
        dh_ref[...] = diff * (1.0 / D)
        sq_ref[...] += jnp.sum(diff * diff, axis=0, keepdims=True)

    return pl.pallas_call(
        body,
        grid=(L // tm,),
        in_specs=[_row_spec(tm, D), _row_spec(tm, D)],
        out_specs=[_row_spec(tm, D), _const_spec((1, D))],
        out_shape=[jax.ShapeDtypeStruct((L, D), F32), jax.ShapeDtypeStruct((1, D), F32)],
        compiler_params=_params("arbitrary"),
        name=name,
    )(h, target)


def _mesh_position():
    x, y, c = lax.axis_index("x"), lax.axis_index("y"), lax.axis_index("c")
    return x, y, c, 4 * x + 2 * y + c


def _flip(x, y, c, k):
    px = 1 - x if k & 4 else x
    py = 1 - y if k & 2 else y
    pc = 1 - c if k & 1 else c
    return (px, py, pc), 4 * px + 2 * py + pc


def _other_chips(x, y):
    return [(1 - x, y), (x, 1 - y), (1 - x, 1 - y)]


def _dev_index(px, py, pc):
    return 4 * px + 2 * py + pc


_ANY = pl.BlockSpec(memory_space=pl.ANY)


def cast_bf16(arrays, name):
    n = len(arrays)
    depth = arrays[0].shape[0]

    def body(*refs):
        for k, src in enumerate(refs[:n]):
            for l in range(depth):
                refs[n + k * depth + l][0] = src[l].astype(BF16)

    return pl.pallas_call(
        body,
        out_shape=[jax.ShapeDtypeStruct((1,) + a.shape[1:], BF16) for a in arrays for _ in range(depth)],
        compiler_params=pltpu.CompilerParams(vmem_limit_bytes=VMEM_LIMIT),
        name=name,
    )(*arrays)


def _gather_steps(srcs, dsts, send_sems, recv_sems, local_sems):
    n = len(srcs)
    x, y, c, me = _mesh_position()
    sibling = (x, y, 1 - c)
    chips = _other_chips(x, y)

    def copy(k, b, block, to, from_src=False):
        dst = dsts[b].at[block]
        return pltpu.make_async_remote_copy(
            src_ref=srcs[b] if from_src else dst, dst_ref=dst, send_sem=send_sems.at[k * n + b],
            recv_sem=recv_sems.at[k * n + b], device_id=to, device_id_type=pl.DeviceIdType.MESH)

    def own():
        return [pltpu.make_async_copy(srcs[b], dsts[b].at[me], local_sems.at[b]) for b in range(n)]

    def first():
        out = []
        for b in range(n):
            out.append(copy(0, b, me, sibling, from_src=True))
            out += [copy(1 + q, b, me, (*chip, c), from_src=True) for q, chip in enumerate(chips)]
        return out

    def passed():
        return [copy(4 + q, b, _dev_index(*chip, c), sibling) for q, chip in enumerate(chips) for b in range(n)]

    def start():
        for cp in own() + first():
            cp.start()

    def forward():
        for q, chip in enumerate(chips):
            for b in range(n):
                copy(1 + q, b, _dev_index(*chip, c), sibling).wait_recv()
        for cp in passed():
            cp.start()

    def finish():
        for b in range(n):
            copy(0, b, _dev_index(x, y, 1 - c), sibling).wait_recv()
            for q, chip in enumerate(chips):
                copy(4 + q, b, _dev_index(*chip, 1 - c), sibling).wait_recv()
        for cp in first() + passed():
            cp.wait_send()
        for cp in own():
            cp.wait()

    return start, forward, finish


def _gather_scratch(n):
    copies = N_DEV - 1
    return [pltpu.SemaphoreType.DMA((copies * n,)), pltpu.SemaphoreType.DMA((copies * n,)),
            pltpu.SemaphoreType.DMA((n,))]


def all_gather(payloads, name):
    n = len(payloads)

    def body(*refs):
        start, forward, finish = _gather_steps(refs[:n], refs[n:2 * n], *refs[2 * n:])
        start()
        forward()
        finish()

    return pl.pallas_call(
        body,
        in_specs=[_ANY] * n,
        out_specs=[_ANY] * n,
        out_shape=[jax.ShapeDtypeStruct((N_DEV,) + p.shape, p.dtype) for p in payloads],
        scratch_shapes=_gather_scratch(n),
        name=name,
    )(*payloads)


def exchange_pair(bigs, small, name):
    n = len(bigs)
    has_small = small is not None
    n_big = N_CHIPS * n
    n_sems = n_big + (N_DEV - 1 if has_small else 0)

    def body(*refs):
        big_refs = refs[:n]
        pos = n + has_small
        sib_refs = refs[pos:pos + n]
        send_sems, recv_sems, local_sem = refs[-3:]
        x, y, c, me = _mesh_position()
        sibling = (x, y, 1 - c)
        copies = []
        for b in range(n):
            for q in range(N_CHIPS):
                copies.append(pltpu.make_async_remote_copy(
                    src_ref=big_refs[b].at[_dev_index(q // 2, q % 2, 1 - c)], dst_ref=sib_refs[b].at[q],
                    send_sem=send_sems.at[b * N_CHIPS + q], recv_sem=recv_sems.at[b * N_CHIPS + q],
                    device_id=sibling, device_id_type=pl.DeviceIdType.MESH))
        waits = list(copies)
        if has_small:
            small_ref, gsmall_ref = refs[n], refs[pos + n]
            own = pltpu.make_async_copy(small_ref.at[me], gsmall_ref.at[me], local_sem.at[0])
            own.start()
            for k in range(1, N_DEV):
                peer, peer_idx = _flip(x, y, c, k)
                s = n_big + k - 1
                copies.append(pltpu.make_async_remote_copy(
                    src_ref=small_ref.at[peer_idx], dst_ref=gsmall_ref.at[me], send_sem=send_sems.at[s],
                    recv_sem=recv_sems.at[s], device_id=peer, device_id_type=pl.DeviceIdType.MESH))
                waits.append(pltpu.make_async_remote_copy(
                    src_ref=small_ref.at[peer_idx], dst_ref=gsmall_ref.at[peer_idx], send_sem=send_sems.at[s],
                    recv_sem=recv_sems.at[s], device_id=peer, device_id_type=pl.DeviceIdType.MESH))
        for cp in copies:
            cp.start()
        for cp in waits:
            cp.wait_recv()
        for cp in copies:
            cp.wait_send()
        if has_small:
            own.wait()

    out_shape = [jax.ShapeDtypeStruct((N_CHIPS,) + b.shape[1:], b.dtype) for b in bigs]
    args = list(bigs)
    if has_small:
        out_shape.append(jax.ShapeDtypeStruct(small.shape, small.dtype))
        args.append(small)
    return pl.pallas_call(
        body,
        in_specs=[_ANY] * len(args),
        out_specs=[_ANY] * len(out_shape),
        out_shape=out_shape,
        scratch_shapes=[pltpu.SemaphoreType.DMA((n_sems,)), pltpu.SemaphoreType.DMA((n_sems,)),
                        pltpu.SemaphoreType.DMA((1,))],
        name=name,
    )(*args)


def pair_sum(bigs, from_sibling, core, name):
    n = len(bigs)

    def body(core_ref, *refs):
        del core_ref
        for mine, sib, out in zip(refs[:n], refs[n:2 * n], refs[2 * n:]):
            out[...] = (mine[...].astype(F32) + sib[...].astype(F32)).astype(out.dtype)

    def slot(shape, picked):
        zeros = (0,) * (len(shape) - 1)
        if picked:
            return pl.BlockSpec((None,) + shape[1:], lambda q, core_ref: (2 * q + core_ref[0],) + zeros)
        return pl.BlockSpec((None,) + shape[1:], lambda q, core_ref: (q,) + zeros)

    grid_spec = pltpu.PrefetchScalarGridSpec(
        num_scalar_prefetch=1,
        grid=(N_CHIPS,),
        in_specs=[slot(b.shape, True) for b in bigs] + [slot(s.shape, False) for s in from_sibling],
        out_specs=[slot(s.shape, False) for s in from_sibling],
    )
    return pl.pallas_call(
        body,
        grid_spec=grid_spec,
        out_shape=[jax.ShapeDtypeStruct(s.shape, s.dtype) for s in from_sibling],
        compiler_params=_params("arbitrary"),
        name=name,
    )(core, *bigs, *from_sibling)


def _chip_exchange_steps(part_refs, got_refs, send_sems, recv_sems, local_sems):
    n = len(part_refs)
    x, y, c, me = _mesh_position()
    mine = 2 * x + y

    def copies(landing):
        out = []
        for q, (px, py) in enumerate(_other_chips(x, y)):
            theirs = 2 * px + py
            for b in range(n):
                out.append(pltpu.make_async_remote_copy(
                    src_ref=part_refs[b].at[theirs], dst_ref=got_refs[b].at[theirs if landing else mine],
                    send_sem=send_sems.at[q * n + b], recv_sem=recv_sems.at[q * n + b],
                    device_id=(px, py, c), device_id_type=pl.DeviceIdType.MESH))
        return out

    def own():
        return [pltpu.make_async_copy(part_refs[b].at[mine], got_refs[b].at[mine], local_sems.at[b])
                for b in range(n)]

    def start():
        for cp in own() + copies(False):
            cp.start()

    def finish():
        for cp in copies(True):
            cp.wait_recv()
        for cp in copies(False):
            cp.wait_send()
        for cp in own():
            cp.wait()

    return start, finish


def _chip_exchange_scratch(n):
    return [pltpu.SemaphoreType.DMA(((N_CHIPS - 1) * n,)), pltpu.SemaphoreType.DMA(((N_CHIPS - 1) * n,)),
            pltpu.SemaphoreType.DMA((n,))]


def exchange_chips(partials, name):
    n = len(partials)

    def body(*refs):
        start, finish = _chip_exchange_steps(refs[:n], refs[n:2 * n], *refs[2 * n:])
        start()
        finish()

    return pl.pallas_call(
        body,
        in_specs=[_ANY] * n,
        out_specs=[_ANY] * n,
        out_shape=[jax.ShapeDtypeStruct(p.shape, p.dtype) for p in partials],
        scratch_shapes=_chip_exchange_scratch(n),
        name=name,
    )(*partials)


def _adamw_math(g, w, m, v):
    m_new = ADAM_B1 * m + (1.0 - ADAM_B1) * g
    v_new = ADAM_B2 * v + (1.0 - ADAM_B2) * (g * g)
    m_hat = m_new / (1.0 - ADAM_B1 ** ADAM_STEP)
    v_hat = v_new / (1.0 - ADAM_B2 ** ADAM_STEP)
    delta = -ADAM_LR * (m_hat / (jnp.sqrt(v_hat) + ADAM_EPS) + ADAM_WD * w)
    return delta, m_new, v_new


def _chip_total(p_ref):
    g = p_ref[0].astype(F32)
    for q in range(1, N_CHIPS):
        g = g + p_ref[q].astype(F32)
    return g


def chip_sum(parts, name):
    depth = len(parts)

    def body(*refs):
        for l in range(depth):
            refs[depth][l] = _chip_total(refs[l])

    return pl.pallas_call(
        body,
        out_shape=jax.ShapeDtypeStruct((depth,) + parts[0].shape[1:], F32),
        compiler_params=pltpu.CompilerParams(vmem_limit_bytes=VMEM_LIMIT),
        name=name,
    )(*parts)


def adamw_shard(parts, w, m, v, name, grad=None):
    depth = w.shape[0]
    n_in = 1 if grad is not None else depth

    def body(*refs):
        w_ref, m_ref, v_ref, g_ref, d_ref, mo_ref, vo_ref = refs[n_in:]
        for l in range(depth):
            g = refs[0][l] if grad is not None else _chip_total(refs[l])
            delta, m_new, v_new = _adamw_math(g, w_ref[l], m_ref[l], v_ref[l])
            g_ref[l] = g
            d_ref[l] = delta
            mo_ref[l] = m_new
            vo_ref[l] = v_new

    return pl.pallas_call(
        body,
        out_shape=[jax.ShapeDtypeStruct(w.shape, F32)] * 4,
        compiler_params=pltpu.CompilerParams(vmem_limit_bytes=VMEM_LIMIT),
        name=name,
    )(*([grad] if grad is not None else parts), w, m, v)


def adamw_packed(parts, w, m, v, name):
    n_slots = parts.shape[0]

    def body(p_ref, w_ref, m_ref, v_ref, g_ref, d_ref, mo_ref, vo_ref):
        g = p_ref[0]
        for s in range(1, n_slots):
            g = g + p_ref[s]
        delta, m_new, v_new = _adamw_math(g, w_ref[...], m_ref[...], v_ref[...])
        g_ref[...] = g
        d_ref[...] = delta
        mo_ref[...] = m_new
        vo_ref[...] = v_new

    return pl.pallas_call(
        body,
        out_shape=[jax.ShapeDtypeStruct(w.shape, F32)] * 4,
        compiler_params=pltpu.CompilerParams(vmem_limit_bytes=VMEM_LIMIT),
        name=name,
    )(parts, w, m, v)


SHARDED = ("w_in", "w_q_b", "w_kv_b", "w_out", "w_gate", "w_up", "w_down")
TRANSPOSED = ("w_in", "w_gate", "w_up")
REPLICATED = ("attn_norm_g", "w_pool", "pool_scale", "q_a_norm_g", "kv_a_norm_g", "q_norm_g", "k_norm_g",
              "ffn_norm_g")


def _pack_flat(arrays, rows):
    flat = jnp.concatenate([a.reshape(-1) for a in arrays])
    return jnp.pad(flat, (0, rows * LANES - flat.shape[0])).reshape(rows, LANES)


def _unpack_flat(packed, shapes):
    flat, out, at = packed.reshape(-1), [], 0
    for shp in shapes:
        n = 1
        for d in shp:
            n *= d
        out.append(flat[at:at + n].reshape(shp))
        at += n
    return out


def _rope_lane_tables(length):
    inv = 1.0 / (ROPE_THETA ** (jnp.arange(0, QK_ROPE_DIM, 2, dtype=F32) / QK_ROPE_DIM))
    ang = jnp.arange(length, dtype=F32)[:, None] * inv[None, :]
    cos, sin = jnp.cos(ang), jnp.sin(ang)
    ones = jnp.ones((length, QK_NOPE_DIM), F32)
    zeros = jnp.zeros((length, QK_NOPE_DIM), F32)
    tail = HEAD_PAD - QK_HEAD_DIM
    cosf = jnp.concatenate([ones, cos, cos, ones[:, :tail]], axis=1)
    sins = jnp.concatenate([zeros, -sin, sin, zeros[:, :tail]], axis=1)
    return cosf, sins


def _pad_lanes(vec, at, width=HEAD_PAD):
    return jnp.pad(vec, (at, width - at - vec.shape[0])).reshape(1, width)


def kernel(x, meta_tokens, attn_norm_g, w_in, w_pool, pool_scale, q_a_norm_g, w_q_b, kv_a_norm_g, w_kv_b, q_norm_g, k_norm_g, w_out, ffn_norm_g, w_gate, w_up, w_down, loss_target, m_meta_tokens, m_attn_norm_g, m_w_in, m_w_pool, m_pool_scale, m_q_a_norm_g, m_w_q_b, m_kv_a_norm_g, m_w_kv_b, m_q_norm_g, m_k_norm_g, m_w_out, m_ffn_norm_g, m_w_gate, m_w_up, m_w_down, v_meta_tokens, v_attn_norm_g, v_w_in, v_w_pool, v_pool_scale, v_q_a_norm_g, v_w_q_b, v_kv_a_norm_g, v_w_kv_b, v_q_norm_g, v_k_norm_g, v_w_out, v_ffn_norm_g, v_w_gate, v_w_up, v_w_down):
    weights = dict(meta_tokens=meta_tokens, attn_norm_g=attn_norm_g, w_in=w_in, w_pool=w_pool, pool_scale=pool_scale,
                   q_a_norm_g=q_a_norm_g, w_q_b=w_q_b, kv_a_norm_g=kv_a_norm_g, w_kv_b=w_kv_b, q_norm_g=q_norm_g,
                   k_norm_g=k_norm_g, w_out=w_out, ffn_norm_g=ffn_norm_g, w_gate=w_gate, w_up=w_up, w_down=w_down)
    mom1 = dict(meta_tokens=m_meta_tokens, attn_norm_g=m_attn_norm_g, w_in=m_w_in, w_pool=m_w_pool,
                pool_scale=m_pool_scale, q_a_norm_g=m_q_a_norm_g, w_q_b=m_w_q_b, kv_a_norm_g=m_kv_a_norm_g,
                w_kv_b=m_w_kv_b, q_norm_g=m_q_norm_g, k_norm_g=m_k_norm_g, w_out=m_w_out, ffn_norm_g=m_ffn_norm_g,
                w_gate=m_w_gate, w_up=m_w_up, w_down=m_w_down)
    mom2 = dict(meta_tokens=v_meta_tokens, attn_norm_g=v_attn_norm_g, w_in=v_w_in, w_pool=v_w_pool,
                pool_scale=v_pool_scale, q_a_norm_g=v_q_a_norm_g, w_q_b=v_w_q_b, kv_a_norm_g=v_kv_a_norm_g,
                w_kv_b=v_w_kv_b, q_norm_g=v_q_norm_g, k_norm_g=v_k_norm_g, w_out=v_w_out, ffn_norm_g=v_ffn_norm_g,
                w_gate=v_w_gate, w_up=v_w_up, w_down=v_w_down)
    order = ("meta_tokens", "attn_norm_g", "w_in", "w_pool", "pool_scale", "q_a_norm_g", "w_q_b", "kv_a_norm_g",
             "w_kv_b", "q_norm_g", "k_norm_g", "w_out", "ffn_norm_g", "w_gate", "w_up", "w_down")
    depth = w_in.shape[0]
    seq = x.shape[1]
    length = N_META + seq
    lp = -(-length // ROW_TILE) * ROW_TILE
    in_cols = w_in.shape[2]

    local = {n: (jnp.swapaxes(weights[n], 1, 2) if n in TRANSPOSED else weights[n]) for n in SHARDED}
    cast = cast_bf16([local[n] for n in SHARDED], "cast_weights")
    shards = [[cast[k * depth + l] for k in range(len(SHARDED))] for l in range(depth)]
    gathered = all_gather(shards[0] + [meta_tokens], "all_gather")
    g8l = [dict(zip(SHARDED, gathered[:-1]))] + [None] * (depth - 1)
    meta_full = jnp.transpose(gathered[-1], (1, 0, 2)).reshape(N_META, D_MODEL)

    s1, s2, s3 = POOL_WIDTH, POOL_WIDTH + Q_LORA_RANK, POOL_WIDTH + Q_LORA_RANK + KV_LORA_RANK
    zpad = lambda n: jnp.zeros((1, n, D_MODEL), BF16)

    def padded_in_proj(w8):
        w_in_t = jnp.swapaxes(w8, 0, 1).reshape(1, N_DEV * in_cols, D_MODEL)
        return jnp.concatenate([w_in_t[:, :s3], zpad(QK_NOPE_DIM), w_in_t[:, s3:],
                                zpad(HEAD_PAD - QK_HEAD_DIM)], axis=1)

    w_in_ps = [None] * depth
    w_pool_b = w_pool.astype(BF16)
    vw = MLA_HEADS * V_HEAD_DIM

    cosf, sins = _rope_lane_tables(lp)
    row = lambda a, l: a[l].reshape(1, -1)

    h = jnp.concatenate([meta_full, x[0], jnp.zeros((lp - length, D_MODEL), F32)], axis=0)
    target = jnp.pad(loss_target[0], ((N_META, lp - length), (0, 0)))
    saved = []
    for l in range(depth):
        gq = _pad_lanes(q_norm_g[l], 0)
        gkn = _pad_lanes(k_norm_g[l, :QK_NOPE_DIM], 0)
        gkr = _pad_lanes(k_norm_g[l, QK_NOPE_DIM:], QK_NOPE_DIM)
        g8 = g8l[l]
        w_in_ps[l] = padded_in_proj(g8["w_in"])
        a, u, c_q, c_kv, kr = norm_mm(
            h, row(attn_norm_g, l), w_in_ps[l], 0,
            [(0, s1), (s1, Q_LORA_RANK), (s2, KV_LORA_RANK), (s3, HEAD_PAD)], [F32] * 4, "in_proj")
        cat = pool_fwd(u, w_pool_b[l], row(pool_scale, l), "pool_fwd")
        qn, q = norm_mm_heads(c_q, row(q_a_norm_g, l), g8["w_q_b"], 0, "q_proj")
        kvn, kv = norm_mm_heads(c_kv, row(kv_a_norm_g, l), g8["w_kv_b"], 0, "kv_proj")
        qp, kp, v = qk_prep_fwd(q, kv, kr, cosf, sins, gq, gkn, gkr, "qk_prep_fwd")
        if l + 1 < depth:
            o, lse, cat, *nxt = attn_fwd(qp, kp, v, cat, "attn_fwd_gather", gather=shards[l + 1])
            g8l[l + 1] = dict(zip(SHARDED, nxt))
        else:
            o, lse, cat = attn_fwd(qp, kp, v, cat, "attn_fwd")
        h_mid, g = mm_res(cat, g8["w_out"], h, 0, "out_proj", gamma=row(ffn_norm_g, l))
        gate, up, act = ffn_up(g, g8["w_gate"], g8["w_up"], 0, "ffn_up")
        h_next = mm_res(act, g8["w_down"], h_mid, 0, "ffn_down")[0]
        saved.append(dict(h=h, a=a, u=u, c_q=c_q, c_kv=c_kv, kr=kr, qn=qn, q=q, kvn=kvn, kv=kv, v=v, qp=qp, kp=kp,
                          o=o, lse=lse, cat=cat, h_mid=h_mid, g=g, gate=gate, up=up, act=act,
                          gq=gq, gkn=gkn, gkr=gkr))
        h = h_next

    dh, sq = loss_head(h, target, seq, "loss_head")
    loss = lax.psum(0.5 / D_MODEL * jnp.sum(sq), ("x", "y", "c"))

    core = lax.axis_index("c").astype(jnp.int32).reshape(1)
    small_grads = {n: [None] * depth for n in REPLICATED}
    got = [None] * depth
    pending = None
    for l in reversed(range(depth)):
        s = saved[l]
        g8 = g8l[l]
        w_in_p = w_in_ps[l]
        slots = {}
        dgate, dup = ffn_bwd_act(dh, g8["w_down"], 0, s["gate"], s["up"], "ffn_bwd_act")
        slot_of = lambda n, full: full.reshape(g8[n].shape[:1] + g8[n].shape[2:])
        slots["w_down"] = slot_of("w_down", mm_tn(s["act"], dh, "dw_down", out_dtype=BF16))
        slots["w_gate"] = slot_of("w_gate", mm_tn(dgate, s["g"], "dw_gate", out_dtype=BF16))
        slots["w_up"] = slot_of("w_up", mm_tn(dup, s["g"], "dw_up", out_dtype=BF16))
        dh_mid, dg_ffn = mm_nt_normbwd([("gathered", dgate, g8["w_gate"]), ("gathered", dup, g8["w_up"])],
                                       s["h_mid"], row(ffn_norm_g, l), dh, 0, "ffn_bwd_in")
        small_grads["ffn_norm_g"][l] = dg_ffn[0]
        slots["w_out"] = slot_of("w_out", mm_tn(s["cat"], dh_mid, "dw_out", out_dtype=BF16))
        dy_pool, do = mm_nt(dh_mid, g8["w_out"], 0, [(0, POOL_WIDTH), (POOL_WIDTH, vw)], "out_proj_bwd")
        du, dw_pool, dscale = pool_bwd(s["u"], dy_pool, w_pool_b[l], row(pool_scale, l), "pool_bwd")
        small_grads["w_pool"][l] = dw_pool
        small_grads["pool_scale"][l] = dscale[0]
        delta = attn_delta(do, s["o"], "attn_delta")
        if pending is None:
            dqp, dkp, dv = attn_bwd(s["qp"], s["kp"], s["v"], do, s["lse"], delta, "attn_bwd")
        else:
            dqp, dkp, dv, *arrived = attn_bwd(s["qp"], s["kp"], s["v"], do, s["lse"], delta, "attn_bwd_exchange",
                                              exchange=pending)
            got[l + 1] = dict(zip(SHARDED, arrived))
        dq, dkv, dkr, dgq, dgkn, dgkr = qk_prep_bwd(dqp, dkp, dv, s["q"], s["kv"], s["kr"], cosf, sins,
                                                    s["gq"], s["gkn"], s["gkr"], "qk_prep_bwd")
        small_grads["q_norm_g"][l] = dgq[0, :QK_HEAD_DIM]
        small_grads["k_norm_g"][l] = jnp.concatenate([dgkn[0, :QK_NOPE_DIM], dgkr[0, QK_NOPE_DIM:QK_HEAD_DIM]])
        heads_first = lambda full: jnp.swapaxes(full.reshape(full.shape[0], MLA_HEADS, HEAD_PAD), 0, 1)
        slots["w_q_b"] = heads_first(mm_tn(s["qn"], dq, "dw_q", out_dtype=BF16))[:, :, :QK_HEAD_DIM]
        dc_q, dg_qa = mm_nt_normbwd([("heads", dq, g8["w_q_b"])], s["c_q"], row(q_a_norm_g, l), None, 0,
                                    "q_proj_bwd")
        small_grads["q_a_norm_g"][l] = dg_qa[0]
        slots["w_kv_b"] = heads_first(mm_tn(s["kvn"], dkv, "dw_kv", out_dtype=BF16))
        dc_kv, dg_kva = mm_nt_normbwd([("heads", dkv, g8["w_kv_b"])], s["c_kv"], row(kv_a_norm_g, l), None, 0,
                                      "kv_proj_bwd")
        small_grads["kv_a_norm_g"][l] = dg_kva[0]
        dw_in_t = jnp.concatenate(
            [mm_tn(du, s["a"], "dw_in_pool", out_dtype=BF16), mm_tn(dc_q, s["a"], "dw_in_q", out_dtype=BF16),
             mm_tn(dc_kv, s["a"], "dw_in_kv", out_dtype=BF16),
             mm_tn(dkr, s["a"], "dw_in_rope", out_dtype=BF16)[QK_NOPE_DIM:QK_HEAD_DIM]], axis=0)
        slots["w_in"] = slot_of("w_in", dw_in_t)
        dh, dg_attn = mm_nt_normbwd(
            [("rows", du, w_in_p, 0), ("rows", dc_q, w_in_p, s1), ("rows", dc_kv, w_in_p, s2),
             ("rows", dkr, w_in_p, s3)],
            s["h"], row(attn_norm_g, l), dh_mid, 0, "in_proj_bwd")
        small_grads["attn_norm_g"][l] = dg_attn[0]

        bigs = [slots[n] for n in SHARDED]
        if l > 0:
            from_sibling = exchange_pair(bigs, None, "grad_exchange_pair")
        else:
            rep_shapes = [weights[n].shape for n in REPLICATED]
            rep_count = sum(int(jnp.size(weights[n])) for n in REPLICATED)
            rep_rows = -(-rep_count // (8 * LANES)) * 8
            rep_packed = _pack_flat([jnp.stack(small_grads[n], axis=0) for n in REPLICATED], rep_rows)
            meta_slots = jnp.transpose(dh[:N_META].reshape(N_META, N_DEV, LANES), (1, 0, 2))
            small_slots = jnp.concatenate(
                [meta_slots, jnp.broadcast_to(rep_packed[None], (N_DEV, rep_rows, LANES))], axis=1)
            *from_sibling, got_small = exchange_pair(bigs, small_slots, "grad_exchange_pair_small")
        pending = list(pair_sum(bigs, list(from_sibling), core, "grad_pair_sum"))
    got[0] = dict(zip(SHARDED, exchange_chips(pending, "grad_exchange_chips")))

    grad_x = dh[N_META:length][None]

    per = [{} for _ in range(4)]
    for n in SHARDED:
        parts = [got[l][n] for l in range(depth)]
        if n in TRANSPOSED:
            grad = jnp.swapaxes(chip_sum(parts, "chip_sum_" + n), 1, 2)
            outs = adamw_shard(None, weights[n], mom1[n], mom2[n], "adamw_" + n, grad=grad)
        else:
            outs = adamw_shard(parts, weights[n], mom1[n], mom2[n], "adamw_" + n)
        for k in range(4):
            per[k][n] = outs[k]
    ps = lambda src: jnp.concatenate(
        [src["meta_tokens"], _pack_flat([src[n] for n in REPLICATED], rep_rows)], axis=0)
    small_out = adamw_packed(got_small, ps(weights), ps(mom1), ps(mom2), "adamw_small")
    for k in range(4):
        per[k]["meta_tokens"] = small_out[k][:N_META]
        per[k].update(zip(REPLICATED, _unpack_flat(small_out[k][N_META:], rep_shapes)))
    return (loss, grad_x, *[per[0][n] for n in order], *[per[1][n] for n in order],
            *[per[2][n] for n in order], *[per[3][n] for n in order])
```

```python
import functools

import jax
import jax.numpy as jnp
from jax import lax
from jax.experimental import pallas as pl
from jax.experimental.pallas import tpu as pltpu

F32 = jnp.float32
BF16 = jnp.bfloat16

D_MODEL = 1024
N_META = 16
POOL_WIDTH = 512
POOL_WINDOWS = (2, 4, 8, 16)
POOL_GROUP_DIM = 128
POOL_HALO = 16
MLA_HEADS = 8
QK_NOPE_DIM = 64
QK_ROPE_DIM = 32
QK_HEAD_DIM = 96
V_HEAD_DIM = 64
HEAD_PAD = 128
Q_LORA_RANK = 384
KV_LORA_RANK = 256
ROPE_THETA = 10000.0
RMS_EPS = 1e-6
ATTN_SCALE = QK_HEAD_DIM ** -0.5

ADAM_LR = 0.001
ADAM_B1 = 0.9
ADAM_B2 = 0.999
ADAM_EPS = 1e-08
ADAM_WD = 0.01
ADAM_STEP = 10

N_DEV = 8
N_CHIPS = 4
LANES = 128
ROW_TILE = 384
LONG_TILE = 1056
VMEM_LIMIT = 56 * 1024 * 1024


def _params(*sem):
    return pltpu.CompilerParams(dimension_semantics=sem, vmem_limit_bytes=VMEM_LIMIT)


def _row_spec(tile, width):
    return pl.BlockSpec((tile, width), lambda i: (i, 0))


def _const_spec(shape):
    return pl.BlockSpec(shape, lambda i: tuple(0 for _ in shape))


def _layer_spec(w, layer):
    return pl.BlockSpec((None,) + w.shape[1:], lambda *_: (layer, 0, 0))


def _gathered_spec(w8, layer):
    return pl.BlockSpec((N_DEV, None) + w8.shape[2:], lambda *_: (0, layer, 0, 0))


def _nt(a, b):
    return lax.dot_general(a, b, (((1,), (1,)), ((), ())), preferred_element_type=F32)


def _tn(a, b):
    return lax.dot_general(a, b, (((0,), (0,)), ((), ())), preferred_element_type=F32)


def _nn(a, b):
    return jnp.dot(a, b, preferred_element_type=F32)


def _silu(g):
    return g * (1.0 / (1.0 + jnp.exp(-g)))


def _rms(xf):
    return lax.rsqrt(jnp.mean(xf * xf, axis=-1, keepdims=True) + RMS_EPS)


def norm_mm(x, gamma, wt, layer, splits, dtypes, name):
    L, K = x.shape
    tm = ROW_TILE

    def body(x_ref, g_ref, w_ref, a_ref, *z_refs):
        xf = x_ref[...]
        a = ((xf * _rms(xf)) * g_ref[...]).astype(BF16)
        a_ref[...] = a
        z = _nt(a, w_ref[...])
        for (s, n), zr in zip(splits, z_refs):
            zr[...] = z[:, s:s + n].astype(zr.dtype)

    widths = [n for _, n in splits]
    return pl.pallas_call(
        body,
        grid=(L // tm,),
        in_specs=[_row_spec(tm, K), _const_spec((1, K)), _layer_spec(wt, layer)],
        out_specs=[_row_spec(tm, K)] + [_row_spec(tm, n) for n in widths],
        out_shape=[jax.ShapeDtypeStruct((L, K), BF16)]
        + [jax.ShapeDtypeStruct((L, n), dt) for n, dt in zip(widths, dtypes)],
        compiler_params=_params("arbitrary"),
        name=name,
    )(x, gamma, wt)


def norm_mm_heads(x, gamma, w8, layer, name):
    L, K = x.shape
    hw = w8.shape[-1]
    tm = ROW_TILE

    def body(x_ref, g_ref, w_ref, a_ref, z_ref):
        xf = x_ref[...]
        a = ((xf * _rms(xf)) * g_ref[...]).astype(BF16)
        a_ref[...] = a
        if hw < HEAD_PAD:
            z_ref[...] = jnp.zeros_like(z_ref)
        for j in range(MLA_HEADS):
            z_ref[:, j * HEAD_PAD:j * HEAD_PAD + hw] = _nn(a, w_ref[j])

    return pl.pallas_call(
        body,
        grid=(L // tm,),
        in_specs=[_row_spec(tm, K), _const_spec((1, K)), _gathered_spec(w8, layer)],
        out_specs=[_row_spec(tm, K), _row_spec(tm, MLA_HEADS * HEAD_PAD)],
        out_shape=[jax.ShapeDtypeStruct((L, K), BF16), jax.ShapeDtypeStruct((L, MLA_HEADS * HEAD_PAD), F32)],
        compiler_params=_params("arbitrary"),
        name=name,
    )(x, gamma, w8)


FF_GROUP = 4


def _ff_spec(w8, layer):
    return pl.BlockSpec((FF_GROUP, None) + w8.shape[2:], lambda j, i: (j, layer, 0, 0))


def ffn_up(g, w_gate8, w_up8, layer, name):
    L, K = g.shape
    fb = w_gate8.shape[-2]
    tm, tf = ROW_TILE, FF_GROUP * fb

    def body(a_ref, wg_ref, wu_ref, gate_ref, up_ref, act_ref):
        a = a_ref[...]
        gate = _nt(a, wg_ref[...].reshape(tf, K))
        up = _nt(a, wu_ref[...].reshape(tf, K))
        gate_ref[...] = gate
        up_ref[...] = up
        act_ref[...] = (_silu(gate) * up).astype(BF16)

    tile = pl.BlockSpec((tm, tf), lambda j, i: (i, j))
    F = N_DEV * fb
    return pl.pallas_call(
        body,
        grid=(N_DEV // FF_GROUP, L // tm),
        in_specs=[pl.BlockSpec((tm, K), lambda j, i: (i, 0)), _ff_spec(w_gate8, layer), _ff_spec(w_up8, layer)],
        out_specs=[tile, tile, tile],
        out_shape=[
            jax.ShapeDtypeStruct((L, F), F32),
            jax.ShapeDtypeStruct((L, F), F32),
            jax.ShapeDtypeStruct((L, F), BF16),
        ],
        compiler_params=_params("arbitrary", "arbitrary"),
        name=name,
    )(g, w_gate8, w_up8)


def mm_res(a, w8, res, layer, name, gamma=None):
    L = a.shape[0]
    kb, N = w8.shape[-2:]
    tm = ROW_TILE
    normed = gamma is not None

    def body(a_ref, w_ref, r_ref, *rest):
        out = r_ref[...] + _nn(a_ref[...], w_ref[...].reshape(N_DEV * kb, N))
        if normed:
            g_ref, o_ref, n_ref = rest
            n_ref[...] = ((out * _rms(out)) * g_ref[...]).astype(BF16)
        else:
            (o_ref,) = rest
        o_ref[...] = out

    in_specs = [_row_spec(tm, N_DEV * kb), _gathered_spec(w8, layer), _row_spec(tm, N)]
    out_specs = [_row_spec(tm, N)]
    out_shape = [jax.ShapeDtypeStruct((L, N), F32)]
    args = [a, w8, res]
    if normed:
        in_specs.append(_const_spec((1, N)))
        out_specs.append(_row_spec(tm, N))
        out_shape.append(jax.ShapeDtypeStruct((L, N), BF16))
        args.append(gamma)
    return pl.pallas_call(
        body,
        grid=(L // tm,),
        in_specs=in_specs,
        out_specs=out_specs,
        out_shape=out_shape,
        compiler_params=_params("arbitrary"),
        name=name,
    )(*args)


def mm_nt(dz, w8, layer, splits, name):
    L, N = dz.shape
    kb = w8.shape[-2]
    tm = ROW_TILE

    def body(dz_ref, w_ref, *o_refs):
        d = _nt(dz_ref[...].astype(BF16), w_ref[...].reshape(N_DEV * kb, N))
        for (s, n), o in zip(splits, o_refs):
            o[...] = d[:, s:s + n]

    return pl.pallas_call(
        body,
        grid=(L // tm,),
        in_specs=[_row_spec(tm, N), _gathered_spec(w8, layer)],
        out_specs=[_row_spec(tm, n) for _, n in splits],
        out_shape=[jax.ShapeDtypeStruct((L, n), F32) for _, n in splits],
        compiler_params=_params("arbitrary"),
        name=name,
    )(dz, w8)


def mm_nt_normbwd(terms, x, gamma, dres, layer, name):
    L, K = x.shape
    tm = ROW_TILE
    n_terms = len(terms)
    has_res = dres is not None
    weights = []
    for t in terms:
        if not any(t[2] is u for u in weights):
            weights.append(t[2])
    which = [[t[2] is u for u in weights].index(True) for t in terms]
    n_in = n_terms + len(weights)

    def body(*refs):
        dz_refs = refs[:n_terms]
        w_refs = [refs[n_terms + n] for n in which]
        x_ref, g_ref = refs[n_in], refs[n_in + 1]
        pos = n_in + 2
        r_ref = refs[pos] if has_res else None
        dx_ref, dg_ref = refs[pos + has_res], refs[pos + has_res + 1]
        da = None
        for t, dz_ref, w_ref in zip(terms, dz_refs, w_refs):
            if t[0] == "rows":
                n, at = t[1].shape[1], t[3]
                parts = [_nn(dz_ref[...].astype(BF16), w_ref[at:at + n, :])]
            elif t[0] == "heads":
                hw = t[2].shape[-1]
                parts = [_nt(dz_ref[:, j * HEAD_PAD:j * HEAD_PAD + hw].astype(BF16), w_ref[j])
                         for j in range(MLA_HEADS)]
            else:
                nb = t[2].shape[-2]
                parts = [_nn(dz_ref[...].astype(BF16), w_ref[...].reshape(N_DEV * nb, K))]
            for p in parts:
                da = p if da is None else da + p
        xf = x_ref[...]
        r = _rms(xf)
        xh = xf * r

        @pl.when(pl.program_id(0) == 0)
        def _():
            dg_ref[...] = jnp.zeros_like(dg_ref)

        dg_ref[...] += jnp.sum(da * xh, axis=0, keepdims=True)
        dxh = da * g_ref[...]
        dx = r * (dxh - xh * jnp.mean(dxh * xh, axis=-1, keepdims=True))
        if has_res:
            dx = dx + r_ref[...]
        dx_ref[...] = dx

    in_specs = [_row_spec(tm, t[1].shape[1]) for t in terms]
    for w in weights:
        in_specs.append(_layer_spec(w, layer) if w.ndim == 3 else _gathered_spec(w, layer))
    in_specs += [_row_spec(tm, K), _const_spec((1, K))]
    args = [t[1] for t in terms] + weights + [x, gamma]
    if has_res:
        in_specs.append(_row_spec(tm, K))
        args.append(dres)
    return pl.pallas_call(
        body,
        grid=(L // tm,),
        in_specs=in_specs,
        out_specs=[_row_spec(tm, K), _const_spec((1, K))],
        out_shape=[jax.ShapeDtypeStruct((L, K), F32), jax.ShapeDtypeStruct((1, K), F32)],
        compiler_params=_params("arbitrary"),
        name=name,
    )(*args)


MAX_OUT_ROWS = 1408


def mm_tn(a, b, name, out_dtype=F32):
    L, K = a.shape
    N = b.shape[-1]
    tk = MAX_OUT_ROWS if (K > MAX_OUT_ROWS and K % MAX_OUT_ROWS == 0) else K
    tl = LONG_TILE if L % LONG_TILE == 0 else ROW_TILE
    n_l = L // tl

    def body(a_ref, b_ref, o_ref, acc):
        l = pl.program_id(1)

        @pl.when(l == 0)
        def _():
            acc[...] = jnp.zeros_like(acc)

        acc[...] += _tn(a_ref[...].astype(BF16), b_ref[...].astype(BF16))

        @pl.when(l == n_l - 1)
        def _():
            o_ref[...] = acc[...].astype(o_ref.dtype)

    return pl.pallas_call(
        body,
        grid=(K // tk, n_l),
        in_specs=[pl.BlockSpec((tl, tk), lambda j, l: (l, j)), pl.BlockSpec((tl, N), lambda j, l: (l, 0))],
        out_specs=pl.BlockSpec((tk, N), lambda j, l: (j, 0)),
        out_shape=jax.ShapeDtypeStruct((K, N), out_dtype),
        scratch_shapes=[pltpu.VMEM((tk, N), F32)],
        compiler_params=_params("arbitrary", "arbitrary"),
        name=name,
    )(a, b)


def ffn_bwd_act(dh, w_down8, layer, gate, up, name):
    L, K = dh.shape
    fb = w_down8.shape[-2]
    tm, tf = ROW_TILE, FF_GROUP * fb

    def body(dh_ref, w_ref, gate_ref, up_ref, dgate_ref, dup_ref):
        dact = _nt(dh_ref[...].astype(BF16), w_ref[...].reshape(tf, K))
        g = gate_ref[...]
        sig = 1.0 / (1.0 + jnp.exp(-g))
        dup_ref[...] = (dact * (g * sig)).astype(BF16)
        dgate_ref[...] = (dact * up_ref[...] * (sig * (1.0 + g * (1.0 - sig)))).astype(BF16)

    tile = pl.BlockSpec((tm, tf), lambda j, i: (i, j))
    F = N_DEV * fb
    return pl.pallas_call(
        body,
        grid=(N_DEV // FF_GROUP, L // tm),
        in_specs=[pl.BlockSpec((tm, K), lambda j, i: (i, 0)), _ff_spec(w_down8, layer), tile, tile],
        out_specs=[tile, tile],
        out_shape=[jax.ShapeDtypeStruct((L, F), BF16), jax.ShapeDtypeStruct((L, F), BF16)],
        compiler_params=_params("arbitrary", "arbitrary"),
        name=name,
    )(dh, w_down8, gate, up)


def _pool_residual(scr, lo, tm, g, w, t):
    cols = slice(g * POOL_GROUP_DIM, (g + 1) * POOL_GROUP_DIM)
    cur = scr[lo:lo + tm, cols]
    s = cur
    for k in range(1, w):
        s = s + scr[lo - k:lo - k + tm, cols]
    cnt = jnp.minimum(t + 1, w).astype(F32)
    return s / cnt - cur


def pool_fwd(u, w_pool, scale, name):
    L, C = u.shape
    tm, halo = ROW_TILE, POOL_HALO

    def body(u_ref, halo_ref, w_ref, s_ref, y_ref, scr):
        i = pl.program_id(0)
        scr[0:halo, :] = jnp.where(i > 0, halo_ref[...], 0.0)
        scr[halo:halo + tm, :] = u_ref[...]
        t = i * tm + lax.broadcasted_iota(jnp.int32, (tm, POOL_GROUP_DIM), 0)
        for g, w in enumerate(POOL_WINDOWS):
            cols = slice(g * POOL_GROUP_DIM, (g + 1) * POOL_GROUP_DIM)
            p = _pool_residual(scr, halo, tm, g, w, t)
            y = _nn(p.astype(BF16), w_ref[g]) * s_ref[:, cols]
            y_ref[:, cols] = y.astype(y_ref.dtype)

    return pl.pallas_call(
        body,
        grid=(L // tm,),
        in_specs=[
            _row_spec(tm, C),
            pl.BlockSpec((halo, C), lambda i: (jnp.maximum(i * (tm // halo) - 1, 0), 0)),
            _const_spec(w_pool.shape),
            _const_spec((1, C)),
        ],
        out_specs=_row_spec(tm, C),
        out_shape=jax.ShapeDtypeStruct((L, 2 * C), BF16),
        scratch_shapes=[pltpu.VMEM((tm + halo, C), F32)],
        compiler_params=_params("arbitrary"),
        name=name,
    )(u, u, w_pool, scale)


def pool_bwd(u, dy, w_pool, scale, name):
    L, C = u.shape
    tm, halo = ROW_TILE, POOL_HALO
    n_tiles = L // tm
    last_halo = L // halo - 1

    def body(u_ref, uh_ref, dy_ref, dyh_ref, w_ref, s_ref, du_ref, dw_ref, ds_ref, scr_u, scr_q):
        i = pl.program_id(0)

        @pl.when(i == 0)
        def _():
            dw_ref[...] = jnp.zeros_like(dw_ref)
            ds_ref[...] = jnp.zeros_like(ds_ref)

        scr_u[0:halo, :] = jnp.where(i > 0, uh_ref[...], 0.0)
        scr_u[halo:halo + tm, :] = u_ref[...]
        t = i * tm + lax.broadcasted_iota(jnp.int32, (tm, POOL_GROUP_DIM), 0)
        th = (i + 1) * tm + lax.broadcasted_iota(jnp.int32, (halo, POOL_GROUP_DIM), 0)
        for g, w in enumerate(POOL_WINDOWS):
            cols = slice(g * POOL_GROUP_DIM, (g + 1) * POOL_GROUP_DIM)
            p = _pool_residual(scr_u, halo, tm, g, w, t).astype(BF16)
            wg = w_ref[g]
            sc = s_ref[:, cols]
            dy = dy_ref[:, cols]
            ds_ref[:, cols] += jnp.sum(dy * _nn(p, wg), axis=0, keepdims=True)
            dys = (dy * sc).astype(BF16)
            dw_ref[g] += _tn(p, dys)
            dp = _nt(dys, wg)
            dyh = jnp.where(i < n_tiles - 1, dyh_ref[:, cols], 0.0)
            dph = _nt((dyh * sc).astype(BF16), wg)
            scr_q[0:tm, cols] = dp / jnp.minimum(t + 1, w).astype(F32)
            scr_q[tm:tm + halo, cols] = dph / jnp.minimum(th + 1, w).astype(F32)
            acc = scr_q[0:tm, cols]
            for k in range(1, w):
                acc = acc + scr_q[k:k + tm, cols]
            du_ref[:, cols] = acc - dp

    return pl.pallas_call(
        body,
        grid=(n_tiles,),
        in_specs=[
            _row_spec(tm, C),
            pl.BlockSpec((halo, C), lambda i: (jnp.maximum(i * (tm // halo) - 1, 0), 0)),
            _row_spec(tm, C),
            pl.BlockSpec((halo, C), lambda i: (jnp.minimum((i + 1) * (tm // halo), last_halo), 0)),
            _const_spec(w_pool.shape),
            _const_spec((1, C)),
        ],
        out_specs=[_row_spec(tm, C), _const_spec(w_pool.shape), _const_spec((1, C))],
        out_shape=[
            jax.ShapeDtypeStruct((L, C), F32),
            jax.ShapeDtypeStruct(w_pool.shape, F32),
            jax.ShapeDtypeStruct((1, C), F32),
        ],
        scratch_shapes=[pltpu.VMEM((tm + halo, C), F32), pltpu.VMEM((tm + halo, C), F32)],
        compiler_params=_params("arbitrary"),
        name=name,
    )(u, u, dy, dy, w_pool, scale)


def _head_masks(rows):
    lane = lax.broadcasted_iota(jnp.int32, (rows, HEAD_PAD), 1)
    return lane, lane < QK_NOPE_DIM, (lane >= QK_NOPE_DIM) & (lane < QK_HEAD_DIM)


def _rope_swap(x, lane):
    half = QK_ROPE_DIM // 2
    swapped = jnp.where(lane < QK_NOPE_DIM + half, pltpu.roll(x, HEAD_PAD - half, 1), pltpu.roll(x, half, 1))
    return jnp.where((lane >= QK_NOPE_DIM) & (lane < QK_HEAD_DIM), swapped, 0.0)


def _seg_mean(v, m_n, m_r):
    mn = jnp.sum(jnp.where(m_n, v, 0.0), axis=1, keepdims=True) * (1.0 / QK_NOPE_DIM)
    mr = jnp.sum(jnp.where(m_r, v, 0.0), axis=1, keepdims=True) * (1.0 / QK_ROPE_DIM)
    return jnp.where(m_n, mn, mr)


def qk_prep_fwd(q, kv, kr, cosf, sins, gq, gkn, gkr, name):
    L = q.shape[0]
    tm = ROW_TILE
    W = MLA_HEADS * HEAD_PAD

    def body(q_ref, kv_ref, kr_ref, c_ref, s_ref, gq_ref, gkn_ref, gkr_ref, qo_ref, ko_ref, vo_ref):
        lane, m_n, m_r = _head_masks(tm)
        cosf_, sins_ = c_ref[...], s_ref[...]
        kr_ = kr_ref[...]
        rk = lax.rsqrt(_seg_mean(kr_ * kr_, m_n, m_r) + RMS_EPS)
        krn = kr_ * rk * gkr_ref[...]
        krf = krn * cosf_ + _rope_swap(krn, lane) * sins_
        for h in range(MLA_HEADS):
            cols = slice(h * HEAD_PAD, (h + 1) * HEAD_PAD)
            qh = q_ref[:, cols]
            qn = qh * lax.rsqrt(_seg_mean(qh * qh, m_n, m_r) + RMS_EPS) * gq_ref[...]
            qo_ref[:, cols] = (qn * cosf_ + _rope_swap(qn, lane) * sins_).astype(BF16)
            kh = jnp.where(m_n, kv_ref[:, cols], 0.0)
            rn = lax.rsqrt(jnp.sum(kh * kh, axis=1, keepdims=True) * (1.0 / QK_NOPE_DIM) + RMS_EPS)
            ko_ref[:, cols] = (kh * rn * gkn_ref[...] + krf).astype(BF16)
        for p in range(MLA_HEADS // 2):
            even = kv_ref[:, 2 * p * HEAD_PAD:(2 * p + 1) * HEAD_PAD]
            odd = kv_ref[:, (2 * p + 1) * HEAD_PAD:(2 * p + 2) * HEAD_PAD]
            pair = jnp.where(m_n, pltpu.roll(even, V_HEAD_DIM, 1), odd)
            vo_ref[:, p * LANES:(p + 1) * LANES] = pair.astype(BF16)

    vec = _const_spec((1, HEAD_PAD))
    return pl.pallas_call(
        body,
        grid=(L // tm,),
        in_specs=[_row_spec(tm, W), _row_spec(tm, W), _row_spec(tm, HEAD_PAD), _row_spec(tm, HEAD_PAD),
                  _row_spec(tm, HEAD_PAD), vec, vec, vec],
        out_specs=[_row_spec(tm, W), _row_spec(tm, W), _row_spec(tm, W // 2)],
        out_shape=[jax.ShapeDtypeStruct((L, W), BF16), jax.ShapeDtypeStruct((L, W), BF16),
                   jax.ShapeDtypeStruct((L, W // 2), BF16)],
        compiler_params=_params("arbitrary"),
        name=name,
    )(q, kv, kr, cosf, sins, gq, gkn, gkr)


def qk_prep_bwd(dqo, dko, dv, q, kv, kr, cosf, sins, gq, gkn, gkr, name):
    L = q.shape[0]
    tm = ROW_TILE
    W = MLA_HEADS * HEAD_PAD

    def body(dqo_ref, dko_ref, dv_ref, q_ref, kv_ref, kr_ref, c_ref, s_ref, gq_ref, gkn_ref, gkr_ref,
             dq_ref, dkv_ref, dkr_ref, dgq_ref, dgkn_ref, dgkr_ref):
        @pl.when(pl.program_id(0) == 0)
        def _():
            dgq_ref[...] = jnp.zeros_like(dgq_ref)
            dgkn_ref[...] = jnp.zeros_like(dgkn_ref)
            dgkr_ref[...] = jnp.zeros_like(dgkr_ref)

        lane, m_n, m_r = _head_masks(tm)
        cosf_, sins_ = c_ref[...], s_ref[...]
        dgq = jnp.zeros((1, HEAD_PAD), F32)
        dgkn = jnp.zeros((1, HEAD_PAD), F32)
        dkrf = jnp.zeros((tm, HEAD_PAD), F32)
        for h in range(MLA_HEADS):
            cols = slice(h * HEAD_PAD, (h + 1) * HEAD_PAD)
            dy = dqo_ref[:, cols]
            dqn = dy * cosf_ + _rope_swap(dy * sins_, lane)
            qh = q_ref[:, cols]
            rinv = lax.rsqrt(_seg_mean(qh * qh, m_n, m_r) + RMS_EPS)
            xh = qh * rinv
            dgq = dgq + jnp.sum(dqn * xh, axis=0, keepdims=True)
            dxh = dqn * gq_ref[...]
            dq_ref[:, cols] = rinv * (dxh - xh * _seg_mean(dxh * xh, m_n, m_r))
            dk = dko_ref[:, cols]
            dkrf = dkrf + dk
            kh = jnp.where(m_n, kv_ref[:, cols], 0.0)
            rn = lax.rsqrt(jnp.sum(kh * kh, axis=1, keepdims=True) * (1.0 / QK_NOPE_DIM) + RMS_EPS)
            xk = kh * rn
            dgkn = dgkn + jnp.sum(dk * xk, axis=0, keepdims=True)
            dxk = dk * gkn_ref[...]
            dkn = rn * (dxk - xk * (jnp.sum(dxk * xk, axis=1, keepdims=True) * (1.0 / QK_NOPE_DIM)))
            dvp = dv_ref[:, (h // 2) * LANES:(h // 2 + 1) * LANES]
            dvh = pltpu.roll(dvp, V_HEAD_DIM, 1) if h % 2 == 0 else dvp
            dkv_ref[:, cols] = jnp.where(m_n, dkn, dvh)
        kr_ = kr_ref[...]
        rk = lax.rsqrt(_seg_mean(kr_ * kr_, m_n, m_r) + RMS_EPS)
        xr = kr_ * rk
        dkrn = dkrf * cosf_ + _rope_swap(dkrf * sins_, lane)
        dgkr_ref[...] += jnp.sum(dkrn * xr, axis=0, keepdims=True)
        dxr = dkrn * gkr_ref[...]
        dkr_ref[...] = rk * (dxr - xr * _seg_mean(dxr * xr, m_n, m_r))
        dgq_ref[...] += dgq
        dgkn_ref[...] += dgkn

    vec = _const_spec((1, HEAD_PAD))
    return pl.pallas_call(
        body,
        grid=(L // tm,),
        in_specs=[_row_spec(tm, W), _row_spec(tm, W), _row_spec(tm, W // 2), _row_spec(tm, W), _row_spec(tm, W),
                  _row_spec(tm, HEAD_PAD), _row_spec(tm, HEAD_PAD), _row_spec(tm, HEAD_PAD), vec, vec, vec],
        out_specs=[_row_spec(tm, W), _row_spec(tm, W), _row_spec(tm, HEAD_PAD), vec, vec, vec],
        out_shape=[jax.ShapeDtypeStruct((L, W), F32), jax.ShapeDtypeStruct((L, W), F32),
                   jax.ShapeDtypeStruct((L, HEAD_PAD), F32)] + [jax.ShapeDtypeStruct((1, HEAD_PAD), F32)] * 3,
        compiler_params=_params("arbitrary"),
        name=name,
    )(dqo, dko, dv, q, kv, kr, cosf, sins, gq, gkn, gkr)


def attn_fwd(qp, kp, v, cat, name, gather=()):
    L = qp.shape[0]
    tq = ROW_TILE
    n_i = L // tq
    pair_w = 2 * HEAD_PAD
    n_pairs = MLA_HEADS // 2
    n_g = len(gather)

    def body(q_ref, k_ref, v_ref, cat_in, *rest):
        del cat_in
        o_ref, lse_ref, cat_ref = rest[n_g:n_g + 3]
        i = pl.program_id(1)
        if n_g:
            pair = pl.program_id(0)
            start, forward, finish = _gather_steps(rest[:n_g], rest[n_g + 3:2 * n_g + 3], *rest[2 * n_g + 3:])
            pl.when((pair == 0) & (i == 0))(start)
            pl.when((pair == n_pairs // 2) & (i == 0))(forward)
        left = lax.broadcasted_iota(jnp.int32, (tq, LANES), 1) < V_HEAD_DIM
        row = lax.broadcasted_iota(jnp.int32, (tq, tq), 0)
        col = lax.broadcasted_iota(jnp.int32, (tq, tq), 1)

        def step(j, carry, masked, width=1):
            rows = pl.ds(pl.multiple_of(j * (tq * width), tq), tq * width)
            vv = v_ref[rows, :]
            out = []
            for hh in range(2):
                cols = slice(hh * HEAD_PAD, (hh + 1) * HEAD_PAD)
                m, l, acc = carry[hh]
                s = _nt(q_ref[:, cols], k_ref[rows, cols]) * ATTN_SCALE
                if masked:
                    s = jnp.where(col <= row, s, -jnp.inf)
                m_new = jnp.maximum(m, jnp.max(s, axis=1, keepdims=True))
                alpha = jnp.exp(m - m_new)
                p = jnp.exp(s - m_new)
                l = alpha * l + jnp.sum(p, axis=1, keepdims=True)
                acc = alpha * acc + _nn(p.astype(BF16), vv)
                out.append((m_new, l, acc))
            return tuple(out)

        one = (jnp.full((tq, 1), -jnp.inf, F32), jnp.zeros((tq, 1), F32), jnp.zeros((tq, LANES), F32))
        doubles = jnp.right_shift(i, 1)
        carry = lax.fori_loop(0, doubles, functools.partial(step, masked=False, width=2), (one, one))
        carry = lax.fori_loop(2 * doubles, i, functools.partial(step, masked=False), carry)
        (m0, l0, a0), (m1, l1, a1) = step(i, carry, True)
        o = jnp.where(left, a0 / l0, a1 / l1)
        o_ref[...] = o
        cat_ref[...] = o.astype(BF16)
        lse_ref[...] = jnp.where(left, m0 + jnp.log(l0), m1 + jnp.log(l1))
        if n_g:
            pl.when((pair == n_pairs - 1) & (i == n_i - 1))(finish)

    return pl.pallas_call(
        body,
        grid=(n_pairs, n_i),
        in_specs=[
            pl.BlockSpec((tq, pair_w), lambda p, i: (i, p)),
            pl.BlockSpec((L, pair_w), lambda p, i: (0, p)),
            pl.BlockSpec((L, LANES), lambda p, i: (0, p)),
            _ANY,
        ] + [_ANY] * n_g,
        out_specs=[
            pl.BlockSpec((tq, LANES), lambda p, i: (i, p)),
            pl.BlockSpec((tq, LANES), lambda p, i: (i, p)),
            pl.BlockSpec((tq, LANES), lambda p, i: (i, n_pairs + p)),
        ] + [_ANY] * n_g,
        out_shape=[jax.ShapeDtypeStruct((L, n_pairs * LANES), F32), jax.ShapeDtypeStruct((L, n_pairs * LANES), F32),
                   jax.ShapeDtypeStruct(cat.shape, cat.dtype)]
        + [jax.ShapeDtypeStruct((N_DEV,) + g.shape, g.dtype) for g in gather],
        scratch_shapes=_gather_scratch(n_g) if n_g else [],
        input_output_aliases={3: 2},
        compiler_params=_params("arbitrary", "arbitrary"),
        name=name,
    )(qp, kp, v, cat, *gather)


def attn_delta(do, o, name):
    L, C = o.shape
    tm = ROW_TILE

    def body(do_ref, o_ref, d_ref):
        left = lax.broadcasted_iota(jnp.int32, (tm, LANES), 1) < V_HEAD_DIM
        for p in range(C // LANES):
            cols = slice(p * LANES, (p + 1) * LANES)
            prod = do_ref[:, cols] * o_ref[:, cols]
            d0 = jnp.sum(jnp.where(left, prod, 0.0), axis=1, keepdims=True)
            d1 = jnp.sum(jnp.where(left, 0.0, prod), axis=1, keepdims=True)
            d_ref[:, cols] = jnp.where(left, d0, d1)

    return pl.pallas_call(
        body,
        grid=(L // tm,),
        in_specs=[_row_spec(tm, C), _row_spec(tm, C)],
        out_specs=_row_spec(tm, C),
        out_shape=jax.ShapeDtypeStruct((L, C), F32),
        compiler_params=_params("arbitrary"),
        name=name,
    )(do, o)


def attn_bwd(qp, kp, v, do, lse, delta, name, exchange=()):
    L = qp.shape[0]
    tq = ROW_TILE
    n_q = L // tq
    pair_w = 2 * HEAD_PAD
    n_pairs = MLA_HEADS // 2
    n_x = len(exchange)

    def body(q_ref, k_ref, v_ref, do_ref, lse_ref, dl_ref, *rest):
        dq_ref, dk_ref, dv_ref = rest[n_x:n_x + 3]
        j = pl.program_id(1)
        if n_x:
            pair = pl.program_id(0)
            start, finish = _chip_exchange_steps(rest[:n_x], rest[n_x + 3:2 * n_x + 3], *rest[2 * n_x + 3:])
            pl.when((pair == 0) & (j == 0))(start)

        @pl.when(j == 0)
        def _():
            dq_ref[...] = jnp.zeros_like(dq_ref)

        dk_ref[...] = jnp.zeros_like(dk_ref)
        dv_ref[...] = jnp.zeros_like(dv_ref)
        row = lax.broadcasted_iota(jnp.int32, (tq, tq), 0)
        col = lax.broadcasted_iota(jnp.int32, (tq, tq), 1)

        def step(t, carry, masked, width=1, first=0):
            rows = pl.ds(pl.multiple_of((first + t * width) * tq, tq), tq * width)
            do = do_ref[rows, :]
            left = lax.broadcasted_iota(jnp.int32, do.shape, 1) < V_HEAD_DIM
            vv = v_ref[...]
            dv = None
            for hh in range(2):
                cols = slice(hh * HEAD_PAD, (hh + 1) * HEAD_PAD)
                stat = slice(hh * V_HEAD_DIM, hh * V_HEAD_DIM + 1)
                q = q_ref[rows, cols]
                k = k_ref[:, cols]
                dom = jnp.where(left if hh == 0 else jnp.logical_not(left), do, 0.0).astype(BF16)
                s = _nt(q, k) * ATTN_SCALE
                p = jnp.exp(s - lse_ref[rows, stat])
                if masked:
                    p = jnp.where(col <= row, p, 0.0)
                dp = _nt(dom, vv)
                ds = (p * (dp - dl_ref[rows, stat]) * ATTN_SCALE).astype(BF16)
                dq_ref[rows, cols] += _nn(ds, k)
                dk_ref[:, cols] += _tn(ds, q)
                t = _tn(p.astype(BF16), dom)
                dv = t if dv is None else dv + t
            dv_ref[...] += dv
            return carry

        step(j, 0, True)
        doubles = jnp.right_shift(n_q - 1 - j, 1)
        lax.fori_loop(0, doubles, functools.partial(step, masked=False, width=2, first=j + 1), 0)
        lax.fori_loop(j + 1 + 2 * doubles, n_q, functools.partial(step, masked=False), 0)
        if n_x:
            pl.when((pair == n_pairs - 1) & (j == n_q - 1))(finish)

    return pl.pallas_call(
        body,
        grid=(n_pairs, n_q),
        in_specs=[
            pl.BlockSpec((L, pair_w), lambda p, j: (0, p)),
            pl.BlockSpec((tq, pair_w), lambda p, j: (j, p)),
            pl.BlockSpec((tq, LANES), lambda p, j: (j, p)),
            pl.BlockSpec((L, LANES), lambda p, j: (0, p)),
            pl.BlockSpec((L, LANES), lambda p, j: (0, p)),
            pl.BlockSpec((L, LANES), lambda p, j: (0, p)),
        ] + [_ANY] * n_x,
        out_specs=[
            pl.BlockSpec((L, pair_w), lambda p, j: (0, p)),
            pl.BlockSpec((tq, pair_w), lambda p, j: (j, p)),
            pl.BlockSpec((tq, LANES), lambda p, j: (j, p)),
        ] + [_ANY] * n_x,
        out_shape=[
            jax.ShapeDtypeStruct((L, n_pairs * pair_w), F32),
            jax.ShapeDtypeStruct((L, n_pairs * pair_w), F32),
            jax.ShapeDtypeStruct((L, n_pairs * LANES), F32),
        ] + [jax.ShapeDtypeStruct(e.shape, e.dtype) for e in exchange],
        scratch_shapes=_chip_exchange_scratch(n_x) if n_x else [],
        compiler_params=_params("arbitrary", "arbitrary"),
        name=name,
    )(qp, kp, v, do, lse, delta, *exchange)


def loss_head(h, target, n_real, name):
    L, D = h.shape
    tm = ROW_TILE

    def body(h_ref, t_ref, dh_ref, sq_ref):
        i = pl.program_id(0)

        @pl.when(i == 0)
        def _():
            sq_ref[...] = jnp.zeros_like(sq_ref)

        t = i * tm + lax.broadcasted_iota(jnp.int32, (tm, D), 0)
        real = (t >= N_META) & (t < N_META + n_real)
        diff = jnp.where(real, h_ref[...] - t_ref[...], 0.0)
        dh_ref[...] = diff * (1.0 / D)
        sq_ref[...] += jnp.sum(diff * diff, axis=0, keepdims=True)

    return pl.pallas_call(
        body,
        grid=(L // tm,),
        in_specs=[_row_spec(tm, D), _row_spec(tm, D)],
        out_specs=[_row_spec(tm, D), _const_spec((1, D))],
        out_shape=[jax.ShapeDtypeStruct((L, D), F32), jax.ShapeDtypeStruct((1, D), F32)],
        compiler_params=_params("arbitrary"),
        name=name,
    )(h, target)


def _mesh_position():
    x, y, c = lax.axis_index("x"), lax.axis_index("y"), lax.axis_index("c")
    return x, y, c, 4 * x + 2 * y + c


def _flip(x, y, c, k):
    px = 1 - x if k & 4 else x
    py = 1 - y if k & 2 else y
    pc = 1 - c if k & 1 else c
    return (px, py, pc), 4 * px + 2 * py + pc


def _other_chips(x, y):
    return [(1 - x, y), (x, 1 - y), (1 - x, 1 - y)]


def _dev_index(px, py, pc):
    return 4 * px + 2 * py + pc


_ANY = pl.BlockSpec(memory_space=pl.ANY)


def cast_bf16(arrays, name):
    n = len(arrays)
    depth = arrays[0].shape[0]

    def body(*refs):
        for k, src in enumerate(refs[:n]):
            for l in range(depth):
                refs[n + k * depth + l][0] = src[l].astype(BF16)

    return pl.pallas_call(
        body,
        out_shape=[jax.ShapeDtypeStruct((1,) + a.shape[1:], BF16) for a in arrays for _ in range(depth)],
        compiler_params=pltpu.CompilerParams(vmem_limit_bytes=VMEM_LIMIT),
        name=name,
    )(*arrays)


def _gather_steps(srcs, dsts, send_sems, recv_sems, local_sems):
    n = len(srcs)
    x, y, c, me = _mesh_position()
    sibling = (x, y, 1 - c)
    chips = _other_chips(x, y)

    def copy(k, b, block, to, from_src=False):
        dst = dsts[b].at[block]
        return pltpu.make_async_remote_copy(
            src_ref=srcs[b] if from_src else dst, dst_ref=dst, send_sem=send_sems.at[k * n + b],
            recv_sem=recv_sems.at[k * n + b], device_id=to, device_id_type=pl.DeviceIdType.MESH)

    def own():
        return [pltpu.make_async_copy(srcs[b], dsts[b].at[me], local_sems.at[b]) for b in range(n)]

    def first():
        out = []
        for b in range(n):
            out.append(copy(0, b, me, sibling, from_src=True))
            out += [copy(1 + q, b, me, (*chip, c), from_src=True) for q, chip in enumerate(chips)]
        return out

    def passed():
        return [copy(4 + q, b, _dev_index(*chip, c), sibling) for q, chip in enumerate(chips) for b in range(n)]

    def start():
        for cp in own() + first():
            cp.start()

    def forward():
        for q, chip in enumerate(chips):
            for b in range(n):
                copy(1 + q, b, _dev_index(*chip, c), sibling).wait_recv()
        for cp in passed():
            cp.start()

    def finish():
        for b in range(n):
            copy(0, b, _dev_index(x, y, 1 - c), sibling).wait_recv()
            for q, chip in enumerate(chips):
                copy(4 + q, b, _dev_index(*chip, 1 - c), sibling).wait_recv()
        for cp in first() + passed():
            cp.wait_send()
        for cp in own():
            cp.wait()

    return start, forward, finish


def _gather_scratch(n):
    copies = N_DEV - 1
    return [pltpu.SemaphoreType.DMA((copies * n,)), pltpu.SemaphoreType.DMA((copies * n,)),
            pltpu.SemaphoreType.DMA((n,))]


def all_gather(payloads, name):
    n = len(payloads)

    def body(*refs):
        start, forward, finish = _gather_steps(refs[:n], refs[n:2 * n], *refs[2 * n:])
        start()
        forward()
        finish()

    return pl.pallas_call(
        body,
        in_specs=[_ANY] * n,
        out_specs=[_ANY] * n,
        out_shape=[jax.ShapeDtypeStruct((N_DEV,) + p.shape, p.dtype) for p in payloads],
        scratch_shapes=_gather_scratch(n),
        name=name,
    )(*payloads)


def exchange_pair(bigs, small, name):
    n = len(bigs)
    has_small = small is not None
    n_big = N_CHIPS * n
    n_sems = n_big + (N_DEV - 1 if has_small else 0)

    def body(*refs):
        big_refs = refs[:n]
        pos = n + has_small
        sib_refs = refs[pos:pos + n]
        send_sems, recv_sems, local_sem = refs[-3:]
        x, y, c, me = _mesh_position()
        sibling = (x, y, 1 - c)
        copies = []
        for b in range(n):
            for q in range(N_CHIPS):
                copies.append(pltpu.make_async_remote_copy(
                    src_ref=big_refs[b].at[_dev_index(q // 2, q % 2, 1 - c)], dst_ref=sib_refs[b].at[q],
                    send_sem=send_sems.at[b * N_CHIPS + q], recv_sem=recv_sems.at[b * N_CHIPS + q],
                    device_id=sibling, device_id_type=pl.DeviceIdType.MESH))
        waits = list(copies)
        if has_small:
            small_ref, gsmall_ref = refs[n], refs[pos + n]
            own = pltpu.make_async_copy(small_ref.at[me], gsmall_ref.at[me], local_sem.at[0])
            own.start()
            for k in range(1, N_DEV):
                peer, peer_idx = _flip(x, y, c, k)
                s = n_big + k - 1
                copies.append(pltpu.make_async_remote_copy(
                    src_ref=small_ref.at[peer_idx], dst_ref=gsmall_ref.at[me], send_sem=send_sems.at[s],
                    recv_sem=recv_sems.at[s], device_id=peer, device_id_type=pl.DeviceIdType.MESH))
                waits.append(pltpu.make_async_remote_copy(
                    src_ref=small_ref.at[peer_idx], dst_ref=gsmall_ref.at[peer_idx], send_sem=send_sems.at[s],
                    recv_sem=recv_sems.at[s], device_id=peer, device_id_type=pl.DeviceIdType.MESH))
        for cp in copies:
            cp.start()
        for cp in waits:
            cp.wait_recv()
        for cp in copies:
            cp.wait_send()
        if has_small:
            own.wait()

    out_shape = [jax.ShapeDtypeStruct((N_CHIPS,) + b.shape[1:], b.dtype) for b in bigs]
    args = list(bigs)
    if has_small:
        out_shape.append(jax.ShapeDtypeStruct(small.shape, small.dtype))
        args.append(small)
    return pl.pallas_call(
        body,
        in_specs=[_ANY] * len(args),
        out_specs=[_ANY] * len(out_shape),
        out_shape=out_shape,
        scratch_shapes=[pltpu.SemaphoreType.DMA((n_sems,)), pltpu.SemaphoreType.DMA((n_sems,)),
                        pltpu.SemaphoreType.DMA((1,))],
        name=name,
    )(*args)


def pair_sum(bigs, from_sibling, core, name):
    n = len(bigs)

    def body(core_ref, *refs):
        del core_ref
        for mine, sib, out in zip(refs[:n], refs[n:2 * n], refs[2 * n:]):
            out[...] = (mine[...].astype(F32) + sib[...].astype(F32)).astype(out.dtype)

    def slot(shape, picked):
        zeros = (0,) * (len(shape) - 1)
        if picked:
            return pl.BlockSpec((None,) + shape[1:], lambda q, core_ref: (2 * q + core_ref[0],) + zeros)
        return pl.BlockSpec((None,) + shape[1:], lambda q, core_ref: (q,) + zeros)

    grid_spec = pltpu.PrefetchScalarGridSpec(
        num_scalar_prefetch=1,
        grid=(N_CHIPS,),
        in_specs=[slot(b.shape, True) for b in bigs] + [slot(s.shape, False) for s in from_sibling],
        out_specs=[slot(s.shape, False) for s in from_sibling],
    )
    return pl.pallas_call(
        body,
        grid_spec=grid_spec,
        out_shape=[jax.ShapeDtypeStruct(s.shape, s.dtype) for s in from_sibling],
        compiler_params=_params("arbitrary"),
        name=name,
    )(core, *bigs, *from_sibling)


def _chip_exchange_steps(part_refs, got_refs, send_sems, recv_sems, local_sems):
    n = len(part_refs)
    x, y, c, me = _mesh_position()
    mine = 2 * x + y

    def copies(landing):
        out = []
        for q, (px, py) in enumerate(_other_chips(x, y)):
            theirs = 2 * px + py
            for b in range(n):
                out.append(pltpu.make_async_remote_copy(
                    src_ref=part_refs[b].at[theirs], dst_ref=got_refs[b].at[theirs if landing else mine],
                    send_sem=send_sems.at[q * n + b], recv_sem=recv_sems.at[q * n + b],
                    device_id=(px, py, c), device_id_type=pl.DeviceIdType.MESH))
        return out

    def own():
        return [pltpu.make_async_copy(part_refs[b].at[mine], got_refs[b].at[mine], local_sems.at[b])
                for b in range(n)]

    def start():
        for cp in own() + copies(False):
            cp.start()

    def finish():
        for cp in copies(True):
            cp.wait_recv()
        for cp in copies(False):
            cp.wait_send()
        for cp in own():
            cp.wait()

    return start, finish


def _chip_exchange_scratch(n):
    return [pltpu.SemaphoreType.DMA(((N_CHIPS - 1) * n,)), pltpu.SemaphoreType.DMA(((N_CHIPS - 1) * n,)),
            pltpu.SemaphoreType.DMA((n,))]


def exchange_chips(partials, name):
    n = len(partials)

    def body(*refs):
        start, finish = _chip_exchange_steps(refs[:n], refs[n:2 * n], *refs[2 * n:])
        start()
        finish()

    return pl.pallas_call(
        body,
        in_specs=[_ANY] * n,
        out_specs=[_ANY] * n,
        out_shape=[jax.ShapeDtypeStruct(p.shape, p.dtype) for p in partials],
        scratch_shapes=_chip_exchange_scratch(n),
        name=name,
    )(*partials)


def _adamw_math(g, w, m, v):
    m_new = ADAM_B1 * m + (1.0 - ADAM_B1) * g
    v_new = ADAM_B2 * v + (1.0 - ADAM_B2) * (g * g)
    m_hat = m_new / (1.0 - ADAM_B1 ** ADAM_STEP)
    v_hat = v_new / (1.0 - ADAM_B2 ** ADAM_STEP)
    delta = -ADAM_LR * (m_hat / (jnp.sqrt(v_hat) + ADAM_EPS) + ADAM_WD * w)
    return delta, m_new, v_new


def _chip_total(p_ref):
    g = p_ref[0].astype(F32)
    for q in range(1, N_CHIPS):
        g = g + p_ref[q].astype(F32)
    return g


def chip_sum(parts, name):
    depth = len(parts)

    def body(*refs):
        for l in range(depth):
            refs[depth][l] = _chip_total(refs[l])

    return pl.pallas_call(
        body,
        out_shape=jax.ShapeDtypeStruct((depth,) + parts[0].shape[1:], F32),
        compiler_params=pltpu.CompilerParams(vmem_limit_bytes=VMEM_LIMIT),
        name=name,
    )(*parts)


def adamw_shard(parts, w, m, v, name, grad=None):
    depth = w.shape[0]
    n_in = 1 if grad is not None else depth

    def body(*refs):
        w_ref, m_ref, v_ref, g_ref, d_ref, mo_ref, vo_ref = refs[n_in:]
        for l in range(depth):
            g = refs[0][l] if grad is not None else _chip_total(refs[l])
            delta, m_new, v_new = _adamw_math(g, w_ref[l], m_ref[l], v_ref[l])
            g_ref[l] = g
            d_ref[l] = delta
            mo_ref[l] = m_new
            vo_ref[l] = v_new

    return pl.pallas_call(
        body,
        out_shape=[jax.ShapeDtypeStruct(w.shape, F32)] * 4,
        compiler_params=pltpu.CompilerParams(vmem_limit_bytes=VMEM_LIMIT),
        name=name,
    )(*([grad] if grad is not None else parts), w, m, v)


def adamw_packed(parts, w, m, v, name):
    n_slots = parts.shape[0]

    def body(p_ref, w_ref, m_ref, v_ref, g_ref, d_ref, mo_ref, vo_ref):
        g = p_ref[0]
        for s in range(1, n_slots):
            g = g + p_ref[s]
        delta, m_new, v_new = _adamw_math(g, w_ref[...], m_ref[...], v_ref[...])
        g_ref[...] = g
        d_ref[...] = delta
        mo_ref[...] = m_new
        vo_ref[...] = v_new

    return pl.pallas_call(
        body,
        out_shape=[jax.ShapeDtypeStruct(w.shape, F32)] * 4,
        compiler_params=pltpu.CompilerParams(vmem_limit_bytes=VMEM_LIMIT),
        name=name,
    )(parts, w, m, v)


SHARDED = ("w_in", "w_q_b", "w_kv_b", "w_out", "w_gate", "w_up", "w_down")
TRANSPOSED = ("w_in", "w_gate", "w_up")
BEFORE_ATTENTION = ("w_in", "w_q_b", "w_kv_b")
FFN_WEIGHTS = ("w_gate", "w_up", "w_down")
REPLICATED = ("attn_norm_g", "w_pool", "pool_scale", "q_a_norm_g", "kv_a_norm_g", "q_norm_g", "k_norm_g",
              "ffn_norm_g")


def _pack_flat(arrays, rows):
    flat = jnp.concatenate([a.reshape(-1) for a in arrays])
    return jnp.pad(flat, (0, rows * LANES - flat.shape[0])).reshape(rows, LANES)


def _unpack_flat(packed, shapes):
    flat, out, at = packed.reshape(-1), [], 0
    for shp in shapes:
        n = 1
        for d in shp:
            n *= d
        out.append(flat[at:at + n].reshape(shp))
        at += n
    return out


def _rope_lane_tables(length):
    inv = 1.0 / (ROPE_THETA ** (jnp.arange(0, QK_ROPE_DIM, 2, dtype=F32) / QK_ROPE_DIM))
    ang = jnp.arange(length, dtype=F32)[:, None] * inv[None, :]
    cos, sin = jnp.cos(ang), jnp.sin(ang)
    ones = jnp.ones((length, QK_NOPE_DIM), F32)
    zeros = jnp.zeros((length, QK_NOPE_DIM), F32)
    tail = HEAD_PAD - QK_HEAD_DIM
    cosf = jnp.concatenate([ones, cos, cos, ones[:, :tail]], axis=1)
    sins = jnp.concatenate([zeros, -sin, sin, zeros[:, :tail]], axis=1)
    return cosf, sins


def _pad_lanes(vec, at, width=HEAD_PAD):
    return jnp.pad(vec, (at, width - at - vec.shape[0])).reshape(1, width)


def kernel(x, meta_tokens, attn_norm_g, w_in, w_pool, pool_scale, q_a_norm_g, w_q_b, kv_a_norm_g, w_kv_b, q_norm_g, k_norm_g, w_out, ffn_norm_g, w_gate, w_up, w_down, loss_target, m_meta_tokens, m_attn_norm_g, m_w_in, m_w_pool, m_pool_scale, m_q_a_norm_g, m_w_q_b, m_kv_a_norm_g, m_w_kv_b, m_q_norm_g, m_k_norm_g, m_w_out, m_ffn_norm_g, m_w_gate, m_w_up, m_w_down, v_meta_tokens, v_attn_norm_g, v_w_in, v_w_pool, v_pool_scale, v_q_a_norm_g, v_w_q_b, v_kv_a_norm_g, v_w_kv_b, v_q_norm_g, v_k_norm_g, v_w_out, v_ffn_norm_g, v_w_gate, v_w_up, v_w_down):
    weights = dict(meta_tokens=meta_tokens, attn_norm_g=attn_norm_g, w_in=w_in, w_pool=w_pool, pool_scale=pool_scale,
                   q_a_norm_g=q_a_norm_g, w_q_b=w_q_b, kv_a_norm_g=kv_a_norm_g, w_kv_b=w_kv_b, q_norm_g=q_norm_g,
                   k_norm_g=k_norm_g, w_out=w_out, ffn_norm_g=ffn_norm_g, w_gate=w_gate, w_up=w_up, w_down=w_down)
    mom1 = dict(meta_tokens=m_meta_tokens, attn_norm_g=m_attn_norm_g, w_in=m_w_in, w_pool=m_w_pool,
                pool_scale=m_pool_scale, q_a_norm_g=m_q_a_norm_g, w_q_b=m_w_q_b, kv_a_norm_g=m_kv_a_norm_g,
                w_kv_b=m_w_kv_b, q_norm_g=m_q_norm_g, k_norm_g=m_k_norm_g, w_out=m_w_out, ffn_norm_g=m_ffn_norm_g,
                w_gate=m_w_gate, w_up=m_w_up, w_down=m_w_down)
    mom2 = dict(meta_tokens=v_meta_tokens, attn_norm_g=v_attn_norm_g, w_in=v_w_in, w_pool=v_w_pool,
                pool_scale=v_pool_scale, q_a_norm_g=v_q_a_norm_g, w_q_b=v_w_q_b, kv_a_norm_g=v_kv_a_norm_g,
                w_kv_b=v_w_kv_b, q_norm_g=v_q_norm_g, k_norm_g=v_k_norm_g, w_out=v_w_out, ffn_norm_g=v_ffn_norm_g,
                w_gate=v_w_gate, w_up=v_w_up, w_down=v_w_down)
    order = ("meta_tokens", "attn_norm_g", "w_in", "w_pool", "pool_scale", "q_a_norm_g", "w_q_b", "kv_a_norm_g",
             "w_kv_b", "q_norm_g", "k_norm_g", "w_out", "ffn_norm_g", "w_gate", "w_up", "w_down")
    depth = w_in.shape[0]
    seq = x.shape[1]
    length = N_META + seq
    lp = -(-length // ROW_TILE) * ROW_TILE
    in_cols = w_in.shape[2]

    local = {n: (jnp.swapaxes(weights[n], 1, 2) if n in TRANSPOSED else weights[n]) for n in SHARDED}
    cast = cast_bf16([local[n] for n in SHARDED], "cast_weights")
    shards = [{n: cast[k * depth + l] for k, n in enumerate(SHARDED)} for l in range(depth)]
    gathered = all_gather([shards[0][n] for n in BEFORE_ATTENTION] + [meta_tokens], "all_gather")
    g8l = [dict(zip(BEFORE_ATTENTION, gathered[:-1]))] + [{} for _ in range(depth - 1)]
    meta_full = jnp.transpose(gathered[-1], (1, 0, 2)).reshape(N_META, D_MODEL)

    s1, s2, s3 = POOL_WIDTH, POOL_WIDTH + Q_LORA_RANK, POOL_WIDTH + Q_LORA_RANK + KV_LORA_RANK
    zpad = lambda n: jnp.zeros((1, n, D_MODEL), BF16)

    def padded_in_proj(w8):
        w_in_t = jnp.swapaxes(w8, 0, 1).reshape(1, N_DEV * in_cols, D_MODEL)
        return jnp.concatenate([w_in_t[:, :s3], zpad(QK_NOPE_DIM), w_in_t[:, s3:],
                                zpad(HEAD_PAD - QK_HEAD_DIM)], axis=1)

    w_in_ps = [None] * depth
    w_pool_b = w_pool.astype(BF16)
    vw = MLA_HEADS * V_HEAD_DIM

    cosf, sins = _rope_lane_tables(lp)
    row = lambda a, l: a[l].reshape(1, -1)

    h = jnp.concatenate([meta_full, x[0], jnp.zeros((lp - length, D_MODEL), F32)], axis=0)
    target = jnp.pad(loss_target[0], ((N_META, lp - length), (0, 0)))
    saved = []
    for l in range(depth):
        gq = _pad_lanes(q_norm_g[l], 0)
        gkn = _pad_lanes(k_norm_g[l, :QK_NOPE_DIM], 0)
        gkr = _pad_lanes(k_norm_g[l, QK_NOPE_DIM:], QK_NOPE_DIM)
        g8 = g8l[l]
        w_in_ps[l] = padded_in_proj(g8["w_in"])
        a, u, c_q, c_kv, kr = norm_mm(
            h, row(attn_norm_g, l), w_in_ps[l], 0,
            [(0, s1), (s1, Q_LORA_RANK), (s2, KV_LORA_RANK), (s3, HEAD_PAD)], [F32] * 4, "in_proj")
        cat = pool_fwd(u, w_pool_b[l], row(pool_scale, l), "pool_fwd")
        qn, q = norm_mm_heads(c_q, row(q_a_norm_g, l), g8["w_q_b"], 0, "q_proj")
        kvn, kv = norm_mm_heads(c_kv, row(kv_a_norm_g, l), g8["w_kv_b"], 0, "kv_proj")
        qp, kp, v = qk_prep_fwd(q, kv, kr, cosf, sins, gq, gkn, gkr, "qk_prep_fwd")
        riders = [(l, n) for n in SHARDED if n not in g8] + [(l + 1, n) for n in SHARDED if l + 1 < depth]
        if riders:
            o, lse, cat, *arrived = attn_fwd(qp, kp, v, cat, "attn_fwd_gather",
                                             gather=[shards[k][n] for k, n in riders])
            for (k, n), w8 in zip(riders, arrived):
                g8l[k][n] = w8
        else:
            o, lse, cat = attn_fwd(qp, kp, v, cat, "attn_fwd")
        h_mid, g = mm_res(cat, g8["w_out"], h, 0, "out_proj", gamma=row(ffn_norm_g, l))
        gate, up, act = ffn_up(g, g8["w_gate"], g8["w_up"], 0, "ffn_up")
        h_next = mm_res(act, g8["w_down"], h_mid, 0, "ffn_down")[0]
        saved.append(dict(h=h, a=a, u=u, c_q=c_q, c_kv=c_kv, kr=kr, qn=qn, q=q, kvn=kvn, kv=kv, v=v, qp=qp, kp=kp,
                          o=o, lse=lse, cat=cat, h_mid=h_mid, g=g, gate=gate, up=up, act=act,
                          gq=gq, gkn=gkn, gkr=gkr))
        h = h_next

    dh, sq = loss_head(h, target, seq, "loss_head")
    loss = lax.psum(0.5 / D_MODEL * jnp.sum(sq), ("x", "y", "c"))

    core = lax.axis_index("c").astype(jnp.int32).reshape(1)
    small_grads = {n: [None] * depth for n in REPLICATED}
    pending = []
    got = [{} for _ in range(depth)]

    def pair_reduce(l, names, slots, small):
        bigs = [slots[n] for n in names]
        if small is None:
            from_sibling, got_small = exchange_pair(bigs, None, "grad_exchange_pair"), None
        else:
            *from_sibling, got_small = exchange_pair(bigs, small, "grad_exchange_pair_small")
        partial = pair_sum(bigs, list(from_sibling), core, "grad_pair_sum")
        return [(l, n, p) for n, p in zip(names, partial)], got_small

    for l in reversed(range(depth)):
        s = saved[l]
        g8 = g8l[l]
        w_in_p = w_in_ps[l]
        slots = {}
        dgate, dup = ffn_bwd_act(dh, g8["w_down"], 0, s["gate"], s["up"], "ffn_bwd_act")
        slot_of = lambda n, full: full.reshape(g8[n].shape[:1] + g8[n].shape[2:])
        slots["w_down"] = slot_of("w_down", mm_tn(s["act"], dh, "dw_down", out_dtype=BF16))
        slots["w_gate"] = slot_of("w_gate", mm_tn(dgate, s["g"], "dw_gate", out_dtype=BF16))
        slots["w_up"] = slot_of("w_up", mm_tn(dup, s["g"], "dw_up", out_dtype=BF16))
        pending += pair_reduce(l, FFN_WEIGHTS, slots, None)[0]
        dh_mid, dg_ffn = mm_nt_normbwd([("gathered", dgate, g8["w_gate"]), ("gathered", dup, g8["w_up"])],
                                       s["h_mid"], row(ffn_norm_g, l), dh, 0, "ffn_bwd_in")
        small_grads["ffn_norm_g"][l] = dg_ffn[0]
        slots["w_out"] = slot_of("w_out", mm_tn(s["cat"], dh_mid, "dw_out", out_dtype=BF16))
        dy_pool, do = mm_nt(dh_mid, g8["w_out"], 0, [(0, POOL_WIDTH), (POOL_WIDTH, vw)], "out_proj_bwd")
        du, dw_pool, dscale = pool_bwd(s["u"], dy_pool, w_pool_b[l], row(pool_scale, l), "pool_bwd")
        small_grads["w_pool"][l] = dw_pool
        small_grads["pool_scale"][l] = dscale[0]
        delta = attn_delta(do, s["o"], "attn_delta")
        dqp, dkp, dv, *arrived = attn_bwd(s["qp"], s["kp"], s["v"], do, s["lse"], delta, "attn_bwd_exchange",
                                          exchange=[p for _, _, p in pending])
        for (k, n, _), parts in zip(pending, arrived):
            got[k][n] = parts
        dq, dkv, dkr, dgq, dgkn, dgkr = qk_prep_bwd(dqp, dkp, dv, s["q"], s["kv"], s["kr"], cosf, sins,
                                                    s["gq"], s["gkn"], s["gkr"], "qk_prep_bwd")
        small_grads["q_norm_g"][l] = dgq[0, :QK_HEAD_DIM]
        small_grads["k_norm_g"][l] = jnp.concatenate([dgkn[0, :QK_NOPE_DIM], dgkr[0, QK_NOPE_DIM:QK_HEAD_DIM]])
        heads_first = lambda full: jnp.swapaxes(full.reshape(full.shape[0], MLA_HEADS, HEAD_PAD), 0, 1)
        slots["w_q_b"] = heads_first(mm_tn(s["qn"], dq, "dw_q", out_dtype=BF16))[:, :, :QK_HEAD_DIM]
        dc_q, dg_qa = mm_nt_normbwd([("heads", dq, g8["w_q_b"])], s["c_q"], row(q_a_norm_g, l), None, 0,
                                    "q_proj_bwd")
        small_grads["q_a_norm_g"][l] = dg_qa[0]
        slots["w_kv_b"] = heads_first(mm_tn(s["kvn"], dkv, "dw_kv", out_dtype=BF16))
        dc_kv, dg_kva = mm_nt_normbwd([("heads", dkv, g8["w_kv_b"])], s["c_kv"], row(kv_a_norm_g, l), None, 0,
                                      "kv_proj_bwd")
        small_grads["kv_a_norm_g"][l] = dg_kva[0]
        dw_in_t = jnp.concatenate(
            [mm_tn(du, s["a"], "dw_in_pool", out_dtype=BF16), mm_tn(dc_q, s["a"], "dw_in_q", out_dtype=BF16),
             mm_tn(dc_kv, s["a"], "dw_in_kv", out_dtype=BF16),
             mm_tn(dkr, s["a"], "dw_in_rope", out_dtype=BF16)[QK_NOPE_DIM:QK_HEAD_DIM]], axis=0)
        slots["w_in"] = slot_of("w_in", dw_in_t)
        dh, dg_attn = mm_nt_normbwd(
            [("rows", du, w_in_p, 0), ("rows", dc_q, w_in_p, s1), ("rows", dc_kv, w_in_p, s2),
             ("rows", dkr, w_in_p, s3)],
            s["h"], row(attn_norm_g, l), dh_mid, 0, "in_proj_bwd")
        small_grads["attn_norm_g"][l] = dg_attn[0]

        small_slots = None
        if l == 0:
            rep_shapes = [weights[n].shape for n in REPLICATED]
            rep_count = sum(int(jnp.size(weights[n])) for n in REPLICATED)
            rep_rows = -(-rep_count // (8 * LANES)) * 8
            rep_packed = _pack_flat([jnp.stack(small_grads[n], axis=0) for n in REPLICATED], rep_rows)
            meta_slots = jnp.transpose(dh[:N_META].reshape(N_META, N_DEV, LANES), (1, 0, 2))
            small_slots = jnp.concatenate(
                [meta_slots, jnp.broadcast_to(rep_packed[None], (N_DEV, rep_rows, LANES))], axis=1)
        pending, got_small = pair_reduce(l, [n for n in SHARDED if n not in FFN_WEIGHTS], slots, small_slots)
    for (k, n, _), parts in zip(pending, exchange_chips([p for _, _, p in pending], "grad_exchange_chips")):
        got[k][n] = parts

    grad_x = dh[N_META:length][None]

    per = [{} for _ in range(4)]
    for n in SHARDED:
        parts = [got[l][n] for l in range(depth)]
        if n in TRANSPOSED:
            grad = jnp.swapaxes(chip_sum(parts, "chip_sum_" + n), 1, 2)
            outs = adamw_shard(None, weights[n], mom1[n], mom2[n], "adamw_" + n, grad=grad)
        else:
            outs = adamw_shard(parts, weights[n], mom1[n], mom2[n], "adamw_" + n)
        for k in range(4):
            per[k][n] = outs[k]
    ps = lambda src: jnp.concatenate(
        [src["meta_tokens"], _pack_flat([src[n] for n in REPLICATED], rep_rows)], axis=0)
    small_out = adamw_packed(got_small, ps(weights), ps(mom1), ps(mom2), "adamw_small")
    for k in range(4):
        per[k]["meta_tokens"] = small_out[k][:N_META]
        per[k].update(zip(REPLICATED, _unpack_flat(small_out[k][N_META:], rep_shapes)))
    return (loss, grad_x, *[per[0][n] for n in order], *[per[1][n] for n in order],
            *[per[2][n] for n in order], *[per[3][n] for n in order])
```

```python
import functools

import jax
import jax.numpy as jnp
from jax import lax
from jax.experimental import pallas as pl
from jax.experimental.pallas import tpu as pltpu

F32 = jnp.float32
BF16 = jnp.bfloat16

D_MODEL = 1024
N_META = 16
POOL_WIDTH = 512
POOL_WINDOWS = (2, 4, 8, 16)
POOL_GROUP_DIM = 128
POOL_HALO = 16
MLA_HEADS = 8
QK_NOPE_DIM = 64
QK_ROPE_DIM = 32
QK_HEAD_DIM = 96
V_HEAD_DIM = 64
HEAD_PAD = 128
Q_LORA_RANK = 384
KV_LORA_RANK = 256
ROPE_THETA = 10000.0
RMS_EPS = 1e-6
ATTN_SCALE = QK_HEAD_DIM ** -0.5

ADAM_LR = 0.001
ADAM_B1 = 0.9
ADAM_B2 = 0.999
ADAM_EPS = 1e-08
ADAM_WD = 0.01
ADAM_STEP = 10

N_DEV = 8
N_CHIPS = 4
LANES = 128
ROW_TILE = 384
LONG_TILE = 1056
VMEM_LIMIT = 56 * 1024 * 1024


def _params(*sem):
    return pltpu.CompilerParams(dimension_semantics=sem, vmem_limit_bytes=VMEM_LIMIT)


def _row_spec(tile, width):
    return pl.BlockSpec((tile, width), lambda i: (i, 0))


def _const_spec(shape):
    return pl.BlockSpec(shape, lambda i: tuple(0 for _ in shape))


def _layer_spec(w, layer):
    return pl.BlockSpec((None,) + w.shape[1:], lambda *_: (layer, 0, 0))


def _gathered_spec(w8, layer):
    return pl.BlockSpec((N_DEV, None) + w8.shape[2:], lambda *_: (0, layer, 0, 0))


def _nt(a, b):
    return lax.dot_general(a, b, (((1,), (1,)), ((), ())), preferred_element_type=F32)


def _tn(a, b):
    return lax.dot_general(a, b, (((0,), (0,)), ((), ())), preferred_element_type=F32)


def _nn(a, b):
    return jnp.dot(a, b, preferred_element_type=F32)


def _silu(g):
    return g * (1.0 / (1.0 + jnp.exp(-g)))


def _rms(xf):
    return lax.rsqrt(jnp.mean(xf * xf, axis=-1, keepdims=True) + RMS_EPS)


def norm_mm(x, gamma, wt, layer, splits, dtypes, name):
    L, K = x.shape
    tm = ROW_TILE

    def body(x_ref, g_ref, w_ref, a_ref, *z_refs):
        xf = x_ref[...]
        a = ((xf * _rms(xf)) * g_ref[...]).astype(BF16)
        a_ref[...] = a
        z = _nt(a, w_ref[...])
        for (s, n), zr in zip(splits, z_refs):
            zr[...] = z[:, s:s + n].astype(zr.dtype)

    widths = [n for _, n in splits]
    return pl.pallas_call(
        body,
        grid=(L // tm,),
        in_specs=[_row_spec(tm, K), _const_spec((1, K)), _layer_spec(wt, layer)],
        out_specs=[_row_spec(tm, K)] + [_row_spec(tm, n) for n in widths],
        out_shape=[jax.ShapeDtypeStruct((L, K), BF16)]
        + [jax.ShapeDtypeStruct((L, n), dt) for n, dt in zip(widths, dtypes)],
        compiler_params=_params("arbitrary"),
        name=name,
    )(x, gamma, wt)


def norm_mm_heads(x, gamma, w8, layer, name):
    L, K = x.shape
    hw = w8.shape[-1]
    tm = ROW_TILE

    def body(x_ref, g_ref, w_ref, a_ref, z_ref):
        xf = x_ref[...]
        a = ((xf * _rms(xf)) * g_ref[...]).astype(BF16)
        a_ref[...] = a
        if hw < HEAD_PAD:
            z_ref[...] = jnp.zeros_like(z_ref)
        for j in range(MLA_HEADS):
            z_ref[:, j * HEAD_PAD:j * HEAD_PAD + hw] = _nn(a, w_ref[j])

    return pl.pallas_call(
        body,
        grid=(L // tm,),
        in_specs=[_row_spec(tm, K), _const_spec((1, K)), _gathered_spec(w8, layer)],
        out_specs=[_row_spec(tm, K), _row_spec(tm, MLA_HEADS * HEAD_PAD)],
        out_shape=[jax.ShapeDtypeStruct((L, K), BF16), jax.ShapeDtypeStruct((L, MLA_HEADS * HEAD_PAD), F32)],
        compiler_params=_params("arbitrary"),
        name=name,
    )(x, gamma, w8)


FF_GROUP = 4


def _ff_spec(w8, layer):
    return pl.BlockSpec((FF_GROUP, None) + w8.shape[2:], lambda j, i: (j, layer, 0, 0))


def ffn_up(g, w_gate8, w_up8, layer, name):
    L, K = g.shape
    fb = w_gate8.shape[-2]
    tm, tf = ROW_TILE, FF_GROUP * fb

    def body(a_ref, wg_ref, wu_ref, gate_ref, up_ref, act_ref):
        a = a_ref[...]
        gate = _nt(a, wg_ref[...].reshape(tf, K))
        up = _nt(a, wu_ref[...].reshape(tf, K))
        gate_ref[...] = gate.astype(BF16)
        up_ref[...] = up.astype(BF16)
        act_ref[...] = (_silu(gate) * up).astype(BF16)

    tile = pl.BlockSpec((tm, tf), lambda j, i: (i, j))
    F = N_DEV * fb
    return pl.pallas_call(
        body,
        grid=(N_DEV // FF_GROUP, L // tm),
        in_specs=[pl.BlockSpec((tm, K), lambda j, i: (i, 0)), _ff_spec(w_gate8, layer), _ff_spec(w_up8, layer)],
        out_specs=[tile, tile, tile],
        out_shape=[
            jax.ShapeDtypeStruct((L, F), BF16),
            jax.ShapeDtypeStruct((L, F), BF16),
            jax.ShapeDtypeStruct((L, F), BF16),
        ],
        compiler_params=_params("arbitrary", "arbitrary"),
        name=name,
    )(g, w_gate8, w_up8)


def mm_res(a, w8, res, layer, name, gamma=None):
    L = a.shape[0]
    kb, N = w8.shape[-2:]
    tm = ROW_TILE
    normed = gamma is not None

    def body(a_ref, w_ref, r_ref, *rest):
        out = r_ref[...] + _nn(a_ref[...], w_ref[...].reshape(N_DEV * kb, N))
        if normed:
            g_ref, o_ref, n_ref = rest
            n_ref[...] = ((out * _rms(out)) * g_ref[...]).astype(BF16)
        else:
            (o_ref,) = rest
        o_ref[...] = out

    in_specs = [_row_spec(tm, N_DEV * kb), _gathered_spec(w8, layer), _row_spec(tm, N)]
    out_specs = [_row_spec(tm, N)]
    out_shape = [jax.ShapeDtypeStruct((L, N), F32)]
    args = [a, w8, res]
    if normed:
        in_specs.append(_const_spec((1, N)))
        out_specs.append(_row_spec(tm, N))
        out_shape.append(jax.ShapeDtypeStruct((L, N), BF16))
        args.append(gamma)
    return pl.pallas_call(
        body,
        grid=(L // tm,),
        in_specs=in_specs,
        out_specs=out_specs,
        out_shape=out_shape,
        compiler_params=_params("arbitrary"),
        name=name,
    )(*args)


def out_proj_bwd(dz, w8, layer, o, name):
    L, N = dz.shape
    kb = w8.shape[-2]
    C = o.shape[1]
    tm = ROW_TILE

    def body(dz_ref, w_ref, o_ref, dy_ref, do_ref, dl_ref):
        d = _nt(dz_ref[...].astype(BF16), w_ref[...].reshape(N_DEV * kb, N))
        dy_ref[...] = d[:, :C]
        do_ref[...] = d[:, C:]
        left = lax.broadcasted_iota(jnp.int32, (tm, LANES), 1) < V_HEAD_DIM
        for p in range(C // LANES):
            cols = slice(p * LANES, (p + 1) * LANES)
            prod = d[:, C + p * LANES:C + (p + 1) * LANES] * o_ref[:, cols]
            d0 = jnp.sum(jnp.where(left, prod, 0.0), axis=1, keepdims=True)
            d1 = jnp.sum(jnp.where(left, 0.0, prod), axis=1, keepdims=True)
            dl_ref[:, cols] = jnp.where(left, d0, d1)

    return pl.pallas_call(
        body,
        grid=(L // tm,),
        in_specs=[_row_spec(tm, N), _gathered_spec(w8, layer), _row_spec(tm, C)],
        out_specs=[_row_spec(tm, C)] * 3,
        out_shape=[jax.ShapeDtypeStruct((L, C), F32)] * 3,
        compiler_params=_params("arbitrary"),
        name=name,
    )(dz, w8, o)


def mm_nt_normbwd(terms, x, gamma, dres, layer, name):
    L, K = x.shape
    tm = ROW_TILE
    n_terms = len(terms)
    has_res = dres is not None
    weights = []
    for t in terms:
        if not any(t[2] is u for u in weights):
            weights.append(t[2])
    which = [[t[2] is u for u in weights].index(True) for t in terms]
    n_in = n_terms + len(weights)

    def body(*refs):
        dz_refs = refs[:n_terms]
        w_refs = [refs[n_terms + n] for n in which]
        x_ref, g_ref = refs[n_in], refs[n_in + 1]
        pos = n_in + 2
        r_ref = refs[pos] if has_res else None
        dx_ref, dg_ref = refs[pos + has_res], refs[pos + has_res + 1]
        da = None
        for t, dz_ref, w_ref in zip(terms, dz_refs, w_refs):
            if t[0] == "rows":
                n, at = t[1].shape[1], t[3]
                parts = [_nn(dz_ref[...].astype(BF16), w_ref[at:at + n, :])]
            elif t[0] == "heads":
                hw = t[2].shape[-1]
                parts = [_nt(dz_ref[:, j * HEAD_PAD:j * HEAD_PAD + hw].astype(BF16), w_ref[j])
                         for j in range(MLA_HEADS)]
            else:
                nb = t[2].shape[-2]
                parts = [_nn(dz_ref[...].astype(BF16), w_ref[...].reshape(N_DEV * nb, K))]
            for p in parts:
                da = p if da is None else da + p
        xf = x_ref[...]
        r = _rms(xf)
        xh = xf * r

        @pl.when(pl.program_id(0) == 0)
        def _():
            dg_ref[...] = jnp.zeros_like(dg_ref)

        dg_ref[...] += jnp.sum(da * xh, axis=0, keepdims=True)
        dxh = da * g_ref[...]
        dx = r * (dxh - xh * jnp.mean(dxh * xh, axis=-1, keepdims=True))
        if has_res:
            dx = dx + r_ref[...]
        dx_ref[...] = dx

    in_specs = [_row_spec(tm, t[1].shape[1]) for t in terms]
    for w in weights:
        in_specs.append(_layer_spec(w, layer) if w.ndim == 3 else _gathered_spec(w, layer))
    in_specs += [_row_spec(tm, K), _const_spec((1, K))]
    args = [t[1] for t in terms] + weights + [x, gamma]
    if has_res:
        in_specs.append(_row_spec(tm, K))
        args.append(dres)
    return pl.pallas_call(
        body,
        grid=(L // tm,),
        in_specs=in_specs,
        out_specs=[_row_spec(tm, K), _const_spec((1, K))],
        out_shape=[jax.ShapeDtypeStruct((L, K), F32), jax.ShapeDtypeStruct((1, K), F32)],
        compiler_params=_params("arbitrary"),
        name=name,
    )(*args)


MAX_OUT_ROWS = 1408


def mm_tn(a, b, name, out_dtype=F32):
    L, K = a.shape
    N = b.shape[-1]
    tk = MAX_OUT_ROWS if (K > MAX_OUT_ROWS and K % MAX_OUT_ROWS == 0) else K
    tl = LONG_TILE if L % LONG_TILE == 0 else ROW_TILE
    n_l = L // tl

    def body(a_ref, b_ref, o_ref, acc):
        l = pl.program_id(1)

        @pl.when(l == 0)
        def _():
            acc[...] = jnp.zeros_like(acc)

        acc[...] += _tn(a_ref[...].astype(BF16), b_ref[...].astype(BF16))

        @pl.when(l == n_l - 1)
        def _():
            o_ref[...] = acc[...].astype(o_ref.dtype)

    return pl.pallas_call(
        body,
        grid=(K // tk, n_l),
        in_specs=[pl.BlockSpec((tl, tk), lambda j, l: (l, j)), pl.BlockSpec((tl, N), lambda j, l: (l, 0))],
        out_specs=pl.BlockSpec((tk, N), lambda j, l: (j, 0)),
        out_shape=jax.ShapeDtypeStruct((K, N), out_dtype),
        scratch_shapes=[pltpu.VMEM((tk, N), F32)],
        compiler_params=_params("arbitrary", "arbitrary"),
        name=name,
    )(a, b)


def ffn_bwd_act(dh, w_down8, layer, gate, up, name):
    L, K = dh.shape
    fb = w_down8.shape[-2]
    tm, tf = ROW_TILE, FF_GROUP * fb

    def body(dh_ref, w_ref, gate_ref, up_ref, dgate_ref, dup_ref):
        dact = _nt(dh_ref[...].astype(BF16), w_ref[...].reshape(tf, K))
        g = gate_ref[...].astype(F32)
        sig = 1.0 / (1.0 + jnp.exp(-g))
        dup_ref[...] = (dact * (g * sig)).astype(BF16)
        dgate_ref[...] = (dact * up_ref[...].astype(F32) * (sig * (1.0 + g * (1.0 - sig)))).astype(BF16)

    tile = pl.BlockSpec((tm, tf), lambda j, i: (i, j))
    F = N_DEV * fb
    return pl.pallas_call(
        body,
        grid=(N_DEV // FF_GROUP, L // tm),
        in_specs=[pl.BlockSpec((tm, K), lambda j, i: (i, 0)), _ff_spec(w_down8, layer), tile, tile],
        out_specs=[tile, tile],
        out_shape=[jax.ShapeDtypeStruct((L, F), BF16), jax.ShapeDtypeStruct((L, F), BF16)],
        compiler_params=_params("arbitrary", "arbitrary"),
        name=name,
    )(dh, w_down8, gate, up)


def _pool_residual(scr, lo, tm, g, w, t):
    cols = slice(g * POOL_GROUP_DIM, (g + 1) * POOL_GROUP_DIM)
    cur = scr[lo:lo + tm, cols]
    s = cur
    for k in range(1, w):
        s = s + scr[lo - k:lo - k + tm, cols]
    cnt = jnp.minimum(t + 1, w).astype(F32)
    return s / cnt - cur


def pool_fwd(u, w_pool, scale, name):
    L, C = u.shape
    tm, halo = ROW_TILE, POOL_HALO

    def body(u_ref, halo_ref, w_ref, s_ref, y_ref, scr):
        i = pl.program_id(0)
        scr[0:halo, :] = jnp.where(i > 0, halo_ref[...], 0.0)
        scr[halo:halo + tm, :] = u_ref[...]
        t = i * tm + lax.broadcasted_iota(jnp.int32, (tm, POOL_GROUP_DIM), 0)
        for g, w in enumerate(POOL_WINDOWS):
            cols = slice(g * POOL_GROUP_DIM, (g + 1) * POOL_GROUP_DIM)
            p = _pool_residual(scr, halo, tm, g, w, t)
            y = _nn(p.astype(BF16), w_ref[g]) * s_ref[:, cols]
            y_ref[:, cols] = y.astype(y_ref.dtype)

    return pl.pallas_call(
        body,
        grid=(L // tm,),
        in_specs=[
            _row_spec(tm, C),
            pl.BlockSpec((halo, C), lambda i: (jnp.maximum(i * (tm // halo) - 1, 0), 0)),
            _const_spec(w_pool.shape),
            _const_spec((1, C)),
        ],
        out_specs=_row_spec(tm, C),
        out_shape=jax.ShapeDtypeStruct((L, 2 * C), BF16),
        scratch_shapes=[pltpu.VMEM((tm + halo, C), F32)],
        compiler_params=_params("arbitrary"),
        name=name,
    )(u, u, w_pool, scale)


def pool_bwd(u, dy, w_pool, scale, name):
    L, C = u.shape
    tm, halo = ROW_TILE, POOL_HALO
    n_tiles = L // tm
    last_halo = L // halo - 1

    def body(u_ref, uh_ref, dy_ref, dyh_ref, w_ref, s_ref, du_ref, dw_ref, ds_ref, scr_u, scr_q):
        i = pl.program_id(0)

        @pl.when(i == 0)
        def _():
            dw_ref[...] = jnp.zeros_like(dw_ref)
            ds_ref[...] = jnp.zeros_like(ds_ref)

        scr_u[0:halo, :] = jnp.where(i > 0, uh_ref[...], 0.0)
        scr_u[halo:halo + tm, :] = u_ref[...]
        t = i * tm + lax.broadcasted_iota(jnp.int32, (tm, POOL_GROUP_DIM), 0)
        th = (i + 1) * tm + lax.broadcasted_iota(jnp.int32, (halo, POOL_GROUP_DIM), 0)
        for g, w in enumerate(POOL_WINDOWS):
            cols = slice(g * POOL_GROUP_DIM, (g + 1) * POOL_GROUP_DIM)
            p = _pool_residual(scr_u, halo, tm, g, w, t).astype(BF16)
            wg = w_ref[g]
            sc = s_ref[:, cols]
            dy = dy_ref[:, cols]
            ds_ref[:, cols] += jnp.sum(dy * _nn(p, wg), axis=0, keepdims=True)
            dys = (dy * sc).astype(BF16)
            dw_ref[g] += _tn(p, dys)
            dp = _nt(dys, wg)
            dyh = jnp.where(i < n_tiles - 1, dyh_ref[:, cols], 0.0)
            dph = _nt((dyh * sc).astype(BF16), wg)
            scr_q[0:tm, cols] = dp / jnp.minimum(t + 1, w).astype(F32)
            scr_q[tm:tm + halo, cols] = dph / jnp.minimum(th + 1, w).astype(F32)
            acc = scr_q[0:tm, cols]
            for k in range(1, w):
                acc = acc + scr_q[k:k + tm, cols]
            du_ref[:, cols] = acc - dp

    return pl.pallas_call(
        body,
        grid=(n_tiles,),
        in_specs=[
            _row_spec(tm, C),
            pl.BlockSpec((halo, C), lambda i: (jnp.maximum(i * (tm // halo) - 1, 0), 0)),
            _row_spec(tm, C),
            pl.BlockSpec((halo, C), lambda i: (jnp.minimum((i + 1) * (tm // halo), last_halo), 0)),
            _const_spec(w_pool.shape),
            _const_spec((1, C)),
        ],
        out_specs=[_row_spec(tm, C), _const_spec(w_pool.shape), _const_spec((1, C))],
        out_shape=[
            jax.ShapeDtypeStruct((L, C), F32),
            jax.ShapeDtypeStruct(w_pool.shape, F32),
            jax.ShapeDtypeStruct((1, C), F32),
        ],
        scratch_shapes=[pltpu.VMEM((tm + halo, C), F32), pltpu.VMEM((tm + halo, C), F32)],
        compiler_params=_params("arbitrary"),
        name=name,
    )(u, u, dy, dy, w_pool, scale)


def _head_masks(rows):
    lane = lax.broadcasted_iota(jnp.int32, (rows, HEAD_PAD), 1)
    return lane, lane < QK_NOPE_DIM, (lane >= QK_NOPE_DIM) & (lane < QK_HEAD_DIM)


def _rope_swap(x, lane):
    half = QK_ROPE_DIM // 2
    swapped = jnp.where(lane < QK_NOPE_DIM + half, pltpu.roll(x, HEAD_PAD - half, 1), pltpu.roll(x, half, 1))
    return jnp.where((lane >= QK_NOPE_DIM) & (lane < QK_HEAD_DIM), swapped, 0.0)


def _seg_mean(v, m_n, m_r):
    mn = jnp.sum(jnp.where(m_n, v, 0.0), axis=1, keepdims=True) * (1.0 / QK_NOPE_DIM)
    mr = jnp.sum(jnp.where(m_r, v, 0.0), axis=1, keepdims=True) * (1.0 / QK_ROPE_DIM)
    return jnp.where(m_n, mn, mr)


def qk_prep_fwd(q, kv, kr, cosf, sins, gq, gkn, gkr, name):
    L = q.shape[0]
    tm = ROW_TILE
    W = MLA_HEADS * HEAD_PAD

    def body(q_ref, kv_ref, kr_ref, c_ref, s_ref, gq_ref, gkn_ref, gkr_ref, qo_ref, ko_ref, vo_ref):
        lane, m_n, m_r = _head_masks(tm)
        cosf_, sins_ = c_ref[...], s_ref[...]
        kr_ = kr_ref[...]
        rk = lax.rsqrt(_seg_mean(kr_ * kr_, m_n, m_r) + RMS_EPS)
        krn = kr_ * rk * gkr_ref[...]
        krf = krn * cosf_ + _rope_swap(krn, lane) * sins_
        for h in range(MLA_HEADS):
            cols = slice(h * HEAD_PAD, (h + 1) * HEAD_PAD)
            qh = q_ref[:, cols]
            qn = qh * lax.rsqrt(_seg_mean(qh * qh, m_n, m_r) + RMS_EPS) * gq_ref[...]
            qo_ref[:, cols] = (qn * cosf_ + _rope_swap(qn, lane) * sins_).astype(BF16)
            kh = jnp.where(m_n, kv_ref[:, cols], 0.0)
            rn = lax.rsqrt(jnp.sum(kh * kh, axis=1, keepdims=True) * (1.0 / QK_NOPE_DIM) + RMS_EPS)
            ko_ref[:, cols] = (kh * rn * gkn_ref[...] + krf).astype(BF16)
        for p in range(MLA_HEADS // 2):
            even = kv_ref[:, 2 * p * HEAD_PAD:(2 * p + 1) * HEAD_PAD]
            odd = kv_ref[:, (2 * p + 1) * HEAD_PAD:(2 * p + 2) * HEAD_PAD]
            pair = jnp.where(m_n, pltpu.roll(even, V_HEAD_DIM, 1), odd)
            vo_ref[:, p * LANES:(p + 1) * LANES] = pair.astype(BF16)

    vec = _const_spec((1, HEAD_PAD))
    return pl.pallas_call(
        body,
        grid=(L // tm,),
        in_specs=[_row_spec(tm, W), _row_spec(tm, W), _row_spec(tm, HEAD_PAD), _row_spec(tm, HEAD_PAD),
                  _row_spec(tm, HEAD_PAD), vec, vec, vec],
        out_specs=[_row_spec(tm, W), _row_spec(tm, W), _row_spec(tm, W // 2)],
        out_shape=[jax.ShapeDtypeStruct((L, W), BF16), jax.ShapeDtypeStruct((L, W), BF16),
                   jax.ShapeDtypeStruct((L, W // 2), BF16)],
        compiler_params=_params("arbitrary"),
        name=name,
    )(q, kv, kr, cosf, sins, gq, gkn, gkr)


def qk_prep_bwd(dqo, dko, dv, q, kv, kr, cosf, sins, gq, gkn, gkr, name):
    L = q.shape[0]
    tm = ROW_TILE
    W = MLA_HEADS * HEAD_PAD

    def body(dqo_ref, dko_ref, dv_ref, q_ref, kv_ref, kr_ref, c_ref, s_ref, gq_ref, gkn_ref, gkr_ref,
             dq_ref, dkv_ref, dkr_ref, dgq_ref, dgkn_ref, dgkr_ref):
        @pl.when(pl.program_id(0) == 0)
        def _():
            dgq_ref[...] = jnp.zeros_like(dgq_ref)
            dgkn_ref[...] = jnp.zeros_like(dgkn_ref)
            dgkr_ref[...] = jnp.zeros_like(dgkr_ref)

        lane, m_n, m_r = _head_masks(tm)
        cosf_, sins_ = c_ref[...], s_ref[...]
        dgq = jnp.zeros((1, HEAD_PAD), F32)
        dgkn = jnp.zeros((1, HEAD_PAD), F32)
        dkrf = jnp.zeros((tm, HEAD_PAD), F32)
        for h in range(MLA_HEADS):
            cols = slice(h * HEAD_PAD, (h + 1) * HEAD_PAD)
            dy = dqo_ref[:, cols]
            dqn = dy * cosf_ + _rope_swap(dy * sins_, lane)
            qh = q_ref[:, cols]
            rinv = lax.rsqrt(_seg_mean(qh * qh, m_n, m_r) + RMS_EPS)
            xh = qh * rinv
            dgq = dgq + jnp.sum(dqn * xh, axis=0, keepdims=True)
            dxh = dqn * gq_ref[...]
            dq_ref[:, cols] = rinv * (dxh - xh * _seg_mean(dxh * xh, m_n, m_r))
            dk = dko_ref[:, cols]
            dkrf = dkrf + dk
            kh = jnp.where(m_n, kv_ref[:, cols], 0.0)
            rn = lax.rsqrt(jnp.sum(kh * kh, axis=1, keepdims=True) * (1.0 / QK_NOPE_DIM) + RMS_EPS)
            xk = kh * rn
            dgkn = dgkn + jnp.sum(dk * xk, axis=0, keepdims=True)
            dxk = dk * gkn_ref[...]
            dkn = rn * (dxk - xk * (jnp.sum(dxk * xk, axis=1, keepdims=True) * (1.0 / QK_NOPE_DIM)))
            dvp = dv_ref[:, (h // 2) * LANES:(h // 2 + 1) * LANES]
            dvh = pltpu.roll(dvp, V_HEAD_DIM, 1) if h % 2 == 0 else dvp
            dkv_ref[:, cols] = jnp.where(m_n, dkn, dvh)
        kr_ = kr_ref[...]
        rk = lax.rsqrt(_seg_mean(kr_ * kr_, m_n, m_r) + RMS_EPS)
        xr = kr_ * rk
        dkrn = dkrf * cosf_ + _rope_swap(dkrf * sins_, lane)
        dgkr_ref[...] += jnp.sum(dkrn * xr, axis=0, keepdims=True)
        dxr = dkrn * gkr_ref[...]
        dkr_ref[...] = rk * (dxr - xr * _seg_mean(dxr * xr, m_n, m_r))
        dgq_ref[...] += dgq
        dgkn_ref[...] += dgkn

    vec = _const_spec((1, HEAD_PAD))
    return pl.pallas_call(
        body,
        grid=(L // tm,),
        in_specs=[_row_spec(tm, W), _row_spec(tm, W), _row_spec(tm, W // 2), _row_spec(tm, W), _row_spec(tm, W),
                  _row_spec(tm, HEAD_PAD), _row_spec(tm, HEAD_PAD), _row_spec(tm, HEAD_PAD), vec, vec, vec],
        out_specs=[_row_spec(tm, W), _row_spec(tm, W), _row_spec(tm, HEAD_PAD), vec, vec, vec],
        out_shape=[jax.ShapeDtypeStruct((L, W), F32), jax.ShapeDtypeStruct((L, W), F32),
                   jax.ShapeDtypeStruct((L, HEAD_PAD), F32)] + [jax.ShapeDtypeStruct((1, HEAD_PAD), F32)] * 3,
        compiler_params=_params("arbitrary"),
        name=name,
    )(dqo, dko, dv, q, kv, kr, cosf, sins, gq, gkn, gkr)


def attn_fwd(qp, kp, v, cat, name, gather=()):
    L = qp.shape[0]
    tq = ROW_TILE
    n_i = L // tq
    pair_w = 2 * HEAD_PAD
    n_pairs = MLA_HEADS // 2
    n_g = len(gather)

    def body(q_ref, k_ref, v_ref, cat_in, *rest):
        del cat_in
        o_ref, lse_ref, cat_ref = rest[n_g:n_g + 3]
        i = pl.program_id(1)
        if n_g:
            pair = pl.program_id(0)
            start, forward, finish = _gather_steps(rest[:n_g], rest[n_g + 3:2 * n_g + 3], *rest[2 * n_g + 3:])
            pl.when((pair == 0) & (i == 0))(start)
            pl.when((pair == n_pairs // 2) & (i == 0))(forward)
        left = lax.broadcasted_iota(jnp.int32, (tq, LANES), 1) < V_HEAD_DIM
        row = lax.broadcasted_iota(jnp.int32, (tq, tq), 0)
        col = lax.broadcasted_iota(jnp.int32, (tq, tq), 1)

        def step(j, carry, masked, width=1):
            rows = pl.ds(pl.multiple_of(j * (tq * width), tq), tq * width)
            vv = v_ref[rows, :]
            out = []
            for hh in range(2):
                cols = slice(hh * HEAD_PAD, (hh + 1) * HEAD_PAD)
                m, l, acc = carry[hh]
                s = _nt(q_ref[:, cols], k_ref[rows, cols]) * ATTN_SCALE
                if masked:
                    s = jnp.where(col <= row, s, -jnp.inf)
                m_new = jnp.maximum(m, jnp.max(s, axis=1, keepdims=True))
                alpha = jnp.exp(m - m_new)
                p = jnp.exp(s - m_new)
                l = alpha * l + jnp.sum(p, axis=1, keepdims=True)
                acc = alpha * acc + _nn(p.astype(BF16), vv)
                out.append((m_new, l, acc))
            return tuple(out)

        one = (jnp.full((tq, 1), -jnp.inf, F32), jnp.zeros((tq, 1), F32), jnp.zeros((tq, LANES), F32))
        doubles = jnp.right_shift(i, 1)
        carry = lax.fori_loop(0, doubles, functools.partial(step, masked=False, width=2), (one, one))
        carry = lax.fori_loop(2 * doubles, i, functools.partial(step, masked=False), carry)
        (m0, l0, a0), (m1, l1, a1) = step(i, carry, True)
        o = jnp.where(left, a0 / l0, a1 / l1)
        o_ref[...] = o
        cat_ref[...] = o.astype(BF16)
        lse_ref[...] = jnp.where(left, m0 + jnp.log(l0), m1 + jnp.log(l1))
        if n_g:
            pl.when((pair == n_pairs - 1) & (i == n_i - 1))(finish)

    return pl.pallas_call(
        body,
        grid=(n_pairs, n_i),
        in_specs=[
            pl.BlockSpec((tq, pair_w), lambda p, i: (i, p)),
            pl.BlockSpec((L, pair_w), lambda p, i: (0, p)),
            pl.BlockSpec((L, LANES), lambda p, i: (0, p)),
            _ANY,
        ] + [_ANY] * n_g,
        out_specs=[
            pl.BlockSpec((tq, LANES), lambda p, i: (i, p)),
            pl.BlockSpec((tq, LANES), lambda p, i: (i, p)),
            pl.BlockSpec((tq, LANES), lambda p, i: (i, n_pairs + p)),
        ] + [_ANY] * n_g,
        out_shape=[jax.ShapeDtypeStruct((L, n_pairs * LANES), F32), jax.ShapeDtypeStruct((L, n_pairs * LANES), F32),
                   jax.ShapeDtypeStruct(cat.shape, cat.dtype)]
        + [jax.ShapeDtypeStruct((N_DEV,) + g.shape, g.dtype) for g in gather],
        scratch_shapes=_gather_scratch(n_g) if n_g else [],
        input_output_aliases={3: 2},
        compiler_params=_params("arbitrary", "arbitrary"),
        name=name,
    )(qp, kp, v, cat, *gather)


def attn_bwd(qp, kp, v, do, lse, delta, name, exchange=()):
    L = qp.shape[0]
    tq = ROW_TILE
    n_q = L // tq
    pair_w = 2 * HEAD_PAD
    n_pairs = MLA_HEADS // 2
    n_x = len(exchange)

    def body(q_ref, k_ref, v_ref, do_ref, lse_ref, dl_ref, *rest):
        dq_ref, dk_ref, dv_ref = rest[n_x:n_x + 3]
        j = pl.program_id(1)
        if n_x:
            pair = pl.program_id(0)
            start, finish = _chip_exchange_steps(rest[:n_x], rest[n_x + 3:2 * n_x + 3], *rest[2 * n_x + 3:])
            pl.when((pair == 0) & (j == 0))(start)

        @pl.when(j == 0)
        def _():
            dq_ref[...] = jnp.zeros_like(dq_ref)

        dk_ref[...] = jnp.zeros_like(dk_ref)
        dv_ref[...] = jnp.zeros_like(dv_ref)
        row = lax.broadcasted_iota(jnp.int32, (tq, tq), 0)
        col = lax.broadcasted_iota(jnp.int32, (tq, tq), 1)

        def step(t, carry, masked, width=1, first=0):
            rows = pl.ds(pl.multiple_of((first + t * width) * tq, tq), tq * width)
            do = do_ref[rows, :]
            left = lax.broadcasted_iota(jnp.int32, do.shape, 1) < V_HEAD_DIM
            vv = v_ref[...]
            dv = None
            for hh in range(2):
                cols = slice(hh * HEAD_PAD, (hh + 1) * HEAD_PAD)
                stat = slice(hh * V_HEAD_DIM, hh * V_HEAD_DIM + 1)
                q = q_ref[rows, cols]
                k = k_ref[:, cols]
                dom = jnp.where(left if hh == 0 else jnp.logical_not(left), do, 0.0).astype(BF16)
                s = _nt(q, k) * ATTN_SCALE
                p = jnp.exp(s - lse_ref[rows, stat])
                if masked:
                    p = jnp.where(col <= row, p, 0.0)
                dp = _nt(dom, vv)
                ds = (p * (dp - dl_ref[rows, stat]) * ATTN_SCALE).astype(BF16)
                dq_ref[rows, cols] += _nn(ds, k)
                dk_ref[:, cols] += _tn(ds, q)
                t = _tn(p.astype(BF16), dom)
                dv = t if dv is None else dv + t
            dv_ref[...] += dv
            return carry

        step(j, 0, True)
        doubles = jnp.right_shift(n_q - 1 - j, 1)
        lax.fori_loop(0, doubles, functools.partial(step, masked=False, width=2, first=j + 1), 0)
        lax.fori_loop(j + 1 + 2 * doubles, n_q, functools.partial(step, masked=False), 0)
        if n_x:
            pl.when((pair == n_pairs - 1) & (j == n_q - 1))(finish)

    return pl.pallas_call(
        body,
        grid=(n_pairs, n_q),
        in_specs=[
            pl.BlockSpec((L, pair_w), lambda p, j: (0, p)),
            pl.BlockSpec((tq, pair_w), lambda p, j: (j, p)),
            pl.BlockSpec((tq, LANES), lambda p, j: (j, p)),
            pl.BlockSpec((L, LANES), lambda p, j: (0, p)),
            pl.BlockSpec((L, LANES), lambda p, j: (0, p)),
            pl.BlockSpec((L, LANES), lambda p, j: (0, p)),
        ] + [_ANY] * n_x,
        out_specs=[
            pl.BlockSpec((L, pair_w), lambda p, j: (0, p)),
            pl.BlockSpec((tq, pair_w), lambda p, j: (j, p)),
            pl.BlockSpec((tq, LANES), lambda p, j: (j, p)),
        ] + [_ANY] * n_x,
        out_shape=[
            jax.ShapeDtypeStruct((L, n_pairs * pair_w), F32),
            jax.ShapeDtypeStruct((L, n_pairs * pair_w), F32),
            jax.ShapeDtypeStruct((L, n_pairs * LANES), F32),
        ] + [jax.ShapeDtypeStruct(e.shape, e.dtype) for e in exchange],
        scratch_shapes=_chip_exchange_scratch(n_x) if n_x else [],
        compiler_params=_params("arbitrary", "arbitrary"),
        name=name,
    )(qp, kp, v, do, lse, delta, *exchange)


def loss_head(h, target, n_real, name):
    L, D = h.shape
    tm = ROW_TILE

    def body(h_ref, t_ref, dh_ref, sq_ref):
        i = pl.program_id(0)

        @pl.when(i == 0)
        def _():
            sq_ref[...] = jnp.zeros_like(sq_ref)

        t = i * tm + lax.broadcasted_iota(jnp.int32, (tm, D), 0)
        real = (t >= N_META) & (t < N_META + n_real)
        diff = jnp.where(real, h_ref[...] - t_ref[...], 0.0)
        dh_ref[...] = diff * (1.0 / D)
        sq_ref[...] += jnp.sum(diff * diff, axis=0, keepdims=True)

    return pl.pallas_call(
        body,
        grid=(L // tm,),
        in_specs=[_row_spec(tm, D), _row_spec(tm, D)],
        out_specs=[_row_spec(tm, D), _const_spec((1, D))],
        out_shape=[jax.ShapeDtypeStruct((L, D), F32), jax.ShapeDtypeStruct((1, D), F32)],
        compiler_params=_params("arbitrary"),
        name=name,
    )(h, target)


def _mesh_position():
    x, y, c = lax.axis_index("x"), lax.axis_index("y"), lax.axis_index("c")
    return x, y, c, 4 * x + 2 * y + c


def _flip(x, y, c, k):
    px = 1 - x if k & 4 else x
    py = 1 - y if k & 2 else y
    pc = 1 - c if k & 1 else c
    return (px, py, pc), 4 * px + 2 * py + pc


def _other_chips(x, y):
    return [(1 - x, y), (x, 1 - y), (1 - x, 1 - y)]


def _dev_index(px, py, pc):
    return 4 * px + 2 * py + pc


_ANY = pl.BlockSpec(memory_space=pl.ANY)


def cast_bf16(arrays, name):
    n = len(arrays)
    depth = arrays[0].shape[0]

    def body(*refs):
        for k, src in enumerate(refs[:n]):
            for l in range(depth):
                refs[n + k * depth + l][0] = src[l].astype(BF16)

    return pl.pallas_call(
        body,
        out_shape=[jax.ShapeDtypeStruct((1,) + a.shape[1:], BF16) for a in arrays for _ in range(depth)],
        compiler_params=pltpu.CompilerParams(vmem_limit_bytes=VMEM_LIMIT),
        name=name,
    )(*arrays)


def _gather_steps(srcs, dsts, send_sems, recv_sems, local_sems):
    n = len(srcs)
    x, y, c, me = _mesh_position()
    sibling = (x, y, 1 - c)
    chips = _other_chips(x, y)

    def copy(k, b, block, to, from_src=False):
        dst = dsts[b].at[block]
        return pltpu.make_async_remote_copy(
            src_ref=srcs[b] if from_src else dst, dst_ref=dst, send_sem=send_sems.at[k * n + b],
            recv_sem=recv_sems.at[k * n + b], device_id=to, device_id_type=pl.DeviceIdType.MESH)

    def own():
        return [pltpu.make_async_copy(srcs[b], dsts[b].at[me], local_sems.at[b]) for b in range(n)]

    def first():
        out = []
        for b in range(n):
            out.append(copy(0, b, me, sibling, from_src=True))
            out += [copy(1 + q, b, me, (*chip, c), from_src=True) for q, chip in enumerate(chips)]
        return out

    def passed():
        return [copy(4 + q, b, _dev_index(*chip, c), sibling) for q, chip in enumerate(chips) for b in range(n)]

    def start():
        for cp in own() + first():
            cp.start()

    def forward():
        for q, chip in enumerate(chips):
            for b in range(n):
                copy(1 + q, b, _dev_index(*chip, c), sibling).wait_recv()
        for cp in passed():
            cp.start()

    def finish():
        for b in range(n):
            copy(0, b, _dev_index(x, y, 1 - c), sibling).wait_recv()
            for q, chip in enumerate(chips):
                copy(4 + q, b, _dev_index(*chip, 1 - c), sibling).wait_recv()
        for cp in first() + passed():
            cp.wait_send()
        for cp in own():
            cp.wait()

    return start, forward, finish


def _gather_scratch(n):
    copies = N_DEV - 1
    return [pltpu.SemaphoreType.DMA((copies * n,)), pltpu.SemaphoreType.DMA((copies * n,)),
            pltpu.SemaphoreType.DMA((n,))]


def all_gather(payloads, name):
    n = len(payloads)

    def body(*refs):
        start, forward, finish = _gather_steps(refs[:n], refs[n:2 * n], *refs[2 * n:])
        start()
        forward()
        finish()

    return pl.pallas_call(
        body,
        in_specs=[_ANY] * n,
        out_specs=[_ANY] * n,
        out_shape=[jax.ShapeDtypeStruct((N_DEV,) + p.shape, p.dtype) for p in payloads],
        scratch_shapes=_gather_scratch(n),
        name=name,
    )(*payloads)


def exchange_pair(bigs, small, name):
    n = len(bigs)
    has_small = small is not None
    n_big = N_CHIPS * n
    n_sems = n_big + (N_DEV - 1 if has_small else 0)

    def body(*refs):
        big_refs = refs[:n]
        pos = n + has_small
        sib_refs = refs[pos:pos + n]
        send_sems, recv_sems, local_sem = refs[-3:]
        x, y, c, me = _mesh_position()
        sibling = (x, y, 1 - c)
        copies = []
        for b in range(n):
            for q in range(N_CHIPS):
                copies.append(pltpu.make_async_remote_copy(
                    src_ref=big_refs[b].at[_dev_index(q // 2, q % 2, 1 - c)], dst_ref=sib_refs[b].at[q],
                    send_sem=send_sems.at[b * N_CHIPS + q], recv_sem=recv_sems.at[b * N_CHIPS + q],
                    device_id=sibling, device_id_type=pl.DeviceIdType.MESH))
        waits = list(copies)
        if has_small:
            small_ref, gsmall_ref = refs[n], refs[pos + n]
            own = pltpu.make_async_copy(small_ref.at[me], gsmall_ref.at[me], local_sem.at[0])
            own.start()
            for k in range(1, N_DEV):
                peer, peer_idx = _flip(x, y, c, k)
                s = n_big + k - 1
                copies.append(pltpu.make_async_remote_copy(
                    src_ref=small_ref.at[peer_idx], dst_ref=gsmall_ref.at[me], send_sem=send_sems.at[s],
                    recv_sem=recv_sems.at[s], device_id=peer, device_id_type=pl.DeviceIdType.MESH))
                waits.append(pltpu.make_async_remote_copy(
                    src_ref=small_ref.at[peer_idx], dst_ref=gsmall_ref.at[peer_idx], send_sem=send_sems.at[s],
                    recv_sem=recv_sems.at[s], device_id=peer, device_id_type=pl.DeviceIdType.MESH))
        for cp in copies:
            cp.start()
        for cp in waits:
            cp.wait_recv()
        for cp in copies:
            cp.wait_send()
        if has_small:
            own.wait()

    out_shape = [jax.ShapeDtypeStruct((N_CHIPS,) + b.shape[1:], b.dtype) for b in bigs]
    args = list(bigs)
    if has_small:
        out_shape.append(jax.ShapeDtypeStruct(small.shape, small.dtype))
        args.append(small)
    return pl.pallas_call(
        body,
        in_specs=[_ANY] * len(args),
        out_specs=[_ANY] * len(out_shape),
        out_shape=out_shape,
        scratch_shapes=[pltpu.SemaphoreType.DMA((n_sems,)), pltpu.SemaphoreType.DMA((n_sems,)),
                        pltpu.SemaphoreType.DMA((1,))],
        name=name,
    )(*args)


def pair_sum(bigs, from_sibling, core, name):
    n = len(bigs)

    def body(core_ref, *refs):
        del core_ref
        for mine, sib, out in zip(refs[:n], refs[n:2 * n], refs[2 * n:]):
            out[...] = (mine[...].astype(F32) + sib[...].astype(F32)).astype(out.dtype)

    def slot(shape, picked):
        zeros = (0,) * (len(shape) - 1)
        if picked:
            return pl.BlockSpec((None,) + shape[1:], lambda q, core_ref: (2 * q + core_ref[0],) + zeros)
        return pl.BlockSpec((None,) + shape[1:], lambda q, core_ref: (q,) + zeros)

    grid_spec = pltpu.PrefetchScalarGridSpec(
        num_scalar_prefetch=1,
        grid=(N_CHIPS,),
        in_specs=[slot(b.shape, True) for b in bigs] + [slot(s.shape, False) for s in from_sibling],
        out_specs=[slot(s.shape, False) for s in from_sibling],
    )
    return pl.pallas_call(
        body,
        grid_spec=grid_spec,
        out_shape=[jax.ShapeDtypeStruct(s.shape, s.dtype) for s in from_sibling],
        compiler_params=_params("arbitrary"),
        name=name,
    )(core, *bigs, *from_sibling)


def _chip_exchange_steps(part_refs, got_refs, send_sems, recv_sems, local_sems):
    n = len(part_refs)
    x, y, c, me = _mesh_position()
    mine = 2 * x + y

    def copies(landing):
        out = []
        for q, (px, py) in enumerate(_other_chips(x, y)):
            theirs = 2 * px + py
            for b in range(n):
                out.append(pltpu.make_async_remote_copy(
                    src_ref=part_refs[b].at[theirs], dst_ref=got_refs[b].at[theirs if landing else mine],
                    send_sem=send_sems.at[q * n + b], recv_sem=recv_sems.at[q * n + b],
                    device_id=(px, py, c), device_id_type=pl.DeviceIdType.MESH))
        return out

    def own():
        return [pltpu.make_async_copy(part_refs[b].at[mine], got_refs[b].at[mine], local_sems.at[b])
                for b in range(n)]

    def start():
        for cp in own() + copies(False):
            cp.start()

    def finish():
        for cp in copies(True):
            cp.wait_recv()
        for cp in copies(False):
            cp.wait_send()
        for cp in own():
            cp.wait()

    return start, finish


def _chip_exchange_scratch(n):
    return [pltpu.SemaphoreType.DMA(((N_CHIPS - 1) * n,)), pltpu.SemaphoreType.DMA(((N_CHIPS - 1) * n,)),
            pltpu.SemaphoreType.DMA((n,))]


def exchange_chips(partials, name):
    n = len(partials)

    def body(*refs):
        start, finish = _chip_exchange_steps(refs[:n], refs[n:2 * n], *refs[2 * n:])
        start()
        finish()

    return pl.pallas_call(
        body,
        in_specs=[_ANY] * n,
        out_specs=[_ANY] * n,
        out_shape=[jax.ShapeDtypeStruct(p.shape, p.dtype) for p in partials],
        scratch_shapes=_chip_exchange_scratch(n),
        name=name,
    )(*partials)


def _adamw_math(g, w, m, v):
    m_new = ADAM_B1 * m + (1.0 - ADAM_B1) * g
    v_new = ADAM_B2 * v + (1.0 - ADAM_B2) * (g * g)
    m_hat = m_new / (1.0 - ADAM_B1 ** ADAM_STEP)
    v_hat = v_new / (1.0 - ADAM_B2 ** ADAM_STEP)
    delta = -ADAM_LR * (m_hat / (jnp.sqrt(v_hat) + ADAM_EPS) + ADAM_WD * w)
    return delta, m_new, v_new


def _chip_total(p_ref):
    g = p_ref[0].astype(F32)
    for q in range(1, N_CHIPS):
        g = g + p_ref[q].astype(F32)
    return g


def chip_sum(parts, name):
    depth = len(parts)

    def body(*refs):
        for l in range(depth):
            refs[depth][l] = _chip_total(refs[l])

    return pl.pallas_call(
        body,
        out_shape=jax.ShapeDtypeStruct((depth,) + parts[0].shape[1:], F32),
        compiler_params=pltpu.CompilerParams(vmem_limit_bytes=VMEM_LIMIT),
        name=name,
    )(*parts)


def adamw_shard(parts, w, m, v, name, grad=None):
    depth = w.shape[0]
    n_in = 1 if grad is not None else depth

    def body(*refs):
        w_ref, m_ref, v_ref, g_ref, d_ref, mo_ref, vo_ref = refs[n_in:]
        for l in range(depth):
            g = refs[0][l] if grad is not None else _chip_total(refs[l])
            delta, m_new, v_new = _adamw_math(g, w_ref[l], m_ref[l], v_ref[l])
            g_ref[l] = g
            d_ref[l] = delta
            mo_ref[l] = m_new
            vo_ref[l] = v_new

    return pl.pallas_call(
        body,
        out_shape=[jax.ShapeDtypeStruct(w.shape, F32)] * 4,
        compiler_params=pltpu.CompilerParams(vmem_limit_bytes=VMEM_LIMIT),
        name=name,
    )(*([grad] if grad is not None else parts), w, m, v)


def adamw_packed(parts, w, m, v, name):
    n_slots = parts.shape[0]

    def body(p_ref, w_ref, m_ref, v_ref, g_ref, d_ref, mo_ref, vo_ref):
        g = p_ref[0]
        for s in range(1, n_slots):
            g = g + p_ref[s]
        delta, m_new, v_new = _adamw_math(g, w_ref[...], m_ref[...], v_ref[...])
        g_ref[...] = g
        d_ref[...] = delta
        mo_ref[...] = m_new
        vo_ref[...] = v_new

    return pl.pallas_call(
        body,
        out_shape=[jax.ShapeDtypeStruct(w.shape, F32)] * 4,
        compiler_params=pltpu.CompilerParams(vmem_limit_bytes=VMEM_LIMIT),
        name=name,
    )(parts, w, m, v)


SHARDED = ("w_in", "w_q_b", "w_kv_b", "w_out", "w_gate", "w_up", "w_down")
TRANSPOSED = ("w_in", "w_gate", "w_up")
BEFORE_ATTENTION = ("w_in", "w_q_b", "w_kv_b")
FFN_WEIGHTS = ("w_gate", "w_up", "w_down")
REPLICATED = ("attn_norm_g", "w_pool", "pool_scale", "q_a_norm_g", "kv_a_norm_g", "q_norm_g", "k_norm_g",
              "ffn_norm_g")


def _pack_flat(arrays, rows):
    flat = jnp.concatenate([a.reshape(-1) for a in arrays])
    return jnp.pad(flat, (0, rows * LANES - flat.shape[0])).reshape(rows, LANES)


def _unpack_flat(packed, shapes):
    flat, out, at = packed.reshape(-1), [], 0
    for shp in shapes:
        n = 1
        for d in shp:
            n *= d
        out.append(flat[at:at + n].reshape(shp))
        at += n
    return out


def _rope_lane_tables(length):
    inv = 1.0 / (ROPE_THETA ** (jnp.arange(0, QK_ROPE_DIM, 2, dtype=F32) / QK_ROPE_DIM))
    ang = jnp.arange(length, dtype=F32)[:, None] * inv[None, :]
    cos, sin = jnp.cos(ang), jnp.sin(ang)
    ones = jnp.ones((length, QK_NOPE_DIM), F32)
    zeros = jnp.zeros((length, QK_NOPE_DIM), F32)
    tail = HEAD_PAD - QK_HEAD_DIM
    cosf = jnp.concatenate([ones, cos, cos, ones[:, :tail]], axis=1)
    sins = jnp.concatenate([zeros, -sin, sin, zeros[:, :tail]], axis=1)
    return cosf, sins


def _pad_lanes(vec, at, width=HEAD_PAD):
    return jnp.pad(vec, (at, width - at - vec.shape[0])).reshape(1, width)


def kernel(x, meta_tokens, attn_norm_g, w_in, w_pool, pool_scale, q_a_norm_g, w_q_b, kv_a_norm_g, w_kv_b, q_norm_g, k_norm_g, w_out, ffn_norm_g, w_gate, w_up, w_down, loss_target, m_meta_tokens, m_attn_norm_g, m_w_in, m_w_pool, m_pool_scale, m_q_a_norm_g, m_w_q_b, m_kv_a_norm_g, m_w_kv_b, m_q_norm_g, m_k_norm_g, m_w_out, m_ffn_norm_g, m_w_gate, m_w_up, m_w_down, v_meta_tokens, v_attn_norm_g, v_w_in, v_w_pool, v_pool_scale, v_q_a_norm_g, v_w_q_b, v_kv_a_norm_g, v_w_kv_b, v_q_norm_g, v_k_norm_g, v_w_out, v_ffn_norm_g, v_w_gate, v_w_up, v_w_down):
    weights = dict(meta_tokens=meta_tokens, attn_norm_g=attn_norm_g, w_in=w_in, w_pool=w_pool, pool_scale=pool_scale,
                   q_a_norm_g=q_a_norm_g, w_q_b=w_q_b, kv_a_norm_g=kv_a_norm_g, w_kv_b=w_kv_b, q_norm_g=q_norm_g,
                   k_norm_g=k_norm_g, w_out=w_out, ffn_norm_g=ffn_norm_g, w_gate=w_gate, w_up=w_up, w_down=w_down)
    mom1 = dict(meta_tokens=m_meta_tokens, attn_norm_g=m_attn_norm_g, w_in=m_w_in, w_pool=m_w_pool,
                pool_scale=m_pool_scale, q_a_norm_g=m_q_a_norm_g, w_q_b=m_w_q_b, kv_a_norm_g=m_kv_a_norm_g,
                w_kv_b=m_w_kv_b, q_norm_g=m_q_norm_g, k_norm_g=m_k_norm_g, w_out=m_w_out, ffn_norm_g=m_ffn_norm_g,
                w_gate=m_w_gate, w_up=m_w_up, w_down=m_w_down)
    mom2 = dict(meta_tokens=v_meta_tokens, attn_norm_g=v_attn_norm_g, w_in=v_w_in, w_pool=v_w_pool,
                pool_scale=v_pool_scale, q_a_norm_g=v_q_a_norm_g, w_q_b=v_w_q_b, kv_a_norm_g=v_kv_a_norm_g,
                w_kv_b=v_w_kv_b, q_norm_g=v_q_norm_g, k_norm_g=v_k_norm_g, w_out=v_w_out, ffn_norm_g=v_ffn_norm_g,
                w_gate=v_w_gate, w_up=v_w_up, w_down=v_w_down)
    order = ("meta_tokens", "attn_norm_g", "w_in", "w_pool", "pool_scale", "q_a_norm_g", "w_q_b", "kv_a_norm_g",
             "w_kv_b", "q_norm_g", "k_norm_g", "w_out", "ffn_norm_g", "w_gate", "w_up", "w_down")
    depth = w_in.shape[0]
    seq = x.shape[1]
    length = N_META + seq
    lp = -(-length // ROW_TILE) * ROW_TILE
    in_cols = w_in.shape[2]

    local = {n: (jnp.swapaxes(weights[n], 1, 2) if n in TRANSPOSED else weights[n]) for n in SHARDED}
    cast = cast_bf16([local[n] for n in SHARDED], "cast_weights")
    shards = [{n: cast[k * depth + l] for k, n in enumerate(SHARDED)} for l in range(depth)]
    gathered = all_gather([shards[0][n] for n in BEFORE_ATTENTION] + [meta_tokens], "all_gather")
    g8l = [dict(zip(BEFORE_ATTENTION, gathered[:-1]))] + [{} for _ in range(depth - 1)]
    meta_full = jnp.transpose(gathered[-1], (1, 0, 2)).reshape(N_META, D_MODEL)

    s1, s2, s3 = POOL_WIDTH, POOL_WIDTH + Q_LORA_RANK, POOL_WIDTH + Q_LORA_RANK + KV_LORA_RANK
    zpad = lambda n: jnp.zeros((1, n, D_MODEL), BF16)

    def padded_in_proj(w8):
        w_in_t = jnp.swapaxes(w8, 0, 1).reshape(1, N_DEV * in_cols, D_MODEL)
        return jnp.concatenate([w_in_t[:, :s3], zpad(QK_NOPE_DIM), w_in_t[:, s3:],
                                zpad(HEAD_PAD - QK_HEAD_DIM)], axis=1)

    w_in_ps = [None] * depth
    w_pool_b = w_pool.astype(BF16)

    cosf, sins = _rope_lane_tables(lp)
    row = lambda a, l: a[l].reshape(1, -1)

    h = jnp.concatenate([meta_full, x[0], jnp.zeros((lp - length, D_MODEL), F32)], axis=0)
    target = jnp.pad(loss_target[0], ((N_META, lp - length), (0, 0)))
    saved = []
    for l in range(depth):
        gq = _pad_lanes(q_norm_g[l], 0)
        gkn = _pad_lanes(k_norm_g[l, :QK_NOPE_DIM], 0)
        gkr = _pad_lanes(k_norm_g[l, QK_NOPE_DIM:], QK_NOPE_DIM)
        g8 = g8l[l]
        w_in_ps[l] = padded_in_proj(g8["w_in"])
        a, u, c_q, c_kv, kr = norm_mm(
            h, row(attn_norm_g, l), w_in_ps[l], 0,
            [(0, s1), (s1, Q_LORA_RANK), (s2, KV_LORA_RANK), (s3, HEAD_PAD)], [F32] * 4, "in_proj")
        cat = pool_fwd(u, w_pool_b[l], row(pool_scale, l), "pool_fwd")
        qn, q = norm_mm_heads(c_q, row(q_a_norm_g, l), g8["w_q_b"], 0, "q_proj")
        kvn, kv = norm_mm_heads(c_kv, row(kv_a_norm_g, l), g8["w_kv_b"], 0, "kv_proj")
        qp, kp, v = qk_prep_fwd(q, kv, kr, cosf, sins, gq, gkn, gkr, "qk_prep_fwd")
        riders = [(l, n) for n in SHARDED if n not in g8]
        riders += [(l + 1, n) for n in BEFORE_ATTENTION if l + 1 < depth]
        o, lse, cat, *arrived = attn_fwd(qp, kp, v, cat, "attn_fwd_gather", gather=[shards[k][n] for k, n in riders])
        for (k, n), w8 in zip(riders, arrived):
            g8l[k][n] = w8
        h_mid, g = mm_res(cat, g8["w_out"], h, 0, "out_proj", gamma=row(ffn_norm_g, l))
        gate, up, act = ffn_up(g, g8["w_gate"], g8["w_up"], 0, "ffn_up")
        h_next = mm_res(act, g8["w_down"], h_mid, 0, "ffn_down")[0]
        saved.append(dict(h=h, a=a, u=u, c_q=c_q, c_kv=c_kv, kr=kr, qn=qn, q=q, kvn=kvn, kv=kv, v=v, qp=qp, kp=kp,
                          o=o, lse=lse, cat=cat, h_mid=h_mid, g=g, gate=gate, up=up, act=act,
                          gq=gq, gkn=gkn, gkr=gkr))
        h = h_next

    dh, sq = loss_head(h, target, seq, "loss_head")
    loss = lax.psum(0.5 / D_MODEL * jnp.sum(sq), ("x", "y", "c"))

    core = lax.axis_index("c").astype(jnp.int32).reshape(1)
    small_grads = {n: [None] * depth for n in REPLICATED}
    pending = []
    got = [{} for _ in range(depth)]

    def pair_reduce(l, names, slots, small):
        bigs = [slots[n] for n in names]
        if small is None:
            from_sibling, got_small = exchange_pair(bigs, None, "grad_exchange_pair"), None
        else:
            *from_sibling, got_small = exchange_pair(bigs, small, "grad_exchange_pair_small")
        partial = pair_sum(bigs, list(from_sibling), core, "grad_pair_sum")
        return [(l, n, p) for n, p in zip(names, partial)], got_small

    for l in reversed(range(depth)):
        s = saved[l]
        g8 = g8l[l]
        w_in_p = w_in_ps[l]
        slots = {}
        dgate, dup = ffn_bwd_act(dh, g8["w_down"], 0, s["gate"], s["up"], "ffn_bwd_act")
        slot_of = lambda n, full: full.reshape(g8[n].shape[:1] + g8[n].shape[2:])
        slots["w_down"] = slot_of("w_down", mm_tn(s["act"], dh, "dw_down", out_dtype=BF16))
        slots["w_gate"] = slot_of("w_gate", mm_tn(dgate, s["g"], "dw_gate", out_dtype=BF16))
        slots["w_up"] = slot_of("w_up", mm_tn(dup, s["g"], "dw_up", out_dtype=BF16))
        pending += pair_reduce(l, FFN_WEIGHTS, slots, None)[0]
        dh_mid, dg_ffn = mm_nt_normbwd([("gathered", dgate, g8["w_gate"]), ("gathered", dup, g8["w_up"])],
                                       s["h_mid"], row(ffn_norm_g, l), dh, 0, "ffn_bwd_in")
        small_grads["ffn_norm_g"][l] = dg_ffn[0]
        slots["w_out"] = slot_of("w_out", mm_tn(s["cat"], dh_mid, "dw_out", out_dtype=BF16))
        dy_pool, do, delta = out_proj_bwd(dh_mid, g8["w_out"], 0, s["o"], "out_proj_bwd")
        du, dw_pool, dscale = pool_bwd(s["u"], dy_pool, w_pool_b[l], row(pool_scale, l), "pool_bwd")
        small_grads["w_pool"][l] = dw_pool
        small_grads["pool_scale"][l] = dscale[0]
        dqp, dkp, dv, *arrived = attn_bwd(s["qp"], s["kp"], s["v"], do, s["lse"], delta, "attn_bwd_exchange",
                                          exchange=[p for _, _, p in pending])
        for (k, n, _), parts in zip(pending, arrived):
            got[k][n] = parts
        dq, dkv, dkr, dgq, dgkn, dgkr = qk_prep_bwd(dqp, dkp, dv, s["q"], s["kv"], s["kr"], cosf, sins,
                                                    s["gq"], s["gkn"], s["gkr"], "qk_prep_bwd")
        small_grads["q_norm_g"][l] = dgq[0, :QK_HEAD_DIM]
        small_grads["k_norm_g"][l] = jnp.concatenate([dgkn[0, :QK_NOPE_DIM], dgkr[0, QK_NOPE_DIM:QK_HEAD_DIM]])
        heads_first = lambda full: jnp.swapaxes(full.reshape(full.shape[0], MLA_HEADS, HEAD_PAD), 0, 1)
        slots["w_q_b"] = heads_first(mm_tn(s["qn"], dq, "dw_q", out_dtype=BF16))[:, :, :QK_HEAD_DIM]
        dc_q, dg_qa = mm_nt_normbwd([("heads", dq, g8["w_q_b"])], s["c_q"], row(q_a_norm_g, l), None, 0,
                                    "q_proj_bwd")
        small_grads["q_a_norm_g"][l] = dg_qa[0]
        slots["w_kv_b"] = heads_first(mm_tn(s["kvn"], dkv, "dw_kv", out_dtype=BF16))
        dc_kv, dg_kva = mm_nt_normbwd([("heads", dkv, g8["w_kv_b"])], s["c_kv"], row(kv_a_norm_g, l), None, 0,
                                      "kv_proj_bwd")
        small_grads["kv_a_norm_g"][l] = dg_kva[0]
        dw_in_t = jnp.concatenate(
            [mm_tn(du, s["a"], "dw_in_pool", out_dtype=BF16), mm_tn(dc_q, s["a"], "dw_in_q", out_dtype=BF16),
             mm_tn(dc_kv, s["a"], "dw_in_kv", out_dtype=BF16),
             mm_tn(dkr, s["a"], "dw_in_rope", out_dtype=BF16)[QK_NOPE_DIM:QK_HEAD_DIM]], axis=0)
        slots["w_in"] = slot_of("w_in", dw_in_t)
        dh, dg_attn = mm_nt_normbwd(
            [("rows", du, w_in_p, 0), ("rows", dc_q, w_in_p, s1), ("rows", dc_kv, w_in_p, s2),
             ("rows", dkr, w_in_p, s3)],
            s["h"], row(attn_norm_g, l), dh_mid, 0, "in_proj_bwd")
        small_grads["attn_norm_g"][l] = dg_attn[0]

        small_slots = None
        if l == 0:
            rep_shapes = [weights[n].shape for n in REPLICATED]
            rep_count = sum(int(jnp.size(weights[n])) for n in REPLICATED)
            rep_rows = -(-rep_count // (8 * LANES)) * 8
            rep_packed = _pack_flat([jnp.stack(small_grads[n], axis=0) for n in REPLICATED], rep_rows)
            meta_slots = jnp.transpose(dh[:N_META].reshape(N_META, N_DEV, LANES), (1, 0, 2))
            small_slots = jnp.concatenate(
                [meta_slots, jnp.broadcast_to(rep_packed[None], (N_DEV, rep_rows, LANES))], axis=1)
        pending, got_small = pair_reduce(l, [n for n in SHARDED if n not in FFN_WEIGHTS], slots, small_slots)
    for (k, n, _), parts in zip(pending, exchange_chips([p for _, _, p in pending], "grad_exchange_chips")):
        got[k][n] = parts

    grad_x = dh[N_META:length][None]

    per = [{} for _ in range(4)]
    for n in SHARDED:
        parts = [got[l][n] for l in range(depth)]
        if n in TRANSPOSED:
            grad = jnp.swapaxes(chip_sum(parts, "chip_sum_" + n), 1, 2)
            outs = adamw_shard(None, weights[n], mom1[n], mom2[n], "adamw_" + n, grad=grad)
        else:
            outs = adamw_shard(parts, weights[n], mom1[n], mom2[n], "adamw_" + n)
        for k in range(4):
            per[k][n] = outs[k]
    ps = lambda src: jnp.concatenate(
        [src["meta_tokens"], _pack_flat([src[n] for n in REPLICATED], rep_rows)], axis=0)
    small_out = adamw_packed(got_small, ps(weights), ps(mom1), ps(mom2), "adamw_small")
    for k in range(4):
        per[k]["meta_tokens"] = small_out[k][:N_META]
        per[k].update(zip(REPLICATED, _unpack_flat(small_out[k][N_META:], rep_shapes)))
    return (loss, grad_x, *[per[0][n] for n in order], *[per[1][n] for n in order],
            *[per[2][n] for n in order], *[per[3][n] for n in order])
```

```python
import functools

import jax
import jax.numpy as jnp
from jax import lax
from jax.experimental import pallas as pl
from jax.experimental.pallas import tpu as pltpu

F32 = jnp.float32
BF16 = jnp.bfloat16

D_MODEL = 1024
N_META = 16
POOL_WIDTH = 512
POOL_WINDOWS = (2, 4, 8, 16)
POOL_GROUP_DIM = 128
POOL_HALO = 16
MLA_HEADS = 8
QK_NOPE_DIM = 64
QK_ROPE_DIM = 32
QK_HEAD_DIM = 96
V_HEAD_DIM = 64
HEAD_PAD = 128
Q_LORA_RANK = 384
KV_LORA_RANK = 256
ROPE_THETA = 10000.0
RMS_EPS = 1e-6
ATTN_SCALE = QK_HEAD_DIM ** -0.5

ADAM_LR = 0.001
ADAM_B1 = 0.9
ADAM_B2 = 0.999
ADAM_EPS = 1e-08
ADAM_WD = 0.01
ADAM_STEP = 10

N_DEV = 8
N_CHIPS = 4
LANES = 128
ROW_TILE = 384
LONG_TILE = 1056
VMEM_LIMIT = 56 * 1024 * 1024


def _params(*sem):
    return pltpu.CompilerParams(dimension_semantics=sem, vmem_limit_bytes=VMEM_LIMIT)


def _row_spec(tile, width):
    return pl.BlockSpec((tile, width), lambda i: (i, 0))


def _const_spec(shape):
    return pl.BlockSpec(shape, lambda i: tuple(0 for _ in shape))


def _layer_spec(w, layer):
    return pl.BlockSpec((None,) + w.shape[1:], lambda *_: (layer, 0, 0))


def _gathered_spec(w8, layer):
    return pl.BlockSpec((N_DEV, None) + w8.shape[2:], lambda *_: (0, layer, 0, 0))


def _nt(a, b):
    return lax.dot_general(a, b, (((1,), (1,)), ((), ())), preferred_element_type=F32)


def _tn(a, b):
    return lax.dot_general(a, b, (((0,), (0,)), ((), ())), preferred_element_type=F32)


def _nn(a, b):
    return jnp.dot(a, b, preferred_element_type=F32)


def _silu(g):
    return g * (1.0 / (1.0 + jnp.exp(-g)))


def _rms(xf):
    return lax.rsqrt(jnp.mean(xf * xf, axis=-1, keepdims=True) + RMS_EPS)


def norm_mm(x, gamma, wt, layer, splits, dtypes, name):
    L, K = x.shape
    tm = ROW_TILE

    def body(x_ref, g_ref, w_ref, a_ref, *z_refs):
        xf = x_ref[...]
        a = ((xf * _rms(xf)) * g_ref[...]).astype(BF16)
        a_ref[...] = a
        z = _nt(a, w_ref[...])
        for (s, n), zr in zip(splits, z_refs):
            zr[...] = z[:, s:s + n].astype(zr.dtype)

    widths = [n for _, n in splits]
    return pl.pallas_call(
        body,
        grid=(L // tm,),
        in_specs=[_row_spec(tm, K), _const_spec((1, K)), _layer_spec(wt, layer)],
        out_specs=[_row_spec(tm, K)] + [_row_spec(tm, n) for n in widths],
        out_shape=[jax.ShapeDtypeStruct((L, K), BF16)]
        + [jax.ShapeDtypeStruct((L, n), dt) for n, dt in zip(widths, dtypes)],
        compiler_params=_params("arbitrary"),
        name=name,
    )(x, gamma, wt)


def norm_mm_heads(x, gamma, w8, layer, name):
    L, K = x.shape
    hw = w8.shape[-1]
    tm = ROW_TILE

    def body(x_ref, g_ref, w_ref, a_ref, z_ref):
        xf = x_ref[...]
        a = ((xf * _rms(xf)) * g_ref[...]).astype(BF16)
        a_ref[...] = a
        if hw < HEAD_PAD:
            z_ref[...] = jnp.zeros_like(z_ref)
        for j in range(MLA_HEADS):
            z_ref[:, j * HEAD_PAD:j * HEAD_PAD + hw] = _nn(a, w_ref[j])

    return pl.pallas_call(
        body,
        grid=(L // tm,),
        in_specs=[_row_spec(tm, K), _const_spec((1, K)), _gathered_spec(w8, layer)],
        out_specs=[_row_spec(tm, K), _row_spec(tm, MLA_HEADS * HEAD_PAD)],
        out_shape=[jax.ShapeDtypeStruct((L, K), BF16), jax.ShapeDtypeStruct((L, MLA_HEADS * HEAD_PAD), F32)],
        compiler_params=_params("arbitrary"),
        name=name,
    )(x, gamma, w8)


FF_GROUP = 4


def _ff_spec(w8, layer):
    return pl.BlockSpec((FF_GROUP, None) + w8.shape[2:], lambda j, i: (j, layer, 0, 0))


def ffn_up(g, w_gate8, w_up8, layer, name):
    L, K = g.shape
    fb = w_gate8.shape[-2]
    tm, tf = ROW_TILE, FF_GROUP * fb

    def body(a_ref, wg_ref, wu_ref, gate_ref, up_ref, act_ref):
        a = a_ref[...]
        gate = _nt(a, wg_ref[...].reshape(tf, K))
        up = _nt(a, wu_ref[...].reshape(tf, K))
        gate_ref[...] = gate.astype(BF16)
        up_ref[...] = up.astype(BF16)
        act_ref[...] = (_silu(gate) * up).astype(BF16)

    tile = pl.BlockSpec((tm, tf), lambda j, i: (i, j))
    F = N_DEV * fb
    return pl.pallas_call(
        body,
        grid=(N_DEV // FF_GROUP, L // tm),
        in_specs=[pl.BlockSpec((tm, K), lambda j, i: (i, 0)), _ff_spec(w_gate8, layer), _ff_spec(w_up8, layer)],
        out_specs=[tile, tile, tile],
        out_shape=[
            jax.ShapeDtypeStruct((L, F), BF16),
            jax.ShapeDtypeStruct((L, F), BF16),
            jax.ShapeDtypeStruct((L, F), BF16),
        ],
        compiler_params=_params("arbitrary", "arbitrary"),
        name=name,
    )(g, w_gate8, w_up8)


def mm_res(a, w8, res, layer, name, gamma=None):
    L = a.shape[0]
    kb, N = w8.shape[-2:]
    tm = ROW_TILE
    normed = gamma is not None

    def body(a_ref, w_ref, r_ref, *rest):
        out = r_ref[...] + _nn(a_ref[...], w_ref[...].reshape(N_DEV * kb, N))
        if normed:
            g_ref, o_ref, n_ref = rest
            n_ref[...] = ((out * _rms(out)) * g_ref[...]).astype(BF16)
        else:
            (o_ref,) = rest
        o_ref[...] = out

    in_specs = [_row_spec(tm, N_DEV * kb), _gathered_spec(w8, layer), _row_spec(tm, N)]
    out_specs = [_row_spec(tm, N)]
    out_shape = [jax.ShapeDtypeStruct((L, N), F32)]
    args = [a, w8, res]
    if normed:
        in_specs.append(_const_spec((1, N)))
        out_specs.append(_row_spec(tm, N))
        out_shape.append(jax.ShapeDtypeStruct((L, N), BF16))
        args.append(gamma)
    return pl.pallas_call(
        body,
        grid=(L // tm,),
        in_specs=in_specs,
        out_specs=out_specs,
        out_shape=out_shape,
        compiler_params=_params("arbitrary"),
        name=name,
    )(*args)


def out_proj_bwd(dz, w8, layer, o, name):
    L, N = dz.shape
    kb = w8.shape[-2]
    C = o.shape[1]
    tm = ROW_TILE

    def body(dz_ref, w_ref, o_ref, dy_ref, do_ref, dl_ref):
        d = _nt(dz_ref[...].astype(BF16), w_ref[...].reshape(N_DEV * kb, N))
        dy_ref[...] = d[:, :C]
        do_ref[...] = d[:, C:]
        left = lax.broadcasted_iota(jnp.int32, (tm, LANES), 1) < V_HEAD_DIM
        for p in range(C // LANES):
            cols = slice(p * LANES, (p + 1) * LANES)
            prod = d[:, C + p * LANES:C + (p + 1) * LANES] * o_ref[:, cols]
            d0 = jnp.sum(jnp.where(left, prod, 0.0), axis=1, keepdims=True)
            d1 = jnp.sum(jnp.where(left, 0.0, prod), axis=1, keepdims=True)
            dl_ref[:, cols] = jnp.where(left, d0, d1)

    return pl.pallas_call(
        body,
        grid=(L // tm,),
        in_specs=[_row_spec(tm, N), _gathered_spec(w8, layer), _row_spec(tm, C)],
        out_specs=[_row_spec(tm, C)] * 3,
        out_shape=[jax.ShapeDtypeStruct((L, C), F32)] * 3,
        compiler_params=_params("arbitrary"),
        name=name,
    )(dz, w8, o)


def mm_nt_normbwd(terms, x, gamma, dres, layer, name):
    L, K = x.shape
    tm = ROW_TILE
    n_terms = len(terms)
    has_res = dres is not None
    weights = []
    for t in terms:
        if not any(t[2] is u for u in weights):
            weights.append(t[2])
    which = [[t[2] is u for u in weights].index(True) for t in terms]
    n_in = n_terms + len(weights)

    def body(*refs):
        dz_refs = refs[:n_terms]
        w_refs = [refs[n_terms + n] for n in which]
        x_ref, g_ref = refs[n_in], refs[n_in + 1]
        pos = n_in + 2
        r_ref = refs[pos] if has_res else None
        dx_ref, dg_ref = refs[pos + has_res], refs[pos + has_res + 1]
        da = None
        for t, dz_ref, w_ref in zip(terms, dz_refs, w_refs):
            if t[0] == "rows":
                n, at = t[1].shape[1], t[3]
                parts = [_nn(dz_ref[...].astype(BF16), w_ref[at:at + n, :])]
            elif t[0] == "heads":
                hw = t[2].shape[-1]
                parts = [_nt(dz_ref[:, j * HEAD_PAD:j * HEAD_PAD + hw].astype(BF16), w_ref[j])
                         for j in range(MLA_HEADS)]
            else:
                nb = t[2].shape[-2]
                parts = [_nn(dz_ref[...].astype(BF16), w_ref[...].reshape(N_DEV * nb, K))]
            for p in parts:
                da = p if da is None else da + p
        xf = x_ref[...]
        r = _rms(xf)
        xh = xf * r

        @pl.when(pl.program_id(0) == 0)
        def _():
            dg_ref[...] = jnp.zeros_like(dg_ref)

        dg_ref[...] += jnp.sum(da * xh, axis=0, keepdims=True)
        dxh = da * g_ref[...]
        dx = r * (dxh - xh * jnp.mean(dxh * xh, axis=-1, keepdims=True))
        if has_res:
            dx = dx + r_ref[...]
        dx_ref[...] = dx

    in_specs = [_row_spec(tm, t[1].shape[1]) for t in terms]
    for w in weights:
        in_specs.append(_layer_spec(w, layer) if w.ndim == 3 else _gathered_spec(w, layer))
    in_specs += [_row_spec(tm, K), _const_spec((1, K))]
    args = [t[1] for t in terms] + weights + [x, gamma]
    if has_res:
        in_specs.append(_row_spec(tm, K))
        args.append(dres)
    return pl.pallas_call(
        body,
        grid=(L // tm,),
        in_specs=in_specs,
        out_specs=[_row_spec(tm, K), _const_spec((1, K))],
        out_shape=[jax.ShapeDtypeStruct((L, K), F32), jax.ShapeDtypeStruct((1, K), F32)],
        compiler_params=_params("arbitrary"),
        name=name,
    )(*args)


MAX_OUT_ROWS = 1408


def mm_tn(a, b, name, out_dtype=F32):
    L, K = a.shape
    N = b.shape[-1]
    tk = MAX_OUT_ROWS if (K > MAX_OUT_ROWS and K % MAX_OUT_ROWS == 0) else K
    tl = LONG_TILE if L % LONG_TILE == 0 else ROW_TILE
    n_l = L // tl

    def body(a_ref, b_ref, o_ref, acc):
        l = pl.program_id(1)

        @pl.when(l == 0)
        def _():
            acc[...] = jnp.zeros_like(acc)

        acc[...] += _tn(a_ref[...].astype(BF16), b_ref[...].astype(BF16))

        @pl.when(l == n_l - 1)
        def _():
            o_ref[...] = acc[...].astype(o_ref.dtype)

    return pl.pallas_call(
        body,
        grid=(K // tk, n_l),
        in_specs=[pl.BlockSpec((tl, tk), lambda j, l: (l, j)), pl.BlockSpec((tl, N), lambda j, l: (l, 0))],
        out_specs=pl.BlockSpec((tk, N), lambda j, l: (j, 0)),
        out_shape=jax.ShapeDtypeStruct((K, N), out_dtype),
        scratch_shapes=[pltpu.VMEM((tk, N), F32)],
        compiler_params=_params("arbitrary", "arbitrary"),
        name=name,
    )(a, b)


def ffn_bwd_act(dh, w_down8, layer, gate, up, name):
    L, K = dh.shape
    fb = w_down8.shape[-2]
    tm, tf = ROW_TILE, FF_GROUP * fb

    def body(dh_ref, w_ref, gate_ref, up_ref, dgate_ref, dup_ref):
        dact = _nt(dh_ref[...].astype(BF16), w_ref[...].reshape(tf, K))
        g = gate_ref[...].astype(F32)
        sig = 0.5 * jnp.tanh(0.5 * g) + 0.5
        dup_ref[...] = (dact * (g * sig)).astype(BF16)
        dgate_ref[...] = (dact * up_ref[...].astype(F32) * (sig * (1.0 + g * (1.0 - sig)))).astype(BF16)

    tile = pl.BlockSpec((tm, tf), lambda j, i: (i, j))
    F = N_DEV * fb
    return pl.pallas_call(
        body,
        grid=(N_DEV // FF_GROUP, L // tm),
        in_specs=[pl.BlockSpec((tm, K), lambda j, i: (i, 0)), _ff_spec(w_down8, layer), tile, tile],
        out_specs=[tile, tile],
        out_shape=[jax.ShapeDtypeStruct((L, F), BF16), jax.ShapeDtypeStruct((L, F), BF16)],
        compiler_params=_params("arbitrary", "arbitrary"),
        name=name,
    )(dh, w_down8, gate, up)


def _pool_residual(scr, lo, tm, g, w, t):
    cols = slice(g * POOL_GROUP_DIM, (g + 1) * POOL_GROUP_DIM)
    cur = scr[lo:lo + tm, cols]
    s = cur
    for k in range(1, w):
        s = s + scr[lo - k:lo - k + tm, cols]
    cnt = jnp.minimum(t + 1, w).astype(F32)
    return s / cnt - cur


def pool_fwd(u, w_pool, scale, name):
    L, C = u.shape
    tm, halo = ROW_TILE, POOL_HALO

    def body(u_ref, halo_ref, w_ref, s_ref, y_ref, scr):
        i = pl.program_id(0)
        scr[0:halo, :] = jnp.where(i > 0, halo_ref[...], 0.0)
        scr[halo:halo + tm, :] = u_ref[...]
        t = i * tm + lax.broadcasted_iota(jnp.int32, (tm, POOL_GROUP_DIM), 0)
        for g, w in enumerate(POOL_WINDOWS):
            cols = slice(g * POOL_GROUP_DIM, (g + 1) * POOL_GROUP_DIM)
            p = _pool_residual(scr, halo, tm, g, w, t)
            y = _nn(p.astype(BF16), w_ref[g]) * s_ref[:, cols]
            y_ref[:, cols] = y.astype(y_ref.dtype)

    return pl.pallas_call(
        body,
        grid=(L // tm,),
        in_specs=[
            _row_spec(tm, C),
            pl.BlockSpec((halo, C), lambda i: (jnp.maximum(i * (tm // halo) - 1, 0), 0)),
            _const_spec(w_pool.shape),
            _const_spec((1, C)),
        ],
        out_specs=_row_spec(tm, C),
        out_shape=jax.ShapeDtypeStruct((L, 2 * C), BF16),
        scratch_shapes=[pltpu.VMEM((tm + halo, C), F32)],
        compiler_params=_params("arbitrary"),
        name=name,
    )(u, u, w_pool, scale)


def pool_bwd(u, dy, w_pool, scale, name):
    L, C = u.shape
    tm, halo = ROW_TILE, POOL_HALO
    n_tiles = L // tm
    last_halo = L // halo - 1

    def body(u_ref, uh_ref, dy_ref, dyh_ref, w_ref, s_ref, du_ref, dw_ref, ds_ref, scr_u, scr_q):
        i = pl.program_id(0)

        @pl.when(i == 0)
        def _():
            dw_ref[...] = jnp.zeros_like(dw_ref)
            ds_ref[...] = jnp.zeros_like(ds_ref)

        scr_u[0:halo, :] = jnp.where(i > 0, uh_ref[...], 0.0)
        scr_u[halo:halo + tm, :] = u_ref[...]
        t = i * tm + lax.broadcasted_iota(jnp.int32, (tm, POOL_GROUP_DIM), 0)
        th = (i + 1) * tm + lax.broadcasted_iota(jnp.int32, (halo, POOL_GROUP_DIM), 0)
        for g, w in enumerate(POOL_WINDOWS):
            cols = slice(g * POOL_GROUP_DIM, (g + 1) * POOL_GROUP_DIM)
            p = _pool_residual(scr_u, halo, tm, g, w, t).astype(BF16)
            wg = w_ref[g]
            sc = s_ref[:, cols]
            dy = dy_ref[:, cols]
            ds_ref[:, cols] += jnp.sum(dy * _nn(p, wg), axis=0, keepdims=True)
            dys = (dy * sc).astype(BF16)
            dw_ref[g] += _tn(p, dys)
            dp = _nt(dys, wg)
            dyh = jnp.where(i < n_tiles - 1, dyh_ref[:, cols], 0.0)
            dph = _nt((dyh * sc).astype(BF16), wg)
            scr_q[0:tm, cols] = dp / jnp.minimum(t + 1, w).astype(F32)
            scr_q[tm:tm + halo, cols] = dph / jnp.minimum(th + 1, w).astype(F32)
            acc = scr_q[0:tm, cols]
            for k in range(1, w):
                acc = acc + scr_q[k:k + tm, cols]
            du_ref[:, cols] = acc - dp

    return pl.pallas_call(
        body,
        grid=(n_tiles,),
        in_specs=[
            _row_spec(tm, C),
            pl.BlockSpec((halo, C), lambda i: (jnp.maximum(i * (tm // halo) - 1, 0), 0)),
            _row_spec(tm, C),
            pl.BlockSpec((halo, C), lambda i: (jnp.minimum((i + 1) * (tm // halo), last_halo), 0)),
            _const_spec(w_pool.shape),
            _const_spec((1, C)),
        ],
        out_specs=[_row_spec(tm, C), _const_spec(w_pool.shape), _const_spec((1, C))],
        out_shape=[
            jax.ShapeDtypeStruct((L, C), F32),
            jax.ShapeDtypeStruct(w_pool.shape, F32),
            jax.ShapeDtypeStruct((1, C), F32),
        ],
        scratch_shapes=[pltpu.VMEM((tm + halo, C), F32), pltpu.VMEM((tm + halo, C), F32)],
        compiler_params=_params("arbitrary"),
        name=name,
    )(u, u, dy, dy, w_pool, scale)


def _head_masks(rows):
    lane = lax.broadcasted_iota(jnp.int32, (rows, HEAD_PAD), 1)
    return lane, lane < QK_NOPE_DIM, (lane >= QK_NOPE_DIM) & (lane < QK_HEAD_DIM)


def _rope_swap(x, lane):
    half = QK_ROPE_DIM // 2
    swapped = jnp.where(lane < QK_NOPE_DIM + half, pltpu.roll(x, HEAD_PAD - half, 1), pltpu.roll(x, half, 1))
    return jnp.where((lane >= QK_NOPE_DIM) & (lane < QK_HEAD_DIM), swapped, 0.0)


def _seg_mean(v, seg_ref):
    hi = v.astype(BF16)
    lo = (v - hi.astype(F32)).astype(BF16)
    seg = seg_ref[...]
    return _nn(hi, seg) + _nn(lo, seg)


def _segment_matrix():
    lane = jnp.arange(HEAD_PAD)
    seg = jnp.where(lane < QK_NOPE_DIM, 0, jnp.where(lane < QK_HEAD_DIM, 1, 2))
    inv = jnp.where(lane < QK_NOPE_DIM, 1.0 / QK_NOPE_DIM, jnp.where(lane < QK_HEAD_DIM, 1.0 / QK_ROPE_DIM, 0.0))
    return jnp.where(seg[:, None] == seg[None, :], inv[None, :], 0.0).astype(BF16)


def qk_prep_fwd(q, kv, kr, cosf, sins, seg, gq, gkn, gkr, name):
    L = q.shape[0]
    tm = ROW_TILE
    W = MLA_HEADS * HEAD_PAD

    def body(q_ref, kv_ref, kr_ref, c_ref, s_ref, seg_ref, gq_ref, gkn_ref, gkr_ref, qo_ref, ko_ref, vo_ref):
        lane, m_n, _ = _head_masks(tm)
        cosf_, sins_ = c_ref[...], s_ref[...]
        kr_ = kr_ref[...]
        rk = lax.rsqrt(_seg_mean(kr_ * kr_, seg_ref) + RMS_EPS)
        krn = kr_ * rk * gkr_ref[...]
        krf = krn * cosf_ + _rope_swap(krn, lane) * sins_
        for h in range(MLA_HEADS):
            cols = slice(h * HEAD_PAD, (h + 1) * HEAD_PAD)
            qh = q_ref[:, cols]
            qn = qh * lax.rsqrt(_seg_mean(qh * qh, seg_ref) + RMS_EPS) * gq_ref[...]
            qo_ref[:, cols] = (qn * cosf_ + _rope_swap(qn, lane) * sins_).astype(BF16)
            kh = jnp.where(m_n, kv_ref[:, cols], 0.0)
            rn = lax.rsqrt(_seg_mean(kh * kh, seg_ref) + RMS_EPS)
            ko_ref[:, cols] = (kh * rn * gkn_ref[...] + krf).astype(BF16)
        for p in range(MLA_HEADS // 2):
            even = kv_ref[:, 2 * p * HEAD_PAD:(2 * p + 1) * HEAD_PAD]
            odd = kv_ref[:, (2 * p + 1) * HEAD_PAD:(2 * p + 2) * HEAD_PAD]
            pair = jnp.where(m_n, pltpu.roll(even, V_HEAD_DIM, 1), odd)
            vo_ref[:, p * LANES:(p + 1) * LANES] = pair.astype(BF16)

    vec = _const_spec((1, HEAD_PAD))
    return pl.pallas_call(
        body,
        grid=(L // tm,),
        in_specs=[_row_spec(tm, W), _row_spec(tm, W), _row_spec(tm, HEAD_PAD), _row_spec(tm, HEAD_PAD),
                  _row_spec(tm, HEAD_PAD), _const_spec((HEAD_PAD, HEAD_PAD)), vec, vec, vec],
        out_specs=[_row_spec(tm, W), _row_spec(tm, W), _row_spec(tm, W // 2)],
        out_shape=[jax.ShapeDtypeStruct((L, W), BF16), jax.ShapeDtypeStruct((L, W), BF16),
                   jax.ShapeDtypeStruct((L, W // 2), BF16)],
        compiler_params=_params("arbitrary"),
        name=name,
    )(q, kv, kr, cosf, sins, seg, gq, gkn, gkr)


def qk_prep_bwd(dqo, dko, dv, q, kv, kr, cosf, sins, seg, gq, gkn, gkr, name):
    L = q.shape[0]
    tm = ROW_TILE
    W = MLA_HEADS * HEAD_PAD

    def body(dqo_ref, dko_ref, dv_ref, q_ref, kv_ref, kr_ref, c_ref, s_ref, seg_ref, gq_ref, gkn_ref, gkr_ref,
             dq_ref, dkv_ref, dkr_ref, dgq_ref, dgkn_ref, dgkr_ref):
        @pl.when(pl.program_id(0) == 0)
        def _():
            dgq_ref[...] = jnp.zeros_like(dgq_ref)
            dgkn_ref[...] = jnp.zeros_like(dgkn_ref)
            dgkr_ref[...] = jnp.zeros_like(dgkr_ref)

        lane, m_n, _ = _head_masks(tm)
        cosf_, sins_ = c_ref[...], s_ref[...]
        dgq = jnp.zeros((1, HEAD_PAD), F32)
        dgkn = jnp.zeros((1, HEAD_PAD), F32)
        dkrf = jnp.zeros((tm, HEAD_PAD), F32)
        for h in range(MLA_HEADS):
            cols = slice(h * HEAD_PAD, (h + 1) * HEAD_PAD)
            dy = dqo_ref[:, cols]
            dqn = dy * cosf_ + _rope_swap(dy * sins_, lane)
            qh = q_ref[:, cols]
            rinv = lax.rsqrt(_seg_mean(qh * qh, seg_ref) + RMS_EPS)
            xh = qh * rinv
            dgq = dgq + jnp.sum(dqn * xh, axis=0, keepdims=True)
            dxh = dqn * gq_ref[...]
            dq_ref[:, cols] = rinv * (dxh - xh * _seg_mean(dxh * xh, seg_ref))
            dk = dko_ref[:, cols]
            dkrf = dkrf + dk
            kh = jnp.where(m_n, kv_ref[:, cols], 0.0)
            rn = lax.rsqrt(_seg_mean(kh * kh, seg_ref) + RMS_EPS)
            xk = kh * rn
            dgkn = dgkn + jnp.sum(dk * xk, axis=0, keepdims=True)
            dxk = dk * gkn_ref[...]
            dkn = rn * (dxk - xk * _seg_mean(dxk * xk, seg_ref))
            dvp = dv_ref[:, (h // 2) * LANES:(h // 2 + 1) * LANES]
            dvh = pltpu.roll(dvp, V_HEAD_DIM, 1) if h % 2 == 0 else dvp
            dkv_ref[:, cols] = jnp.where(m_n, dkn, dvh)
        kr_ = kr_ref[...]
        rk = lax.rsqrt(_seg_mean(kr_ * kr_, seg_ref) + RMS_EPS)
        xr = kr_ * rk
        dkrn = dkrf * cosf_ + _rope_swap(dkrf * sins_, lane)
        dgkr_ref[...] += jnp.sum(dkrn * xr, axis=0, keepdims=True)
        dxr = dkrn * gkr_ref[...]
        dkr_ref[...] = rk * (dxr - xr * _seg_mean(dxr * xr, seg_ref))
        dgq_ref[...] += dgq
        dgkn_ref[...] += dgkn

    vec = _const_spec((1, HEAD_PAD))
    return pl.pallas_call(
        body,
        grid=(L // tm,),
        in_specs=[_row_spec(tm, W), _row_spec(tm, W), _row_spec(tm, W // 2), _row_spec(tm, W), _row_spec(tm, W),
                  _row_spec(tm, HEAD_PAD), _row_spec(tm, HEAD_PAD), _row_spec(tm, HEAD_PAD),
                  _const_spec((HEAD_PAD, HEAD_PAD)), vec, vec, vec],
        out_specs=[_row_spec(tm, W), _row_spec(tm, W), _row_spec(tm, HEAD_PAD), vec, vec, vec],
        out_shape=[jax.ShapeDtypeStruct((L, W), F32), jax.ShapeDtypeStruct((L, W), F32),
                   jax.ShapeDtypeStruct((L, HEAD_PAD), F32)] + [jax.ShapeDtypeStruct((1, HEAD_PAD), F32)] * 3,
        compiler_params=_params("arbitrary"),
        name=name,
    )(dqo, dko, dv, q, kv, kr, cosf, sins, seg, gq, gkn, gkr)


def attn_fwd(qp, kp, v, cat, name, gather=()):
    L = qp.shape[0]
    tq = ROW_TILE
    n_i = L // tq
    pair_w = 2 * HEAD_PAD
    n_pairs = MLA_HEADS // 2
    n_g = len(gather)

    def body(q_ref, k_ref, v_ref, cat_in, *rest):
        del cat_in
        o_ref, lse_ref, cat_ref = rest[n_g:n_g + 3]
        i = pl.program_id(1)
        if n_g:
            pair = pl.program_id(0)
            start, forward, finish = _gather_steps(rest[:n_g], rest[n_g + 3:2 * n_g + 3], *rest[2 * n_g + 3:])
            pl.when((pair == 0) & (i == 0))(start)
            pl.when((pair == n_pairs // 2) & (i == 0))(forward)
        left = lax.broadcasted_iota(jnp.int32, (tq, LANES), 1) < V_HEAD_DIM
        row = lax.broadcasted_iota(jnp.int32, (tq, tq), 0)
        col = lax.broadcasted_iota(jnp.int32, (tq, tq), 1)

        def step(j, carry, masked, width=1):
            rows = pl.ds(pl.multiple_of(j * (tq * width), tq), tq * width)
            vv = v_ref[rows, :]
            out = []
            for hh in range(2):
                cols = slice(hh * HEAD_PAD, (hh + 1) * HEAD_PAD)
                m, l, acc = carry[hh]
                s = _nt(q_ref[:, cols], k_ref[rows, cols]) * ATTN_SCALE
                if masked:
                    s = jnp.where(col <= row, s, -jnp.inf)
                m_new = jnp.maximum(m, jnp.max(s, axis=1, keepdims=True))
                alpha = jnp.exp(m - m_new)
                p = jnp.exp(s - m_new)
                l = alpha * l + jnp.sum(p, axis=1, keepdims=True)
                acc = alpha * acc + _nn(p.astype(BF16), vv)
                out.append((m_new, l, acc))
            return tuple(out)

        one = (jnp.full((tq, 1), -jnp.inf, F32), jnp.zeros((tq, 1), F32), jnp.zeros((tq, LANES), F32))
        doubles = jnp.right_shift(i, 1)
        carry = lax.fori_loop(0, doubles, functools.partial(step, masked=False, width=2), (one, one))
        carry = lax.fori_loop(2 * doubles, i, functools.partial(step, masked=False), carry)
        (m0, l0, a0), (m1, l1, a1) = step(i, carry, True)
        o = jnp.where(left, a0 / l0, a1 / l1)
        o_ref[...] = o
        cat_ref[...] = o.astype(BF16)
        lse_ref[...] = jnp.where(left, m0 + jnp.log(l0), m1 + jnp.log(l1))
        if n_g:
            pl.when((pair == n_pairs - 1) & (i == n_i - 1))(finish)

    return pl.pallas_call(
        body,
        grid=(n_pairs, n_i),
        in_specs=[
            pl.BlockSpec((tq, pair_w), lambda p, i: (i, p)),
            pl.BlockSpec((L, pair_w), lambda p, i: (0, p)),
            pl.BlockSpec((L, LANES), lambda p, i: (0, p)),
            _ANY,
        ] + [_ANY] * n_g,
        out_specs=[
            pl.BlockSpec((tq, LANES), lambda p, i: (i, p)),
            pl.BlockSpec((tq, LANES), lambda p, i: (i, p)),
            pl.BlockSpec((tq, LANES), lambda p, i: (i, n_pairs + p)),
        ] + [_ANY] * n_g,
        out_shape=[jax.ShapeDtypeStruct((L, n_pairs * LANES), F32), jax.ShapeDtypeStruct((L, n_pairs * LANES), F32),
                   jax.ShapeDtypeStruct(cat.shape, cat.dtype)]
        + [jax.ShapeDtypeStruct((N_DEV,) + g.shape, g.dtype) for g in gather],
        scratch_shapes=_gather_scratch(n_g) if n_g else [],
        input_output_aliases={3: 2},
        compiler_params=_params("arbitrary", "arbitrary"),
        name=name,
    )(qp, kp, v, cat, *gather)


def attn_bwd(qp, kp, v, do, lse, delta, name, exchange=()):
    L = qp.shape[0]
    tq = ROW_TILE
    n_q = L // tq
    pair_w = 2 * HEAD_PAD
    n_pairs = MLA_HEADS // 2
    n_x = len(exchange)

    def body(q_ref, k_ref, v_ref, do_ref, lse_ref, dl_ref, *rest):
        dq_ref, dk_ref, dv_ref = rest[n_x:n_x + 3]
        j = pl.program_id(1)
        if n_x:
            pair = pl.program_id(0)
            start, finish = _chip_exchange_steps(rest[:n_x], rest[n_x + 3:2 * n_x + 3], *rest[2 * n_x + 3:])
            pl.when((pair == 0) & (j == 0))(start)

        @pl.when(j == 0)
        def _():
            dq_ref[...] = jnp.zeros_like(dq_ref)

        dk_ref[...] = jnp.zeros_like(dk_ref)
        dv_ref[...] = jnp.zeros_like(dv_ref)
        row = lax.broadcasted_iota(jnp.int32, (tq, tq), 0)
        col = lax.broadcasted_iota(jnp.int32, (tq, tq), 1)

        def step(t, carry, masked, width=1, first=0):
            rows = pl.ds(pl.multiple_of((first + t * width) * tq, tq), tq * width)
            do = do_ref[rows, :]
            left = lax.broadcasted_iota(jnp.int32, do.shape, 1) < V_HEAD_DIM
            vv = v_ref[...]
            dv = None
            for hh in range(2):
                cols = slice(hh * HEAD_PAD, (hh + 1) * HEAD_PAD)
                stat = slice(hh * V_HEAD_DIM, hh * V_HEAD_DIM + 1)
                q = q_ref[rows, cols]
                k = k_ref[:, cols]
                dom = jnp.where(left if hh == 0 else jnp.logical_not(left), do, 0.0).astype(BF16)
                s = _nt(q, k) * ATTN_SCALE
                p = jnp.exp(s - lse_ref[rows, stat])
                if masked:
                    p = jnp.where(col <= row, p, 0.0)
                dp = _nt(dom, vv)
                ds = (p * (dp - dl_ref[rows, stat]) * ATTN_SCALE).astype(BF16)
                dq_ref[rows, cols] += _nn(ds, k)
                dk_ref[:, cols] += _tn(ds, q)
                t = _tn(p.astype(BF16), dom)
                dv = t if dv is None else dv + t
            dv_ref[...] += dv
            return carry

        step(j, 0, True)
        doubles = jnp.right_shift(n_q - 1 - j, 1)
        lax.fori_loop(0, doubles, functools.partial(step, masked=False, width=2, first=j + 1), 0)
        lax.fori_loop(j + 1 + 2 * doubles, n_q, functools.partial(step, masked=False), 0)
        if n_x:
            pl.when((pair == n_pairs - 1) & (j == n_q - 1))(finish)

    return pl.pallas_call(
        body,
        grid=(n_pairs, n_q),
        in_specs=[
            pl.BlockSpec((L, pair_w), lambda p, j: (0, p)),
            pl.BlockSpec((tq, pair_w), lambda p, j: (j, p)),
            pl.BlockSpec((tq, LANES), lambda p, j: (j, p)),
            pl.BlockSpec((L, LANES), lambda p, j: (0, p)),
            pl.BlockSpec((L, LANES), lambda p, j: (0, p)),
            pl.BlockSpec((L, LANES), lambda p, j: (0, p)),
        ] + [_ANY] * n_x,
        out_specs=[
            pl.BlockSpec((L, pair_w), lambda p, j: (0, p)),
            pl.BlockSpec((tq, pair_w), lambda p, j: (j, p)),
            pl.BlockSpec((tq, LANES), lambda p, j: (j, p)),
        ] + [_ANY] * n_x,
        out_shape=[
            jax.ShapeDtypeStruct((L, n_pairs * pair_w), F32),
            jax.ShapeDtypeStruct((L, n_pairs * pair_w), F32),
            jax.ShapeDtypeStruct((L, n_pairs * LANES), F32),
        ] + [jax.ShapeDtypeStruct(e.shape, e.dtype) for e in exchange],
        scratch_shapes=_chip_exchange_scratch(n_x) if n_x else [],
        compiler_params=_params("arbitrary", "arbitrary"),
        name=name,
    )(qp, kp, v, do, lse, delta, *exchange)


def loss_head(h, target, n_real, name):
    L, D = h.shape
    tm = ROW_TILE

    def body(h_ref, t_ref, dh_ref, sq_ref):
        i = pl.program_id(0)

        @pl.when(i == 0)
        def _():
            sq_ref[...] = jnp.zeros_like(sq_ref)

        t = i * tm + lax.broadcasted_iota(jnp.int32, (tm, D), 0)
        real = (t >= N_META) & (t < N_META + n_real)
        diff = jnp.where(real, h_ref[...] - t_ref[...], 0.0)
        dh_ref[...] = diff * (1.0 / D)
        sq_ref[...] += jnp.sum(diff * diff, axis=0, keepdims=True)

    return pl.pallas_call(
        body,
        grid=(L // tm,),
        in_specs=[_row_spec(tm, D), _row_spec(tm, D)],
        out_specs=[_row_spec(tm, D), _const_spec((1, D))],
        out_shape=[jax.ShapeDtypeStruct((L, D), F32), jax.ShapeDtypeStruct((1, D), F32)],
        compiler_params=_params("arbitrary"),
        name=name,
    )(h, target)


def _mesh_position():
    x, y, c = lax.axis_index("x"), lax.axis_index("y"), lax.axis_index("c")
    return x, y, c, 4 * x + 2 * y + c


def _flip(x, y, c, k):
    px = 1 - x if k & 4 else x
    py = 1 - y if k & 2 else y
    pc = 1 - c if k & 1 else c
    return (px, py, pc), 4 * px + 2 * py + pc


def _other_chips(x, y):
    return [(1 - x, y), (x, 1 - y), (1 - x, 1 - y)]


def _dev_index(px, py, pc):
    return 4 * px + 2 * py + pc


_ANY = pl.BlockSpec(memory_space=pl.ANY)


def cast_bf16(arrays, name):
    n = len(arrays)
    depth = arrays[0].shape[0]

    def body(*refs):
        for k, src in enumerate(refs[:n]):
            for l in range(depth):
                refs[n + k * depth + l][0] = src[l].astype(BF16)

    return pl.pallas_call(
        body,
        out_shape=[jax.ShapeDtypeStruct((1,) + a.shape[1:], BF16) for a in arrays for _ in range(depth)],
        compiler_params=pltpu.CompilerParams(vmem_limit_bytes=VMEM_LIMIT),
        name=name,
    )(*arrays)


def _gather_steps(srcs, dsts, send_sems, recv_sems, local_sems):
    n = len(srcs)
    x, y, c, me = _mesh_position()
    sibling = (x, y, 1 - c)
    chips = _other_chips(x, y)

    def copy(k, b, block, to, from_src=False):
        dst = dsts[b].at[block]
        return pltpu.make_async_remote_copy(
            src_ref=srcs[b] if from_src else dst, dst_ref=dst, send_sem=send_sems.at[k * n + b],
            recv_sem=recv_sems.at[k * n + b], device_id=to, device_id_type=pl.DeviceIdType.MESH)

    def own():
        return [pltpu.make_async_copy(srcs[b], dsts[b].at[me], local_sems.at[b]) for b in range(n)]

    def first():
        out = []
        for b in range(n):
            out.append(copy(0, b, me, sibling, from_src=True))
            out += [copy(1 + q, b, me, (*chip, c), from_src=True) for q, chip in enumerate(chips)]
        return out

    def passed():
        return [copy(4 + q, b, _dev_index(*chip, c), sibling) for q, chip in enumerate(chips) for b in range(n)]

    def start():
        for cp in own() + first():
            cp.start()

    def forward():
        for q, chip in enumerate(chips):
            for b in range(n):
                copy(1 + q, b, _dev_index(*chip, c), sibling).wait_recv()
        for cp in passed():
            cp.start()

    def finish():
        for b in range(n):
            copy(0, b, _dev_index(x, y, 1 - c), sibling).wait_recv()
            for q, chip in enumerate(chips):
                copy(4 + q, b, _dev_index(*chip, 1 - c), sibling).wait_recv()
        for cp in first() + passed():
            cp.wait_send()
        for cp in own():
            cp.wait()

    return start, forward, finish


def _gather_scratch(n):
    copies = N_DEV - 1
    return [pltpu.SemaphoreType.DMA((copies * n,)), pltpu.SemaphoreType.DMA((copies * n,)),
            pltpu.SemaphoreType.DMA((n,))]


def all_gather(payloads, name):
    n = len(payloads)

    def body(*refs):
        start, forward, finish = _gather_steps(refs[:n], refs[n:2 * n], *refs[2 * n:])
        start()
        forward()
        finish()

    return pl.pallas_call(
        body,
        in_specs=[_ANY] * n,
        out_specs=[_ANY] * n,
        out_shape=[jax.ShapeDtypeStruct((N_DEV,) + p.shape, p.dtype) for p in payloads],
        scratch_shapes=_gather_scratch(n),
        name=name,
    )(*payloads)


def exchange_pair(bigs, small, name):
    n = len(bigs)
    has_small = small is not None
    n_big = N_CHIPS * n
    n_sems = n_big + (N_DEV - 1 if has_small else 0)

    def body(*refs):
        big_refs = refs[:n]
        pos = n + has_small
        sib_refs = refs[pos:pos + n]
        send_sems, recv_sems, local_sem = refs[-3:]
        x, y, c, me = _mesh_position()
        sibling = (x, y, 1 - c)
        copies = []
        for b in range(n):
            for q in range(N_CHIPS):
                copies.append(pltpu.make_async_remote_copy(
                    src_ref=big_refs[b].at[_dev_index(q // 2, q % 2, 1 - c)], dst_ref=sib_refs[b].at[q],
                    send_sem=send_sems.at[b * N_CHIPS + q], recv_sem=recv_sems.at[b * N_CHIPS + q],
                    device_id=sibling, device_id_type=pl.DeviceIdType.MESH))
        waits = list(copies)
        if has_small:
            small_ref, gsmall_ref = refs[n], refs[pos + n]
            own = pltpu.make_async_copy(small_ref.at[me], gsmall_ref.at[me], local_sem.at[0])
            own.start()
            for k in range(1, N_DEV):
                peer, peer_idx = _flip(x, y, c, k)
                s = n_big + k - 1
                copies.append(pltpu.make_async_remote_copy(
                    src_ref=small_ref.at[peer_idx], dst_ref=gsmall_ref.at[me], send_sem=send_sems.at[s],
                    recv_sem=recv_sems.at[s], device_id=peer, device_id_type=pl.DeviceIdType.MESH))
                waits.append(pltpu.make_async_remote_copy(
                    src_ref=small_ref.at[peer_idx], dst_ref=gsmall_ref.at[peer_idx], send_sem=send_sems.at[s],
                    recv_sem=recv_sems.at[s], device_id=peer, device_id_type=pl.DeviceIdType.MESH))
        for cp in copies:
            cp.start()
        for cp in waits:
            cp.wait_recv()
        for cp in copies:
            cp.wait_send()
        if has_small:
            own.wait()

    out_shape = [jax.ShapeDtypeStruct((N_CHIPS,) + b.shape[1:], b.dtype) for b in bigs]
    args = list(bigs)
    if has_small:
        out_shape.append(jax.ShapeDtypeStruct(small.shape, small.dtype))
        args.append(small)
    return pl.pallas_call(
        body,
        in_specs=[_ANY] * len(args),
        out_specs=[_ANY] * len(out_shape),
        out_shape=out_shape,
        scratch_shapes=[pltpu.SemaphoreType.DMA((n_sems,)), pltpu.SemaphoreType.DMA((n_sems,)),
                        pltpu.SemaphoreType.DMA((1,))],
        name=name,
    )(*args)


def pair_sum(bigs, from_sibling, core, name):
    n = len(bigs)

    def body(core_ref, *refs):
        del core_ref
        for mine, sib, out in zip(refs[:n], refs[n:2 * n], refs[2 * n:]):
            out[...] = (mine[...].astype(F32) + sib[...].astype(F32)).astype(out.dtype)

    def slot(shape, picked):
        zeros = (0,) * (len(shape) - 1)
        if picked:
            return pl.BlockSpec((None,) + shape[1:], lambda q, core_ref: (2 * q + core_ref[0],) + zeros)
        return pl.BlockSpec((None,) + shape[1:], lambda q, core_ref: (q,) + zeros)

    grid_spec = pltpu.PrefetchScalarGridSpec(
        num_scalar_prefetch=1,
        grid=(N_CHIPS,),
        in_specs=[slot(b.shape, True) for b in bigs] + [slot(s.shape, False) for s in from_sibling],
        out_specs=[slot(s.shape, False) for s in from_sibling],
    )
    return pl.pallas_call(
        body,
        grid_spec=grid_spec,
        out_shape=[jax.ShapeDtypeStruct(s.shape, s.dtype) for s in from_sibling],
        compiler_params=_params("arbitrary"),
        name=name,
    )(core, *bigs, *from_sibling)


def _chip_exchange_steps(part_refs, got_refs, send_sems, recv_sems, local_sems):
    n = len(part_refs)
    x, y, c, me = _mesh_position()
    mine = 2 * x + y

    def copies(landing):
        out = []
        for q, (px, py) in enumerate(_other_chips(x, y)):
            theirs = 2 * px + py
            for b in range(n):
                out.append(pltpu.make_async_remote_copy(
                    src_ref=part_refs[b].at[theirs], dst_ref=got_refs[b].at[theirs if landing else mine],
                    send_sem=send_sems.at[q * n + b], recv_sem=recv_sems.at[q * n + b],
                    device_id=(px, py, c), device_id_type=pl.DeviceIdType.MESH))
        return out

    def own():
        return [pltpu.make_async_copy(part_refs[b].at[mine], got_refs[b].at[mine], local_sems.at[b])
                for b in range(n)]

    def start():
        for cp in own() + copies(False):
            cp.start()

    def finish():
        for cp in copies(True):
            cp.wait_recv()
        for cp in copies(False):
            cp.wait_send()
        for cp in own():
            cp.wait()

    return start, finish


def _chip_exchange_scratch(n):
    return [pltpu.SemaphoreType.DMA(((N_CHIPS - 1) * n,)), pltpu.SemaphoreType.DMA(((N_CHIPS - 1) * n,)),
            pltpu.SemaphoreType.DMA((n,))]


def exchange_chips(partials, name):
    n = len(partials)

    def body(*refs):
        start, finish = _chip_exchange_steps(refs[:n], refs[n:2 * n], *refs[2 * n:])
        start()
        finish()

    return pl.pallas_call(
        body,
        in_specs=[_ANY] * n,
        out_specs=[_ANY] * n,
        out_shape=[jax.ShapeDtypeStruct(p.shape, p.dtype) for p in partials],
        scratch_shapes=_chip_exchange_scratch(n),
        name=name,
    )(*partials)


def _adamw_math(g, w, m, v):
    m_new = ADAM_B1 * m + (1.0 - ADAM_B1) * g
    v_new = ADAM_B2 * v + (1.0 - ADAM_B2) * (g * g)
    m_hat = m_new / (1.0 - ADAM_B1 ** ADAM_STEP)
    v_hat = v_new / (1.0 - ADAM_B2 ** ADAM_STEP)
    delta = -ADAM_LR * (m_hat / (jnp.sqrt(v_hat) + ADAM_EPS) + ADAM_WD * w)
    return delta, m_new, v_new


def _chip_total(p_ref):
    g = p_ref[0].astype(F32)
    for q in range(1, N_CHIPS):
        g = g + p_ref[q].astype(F32)
    return g


def chip_sum(parts, name):
    depth = len(parts)

    def body(*refs):
        for l in range(depth):
            refs[depth][l] = _chip_total(refs[l])

    return pl.pallas_call(
        body,
        out_shape=jax.ShapeDtypeStruct((depth,) + parts[0].shape[1:], F32),
        compiler_params=pltpu.CompilerParams(vmem_limit_bytes=VMEM_LIMIT),
        name=name,
    )(*parts)


def adamw_shard(parts, w, m, v, name, grad=None):
    depth = w.shape[0]
    n_in = 1 if grad is not None else depth

    def body(*refs):
        w_ref, m_ref, v_ref, g_ref, d_ref, mo_ref, vo_ref = refs[n_in:]
        for l in range(depth):
            g = refs[0][l] if grad is not None else _chip_total(refs[l])
            delta, m_new, v_new = _adamw_math(g, w_ref[l], m_ref[l], v_ref[l])
            g_ref[l] = g
            d_ref[l] = delta
            mo_ref[l] = m_new
            vo_ref[l] = v_new

    return pl.pallas_call(
        body,
        out_shape=[jax.ShapeDtypeStruct(w.shape, F32)] * 4,
        compiler_params=pltpu.CompilerParams(vmem_limit_bytes=VMEM_LIMIT),
        name=name,
    )(*([grad] if grad is not None else parts), w, m, v)


def adamw_packed(parts, w, m, v, name):
    n_slots = parts.shape[0]

    def body(p_ref, w_ref, m_ref, v_ref, g_ref, d_ref, mo_ref, vo_ref):
        g = p_ref[0]
        for s in range(1, n_slots):
            g = g + p_ref[s]
        delta, m_new, v_new = _adamw_math(g, w_ref[...], m_ref[...], v_ref[...])
        g_ref[...] = g
        d_ref[...] = delta
        mo_ref[...] = m_new
        vo_ref[...] = v_new

    return pl.pallas_call(
        body,
        out_shape=[jax.ShapeDtypeStruct(w.shape, F32)] * 4,
        compiler_params=pltpu.CompilerParams(vmem_limit_bytes=VMEM_LIMIT),
        name=name,
    )(parts, w, m, v)


SHARDED = ("w_in", "w_q_b", "w_kv_b", "w_out", "w_gate", "w_up", "w_down")
TRANSPOSED = ("w_in", "w_gate", "w_up")
BEFORE_ATTENTION = ("w_in", "w_q_b", "w_kv_b")
FFN_WEIGHTS = ("w_gate", "w_up", "w_down")
REPLICATED = ("attn_norm_g", "w_pool", "pool_scale", "q_a_norm_g", "kv_a_norm_g", "q_norm_g", "k_norm_g",
              "ffn_norm_g")


def _pack_flat(arrays, rows):
    flat = jnp.concatenate([a.reshape(-1) for a in arrays])
    return jnp.pad(flat, (0, rows * LANES - flat.shape[0])).reshape(rows, LANES)


def _unpack_flat(packed, shapes):
    flat, out, at = packed.reshape(-1), [], 0
    for shp in shapes:
        n = 1
        for d in shp:
            n *= d
        out.append(flat[at:at + n].reshape(shp))
        at += n
    return out


def _rope_lane_tables(length):
    inv = 1.0 / (ROPE_THETA ** (jnp.arange(0, QK_ROPE_DIM, 2, dtype=F32) / QK_ROPE_DIM))
    ang = jnp.arange(length, dtype=F32)[:, None] * inv[None, :]
    cos, sin = jnp.cos(ang), jnp.sin(ang)
    ones = jnp.ones((length, QK_NOPE_DIM), F32)
    zeros = jnp.zeros((length, QK_NOPE_DIM), F32)
    tail = HEAD_PAD - QK_HEAD_DIM
    cosf = jnp.concatenate([ones, cos, cos, ones[:, :tail]], axis=1)
    sins = jnp.concatenate([zeros, -sin, sin, zeros[:, :tail]], axis=1)
    return cosf, sins


def _pad_lanes(vec, at, width=HEAD_PAD):
    return jnp.pad(vec, (at, width - at - vec.shape[0])).reshape(1, width)


def kernel(x, meta_tokens, attn_norm_g, w_in, w_pool, pool_scale, q_a_norm_g, w_q_b, kv_a_norm_g, w_kv_b, q_norm_g, k_norm_g, w_out, ffn_norm_g, w_gate, w_up, w_down, loss_target, m_meta_tokens, m_attn_norm_g, m_w_in, m_w_pool, m_pool_scale, m_q_a_norm_g, m_w_q_b, m_kv_a_norm_g, m_w_kv_b, m_q_norm_g, m_k_norm_g, m_w_out, m_ffn_norm_g, m_w_gate, m_w_up, m_w_down, v_meta_tokens, v_attn_norm_g, v_w_in, v_w_pool, v_pool_scale, v_q_a_norm_g, v_w_q_b, v_kv_a_norm_g, v_w_kv_b, v_q_norm_g, v_k_norm_g, v_w_out, v_ffn_norm_g, v_w_gate, v_w_up, v_w_down):
    weights = dict(meta_tokens=meta_tokens, attn_norm_g=attn_norm_g, w_in=w_in, w_pool=w_pool, pool_scale=pool_scale,
                   q_a_norm_g=q_a_norm_g, w_q_b=w_q_b, kv_a_norm_g=kv_a_norm_g, w_kv_b=w_kv_b, q_norm_g=q_norm_g,
                   k_norm_g=k_norm_g, w_out=w_out, ffn_norm_g=ffn_norm_g, w_gate=w_gate, w_up=w_up, w_down=w_down)
    mom1 = dict(meta_tokens=m_meta_tokens, attn_norm_g=m_attn_norm_g, w_in=m_w_in, w_pool=m_w_pool,
                pool_scale=m_pool_scale, q_a_norm_g=m_q_a_norm_g, w_q_b=m_w_q_b, kv_a_norm_g=m_kv_a_norm_g,
                w_kv_b=m_w_kv_b, q_norm_g=m_q_norm_g, k_norm_g=m_k_norm_g, w_out=m_w_out, ffn_norm_g=m_ffn_norm_g,
                w_gate=m_w_gate, w_up=m_w_up, w_down=m_w_down)
    mom2 = dict(meta_tokens=v_meta_tokens, attn_norm_g=v_attn_norm_g, w_in=v_w_in, w_pool=v_w_pool,
                pool_scale=v_pool_scale, q_a_norm_g=v_q_a_norm_g, w_q_b=v_w_q_b, kv_a_norm_g=v_kv_a_norm_g,
                w_kv_b=v_w_kv_b, q_norm_g=v_q_norm_g, k_norm_g=v_k_norm_g, w_out=v_w_out, ffn_norm_g=v_ffn_norm_g,
                w_gate=v_w_gate, w_up=v_w_up, w_down=v_w_down)
    order = ("meta_tokens", "attn_norm_g", "w_in", "w_pool", "pool_scale", "q_a_norm_g", "w_q_b", "kv_a_norm_g",
             "w_kv_b", "q_norm_g", "k_norm_g", "w_out", "ffn_norm_g", "w_gate", "w_up", "w_down")
    depth = w_in.shape[0]
    seq = x.shape[1]
    length = N_META + seq
    lp = -(-length // ROW_TILE) * ROW_TILE
    in_cols = w_in.shape[2]

    local = {n: (jnp.swapaxes(weights[n], 1, 2) if n in TRANSPOSED else weights[n]) for n in SHARDED}
    cast = cast_bf16([local[n] for n in SHARDED], "cast_weights")
    shards = [{n: cast[k * depth + l] for k, n in enumerate(SHARDED)} for l in range(depth)]
    gathered = all_gather([shards[0][n] for n in BEFORE_ATTENTION] + [meta_tokens], "all_gather")
    g8l = [dict(zip(BEFORE_ATTENTION, gathered[:-1]))] + [{} for _ in range(depth - 1)]
    meta_full = jnp.transpose(gathered[-1], (1, 0, 2)).reshape(N_META, D_MODEL)

    s1, s2, s3 = POOL_WIDTH, POOL_WIDTH + Q_LORA_RANK, POOL_WIDTH + Q_LORA_RANK + KV_LORA_RANK
    zpad = lambda n: jnp.zeros((1, n, D_MODEL), BF16)

    def padded_in_proj(w8):
        w_in_t = jnp.swapaxes(w8, 0, 1).reshape(1, N_DEV * in_cols, D_MODEL)
        return jnp.concatenate([w_in_t[:, :s3], zpad(QK_NOPE_DIM), w_in_t[:, s3:],
                                zpad(HEAD_PAD - QK_HEAD_DIM)], axis=1)

    w_in_ps = [None] * depth
    w_pool_b = w_pool.astype(BF16)

    cosf, sins = _rope_lane_tables(lp)
    seg = _segment_matrix()
    row = lambda a, l: a[l].reshape(1, -1)

    h = jnp.concatenate([meta_full, x[0], jnp.zeros((lp - length, D_MODEL), F32)], axis=0)
    target = jnp.pad(loss_target[0], ((N_META, lp - length), (0, 0)))
    saved = []
    for l in range(depth):
        gq = _pad_lanes(q_norm_g[l], 0)
        gkn = _pad_lanes(k_norm_g[l, :QK_NOPE_DIM], 0)
        gkr = _pad_lanes(k_norm_g[l, QK_NOPE_DIM:], QK_NOPE_DIM)
        g8 = g8l[l]
        w_in_ps[l] = padded_in_proj(g8["w_in"])
        a, u, c_q, c_kv, kr = norm_mm(
            h, row(attn_norm_g, l), w_in_ps[l], 0,
            [(0, s1), (s1, Q_LORA_RANK), (s2, KV_LORA_RANK), (s3, HEAD_PAD)], [F32] * 4, "in_proj")
        cat = pool_fwd(u, w_pool_b[l], row(pool_scale, l), "pool_fwd")
        qn, q = norm_mm_heads(c_q, row(q_a_norm_g, l), g8["w_q_b"], 0, "q_proj")
        kvn, kv = norm_mm_heads(c_kv, row(kv_a_norm_g, l), g8["w_kv_b"], 0, "kv_proj")
        qp, kp, v = qk_prep_fwd(q, kv, kr, cosf, sins, seg, gq, gkn, gkr, "qk_prep_fwd")
        riders = [(l, n) for n in SHARDED if n not in g8]
        riders += [(l + 1, n) for n in BEFORE_ATTENTION if l + 1 < depth]
        o, lse, cat, *arrived = attn_fwd(qp, kp, v, cat, "attn_fwd_gather", gather=[shards[k][n] for k, n in riders])
        for (k, n), w8 in zip(riders, arrived):
            g8l[k][n] = w8
        h_mid, g = mm_res(cat, g8["w_out"], h, 0, "out_proj", gamma=row(ffn_norm_g, l))
        gate, up, act = ffn_up(g, g8["w_gate"], g8["w_up"], 0, "ffn_up")
        h_next = mm_res(act, g8["w_down"], h_mid, 0, "ffn_down")[0]
        saved.append(dict(h=h, a=a, u=u, c_q=c_q, c_kv=c_kv, kr=kr, qn=qn, q=q, kvn=kvn, kv=kv, v=v, qp=qp, kp=kp,
                          o=o, lse=lse, cat=cat, h_mid=h_mid, g=g, gate=gate, up=up, act=act,
                          gq=gq, gkn=gkn, gkr=gkr))
        h = h_next

    dh, sq = loss_head(h, target, seq, "loss_head")
    loss = lax.psum(0.5 / D_MODEL * jnp.sum(sq), ("x", "y", "c"))

    core = lax.axis_index("c").astype(jnp.int32).reshape(1)
    small_grads = {n: [None] * depth for n in REPLICATED}
    pending = []
    got = [{} for _ in range(depth)]

    def pair_reduce(l, names, slots, small):
        bigs = [slots[n] for n in names]
        if small is None:
            from_sibling, got_small = exchange_pair(bigs, None, "grad_exchange_pair"), None
        else:
            *from_sibling, got_small = exchange_pair(bigs, small, "grad_exchange_pair_small")
        partial = pair_sum(bigs, list(from_sibling), core, "grad_pair_sum")
        return [(l, n, p) for n, p in zip(names, partial)], got_small

    for l in reversed(range(depth)):
        s = saved[l]
        g8 = g8l[l]
        w_in_p = w_in_ps[l]
        slots = {}
        dgate, dup = ffn_bwd_act(dh, g8["w_down"], 0, s["gate"], s["up"], "ffn_bwd_act")
        slot_of = lambda n, full: full.reshape(g8[n].shape[:1] + g8[n].shape[2:])
        slots["w_down"] = slot_of("w_down", mm_tn(s["act"], dh, "dw_down", out_dtype=BF16))
        slots["w_gate"] = slot_of("w_gate", mm_tn(dgate, s["g"], "dw_gate", out_dtype=BF16))
        slots["w_up"] = slot_of("w_up", mm_tn(dup, s["g"], "dw_up", out_dtype=BF16))
        pending += pair_reduce(l, FFN_WEIGHTS, slots, None)[0]
        dh_mid, dg_ffn = mm_nt_normbwd([("gathered", dgate, g8["w_gate"]), ("gathered", dup, g8["w_up"])],
                                       s["h_mid"], row(ffn_norm_g, l), dh, 0, "ffn_bwd_in")
        small_grads["ffn_norm_g"][l] = dg_ffn[0]
        slots["w_out"] = slot_of("w_out", mm_tn(s["cat"], dh_mid, "dw_out", out_dtype=BF16))
        dy_pool, do, delta = out_proj_bwd(dh_mid, g8["w_out"], 0, s["o"], "out_proj_bwd")
        du, dw_pool, dscale = pool_bwd(s["u"], dy_pool, w_pool_b[l], row(pool_scale, l), "pool_bwd")
        small_grads["w_pool"][l] = dw_pool
        small_grads["pool_scale"][l] = dscale[0]
        dqp, dkp, dv, *arrived = attn_bwd(s["qp"], s["kp"], s["v"], do, s["lse"], delta, "attn_bwd_exchange",
                                          exchange=[p for _, _, p in pending])
        for (k, n, _), parts in zip(pending, arrived):
            got[k][n] = parts
        dq, dkv, dkr, dgq, dgkn, dgkr = qk_prep_bwd(dqp, dkp, dv, s["q"], s["kv"], s["kr"], cosf, sins, seg,
                                                    s["gq"], s["gkn"], s["gkr"], "qk_prep_bwd")
        small_grads["q_norm_g"][l] = dgq[0, :QK_HEAD_DIM]
        small_grads["k_norm_g"][l] = jnp.concatenate([dgkn[0, :QK_NOPE_DIM], dgkr[0, QK_NOPE_DIM:QK_HEAD_DIM]])
        heads_first = lambda full: jnp.swapaxes(full.reshape(full.shape[0], MLA_HEADS, HEAD_PAD), 0, 1)
        slots["w_q_b"] = heads_first(mm_tn(s["qn"], dq, "dw_q", out_dtype=BF16))[:, :, :QK_HEAD_DIM]
        dc_q, dg_qa = mm_nt_normbwd([("heads", dq, g8["w_q_b"])], s["c_q"], row(q_a_norm_g, l), None, 0,
                                    "q_proj_bwd")
        small_grads["q_a_norm_g"][l] = dg_qa[0]
        slots["w_kv_b"] = heads_first(mm_tn(s["kvn"], dkv, "dw_kv", out_dtype=BF16))
        dc_kv, dg_kva = mm_nt_normbwd([("heads", dkv, g8["w_kv_b"])], s["c_kv"], row(kv_a_norm_g, l), None, 0,
                                      "kv_proj_bwd")
        small_grads["kv_a_norm_g"][l] = dg_kva[0]
        dw_in_t = jnp.concatenate(
            [mm_tn(du, s["a"], "dw_in_pool", out_dtype=BF16), mm_tn(dc_q, s["a"], "dw_in_q", out_dtype=BF16),
             mm_tn(dc_kv, s["a"], "dw_in_kv", out_dtype=BF16),
             mm_tn(dkr, s["a"], "dw_in_rope", out_dtype=BF16)[QK_NOPE_DIM:QK_HEAD_DIM]], axis=0)
        slots["w_in"] = slot_of("w_in", dw_in_t)
        dh, dg_attn = mm_nt_normbwd(
            [("rows", du, w_in_p, 0), ("rows", dc_q, w_in_p, s1), ("rows", dc_kv, w_in_p, s2),
             ("rows", dkr, w_in_p, s3)],
            s["h"], row(attn_norm_g, l), dh_mid, 0, "in_proj_bwd")
        small_grads["attn_norm_g"][l] = dg_attn[0]

        small_slots = None
        if l == 0:
            rep_shapes = [weights[n].shape for n in REPLICATED]
            rep_count = sum(int(jnp.size(weights[n])) for n in REPLICATED)
            rep_rows = -(-rep_count // (8 * LANES)) * 8
            rep_packed = _pack_flat([jnp.stack(small_grads[n], axis=0) for n in REPLICATED], rep_rows)
            meta_slots = jnp.transpose(dh[:N_META].reshape(N_META, N_DEV, LANES), (1, 0, 2))
            small_slots = jnp.concatenate(
                [meta_slots, jnp.broadcast_to(rep_packed[None], (N_DEV, rep_rows, LANES))], axis=1)
        pending, got_small = pair_reduce(l, [n for n in SHARDED if n not in FFN_WEIGHTS], slots, small_slots)
    for (k, n, _), parts in zip(pending, exchange_chips([p for _, _, p in pending], "grad_exchange_chips")):
        got[k][n] = parts

    grad_x = dh[N_META:length][None]

    per = [{} for _ in range(4)]
    for n in SHARDED:
        parts = [got[l][n] for l in range(depth)]
        if n in TRANSPOSED:
            grad = jnp.swapaxes(chip_sum(parts, "chip_sum_" + n), 1, 2)
            outs = adamw_shard(None, weights[n], mom1[n], mom2[n], "adamw_" + n, grad=grad)
        else:
            outs = adamw_shard(parts, weights[n], mom1[n], mom2[n], "adamw_" + n)
        for k in range(4):
            per[k][n] = outs[k]
    ps = lambda src: jnp.concatenate(
        [src["meta_tokens"], _pack_flat([src[n] for n in REPLICATED], rep_rows)], axis=0)
    small_out = adamw_packed(got_small, ps(weights), ps(mom1), ps(mom2), "adamw_small")
    for k in range(4):
        per[k]["meta_tokens"] = small_out[k][:N_META]
        per[k].update(zip(REPLICATED, _unpack_flat(small_out[k][N_META:], rep_shapes)))
    return (loss, grad_x, *[per[0][n] for n in order], *[per[1][n] for n in order],
            *[per[2][n] for n in order], *[per[3][n] for n in order])
```

```python
import functools

import jax
import jax.numpy as jnp
from jax import lax
from jax.experimental import pallas as pl
from jax.experimental.pallas import tpu as pltpu

F32 = jnp.float32
BF16 = jnp.bfloat16

D_MODEL = 1024
N_META = 16
POOL_WIDTH = 512
POOL_WINDOWS = (2, 4, 8, 16)
POOL_GROUP_DIM = 128
POOL_HALO = 16
MLA_HEADS = 8
QK_NOPE_DIM = 64
QK_ROPE_DIM = 32
QK_HEAD_DIM = 96
V_HEAD_DIM = 64
HEAD_PAD = 128
Q_LORA_RANK = 384
KV_LORA_RANK = 256
ROPE_THETA = 10000.0
RMS_EPS = 1e-6
ATTN_SCALE = QK_HEAD_DIM ** -0.5
LOG2_E = 1.4426950408889634
SCORE_SCALE = ATTN_SCALE * LOG2_E

ADAM_LR = 0.001
ADAM_B1 = 0.9
ADAM_B2 = 0.999
ADAM_EPS = 1e-08
ADAM_WD = 0.01
ADAM_STEP = 10

N_DEV = 8
N_CHIPS = 4
LANES = 128
ROW_TILE = 384
LONG_TILE = 1056
VMEM_LIMIT = 56 * 1024 * 1024


def _params(*sem):
    return pltpu.CompilerParams(dimension_semantics=sem, vmem_limit_bytes=VMEM_LIMIT)


def _row_spec(tile, width):
    return pl.BlockSpec((tile, width), lambda i: (i, 0))


def _const_spec(shape):
    return pl.BlockSpec(shape, lambda i: tuple(0 for _ in shape))


def _layer_spec(w, layer):
    return pl.BlockSpec((None,) + w.shape[1:], lambda *_: (layer, 0, 0))


def _gathered_spec(w8, layer):
    return pl.BlockSpec((N_DEV, None) + w8.shape[2:], lambda *_: (0, layer, 0, 0))


def _nt(a, b):
    return lax.dot_general(a, b, (((1,), (1,)), ((), ())), preferred_element_type=F32)


def _tn(a, b):
    return lax.dot_general(a, b, (((0,), (0,)), ((), ())), preferred_element_type=F32)


def _nn(a, b):
    return jnp.dot(a, b, preferred_element_type=F32)


def _silu(g):
    return g * (1.0 / (1.0 + jnp.exp(-g)))


def _rms(xf):
    return lax.rsqrt(jnp.mean(xf * xf, axis=-1, keepdims=True) + RMS_EPS)


def norm_mm(x, gamma, wt, layer, splits, dtypes, name):
    L, K = x.shape
    tm = ROW_TILE

    def body(x_ref, g_ref, w_ref, a_ref, *z_refs):
        xf = x_ref[...]
        a = ((xf * _rms(xf)) * g_ref[...]).astype(BF16)
        a_ref[...] = a
        z = _nt(a, w_ref[...])
        for (s, n), zr in zip(splits, z_refs):
            zr[...] = z[:, s:s + n].astype(zr.dtype)

    widths = [n for _, n in splits]
    return pl.pallas_call(
        body,
        grid=(L // tm,),
        in_specs=[_row_spec(tm, K), _const_spec((1, K)), _layer_spec(wt, layer)],
        out_specs=[_row_spec(tm, K)] + [_row_spec(tm, n) for n in widths],
        out_shape=[jax.ShapeDtypeStruct((L, K), BF16)]
        + [jax.ShapeDtypeStruct((L, n), dt) for n, dt in zip(widths, dtypes)],
        compiler_params=_params("arbitrary"),
        name=name,
    )(x, gamma, wt)


def norm_mm_heads(x, gamma, w8, layer, name):
    L, K = x.shape
    hw = w8.shape[-1]
    tm = ROW_TILE

    def body(x_ref, g_ref, w_ref, a_ref, z_ref):
        xf = x_ref[...]
        a = ((xf * _rms(xf)) * g_ref[...]).astype(BF16)
        a_ref[...] = a
        if hw < HEAD_PAD:
            z_ref[...] = jnp.zeros_like(z_ref)
        for j in range(MLA_HEADS):
            z_ref[:, j * HEAD_PAD:j * HEAD_PAD + hw] = _nn(a, w_ref[j])

    return pl.pallas_call(
        body,
        grid=(L // tm,),
        in_specs=[_row_spec(tm, K), _const_spec((1, K)), _gathered_spec(w8, layer)],
        out_specs=[_row_spec(tm, K), _row_spec(tm, MLA_HEADS * HEAD_PAD)],
        out_shape=[jax.ShapeDtypeStruct((L, K), BF16), jax.ShapeDtypeStruct((L, MLA_HEADS * HEAD_PAD), F32)],
        compiler_params=_params("arbitrary"),
        name=name,
    )(x, gamma, w8)


FF_GROUP = 4


def _ff_spec(w8, layer):
    return pl.BlockSpec((FF_GROUP, None) + w8.shape[2:], lambda j, i: (j, layer, 0, 0))


def ffn_up(g, w_gate8, w_up8, layer, name):
    L, K = g.shape
    fb = w_gate8.shape[-2]
    tm, tf = ROW_TILE, FF_GROUP * fb

    def body(a_ref, wg_ref, wu_ref, gate_ref, up_ref, act_ref):
        a = a_ref[...]
        gate = _nt(a, wg_ref[...].reshape(tf, K))
        up = _nt(a, wu_ref[...].reshape(tf, K))
        gate_ref[...] = gate.astype(BF16)
        up_ref[...] = up.astype(BF16)
        act_ref[...] = (_silu(gate) * up).astype(BF16)

    tile = pl.BlockSpec((tm, tf), lambda j, i: (i, j))
    F = N_DEV * fb
    return pl.pallas_call(
        body,
        grid=(N_DEV // FF_GROUP, L // tm),
        in_specs=[pl.BlockSpec((tm, K), lambda j, i: (i, 0)), _ff_spec(w_gate8, layer), _ff_spec(w_up8, layer)],
        out_specs=[tile, tile, tile],
        out_shape=[
            jax.ShapeDtypeStruct((L, F), BF16),
            jax.ShapeDtypeStruct((L, F), BF16),
            jax.ShapeDtypeStruct((L, F), BF16),
        ],
        compiler_params=_params("arbitrary", "arbitrary"),
        name=name,
    )(g, w_gate8, w_up8)


def mm_res(a, w8, res, layer, name, gamma=None):
    L = a.shape[0]
    kb, N = w8.shape[-2:]
    tm = ROW_TILE
    normed = gamma is not None

    def body(a_ref, w_ref, r_ref, *rest):
        out = r_ref[...] + _nn(a_ref[...], w_ref[...].reshape(N_DEV * kb, N))
        if normed:
            g_ref, o_ref, n_ref = rest
            n_ref[...] = ((out * _rms(out)) * g_ref[...]).astype(BF16)
        else:
            (o_ref,) = rest
        o_ref[...] = out

    in_specs = [_row_spec(tm, N_DEV * kb), _gathered_spec(w8, layer), _row_spec(tm, N)]
    out_specs = [_row_spec(tm, N)]
    out_shape = [jax.ShapeDtypeStruct((L, N), F32)]
    args = [a, w8, res]
    if normed:
        in_specs.append(_const_spec((1, N)))
        out_specs.append(_row_spec(tm, N))
        out_shape.append(jax.ShapeDtypeStruct((L, N), BF16))
        args.append(gamma)
    return pl.pallas_call(
        body,
        grid=(L // tm,),
        in_specs=in_specs,
        out_specs=out_specs,
        out_shape=out_shape,
        compiler_params=_params("arbitrary"),
        name=name,
    )(*args)


def out_proj_bwd(dz, w8, layer, o, name):
    L, N = dz.shape
    kb = w8.shape[-2]
    C = o.shape[1]
    tm = ROW_TILE

    def body(dz_ref, w_ref, o_ref, dy_ref, do_ref, dl_ref):
        d = _nt(dz_ref[...].astype(BF16), w_ref[...].reshape(N_DEV * kb, N))
        dy_ref[...] = d[:, :C]
        do_ref[...] = d[:, C:]
        left = lax.broadcasted_iota(jnp.int32, (tm, LANES), 1) < V_HEAD_DIM
        for p in range(C // LANES):
            cols = slice(p * LANES, (p + 1) * LANES)
            prod = d[:, C + p * LANES:C + (p + 1) * LANES] * o_ref[:, cols]
            d0 = jnp.sum(jnp.where(left, prod, 0.0), axis=1, keepdims=True)
            d1 = jnp.sum(jnp.where(left, 0.0, prod), axis=1, keepdims=True)
            dl_ref[:, cols] = jnp.where(left, d0, d1)

    return pl.pallas_call(
        body,
        grid=(L // tm,),
        in_specs=[_row_spec(tm, N), _gathered_spec(w8, layer), _row_spec(tm, C)],
        out_specs=[_row_spec(tm, C)] * 3,
        out_shape=[jax.ShapeDtypeStruct((L, C), F32)] * 3,
        compiler_params=_params("arbitrary"),
        name=name,
    )(dz, w8, o)


def mm_nt_normbwd(terms, x, gamma, dres, layer, name):
    L, K = x.shape
    tm = ROW_TILE
    n_terms = len(terms)
    has_res = dres is not None
    weights = []
    for t in terms:
        if not any(t[2] is u for u in weights):
            weights.append(t[2])
    which = [[t[2] is u for u in weights].index(True) for t in terms]
    n_in = n_terms + len(weights)

    def body(*refs):
        dz_refs = refs[:n_terms]
        w_refs = [refs[n_terms + n] for n in which]
        x_ref, g_ref = refs[n_in], refs[n_in + 1]
        pos = n_in + 2
        r_ref = refs[pos] if has_res else None
        dx_ref, dg_ref = refs[pos + has_res], refs[pos + has_res + 1]
        da = None
        for t, dz_ref, w_ref in zip(terms, dz_refs, w_refs):
            if t[0] == "rows":
                n, at = t[1].shape[1], t[3]
                parts = [_nn(dz_ref[...].astype(BF16), w_ref[at:at + n, :])]
            elif t[0] == "heads":
                hw = t[2].shape[-1]
                parts = [_nt(dz_ref[:, j * HEAD_PAD:j * HEAD_PAD + hw].astype(BF16), w_ref[j])
                         for j in range(MLA_HEADS)]
            else:
                nb = t[2].shape[-2]
                parts = [_nn(dz_ref[...].astype(BF16), w_ref[...].reshape(N_DEV * nb, K))]
            for p in parts:
                da = p if da is None else da + p
        xf = x_ref[...]
        r = _rms(xf)
        xh = xf * r

        @pl.when(pl.program_id(0) == 0)
        def _():
            dg_ref[...] = jnp.zeros_like(dg_ref)

        dg_ref[...] += jnp.sum(da * xh, axis=0, keepdims=True)
        dxh = da * g_ref[...]
        dx = r * (dxh - xh * jnp.mean(dxh * xh, axis=-1, keepdims=True))
        if has_res:
            dx = dx + r_ref[...]
        dx_ref[...] = dx

    in_specs = [_row_spec(tm, t[1].shape[1]) for t in terms]
    for w in weights:
        in_specs.append(_layer_spec(w, layer) if w.ndim == 3 else _gathered_spec(w, layer))
    in_specs += [_row_spec(tm, K), _const_spec((1, K))]
    args = [t[1] for t in terms] + weights + [x, gamma]
    if has_res:
        in_specs.append(_row_spec(tm, K))
        args.append(dres)
    return pl.pallas_call(
        body,
        grid=(L // tm,),
        in_specs=in_specs,
        out_specs=[_row_spec(tm, K), _const_spec((1, K))],
        out_shape=[jax.ShapeDtypeStruct((L, K), F32), jax.ShapeDtypeStruct((1, K), F32)],
        compiler_params=_params("arbitrary"),
        name=name,
    )(*args)


MAX_OUT_ROWS = 1408


def mm_tn(a_list, b, name, out_dtype=F32):
    n = len(a_list)
    L, N = b.shape
    tks = [MAX_OUT_ROWS if (a.shape[1] > MAX_OUT_ROWS and a.shape[1] % MAX_OUT_ROWS == 0) else a.shape[1]
           for a in a_list]
    blocks = a_list[0].shape[1] // tks[0]
    assert all(a.shape[1] // tk == blocks for a, tk in zip(a_list, tks))
    tl = LONG_TILE if L % LONG_TILE == 0 else ROW_TILE
    n_l = L // tl

    def body(*refs):
        a_refs, b_ref = refs[:n], refs[n]
        o_refs, accs = refs[n + 1:2 * n + 1], refs[2 * n + 1:]
        l = pl.program_id(1)
        bt = b_ref[...].astype(BF16)
        for a_ref, o_ref, acc in zip(a_refs, o_refs, accs):
            @pl.when(l == 0)
            def _(acc=acc):
                acc[...] = jnp.zeros_like(acc)

            acc[...] += _tn(a_ref[...].astype(BF16), bt)

            @pl.when(l == n_l - 1)
            def _(acc=acc, o_ref=o_ref):
                o_ref[...] = acc[...].astype(o_ref.dtype)

    return pl.pallas_call(
        body,
        grid=(blocks, n_l),
        in_specs=[pl.BlockSpec((tl, tk), lambda j, l: (l, j)) for tk in tks]
        + [pl.BlockSpec((tl, N), lambda j, l: (l, 0))],
        out_specs=[pl.BlockSpec((tk, N), lambda j, l: (j, 0)) for tk in tks],
        out_shape=[jax.ShapeDtypeStruct((a.shape[1], N), out_dtype) for a in a_list],
        scratch_shapes=[pltpu.VMEM((tk, N), F32) for tk in tks],
        compiler_params=_params("arbitrary", "arbitrary"),
        name=name,
    )(*a_list, b)


def ffn_bwd_act(dh, w_down8, layer, gate, up, name):
    L, K = dh.shape
    fb = w_down8.shape[-2]
    tm, tf = ROW_TILE, FF_GROUP * fb

    def body(dh_ref, w_ref, gate_ref, up_ref, dgate_ref, dup_ref):
        dact = _nt(dh_ref[...].astype(BF16), w_ref[...].reshape(tf, K))
        g = gate_ref[...].astype(F32)
        sig = 0.5 * jnp.tanh(0.5 * g) + 0.5
        dup_ref[...] = (dact * (g * sig)).astype(BF16)
        dgate_ref[...] = (dact * up_ref[...].astype(F32) * (sig * (1.0 + g * (1.0 - sig)))).astype(BF16)

    tile = pl.BlockSpec((tm, tf), lambda j, i: (i, j))
    F = N_DEV * fb
    return pl.pallas_call(
        body,
        grid=(N_DEV // FF_GROUP, L // tm),
        in_specs=[pl.BlockSpec((tm, K), lambda j, i: (i, 0)), _ff_spec(w_down8, layer), tile, tile],
        out_specs=[tile, tile],
        out_shape=[jax.ShapeDtypeStruct((L, F), BF16), jax.ShapeDtypeStruct((L, F), BF16)],
        compiler_params=_params("arbitrary", "arbitrary"),
        name=name,
    )(dh, w_down8, gate, up)


def _pool_residual(scr, lo, tm, g, w, t):
    cols = slice(g * POOL_GROUP_DIM, (g + 1) * POOL_GROUP_DIM)
    cur = scr[lo:lo + tm, cols]
    s = cur
    for k in range(1, w):
        s = s + scr[lo - k:lo - k + tm, cols]
    cnt = jnp.minimum(t + 1, w).astype(F32)
    return s / cnt - cur


def pool_fwd(u, w_pool, scale, name):
    L, C = u.shape
    tm, halo = ROW_TILE, POOL_HALO

    def body(u_ref, halo_ref, w_ref, s_ref, y_ref, scr):
        i = pl.program_id(0)
        scr[0:halo, :] = jnp.where(i > 0, halo_ref[...], 0.0)
        scr[halo:halo + tm, :] = u_ref[...]
        t = i * tm + lax.broadcasted_iota(jnp.int32, (tm, POOL_GROUP_DIM), 0)
        for g, w in enumerate(POOL_WINDOWS):
            cols = slice(g * POOL_GROUP_DIM, (g + 1) * POOL_GROUP_DIM)
            p = _pool_residual(scr, halo, tm, g, w, t)
            y = _nn(p.astype(BF16), w_ref[g]) * s_ref[:, cols]
            y_ref[:, cols] = y.astype(y_ref.dtype)

    return pl.pallas_call(
        body,
        grid=(L // tm,),
        in_specs=[
            _row_spec(tm, C),
            pl.BlockSpec((halo, C), lambda i: (jnp.maximum(i * (tm // halo) - 1, 0), 0)),
            _const_spec(w_pool.shape),
            _const_spec((1, C)),
        ],
        out_specs=_row_spec(tm, C),
        out_shape=jax.ShapeDtypeStruct((L, 2 * C), BF16),
        scratch_shapes=[pltpu.VMEM((tm + halo, C), F32)],
        compiler_params=_params("arbitrary"),
        name=name,
    )(u, u, w_pool, scale)


def pool_bwd(u, dy, w_pool, scale, name):
    L, C = u.shape
    tm, halo = ROW_TILE, POOL_HALO
    n_tiles = L // tm
    last_halo = L // halo - 1

    def body(u_ref, uh_ref, dy_ref, dyh_ref, w_ref, s_ref, du_ref, dw_ref, ds_ref, scr_u, scr_q):
        i = pl.program_id(0)

        @pl.when(i == 0)
        def _():
            dw_ref[...] = jnp.zeros_like(dw_ref)
            ds_ref[...] = jnp.zeros_like(ds_ref)

        scr_u[0:halo, :] = jnp.where(i > 0, uh_ref[...], 0.0)
        scr_u[halo:halo + tm, :] = u_ref[...]
        t = i * tm + lax.broadcasted_iota(jnp.int32, (tm, POOL_GROUP_DIM), 0)
        th = (i + 1) * tm + lax.broadcasted_iota(jnp.int32, (halo, POOL_GROUP_DIM), 0)
        for g, w in enumerate(POOL_WINDOWS):
            cols = slice(g * POOL_GROUP_DIM, (g + 1) * POOL_GROUP_DIM)
            p = _pool_residual(scr_u, halo, tm, g, w, t).astype(BF16)
            wg = w_ref[g]
            sc = s_ref[:, cols]
            dy = dy_ref[:, cols]
            ds_ref[:, cols] += jnp.sum(dy * _nn(p, wg), axis=0, keepdims=True)
            dys = (dy * sc).astype(BF16)
            dw_ref[g] += _tn(p, dys)
            dp = _nt(dys, wg)
            dyh = jnp.where(i < n_tiles - 1, dyh_ref[:, cols], 0.0)
            dph = _nt((dyh * sc).astype(BF16), wg)
            scr_q[0:tm, cols] = dp / jnp.minimum(t + 1, w).astype(F32)
            scr_q[tm:tm + halo, cols] = dph / jnp.minimum(th + 1, w).astype(F32)
            acc = scr_q[0:tm, cols]
            for k in range(1, w):
                acc = acc + scr_q[k:k + tm, cols]
            du_ref[:, cols] = acc - dp

    return pl.pallas_call(
        body,
        grid=(n_tiles,),
        in_specs=[
            _row_spec(tm, C),
            pl.BlockSpec((halo, C), lambda i: (jnp.maximum(i * (tm // halo) - 1, 0), 0)),
            _row_spec(tm, C),
            pl.BlockSpec((halo, C), lambda i: (jnp.minimum((i + 1) * (tm // halo), last_halo), 0)),
            _const_spec(w_pool.shape),
            _const_spec((1, C)),
        ],
        out_specs=[_row_spec(tm, C), _const_spec(w_pool.shape), _const_spec((1, C))],
        out_shape=[
            jax.ShapeDtypeStruct((L, C), F32),
            jax.ShapeDtypeStruct(w_pool.shape, F32),
            jax.ShapeDtypeStruct((1, C), F32),
        ],
        scratch_shapes=[pltpu.VMEM((tm + halo, C), F32), pltpu.VMEM((tm + halo, C), F32)],
        compiler_params=_params("arbitrary"),
        name=name,
    )(u, u, dy, dy, w_pool, scale)


def _head_masks(rows):
    lane = lax.broadcasted_iota(jnp.int32, (rows, HEAD_PAD), 1)
    return lane, lane < QK_NOPE_DIM, (lane >= QK_NOPE_DIM) & (lane < QK_HEAD_DIM)


def _rope_swap(x, lane):
    half = QK_ROPE_DIM // 2
    swapped = jnp.where(lane < QK_NOPE_DIM + half, pltpu.roll(x, HEAD_PAD - half, 1), pltpu.roll(x, half, 1))
    return jnp.where((lane >= QK_NOPE_DIM) & (lane < QK_HEAD_DIM), swapped, 0.0)


def _seg_mean(v, seg_ref):
    hi = v.astype(BF16)
    lo = (v - hi.astype(F32)).astype(BF16)
    seg = seg_ref[...]
    return _nn(hi, seg) + _nn(lo, seg)


def _segment_matrix():
    lane = jnp.arange(HEAD_PAD)
    seg = jnp.where(lane < QK_NOPE_DIM, 0, jnp.where(lane < QK_HEAD_DIM, 1, 2))
    inv = jnp.where(lane < QK_NOPE_DIM, 1.0 / QK_NOPE_DIM, jnp.where(lane < QK_HEAD_DIM, 1.0 / QK_ROPE_DIM, 0.0))
    return jnp.where(seg[:, None] == seg[None, :], inv[None, :], 0.0).astype(BF16)


def qk_prep_fwd(q, kv, kr, cosf, sins, seg, gq, gkn, gkr, name):
    L = q.shape[0]
    tm = ROW_TILE
    W = MLA_HEADS * HEAD_PAD

    def body(q_ref, kv_ref, kr_ref, c_ref, s_ref, seg_ref, gq_ref, gkn_ref, gkr_ref, qo_ref, ko_ref, vo_ref):
        lane, m_n, _ = _head_masks(tm)
        cosf_, sins_ = c_ref[...], s_ref[...]
        kr_ = kr_ref[...]
        rk = lax.rsqrt(_seg_mean(kr_ * kr_, seg_ref) + RMS_EPS)
        krn = kr_ * rk * gkr_ref[...]
        krf = krn * cosf_ + _rope_swap(krn, lane) * sins_
        for h in range(MLA_HEADS):
            cols = slice(h * HEAD_PAD, (h + 1) * HEAD_PAD)
            qh = q_ref[:, cols]
            qn = qh * lax.rsqrt(_seg_mean(qh * qh, seg_ref) + RMS_EPS) * gq_ref[...]
            qo_ref[:, cols] = (qn * cosf_ + _rope_swap(qn, lane) * sins_).astype(BF16)
            kh = jnp.where(m_n, kv_ref[:, cols], 0.0)
            rn = lax.rsqrt(_seg_mean(kh * kh, seg_ref) + RMS_EPS)
            ko_ref[:, cols] = (kh * rn * gkn_ref[...] + krf).astype(BF16)
        for p in range(MLA_HEADS // 2):
            even = kv_ref[:, 2 * p * HEAD_PAD:(2 * p + 1) * HEAD_PAD]
            odd = kv_ref[:, (2 * p + 1) * HEAD_PAD:(2 * p + 2) * HEAD_PAD]
            pair = jnp.where(m_n, pltpu.roll(even, V_HEAD_DIM, 1), odd)
            vo_ref[:, p * LANES:(p + 1) * LANES] = pair.astype(BF16)

    vec = _const_spec((1, HEAD_PAD))
    return pl.pallas_call(
        body,
        grid=(L // tm,),
        in_specs=[_row_spec(tm, W), _row_spec(tm, W), _row_spec(tm, HEAD_PAD), _row_spec(tm, HEAD_PAD),
                  _row_spec(tm, HEAD_PAD), _const_spec((HEAD_PAD, HEAD_PAD)), vec, vec, vec],
        out_specs=[_row_spec(tm, W), _row_spec(tm, W), _row_spec(tm, W // 2)],
        out_shape=[jax.ShapeDtypeStruct((L, W), BF16), jax.ShapeDtypeStruct((L, W), BF16),
                   jax.ShapeDtypeStruct((L, W // 2), BF16)],
        compiler_params=_params("arbitrary"),
        name=name,
    )(q, kv, kr, cosf, sins, seg, gq, gkn, gkr)


def qk_prep_bwd(dqo, dko, dv, q, kv, kr, cosf, sins, seg, gq, gkn, gkr, name):
    L = q.shape[0]
    tm = ROW_TILE
    W = MLA_HEADS * HEAD_PAD

    def body(dqo_ref, dko_ref, dv_ref, q_ref, kv_ref, kr_ref, c_ref, s_ref, seg_ref, gq_ref, gkn_ref, gkr_ref,
             dq_ref, dkv_ref, dkr_ref, dgq_ref, dgkn_ref, dgkr_ref):
        @pl.when(pl.program_id(0) == 0)
        def _():
            dgq_ref[...] = jnp.zeros_like(dgq_ref)
            dgkn_ref[...] = jnp.zeros_like(dgkn_ref)
            dgkr_ref[...] = jnp.zeros_like(dgkr_ref)

        lane, m_n, _ = _head_masks(tm)
        cosf_, sins_ = c_ref[...], s_ref[...]
        dgq = jnp.zeros((1, HEAD_PAD), F32)
        dgkn = jnp.zeros((1, HEAD_PAD), F32)
        dkrf = jnp.zeros((tm, HEAD_PAD), F32)
        for h in range(MLA_HEADS):
            cols = slice(h * HEAD_PAD, (h + 1) * HEAD_PAD)
            dy = dqo_ref[:, cols]
            dqn = dy * cosf_ + _rope_swap(dy * sins_, lane)
            qh = q_ref[:, cols]
            rinv = lax.rsqrt(_seg_mean(qh * qh, seg_ref) + RMS_EPS)
            xh = qh * rinv
            dgq = dgq + jnp.sum(dqn * xh, axis=0, keepdims=True)
            dxh = dqn * gq_ref[...]
            dq_ref[:, cols] = rinv * (dxh - xh * _seg_mean(dxh * xh, seg_ref))
            dk = dko_ref[:, cols]
            dkrf = dkrf + dk
            kh = jnp.where(m_n, kv_ref[:, cols], 0.0)
            rn = lax.rsqrt(_seg_mean(kh * kh, seg_ref) + RMS_EPS)
            xk = kh * rn
            dgkn = dgkn + jnp.sum(dk * xk, axis=0, keepdims=True)
            dxk = dk * gkn_ref[...]
            dkn = rn * (dxk - xk * _seg_mean(dxk * xk, seg_ref))
            dvp = dv_ref[:, (h // 2) * LANES:(h // 2 + 1) * LANES]
            dvh = pltpu.roll(dvp, V_HEAD_DIM, 1) if h % 2 == 0 else dvp
            dkv_ref[:, cols] = jnp.where(m_n, dkn, dvh)
        kr_ = kr_ref[...]
        rk = lax.rsqrt(_seg_mean(kr_ * kr_, seg_ref) + RMS_EPS)
        xr = kr_ * rk
        dkrn = dkrf * cosf_ + _rope_swap(dkrf * sins_, lane)
        dgkr_ref[...] += jnp.sum(dkrn * xr, axis=0, keepdims=True)
        dxr = dkrn * gkr_ref[...]
        dkr_ref[...] = rk * (dxr - xr * _seg_mean(dxr * xr, seg_ref))
        dgq_ref[...] += dgq
        dgkn_ref[...] += dgkn

    vec = _const_spec((1, HEAD_PAD))
    return pl.pallas_call(
        body,
        grid=(L // tm,),
        in_specs=[_row_spec(tm, W), _row_spec(tm, W), _row_spec(tm, W // 2), _row_spec(tm, W), _row_spec(tm, W),
                  _row_spec(tm, HEAD_PAD), _row_spec(tm, HEAD_PAD), _row_spec(tm, HEAD_PAD),
                  _const_spec((HEAD_PAD, HEAD_PAD)), vec, vec, vec],
        out_specs=[_row_spec(tm, W), _row_spec(tm, W), _row_spec(tm, HEAD_PAD), vec, vec, vec],
        out_shape=[jax.ShapeDtypeStruct((L, W), F32), jax.ShapeDtypeStruct((L, W), F32),
                   jax.ShapeDtypeStruct((L, HEAD_PAD), F32)] + [jax.ShapeDtypeStruct((1, HEAD_PAD), F32)] * 3,
        compiler_params=_params("arbitrary"),
        name=name,
    )(dqo, dko, dv, q, kv, kr, cosf, sins, seg, gq, gkn, gkr)


def attn_fwd(qp, kp, v, cat, name, gather=()):
    L = qp.shape[0]
    tq = ROW_TILE
    n_i = L // tq
    pair_w = 2 * HEAD_PAD
    n_pairs = MLA_HEADS // 2
    n_g = len(gather)

    def body(q_ref, k_ref, v_ref, cat_in, *rest):
        del cat_in
        o_ref, lse_ref, cat_ref = rest[n_g:n_g + 3]
        i = pl.program_id(1)
        if n_g:
            pair = pl.program_id(0)
            start, forward, finish = _gather_steps(rest[:n_g], rest[n_g + 3:2 * n_g + 3], *rest[2 * n_g + 3:])
            pl.when((pair == 0) & (i == 0))(start)
            pl.when((pair == n_pairs // 2) & (i == 0))(forward)
        left = lax.broadcasted_iota(jnp.int32, (tq, LANES), 1) < V_HEAD_DIM
        row = lax.broadcasted_iota(jnp.int32, (tq, tq), 0)
        col = lax.broadcasted_iota(jnp.int32, (tq, tq), 1)

        def step(j, carry, masked, width=1):
            rows = pl.ds(pl.multiple_of(j * (tq * width), tq), tq * width)
            vv = v_ref[rows, :]
            out = []
            for hh in range(2):
                cols = slice(hh * HEAD_PAD, (hh + 1) * HEAD_PAD)
                m, l, acc = carry[hh]
                s = _nt(q_ref[:, cols], k_ref[rows, cols]) * SCORE_SCALE
                if masked:
                    s = jnp.where(col <= row, s, -jnp.inf)
                m_new = jnp.maximum(m, jnp.max(s, axis=1, keepdims=True))
                alpha = jnp.exp2(m - m_new)
                p = jnp.exp2(s - m_new)
                l = alpha * l + jnp.sum(p, axis=1, keepdims=True)
                acc = alpha * acc + _nn(p.astype(BF16), vv)
                out.append((m_new, l, acc))
            return tuple(out)

        one = (jnp.full((tq, 1), -jnp.inf, F32), jnp.zeros((tq, 1), F32), jnp.zeros((tq, LANES), F32))
        doubles = jnp.right_shift(i, 1)
        carry = lax.fori_loop(0, doubles, functools.partial(step, masked=False, width=2), (one, one))
        carry = lax.fori_loop(2 * doubles, i, functools.partial(step, masked=False), carry)
        (m0, l0, a0), (m1, l1, a1) = step(i, carry, True)
        o = jnp.where(left, a0 / l0, a1 / l1)
        o_ref[...] = o
        cat_ref[...] = o.astype(BF16)
        lse_ref[...] = jnp.where(left, m0 + jnp.log2(l0), m1 + jnp.log2(l1))
        if n_g:
            pl.when((pair == n_pairs - 1) & (i == n_i - 1))(finish)

    return pl.pallas_call(
        body,
        grid=(n_pairs, n_i),
        in_specs=[
            pl.BlockSpec((tq, pair_w), lambda p, i: (i, p)),
            pl.BlockSpec((L, pair_w), lambda p, i: (0, p)),
            pl.BlockSpec((L, LANES), lambda p, i: (0, p)),
            _ANY,
        ] + [_ANY] * n_g,
        out_specs=[
            pl.BlockSpec((tq, LANES), lambda p, i: (i, p)),
            pl.BlockSpec((tq, LANES), lambda p, i: (i, p)),
            pl.BlockSpec((tq, LANES), lambda p, i: (i, n_pairs + p)),
        ] + [_ANY] * n_g,
        out_shape=[jax.ShapeDtypeStruct((L, n_pairs * LANES), F32), jax.ShapeDtypeStruct((L, n_pairs * LANES), F32),
                   jax.ShapeDtypeStruct(cat.shape, cat.dtype)]
        + [jax.ShapeDtypeStruct((N_DEV,) + g.shape, g.dtype) for g in gather],
        scratch_shapes=_gather_scratch(n_g) if n_g else [],
        input_output_aliases={3: 2},
        compiler_params=_params("arbitrary", "arbitrary"),
        name=name,
    )(qp, kp, v, cat, *gather)


def attn_bwd(qp, kp, v, do, lse, delta, name, exchange=(), gather=()):
    L = qp.shape[0]
    tq = ROW_TILE
    n_q = L // tq
    pair_w = 2 * HEAD_PAD
    n_pairs = MLA_HEADS // 2
    n_x, n_g = len(exchange), len(gather)
    n_r = n_x + n_g

    def body(q_ref, k_ref, v_ref, do_ref, lse_ref, dl_ref, *rest):
        dq_ref, dk_ref, dv_ref = rest[n_r:n_r + 3]
        riders_in, riders_out, sems = rest[:n_r], rest[n_r + 3:2 * n_r + 3], rest[2 * n_r + 3:]
        j = pl.program_id(1)
        pair = pl.program_id(0)
        steps = []
        if n_x:
            steps.append(_chip_exchange_steps(riders_in[:n_x], riders_out[:n_x], *sems[:3]))
        if n_g:
            steps.append(_direct_gather_steps(riders_in[n_x:], riders_out[n_x:], *sems[-3:]))
        for start, _ in steps:
            pl.when((pair == 0) & (j == 0))(start)

        @pl.when(j == 0)
        def _():
            dq_ref[...] = jnp.zeros_like(dq_ref)

        dk_ref[...] = jnp.zeros_like(dk_ref)
        dv_ref[...] = jnp.zeros_like(dv_ref)
        row = lax.broadcasted_iota(jnp.int32, (tq, tq), 0)
        col = lax.broadcasted_iota(jnp.int32, (tq, tq), 1)

        def step(t, carry, masked, width=1, first=0):
            rows = pl.ds(pl.multiple_of((first + t * width) * tq, tq), tq * width)
            do = do_ref[rows, :]
            left = lax.broadcasted_iota(jnp.int32, do.shape, 1) < V_HEAD_DIM
            vv = v_ref[...]
            dv = None
            for hh in range(2):
                cols = slice(hh * HEAD_PAD, (hh + 1) * HEAD_PAD)
                stat = slice(hh * V_HEAD_DIM, hh * V_HEAD_DIM + 1)
                q = q_ref[rows, cols]
                k = k_ref[:, cols]
                dom = jnp.where(left if hh == 0 else jnp.logical_not(left), do, 0.0).astype(BF16)
                s = _nt(q, k) * SCORE_SCALE
                p = jnp.exp2(s - lse_ref[rows, stat])
                if masked:
                    p = jnp.where(col <= row, p, 0.0)
                dp = _nt(dom, vv)
                ds = (p * (dp - dl_ref[rows, stat]) * ATTN_SCALE).astype(BF16)
                dq_ref[rows, cols] += _nn(ds, k)
                dk_ref[:, cols] += _tn(ds, q)
                t = _tn(p.astype(BF16), dom)
                dv = t if dv is None else dv + t
            dv_ref[...] += dv
            return carry

        step(j, 0, True)
        doubles = jnp.right_shift(n_q - 1 - j, 1)
        lax.fori_loop(0, doubles, functools.partial(step, masked=False, width=2, first=j + 1), 0)
        lax.fori_loop(j + 1 + 2 * doubles, n_q, functools.partial(step, masked=False), 0)
        for _, finish in steps:
            pl.when((pair == n_pairs - 1) & (j == n_q - 1))(finish)

    return pl.pallas_call(
        body,
        grid=(n_pairs, n_q),
        in_specs=[
            pl.BlockSpec((L, pair_w), lambda p, j: (0, p)),
            pl.BlockSpec((tq, pair_w), lambda p, j: (j, p)),
            pl.BlockSpec((tq, LANES), lambda p, j: (j, p)),
            pl.BlockSpec((L, LANES), lambda p, j: (0, p)),
            pl.BlockSpec((L, LANES), lambda p, j: (0, p)),
            pl.BlockSpec((L, LANES), lambda p, j: (0, p)),
        ] + [_ANY] * n_r,
        out_specs=[
            pl.BlockSpec((L, pair_w), lambda p, j: (0, p)),
            pl.BlockSpec((tq, pair_w), lambda p, j: (j, p)),
            pl.BlockSpec((tq, LANES), lambda p, j: (j, p)),
        ] + [_ANY] * n_r,
        out_shape=[
            jax.ShapeDtypeStruct((L, n_pairs * pair_w), F32),
            jax.ShapeDtypeStruct((L, n_pairs * pair_w), F32),
            jax.ShapeDtypeStruct((L, n_pairs * LANES), F32),
        ] + [jax.ShapeDtypeStruct(e.shape, e.dtype) for e in exchange]
        + [jax.ShapeDtypeStruct((N_DEV,) + g.shape, g.dtype) for g in gather],
        scratch_shapes=(_chip_exchange_scratch(n_x) if n_x else []) + (_direct_gather_scratch(n_g) if n_g else []),
        compiler_params=_params("arbitrary", "arbitrary"),
        name=name,
    )(qp, kp, v, do, lse, delta, *exchange, *gather)


def loss_head(h, target, n_real, name):
    L, D = h.shape
    tm = ROW_TILE

    def body(h_ref, t_ref, dh_ref, sq_ref):
        i = pl.program_id(0)

        @pl.when(i == 0)
        def _():
            sq_ref[...] = jnp.zeros_like(sq_ref)

        t = i * tm + lax.broadcasted_iota(jnp.int32, (tm, D), 0)
        real = (t >= N_META) & (t < N_META + n_real)
        diff = jnp.where(real, h_ref[...] - t_ref[...], 0.0)
        dh_ref[...] = diff * (1.0 / D)
        sq_ref[...] += jnp.sum(diff * diff, axis=0, keepdims=True)

    return pl.pallas_call(
        body,
        grid=(L // tm,),
        in_specs=[_row_spec(tm, D), _row_spec(tm, D)],
        out_specs=[_row_spec(tm, D), _const_spec((1, D))],
        out_shape=[jax.ShapeDtypeStruct((L, D), F32), jax.ShapeDtypeStruct((1, D), F32)],
        compiler_params=_params("arbitrary"),
        name=name,
    )(h, target)


def _mesh_position():
    x, y, c = lax.axis_index("x"), lax.axis_index("y"), lax.axis_index("c")
    return x, y, c, 4 * x + 2 * y + c


def _flip(x, y, c, k):
    px = 1 - x if k & 4 else x
    py = 1 - y if k & 2 else y
    pc = 1 - c if k & 1 else c
    return (px, py, pc), 4 * px + 2 * py + pc


def _other_chips(x, y):
    return [(1 - x, y), (x, 1 - y), (1 - x, 1 - y)]


def _dev_index(px, py, pc):
    return 4 * px + 2 * py + pc


_ANY = pl.BlockSpec(memory_space=pl.ANY)


def cast_bf16(arrays, name):
    n = len(arrays)
    depth = arrays[0].shape[0]

    def body(*refs):
        for k, src in enumerate(refs[:n]):
            for l in range(depth):
                refs[n + k * depth + l][0] = src[l].astype(BF16)

    return pl.pallas_call(
        body,
        out_shape=[jax.ShapeDtypeStruct((1,) + a.shape[1:], BF16) for a in arrays for _ in range(depth)],
        compiler_params=pltpu.CompilerParams(vmem_limit_bytes=VMEM_LIMIT),
        name=name,
    )(*arrays)


def _gather_steps(srcs, dsts, send_sems, recv_sems, local_sems):
    n = len(srcs)
    x, y, c, me = _mesh_position()
    sibling = (x, y, 1 - c)
    chips = _other_chips(x, y)

    def copy(k, b, block, to, from_src=False):
        dst = dsts[b].at[block]
        return pltpu.make_async_remote_copy(
            src_ref=srcs[b] if from_src else dst, dst_ref=dst, send_sem=send_sems.at[k * n + b],
            recv_sem=recv_sems.at[k * n + b], device_id=to, device_id_type=pl.DeviceIdType.MESH)

    def own():
        return [pltpu.make_async_copy(srcs[b], dsts[b].at[me], local_sems.at[b]) for b in range(n)]

    def first():
        out = []
        for b in range(n):
            out.append(copy(0, b, me, sibling, from_src=True))
            out += [copy(1 + q, b, me, (*chip, c), from_src=True) for q, chip in enumerate(chips)]
        return out

    def passed():
        return [copy(4 + q, b, _dev_index(*chip, c), sibling) for q, chip in enumerate(chips) for b in range(n)]

    def start():
        for cp in own() + first():
            cp.start()

    def forward():
        for q, chip in enumerate(chips):
            for b in range(n):
                copy(1 + q, b, _dev_index(*chip, c), sibling).wait_recv()
        for cp in passed():
            cp.start()

    def finish():
        for b in range(n):
            copy(0, b, _dev_index(x, y, 1 - c), sibling).wait_recv()
            for q, chip in enumerate(chips):
                copy(4 + q, b, _dev_index(*chip, 1 - c), sibling).wait_recv()
        for cp in first() + passed():
            cp.wait_send()
        for cp in own():
            cp.wait()

    return start, forward, finish


def _gather_scratch(n):
    copies = N_DEV - 1
    return [pltpu.SemaphoreType.DMA((copies * n,)), pltpu.SemaphoreType.DMA((copies * n,)),
            pltpu.SemaphoreType.DMA((n,))]


def all_gather(payloads, name):
    n = len(payloads)

    def body(*refs):
        start, forward, finish = _gather_steps(refs[:n], refs[n:2 * n], *refs[2 * n:])
        start()
        forward()
        finish()

    return pl.pallas_call(
        body,
        in_specs=[_ANY] * n,
        out_specs=[_ANY] * n,
        out_shape=[jax.ShapeDtypeStruct((N_DEV,) + p.shape, p.dtype) for p in payloads],
        scratch_shapes=_gather_scratch(n),
        name=name,
    )(*payloads)


def exchange_pair(bigs, small, name):
    n = len(bigs)
    has_small = small is not None
    n_big = N_CHIPS * n
    n_sems = n_big + (N_DEV - 1 if has_small else 0)

    def body(*refs):
        big_refs = refs[:n]
        pos = n + has_small
        sib_refs = refs[pos:pos + n]
        send_sems, recv_sems, local_sem = refs[-3:]
        x, y, c, me = _mesh_position()
        sibling = (x, y, 1 - c)
        copies = []
        for b in range(n):
            for q in range(N_CHIPS):
                copies.append(pltpu.make_async_remote_copy(
                    src_ref=big_refs[b].at[_dev_index(q // 2, q % 2, 1 - c)], dst_ref=sib_refs[b].at[q],
                    send_sem=send_sems.at[b * N_CHIPS + q], recv_sem=recv_sems.at[b * N_CHIPS + q],
                    device_id=sibling, device_id_type=pl.DeviceIdType.MESH))
        waits = list(copies)
        if has_small:
            small_ref, gsmall_ref = refs[n], refs[pos + n]
            own = pltpu.make_async_copy(small_ref.at[me], gsmall_ref.at[me], local_sem.at[0])
            own.start()
            for k in range(1, N_DEV):
                peer, peer_idx = _flip(x, y, c, k)
                s = n_big + k - 1
                copies.append(pltpu.make_async_remote_copy(
                    src_ref=small_ref.at[peer_idx], dst_ref=gsmall_ref.at[me], send_sem=send_sems.at[s],
                    recv_sem=recv_sems.at[s], device_id=peer, device_id_type=pl.DeviceIdType.MESH))
                waits.append(pltpu.make_async_remote_copy(
                    src_ref=small_ref.at[peer_idx], dst_ref=gsmall_ref.at[peer_idx], send_sem=send_sems.at[s],
                    recv_sem=recv_sems.at[s], device_id=peer, device_id_type=pl.DeviceIdType.MESH))
        for cp in copies:
            cp.start()
        for cp in waits:
            cp.wait_recv()
        for cp in copies:
            cp.wait_send()
        if has_small:
            own.wait()

    out_shape = [jax.ShapeDtypeStruct((N_CHIPS,) + b.shape[1:], b.dtype) for b in bigs]
    args = list(bigs)
    if has_small:
        out_shape.append(jax.ShapeDtypeStruct(small.shape, small.dtype))
        args.append(small)
    return pl.pallas_call(
        body,
        in_specs=[_ANY] * len(args),
        out_specs=[_ANY] * len(out_shape),
        out_shape=out_shape,
        scratch_shapes=[pltpu.SemaphoreType.DMA((n_sems,)), pltpu.SemaphoreType.DMA((n_sems,)),
                        pltpu.SemaphoreType.DMA((1,))],
        name=name,
    )(*args)


def pair_sum(bigs, from_sibling, core, name):
    n = len(bigs)

    def body(core_ref, *refs):
        del core_ref
        for mine, sib, out in zip(refs[:n], refs[n:2 * n], refs[2 * n:]):
            out[...] = (mine[...].astype(F32) + sib[...].astype(F32)).astype(out.dtype)

    def slot(shape, picked):
        zeros = (0,) * (len(shape) - 1)
        if picked:
            return pl.BlockSpec((None,) + shape[1:], lambda q, core_ref: (2 * q + core_ref[0],) + zeros)
        return pl.BlockSpec((None,) + shape[1:], lambda q, core_ref: (q,) + zeros)

    grid_spec = pltpu.PrefetchScalarGridSpec(
        num_scalar_prefetch=1,
        grid=(N_CHIPS,),
        in_specs=[slot(b.shape, True) for b in bigs] + [slot(s.shape, False) for s in from_sibling],
        out_specs=[slot(s.shape, False) for s in from_sibling],
    )
    return pl.pallas_call(
        body,
        grid_spec=grid_spec,
        out_shape=[jax.ShapeDtypeStruct(s.shape, s.dtype) for s in from_sibling],
        compiler_params=_params("arbitrary"),
        name=name,
    )(core, *bigs, *from_sibling)


def _chip_exchange_steps(part_refs, got_refs, send_sems, recv_sems, local_sems):
    n = len(part_refs)
    x, y, c, me = _mesh_position()
    mine = 2 * x + y

    def copies(landing):
        out = []
        for q, (px, py) in enumerate(_other_chips(x, y)):
            theirs = 2 * px + py
            for b in range(n):
                out.append(pltpu.make_async_remote_copy(
                    src_ref=part_refs[b].at[theirs], dst_ref=got_refs[b].at[theirs if landing else mine],
                    send_sem=send_sems.at[q * n + b], recv_sem=recv_sems.at[q * n + b],
                    device_id=(px, py, c), device_id_type=pl.DeviceIdType.MESH))
        return out

    def own():
        return [pltpu.make_async_copy(part_refs[b].at[mine], got_refs[b].at[mine], local_sems.at[b])
                for b in range(n)]

    def start():
        for cp in own() + copies(False):
            cp.start()

    def finish():
        for cp in copies(True):
            cp.wait_recv()
        for cp in copies(False):
            cp.wait_send()
        for cp in own():
            cp.wait()

    return start, finish


def _chip_exchange_scratch(n):
    return [pltpu.SemaphoreType.DMA(((N_CHIPS - 1) * n,)), pltpu.SemaphoreType.DMA(((N_CHIPS - 1) * n,)),
            pltpu.SemaphoreType.DMA((n,))]


def _direct_gather_steps(srcs, dsts, send_sems, recv_sems, local_sems):
    n = len(srcs)
    x, y, c, me = _mesh_position()

    def copies(landing):
        out = []
        for k in range(1, N_DEV):
            peer, peer_idx = _flip(x, y, c, k)
            for b in range(n):
                out.append(pltpu.make_async_remote_copy(
                    src_ref=srcs[b], dst_ref=dsts[b].at[peer_idx if landing else me],
                    send_sem=send_sems.at[(k - 1) * n + b], recv_sem=recv_sems.at[(k - 1) * n + b],
                    device_id=peer, device_id_type=pl.DeviceIdType.MESH))
        return out

    def own():
        return [pltpu.make_async_copy(srcs[b], dsts[b].at[me], local_sems.at[b]) for b in range(n)]

    def start():
        for cp in own() + copies(False):
            cp.start()

    def finish():
        for cp in copies(True):
            cp.wait_recv()
        for cp in copies(False):
            cp.wait_send()
        for cp in own():
            cp.wait()

    return start, finish


def _direct_gather_scratch(n):
    return [pltpu.SemaphoreType.DMA(((N_DEV - 1) * n,)), pltpu.SemaphoreType.DMA(((N_DEV - 1) * n,)),
            pltpu.SemaphoreType.DMA((n,))]


def exchange_chips(partials, name):
    n = len(partials)

    def body(*refs):
        start, finish = _chip_exchange_steps(refs[:n], refs[n:2 * n], *refs[2 * n:])
        start()
        finish()

    return pl.pallas_call(
        body,
        in_specs=[_ANY] * n,
        out_specs=[_ANY] * n,
        out_shape=[jax.ShapeDtypeStruct(p.shape, p.dtype) for p in partials],
        scratch_shapes=_chip_exchange_scratch(n),
        name=name,
    )(*partials)


def _adamw_math(g, w, m, v):
    m_new = ADAM_B1 * m + (1.0 - ADAM_B1) * g
    v_new = ADAM_B2 * v + (1.0 - ADAM_B2) * (g * g)
    m_hat = m_new / (1.0 - ADAM_B1 ** ADAM_STEP)
    v_hat = v_new / (1.0 - ADAM_B2 ** ADAM_STEP)
    delta = -ADAM_LR * (m_hat / (jnp.sqrt(v_hat) + ADAM_EPS) + ADAM_WD * w)
    return delta, m_new, v_new


def _chip_total(p_ref):
    g = p_ref[0].astype(F32)
    for q in range(1, N_CHIPS):
        g = g + p_ref[q].astype(F32)
    return g


def chip_sum(parts, name):
    depth = len(parts)

    def body(*refs):
        for l in range(depth):
            refs[depth][l] = _chip_total(refs[l])

    return pl.pallas_call(
        body,
        out_shape=jax.ShapeDtypeStruct((depth,) + parts[0].shape[1:], F32),
        compiler_params=pltpu.CompilerParams(vmem_limit_bytes=VMEM_LIMIT),
        name=name,
    )(*parts)


def adamw_shard(parts, w, m, v, name, grad=None):
    depth = w.shape[0]
    n_in = 1 if grad is not None else depth

    def body(*refs):
        w_ref, m_ref, v_ref, g_ref, d_ref, mo_ref, vo_ref = refs[n_in:]
        for l in range(depth):
            g = refs[0][l] if grad is not None else _chip_total(refs[l])
            delta, m_new, v_new = _adamw_math(g, w_ref[l], m_ref[l], v_ref[l])
            g_ref[l] = g
            d_ref[l] = delta
            mo_ref[l] = m_new
            vo_ref[l] = v_new

    return pl.pallas_call(
        body,
        out_shape=[jax.ShapeDtypeStruct(w.shape, F32)] * 4,
        compiler_params=pltpu.CompilerParams(vmem_limit_bytes=VMEM_LIMIT),
        name=name,
    )(*([grad] if grad is not None else parts), w, m, v)


def adamw_packed(parts, w, m, v, name):
    n_slots = parts.shape[0]

    def body(p_ref, w_ref, m_ref, v_ref, g_ref, d_ref, mo_ref, vo_ref):
        g = p_ref[0]
        for s in range(1, n_slots):
            g = g + p_ref[s]
        delta, m_new, v_new = _adamw_math(g, w_ref[...], m_ref[...], v_ref[...])
        g_ref[...] = g
        d_ref[...] = delta
        mo_ref[...] = m_new
        vo_ref[...] = v_new

    return pl.pallas_call(
        body,
        out_shape=[jax.ShapeDtypeStruct(w.shape, F32)] * 4,
        compiler_params=pltpu.CompilerParams(vmem_limit_bytes=VMEM_LIMIT),
        name=name,
    )(parts, w, m, v)


SHARDED = ("w_in", "w_q_b", "w_kv_b", "w_out", "w_gate", "w_up", "w_down")
TRANSPOSED = ("w_in", "w_gate", "w_up")
BEFORE_ATTENTION = ("w_in", "w_q_b", "w_kv_b")
FFN_WEIGHTS = ("w_gate", "w_up", "w_down")
REPLICATED = ("attn_norm_g", "pool_scale", "q_a_norm_g", "kv_a_norm_g", "q_norm_g", "k_norm_g", "ffn_norm_g")


def _pack_flat(arrays, rows):
    flat = jnp.concatenate([a.reshape(-1) for a in arrays])
    return jnp.pad(flat, (0, rows * LANES - flat.shape[0])).reshape(rows, LANES)


def _unpack_flat(packed, shapes):
    flat, out, at = packed.reshape(-1), [], 0
    for shp in shapes:
        n = 1
        for d in shp:
            n *= d
        out.append(flat[at:at + n].reshape(shp))
        at += n
    return out


def _rope_lane_tables(length):
    inv = 1.0 / (ROPE_THETA ** (jnp.arange(0, QK_ROPE_DIM, 2, dtype=F32) / QK_ROPE_DIM))
    ang = jnp.arange(length, dtype=F32)[:, None] * inv[None, :]
    cos, sin = jnp.cos(ang), jnp.sin(ang)
    ones = jnp.ones((length, QK_NOPE_DIM), F32)
    zeros = jnp.zeros((length, QK_NOPE_DIM), F32)
    tail = HEAD_PAD - QK_HEAD_DIM
    cosf = jnp.concatenate([ones, cos, cos, ones[:, :tail]], axis=1)
    sins = jnp.concatenate([zeros, -sin, sin, zeros[:, :tail]], axis=1)
    return cosf, sins


def _pad_lanes(vec, at, width=HEAD_PAD):
    return jnp.pad(vec, (at, width - at - vec.shape[0])).reshape(1, width)


def kernel(x, meta_tokens, attn_norm_g, w_in, w_pool, pool_scale, q_a_norm_g, w_q_b, kv_a_norm_g, w_kv_b, q_norm_g, k_norm_g, w_out, ffn_norm_g, w_gate, w_up, w_down, loss_target, m_meta_tokens, m_attn_norm_g, m_w_in, m_w_pool, m_pool_scale, m_q_a_norm_g, m_w_q_b, m_kv_a_norm_g, m_w_kv_b, m_q_norm_g, m_k_norm_g, m_w_out, m_ffn_norm_g, m_w_gate, m_w_up, m_w_down, v_meta_tokens, v_attn_norm_g, v_w_in, v_w_pool, v_pool_scale, v_q_a_norm_g, v_w_q_b, v_kv_a_norm_g, v_w_kv_b, v_q_norm_g, v_k_norm_g, v_w_out, v_ffn_norm_g, v_w_gate, v_w_up, v_w_down):
    weights = dict(meta_tokens=meta_tokens, attn_norm_g=attn_norm_g, w_in=w_in, w_pool=w_pool, pool_scale=pool_scale,
                   q_a_norm_g=q_a_norm_g, w_q_b=w_q_b, kv_a_norm_g=kv_a_norm_g, w_kv_b=w_kv_b, q_norm_g=q_norm_g,
                   k_norm_g=k_norm_g, w_out=w_out, ffn_norm_g=ffn_norm_g, w_gate=w_gate, w_up=w_up, w_down=w_down)
    mom1 = dict(meta_tokens=m_meta_tokens, attn_norm_g=m_attn_norm_g, w_in=m_w_in, w_pool=m_w_pool,
                pool_scale=m_pool_scale, q_a_norm_g=m_q_a_norm_g, w_q_b=m_w_q_b, kv_a_norm_g=m_kv_a_norm_g,
                w_kv_b=m_w_kv_b, q_norm_g=m_q_norm_g, k_norm_g=m_k_norm_g, w_out=m_w_out, ffn_norm_g=m_ffn_norm_g,
                w_gate=m_w_gate, w_up=m_w_up, w_down=m_w_down)
    mom2 = dict(meta_tokens=v_meta_tokens, attn_norm_g=v_attn_norm_g, w_in=v_w_in, w_pool=v_w_pool,
                pool_scale=v_pool_scale, q_a_norm_g=v_q_a_norm_g, w_q_b=v_w_q_b, kv_a_norm_g=v_kv_a_norm_g,
                w_kv_b=v_w_kv_b, q_norm_g=v_q_norm_g, k_norm_g=v_k_norm_g, w_out=v_w_out, ffn_norm_g=v_ffn_norm_g,
                w_gate=v_w_gate, w_up=v_w_up, w_down=v_w_down)
    order = ("meta_tokens", "attn_norm_g", "w_in", "w_pool", "pool_scale", "q_a_norm_g", "w_q_b", "kv_a_norm_g",
             "w_kv_b", "q_norm_g", "k_norm_g", "w_out", "ffn_norm_g", "w_gate", "w_up", "w_down")
    depth = w_in.shape[0]
    seq = x.shape[1]
    length = N_META + seq
    lp = -(-length // ROW_TILE) * ROW_TILE
    in_cols = w_in.shape[2]

    local = {n: (jnp.swapaxes(weights[n], 1, 2) if n in TRANSPOSED else weights[n]) for n in SHARDED}
    cast = cast_bf16([local[n] for n in SHARDED], "cast_weights")
    shards = [{n: cast[k * depth + l] for k, n in enumerate(SHARDED)} for l in range(depth)]
    gathered = all_gather([shards[0][n] for n in BEFORE_ATTENTION] + [meta_tokens], "all_gather")
    g8l = [dict(zip(BEFORE_ATTENTION, gathered[:-1]))] + [{} for _ in range(depth - 1)]
    meta_full = jnp.transpose(gathered[-1], (1, 0, 2)).reshape(N_META, D_MODEL)

    s1, s2, s3 = POOL_WIDTH, POOL_WIDTH + Q_LORA_RANK, POOL_WIDTH + Q_LORA_RANK + KV_LORA_RANK
    zpad = lambda n: jnp.zeros((1, n, D_MODEL), BF16)

    def padded_in_proj(w8):
        w_in_t = jnp.swapaxes(w8, 0, 1).reshape(1, N_DEV * in_cols, D_MODEL)
        return jnp.concatenate([w_in_t[:, :s3], zpad(QK_NOPE_DIM), w_in_t[:, s3:],
                                zpad(HEAD_PAD - QK_HEAD_DIM)], axis=1)

    w_in_ps = [None] * depth
    w_pool_b = w_pool.astype(BF16)

    cosf, sins = _rope_lane_tables(lp)
    seg = _segment_matrix()
    row = lambda a, l: a[l].reshape(1, -1)

    h = jnp.concatenate([meta_full, x[0], jnp.zeros((lp - length, D_MODEL), F32)], axis=0)
    target = jnp.pad(loss_target[0], ((N_META, lp - length), (0, 0)))
    saved = []
    for l in range(depth):
        gq = _pad_lanes(q_norm_g[l], 0)
        gkn = _pad_lanes(k_norm_g[l, :QK_NOPE_DIM], 0)
        gkr = _pad_lanes(k_norm_g[l, QK_NOPE_DIM:], QK_NOPE_DIM)
        g8 = g8l[l]
        w_in_ps[l] = padded_in_proj(g8["w_in"])
        a, u, c_q, c_kv, kr = norm_mm(
            h, row(attn_norm_g, l), w_in_ps[l], 0,
            [(0, s1), (s1, Q_LORA_RANK), (s2, KV_LORA_RANK), (s3, HEAD_PAD)], [F32] * 4, "in_proj")
        cat = pool_fwd(u, w_pool_b[l], row(pool_scale, l), "pool_fwd")
        qn, q = norm_mm_heads(c_q, row(q_a_norm_g, l), g8["w_q_b"], 0, "q_proj")
        kvn, kv = norm_mm_heads(c_kv, row(kv_a_norm_g, l), g8["w_kv_b"], 0, "kv_proj")
        qp, kp, v = qk_prep_fwd(q, kv, kr, cosf, sins, seg, gq, gkn, gkr, "qk_prep_fwd")
        riders = [(l, n) for n in SHARDED if n not in g8]
        riders += [(l + 1, n) for n in BEFORE_ATTENTION if l + 1 < depth]
        o, lse, cat, *arrived = attn_fwd(qp, kp, v, cat, "attn_fwd_gather", gather=[shards[k][n] for k, n in riders])
        for (k, n), w8 in zip(riders, arrived):
            g8l[k][n] = w8
        h_mid, g = mm_res(cat, g8["w_out"], h, 0, "out_proj", gamma=row(ffn_norm_g, l))
        gate, up, act = ffn_up(g, g8["w_gate"], g8["w_up"], 0, "ffn_up")
        h_next = mm_res(act, g8["w_down"], h_mid, 0, "ffn_down")[0]
        saved.append(dict(h=h, a=a, u=u, c_q=c_q, c_kv=c_kv, kr=kr, qn=qn, q=q, kvn=kvn, kv=kv, v=v, qp=qp, kp=kp,
                          o=o, lse=lse, cat=cat, h_mid=h_mid, g=g, gate=gate, up=up, act=act,
                          gq=gq, gkn=gkn, gkr=gkr))
        h = h_next

    dh, sq = loss_head(h, target, seq, "loss_head")
    loss = lax.psum(0.5 / D_MODEL * jnp.sum(sq), ("x", "y", "c"))

    core = lax.axis_index("c").astype(jnp.int32).reshape(1)
    small_grads = {n: [None] * depth for n in REPLICATED + ("w_pool",)}
    pending = []
    got = [{} for _ in range(depth)]

    def pair_reduce(l, names, slots, small):
        bigs = [slots[n] for n in names]
        if small is None:
            from_sibling, got_small = exchange_pair(bigs, None, "grad_exchange_pair"), None
        else:
            *from_sibling, got_small = exchange_pair(bigs, small, "grad_exchange_pair_small")
        partial = pair_sum(bigs, list(from_sibling), core, "grad_pair_sum")
        return [(l, n, p) for n, p in zip(names, partial)], got_small

    for l in reversed(range(depth)):
        s = saved[l]
        g8 = g8l[l]
        w_in_p = w_in_ps[l]
        slots = {}
        dgate, dup = ffn_bwd_act(dh, g8["w_down"], 0, s["gate"], s["up"], "ffn_bwd_act")
        slot_of = lambda n, full: full.reshape(g8[n].shape[:1] + g8[n].shape[2:])
        slots["w_down"] = slot_of("w_down", mm_tn([s["act"]], dh, "dw_down", out_dtype=BF16)[0])
        dw_gate_t, dw_up_t = mm_tn([dgate, dup], s["g"], "dw_gate_up", out_dtype=BF16)
        slots["w_gate"] = slot_of("w_gate", dw_gate_t)
        slots["w_up"] = slot_of("w_up", dw_up_t)
        pending += pair_reduce(l, FFN_WEIGHTS, slots, None)[0]
        dh_mid, dg_ffn = mm_nt_normbwd([("gathered", dgate, g8["w_gate"]), ("gathered", dup, g8["w_up"])],
                                       s["h_mid"], row(ffn_norm_g, l), dh, 0, "ffn_bwd_in")
        small_grads["ffn_norm_g"][l] = dg_ffn[0]
        slots["w_out"] = slot_of("w_out", mm_tn([s["cat"]], dh_mid, "dw_out", out_dtype=BF16)[0])
        dy_pool, do, delta = out_proj_bwd(dh_mid, g8["w_out"], 0, s["o"], "out_proj_bwd")
        du, dw_pool, dscale = pool_bwd(s["u"], dy_pool, w_pool_b[l], row(pool_scale, l), "pool_bwd")
        small_grads["w_pool"][l] = dw_pool
        small_grads["pool_scale"][l] = dscale[0]
        pool_rider = [jnp.stack(small_grads["w_pool"], axis=0).reshape(-1, LANES)] if l == 0 else []
        dqp, dkp, dv, *arrived = attn_bwd(s["qp"], s["kp"], s["v"], do, s["lse"], delta, "attn_bwd_exchange",
                                          exchange=[p for _, _, p in pending], gather=pool_rider)
        if pool_rider:
            got_pool = arrived.pop()
        for (k, n, _), parts in zip(pending, arrived):
            got[k][n] = parts
        dq, dkv, dkr, dgq, dgkn, dgkr = qk_prep_bwd(dqp, dkp, dv, s["q"], s["kv"], s["kr"], cosf, sins, seg,
                                                    s["gq"], s["gkn"], s["gkr"], "qk_prep_bwd")
        small_grads["q_norm_g"][l] = dgq[0, :QK_HEAD_DIM]
        small_grads["k_norm_g"][l] = jnp.concatenate([dgkn[0, :QK_NOPE_DIM], dgkr[0, QK_NOPE_DIM:QK_HEAD_DIM]])
        heads_first = lambda full: jnp.swapaxes(full.reshape(full.shape[0], MLA_HEADS, HEAD_PAD), 0, 1)
        slots["w_q_b"] = heads_first(mm_tn([s["qn"]], dq, "dw_q", out_dtype=BF16)[0])[:, :, :QK_HEAD_DIM]
        dc_q, dg_qa = mm_nt_normbwd([("heads", dq, g8["w_q_b"])], s["c_q"], row(q_a_norm_g, l), None, 0,
                                    "q_proj_bwd")
        small_grads["q_a_norm_g"][l] = dg_qa[0]
        slots["w_kv_b"] = heads_first(mm_tn([s["kvn"]], dkv, "dw_kv", out_dtype=BF16)[0])
        dc_kv, dg_kva = mm_nt_normbwd([("heads", dkv, g8["w_kv_b"])], s["c_kv"], row(kv_a_norm_g, l), None, 0,
                                      "kv_proj_bwd")
        small_grads["kv_a_norm_g"][l] = dg_kva[0]
        dw_pool_t, dw_q_t, dw_kv_t, dw_rope_t = mm_tn([du, dc_q, dc_kv, dkr], s["a"], "dw_in", out_dtype=BF16)
        dw_in_t = jnp.concatenate([dw_pool_t, dw_q_t, dw_kv_t, dw_rope_t[QK_NOPE_DIM:QK_HEAD_DIM]], axis=0)
        slots["w_in"] = slot_of("w_in", dw_in_t)
        dh, dg_attn = mm_nt_normbwd(
            [("rows", du, w_in_p, 0), ("rows", dc_q, w_in_p, s1), ("rows", dc_kv, w_in_p, s2),
             ("rows", dkr, w_in_p, s3)],
            s["h"], row(attn_norm_g, l), dh_mid, 0, "in_proj_bwd")
        small_grads["attn_norm_g"][l] = dg_attn[0]

        small_slots = None
        if l == 0:
            rep_shapes = [weights[n].shape for n in REPLICATED]
            rep_count = sum(int(jnp.size(weights[n])) for n in REPLICATED)
            rep_rows = -(-rep_count // (8 * LANES)) * 8
            rep_packed = _pack_flat([jnp.stack(small_grads[n], axis=0) for n in REPLICATED], rep_rows)
            meta_slots = jnp.transpose(dh[:N_META].reshape(N_META, N_DEV, LANES), (1, 0, 2))
            small_slots = jnp.concatenate(
                [meta_slots, jnp.broadcast_to(rep_packed[None], (N_DEV, rep_rows, LANES))], axis=1)
        pending, got_small = pair_reduce(l, [n for n in SHARDED if n not in FFN_WEIGHTS], slots, small_slots)
    for (k, n, _), parts in zip(pending, exchange_chips([p for _, _, p in pending], "grad_exchange_chips")):
        got[k][n] = parts

    grad_x = dh[N_META:length][None]

    per = [{} for _ in range(4)]
    for n in SHARDED:
        parts = [got[l][n] for l in range(depth)]
        if n in TRANSPOSED:
            grad = jnp.swapaxes(chip_sum(parts, "chip_sum_" + n), 1, 2)
            outs = adamw_shard(None, weights[n], mom1[n], mom2[n], "adamw_" + n, grad=grad)
        else:
            outs = adamw_shard(parts, weights[n], mom1[n], mom2[n], "adamw_" + n)
        for k in range(4):
            per[k][n] = outs[k]
    ps = lambda src: jnp.concatenate(
        [src["meta_tokens"], _pack_flat([src[n] for n in REPLICATED], rep_rows)], axis=0)
    small_out = adamw_packed(got_small, ps(weights), ps(mom1), ps(mom2), "adamw_small")
    as_rows = lambda a: a.reshape(-1, LANES)
    pool_out = adamw_packed(got_pool, as_rows(w_pool), as_rows(m_w_pool), as_rows(v_w_pool), "adamw_w_pool")
    for k in range(4):
        per[k]["meta_tokens"] = small_out[k][:N_META]
        per[k].update(zip(REPLICATED, _unpack_flat(small_out[k][N_META:], rep_shapes)))
        per[k]["w_pool"] = pool_out[k].reshape(w_pool.shape)
    return (loss, grad_x, *[per[0][n] for n in order], *[per[1][n] for n in order],
            *[per[2][n] for n in order], *[per[3][n] for n in order])
```

```python
import functools

import jax
import jax.numpy as jnp
from jax import lax
from jax.experimental import pallas as pl
from jax.experimental.pallas import tpu as pltpu

F32 = jnp.float32
BF16 = jnp.bfloat16

D_MODEL = 1024
N_META = 16
POOL_WIDTH = 512
POOL_WINDOWS = (2, 4, 8, 16)
POOL_GROUP_DIM = 128
POOL_HALO = 16
MLA_HEADS = 8
QK_NOPE_DIM = 64
QK_ROPE_DIM = 32
QK_HEAD_DIM = 96
V_HEAD_DIM = 64
HEAD_PAD = 128
Q_LORA_RANK = 384
KV_LORA_RANK = 256
ROPE_THETA = 10000.0
RMS_EPS = 1e-6
ATTN_SCALE = QK_HEAD_DIM ** -0.5
LOG2_E = 1.4426950408889634
SCORE_SCALE = ATTN_SCALE * LOG2_E

ADAM_LR = 0.001
ADAM_B1 = 0.9
ADAM_B2 = 0.999
ADAM_EPS = 1e-08
ADAM_WD = 0.01
ADAM_STEP = 10

N_DEV = 8
N_CHIPS = 4
LANES = 128
ROW_TILE = 384
LONG_TILE = 1056
VMEM_LIMIT = 56 * 1024 * 1024


def _params(*sem):
    return pltpu.CompilerParams(dimension_semantics=sem, vmem_limit_bytes=VMEM_LIMIT)


def _row_spec(tile, width):
    return pl.BlockSpec((tile, width), lambda i: (i, 0))


def _const_spec(shape):
    return pl.BlockSpec(shape, lambda i: tuple(0 for _ in shape))


def _layer_spec(w, layer):
    return pl.BlockSpec((None,) + w.shape[1:], lambda *_: (layer, 0, 0))


def _gathered_spec(w8, layer):
    return pl.BlockSpec((N_DEV, None) + w8.shape[2:], lambda *_: (0, layer, 0, 0))


def _nt(a, b):
    return lax.dot_general(a, b, (((1,), (1,)), ((), ())), preferred_element_type=F32)


def _tn(a, b):
    return lax.dot_general(a, b, (((0,), (0,)), ((), ())), preferred_element_type=F32)


def _nn(a, b):
    return jnp.dot(a, b, preferred_element_type=F32)


def _silu(g):
    return g * (1.0 / (1.0 + jnp.exp(-g)))


def _rms(xf):
    return lax.rsqrt(jnp.mean(xf * xf, axis=-1, keepdims=True) + RMS_EPS)


def norm_mm(x, gamma, wt, layer, splits, dtypes, name):
    L, K = x.shape
    tm = ROW_TILE

    def body(x_ref, g_ref, w_ref, a_ref, *z_refs):
        xf = x_ref[...]
        a = ((xf * _rms(xf)) * g_ref[...]).astype(BF16)
        a_ref[...] = a
        z = _nt(a, w_ref[...])
        for (s, n), zr in zip(splits, z_refs):
            zr[...] = z[:, s:s + n].astype(zr.dtype)

    widths = [n for _, n in splits]
    return pl.pallas_call(
        body,
        grid=(L // tm,),
        in_specs=[_row_spec(tm, K), _const_spec((1, K)), _layer_spec(wt, layer)],
        out_specs=[_row_spec(tm, K)] + [_row_spec(tm, n) for n in widths],
        out_shape=[jax.ShapeDtypeStruct((L, K), BF16)]
        + [jax.ShapeDtypeStruct((L, n), dt) for n, dt in zip(widths, dtypes)],
        compiler_params=_params("arbitrary"),
        name=name,
    )(x, gamma, wt)


def norm_mm_heads(x, gamma, w8, layer, name, transposed=False):
    L, K = x.shape
    hw = w8.shape[-2] if transposed else w8.shape[-1]
    tm = ROW_TILE

    def body(x_ref, g_ref, w_ref, a_ref, z_ref):
        xf = x_ref[...]
        a = ((xf * _rms(xf)) * g_ref[...]).astype(BF16)
        a_ref[...] = a
        if hw < HEAD_PAD:
            z_ref[...] = jnp.zeros_like(z_ref)
        for j in range(MLA_HEADS):
            z_ref[:, j * HEAD_PAD:j * HEAD_PAD + hw] = _nt(a, w_ref[j]) if transposed else _nn(a, w_ref[j])

    return pl.pallas_call(
        body,
        grid=(L // tm,),
        in_specs=[_row_spec(tm, K), _const_spec((1, K)), _gathered_spec(w8, layer)],
        out_specs=[_row_spec(tm, K), _row_spec(tm, MLA_HEADS * HEAD_PAD)],
        out_shape=[jax.ShapeDtypeStruct((L, K), BF16), jax.ShapeDtypeStruct((L, MLA_HEADS * HEAD_PAD), F32)],
        compiler_params=_params("arbitrary"),
        name=name,
    )(x, gamma, w8)


FF_GROUP = 4


def _ff_spec(w8, layer):
    return pl.BlockSpec((FF_GROUP, None) + w8.shape[2:], lambda j, i: (j, layer, 0, 0))


def ffn_up(g, w_gate8, w_up8, layer, name):
    L, K = g.shape
    fb = w_gate8.shape[-2]
    tm, tf = ROW_TILE, FF_GROUP * fb

    def body(a_ref, wg_ref, wu_ref, gate_ref, up_ref, act_ref):
        a = a_ref[...]
        gate = _nt(a, wg_ref[...].reshape(tf, K))
        up = _nt(a, wu_ref[...].reshape(tf, K))
        gate_ref[...] = gate.astype(BF16)
        up_ref[...] = up.astype(BF16)
        act_ref[...] = (_silu(gate) * up).astype(BF16)

    tile = pl.BlockSpec((tm, tf), lambda j, i: (i, j))
    F = N_DEV * fb
    return pl.pallas_call(
        body,
        grid=(N_DEV // FF_GROUP, L // tm),
        in_specs=[pl.BlockSpec((tm, K), lambda j, i: (i, 0)), _ff_spec(w_gate8, layer), _ff_spec(w_up8, layer)],
        out_specs=[tile, tile, tile],
        out_shape=[
            jax.ShapeDtypeStruct((L, F), BF16),
            jax.ShapeDtypeStruct((L, F), BF16),
            jax.ShapeDtypeStruct((L, F), BF16),
        ],
        compiler_params=_params("arbitrary", "arbitrary"),
        name=name,
    )(g, w_gate8, w_up8)


def mm_res(a, w8, res, layer, name, gamma=None):
    L = a.shape[0]
    kb, N = w8.shape[-2:]
    tm = ROW_TILE
    normed = gamma is not None

    def body(a_ref, w_ref, r_ref, *rest):
        out = r_ref[...] + _nn(a_ref[...], w_ref[...].reshape(N_DEV * kb, N))
        if normed:
            g_ref, o_ref, n_ref = rest
            n_ref[...] = ((out * _rms(out)) * g_ref[...]).astype(BF16)
        else:
            (o_ref,) = rest
        o_ref[...] = out

    in_specs = [_row_spec(tm, N_DEV * kb), _gathered_spec(w8, layer), _row_spec(tm, N)]
    out_specs = [_row_spec(tm, N)]
    out_shape = [jax.ShapeDtypeStruct((L, N), F32)]
    args = [a, w8, res]
    if normed:
        in_specs.append(_const_spec((1, N)))
        out_specs.append(_row_spec(tm, N))
        out_shape.append(jax.ShapeDtypeStruct((L, N), BF16))
        args.append(gamma)
    return pl.pallas_call(
        body,
        grid=(L // tm,),
        in_specs=in_specs,
        out_specs=out_specs,
        out_shape=out_shape,
        compiler_params=_params("arbitrary"),
        name=name,
    )(*args)


def out_proj_bwd(dz, w8, layer, o, name):
    L, N = dz.shape
    kb = w8.shape[-2]
    C = o.shape[1]
    tm = ROW_TILE

    def body(dz_ref, w_ref, o_ref, dy_ref, do_ref, dl_ref):
        d = _nt(dz_ref[...].astype(BF16), w_ref[...].reshape(N_DEV * kb, N))
        dy_ref[...] = d[:, :C]
        do_ref[...] = d[:, C:]
        left = lax.broadcasted_iota(jnp.int32, (tm, LANES), 1) < V_HEAD_DIM
        for p in range(C // LANES):
            cols = slice(p * LANES, (p + 1) * LANES)
            prod = d[:, C + p * LANES:C + (p + 1) * LANES] * o_ref[:, cols]
            d0 = jnp.sum(jnp.where(left, prod, 0.0), axis=1, keepdims=True)
            d1 = jnp.sum(jnp.where(left, 0.0, prod), axis=1, keepdims=True)
            dl_ref[:, cols] = jnp.where(left, d0, d1)

    return pl.pallas_call(
        body,
        grid=(L // tm,),
        in_specs=[_row_spec(tm, N), _gathered_spec(w8, layer), _row_spec(tm, C)],
        out_specs=[_row_spec(tm, C)] * 3,
        out_shape=[jax.ShapeDtypeStruct((L, C), F32)] * 3,
        compiler_params=_params("arbitrary"),
        name=name,
    )(dz, w8, o)


def mm_nt_normbwd(terms, x, gamma, dres, layer, name):
    L, K = x.shape
    tm = ROW_TILE
    n_terms = len(terms)
    has_res = dres is not None
    weights = []
    for t in terms:
        if not any(t[2] is u for u in weights):
            weights.append(t[2])
    which = [[t[2] is u for u in weights].index(True) for t in terms]
    n_in = n_terms + len(weights)

    def body(*refs):
        dz_refs = refs[:n_terms]
        w_refs = [refs[n_terms + n] for n in which]
        x_ref, g_ref = refs[n_in], refs[n_in + 1]
        pos = n_in + 2
        r_ref = refs[pos] if has_res else None
        dx_ref, dg_ref = refs[pos + has_res], refs[pos + has_res + 1]
        da = None
        for t, dz_ref, w_ref in zip(terms, dz_refs, w_refs):
            if t[0] == "rows":
                n, at = t[1].shape[1], t[3]
                parts = [_nn(dz_ref[...].astype(BF16), w_ref[at:at + n, :])]
            elif t[0] == "heads":
                hw = t[2].shape[-1]
                parts = [_nt(dz_ref[:, j * HEAD_PAD:j * HEAD_PAD + hw].astype(BF16), w_ref[j])
                         for j in range(MLA_HEADS)]
            elif t[0] == "heads_t":
                hw = t[2].shape[-2]
                parts = [_nn(dz_ref[:, j * HEAD_PAD:j * HEAD_PAD + hw].astype(BF16), w_ref[j])
                         for j in range(MLA_HEADS)]
            else:
                nb = t[2].shape[-2]
                parts = [_nn(dz_ref[...].astype(BF16), w_ref[...].reshape(N_DEV * nb, K))]
            for p in parts:
                da = p if da is None else da + p
        xf = x_ref[...]
        r = _rms(xf)
        xh = xf * r

        @pl.when(pl.program_id(0) == 0)
        def _():
            dg_ref[...] = jnp.zeros_like(dg_ref)

        dg_ref[...] += jnp.sum(da * xh, axis=0, keepdims=True)
        dxh = da * g_ref[...]
        dx = r * (dxh - xh * jnp.mean(dxh * xh, axis=-1, keepdims=True))
        if has_res:
            dx = dx + r_ref[...]
        dx_ref[...] = dx

    in_specs = [_row_spec(tm, t[1].shape[1]) for t in terms]
    for w in weights:
        in_specs.append(_layer_spec(w, layer) if w.ndim == 3 else _gathered_spec(w, layer))
    in_specs += [_row_spec(tm, K), _const_spec((1, K))]
    args = [t[1] for t in terms] + weights + [x, gamma]
    if has_res:
        in_specs.append(_row_spec(tm, K))
        args.append(dres)
    return pl.pallas_call(
        body,
        grid=(L // tm,),
        in_specs=in_specs,
        out_specs=[_row_spec(tm, K), _const_spec((1, K))],
        out_shape=[jax.ShapeDtypeStruct((L, K), F32), jax.ShapeDtypeStruct((1, K), F32)],
        compiler_params=_params("arbitrary"),
        name=name,
    )(*args)


MAX_OUT_ROWS = 1408


def mm_tn(a_list, b, name, out_dtype=F32):
    n = len(a_list)
    L, N = b.shape
    tks = [MAX_OUT_ROWS if (a.shape[1] > MAX_OUT_ROWS and a.shape[1] % MAX_OUT_ROWS == 0) else a.shape[1]
           for a in a_list]
    blocks = a_list[0].shape[1] // tks[0]
    assert all(a.shape[1] // tk == blocks for a, tk in zip(a_list, tks))
    tl = LONG_TILE if L % LONG_TILE == 0 else ROW_TILE
    n_l = L // tl

    def body(*refs):
        a_refs, b_ref = refs[:n], refs[n]
        o_refs, accs = refs[n + 1:2 * n + 1], refs[2 * n + 1:]
        l = pl.program_id(1)
        bt = b_ref[...].astype(BF16)
        for a_ref, o_ref, acc in zip(a_refs, o_refs, accs):
            @pl.when(l == 0)
            def _(acc=acc):
                acc[...] = jnp.zeros_like(acc)

            acc[...] += _tn(a_ref[...].astype(BF16), bt)

            @pl.when(l == n_l - 1)
            def _(acc=acc, o_ref=o_ref):
                o_ref[...] = acc[...].astype(o_ref.dtype)

    return pl.pallas_call(
        body,
        grid=(blocks, n_l),
        in_specs=[pl.BlockSpec((tl, tk), lambda j, l: (l, j)) for tk in tks]
        + [pl.BlockSpec((tl, N), lambda j, l: (l, 0))],
        out_specs=[pl.BlockSpec((tk, N), lambda j, l: (j, 0)) for tk in tks],
        out_shape=[jax.ShapeDtypeStruct((a.shape[1], N), out_dtype) for a in a_list],
        scratch_shapes=[pltpu.VMEM((tk, N), F32) for tk in tks],
        compiler_params=_params("arbitrary", "arbitrary"),
        name=name,
    )(*a_list, b)


def ffn_bwd_act(dh, w_down8, layer, gate, up, name):
    L, K = dh.shape
    fb = w_down8.shape[-2]
    tm, tf = ROW_TILE, FF_GROUP * fb

    def body(dh_ref, w_ref, gate_ref, up_ref, dgate_ref, dup_ref):
        dact = _nt(dh_ref[...].astype(BF16), w_ref[...].reshape(tf, K))
        g = gate_ref[...].astype(F32)
        sig = 0.5 * jnp.tanh(0.5 * g) + 0.5
        dup_ref[...] = (dact * (g * sig)).astype(BF16)
        dgate_ref[...] = (dact * up_ref[...].astype(F32) * (sig * (1.0 + g * (1.0 - sig)))).astype(BF16)

    tile = pl.BlockSpec((tm, tf), lambda j, i: (i, j))
    F = N_DEV * fb
    return pl.pallas_call(
        body,
        grid=(N_DEV // FF_GROUP, L // tm),
        in_specs=[pl.BlockSpec((tm, K), lambda j, i: (i, 0)), _ff_spec(w_down8, layer), tile, tile],
        out_specs=[tile, tile],
        out_shape=[jax.ShapeDtypeStruct((L, F), BF16), jax.ShapeDtypeStruct((L, F), BF16)],
        compiler_params=_params("arbitrary", "arbitrary"),
        name=name,
    )(dh, w_down8, gate, up)


def _pool_residual(scr, lo, tm, g, w, t):
    cols = slice(g * POOL_GROUP_DIM, (g + 1) * POOL_GROUP_DIM)
    cur = scr[lo:lo + tm, cols]
    s = cur
    for k in range(1, w):
        s = s + scr[lo - k:lo - k + tm, cols]
    cnt = jnp.minimum(t + 1, w).astype(F32)
    return s / cnt - cur


def pool_fwd(u, w_pool, scale, name):
    L, C = u.shape
    tm, halo = ROW_TILE, POOL_HALO

    def body(u_ref, halo_ref, w_ref, s_ref, y_ref, scr):
        i = pl.program_id(0)
        scr[0:halo, :] = jnp.where(i > 0, halo_ref[...], 0.0)
        scr[halo:halo + tm, :] = u_ref[...]
        t = i * tm + lax.broadcasted_iota(jnp.int32, (tm, POOL_GROUP_DIM), 0)
        for g, w in enumerate(POOL_WINDOWS):
            cols = slice(g * POOL_GROUP_DIM, (g + 1) * POOL_GROUP_DIM)
            p = _pool_residual(scr, halo, tm, g, w, t)
            y = _nn(p.astype(BF16), w_ref[g]) * s_ref[:, cols]
            y_ref[:, cols] = y.astype(y_ref.dtype)

    return pl.pallas_call(
        body,
        grid=(L // tm,),
        in_specs=[
            _row_spec(tm, C),
            pl.BlockSpec((halo, C), lambda i: (jnp.maximum(i * (tm // halo) - 1, 0), 0)),
            _const_spec(w_pool.shape),
            _const_spec((1, C)),
        ],
        out_specs=_row_spec(tm, C),
        out_shape=jax.ShapeDtypeStruct((L, 2 * C), BF16),
        scratch_shapes=[pltpu.VMEM((tm + halo, C), F32)],
        compiler_params=_params("arbitrary"),
        name=name,
    )(u, u, w_pool, scale)


def pool_bwd(u, dy, w_pool, scale, name):
    L, C = u.shape
    tm, halo = ROW_TILE, POOL_HALO
    n_tiles = L // tm
    last_halo = L // halo - 1

    def body(u_ref, uh_ref, dy_ref, dyh_ref, w_ref, s_ref, du_ref, dw_ref, ds_ref, scr_u, scr_q):
        i = pl.program_id(0)

        @pl.when(i == 0)
        def _():
            dw_ref[...] = jnp.zeros_like(dw_ref)
            ds_ref[...] = jnp.zeros_like(ds_ref)

        scr_u[0:halo, :] = jnp.where(i > 0, uh_ref[...], 0.0)
        scr_u[halo:halo + tm, :] = u_ref[...]
        t = i * tm + lax.broadcasted_iota(jnp.int32, (tm, POOL_GROUP_DIM), 0)
        th = (i + 1) * tm + lax.broadcasted_iota(jnp.int32, (halo, POOL_GROUP_DIM), 0)
        for g, w in enumerate(POOL_WINDOWS):
            cols = slice(g * POOL_GROUP_DIM, (g + 1) * POOL_GROUP_DIM)
            p = _pool_residual(scr_u, halo, tm, g, w, t).astype(BF16)
            wg = w_ref[g]
            sc = s_ref[:, cols]
            dy = dy_ref[:, cols]
            ds_ref[:, cols] += jnp.sum(dy * _nn(p, wg), axis=0, keepdims=True)
            dys = (dy * sc).astype(BF16)
            dw_ref[g] += _tn(p, dys)
            dp = _nt(dys, wg)
            dyh = jnp.where(i < n_tiles - 1, dyh_ref[:, cols], 0.0)
            dph = _nt((dyh * sc).astype(BF16), wg)
            scr_q[0:tm, cols] = dp / jnp.minimum(t + 1, w).astype(F32)
            scr_q[tm:tm + halo, cols] = dph / jnp.minimum(th + 1, w).astype(F32)
            acc = scr_q[0:tm, cols]
            for k in range(1, w):
                acc = acc + scr_q[k:k + tm, cols]
            du_ref[:, cols] = acc - dp

    return pl.pallas_call(
        body,
        grid=(n_tiles,),
        in_specs=[
            _row_spec(tm, C),
            pl.BlockSpec((halo, C), lambda i: (jnp.maximum(i * (tm // halo) - 1, 0), 0)),
            _row_spec(tm, C),
            pl.BlockSpec((halo, C), lambda i: (jnp.minimum((i + 1) * (tm // halo), last_halo), 0)),
            _const_spec(w_pool.shape),
            _const_spec((1, C)),
        ],
        out_specs=[_row_spec(tm, C), _const_spec(w_pool.shape), _const_spec((1, C))],
        out_shape=[
            jax.ShapeDtypeStruct((L, C), F32),
            jax.ShapeDtypeStruct(w_pool.shape, F32),
            jax.ShapeDtypeStruct((1, C), F32),
        ],
        scratch_shapes=[pltpu.VMEM((tm + halo, C), F32), pltpu.VMEM((tm + halo, C), F32)],
        compiler_params=_params("arbitrary"),
        name=name,
    )(u, u, dy, dy, w_pool, scale)


def _head_masks(rows):
    lane = lax.broadcasted_iota(jnp.int32, (rows, HEAD_PAD), 1)
    return lane, lane < QK_NOPE_DIM, (lane >= QK_NOPE_DIM) & (lane < QK_HEAD_DIM)


def _rope_swap(x, lane):
    half = QK_ROPE_DIM // 2
    swapped = jnp.where(lane < QK_NOPE_DIM + half, pltpu.roll(x, HEAD_PAD - half, 1), pltpu.roll(x, half, 1))
    return jnp.where((lane >= QK_NOPE_DIM) & (lane < QK_HEAD_DIM), swapped, 0.0)


def _seg_mean(v, seg_ref):
    hi = v.astype(BF16)
    lo = (v - hi.astype(F32)).astype(BF16)
    seg = seg_ref[...]
    return _nn(hi, seg) + _nn(lo, seg)


def _segment_matrix():
    lane = jnp.arange(HEAD_PAD)
    seg = jnp.where(lane < QK_NOPE_DIM, 0, jnp.where(lane < QK_HEAD_DIM, 1, 2))
    inv = jnp.where(lane < QK_NOPE_DIM, 1.0 / QK_NOPE_DIM, jnp.where(lane < QK_HEAD_DIM, 1.0 / QK_ROPE_DIM, 0.0))
    return jnp.where(seg[:, None] == seg[None, :], inv[None, :], 0.0).astype(BF16)


def qk_prep_fwd(q, kv, kr, cosf, sins, seg, gq, gkn, gkr, name):
    L = q.shape[0]
    tm = ROW_TILE
    W = MLA_HEADS * HEAD_PAD

    def body(q_ref, kv_ref, kr_ref, c_ref, s_ref, seg_ref, gq_ref, gkn_ref, gkr_ref, qo_ref, ko_ref, vo_ref):
        lane, m_n, _ = _head_masks(tm)
        cosf_, sins_ = c_ref[...], s_ref[...]
        kr_ = kr_ref[...]
        rk = lax.rsqrt(_seg_mean(kr_ * kr_, seg_ref) + RMS_EPS)
        krn = kr_ * rk * gkr_ref[...]
        krf = krn * cosf_ + _rope_swap(krn, lane) * sins_
        for h in range(MLA_HEADS):
            cols = slice(h * HEAD_PAD, (h + 1) * HEAD_PAD)
            qh = q_ref[:, cols]
            qn = qh * lax.rsqrt(_seg_mean(qh * qh, seg_ref) + RMS_EPS) * gq_ref[...]
            qo_ref[:, cols] = (qn * cosf_ + _rope_swap(qn, lane) * sins_).astype(BF16)
            kh = jnp.where(m_n, kv_ref[:, cols], 0.0)
            rn = lax.rsqrt(_seg_mean(kh * kh, seg_ref) + RMS_EPS)
            ko_ref[:, cols] = (kh * rn * gkn_ref[...] + krf).astype(BF16)
        for p in range(MLA_HEADS // 2):
            even = kv_ref[:, 2 * p * HEAD_PAD:(2 * p + 1) * HEAD_PAD]
            odd = kv_ref[:, (2 * p + 1) * HEAD_PAD:(2 * p + 2) * HEAD_PAD]
            pair = jnp.where(m_n, pltpu.roll(even, V_HEAD_DIM, 1), odd)
            vo_ref[:, p * LANES:(p + 1) * LANES] = pair.astype(BF16)

    vec = _const_spec((1, HEAD_PAD))
    return pl.pallas_call(
        body,
        grid=(L // tm,),
        in_specs=[_row_spec(tm, W), _row_spec(tm, W), _row_spec(tm, HEAD_PAD), _row_spec(tm, HEAD_PAD),
                  _row_spec(tm, HEAD_PAD), _const_spec((HEAD_PAD, HEAD_PAD)), vec, vec, vec],
        out_specs=[_row_spec(tm, W), _row_spec(tm, W), _row_spec(tm, W // 2)],
        out_shape=[jax.ShapeDtypeStruct((L, W), BF16), jax.ShapeDtypeStruct((L, W), BF16),
                   jax.ShapeDtypeStruct((L, W // 2), BF16)],
        compiler_params=_params("arbitrary"),
        name=name,
    )(q, kv, kr, cosf, sins, seg, gq, gkn, gkr)


def qk_prep_bwd(dqo, dko, dv, q, kv, kr, cosf, sins, seg, gq, gkn, gkr, name):
    L = q.shape[0]
    tm = ROW_TILE
    W = MLA_HEADS * HEAD_PAD

    def body(dqo_ref, dko_ref, dv_ref, q_ref, kv_ref, kr_ref, c_ref, s_ref, seg_ref, gq_ref, gkn_ref, gkr_ref,
             dq_ref, dkv_ref, dkr_ref, dgq_ref, dgkn_ref, dgkr_ref):
        @pl.when(pl.program_id(0) == 0)
        def _():
            dgq_ref[...] = jnp.zeros_like(dgq_ref)
            dgkn_ref[...] = jnp.zeros_like(dgkn_ref)
            dgkr_ref[...] = jnp.zeros_like(dgkr_ref)

        lane, m_n, _ = _head_masks(tm)
        cosf_, sins_ = c_ref[...], s_ref[...]
        dgq = jnp.zeros((1, HEAD_PAD), F32)
        dgkn = jnp.zeros((1, HEAD_PAD), F32)
        dkrf = jnp.zeros((tm, HEAD_PAD), F32)
        for h in range(MLA_HEADS):
            cols = slice(h * HEAD_PAD, (h + 1) * HEAD_PAD)
            dy = dqo_ref[:, cols]
            dqn = dy * cosf_ + _rope_swap(dy * sins_, lane)
            qh = q_ref[:, cols]
            rinv = lax.rsqrt(_seg_mean(qh * qh, seg_ref) + RMS_EPS)
            xh = qh * rinv
            dgq = dgq + jnp.sum(dqn * xh, axis=0, keepdims=True)
            dxh = dqn * gq_ref[...]
            dq_ref[:, cols] = rinv * (dxh - xh * _seg_mean(dxh * xh, seg_ref))
            dk = dko_ref[:, cols]
            dkrf = dkrf + dk
            kh = jnp.where(m_n, kv_ref[:, cols], 0.0)
            rn = lax.rsqrt(_seg_mean(kh * kh, seg_ref) + RMS_EPS)
            xk = kh * rn
            dgkn = dgkn + jnp.sum(dk * xk, axis=0, keepdims=True)
            dxk = dk * gkn_ref[...]
            dkn = rn * (dxk - xk * _seg_mean(dxk * xk, seg_ref))
            dvp = dv_ref[:, (h // 2) * LANES:(h // 2 + 1) * LANES]
            dvh = pltpu.roll(dvp, V_HEAD_DIM, 1) if h % 2 == 0 else dvp
            dkv_ref[:, cols] = jnp.where(m_n, dkn, dvh)
        kr_ = kr_ref[...]
        rk = lax.rsqrt(_seg_mean(kr_ * kr_, seg_ref) + RMS_EPS)
        xr = kr_ * rk
        dkrn = dkrf * cosf_ + _rope_swap(dkrf * sins_, lane)
        dgkr_ref[...] += jnp.sum(dkrn * xr, axis=0, keepdims=True)
        dxr = dkrn * gkr_ref[...]
        dkr_ref[...] = rk * (dxr - xr * _seg_mean(dxr * xr, seg_ref))
        dgq_ref[...] += dgq
        dgkn_ref[...] += dgkn

    vec = _const_spec((1, HEAD_PAD))
    return pl.pallas_call(
        body,
        grid=(L // tm,),
        in_specs=[_row_spec(tm, W), _row_spec(tm, W), _row_spec(tm, W // 2), _row_spec(tm, W), _row_spec(tm, W),
                  _row_spec(tm, HEAD_PAD), _row_spec(tm, HEAD_PAD), _row_spec(tm, HEAD_PAD),
                  _const_spec((HEAD_PAD, HEAD_PAD)), vec, vec, vec],
        out_specs=[_row_spec(tm, W), _row_spec(tm, W), _row_spec(tm, HEAD_PAD), vec, vec, vec],
        out_shape=[jax.ShapeDtypeStruct((L, W), F32), jax.ShapeDtypeStruct((L, W), F32),
                   jax.ShapeDtypeStruct((L, HEAD_PAD), F32)] + [jax.ShapeDtypeStruct((1, HEAD_PAD), F32)] * 3,
        compiler_params=_params("arbitrary"),
        name=name,
    )(dqo, dko, dv, q, kv, kr, cosf, sins, seg, gq, gkn, gkr)


def attn_fwd(qp, kp, v, cat, name, gather=()):
    L = qp.shape[0]
    tq = ROW_TILE
    n_i = L // tq
    pair_w = 2 * HEAD_PAD
    n_pairs = MLA_HEADS // 2
    n_g = len(gather)

    def body(q_ref, k_ref, v_ref, cat_in, *rest):
        del cat_in
        o_ref, lse_ref, cat_ref = rest[n_g:n_g + 3]
        i = pl.program_id(1)
        if n_g:
            pair = pl.program_id(0)
            start, forward, finish = _gather_steps(rest[:n_g], rest[n_g + 3:2 * n_g + 3], *rest[2 * n_g + 3:])
            pl.when((pair == 0) & (i == 0))(start)
            pl.when((pair == n_pairs // 2) & (i == 0))(forward)
        left = lax.broadcasted_iota(jnp.int32, (tq, LANES), 1) < V_HEAD_DIM
        row = lax.broadcasted_iota(jnp.int32, (tq, tq), 0)
        col = lax.broadcasted_iota(jnp.int32, (tq, tq), 1)

        def step(j, carry, masked, width=1):
            rows = pl.ds(pl.multiple_of(j * (tq * width), tq), tq * width)
            vv = v_ref[rows, :]
            out = []
            for hh in range(2):
                cols = slice(hh * HEAD_PAD, (hh + 1) * HEAD_PAD)
                m, l, acc = carry[hh]
                s = _nt(q_ref[:, cols], k_ref[rows, cols]) * SCORE_SCALE
                if masked:
                    s = jnp.where(col <= row, s, -jnp.inf)
                m_new = jnp.maximum(m, jnp.max(s, axis=1, keepdims=True))
                alpha = jnp.exp2(m - m_new)
                p = jnp.exp2(s - m_new)
                l = alpha * l + jnp.sum(p, axis=1, keepdims=True)
                acc = alpha * acc + _nn(p.astype(BF16), vv)
                out.append((m_new, l, acc))
            return tuple(out)

        one = (jnp.full((tq, 1), -jnp.inf, F32), jnp.zeros((tq, 1), F32), jnp.zeros((tq, LANES), F32))
        doubles = jnp.right_shift(i, 1)
        carry = lax.fori_loop(0, doubles, functools.partial(step, masked=False, width=2), (one, one))
        carry = lax.fori_loop(2 * doubles, i, functools.partial(step, masked=False), carry)
        (m0, l0, a0), (m1, l1, a1) = step(i, carry, True)
        o = jnp.where(left, a0 / l0, a1 / l1)
        o_ref[...] = o
        cat_ref[...] = o.astype(BF16)
        lse_ref[...] = jnp.where(left, m0 + jnp.log2(l0), m1 + jnp.log2(l1))
        if n_g:
            pl.when((pair == n_pairs - 1) & (i == n_i - 1))(finish)

    return pl.pallas_call(
        body,
        grid=(n_pairs, n_i),
        in_specs=[
            pl.BlockSpec((tq, pair_w), lambda p, i: (i, p)),
            pl.BlockSpec((L, pair_w), lambda p, i: (0, p)),
            pl.BlockSpec((L, LANES), lambda p, i: (0, p)),
            _ANY,
        ] + [_ANY] * n_g,
        out_specs=[
            pl.BlockSpec((tq, LANES), lambda p, i: (i, p)),
            pl.BlockSpec((tq, LANES), lambda p, i: (i, p)),
            pl.BlockSpec((tq, LANES), lambda p, i: (i, n_pairs + p)),
        ] + [_ANY] * n_g,
        out_shape=[jax.ShapeDtypeStruct((L, n_pairs * LANES), F32), jax.ShapeDtypeStruct((L, n_pairs * LANES), F32),
                   jax.ShapeDtypeStruct(cat.shape, cat.dtype)]
        + [jax.ShapeDtypeStruct((N_DEV,) + g.shape, g.dtype) for g in gather],
        scratch_shapes=_gather_scratch(n_g) if n_g else [],
        input_output_aliases={3: 2},
        compiler_params=_params("arbitrary", "arbitrary"),
        name=name,
    )(qp, kp, v, cat, *gather)


def attn_bwd(qp, kp, v, do, lse, delta, name, exchange=(), gather=()):
    L = qp.shape[0]
    tq = ROW_TILE
    n_q = L // tq
    pair_w = 2 * HEAD_PAD
    n_pairs = MLA_HEADS // 2
    n_x, n_g = len(exchange), len(gather)
    n_r = n_x + n_g

    def body(q_ref, k_ref, v_ref, do_ref, lse_ref, dl_ref, *rest):
        dq_ref, dk_ref, dv_ref = rest[n_r:n_r + 3]
        riders_in, riders_out, sems = rest[:n_r], rest[n_r + 3:2 * n_r + 3], rest[2 * n_r + 3:]
        j = pl.program_id(1)
        pair = pl.program_id(0)
        steps = []
        if n_x:
            steps.append(_chip_exchange_steps(riders_in[:n_x], riders_out[:n_x], *sems[:3]))
        if n_g:
            steps.append(_direct_gather_steps(riders_in[n_x:], riders_out[n_x:], *sems[-3:]))
        for start, _ in steps:
            pl.when((pair == 0) & (j == 0))(start)

        @pl.when(j == 0)
        def _():
            dq_ref[...] = jnp.zeros_like(dq_ref)

        dk_ref[...] = jnp.zeros_like(dk_ref)
        dv_ref[...] = jnp.zeros_like(dv_ref)
        row = lax.broadcasted_iota(jnp.int32, (tq, tq), 0)
        col = lax.broadcasted_iota(jnp.int32, (tq, tq), 1)

        def step(t, carry, masked, width=1, first=0):
            rows = pl.ds(pl.multiple_of((first + t * width) * tq, tq), tq * width)
            do = do_ref[rows, :]
            left = lax.broadcasted_iota(jnp.int32, do.shape, 1) < V_HEAD_DIM
            vv = v_ref[...]
            dv = None
            for hh in range(2):
                cols = slice(hh * HEAD_PAD, (hh + 1) * HEAD_PAD)
                stat = slice(hh * V_HEAD_DIM, hh * V_HEAD_DIM + 1)
                q = q_ref[rows, cols]
                k = k_ref[:, cols]
                dom = jnp.where(left if hh == 0 else jnp.logical_not(left), do, 0.0).astype(BF16)
                s = _nt(q, k) * SCORE_SCALE
                p = jnp.exp2(s - lse_ref[rows, stat])
                if masked:
                    p = jnp.where(col <= row, p, 0.0)
                dp = _nt(dom, vv)
                ds = (p * (dp - dl_ref[rows, stat]) * ATTN_SCALE).astype(BF16)
                dq_ref[rows, cols] += _nn(ds, k)
                dk_ref[:, cols] += _tn(ds, q)
                t = _tn(p.astype(BF16), dom)
                dv = t if dv is None else dv + t
            dv_ref[...] += dv
            return carry

        step(j, 0, True)
        doubles = jnp.right_shift(n_q - 1 - j, 1)
        lax.fori_loop(0, doubles, functools.partial(step, masked=False, width=2, first=j + 1), 0)
        lax.fori_loop(j + 1 + 2 * doubles, n_q, functools.partial(step, masked=False), 0)
        for _, finish in steps:
            pl.when((pair == n_pairs - 1) & (j == n_q - 1))(finish)

    return pl.pallas_call(
        body,
        grid=(n_pairs, n_q),
        in_specs=[
            pl.BlockSpec((L, pair_w), lambda p, j: (0, p)),
            pl.BlockSpec((tq, pair_w), lambda p, j: (j, p)),
            pl.BlockSpec((tq, LANES), lambda p, j: (j, p)),
            pl.BlockSpec((L, LANES), lambda p, j: (0, p)),
            pl.BlockSpec((L, LANES), lambda p, j: (0, p)),
            pl.BlockSpec((L, LANES), lambda p, j: (0, p)),
        ] + [_ANY] * n_r,
        out_specs=[
            pl.BlockSpec((L, pair_w), lambda p, j: (0, p)),
            pl.BlockSpec((tq, pair_w), lambda p, j: (j, p)),
            pl.BlockSpec((tq, LANES), lambda p, j: (j, p)),
        ] + [_ANY] * n_r,
        out_shape=[
            jax.ShapeDtypeStruct((L, n_pairs * pair_w), F32),
            jax.ShapeDtypeStruct((L, n_pairs * pair_w), F32),
            jax.ShapeDtypeStruct((L, n_pairs * LANES), F32),
        ] + [jax.ShapeDtypeStruct(e.shape, e.dtype) for e in exchange]
        + [jax.ShapeDtypeStruct((N_DEV,) + g.shape, g.dtype) for g in gather],
        scratch_shapes=(_chip_exchange_scratch(n_x) if n_x else []) + (_direct_gather_scratch(n_g) if n_g else []),
        compiler_params=_params("arbitrary", "arbitrary"),
        name=name,
    )(qp, kp, v, do, lse, delta, *exchange, *gather)


def loss_head(h, target, n_real, name):
    L, D = h.shape
    tm = ROW_TILE

    def body(h_ref, t_ref, dh_ref, sq_ref):
        i = pl.program_id(0)

        @pl.when(i == 0)
        def _():
            sq_ref[...] = jnp.zeros_like(sq_ref)

        t = i * tm + lax.broadcasted_iota(jnp.int32, (tm, D), 0)
        real = (t >= N_META) & (t < N_META + n_real)
        diff = jnp.where(real, h_ref[...] - t_ref[...], 0.0)
        dh_ref[...] = diff * (1.0 / D)
        sq_ref[...] += jnp.sum(diff * diff, axis=0, keepdims=True)

    return pl.pallas_call(
        body,
        grid=(L // tm,),
        in_specs=[_row_spec(tm, D), _row_spec(tm, D)],
        out_specs=[_row_spec(tm, D), _const_spec((1, D))],
        out_shape=[jax.ShapeDtypeStruct((L, D), F32), jax.ShapeDtypeStruct((1, D), F32)],
        compiler_params=_params("arbitrary"),
        name=name,
    )(h, target)


def _mesh_position():
    x, y, c = lax.axis_index("x"), lax.axis_index("y"), lax.axis_index("c")
    return x, y, c, 4 * x + 2 * y + c


def _flip(x, y, c, k):
    px = 1 - x if k & 4 else x
    py = 1 - y if k & 2 else y
    pc = 1 - c if k & 1 else c
    return (px, py, pc), 4 * px + 2 * py + pc


def _other_chips(x, y):
    return [(1 - x, y), (x, 1 - y), (1 - x, 1 - y)]


def _dev_index(px, py, pc):
    return 4 * px + 2 * py + pc


_ANY = pl.BlockSpec(memory_space=pl.ANY)


def cast_bf16(arrays, name):
    n = len(arrays)
    depth = arrays[0].shape[0]

    def body(*refs):
        for k, src in enumerate(refs[:n]):
            for l in range(depth):
                refs[n + k * depth + l][0] = src[l].astype(BF16)

    return pl.pallas_call(
        body,
        out_shape=[jax.ShapeDtypeStruct((1,) + a.shape[1:], BF16) for a in arrays for _ in range(depth)],
        compiler_params=pltpu.CompilerParams(vmem_limit_bytes=VMEM_LIMIT),
        name=name,
    )(*arrays)


def _gather_steps(srcs, dsts, send_sems, recv_sems, local_sems):
    n = len(srcs)
    x, y, c, me = _mesh_position()
    sibling = (x, y, 1 - c)
    chips = _other_chips(x, y)

    def copy(k, b, block, to, from_src=False):
        dst = dsts[b].at[block]
        return pltpu.make_async_remote_copy(
            src_ref=srcs[b] if from_src else dst, dst_ref=dst, send_sem=send_sems.at[k * n + b],
            recv_sem=recv_sems.at[k * n + b], device_id=to, device_id_type=pl.DeviceIdType.MESH)

    def own():
        return [pltpu.make_async_copy(srcs[b], dsts[b].at[me], local_sems.at[b]) for b in range(n)]

    def first():
        out = []
        for b in range(n):
            out.append(copy(0, b, me, sibling, from_src=True))
            out += [copy(1 + q, b, me, (*chip, c), from_src=True) for q, chip in enumerate(chips)]
        return out

    def passed():
        return [copy(4 + q, b, _dev_index(*chip, c), sibling) for q, chip in enumerate(chips) for b in range(n)]

    def start():
        for cp in own() + first():
            cp.start()

    def forward():
        for q, chip in enumerate(chips):
            for b in range(n):
                copy(1 + q, b, _dev_index(*chip, c), sibling).wait_recv()
        for cp in passed():
            cp.start()

    def finish():
        for b in range(n):
            copy(0, b, _dev_index(x, y, 1 - c), sibling).wait_recv()
            for q, chip in enumerate(chips):
                copy(4 + q, b, _dev_index(*chip, 1 - c), sibling).wait_recv()
        for cp in first() + passed():
            cp.wait_send()
        for cp in own():
            cp.wait()

    return start, forward, finish


def _gather_scratch(n):
    copies = N_DEV - 1
    return [pltpu.SemaphoreType.DMA((copies * n,)), pltpu.SemaphoreType.DMA((copies * n,)),
            pltpu.SemaphoreType.DMA((n,))]


def all_gather(payloads, name):
    n = len(payloads)

    def body(*refs):
        start, forward, finish = _gather_steps(refs[:n], refs[n:2 * n], *refs[2 * n:])
        start()
        forward()
        finish()

    return pl.pallas_call(
        body,
        in_specs=[_ANY] * n,
        out_specs=[_ANY] * n,
        out_shape=[jax.ShapeDtypeStruct((N_DEV,) + p.shape, p.dtype) for p in payloads],
        scratch_shapes=_gather_scratch(n),
        name=name,
    )(*payloads)


def exchange_pair(bigs, small, name):
    n = len(bigs)
    has_small = small is not None
    n_big = N_CHIPS * n
    n_sems = n_big + (N_DEV - 1 if has_small else 0)

    def body(*refs):
        big_refs = refs[:n]
        pos = n + has_small
        sib_refs = refs[pos:pos + n]
        send_sems, recv_sems, local_sem = refs[-3:]
        x, y, c, me = _mesh_position()
        sibling = (x, y, 1 - c)
        copies = []
        for b in range(n):
            for q in range(N_CHIPS):
                copies.append(pltpu.make_async_remote_copy(
                    src_ref=big_refs[b].at[_dev_index(q // 2, q % 2, 1 - c)], dst_ref=sib_refs[b].at[q],
                    send_sem=send_sems.at[b * N_CHIPS + q], recv_sem=recv_sems.at[b * N_CHIPS + q],
                    device_id=sibling, device_id_type=pl.DeviceIdType.MESH))
        waits = list(copies)
        if has_small:
            small_ref, gsmall_ref = refs[n], refs[pos + n]
            own = pltpu.make_async_copy(small_ref.at[me], gsmall_ref.at[me], local_sem.at[0])
            own.start()
            for k in range(1, N_DEV):
                peer, peer_idx = _flip(x, y, c, k)
                s = n_big + k - 1
                copies.append(pltpu.make_async_remote_copy(
                    src_ref=small_ref.at[peer_idx], dst_ref=gsmall_ref.at[me], send_sem=send_sems.at[s],
                    recv_sem=recv_sems.at[s], device_id=peer, device_id_type=pl.DeviceIdType.MESH))
                waits.append(pltpu.make_async_remote_copy(
                    src_ref=small_ref.at[peer_idx], dst_ref=gsmall_ref.at[peer_idx], send_sem=send_sems.at[s],
                    recv_sem=recv_sems.at[s], device_id=peer, device_id_type=pl.DeviceIdType.MESH))
        for cp in copies:
            cp.start()
        for cp in waits:
            cp.wait_recv()
        for cp in copies:
            cp.wait_send()
        if has_small:
            own.wait()

    out_shape = [jax.ShapeDtypeStruct((N_CHIPS,) + b.shape[1:], b.dtype) for b in bigs]
    args = list(bigs)
    if has_small:
        out_shape.append(jax.ShapeDtypeStruct(small.shape, small.dtype))
        args.append(small)
    return pl.pallas_call(
        body,
        in_specs=[_ANY] * len(args),
        out_specs=[_ANY] * len(out_shape),
        out_shape=out_shape,
        scratch_shapes=[pltpu.SemaphoreType.DMA((n_sems,)), pltpu.SemaphoreType.DMA((n_sems,)),
                        pltpu.SemaphoreType.DMA((1,))],
        name=name,
    )(*args)


def pair_sum(bigs, from_sibling, core, name):
    n = len(bigs)

    def body(core_ref, *refs):
        del core_ref
        for mine, sib, out in zip(refs[:n], refs[n:2 * n], refs[2 * n:]):
            out[...] = (mine[...].astype(F32) + sib[...].astype(F32)).astype(out.dtype)

    def slot(shape, picked):
        zeros = (0,) * (len(shape) - 1)
        if picked:
            return pl.BlockSpec((None,) + shape[1:], lambda q, core_ref: (2 * q + core_ref[0],) + zeros)
        return pl.BlockSpec((None,) + shape[1:], lambda q, core_ref: (q,) + zeros)

    grid_spec = pltpu.PrefetchScalarGridSpec(
        num_scalar_prefetch=1,
        grid=(N_CHIPS,),
        in_specs=[slot(b.shape, True) for b in bigs] + [slot(s.shape, False) for s in from_sibling],
        out_specs=[slot(s.shape, False) for s in from_sibling],
    )
    return pl.pallas_call(
        body,
        grid_spec=grid_spec,
        out_shape=[jax.ShapeDtypeStruct(s.shape, s.dtype) for s in from_sibling],
        compiler_params=_params("arbitrary"),
        name=name,
    )(core, *bigs, *from_sibling)


def _chip_exchange_steps(part_refs, got_refs, send_sems, recv_sems, local_sems):
    n = len(part_refs)
    x, y, c, me = _mesh_position()
    mine = 2 * x + y

    def copies(landing):
        out = []
        for q, (px, py) in enumerate(_other_chips(x, y)):
            theirs = 2 * px + py
            for b in range(n):
                out.append(pltpu.make_async_remote_copy(
                    src_ref=part_refs[b].at[theirs], dst_ref=got_refs[b].at[theirs if landing else mine],
                    send_sem=send_sems.at[q * n + b], recv_sem=recv_sems.at[q * n + b],
                    device_id=(px, py, c), device_id_type=pl.DeviceIdType.MESH))
        return out

    def own():
        return [pltpu.make_async_copy(part_refs[b].at[mine], got_refs[b].at[mine], local_sems.at[b])
                for b in range(n)]

    def start():
        for cp in own() + copies(False):
            cp.start()

    def finish():
        for cp in copies(True):
            cp.wait_recv()
        for cp in copies(False):
            cp.wait_send()
        for cp in own():
            cp.wait()

    return start, finish


def _chip_exchange_scratch(n):
    return [pltpu.SemaphoreType.DMA(((N_CHIPS - 1) * n,)), pltpu.SemaphoreType.DMA(((N_CHIPS - 1) * n,)),
            pltpu.SemaphoreType.DMA((n,))]


def _direct_gather_steps(srcs, dsts, send_sems, recv_sems, local_sems):
    n = len(srcs)
    x, y, c, me = _mesh_position()

    def copies(landing):
        out = []
        for k in range(1, N_DEV):
            peer, peer_idx = _flip(x, y, c, k)
            for b in range(n):
                out.append(pltpu.make_async_remote_copy(
                    src_ref=srcs[b], dst_ref=dsts[b].at[peer_idx if landing else me],
                    send_sem=send_sems.at[(k - 1) * n + b], recv_sem=recv_sems.at[(k - 1) * n + b],
                    device_id=peer, device_id_type=pl.DeviceIdType.MESH))
        return out

    def own():
        return [pltpu.make_async_copy(srcs[b], dsts[b].at[me], local_sems.at[b]) for b in range(n)]

    def start():
        for cp in own() + copies(False):
            cp.start()

    def finish():
        for cp in copies(True):
            cp.wait_recv()
        for cp in copies(False):
            cp.wait_send()
        for cp in own():
            cp.wait()

    return start, finish


def _direct_gather_scratch(n):
    return [pltpu.SemaphoreType.DMA(((N_DEV - 1) * n,)), pltpu.SemaphoreType.DMA(((N_DEV - 1) * n,)),
            pltpu.SemaphoreType.DMA((n,))]


def exchange_chips(partials, name):
    n = len(partials)

    def body(*refs):
        start, finish = _chip_exchange_steps(refs[:n], refs[n:2 * n], *refs[2 * n:])
        start()
        finish()

    return pl.pallas_call(
        body,
        in_specs=[_ANY] * n,
        out_specs=[_ANY] * n,
        out_shape=[jax.ShapeDtypeStruct(p.shape, p.dtype) for p in partials],
        scratch_shapes=_chip_exchange_scratch(n),
        name=name,
    )(*partials)


def _adamw_math(g, w, m, v):
    m_new = ADAM_B1 * m + (1.0 - ADAM_B1) * g
    v_new = ADAM_B2 * v + (1.0 - ADAM_B2) * (g * g)
    m_hat = m_new / (1.0 - ADAM_B1 ** ADAM_STEP)
    v_hat = v_new / (1.0 - ADAM_B2 ** ADAM_STEP)
    delta = -ADAM_LR * (m_hat / (jnp.sqrt(v_hat) + ADAM_EPS) + ADAM_WD * w)
    return delta, m_new, v_new


def _chip_total(p_ref):
    g = p_ref[0].astype(F32)
    for q in range(1, N_CHIPS):
        g = g + p_ref[q].astype(F32)
    return g


def chip_sum(parts, name):
    depth = len(parts)

    def body(*refs):
        for l in range(depth):
            refs[depth][l] = _chip_total(refs[l])

    return pl.pallas_call(
        body,
        out_shape=jax.ShapeDtypeStruct((depth,) + parts[0].shape[1:], F32),
        compiler_params=pltpu.CompilerParams(vmem_limit_bytes=VMEM_LIMIT),
        name=name,
    )(*parts)


def adamw_shard(parts, w, m, v, name, grad=None):
    depth = w.shape[0]
    n_in = 1 if grad is not None else depth

    def body(*refs):
        w_ref, m_ref, v_ref, g_ref, d_ref, mo_ref, vo_ref = refs[n_in:]
        for l in range(depth):
            g = refs[0][l] if grad is not None else _chip_total(refs[l])
            delta, m_new, v_new = _adamw_math(g, w_ref[l], m_ref[l], v_ref[l])
            g_ref[l] = g
            d_ref[l] = delta
            mo_ref[l] = m_new
            vo_ref[l] = v_new

    return pl.pallas_call(
        body,
        out_shape=[jax.ShapeDtypeStruct(w.shape, F32)] * 4,
        compiler_params=pltpu.CompilerParams(vmem_limit_bytes=VMEM_LIMIT),
        name=name,
    )(*([grad] if grad is not None else parts), w, m, v)


def adamw_packed(parts, w, m, v, name):
    n_slots = parts.shape[0]

    def body(p_ref, w_ref, m_ref, v_ref, g_ref, d_ref, mo_ref, vo_ref):
        g = p_ref[0]
        for s in range(1, n_slots):
            g = g + p_ref[s]
        delta, m_new, v_new = _adamw_math(g, w_ref[...], m_ref[...], v_ref[...])
        g_ref[...] = g
        d_ref[...] = delta
        mo_ref[...] = m_new
        vo_ref[...] = v_new

    return pl.pallas_call(
        body,
        out_shape=[jax.ShapeDtypeStruct(w.shape, F32)] * 4,
        compiler_params=pltpu.CompilerParams(vmem_limit_bytes=VMEM_LIMIT),
        name=name,
    )(parts, w, m, v)


SHARDED = ("w_in", "w_q_b", "w_kv_b", "w_out", "w_gate", "w_up", "w_down")
TRANSPOSED = ("w_in", "w_q_b", "w_gate", "w_up")
ADAM_TRANSPOSED = ("w_q_b", "w_gate", "w_up")
BEFORE_ATTENTION = ("w_in", "w_q_b", "w_kv_b")
FFN_WEIGHTS = ("w_gate", "w_up", "w_down")
REPLICATED = ("attn_norm_g", "pool_scale", "q_a_norm_g", "kv_a_norm_g", "q_norm_g", "k_norm_g", "ffn_norm_g")


def _pack_flat(arrays, rows):
    flat = jnp.concatenate([a.reshape(-1) for a in arrays])
    return jnp.pad(flat, (0, rows * LANES - flat.shape[0])).reshape(rows, LANES)


def _unpack_flat(packed, shapes):
    flat, out, at = packed.reshape(-1), [], 0
    for shp in shapes:
        n = 1
        for d in shp:
            n *= d
        out.append(flat[at:at + n].reshape(shp))
        at += n
    return out


def _rope_lane_tables(length):
    inv = 1.0 / (ROPE_THETA ** (jnp.arange(0, QK_ROPE_DIM, 2, dtype=F32) / QK_ROPE_DIM))
    ang = jnp.arange(length, dtype=F32)[:, None] * inv[None, :]
    cos, sin = jnp.cos(ang), jnp.sin(ang)
    ones = jnp.ones((length, QK_NOPE_DIM), F32)
    zeros = jnp.zeros((length, QK_NOPE_DIM), F32)
    tail = HEAD_PAD - QK_HEAD_DIM
    cosf = jnp.concatenate([ones, cos, cos, ones[:, :tail]], axis=1)
    sins = jnp.concatenate([zeros, -sin, sin, zeros[:, :tail]], axis=1)
    return cosf, sins


def _pad_lanes(vec, at, width=HEAD_PAD):
    return jnp.pad(vec, (at, width - at - vec.shape[0])).reshape(1, width)


def kernel(x, meta_tokens, attn_norm_g, w_in, w_pool, pool_scale, q_a_norm_g, w_q_b, kv_a_norm_g, w_kv_b, q_norm_g, k_norm_g, w_out, ffn_norm_g, w_gate, w_up, w_down, loss_target, m_meta_tokens, m_attn_norm_g, m_w_in, m_w_pool, m_pool_scale, m_q_a_norm_g, m_w_q_b, m_kv_a_norm_g, m_w_kv_b, m_q_norm_g, m_k_norm_g, m_w_out, m_ffn_norm_g, m_w_gate, m_w_up, m_w_down, v_meta_tokens, v_attn_norm_g, v_w_in, v_w_pool, v_pool_scale, v_q_a_norm_g, v_w_q_b, v_kv_a_norm_g, v_w_kv_b, v_q_norm_g, v_k_norm_g, v_w_out, v_ffn_norm_g, v_w_gate, v_w_up, v_w_down):
    weights = dict(meta_tokens=meta_tokens, attn_norm_g=attn_norm_g, w_in=w_in, w_pool=w_pool, pool_scale=pool_scale,
                   q_a_norm_g=q_a_norm_g, w_q_b=w_q_b, kv_a_norm_g=kv_a_norm_g, w_kv_b=w_kv_b, q_norm_g=q_norm_g,
                   k_norm_g=k_norm_g, w_out=w_out, ffn_norm_g=ffn_norm_g, w_gate=w_gate, w_up=w_up, w_down=w_down)
    mom1 = dict(meta_tokens=m_meta_tokens, attn_norm_g=m_attn_norm_g, w_in=m_w_in, w_pool=m_w_pool,
                pool_scale=m_pool_scale, q_a_norm_g=m_q_a_norm_g, w_q_b=m_w_q_b, kv_a_norm_g=m_kv_a_norm_g,
                w_kv_b=m_w_kv_b, q_norm_g=m_q_norm_g, k_norm_g=m_k_norm_g, w_out=m_w_out, ffn_norm_g=m_ffn_norm_g,
                w_gate=m_w_gate, w_up=m_w_up, w_down=m_w_down)
    mom2 = dict(meta_tokens=v_meta_tokens, attn_norm_g=v_attn_norm_g, w_in=v_w_in, w_pool=v_w_pool,
                pool_scale=v_pool_scale, q_a_norm_g=v_q_a_norm_g, w_q_b=v_w_q_b, kv_a_norm_g=v_kv_a_norm_g,
                w_kv_b=v_w_kv_b, q_norm_g=v_q_norm_g, k_norm_g=v_k_norm_g, w_out=v_w_out, ffn_norm_g=v_ffn_norm_g,
                w_gate=v_w_gate, w_up=v_w_up, w_down=v_w_down)
    order = ("meta_tokens", "attn_norm_g", "w_in", "w_pool", "pool_scale", "q_a_norm_g", "w_q_b", "kv_a_norm_g",
             "w_kv_b", "q_norm_g", "k_norm_g", "w_out", "ffn_norm_g", "w_gate", "w_up", "w_down")
    depth = w_in.shape[0]
    seq = x.shape[1]
    length = N_META + seq
    lp = -(-length // ROW_TILE) * ROW_TILE
    in_cols = w_in.shape[2]

    local = {n: (jnp.swapaxes(weights[n], 1, 2) if n in TRANSPOSED else weights[n]) for n in SHARDED}
    cast = cast_bf16([local[n] for n in SHARDED], "cast_weights")
    shards = [{n: cast[k * depth + l] for k, n in enumerate(SHARDED)} for l in range(depth)]
    gathered = all_gather([shards[0][n] for n in BEFORE_ATTENTION] + [meta_tokens], "all_gather")
    g8l = [dict(zip(BEFORE_ATTENTION, gathered[:-1]))] + [{} for _ in range(depth - 1)]
    meta_full = jnp.transpose(gathered[-1], (1, 0, 2)).reshape(N_META, D_MODEL)

    s1, s2, s3 = POOL_WIDTH, POOL_WIDTH + Q_LORA_RANK, POOL_WIDTH + Q_LORA_RANK + KV_LORA_RANK
    zpad = lambda n: jnp.zeros((1, n, D_MODEL), BF16)

    def padded_in_proj(w8):
        w_in_t = jnp.swapaxes(w8, 0, 1).reshape(1, N_DEV * in_cols, D_MODEL)
        return jnp.concatenate([w_in_t[:, :s3], zpad(QK_NOPE_DIM), w_in_t[:, s3:],
                                zpad(HEAD_PAD - QK_HEAD_DIM)], axis=1)

    w_in_ps = [None] * depth
    w_pool_b = w_pool.astype(BF16)

    cosf, sins = _rope_lane_tables(lp)
    seg = _segment_matrix()
    row = lambda a, l: a[l].reshape(1, -1)

    h = jnp.concatenate([meta_full, x[0], jnp.zeros((lp - length, D_MODEL), F32)], axis=0)
    target = jnp.pad(loss_target[0], ((N_META, lp - length), (0, 0)))
    saved = []
    for l in range(depth):
        gq = _pad_lanes(q_norm_g[l], 0)
        gkn = _pad_lanes(k_norm_g[l, :QK_NOPE_DIM], 0)
        gkr = _pad_lanes(k_norm_g[l, QK_NOPE_DIM:], QK_NOPE_DIM)
        g8 = g8l[l]
        w_in_ps[l] = padded_in_proj(g8["w_in"])
        a, u, c_q, c_kv, kr = norm_mm(
            h, row(attn_norm_g, l), w_in_ps[l], 0,
            [(0, s1), (s1, Q_LORA_RANK), (s2, KV_LORA_RANK), (s3, HEAD_PAD)], [F32] * 4, "in_proj")
        cat = pool_fwd(u, w_pool_b[l], row(pool_scale, l), "pool_fwd")
        qn, q = norm_mm_heads(c_q, row(q_a_norm_g, l), g8["w_q_b"], 0, "q_proj", transposed=True)
        kvn, kv = norm_mm_heads(c_kv, row(kv_a_norm_g, l), g8["w_kv_b"], 0, "kv_proj")
        qp, kp, v = qk_prep_fwd(q, kv, kr, cosf, sins, seg, gq, gkn, gkr, "qk_prep_fwd")
        riders = [(l, n) for n in SHARDED if n not in g8]
        riders += [(l + 1, n) for n in BEFORE_ATTENTION if l + 1 < depth]
        o, lse, cat, *arrived = attn_fwd(qp, kp, v, cat, "attn_fwd_gather", gather=[shards[k][n] for k, n in riders])
        for (k, n), w8 in zip(riders, arrived):
            g8l[k][n] = w8
        h_mid, g = mm_res(cat, g8["w_out"], h, 0, "out_proj", gamma=row(ffn_norm_g, l))
        gate, up, act = ffn_up(g, g8["w_gate"], g8["w_up"], 0, "ffn_up")
        h_next = mm_res(act, g8["w_down"], h_mid, 0, "ffn_down")[0]
        saved.append(dict(h=h, a=a, u=u, c_q=c_q, c_kv=c_kv, kr=kr, qn=qn, q=q, kvn=kvn, kv=kv, v=v, qp=qp, kp=kp,
                          o=o, lse=lse, cat=cat, h_mid=h_mid, g=g, gate=gate, up=up, act=act,
                          gq=gq, gkn=gkn, gkr=gkr))
        h = h_next

    dh, sq = loss_head(h, target, seq, "loss_head")
    loss = lax.psum(0.5 / D_MODEL * jnp.sum(sq), ("x", "y", "c"))

    core = lax.axis_index("c").astype(jnp.int32).reshape(1)
    small_grads = {n: [None] * depth for n in REPLICATED + ("w_pool",)}
    pending = []
    got = [{} for _ in range(depth)]

    def pair_reduce(l, names, slots, small):
        bigs = [slots[n] for n in names]
        if small is None:
            from_sibling, got_small = exchange_pair(bigs, None, "grad_exchange_pair"), None
        else:
            *from_sibling, got_small = exchange_pair(bigs, small, "grad_exchange_pair_small")
        partial = pair_sum(bigs, list(from_sibling), core, "grad_pair_sum")
        return [(l, n, p) for n, p in zip(names, partial)], got_small

    for l in reversed(range(depth)):
        s = saved[l]
        g8 = g8l[l]
        w_in_p = w_in_ps[l]
        slots = {}
        dgate, dup = ffn_bwd_act(dh, g8["w_down"], 0, s["gate"], s["up"], "ffn_bwd_act")
        slot_of = lambda n, full: full.reshape(g8[n].shape[:1] + g8[n].shape[2:])
        slots["w_down"] = slot_of("w_down", mm_tn([s["act"]], dh, "dw_down", out_dtype=BF16)[0])
        dw_gate_t, dw_up_t = mm_tn([dgate, dup], s["g"], "dw_gate_up", out_dtype=BF16)
        slots["w_gate"] = slot_of("w_gate", dw_gate_t)
        slots["w_up"] = slot_of("w_up", dw_up_t)
        pending += pair_reduce(l, FFN_WEIGHTS, slots, None)[0]
        dh_mid, dg_ffn = mm_nt_normbwd([("gathered", dgate, g8["w_gate"]), ("gathered", dup, g8["w_up"])],
                                       s["h_mid"], row(ffn_norm_g, l), dh, 0, "ffn_bwd_in")
        small_grads["ffn_norm_g"][l] = dg_ffn[0]
        slots["w_out"] = slot_of("w_out", mm_tn([s["cat"]], dh_mid, "dw_out", out_dtype=BF16)[0])
        dy_pool, do, delta = out_proj_bwd(dh_mid, g8["w_out"], 0, s["o"], "out_proj_bwd")
        du, dw_pool, dscale = pool_bwd(s["u"], dy_pool, w_pool_b[l], row(pool_scale, l), "pool_bwd")
        small_grads["w_pool"][l] = dw_pool
        small_grads["pool_scale"][l] = dscale[0]
        pool_rider = [jnp.stack(small_grads["w_pool"], axis=0).reshape(-1, LANES)] if l == 0 else []
        dqp, dkp, dv, *arrived = attn_bwd(s["qp"], s["kp"], s["v"], do, s["lse"], delta, "attn_bwd_exchange",
                                          exchange=[p for _, _, p in pending], gather=pool_rider)
        if pool_rider:
            got_pool = arrived.pop()
        for (k, n, _), parts in zip(pending, arrived):
            got[k][n] = parts
        dq, dkv, dkr, dgq, dgkn, dgkr = qk_prep_bwd(dqp, dkp, dv, s["q"], s["kv"], s["kr"], cosf, sins, seg,
                                                    s["gq"], s["gkn"], s["gkr"], "qk_prep_bwd")
        small_grads["q_norm_g"][l] = dgq[0, :QK_HEAD_DIM]
        small_grads["k_norm_g"][l] = jnp.concatenate([dgkn[0, :QK_NOPE_DIM], dgkr[0, QK_NOPE_DIM:QK_HEAD_DIM]])
        heads_first = lambda full: jnp.swapaxes(full.reshape(full.shape[0], MLA_HEADS, HEAD_PAD), 0, 1)
        dw_q_t = mm_tn([dq], s["qn"], "dw_q", out_dtype=BF16)[0]
        slots["w_q_b"] = dw_q_t.reshape(MLA_HEADS, HEAD_PAD, -1)[:, :QK_HEAD_DIM]
        dc_q, dg_qa = mm_nt_normbwd([("heads_t", dq, g8["w_q_b"])], s["c_q"], row(q_a_norm_g, l), None, 0,
                                    "q_proj_bwd")
        small_grads["q_a_norm_g"][l] = dg_qa[0]
        slots["w_kv_b"] = heads_first(mm_tn([s["kvn"]], dkv, "dw_kv", out_dtype=BF16)[0])
        dc_kv, dg_kva = mm_nt_normbwd([("heads", dkv, g8["w_kv_b"])], s["c_kv"], row(kv_a_norm_g, l), None, 0,
                                      "kv_proj_bwd")
        small_grads["kv_a_norm_g"][l] = dg_kva[0]
        dw_pool_t, dw_q_t, dw_kv_t, dw_rope_t = mm_tn([du, dc_q, dc_kv, dkr], s["a"], "dw_in", out_dtype=BF16)
        dw_in_t = jnp.concatenate([dw_pool_t, dw_q_t, dw_kv_t, dw_rope_t[QK_NOPE_DIM:QK_HEAD_DIM]], axis=0)
        slots["w_in"] = slot_of("w_in", dw_in_t)
        dh, dg_attn = mm_nt_normbwd(
            [("rows", du, w_in_p, 0), ("rows", dc_q, w_in_p, s1), ("rows", dc_kv, w_in_p, s2),
             ("rows", dkr, w_in_p, s3)],
            s["h"], row(attn_norm_g, l), dh_mid, 0, "in_proj_bwd")
        small_grads["attn_norm_g"][l] = dg_attn[0]

        small_slots = None
        if l == 0:
            rep_shapes = [weights[n].shape for n in REPLICATED]
            rep_count = sum(int(jnp.size(weights[n])) for n in REPLICATED)
            rep_rows = -(-rep_count // (8 * LANES)) * 8
            rep_packed = _pack_flat([jnp.stack(small_grads[n], axis=0) for n in REPLICATED], rep_rows)
            meta_slots = jnp.transpose(dh[:N_META].reshape(N_META, N_DEV, LANES), (1, 0, 2))
            small_slots = jnp.concatenate(
                [meta_slots, jnp.broadcast_to(rep_packed[None], (N_DEV, rep_rows, LANES))], axis=1)
        pending, got_small = pair_reduce(l, [n for n in SHARDED if n not in FFN_WEIGHTS], slots, small_slots)
    for (k, n, _), parts in zip(pending, exchange_chips([p for _, _, p in pending], "grad_exchange_chips")):
        got[k][n] = parts

    grad_x = dh[N_META:length][None]

    per = [{} for _ in range(4)]
    for n in SHARDED:
        parts = [got[l][n] for l in range(depth)]
        if n in ADAM_TRANSPOSED:
            t = lambda a: jnp.swapaxes(a, 1, 2)
            outs = [t(o) for o in adamw_shard(parts, local[n], t(mom1[n]), t(mom2[n]), "adamw_" + n)]
        elif n in TRANSPOSED:
            grad = jnp.swapaxes(chip_sum(parts, "chip_sum_" + n), 1, 2)
            outs = adamw_shard(None, weights[n], mom1[n], mom2[n], "adamw_" + n, grad=grad)
        else:
            outs = adamw_shard(parts, weights[n], mom1[n], mom2[n], "adamw_" + n)
        for k in range(4):
            per[k][n] = outs[k]
    ps = lambda src: jnp.concatenate(
        [src["meta_tokens"], _pack_flat([src[n] for n in REPLICATED], rep_rows)], axis=0)
    small_out = adamw_packed(got_small, ps(weights), ps(mom1), ps(mom2), "adamw_small")
    as_rows = lambda a: a.reshape(-1, LANES)
    pool_out = adamw_packed(got_pool, as_rows(w_pool), as_rows(m_w_pool), as_rows(v_w_pool), "adamw_w_pool")
    for k in range(4):
        per[k]["meta_tokens"] = small_out[k][:N_META]
        per[k].update(zip(REPLICATED, _unpack_flat(small_out[k][N_META:], rep_shapes)))
        per[k]["w_pool"] = pool_out[k].reshape(w_pool.shape)
    return (loss, grad_x, *[per[0][n] for n in order], *[per[1][n] for n in order],
            *[per[2][n] for n in order], *[per[3][n] for n in order])
```

```python
import functools

import jax
import jax.numpy as jnp
from jax import lax
from jax.experimental import pallas as pl
from jax.experimental.pallas import tpu as pltpu

F32 = jnp.float32
BF16 = jnp.bfloat16

D_MODEL = 1024
N_META = 16
POOL_WIDTH = 512
POOL_WINDOWS = (2, 4, 8, 16)
POOL_GROUP_DIM = 128
POOL_HALO = 16
MLA_HEADS = 8
QK_NOPE_DIM = 64
QK_ROPE_DIM = 32
QK_HEAD_DIM = 96
V_HEAD_DIM = 64
HEAD_PAD = 128
Q_LORA_RANK = 384
KV_LORA_RANK = 256
ROPE_THETA = 10000.0
RMS_EPS = 1e-6
ATTN_SCALE = QK_HEAD_DIM ** -0.5
LOG2_E = 1.4426950408889634
SCORE_SCALE = ATTN_SCALE * LOG2_E

ADAM_LR = 0.001
ADAM_B1 = 0.9
ADAM_B2 = 0.999
ADAM_EPS = 1e-08
ADAM_WD = 0.01
ADAM_STEP = 10

N_DEV = 8
N_CHIPS = 4
LANES = 128
ROW_TILE = 384
LONG_TILE = 1056
VMEM_LIMIT = 56 * 1024 * 1024


def _params(*sem):
    return pltpu.CompilerParams(dimension_semantics=sem, vmem_limit_bytes=VMEM_LIMIT)


def _row_spec(tile, width):
    return pl.BlockSpec((tile, width), lambda i: (i, 0))


def _const_spec(shape):
    return pl.BlockSpec(shape, lambda i: tuple(0 for _ in shape))


def _layer_spec(w, layer):
    return pl.BlockSpec((None,) + w.shape[1:], lambda *_: (layer, 0, 0))


def _gathered_spec(w8, layer):
    return pl.BlockSpec((N_DEV, None) + w8.shape[2:], lambda *_: (0, layer, 0, 0))


def _nt(a, b):
    return lax.dot_general(a, b, (((1,), (1,)), ((), ())), preferred_element_type=F32)


def _tn(a, b):
    return lax.dot_general(a, b, (((0,), (0,)), ((), ())), preferred_element_type=F32)


def _nn(a, b):
    return jnp.dot(a, b, preferred_element_type=F32)


def _silu(g):
    return g * (1.0 / (1.0 + jnp.exp(-g)))


def _rms(xf):
    return lax.rsqrt(jnp.mean(xf * xf, axis=-1, keepdims=True) + RMS_EPS)


def norm_mm(x, gamma, wt, layer, splits, dtypes, name):
    L, K = x.shape
    tm = ROW_TILE

    def body(x_ref, g_ref, w_ref, a_ref, *z_refs):
        xf = x_ref[...]
        a = ((xf * _rms(xf)) * g_ref[...]).astype(BF16)
        a_ref[...] = a
        z = _nt(a, w_ref[...])
        for (s, n), zr in zip(splits, z_refs):
            zr[...] = z[:, s:s + n].astype(zr.dtype)

    widths = [n for _, n in splits]
    return pl.pallas_call(
        body,
        grid=(L // tm,),
        in_specs=[_row_spec(tm, K), _const_spec((1, K)), _layer_spec(wt, layer)],
        out_specs=[_row_spec(tm, K)] + [_row_spec(tm, n) for n in widths],
        out_shape=[jax.ShapeDtypeStruct((L, K), BF16)]
        + [jax.ShapeDtypeStruct((L, n), dt) for n, dt in zip(widths, dtypes)],
        compiler_params=_params("arbitrary"),
        name=name,
    )(x, gamma, wt)


def norm_mm_heads(x, gamma, w8, layer, name, transposed=False):
    L, K = x.shape
    hw = w8.shape[-2] if transposed else w8.shape[-1]
    tm = ROW_TILE

    def body(x_ref, g_ref, w_ref, a_ref, z_ref):
        xf = x_ref[...]
        a = ((xf * _rms(xf)) * g_ref[...]).astype(BF16)
        a_ref[...] = a
        if hw < HEAD_PAD:
            z_ref[...] = jnp.zeros_like(z_ref)
        for j in range(MLA_HEADS):
            z_ref[:, j * HEAD_PAD:j * HEAD_PAD + hw] = _nt(a, w_ref[j]) if transposed else _nn(a, w_ref[j])

    return pl.pallas_call(
        body,
        grid=(L // tm,),
        in_specs=[_row_spec(tm, K), _const_spec((1, K)), _gathered_spec(w8, layer)],
        out_specs=[_row_spec(tm, K), _row_spec(tm, MLA_HEADS * HEAD_PAD)],
        out_shape=[jax.ShapeDtypeStruct((L, K), BF16), jax.ShapeDtypeStruct((L, MLA_HEADS * HEAD_PAD), F32)],
        compiler_params=_params("arbitrary"),
        name=name,
    )(x, gamma, w8)


FF_GROUP = 4


def _ff_spec(w8, layer):
    return pl.BlockSpec((FF_GROUP, None) + w8.shape[2:], lambda j, i: (j, layer, 0, 0))


def ffn_up(g, w_gate8, w_up8, layer, name):
    L, K = g.shape
    fb = w_gate8.shape[-2]
    tm, tf = ROW_TILE, FF_GROUP * fb

    def body(a_ref, wg_ref, wu_ref, gate_ref, up_ref, act_ref):
        a = a_ref[...]
        gate = _nt(a, wg_ref[...].reshape(tf, K))
        up = _nt(a, wu_ref[...].reshape(tf, K))
        gate_ref[...] = gate.astype(BF16)
        up_ref[...] = up.astype(BF16)
        act_ref[...] = (_silu(gate) * up).astype(BF16)

    tile = pl.BlockSpec((tm, tf), lambda j, i: (i, j))
    F = N_DEV * fb
    return pl.pallas_call(
        body,
        grid=(N_DEV // FF_GROUP, L // tm),
        in_specs=[pl.BlockSpec((tm, K), lambda j, i: (i, 0)), _ff_spec(w_gate8, layer), _ff_spec(w_up8, layer)],
        out_specs=[tile, tile, tile],
        out_shape=[
            jax.ShapeDtypeStruct((L, F), BF16),
            jax.ShapeDtypeStruct((L, F), BF16),
            jax.ShapeDtypeStruct((L, F), BF16),
        ],
        compiler_params=_params("arbitrary", "arbitrary"),
        name=name,
    )(g, w_gate8, w_up8)


def mm_res(a, w8, res, layer, name, gamma=None):
    L = a.shape[0]
    kb, N = w8.shape[-2:]
    tm = ROW_TILE
    normed = gamma is not None

    def body(a_ref, w_ref, r_ref, *rest):
        out = r_ref[...] + _nn(a_ref[...], w_ref[...].reshape(N_DEV * kb, N))
        if normed:
            g_ref, o_ref, n_ref = rest
            n_ref[...] = ((out * _rms(out)) * g_ref[...]).astype(BF16)
        else:
            (o_ref,) = rest
        o_ref[...] = out

    in_specs = [_row_spec(tm, N_DEV * kb), _gathered_spec(w8, layer), _row_spec(tm, N)]
    out_specs = [_row_spec(tm, N)]
    out_shape = [jax.ShapeDtypeStruct((L, N), F32)]
    args = [a, w8, res]
    if normed:
        in_specs.append(_const_spec((1, N)))
        out_specs.append(_row_spec(tm, N))
        out_shape.append(jax.ShapeDtypeStruct((L, N), BF16))
        args.append(gamma)
    return pl.pallas_call(
        body,
        grid=(L // tm,),
        in_specs=in_specs,
        out_specs=out_specs,
        out_shape=out_shape,
        compiler_params=_params("arbitrary"),
        name=name,
    )(*args)


def out_proj_bwd(dz, w8, layer, o, name):
    L, N = dz.shape
    kb = w8.shape[-2]
    C = o.shape[1]
    tm = ROW_TILE

    def body(dz_ref, w_ref, o_ref, dy_ref, do_ref, dl_ref):
        d = _nt(dz_ref[...].astype(BF16), w_ref[...].reshape(N_DEV * kb, N))
        dy_ref[...] = d[:, :C]
        do_ref[...] = d[:, C:]
        left = lax.broadcasted_iota(jnp.int32, (tm, LANES), 1) < V_HEAD_DIM
        for p in range(C // LANES):
            cols = slice(p * LANES, (p + 1) * LANES)
            prod = d[:, C + p * LANES:C + (p + 1) * LANES] * o_ref[:, cols]
            d0 = jnp.sum(jnp.where(left, prod, 0.0), axis=1, keepdims=True)
            d1 = jnp.sum(jnp.where(left, 0.0, prod), axis=1, keepdims=True)
            dl_ref[:, cols] = jnp.where(left, d0, d1)

    return pl.pallas_call(
        body,
        grid=(L // tm,),
        in_specs=[_row_spec(tm, N), _gathered_spec(w8, layer), _row_spec(tm, C)],
        out_specs=[_row_spec(tm, C)] * 3,
        out_shape=[jax.ShapeDtypeStruct((L, C), F32)] * 3,
        compiler_params=_params("arbitrary"),
        name=name,
    )(dz, w8, o)


def mm_nt_normbwd(terms, x, gamma, dres, layer, name):
    L, K = x.shape
    tm = ROW_TILE
    n_terms = len(terms)
    has_res = dres is not None
    weights = []
    for t in terms:
        if not any(t[2] is u for u in weights):
            weights.append(t[2])
    which = [[t[2] is u for u in weights].index(True) for t in terms]
    n_in = n_terms + len(weights)

    def body(*refs):
        dz_refs = refs[:n_terms]
        w_refs = [refs[n_terms + n] for n in which]
        x_ref, g_ref = refs[n_in], refs[n_in + 1]
        pos = n_in + 2
        r_ref = refs[pos] if has_res else None
        dx_ref, dg_ref = refs[pos + has_res], refs[pos + has_res + 1]
        da = None
        for t, dz_ref, w_ref in zip(terms, dz_refs, w_refs):
            if t[0] == "rows":
                n, at = t[1].shape[1], t[3]
                parts = [_nn(dz_ref[...].astype(BF16), w_ref[at:at + n, :])]
            elif t[0] == "heads":
                hw = t[2].shape[-1]
                parts = [_nt(dz_ref[:, j * HEAD_PAD:j * HEAD_PAD + hw].astype(BF16), w_ref[j])
                         for j in range(MLA_HEADS)]
            elif t[0] == "heads_t":
                hw = t[2].shape[-2]
                parts = [_nn(dz_ref[:, j * HEAD_PAD:j * HEAD_PAD + hw].astype(BF16), w_ref[j])
                         for j in range(MLA_HEADS)]
            else:
                nb = t[2].shape[-2]
                parts = [_nn(dz_ref[...].astype(BF16), w_ref[...].reshape(N_DEV * nb, K))]
            for p in parts:
                da = p if da is None else da + p
        xf = x_ref[...]
        r = _rms(xf)
        xh = xf * r

        @pl.when(pl.program_id(0) == 0)
        def _():
            dg_ref[...] = jnp.zeros_like(dg_ref)

        dg_ref[...] += jnp.sum(da * xh, axis=0, keepdims=True)
        dxh = da * g_ref[...]
        dx = r * (dxh - xh * jnp.mean(dxh * xh, axis=-1, keepdims=True))
        if has_res:
            dx = dx + r_ref[...]
        dx_ref[...] = dx

    in_specs = [_row_spec(tm, t[1].shape[1]) for t in terms]
    for w in weights:
        in_specs.append(_layer_spec(w, layer) if w.ndim == 3 else _gathered_spec(w, layer))
    in_specs += [_row_spec(tm, K), _const_spec((1, K))]
    args = [t[1] for t in terms] + weights + [x, gamma]
    if has_res:
        in_specs.append(_row_spec(tm, K))
        args.append(dres)
    return pl.pallas_call(
        body,
        grid=(L // tm,),
        in_specs=in_specs,
        out_specs=[_row_spec(tm, K), _const_spec((1, K))],
        out_shape=[jax.ShapeDtypeStruct((L, K), F32), jax.ShapeDtypeStruct((1, K), F32)],
        compiler_params=_params("arbitrary"),
        name=name,
    )(*args)


MAX_OUT_ROWS = 1408


def mm_tn(a_list, b, name, out_dtype=F32):
    n = len(a_list)
    L, N = b.shape
    tks = [MAX_OUT_ROWS if (a.shape[1] > MAX_OUT_ROWS and a.shape[1] % MAX_OUT_ROWS == 0) else a.shape[1]
           for a in a_list]
    blocks = a_list[0].shape[1] // tks[0]
    assert all(a.shape[1] // tk == blocks for a, tk in zip(a_list, tks))
    tl = LONG_TILE if L % LONG_TILE == 0 else ROW_TILE
    n_l = L // tl

    def body(*refs):
        a_refs, b_ref = refs[:n], refs[n]
        o_refs, accs = refs[n + 1:2 * n + 1], refs[2 * n + 1:]
        l = pl.program_id(1)
        bt = b_ref[...].astype(BF16)
        for a_ref, o_ref, acc in zip(a_refs, o_refs, accs):
            @pl.when(l == 0)
            def _(acc=acc):
                acc[...] = jnp.zeros_like(acc)

            acc[...] += _tn(a_ref[...].astype(BF16), bt)

            @pl.when(l == n_l - 1)
            def _(acc=acc, o_ref=o_ref):
                o_ref[...] = acc[...].astype(o_ref.dtype)

    return pl.pallas_call(
        body,
        grid=(blocks, n_l),
        in_specs=[pl.BlockSpec((tl, tk), lambda j, l: (l, j)) for tk in tks]
        + [pl.BlockSpec((tl, N), lambda j, l: (l, 0))],
        out_specs=[pl.BlockSpec((tk, N), lambda j, l: (j, 0)) for tk in tks],
        out_shape=[jax.ShapeDtypeStruct((a.shape[1], N), out_dtype) for a in a_list],
        scratch_shapes=[pltpu.VMEM((tk, N), F32) for tk in tks],
        compiler_params=_params("arbitrary", "arbitrary"),
        name=name,
    )(*a_list, b)


def ffn_bwd_act(dh, w_down8, layer, gate, up, name):
    L, K = dh.shape
    fb = w_down8.shape[-2]
    tm, tf = ROW_TILE, FF_GROUP * fb

    def body(dh_ref, w_ref, gate_ref, up_ref, dgate_ref, dup_ref):
        dact = _nt(dh_ref[...].astype(BF16), w_ref[...].reshape(tf, K))
        g = gate_ref[...].astype(F32)
        sig = 0.5 * jnp.tanh(0.5 * g) + 0.5
        dup_ref[...] = (dact * (g * sig)).astype(BF16)
        dgate_ref[...] = (dact * up_ref[...].astype(F32) * (sig * (1.0 + g * (1.0 - sig)))).astype(BF16)

    tile = pl.BlockSpec((tm, tf), lambda j, i: (i, j))
    F = N_DEV * fb
    return pl.pallas_call(
        body,
        grid=(N_DEV // FF_GROUP, L // tm),
        in_specs=[pl.BlockSpec((tm, K), lambda j, i: (i, 0)), _ff_spec(w_down8, layer), tile, tile],
        out_specs=[tile, tile],
        out_shape=[jax.ShapeDtypeStruct((L, F), BF16), jax.ShapeDtypeStruct((L, F), BF16)],
        compiler_params=_params("arbitrary", "arbitrary"),
        name=name,
    )(dh, w_down8, gate, up)


def _pool_residual(scr, lo, tm, g, w, t):
    cols = slice(g * POOL_GROUP_DIM, (g + 1) * POOL_GROUP_DIM)
    cur = scr[lo:lo + tm, cols]
    s = cur
    for k in range(1, w):
        s = s + scr[lo - k:lo - k + tm, cols]
    cnt = jnp.minimum(t + 1, w).astype(F32)
    return s / cnt - cur


def pool_fwd(u, w_pool, scale, name):
    L, C = u.shape
    tm, halo = ROW_TILE, POOL_HALO

    def body(u_ref, halo_ref, w_ref, s_ref, y_ref, scr):
        i = pl.program_id(0)
        scr[0:halo, :] = jnp.where(i > 0, halo_ref[...], 0.0)
        scr[halo:halo + tm, :] = u_ref[...]
        t = i * tm + lax.broadcasted_iota(jnp.int32, (tm, POOL_GROUP_DIM), 0)
        for g, w in enumerate(POOL_WINDOWS):
            cols = slice(g * POOL_GROUP_DIM, (g + 1) * POOL_GROUP_DIM)
            p = _pool_residual(scr, halo, tm, g, w, t)
            y = _nn(p.astype(BF16), w_ref[g]) * s_ref[:, cols]
            y_ref[:, cols] = y.astype(y_ref.dtype)

    return pl.pallas_call(
        body,
        grid=(L // tm,),
        in_specs=[
            _row_spec(tm, C),
            pl.BlockSpec((halo, C), lambda i: (jnp.maximum(i * (tm // halo) - 1, 0), 0)),
            _const_spec(w_pool.shape),
            _const_spec((1, C)),
        ],
        out_specs=_row_spec(tm, C),
        out_shape=jax.ShapeDtypeStruct((L, 2 * C), BF16),
        scratch_shapes=[pltpu.VMEM((tm + halo, C), F32)],
        compiler_params=_params("arbitrary"),
        name=name,
    )(u, u, w_pool, scale)


def pool_bwd(u, dy, w_pool, scale, name):
    L, C = u.shape
    tm, halo = ROW_TILE, POOL_HALO
    n_tiles = L // tm
    last_halo = L // halo - 1

    def body(u_ref, uh_ref, dy_ref, dyh_ref, w_ref, s_ref, du_ref, dw_ref, ds_ref, scr_u, scr_q):
        i = pl.program_id(0)

        @pl.when(i == 0)
        def _():
            dw_ref[...] = jnp.zeros_like(dw_ref)
            ds_ref[...] = jnp.zeros_like(ds_ref)

        scr_u[0:halo, :] = jnp.where(i > 0, uh_ref[...], 0.0)
        scr_u[halo:halo + tm, :] = u_ref[...]
        t = i * tm + lax.broadcasted_iota(jnp.int32, (tm, POOL_GROUP_DIM), 0)
        th = (i + 1) * tm + lax.broadcasted_iota(jnp.int32, (halo, POOL_GROUP_DIM), 0)
        for g, w in enumerate(POOL_WINDOWS):
            cols = slice(g * POOL_GROUP_DIM, (g + 1) * POOL_GROUP_DIM)
            p = _pool_residual(scr_u, halo, tm, g, w, t).astype(BF16)
            wg = w_ref[g]
            sc = s_ref[:, cols]
            dy = dy_ref[:, cols]
            ds_ref[:, cols] += jnp.sum(dy * _nn(p, wg), axis=0, keepdims=True)
            dys = (dy * sc).astype(BF16)
            dw_ref[g] += _tn(p, dys)
            dp = _nt(dys, wg)
            dyh = jnp.where(i < n_tiles - 1, dyh_ref[:, cols], 0.0)
            dph = _nt((dyh * sc).astype(BF16), wg)
            scr_q[0:tm, cols] = dp / jnp.minimum(t + 1, w).astype(F32)
            scr_q[tm:tm + halo, cols] = dph / jnp.minimum(th + 1, w).astype(F32)
            acc = scr_q[0:tm, cols]
            for k in range(1, w):
                acc = acc + scr_q[k:k + tm, cols]
            du_ref[:, cols] = acc - dp

    return pl.pallas_call(
        body,
        grid=(n_tiles,),
        in_specs=[
            _row_spec(tm, C),
            pl.BlockSpec((halo, C), lambda i: (jnp.maximum(i * (tm // halo) - 1, 0), 0)),
            _row_spec(tm, C),
            pl.BlockSpec((halo, C), lambda i: (jnp.minimum((i + 1) * (tm // halo), last_halo), 0)),
            _const_spec(w_pool.shape),
            _const_spec((1, C)),
        ],
        out_specs=[_row_spec(tm, C), _const_spec(w_pool.shape), _const_spec((1, C))],
        out_shape=[
            jax.ShapeDtypeStruct((L, C), F32),
            jax.ShapeDtypeStruct(w_pool.shape, F32),
            jax.ShapeDtypeStruct((1, C), F32),
        ],
        scratch_shapes=[pltpu.VMEM((tm + halo, C), F32), pltpu.VMEM((tm + halo, C), F32)],
        compiler_params=_params("arbitrary"),
        name=name,
    )(u, u, dy, dy, w_pool, scale)


def _head_masks(rows):
    lane = lax.broadcasted_iota(jnp.int32, (rows, HEAD_PAD), 1)
    return lane, lane < QK_NOPE_DIM, (lane >= QK_NOPE_DIM) & (lane < QK_HEAD_DIM)


def _rope_swap(x, lane):
    half = QK_ROPE_DIM // 2
    swapped = jnp.where(lane < QK_NOPE_DIM + half, pltpu.roll(x, HEAD_PAD - half, 1), pltpu.roll(x, half, 1))
    return jnp.where((lane >= QK_NOPE_DIM) & (lane < QK_HEAD_DIM), swapped, 0.0)


def _seg_mean(v, seg_ref):
    hi = v.astype(BF16)
    lo = (v - hi.astype(F32)).astype(BF16)
    seg = seg_ref[...]
    return _nn(hi, seg) + _nn(lo, seg)


def _segment_matrix():
    lane = jnp.arange(HEAD_PAD)
    seg = jnp.where(lane < QK_NOPE_DIM, 0, jnp.where(lane < QK_HEAD_DIM, 1, 2))
    inv = jnp.where(lane < QK_NOPE_DIM, 1.0 / QK_NOPE_DIM, jnp.where(lane < QK_HEAD_DIM, 1.0 / QK_ROPE_DIM, 0.0))
    return jnp.where(seg[:, None] == seg[None, :], inv[None, :], 0.0).astype(BF16)


def qk_prep_fwd(q, kv, kr, cosf, sins, seg, gq, gkn, gkr, name):
    L = q.shape[0]
    tm = ROW_TILE
    W = MLA_HEADS * HEAD_PAD

    def body(q_ref, kv_ref, kr_ref, c_ref, s_ref, seg_ref, gq_ref, gkn_ref, gkr_ref, qo_ref, ko_ref, vo_ref):
        lane, m_n, _ = _head_masks(tm)
        cosf_, sins_ = c_ref[...], s_ref[...]
        kr_ = kr_ref[...]
        rk = lax.rsqrt(_seg_mean(kr_ * kr_, seg_ref) + RMS_EPS)
        krn = kr_ * rk * gkr_ref[...]
        krf = krn * cosf_ + _rope_swap(krn, lane) * sins_
        for h in range(MLA_HEADS):
            cols = slice(h * HEAD_PAD, (h + 1) * HEAD_PAD)
            qh = q_ref[:, cols]
            qn = qh * lax.rsqrt(_seg_mean(qh * qh, seg_ref) + RMS_EPS) * gq_ref[...]
            qo_ref[:, cols] = (qn * cosf_ + _rope_swap(qn, lane) * sins_).astype(BF16)
            kh = jnp.where(m_n, kv_ref[:, cols], 0.0)
            rn = lax.rsqrt(_seg_mean(kh * kh, seg_ref) + RMS_EPS)
            ko_ref[:, cols] = (kh * rn * gkn_ref[...] + krf).astype(BF16)
        for p in range(MLA_HEADS // 2):
            even = kv_ref[:, 2 * p * HEAD_PAD:(2 * p + 1) * HEAD_PAD]
            odd = kv_ref[:, (2 * p + 1) * HEAD_PAD:(2 * p + 2) * HEAD_PAD]
            pair = jnp.where(m_n, pltpu.roll(even, V_HEAD_DIM, 1), odd)
            vo_ref[:, p * LANES:(p + 1) * LANES] = pair.astype(BF16)

    vec = _const_spec((1, HEAD_PAD))
    return pl.pallas_call(
        body,
        grid=(L // tm,),
        in_specs=[_row_spec(tm, W), _row_spec(tm, W), _row_spec(tm, HEAD_PAD), _row_spec(tm, HEAD_PAD),
                  _row_spec(tm, HEAD_PAD), _const_spec((HEAD_PAD, HEAD_PAD)), vec, vec, vec],
        out_specs=[_row_spec(tm, W), _row_spec(tm, W), _row_spec(tm, W // 2)],
        out_shape=[jax.ShapeDtypeStruct((L, W), BF16), jax.ShapeDtypeStruct((L, W), BF16),
                   jax.ShapeDtypeStruct((L, W // 2), BF16)],
        compiler_params=_params("arbitrary"),
        name=name,
    )(q, kv, kr, cosf, sins, seg, gq, gkn, gkr)


def qk_prep_bwd(dqo, dko, dv, q, kv, kr, cosf, sins, seg, gq, gkn, gkr, name):
    L = q.shape[0]
    tm = ROW_TILE
    W = MLA_HEADS * HEAD_PAD

    def body(dqo_ref, dko_ref, dv_ref, q_ref, kv_ref, kr_ref, c_ref, s_ref, seg_ref, gq_ref, gkn_ref, gkr_ref,
             dq_ref, dkv_ref, dkr_ref, dgq_ref, dgkn_ref, dgkr_ref):
        @pl.when(pl.program_id(0) == 0)
        def _():
            dgq_ref[...] = jnp.zeros_like(dgq_ref)
            dgkn_ref[...] = jnp.zeros_like(dgkn_ref)
            dgkr_ref[...] = jnp.zeros_like(dgkr_ref)

        lane, m_n, _ = _head_masks(tm)
        cosf_, sins_ = c_ref[...], s_ref[...]
        dgq = jnp.zeros((1, HEAD_PAD), F32)
        dgkn = jnp.zeros((1, HEAD_PAD), F32)
        dkrf = jnp.zeros((tm, HEAD_PAD), F32)
        for h in range(MLA_HEADS):
            cols = slice(h * HEAD_PAD, (h + 1) * HEAD_PAD)
            dy = dqo_ref[:, cols]
            dqn = dy * cosf_ + _rope_swap(dy * sins_, lane)
            qh = q_ref[:, cols]
            rinv = lax.rsqrt(_seg_mean(qh * qh, seg_ref) + RMS_EPS)
            xh = qh * rinv
            dgq = dgq + jnp.sum(dqn * xh, axis=0, keepdims=True)
            dxh = dqn * gq_ref[...]
            dq_ref[:, cols] = rinv * (dxh - xh * _seg_mean(dxh * xh, seg_ref))
            dk = dko_ref[:, cols]
            dkrf = dkrf + dk
            kh = jnp.where(m_n, kv_ref[:, cols], 0.0)
            rn = lax.rsqrt(_seg_mean(kh * kh, seg_ref) + RMS_EPS)
            xk = kh * rn
            dgkn = dgkn + jnp.sum(dk * xk, axis=0, keepdims=True)
            dxk = dk * gkn_ref[...]
            dkn = rn * (dxk - xk * _seg_mean(dxk * xk, seg_ref))
            dvp = dv_ref[:, (h // 2) * LANES:(h // 2 + 1) * LANES]
            dvh = pltpu.roll(dvp, V_HEAD_DIM, 1) if h % 2 == 0 else dvp
            dkv_ref[:, cols] = jnp.where(m_n, dkn, dvh)
        kr_ = kr_ref[...]
        rk = lax.rsqrt(_seg_mean(kr_ * kr_, seg_ref) + RMS_EPS)
        xr = kr_ * rk
        dkrn = dkrf * cosf_ + _rope_swap(dkrf * sins_, lane)
        dgkr_ref[...] += jnp.sum(dkrn * xr, axis=0, keepdims=True)
        dxr = dkrn * gkr_ref[...]
        dkr_ref[...] = rk * (dxr - xr * _seg_mean(dxr * xr, seg_ref))
        dgq_ref[...] += dgq
        dgkn_ref[...] += dgkn

    vec = _const_spec((1, HEAD_PAD))
    return pl.pallas_call(
        body,
        grid=(L // tm,),
        in_specs=[_row_spec(tm, W), _row_spec(tm, W), _row_spec(tm, W // 2), _row_spec(tm, W), _row_spec(tm, W),
                  _row_spec(tm, HEAD_PAD), _row_spec(tm, HEAD_PAD), _row_spec(tm, HEAD_PAD),
                  _const_spec((HEAD_PAD, HEAD_PAD)), vec, vec, vec],
        out_specs=[_row_spec(tm, W), _row_spec(tm, W), _row_spec(tm, HEAD_PAD), vec, vec, vec],
        out_shape=[jax.ShapeDtypeStruct((L, W), F32), jax.ShapeDtypeStruct((L, W), F32),
                   jax.ShapeDtypeStruct((L, HEAD_PAD), F32)] + [jax.ShapeDtypeStruct((1, HEAD_PAD), F32)] * 3,
        compiler_params=_params("arbitrary"),
        name=name,
    )(dqo, dko, dv, q, kv, kr, cosf, sins, seg, gq, gkn, gkr)


def attn_fwd(qp, kp, v, cat, name, gather=()):
    L = qp.shape[0]
    tq = ROW_TILE
    n_i = L // tq
    pair_w = 2 * HEAD_PAD
    n_pairs = MLA_HEADS // 2
    n_g = len(gather)

    def body(q_ref, k_ref, v_ref, cat_in, *rest):
        del cat_in
        o_ref, lse_ref, cat_ref = rest[n_g:n_g + 3]
        i = pl.program_id(1)
        if n_g:
            pair = pl.program_id(0)
            start, forward, finish = _gather_steps(rest[:n_g], rest[n_g + 3:2 * n_g + 3], *rest[2 * n_g + 3:])
            pl.when((pair == 0) & (i == 0))(start)
            pl.when((pair == n_pairs // 2) & (i == 0))(forward)
        left = lax.broadcasted_iota(jnp.int32, (tq, LANES), 1) < V_HEAD_DIM
        row = lax.broadcasted_iota(jnp.int32, (tq, tq), 0)
        col = lax.broadcasted_iota(jnp.int32, (tq, tq), 1)

        def step(t, carry, masked, width=1, first=0):
            rows = pl.ds(pl.multiple_of((first + t * width) * tq, tq), tq * width)
            vv = v_ref[rows, :]
            out = []
            for hh in range(2):
                cols = slice(hh * HEAD_PAD, (hh + 1) * HEAD_PAD)
                m, l, acc = carry[hh]
                s = _nt(q_ref[:, cols], k_ref[rows, cols]) * SCORE_SCALE
                if masked:
                    s = jnp.where(col <= row, s, -jnp.inf)
                m_new = jnp.maximum(m, jnp.max(s, axis=1, keepdims=True))
                alpha = jnp.exp2(m - m_new)
                p = jnp.exp2(s - m_new)
                l = alpha * l + jnp.sum(p, axis=1, keepdims=True)
                acc = alpha * acc + _nn(p.astype(BF16), vv)
                out.append((m_new, l, acc))
            return tuple(out)

        one = (jnp.full((tq, 1), -jnp.inf, F32), jnp.zeros((tq, 1), F32), jnp.zeros((tq, LANES), F32))
        triples = lax.div(i, 3)
        doubles = jnp.right_shift(i - 3 * triples, 1)
        carry = lax.fori_loop(0, triples, functools.partial(step, masked=False, width=3), (one, one))
        carry = lax.fori_loop(0, doubles, functools.partial(step, masked=False, width=2, first=3 * triples), carry)
        carry = lax.fori_loop(3 * triples + 2 * doubles, i, functools.partial(step, masked=False), carry)
        (m0, l0, a0), (m1, l1, a1) = step(i, carry, True)
        o = jnp.where(left, a0 / l0, a1 / l1)
        o_ref[...] = o
        cat_ref[...] = o.astype(BF16)
        lse_ref[...] = jnp.where(left, m0 + jnp.log2(l0), m1 + jnp.log2(l1))
        if n_g:
            pl.when((pair == n_pairs - 1) & (i == n_i - 1))(finish)

    return pl.pallas_call(
        body,
        grid=(n_pairs, n_i),
        in_specs=[
            pl.BlockSpec((tq, pair_w), lambda p, i: (i, p)),
            pl.BlockSpec((L, pair_w), lambda p, i: (0, p)),
            pl.BlockSpec((L, LANES), lambda p, i: (0, p)),
            _ANY,
        ] + [_ANY] * n_g,
        out_specs=[
            pl.BlockSpec((tq, LANES), lambda p, i: (i, p)),
            pl.BlockSpec((tq, LANES), lambda p, i: (i, p)),
            pl.BlockSpec((tq, LANES), lambda p, i: (i, n_pairs + p)),
        ] + [_ANY] * n_g,
        out_shape=[jax.ShapeDtypeStruct((L, n_pairs * LANES), F32), jax.ShapeDtypeStruct((L, n_pairs * LANES), F32),
                   jax.ShapeDtypeStruct(cat.shape, cat.dtype)]
        + [jax.ShapeDtypeStruct((N_DEV,) + g.shape, g.dtype) for g in gather],
        scratch_shapes=_gather_scratch(n_g) if n_g else [],
        input_output_aliases={3: 2},
        compiler_params=_params("arbitrary", "arbitrary"),
        name=name,
    )(qp, kp, v, cat, *gather)


def attn_bwd(qp, kp, v, do, lse, delta, name, exchange=(), gather=()):
    L = qp.shape[0]
    tq = ROW_TILE
    n_q = L // tq
    pair_w = 2 * HEAD_PAD
    n_pairs = MLA_HEADS // 2
    n_x, n_g = len(exchange), len(gather)
    n_r = n_x + n_g

    def body(q_ref, k_ref, v_ref, do_ref, lse_ref, dl_ref, *rest):
        dq_ref, dk_ref, dv_ref = rest[n_r:n_r + 3]
        riders_in, riders_out, sems = rest[:n_r], rest[n_r + 3:2 * n_r + 3], rest[2 * n_r + 3:]
        j = pl.program_id(1)
        pair = pl.program_id(0)
        steps = []
        if n_x:
            steps.append(_chip_exchange_steps(riders_in[:n_x], riders_out[:n_x], *sems[:3]))
        if n_g:
            steps.append(_direct_gather_steps(riders_in[n_x:], riders_out[n_x:], *sems[-3:]))
        for start, _ in steps:
            pl.when((pair == 0) & (j == 0))(start)

        @pl.when(j == 0)
        def _():
            dq_ref[...] = jnp.zeros_like(dq_ref)

        dk_ref[...] = jnp.zeros_like(dk_ref)
        dv_ref[...] = jnp.zeros_like(dv_ref)
        row = lax.broadcasted_iota(jnp.int32, (tq, tq), 0)
        col = lax.broadcasted_iota(jnp.int32, (tq, tq), 1)

        def step(t, carry, masked, width=1, first=0):
            rows = pl.ds(pl.multiple_of((first + t * width) * tq, tq), tq * width)
            do = do_ref[rows, :]
            left = lax.broadcasted_iota(jnp.int32, do.shape, 1) < V_HEAD_DIM
            vv = v_ref[...]
            dv = None
            for hh in range(2):
                cols = slice(hh * HEAD_PAD, (hh + 1) * HEAD_PAD)
                stat = slice(hh * V_HEAD_DIM, hh * V_HEAD_DIM + 1)
                q = q_ref[rows, cols]
                k = k_ref[:, cols]
                dom = jnp.where(left if hh == 0 else jnp.logical_not(left), do, 0.0).astype(BF16)
                s = _nt(q, k) * SCORE_SCALE
                p = jnp.exp2(s - lse_ref[rows, stat])
                if masked:
                    p = jnp.where(col <= row, p, 0.0)
                dp = _nt(dom, vv)
                ds = (p * (dp - dl_ref[rows, stat]) * ATTN_SCALE).astype(BF16)
                dq_ref[rows, cols] += _nn(ds, k)
                dk_ref[:, cols] += _tn(ds, q)
                t = _tn(p.astype(BF16), dom)
                dv = t if dv is None else dv + t
            dv_ref[...] += dv
            return carry

        step(j, 0, True)
        doubles = jnp.right_shift(n_q - 1 - j, 1)
        lax.fori_loop(0, doubles, functools.partial(step, masked=False, width=2, first=j + 1), 0)
        lax.fori_loop(j + 1 + 2 * doubles, n_q, functools.partial(step, masked=False), 0)
        for _, finish in steps:
            pl.when((pair == n_pairs - 1) & (j == n_q - 1))(finish)

    return pl.pallas_call(
        body,
        grid=(n_pairs, n_q),
        in_specs=[
            pl.BlockSpec((L, pair_w), lambda p, j: (0, p)),
            pl.BlockSpec((tq, pair_w), lambda p, j: (j, p)),
            pl.BlockSpec((tq, LANES), lambda p, j: (j, p)),
            pl.BlockSpec((L, LANES), lambda p, j: (0, p)),
            pl.BlockSpec((L, LANES), lambda p, j: (0, p)),
            pl.BlockSpec((L, LANES), lambda p, j: (0, p)),
        ] + [_ANY] * n_r,
        out_specs=[
            pl.BlockSpec((L, pair_w), lambda p, j: (0, p)),
            pl.BlockSpec((tq, pair_w), lambda p, j: (j, p)),
            pl.BlockSpec((tq, LANES), lambda p, j: (j, p)),
        ] + [_ANY] * n_r,
        out_shape=[
            jax.ShapeDtypeStruct((L, n_pairs * pair_w), F32),
            jax.ShapeDtypeStruct((L, n_pairs * pair_w), F32),
            jax.ShapeDtypeStruct((L, n_pairs * LANES), F32),
        ] + [jax.ShapeDtypeStruct(e.shape, e.dtype) for e in exchange]
        + [jax.ShapeDtypeStruct((N_DEV,) + g.shape, g.dtype) for g in gather],
        scratch_shapes=(_chip_exchange_scratch(n_x) if n_x else []) + (_direct_gather_scratch(n_g) if n_g else []),
        compiler_params=_params("arbitrary", "arbitrary"),
        name=name,
    )(qp, kp, v, do, lse, delta, *exchange, *gather)


def loss_head(h, target, n_real, name):
    L, D = h.shape
    tm = ROW_TILE

    def body(h_ref, t_ref, dh_ref, sq_ref):
        i = pl.program_id(0)

        @pl.when(i == 0)
        def _():
            sq_ref[...] = jnp.zeros_like(sq_ref)

        t = i * tm + lax.broadcasted_iota(jnp.int32, (tm, D), 0)
        real = (t >= N_META) & (t < N_META + n_real)
        diff = jnp.where(real, h_ref[...] - t_ref[...], 0.0)
        dh_ref[...] = diff * (1.0 / D)
        sq_ref[...] += jnp.sum(diff * diff, axis=0, keepdims=True)

    return pl.pallas_call(
        body,
        grid=(L // tm,),
        in_specs=[_row_spec(tm, D), _row_spec(tm, D)],
        out_specs=[_row_spec(tm, D), _const_spec((1, D))],
        out_shape=[jax.ShapeDtypeStruct((L, D), F32), jax.ShapeDtypeStruct((1, D), F32)],
        compiler_params=_params("arbitrary"),
        name=name,
    )(h, target)


def _mesh_position():
    x, y, c = lax.axis_index("x"), lax.axis_index("y"), lax.axis_index("c")
    return x, y, c, 4 * x + 2 * y + c


def _flip(x, y, c, k):
    px = 1 - x if k & 4 else x
    py = 1 - y if k & 2 else y
    pc = 1 - c if k & 1 else c
    return (px, py, pc), 4 * px + 2 * py + pc


def _other_chips(x, y):
    return [(1 - x, y), (x, 1 - y), (1 - x, 1 - y)]


def _dev_index(px, py, pc):
    return 4 * px + 2 * py + pc


_ANY = pl.BlockSpec(memory_space=pl.ANY)


def cast_bf16(arrays, name):
    n = len(arrays)
    depth = arrays[0].shape[0]

    def body(*refs):
        for k, src in enumerate(refs[:n]):
            for l in range(depth):
                refs[n + k * depth + l][0] = src[l].astype(BF16)

    return pl.pallas_call(
        body,
        out_shape=[jax.ShapeDtypeStruct((1,) + a.shape[1:], BF16) for a in arrays for _ in range(depth)],
        compiler_params=pltpu.CompilerParams(vmem_limit_bytes=VMEM_LIMIT),
        name=name,
    )(*arrays)


def _gather_steps(srcs, dsts, send_sems, recv_sems, local_sems):
    n = len(srcs)
    x, y, c, me = _mesh_position()
    sibling = (x, y, 1 - c)
    chips = _other_chips(x, y)

    def copy(k, b, block, to, from_src=False):
        dst = dsts[b].at[block]
        return pltpu.make_async_remote_copy(
            src_ref=srcs[b] if from_src else dst, dst_ref=dst, send_sem=send_sems.at[k * n + b],
            recv_sem=recv_sems.at[k * n + b], device_id=to, device_id_type=pl.DeviceIdType.MESH)

    def own():
        return [pltpu.make_async_copy(srcs[b], dsts[b].at[me], local_sems.at[b]) for b in range(n)]

    def first():
        out = []
        for b in range(n):
            out.append(copy(0, b, me, sibling, from_src=True))
            out += [copy(1 + q, b, me, (*chip, c), from_src=True) for q, chip in enumerate(chips)]
        return out

    def passed():
        return [copy(4 + q, b, _dev_index(*chip, c), sibling) for q, chip in enumerate(chips) for b in range(n)]

    def start():
        for cp in own() + first():
            cp.start()

    def forward():
        for q, chip in enumerate(chips):
            for b in range(n):
                copy(1 + q, b, _dev_index(*chip, c), sibling).wait_recv()
        for cp in passed():
            cp.start()

    def finish():
        for b in range(n):
            copy(0, b, _dev_index(x, y, 1 - c), sibling).wait_recv()
            for q, chip in enumerate(chips):
                copy(4 + q, b, _dev_index(*chip, 1 - c), sibling).wait_recv()
        for cp in first() + passed():
            cp.wait_send()
        for cp in own():
            cp.wait()

    return start, forward, finish


def _gather_scratch(n):
    copies = N_DEV - 1
    return [pltpu.SemaphoreType.DMA((copies * n,)), pltpu.SemaphoreType.DMA((copies * n,)),
            pltpu.SemaphoreType.DMA((n,))]


def all_gather(payloads, name):
    n = len(payloads)

    def body(*refs):
        start, forward, finish = _gather_steps(refs[:n], refs[n:2 * n], *refs[2 * n:])
        start()
        forward()
        finish()

    return pl.pallas_call(
        body,
        in_specs=[_ANY] * n,
        out_specs=[_ANY] * n,
        out_shape=[jax.ShapeDtypeStruct((N_DEV,) + p.shape, p.dtype) for p in payloads],
        scratch_shapes=_gather_scratch(n),
        name=name,
    )(*payloads)


def exchange_pair(bigs, small, name):
    n = len(bigs)
    has_small = small is not None
    n_big = N_CHIPS * n
    n_sems = n_big + (N_DEV - 1 if has_small else 0)

    def body(*refs):
        big_refs = refs[:n]
        pos = n + has_small
        sib_refs = refs[pos:pos + n]
        send_sems, recv_sems, local_sem = refs[-3:]
        x, y, c, me = _mesh_position()
        sibling = (x, y, 1 - c)
        copies = []
        for b in range(n):
            for q in range(N_CHIPS):
                copies.append(pltpu.make_async_remote_copy(
                    src_ref=big_refs[b].at[_dev_index(q // 2, q % 2, 1 - c)], dst_ref=sib_refs[b].at[q],
                    send_sem=send_sems.at[b * N_CHIPS + q], recv_sem=recv_sems.at[b * N_CHIPS + q],
                    device_id=sibling, device_id_type=pl.DeviceIdType.MESH))
        waits = list(copies)
        if has_small:
            small_ref, gsmall_ref = refs[n], refs[pos + n]
            own = pltpu.make_async_copy(small_ref.at[me], gsmall_ref.at[me], local_sem.at[0])
            own.start()
            for k in range(1, N_DEV):
                peer, peer_idx = _flip(x, y, c, k)
                s = n_big + k - 1
                copies.append(pltpu.make_async_remote_copy(
                    src_ref=small_ref.at[peer_idx], dst_ref=gsmall_ref.at[me], send_sem=send_sems.at[s],
                    recv_sem=recv_sems.at[s], device_id=peer, device_id_type=pl.DeviceIdType.MESH))
                waits.append(pltpu.make_async_remote_copy(
                    src_ref=small_ref.at[peer_idx], dst_ref=gsmall_ref.at[peer_idx], send_sem=send_sems.at[s],
                    recv_sem=recv_sems.at[s], device_id=peer, device_id_type=pl.DeviceIdType.MESH))
        for cp in copies:
            cp.start()
        for cp in waits:
            cp.wait_recv()
        for cp in copies:
            cp.wait_send()
        if has_small:
            own.wait()

    out_shape = [jax.ShapeDtypeStruct((N_CHIPS,) + b.shape[1:], b.dtype) for b in bigs]
    args = list(bigs)
    if has_small:
        out_shape.append(jax.ShapeDtypeStruct(small.shape, small.dtype))
        args.append(small)
    return pl.pallas_call(
        body,
        in_specs=[_ANY] * len(args),
        out_specs=[_ANY] * len(out_shape),
        out_shape=out_shape,
        scratch_shapes=[pltpu.SemaphoreType.DMA((n_sems,)), pltpu.SemaphoreType.DMA((n_sems,)),
                        pltpu.SemaphoreType.DMA((1,))],
        name=name,
    )(*args)


def pair_sum(bigs, from_sibling, core, name):
    n = len(bigs)

    def body(core_ref, *refs):
        del core_ref
        for mine, sib, out in zip(refs[:n], refs[n:2 * n], refs[2 * n:]):
            out[...] = (mine[...].astype(F32) + sib[...].astype(F32)).astype(out.dtype)

    def slot(shape, picked):
        zeros = (0,) * (len(shape) - 1)
        if picked:
            return pl.BlockSpec((None,) + shape[1:], lambda q, core_ref: (2 * q + core_ref[0],) + zeros)
        return pl.BlockSpec((None,) + shape[1:], lambda q, core_ref: (q,) + zeros)

    grid_spec = pltpu.PrefetchScalarGridSpec(
        num_scalar_prefetch=1,
        grid=(N_CHIPS,),
        in_specs=[slot(b.shape, True) for b in bigs] + [slot(s.shape, False) for s in from_sibling],
        out_specs=[slot(s.shape, False) for s in from_sibling],
    )
    return pl.pallas_call(
        body,
        grid_spec=grid_spec,
        out_shape=[jax.ShapeDtypeStruct(s.shape, s.dtype) for s in from_sibling],
        compiler_params=_params("arbitrary"),
        name=name,
    )(core, *bigs, *from_sibling)


def _chip_exchange_steps(part_refs, got_refs, send_sems, recv_sems, local_sems):
    n = len(part_refs)
    x, y, c, me = _mesh_position()
    mine = 2 * x + y

    def copies(landing):
        out = []
        for q, (px, py) in enumerate(_other_chips(x, y)):
            theirs = 2 * px + py
            for b in range(n):
                out.append(pltpu.make_async_remote_copy(
                    src_ref=part_refs[b].at[theirs], dst_ref=got_refs[b].at[theirs if landing else mine],
                    send_sem=send_sems.at[q * n + b], recv_sem=recv_sems.at[q * n + b],
                    device_id=(px, py, c), device_id_type=pl.DeviceIdType.MESH))
        return out

    def own():
        return [pltpu.make_async_copy(part_refs[b].at[mine], got_refs[b].at[mine], local_sems.at[b])
                for b in range(n)]

    def start():
        for cp in own() + copies(False):
            cp.start()

    def finish():
        for cp in copies(True):
            cp.wait_recv()
        for cp in copies(False):
            cp.wait_send()
        for cp in own():
            cp.wait()

    return start, finish


def _chip_exchange_scratch(n):
    return [pltpu.SemaphoreType.DMA(((N_CHIPS - 1) * n,)), pltpu.SemaphoreType.DMA(((N_CHIPS - 1) * n,)),
            pltpu.SemaphoreType.DMA((n,))]


def _direct_gather_steps(srcs, dsts, send_sems, recv_sems, local_sems):
    n = len(srcs)
    x, y, c, me = _mesh_position()

    def copies(landing):
        out = []
        for k in range(1, N_DEV):
            peer, peer_idx = _flip(x, y, c, k)
            for b in range(n):
                out.append(pltpu.make_async_remote_copy(
                    src_ref=srcs[b], dst_ref=dsts[b].at[peer_idx if landing else me],
                    send_sem=send_sems.at[(k - 1) * n + b], recv_sem=recv_sems.at[(k - 1) * n + b],
                    device_id=peer, device_id_type=pl.DeviceIdType.MESH))
        return out

    def own():
        return [pltpu.make_async_copy(srcs[b], dsts[b].at[me], local_sems.at[b]) for b in range(n)]

    def start():
        for cp in own() + copies(False):
            cp.start()

    def finish():
        for cp in copies(True):
            cp.wait_recv()
        for cp in copies(False):
            cp.wait_send()
        for cp in own():
            cp.wait()

    return start, finish


def _direct_gather_scratch(n):
    return [pltpu.SemaphoreType.DMA(((N_DEV - 1) * n,)), pltpu.SemaphoreType.DMA(((N_DEV - 1) * n,)),
            pltpu.SemaphoreType.DMA((n,))]


def exchange_chips(partials, name):
    n = len(partials)

    def body(*refs):
        start, finish = _chip_exchange_steps(refs[:n], refs[n:2 * n], *refs[2 * n:])
        start()
        finish()

    return pl.pallas_call(
        body,
        in_specs=[_ANY] * n,
        out_specs=[_ANY] * n,
        out_shape=[jax.ShapeDtypeStruct(p.shape, p.dtype) for p in partials],
        scratch_shapes=_chip_exchange_scratch(n),
        name=name,
    )(*partials)


def _adamw_math(g, w, m, v):
    m_new = ADAM_B1 * m + (1.0 - ADAM_B1) * g
    v_new = ADAM_B2 * v + (1.0 - ADAM_B2) * (g * g)
    m_hat = m_new / (1.0 - ADAM_B1 ** ADAM_STEP)
    v_hat = v_new / (1.0 - ADAM_B2 ** ADAM_STEP)
    delta = -ADAM_LR * (m_hat / (jnp.sqrt(v_hat) + ADAM_EPS) + ADAM_WD * w)
    return delta, m_new, v_new


def _chip_total(p_ref):
    g = p_ref[0].astype(F32)
    for q in range(1, N_CHIPS):
        g = g + p_ref[q].astype(F32)
    return g


def chip_sum(parts, name):
    depth = len(parts)

    def body(*refs):
        for l in range(depth):
            refs[depth][l] = _chip_total(refs[l])

    return pl.pallas_call(
        body,
        out_shape=jax.ShapeDtypeStruct((depth,) + parts[0].shape[1:], F32),
        compiler_params=pltpu.CompilerParams(vmem_limit_bytes=VMEM_LIMIT),
        name=name,
    )(*parts)


def adamw_shard(parts, w, m, v, name, grad=None):
    depth = w.shape[0]
    n_in = 1 if grad is not None else depth

    def body(*refs):
        w_ref, m_ref, v_ref, g_ref, d_ref, mo_ref, vo_ref = refs[n_in:]
        for l in range(depth):
            g = refs[0][l] if grad is not None else _chip_total(refs[l])
            delta, m_new, v_new = _adamw_math(g, w_ref[l], m_ref[l], v_ref[l])
            g_ref[l] = g
            d_ref[l] = delta
            mo_ref[l] = m_new
            vo_ref[l] = v_new

    return pl.pallas_call(
        body,
        out_shape=[jax.ShapeDtypeStruct(w.shape, F32)] * 4,
        compiler_params=pltpu.CompilerParams(vmem_limit_bytes=VMEM_LIMIT),
        name=name,
    )(*([grad] if grad is not None else parts), w, m, v)


def adamw_packed(parts, w, m, v, name):
    n_slots = parts.shape[0]

    def body(p_ref, w_ref, m_ref, v_ref, g_ref, d_ref, mo_ref, vo_ref):
        g = p_ref[0]
        for s in range(1, n_slots):
            g = g + p_ref[s]
        delta, m_new, v_new = _adamw_math(g, w_ref[...], m_ref[...], v_ref[...])
        g_ref[...] = g
        d_ref[...] = delta
        mo_ref[...] = m_new
        vo_ref[...] = v_new

    return pl.pallas_call(
        body,
        out_shape=[jax.ShapeDtypeStruct(w.shape, F32)] * 4,
        compiler_params=pltpu.CompilerParams(vmem_limit_bytes=VMEM_LIMIT),
        name=name,
    )(parts, w, m, v)


SHARDED = ("w_in", "w_q_b", "w_kv_b", "w_out", "w_gate", "w_up", "w_down")
TRANSPOSED = ("w_in", "w_q_b", "w_gate", "w_up")
ADAM_TRANSPOSED = ("w_q_b", "w_gate", "w_up")
BEFORE_ATTENTION = ("w_in", "w_q_b", "w_kv_b")
BEFORE_ATTENTION_BWD = ("w_out", "w_gate", "w_up", "w_down")
REPLICATED = ("attn_norm_g", "pool_scale", "q_a_norm_g", "kv_a_norm_g", "q_norm_g", "k_norm_g", "ffn_norm_g")


def _pack_flat(arrays, rows):
    flat = jnp.concatenate([a.reshape(-1) for a in arrays])
    return jnp.pad(flat, (0, rows * LANES - flat.shape[0])).reshape(rows, LANES)


def _unpack_flat(packed, shapes):
    flat, out, at = packed.reshape(-1), [], 0
    for shp in shapes:
        n = 1
        for d in shp:
            n *= d
        out.append(flat[at:at + n].reshape(shp))
        at += n
    return out


def _rope_lane_tables(length):
    inv = 1.0 / (ROPE_THETA ** (jnp.arange(0, QK_ROPE_DIM, 2, dtype=F32) / QK_ROPE_DIM))
    ang = jnp.arange(length, dtype=F32)[:, None] * inv[None, :]
    cos, sin = jnp.cos(ang), jnp.sin(ang)
    ones = jnp.ones((length, QK_NOPE_DIM), F32)
    zeros = jnp.zeros((length, QK_NOPE_DIM), F32)
    tail = HEAD_PAD - QK_HEAD_DIM
    cosf = jnp.concatenate([ones, cos, cos, ones[:, :tail]], axis=1)
    sins = jnp.concatenate([zeros, -sin, sin, zeros[:, :tail]], axis=1)
    return cosf, sins


def _pad_lanes(vec, at, width=HEAD_PAD):
    return jnp.pad(vec, (at, width - at - vec.shape[0])).reshape(1, width)


def kernel(x, meta_tokens, attn_norm_g, w_in, w_pool, pool_scale, q_a_norm_g, w_q_b, kv_a_norm_g, w_kv_b, q_norm_g, k_norm_g, w_out, ffn_norm_g, w_gate, w_up, w_down, loss_target, m_meta_tokens, m_attn_norm_g, m_w_in, m_w_pool, m_pool_scale, m_q_a_norm_g, m_w_q_b, m_kv_a_norm_g, m_w_kv_b, m_q_norm_g, m_k_norm_g, m_w_out, m_ffn_norm_g, m_w_gate, m_w_up, m_w_down, v_meta_tokens, v_attn_norm_g, v_w_in, v_w_pool, v_pool_scale, v_q_a_norm_g, v_w_q_b, v_kv_a_norm_g, v_w_kv_b, v_q_norm_g, v_k_norm_g, v_w_out, v_ffn_norm_g, v_w_gate, v_w_up, v_w_down):
    weights = dict(meta_tokens=meta_tokens, attn_norm_g=attn_norm_g, w_in=w_in, w_pool=w_pool, pool_scale=pool_scale,
                   q_a_norm_g=q_a_norm_g, w_q_b=w_q_b, kv_a_norm_g=kv_a_norm_g, w_kv_b=w_kv_b, q_norm_g=q_norm_g,
                   k_norm_g=k_norm_g, w_out=w_out, ffn_norm_g=ffn_norm_g, w_gate=w_gate, w_up=w_up, w_down=w_down)
    mom1 = dict(meta_tokens=m_meta_tokens, attn_norm_g=m_attn_norm_g, w_in=m_w_in, w_pool=m_w_pool,
                pool_scale=m_pool_scale, q_a_norm_g=m_q_a_norm_g, w_q_b=m_w_q_b, kv_a_norm_g=m_kv_a_norm_g,
                w_kv_b=m_w_kv_b, q_norm_g=m_q_norm_g, k_norm_g=m_k_norm_g, w_out=m_w_out, ffn_norm_g=m_ffn_norm_g,
                w_gate=m_w_gate, w_up=m_w_up, w_down=m_w_down)
    mom2 = dict(meta_tokens=v_meta_tokens, attn_norm_g=v_attn_norm_g, w_in=v_w_in, w_pool=v_w_pool,
                pool_scale=v_pool_scale, q_a_norm_g=v_q_a_norm_g, w_q_b=v_w_q_b, kv_a_norm_g=v_kv_a_norm_g,
                w_kv_b=v_w_kv_b, q_norm_g=v_q_norm_g, k_norm_g=v_k_norm_g, w_out=v_w_out, ffn_norm_g=v_ffn_norm_g,
                w_gate=v_w_gate, w_up=v_w_up, w_down=v_w_down)
    order = ("meta_tokens", "attn_norm_g", "w_in", "w_pool", "pool_scale", "q_a_norm_g", "w_q_b", "kv_a_norm_g",
             "w_kv_b", "q_norm_g", "k_norm_g", "w_out", "ffn_norm_g", "w_gate", "w_up", "w_down")
    depth = w_in.shape[0]
    seq = x.shape[1]
    length = N_META + seq
    lp = -(-length // ROW_TILE) * ROW_TILE
    in_cols = w_in.shape[2]

    local = {n: (jnp.swapaxes(weights[n], 1, 2) if n in TRANSPOSED else weights[n]) for n in SHARDED}
    cast = cast_bf16([local[n] for n in SHARDED], "cast_weights")
    shards = [{n: cast[k * depth + l] for k, n in enumerate(SHARDED)} for l in range(depth)]
    gathered = all_gather([shards[0][n] for n in BEFORE_ATTENTION] + [meta_tokens], "all_gather")
    g8l = [dict(zip(BEFORE_ATTENTION, gathered[:-1]))] + [{} for _ in range(depth - 1)]
    meta_full = jnp.transpose(gathered[-1], (1, 0, 2)).reshape(N_META, D_MODEL)

    s1, s2, s3 = POOL_WIDTH, POOL_WIDTH + Q_LORA_RANK, POOL_WIDTH + Q_LORA_RANK + KV_LORA_RANK
    zpad = lambda n: jnp.zeros((1, n, D_MODEL), BF16)

    def padded_in_proj(w8):
        w_in_t = jnp.swapaxes(w8, 0, 1).reshape(1, N_DEV * in_cols, D_MODEL)
        return jnp.concatenate([w_in_t[:, :s3], zpad(QK_NOPE_DIM), w_in_t[:, s3:],
                                zpad(HEAD_PAD - QK_HEAD_DIM)], axis=1)

    w_in_ps = [None] * depth
    w_pool_b = w_pool.astype(BF16)

    cosf, sins = _rope_lane_tables(lp)
    seg = _segment_matrix()
    row = lambda a, l: a[l].reshape(1, -1)

    h = jnp.concatenate([meta_full, x[0], jnp.zeros((lp - length, D_MODEL), F32)], axis=0)
    target = jnp.pad(loss_target[0], ((N_META, lp - length), (0, 0)))
    saved = []
    for l in range(depth):
        gq = _pad_lanes(q_norm_g[l], 0)
        gkn = _pad_lanes(k_norm_g[l, :QK_NOPE_DIM], 0)
        gkr = _pad_lanes(k_norm_g[l, QK_NOPE_DIM:], QK_NOPE_DIM)
        g8 = g8l[l]
        w_in_ps[l] = padded_in_proj(g8["w_in"])
        a, u, c_q, c_kv, kr = norm_mm(
            h, row(attn_norm_g, l), w_in_ps[l], 0,
            [(0, s1), (s1, Q_LORA_RANK), (s2, KV_LORA_RANK), (s3, HEAD_PAD)], [F32] * 4, "in_proj")
        cat = pool_fwd(u, w_pool_b[l], row(pool_scale, l), "pool_fwd")
        qn, q = norm_mm_heads(c_q, row(q_a_norm_g, l), g8["w_q_b"], 0, "q_proj", transposed=True)
        kvn, kv = norm_mm_heads(c_kv, row(kv_a_norm_g, l), g8["w_kv_b"], 0, "kv_proj")
        qp, kp, v = qk_prep_fwd(q, kv, kr, cosf, sins, seg, gq, gkn, gkr, "qk_prep_fwd")
        riders = [(l, n) for n in SHARDED if n not in g8]
        riders += [(l + 1, n) for n in BEFORE_ATTENTION if l + 1 < depth]
        o, lse, cat, *arrived = attn_fwd(qp, kp, v, cat, "attn_fwd_gather", gather=[shards[k][n] for k, n in riders])
        for (k, n), w8 in zip(riders, arrived):
            g8l[k][n] = w8
        h_mid, g = mm_res(cat, g8["w_out"], h, 0, "out_proj", gamma=row(ffn_norm_g, l))
        gate, up, act = ffn_up(g, g8["w_gate"], g8["w_up"], 0, "ffn_up")
        h_next = mm_res(act, g8["w_down"], h_mid, 0, "ffn_down")[0]
        saved.append(dict(h=h, a=a, u=u, c_q=c_q, c_kv=c_kv, kr=kr, qn=qn, q=q, kvn=kvn, kv=kv, v=v, qp=qp, kp=kp,
                          o=o, lse=lse, cat=cat, h_mid=h_mid, g=g, gate=gate, up=up, act=act,
                          gq=gq, gkn=gkn, gkr=gkr))
        h = h_next

    dh, sq = loss_head(h, target, seq, "loss_head")
    loss = lax.psum(0.5 / D_MODEL * jnp.sum(sq), ("x", "y", "c"))

    core = lax.axis_index("c").astype(jnp.int32).reshape(1)
    small_grads = {n: [None] * depth for n in REPLICATED + ("w_pool",)}
    pending = []
    got = [{} for _ in range(depth)]

    def pair_reduce(l, names, slots, small):
        bigs = [slots[n] for n in names]
        if small is None:
            from_sibling, got_small = exchange_pair(bigs, None, "grad_exchange_pair"), None
        else:
            *from_sibling, got_small = exchange_pair(bigs, small, "grad_exchange_pair_small")
        partial = pair_sum(bigs, list(from_sibling), core, "grad_pair_sum")
        return [(l, n, p) for n, p in zip(names, partial)], got_small

    for l in reversed(range(depth)):
        s = saved[l]
        g8 = g8l[l]
        w_in_p = w_in_ps[l]
        slots = {}
        dgate, dup = ffn_bwd_act(dh, g8["w_down"], 0, s["gate"], s["up"], "ffn_bwd_act")
        slot_of = lambda n, full: full.reshape(g8[n].shape[:1] + g8[n].shape[2:])
        slots["w_down"] = slot_of("w_down", mm_tn([s["act"]], dh, "dw_down", out_dtype=BF16)[0])
        dw_gate_t, dw_up_t = mm_tn([dgate, dup], s["g"], "dw_gate_up", out_dtype=BF16)
        slots["w_gate"] = slot_of("w_gate", dw_gate_t)
        slots["w_up"] = slot_of("w_up", dw_up_t)
        dh_mid, dg_ffn = mm_nt_normbwd([("gathered", dgate, g8["w_gate"]), ("gathered", dup, g8["w_up"])],
                                       s["h_mid"], row(ffn_norm_g, l), dh, 0, "ffn_bwd_in")
        small_grads["ffn_norm_g"][l] = dg_ffn[0]
        slots["w_out"] = slot_of("w_out", mm_tn([s["cat"]], dh_mid, "dw_out", out_dtype=BF16)[0])
        pending += pair_reduce(l, BEFORE_ATTENTION_BWD, slots, None)[0]
        dy_pool, do, delta = out_proj_bwd(dh_mid, g8["w_out"], 0, s["o"], "out_proj_bwd")
        du, dw_pool, dscale = pool_bwd(s["u"], dy_pool, w_pool_b[l], row(pool_scale, l), "pool_bwd")
        small_grads["w_pool"][l] = dw_pool
        small_grads["pool_scale"][l] = dscale[0]
        pool_rider = [jnp.stack(small_grads["w_pool"], axis=0).reshape(-1, LANES)] if l == 0 else []
        dqp, dkp, dv, *arrived = attn_bwd(s["qp"], s["kp"], s["v"], do, s["lse"], delta, "attn_bwd_exchange",
                                          exchange=[p for _, _, p in pending], gather=pool_rider)
        if pool_rider:
            got_pool = arrived.pop()
        for (k, n, _), parts in zip(pending, arrived):
            got[k][n] = parts
        dq, dkv, dkr, dgq, dgkn, dgkr = qk_prep_bwd(dqp, dkp, dv, s["q"], s["kv"], s["kr"], cosf, sins, seg,
                                                    s["gq"], s["gkn"], s["gkr"], "qk_prep_bwd")
        small_grads["q_norm_g"][l] = dgq[0, :QK_HEAD_DIM]
        small_grads["k_norm_g"][l] = jnp.concatenate([dgkn[0, :QK_NOPE_DIM], dgkr[0, QK_NOPE_DIM:QK_HEAD_DIM]])
        heads_first = lambda full: jnp.swapaxes(full.reshape(full.shape[0], MLA_HEADS, HEAD_PAD), 0, 1)
        dw_q_t = mm_tn([dq], s["qn"], "dw_q", out_dtype=BF16)[0]
        slots["w_q_b"] = dw_q_t.reshape(MLA_HEADS, HEAD_PAD, -1)[:, :QK_HEAD_DIM]
        dc_q, dg_qa = mm_nt_normbwd([("heads_t", dq, g8["w_q_b"])], s["c_q"], row(q_a_norm_g, l), None, 0,
                                    "q_proj_bwd")
        small_grads["q_a_norm_g"][l] = dg_qa[0]
        slots["w_kv_b"] = heads_first(mm_tn([s["kvn"]], dkv, "dw_kv", out_dtype=BF16)[0])
        dc_kv, dg_kva = mm_nt_normbwd([("heads", dkv, g8["w_kv_b"])], s["c_kv"], row(kv_a_norm_g, l), None, 0,
                                      "kv_proj_bwd")
        small_grads["kv_a_norm_g"][l] = dg_kva[0]
        dw_pool_t, dw_q_t, dw_kv_t, dw_rope_t = mm_tn([du, dc_q, dc_kv, dkr], s["a"], "dw_in", out_dtype=BF16)
        dw_in_t = jnp.concatenate([dw_pool_t, dw_q_t, dw_kv_t, dw_rope_t[QK_NOPE_DIM:QK_HEAD_DIM]], axis=0)
        slots["w_in"] = slot_of("w_in", dw_in_t)
        dh, dg_attn = mm_nt_normbwd(
            [("rows", du, w_in_p, 0), ("rows", dc_q, w_in_p, s1), ("rows", dc_kv, w_in_p, s2),
             ("rows", dkr, w_in_p, s3)],
            s["h"], row(attn_norm_g, l), dh_mid, 0, "in_proj_bwd")
        small_grads["attn_norm_g"][l] = dg_attn[0]

        small_slots = None
        if l == 0:
            rep_shapes = [weights[n].shape for n in REPLICATED]
            rep_count = sum(int(jnp.size(weights[n])) for n in REPLICATED)
            rep_rows = -(-rep_count // (8 * LANES)) * 8
            rep_packed = _pack_flat([jnp.stack(small_grads[n], axis=0) for n in REPLICATED], rep_rows)
            meta_slots = jnp.transpose(dh[:N_META].reshape(N_META, N_DEV, LANES), (1, 0, 2))
            small_slots = jnp.concatenate(
                [meta_slots, jnp.broadcast_to(rep_packed[None], (N_DEV, rep_rows, LANES))], axis=1)
        pending, got_small = pair_reduce(l, [n for n in SHARDED if n not in BEFORE_ATTENTION_BWD], slots,
                                         small_slots)
    for (k, n, _), parts in zip(pending, exchange_chips([p for _, _, p in pending], "grad_exchange_chips")):
        got[k][n] = parts

    grad_x = dh[N_META:length][None]

    per = [{} for _ in range(4)]
    for n in SHARDED:
        parts = [got[l][n] for l in range(depth)]
        if n in ADAM_TRANSPOSED:
            t = lambda a: jnp.swapaxes(a, 1, 2)
            outs = [t(o) for o in adamw_shard(parts, local[n], t(mom1[n]), t(mom2[n]), "adamw_" + n)]
        elif n in TRANSPOSED:
            grad = jnp.swapaxes(chip_sum(parts, "chip_sum_" + n), 1, 2)
            outs = adamw_shard(None, weights[n], mom1[n], mom2[n], "adamw_" + n, grad=grad)
        else:
            outs = adamw_shard(parts, weights[n], mom1[n], mom2[n], "adamw_" + n)
        for k in range(4):
            per[k][n] = outs[k]
    ps = lambda src: jnp.concatenate(
        [src["meta_tokens"], _pack_flat([src[n] for n in REPLICATED], rep_rows)], axis=0)
    small_out = adamw_packed(got_small, ps(weights), ps(mom1), ps(mom2), "adamw_small")
    as_rows = lambda a: a.reshape(-1, LANES)
    pool_out = adamw_packed(got_pool, as_rows(w_pool), as_rows(m_w_pool), as_rows(v_w_pool), "adamw_w_pool")
    for k in range(4):
        per[k]["meta_tokens"] = small_out[k][:N_META]
        per[k].update(zip(REPLICATED, _unpack_flat(small_out[k][N_META:], rep_shapes)))
        per[k]["w_pool"] = pool_out[k].reshape(w_pool.shape)
    return (loss, grad_x, *[per[0][n] for n in order], *[per[1][n] for n in order],
            *[per[2][n] for n in order], *[per[3][n] for n in order])
```

```python
import functools

import jax
import jax.numpy as jnp
from jax import lax
from jax.experimental import pallas as pl
from jax.experimental.pallas import tpu as pltpu

F32 = jnp.float32
BF16 = jnp.bfloat16

D_MODEL = 1024
N_META = 16
POOL_WIDTH = 512
POOL_WINDOWS = (2, 4, 8, 16)
POOL_GROUP_DIM = 128
POOL_HALO = 16
MLA_HEADS = 8
QK_NOPE_DIM = 64
QK_ROPE_DIM = 32
QK_HEAD_DIM = 96
V_HEAD_DIM = 64
HEAD_PAD = 128
Q_LORA_RANK = 384
KV_LORA_RANK = 256
ROPE_THETA = 10000.0
RMS_EPS = 1e-6
ATTN_SCALE = QK_HEAD_DIM ** -0.5
LOG2_E = 1.4426950408889634
SCORE_SCALE = ATTN_SCALE * LOG2_E

ADAM_LR = 0.001
ADAM_B1 = 0.9
ADAM_B2 = 0.999
ADAM_EPS = 1e-08
ADAM_WD = 0.01
ADAM_STEP = 10

N_DEV = 8
N_CHIPS = 4
LANES = 128
ROW_TILE = 384
LONG_TILE = 1056
WIDE_TILE = 1056
VMEM_LIMIT = 56 * 1024 * 1024


def _params(*sem):
    return pltpu.CompilerParams(dimension_semantics=sem, vmem_limit_bytes=VMEM_LIMIT)


def _wide_tile(rows):
    return WIDE_TILE if rows % WIDE_TILE == 0 else ROW_TILE


def _row_spec(tile, width):
    return pl.BlockSpec((tile, width), lambda i: (i, 0))


def _const_spec(shape):
    return pl.BlockSpec(shape, lambda i: tuple(0 for _ in shape))


def _layer_spec(w, layer):
    return pl.BlockSpec((None,) + w.shape[1:], lambda *_: (layer, 0, 0))


def _gathered_spec(w8, layer):
    return pl.BlockSpec((N_DEV, None) + w8.shape[2:], lambda *_: (0, layer, 0, 0))


def _nt(a, b):
    return lax.dot_general(a, b, (((1,), (1,)), ((), ())), preferred_element_type=F32)


def _tn(a, b):
    return lax.dot_general(a, b, (((0,), (0,)), ((), ())), preferred_element_type=F32)


def _nn(a, b):
    return jnp.dot(a, b, preferred_element_type=F32)


def _silu(g):
    return g * (1.0 / (1.0 + jnp.exp(-g)))


def _rms(xf):
    return lax.rsqrt(jnp.mean(xf * xf, axis=-1, keepdims=True) + RMS_EPS)


def norm_mm(x, gamma, wt, layer, splits, dtypes, name):
    L, K = x.shape
    tm = _wide_tile(L)

    def body(x_ref, g_ref, w_ref, a_ref, *z_refs):
        xf = x_ref[...]
        a = ((xf * _rms(xf)) * g_ref[...]).astype(BF16)
        a_ref[...] = a
        z = _nt(a, w_ref[...])
        for (s, n), zr in zip(splits, z_refs):
            zr[...] = z[:, s:s + n].astype(zr.dtype)

    widths = [n for _, n in splits]
    return pl.pallas_call(
        body,
        grid=(L // tm,),
        in_specs=[_row_spec(tm, K), _const_spec((1, K)), _layer_spec(wt, layer)],
        out_specs=[_row_spec(tm, K)] + [_row_spec(tm, n) for n in widths],
        out_shape=[jax.ShapeDtypeStruct((L, K), BF16)]
        + [jax.ShapeDtypeStruct((L, n), dt) for n, dt in zip(widths, dtypes)],
        compiler_params=_params("arbitrary"),
        name=name,
    )(x, gamma, wt)


def norm_mm_heads(x, gamma, w8, layer, name, transposed=False):
    L, K = x.shape
    hw = w8.shape[-2] if transposed else w8.shape[-1]
    tm = _wide_tile(L)

    def body(x_ref, g_ref, w_ref, a_ref, z_ref):
        xf = x_ref[...]
        a = ((xf * _rms(xf)) * g_ref[...]).astype(BF16)
        a_ref[...] = a
        if hw < HEAD_PAD:
            z_ref[...] = jnp.zeros_like(z_ref)
        for j in range(MLA_HEADS):
            z_ref[:, j * HEAD_PAD:j * HEAD_PAD + hw] = _nt(a, w_ref[j]) if transposed else _nn(a, w_ref[j])

    return pl.pallas_call(
        body,
        grid=(L // tm,),
        in_specs=[_row_spec(tm, K), _const_spec((1, K)), _gathered_spec(w8, layer)],
        out_specs=[_row_spec(tm, K), _row_spec(tm, MLA_HEADS * HEAD_PAD)],
        out_shape=[jax.ShapeDtypeStruct((L, K), BF16), jax.ShapeDtypeStruct((L, MLA_HEADS * HEAD_PAD), F32)],
        compiler_params=_params("arbitrary"),
        name=name,
    )(x, gamma, w8)


FF_GROUP = 4


def _ff_spec(w8, layer):
    return pl.BlockSpec((FF_GROUP, None) + w8.shape[2:], lambda j, i: (j, layer, 0, 0))


def ffn_up(g, w_gate8, w_up8, layer, name):
    L, K = g.shape
    fb = w_gate8.shape[-2]
    tm, tf = _wide_tile(L), FF_GROUP * fb

    def body(a_ref, wg_ref, wu_ref, gate_ref, up_ref, act_ref):
        a = a_ref[...]
        gate = _nt(a, wg_ref[...].reshape(tf, K))
        up = _nt(a, wu_ref[...].reshape(tf, K))
        gate_ref[...] = gate.astype(BF16)
        up_ref[...] = up.astype(BF16)
        act_ref[...] = (_silu(gate) * up).astype(BF16)

    tile = pl.BlockSpec((tm, tf), lambda j, i: (i, j))
    F = N_DEV * fb
    return pl.pallas_call(
        body,
        grid=(N_DEV // FF_GROUP, L // tm),
        in_specs=[pl.BlockSpec((tm, K), lambda j, i: (i, 0)), _ff_spec(w_gate8, layer), _ff_spec(w_up8, layer)],
        out_specs=[tile, tile, tile],
        out_shape=[
            jax.ShapeDtypeStruct((L, F), BF16),
            jax.ShapeDtypeStruct((L, F), BF16),
            jax.ShapeDtypeStruct((L, F), BF16),
        ],
        compiler_params=_params("arbitrary", "arbitrary"),
        name=name,
    )(g, w_gate8, w_up8)


def mm_res(a, w8, res, layer, name, gamma=None):
    L = a.shape[0]
    kb, N = w8.shape[-2:]
    tm = _wide_tile(L)
    normed = gamma is not None

    def body(a_ref, w_ref, r_ref, *rest):
        out = r_ref[...] + _nn(a_ref[...], w_ref[...].reshape(N_DEV * kb, N))
        if normed:
            g_ref, o_ref, n_ref = rest
            n_ref[...] = ((out * _rms(out)) * g_ref[...]).astype(BF16)
        else:
            (o_ref,) = rest
        o_ref[...] = out

    in_specs = [_row_spec(tm, N_DEV * kb), _gathered_spec(w8, layer), _row_spec(tm, N)]
    out_specs = [_row_spec(tm, N)]
    out_shape = [jax.ShapeDtypeStruct((L, N), F32)]
    args = [a, w8, res]
    if normed:
        in_specs.append(_const_spec((1, N)))
        out_specs.append(_row_spec(tm, N))
        out_shape.append(jax.ShapeDtypeStruct((L, N), BF16))
        args.append(gamma)
    return pl.pallas_call(
        body,
        grid=(L // tm,),
        in_specs=in_specs,
        out_specs=out_specs,
        out_shape=out_shape,
        compiler_params=_params("arbitrary"),
        name=name,
    )(*args)


def out_proj_bwd(dz, w8, layer, o, name):
    L, N = dz.shape
    kb = w8.shape[-2]
    C = o.shape[1]
    tm = _wide_tile(L)

    def body(dz_ref, w_ref, o_ref, dy_ref, do_ref, dl_ref):
        d = _nt(dz_ref[...].astype(BF16), w_ref[...].reshape(N_DEV * kb, N))
        dy_ref[...] = d[:, :C]
        do_ref[...] = d[:, C:]
        left = lax.broadcasted_iota(jnp.int32, (tm, LANES), 1) < V_HEAD_DIM
        for p in range(C // LANES):
            cols = slice(p * LANES, (p + 1) * LANES)
            prod = d[:, C + p * LANES:C + (p + 1) * LANES] * o_ref[:, cols]
            d0 = jnp.sum(jnp.where(left, prod, 0.0), axis=1, keepdims=True)
            d1 = jnp.sum(jnp.where(left, 0.0, prod), axis=1, keepdims=True)
            dl_ref[:, cols] = jnp.where(left, d0, d1)

    return pl.pallas_call(
        body,
        grid=(L // tm,),
        in_specs=[_row_spec(tm, N), _gathered_spec(w8, layer), _row_spec(tm, C)],
        out_specs=[_row_spec(tm, C)] * 3,
        out_shape=[jax.ShapeDtypeStruct((L, C), F32)] * 3,
        compiler_params=_params("arbitrary"),
        name=name,
    )(dz, w8, o)


def mm_nt_normbwd(terms, x, gamma, dres, layer, name, wide=True):
    L, K = x.shape
    tm = _wide_tile(L) if wide else ROW_TILE
    n_terms = len(terms)
    has_res = dres is not None
    weights = []
    for t in terms:
        if not any(t[2] is u for u in weights):
            weights.append(t[2])
    which = [[t[2] is u for u in weights].index(True) for t in terms]
    n_in = n_terms + len(weights)

    def body(*refs):
        dz_refs = refs[:n_terms]
        w_refs = [refs[n_terms + n] for n in which]
        x_ref, g_ref = refs[n_in], refs[n_in + 1]
        pos = n_in + 2
        r_ref = refs[pos] if has_res else None
        dx_ref, dg_ref = refs[pos + has_res], refs[pos + has_res + 1]
        da = None
        for t, dz_ref, w_ref in zip(terms, dz_refs, w_refs):
            if t[0] == "rows":
                n, at = t[1].shape[1], t[3]
                parts = [_nn(dz_ref[...].astype(BF16), w_ref[at:at + n, :])]
            elif t[0] == "heads":
                hw = t[2].shape[-1]
                parts = [_nt(dz_ref[:, j * HEAD_PAD:j * HEAD_PAD + hw].astype(BF16), w_ref[j])
                         for j in range(MLA_HEADS)]
            elif t[0] == "heads_t":
                hw = t[2].shape[-2]
                parts = [_nn(dz_ref[:, j * HEAD_PAD:j * HEAD_PAD + hw].astype(BF16), w_ref[j])
                         for j in range(MLA_HEADS)]
            else:
                nb = t[2].shape[-2]
                parts = [_nn(dz_ref[...].astype(BF16), w_ref[...].reshape(N_DEV * nb, K))]
            for p in parts:
                da = p if da is None else da + p
        xf = x_ref[...]
        r = _rms(xf)
        xh = xf * r

        @pl.when(pl.program_id(0) == 0)
        def _():
            dg_ref[...] = jnp.zeros_like(dg_ref)

        dg_ref[...] += jnp.sum(da * xh, axis=0, keepdims=True)
        dxh = da * g_ref[...]
        dx = r * (dxh - xh * jnp.mean(dxh * xh, axis=-1, keepdims=True))
        if has_res:
            dx = dx + r_ref[...]
        dx_ref[...] = dx

    in_specs = [_row_spec(tm, t[1].shape[1]) for t in terms]
    for w in weights:
        in_specs.append(_layer_spec(w, layer) if w.ndim == 3 else _gathered_spec(w, layer))
    in_specs += [_row_spec(tm, K), _const_spec((1, K))]
    args = [t[1] for t in terms] + weights + [x, gamma]
    if has_res:
        in_specs.append(_row_spec(tm, K))
        args.append(dres)
    return pl.pallas_call(
        body,
        grid=(L // tm,),
        in_specs=in_specs,
        out_specs=[_row_spec(tm, K), _const_spec((1, K))],
        out_shape=[jax.ShapeDtypeStruct((L, K), F32), jax.ShapeDtypeStruct((1, K), F32)],
        compiler_params=_params("arbitrary"),
        name=name,
    )(*args)


MAX_OUT_ROWS = 1408


def mm_tn(a_list, b, name, out_dtype=F32):
    n = len(a_list)
    L, N = b.shape
    tks = [MAX_OUT_ROWS if (a.shape[1] > MAX_OUT_ROWS and a.shape[1] % MAX_OUT_ROWS == 0) else a.shape[1]
           for a in a_list]
    blocks = a_list[0].shape[1] // tks[0]
    assert all(a.shape[1] // tk == blocks for a, tk in zip(a_list, tks))
    tl = LONG_TILE if L % LONG_TILE == 0 else ROW_TILE
    n_l = L // tl

    def body(*refs):
        a_refs, b_ref = refs[:n], refs[n]
        o_refs, accs = refs[n + 1:2 * n + 1], refs[2 * n + 1:]
        l = pl.program_id(1)
        bt = b_ref[...].astype(BF16)
        for a_ref, o_ref, acc in zip(a_refs, o_refs, accs):
            @pl.when(l == 0)
            def _(acc=acc):
                acc[...] = jnp.zeros_like(acc)

            acc[...] += _tn(a_ref[...].astype(BF16), bt)

            @pl.when(l == n_l - 1)
            def _(acc=acc, o_ref=o_ref):
                o_ref[...] = acc[...].astype(o_ref.dtype)

    return pl.pallas_call(
        body,
        grid=(blocks, n_l),
        in_specs=[pl.BlockSpec((tl, tk), lambda j, l: (l, j)) for tk in tks]
        + [pl.BlockSpec((tl, N), lambda j, l: (l, 0))],
        out_specs=[pl.BlockSpec((tk, N), lambda j, l: (j, 0)) for tk in tks],
        out_shape=[jax.ShapeDtypeStruct((a.shape[1], N), out_dtype) for a in a_list],
        scratch_shapes=[pltpu.VMEM((tk, N), F32) for tk in tks],
        compiler_params=_params("arbitrary", "arbitrary"),
        name=name,
    )(*a_list, b)


def ffn_bwd_act(dh, w_down8, layer, gate, up, name):
    L, K = dh.shape
    fb = w_down8.shape[-2]
    tm, tf = _wide_tile(L), FF_GROUP * fb

    def body(dh_ref, w_ref, gate_ref, up_ref, dgate_ref, dup_ref):
        dact = _nt(dh_ref[...].astype(BF16), w_ref[...].reshape(tf, K))
        g = gate_ref[...].astype(F32)
        sig = 0.5 * jnp.tanh(0.5 * g) + 0.5
        dup_ref[...] = (dact * (g * sig)).astype(BF16)
        dgate_ref[...] = (dact * up_ref[...].astype(F32) * (sig * (1.0 + g * (1.0 - sig)))).astype(BF16)

    tile = pl.BlockSpec((tm, tf), lambda j, i: (i, j))
    F = N_DEV * fb
    return pl.pallas_call(
        body,
        grid=(N_DEV // FF_GROUP, L // tm),
        in_specs=[pl.BlockSpec((tm, K), lambda j, i: (i, 0)), _ff_spec(w_down8, layer), tile, tile],
        out_specs=[tile, tile],
        out_shape=[jax.ShapeDtypeStruct((L, F), BF16), jax.ShapeDtypeStruct((L, F), BF16)],
        compiler_params=_params("arbitrary", "arbitrary"),
        name=name,
    )(dh, w_down8, gate, up)


def _pool_residual(scr, lo, tm, g, w, t):
    cols = slice(g * POOL_GROUP_DIM, (g + 1) * POOL_GROUP_DIM)
    cur = scr[lo:lo + tm, cols]
    s = cur
    for k in range(1, w):
        s = s + scr[lo - k:lo - k + tm, cols]
    cnt = jnp.minimum(t + 1, w).astype(F32)
    return s / cnt - cur


def pool_fwd(u, w_pool, scale, name):
    L, C = u.shape
    tm, halo = ROW_TILE, POOL_HALO

    def body(u_ref, halo_ref, w_ref, s_ref, y_ref, scr):
        i = pl.program_id(0)
        scr[0:halo, :] = jnp.where(i > 0, halo_ref[...], 0.0)
        scr[halo:halo + tm, :] = u_ref[...]
        t = i * tm + lax.broadcasted_iota(jnp.int32, (tm, POOL_GROUP_DIM), 0)
        for g, w in enumerate(POOL_WINDOWS):
            cols = slice(g * POOL_GROUP_DIM, (g + 1) * POOL_GROUP_DIM)
            p = _pool_residual(scr, halo, tm, g, w, t)
            y = _nn(p.astype(BF16), w_ref[g]) * s_ref[:, cols]
            y_ref[:, cols] = y.astype(y_ref.dtype)

    return pl.pallas_call(
        body,
        grid=(L // tm,),
        in_specs=[
            _row_spec(tm, C),
            pl.BlockSpec((halo, C), lambda i: (jnp.maximum(i * (tm // halo) - 1, 0), 0)),
            _const_spec(w_pool.shape),
            _const_spec((1, C)),
        ],
        out_specs=_row_spec(tm, C),
        out_shape=jax.ShapeDtypeStruct((L, 2 * C), BF16),
        scratch_shapes=[pltpu.VMEM((tm + halo, C), F32)],
        compiler_params=_params("arbitrary"),
        name=name,
    )(u, u, w_pool, scale)


def pool_bwd(u, dy, w_pool, scale, name):
    L, C = u.shape
    tm, halo = ROW_TILE, POOL_HALO
    n_tiles = L // tm
    last_halo = L // halo - 1

    def body(u_ref, uh_ref, dy_ref, dyh_ref, w_ref, s_ref, du_ref, dw_ref, ds_ref, scr_u, scr_q):
        i = pl.program_id(0)

        @pl.when(i == 0)
        def _():
            dw_ref[...] = jnp.zeros_like(dw_ref)
            ds_ref[...] = jnp.zeros_like(ds_ref)

        scr_u[0:halo, :] = jnp.where(i > 0, uh_ref[...], 0.0)
        scr_u[halo:halo + tm, :] = u_ref[...]
        t = i * tm + lax.broadcasted_iota(jnp.int32, (tm, POOL_GROUP_DIM), 0)
        th = (i + 1) * tm + lax.broadcasted_iota(jnp.int32, (halo, POOL_GROUP_DIM), 0)
        for g, w in enumerate(POOL_WINDOWS):
            cols = slice(g * POOL_GROUP_DIM, (g + 1) * POOL_GROUP_DIM)
            p = _pool_residual(scr_u, halo, tm, g, w, t).astype(BF16)
            wg = w_ref[g]
            sc = s_ref[:, cols]
            dy = dy_ref[:, cols]
            ds_ref[:, cols] += jnp.sum(dy * _nn(p, wg), axis=0, keepdims=True)
            dys = (dy * sc).astype(BF16)
            dw_ref[g] += _tn(p, dys)
            dp = _nt(dys, wg)
            dyh = jnp.where(i < n_tiles - 1, dyh_ref[:, cols], 0.0)
            dph = _nt((dyh * sc).astype(BF16), wg)
            scr_q[0:tm, cols] = dp / jnp.minimum(t + 1, w).astype(F32)
            scr_q[tm:tm + halo, cols] = dph / jnp.minimum(th + 1, w).astype(F32)
            acc = scr_q[0:tm, cols]
            for k in range(1, w):
                acc = acc + scr_q[k:k + tm, cols]
            du_ref[:, cols] = acc - dp

    return pl.pallas_call(
        body,
        grid=(n_tiles,),
        in_specs=[
            _row_spec(tm, C),
            pl.BlockSpec((halo, C), lambda i: (jnp.maximum(i * (tm // halo) - 1, 0), 0)),
            _row_spec(tm, C),
            pl.BlockSpec((halo, C), lambda i: (jnp.minimum((i + 1) * (tm // halo), last_halo), 0)),
            _const_spec(w_pool.shape),
            _const_spec((1, C)),
        ],
        out_specs=[_row_spec(tm, C), _const_spec(w_pool.shape), _const_spec((1, C))],
        out_shape=[
            jax.ShapeDtypeStruct((L, C), F32),
            jax.ShapeDtypeStruct(w_pool.shape, F32),
            jax.ShapeDtypeStruct((1, C), F32),
        ],
        scratch_shapes=[pltpu.VMEM((tm + halo, C), F32), pltpu.VMEM((tm + halo, C), F32)],
        compiler_params=_params("arbitrary"),
        name=name,
    )(u, u, dy, dy, w_pool, scale)


def _head_masks(rows):
    lane = lax.broadcasted_iota(jnp.int32, (rows, HEAD_PAD), 1)
    return lane, lane < QK_NOPE_DIM, (lane >= QK_NOPE_DIM) & (lane < QK_HEAD_DIM)


def _rope_swap(x, lane):
    half = QK_ROPE_DIM // 2
    swapped = jnp.where(lane < QK_NOPE_DIM + half, pltpu.roll(x, HEAD_PAD - half, 1), pltpu.roll(x, half, 1))
    return jnp.where((lane >= QK_NOPE_DIM) & (lane < QK_HEAD_DIM), swapped, 0.0)


def _seg_mean(v, seg_ref):
    hi = v.astype(BF16)
    lo = (v - hi.astype(F32)).astype(BF16)
    seg = seg_ref[...]
    return _nn(hi, seg) + _nn(lo, seg)


def _segment_matrix():
    lane = jnp.arange(HEAD_PAD)
    seg = jnp.where(lane < QK_NOPE_DIM, 0, jnp.where(lane < QK_HEAD_DIM, 1, 2))
    inv = jnp.where(lane < QK_NOPE_DIM, 1.0 / QK_NOPE_DIM, jnp.where(lane < QK_HEAD_DIM, 1.0 / QK_ROPE_DIM, 0.0))
    return jnp.where(seg[:, None] == seg[None, :], inv[None, :], 0.0).astype(BF16)


def qk_prep_fwd(q, kv, kr, cosf, sins, seg, gq, gkn, gkr, name):
    L = q.shape[0]
    tm = _wide_tile(L)
    W = MLA_HEADS * HEAD_PAD

    def body(q_ref, kv_ref, kr_ref, c_ref, s_ref, seg_ref, gq_ref, gkn_ref, gkr_ref, qo_ref, ko_ref, vo_ref):
        lane, m_n, _ = _head_masks(tm)
        cosf_, sins_ = c_ref[...], s_ref[...]
        kr_ = kr_ref[...]
        rk = lax.rsqrt(_seg_mean(kr_ * kr_, seg_ref) + RMS_EPS)
        krn = kr_ * rk * gkr_ref[...]
        krf = krn * cosf_ + _rope_swap(krn, lane) * sins_
        for h in range(MLA_HEADS):
            cols = slice(h * HEAD_PAD, (h + 1) * HEAD_PAD)
            qh = q_ref[:, cols]
            qn = qh * lax.rsqrt(_seg_mean(qh * qh, seg_ref) + RMS_EPS) * gq_ref[...]
            qo_ref[:, cols] = (qn * cosf_ + _rope_swap(qn, lane) * sins_).astype(BF16)
            kh = jnp.where(m_n, kv_ref[:, cols], 0.0)
            rn = lax.rsqrt(_seg_mean(kh * kh, seg_ref) + RMS_EPS)
            ko_ref[:, cols] = (kh * rn * gkn_ref[...] + krf).astype(BF16)
        for p in range(MLA_HEADS // 2):
            even = kv_ref[:, 2 * p * HEAD_PAD:(2 * p + 1) * HEAD_PAD]
            odd = kv_ref[:, (2 * p + 1) * HEAD_PAD:(2 * p + 2) * HEAD_PAD]
            pair = jnp.where(m_n, pltpu.roll(even, V_HEAD_DIM, 1), odd)
            vo_ref[:, p * LANES:(p + 1) * LANES] = pair.astype(BF16)

    vec = _const_spec((1, HEAD_PAD))
    return pl.pallas_call(
        body,
        grid=(L // tm,),
        in_specs=[_row_spec(tm, W), _row_spec(tm, W), _row_spec(tm, HEAD_PAD), _row_spec(tm, HEAD_PAD),
                  _row_spec(tm, HEAD_PAD), _const_spec((HEAD_PAD, HEAD_PAD)), vec, vec, vec],
        out_specs=[_row_spec(tm, W), _row_spec(tm, W), _row_spec(tm, W // 2)],
        out_shape=[jax.ShapeDtypeStruct((L, W), BF16), jax.ShapeDtypeStruct((L, W), BF16),
                   jax.ShapeDtypeStruct((L, W // 2), BF16)],
        compiler_params=_params("arbitrary"),
        name=name,
    )(q, kv, kr, cosf, sins, seg, gq, gkn, gkr)


def qk_prep_bwd(dqo, dko, dv, q, kv, kr, cosf, sins, seg, gq, gkn, gkr, name):
    L = q.shape[0]
    tm = ROW_TILE
    W = MLA_HEADS * HEAD_PAD

    def body(dqo_ref, dko_ref, dv_ref, q_ref, kv_ref, kr_ref, c_ref, s_ref, seg_ref, gq_ref, gkn_ref, gkr_ref,
             dq_ref, dkv_ref, dkr_ref, dgq_ref, dgkn_ref, dgkr_ref):
        @pl.when(pl.program_id(0) == 0)
        def _():
            dgq_ref[...] = jnp.zeros_like(dgq_ref)
            dgkn_ref[...] = jnp.zeros_like(dgkn_ref)
            dgkr_ref[...] = jnp.zeros_like(dgkr_ref)

        lane, m_n, _ = _head_masks(tm)
        cosf_, sins_ = c_ref[...], s_ref[...]
        dgq = jnp.zeros((1, HEAD_PAD), F32)
        dgkn = jnp.zeros((1, HEAD_PAD), F32)
        dkrf = jnp.zeros((tm, HEAD_PAD), F32)
        for h in range(MLA_HEADS):
            cols = slice(h * HEAD_PAD, (h + 1) * HEAD_PAD)
            dy = dqo_ref[:, cols]
            dqn = dy * cosf_ + _rope_swap(dy * sins_, lane)
            qh = q_ref[:, cols]
            rinv = lax.rsqrt(_seg_mean(qh * qh, seg_ref) + RMS_EPS)
            xh = qh * rinv
            dgq = dgq + jnp.sum(dqn * xh, axis=0, keepdims=True)
            dxh = dqn * gq_ref[...]
            dq_ref[:, cols] = rinv * (dxh - xh * _seg_mean(dxh * xh, seg_ref))
            dk = dko_ref[:, cols]
            dkrf = dkrf + dk
            kh = jnp.where(m_n, kv_ref[:, cols], 0.0)
            rn = lax.rsqrt(_seg_mean(kh * kh, seg_ref) + RMS_EPS)
            xk = kh * rn
            dgkn = dgkn + jnp.sum(dk * xk, axis=0, keepdims=True)
            dxk = dk * gkn_ref[...]
            dkn = rn * (dxk - xk * _seg_mean(dxk * xk, seg_ref))
            dvp = dv_ref[:, (h // 2) * LANES:(h // 2 + 1) * LANES]
            dvh = pltpu.roll(dvp, V_HEAD_DIM, 1) if h % 2 == 0 else dvp
            dkv_ref[:, cols] = jnp.where(m_n, dkn, dvh)
        kr_ = kr_ref[...]
        rk = lax.rsqrt(_seg_mean(kr_ * kr_, seg_ref) + RMS_EPS)
        xr = kr_ * rk
        dkrn = dkrf * cosf_ + _rope_swap(dkrf * sins_, lane)
        dgkr_ref[...] += jnp.sum(dkrn * xr, axis=0, keepdims=True)
        dxr = dkrn * gkr_ref[...]
        dkr_ref[...] = rk * (dxr - xr * _seg_mean(dxr * xr, seg_ref))
        dgq_ref[...] += dgq
        dgkn_ref[...] += dgkn

    vec = _const_spec((1, HEAD_PAD))
    return pl.pallas_call(
        body,
        grid=(L // tm,),
        in_specs=[_row_spec(tm, W), _row_spec(tm, W), _row_spec(tm, W // 2), _row_spec(tm, W), _row_spec(tm, W),
                  _row_spec(tm, HEAD_PAD), _row_spec(tm, HEAD_PAD), _row_spec(tm, HEAD_PAD),
                  _const_spec((HEAD_PAD, HEAD_PAD)), vec, vec, vec],
        out_specs=[_row_spec(tm, W), _row_spec(tm, W), _row_spec(tm, HEAD_PAD), vec, vec, vec],
        out_shape=[jax.ShapeDtypeStruct((L, W), F32), jax.ShapeDtypeStruct((L, W), F32),
                   jax.ShapeDtypeStruct((L, HEAD_PAD), F32)] + [jax.ShapeDtypeStruct((1, HEAD_PAD), F32)] * 3,
        compiler_params=_params("arbitrary"),
        name=name,
    )(dqo, dko, dv, q, kv, kr, cosf, sins, seg, gq, gkn, gkr)


def attn_fwd(qp, kp, v, cat, name, gather=()):
    L = qp.shape[0]
    tq = ROW_TILE
    n_i = L // tq
    pair_w = 2 * HEAD_PAD
    n_pairs = MLA_HEADS // 2
    n_g = len(gather)

    def body(q_ref, k_ref, v_ref, cat_in, *rest):
        del cat_in
        o_ref, lse_ref, cat_ref = rest[n_g:n_g + 3]
        i = pl.program_id(1)
        if n_g:
            pair = pl.program_id(0)
            start, forward, finish = _gather_steps(rest[:n_g], rest[n_g + 3:2 * n_g + 3], *rest[2 * n_g + 3:])
            pl.when((pair == 0) & (i == 0))(start)
            pl.when((pair == n_pairs // 2) & (i == 0))(forward)
        left = lax.broadcasted_iota(jnp.int32, (tq, LANES), 1) < V_HEAD_DIM
        row = lax.broadcasted_iota(jnp.int32, (tq, tq), 0)
        col = lax.broadcasted_iota(jnp.int32, (tq, tq), 1)

        def step(t, carry, masked, width=1, first=0):
            rows = pl.ds(pl.multiple_of((first + t * width) * tq, tq), tq * width)
            vv = v_ref[rows, :]
            out = []
            for hh in range(2):
                cols = slice(hh * HEAD_PAD, (hh + 1) * HEAD_PAD)
                m, l, acc = carry[hh]
                s = _nt(q_ref[:, cols], k_ref[rows, cols]) * SCORE_SCALE
                if masked:
                    s = jnp.where(col <= row, s, -jnp.inf)
                m_new = jnp.maximum(m, jnp.max(s, axis=1, keepdims=True))
                alpha = jnp.exp2(m - m_new)
                p = jnp.exp2(s - m_new)
                l = alpha * l + jnp.sum(p, axis=1, keepdims=True)
                acc = alpha * acc + _nn(p.astype(BF16), vv)
                out.append((m_new, l, acc))
            return tuple(out)

        one = (jnp.full((tq, 1), -jnp.inf, F32), jnp.zeros((tq, 1), F32), jnp.zeros((tq, LANES), F32))
        doubles = jnp.right_shift(i, 1)
        carry = lax.fori_loop(0, doubles, functools.partial(step, masked=False, width=2), (one, one))
        carry = lax.fori_loop(2 * doubles, i, functools.partial(step, masked=False), carry)
        (m0, l0, a0), (m1, l1, a1) = step(i, carry, True)
        o = jnp.where(left, a0 / l0, a1 / l1)
        o_ref[...] = o
        cat_ref[...] = o.astype(BF16)
        lse_ref[...] = jnp.where(left, m0 + jnp.log2(l0), m1 + jnp.log2(l1))
        if n_g:
            pl.when((pair == n_pairs - 1) & (i == n_i - 1))(finish)

    return pl.pallas_call(
        body,
        grid=(n_pairs, n_i),
        in_specs=[
            pl.BlockSpec((tq, pair_w), lambda p, i: (i, p)),
            pl.BlockSpec((L, pair_w), lambda p, i: (0, p)),
            pl.BlockSpec((L, LANES), lambda p, i: (0, p)),
            _ANY,
        ] + [_ANY] * n_g,
        out_specs=[
            pl.BlockSpec((tq, LANES), lambda p, i: (i, p)),
            pl.BlockSpec((tq, LANES), lambda p, i: (i, p)),
            pl.BlockSpec((tq, LANES), lambda p, i: (i, n_pairs + p)),
        ] + [_ANY] * n_g,
        out_shape=[jax.ShapeDtypeStruct((L, n_pairs * LANES), F32), jax.ShapeDtypeStruct((L, n_pairs * LANES), F32),
                   jax.ShapeDtypeStruct(cat.shape, cat.dtype)]
        + [jax.ShapeDtypeStruct((N_DEV,) + g.shape, g.dtype) for g in gather],
        scratch_shapes=_gather_scratch(n_g) if n_g else [],
        input_output_aliases={3: 2},
        compiler_params=_params("arbitrary", "arbitrary"),
        name=name,
    )(qp, kp, v, cat, *gather)


def attn_bwd(qp, kp, v, do, lse, delta, name, exchange=(), gather=()):
    L = qp.shape[0]
    tq = ROW_TILE
    n_q = L // tq
    pair_w = 2 * HEAD_PAD
    n_pairs = MLA_HEADS // 2
    n_x, n_g = len(exchange), len(gather)
    n_r = n_x + n_g

    def body(q_ref, k_ref, v_ref, do_ref, lse_ref, dl_ref, *rest):
        dq_ref, dk_ref, dv_ref = rest[n_r:n_r + 3]
        riders_in, riders_out, sems = rest[:n_r], rest[n_r + 3:2 * n_r + 3], rest[2 * n_r + 3:]
        j = pl.program_id(1)
        pair = pl.program_id(0)
        steps = []
        if n_x:
            steps.append(_chip_exchange_steps(riders_in[:n_x], riders_out[:n_x], *sems[:3]))
        if n_g:
            steps.append(_direct_gather_steps(riders_in[n_x:], riders_out[n_x:], *sems[-3:]))
        for start, _ in steps:
            pl.when((pair == 0) & (j == 0))(start)

        @pl.when(j == 0)
        def _():
            dq_ref[...] = jnp.zeros_like(dq_ref)

        dk_ref[...] = jnp.zeros_like(dk_ref)
        dv_ref[...] = jnp.zeros_like(dv_ref)
        row = lax.broadcasted_iota(jnp.int32, (tq, tq), 0)
        col = lax.broadcasted_iota(jnp.int32, (tq, tq), 1)

        def step(t, carry, masked, width=1, first=0):
            rows = pl.ds(pl.multiple_of((first + t * width) * tq, tq), tq * width)
            do = do_ref[rows, :]
            left = lax.broadcasted_iota(jnp.int32, do.shape, 1) < V_HEAD_DIM
            vv = v_ref[...]
            dv = None
            for hh in range(2):
                cols = slice(hh * HEAD_PAD, (hh + 1) * HEAD_PAD)
                stat = slice(hh * V_HEAD_DIM, hh * V_HEAD_DIM + 1)
                q = q_ref[rows, cols]
                k = k_ref[:, cols]
                dom = jnp.where(left if hh == 0 else jnp.logical_not(left), do, 0.0).astype(BF16)
                s = _nt(q, k) * SCORE_SCALE
                p = jnp.exp2(s - lse_ref[rows, stat])
                if masked:
                    p = jnp.where(col <= row, p, 0.0)
                dp = _nt(dom, vv)
                ds = (p * (dp - dl_ref[rows, stat]) * ATTN_SCALE).astype(BF16)
                dq_ref[rows, cols] += _nn(ds, k)
                dk_ref[:, cols] += _tn(ds, q)
                t = _tn(p.astype(BF16), dom)
                dv = t if dv is None else dv + t
            dv_ref[...] += dv
            return carry

        step(j, 0, True)
        doubles = jnp.right_shift(n_q - 1 - j, 1)
        lax.fori_loop(0, doubles, functools.partial(step, masked=False, width=2, first=j + 1), 0)
        lax.fori_loop(j + 1 + 2 * doubles, n_q, functools.partial(step, masked=False), 0)
        for _, finish in steps:
            pl.when((pair == n_pairs - 1) & (j == n_q - 1))(finish)

    return pl.pallas_call(
        body,
        grid=(n_pairs, n_q),
        in_specs=[
            pl.BlockSpec((L, pair_w), lambda p, j: (0, p)),
            pl.BlockSpec((tq, pair_w), lambda p, j: (j, p)),
            pl.BlockSpec((tq, LANES), lambda p, j: (j, p)),
            pl.BlockSpec((L, LANES), lambda p, j: (0, p)),
            pl.BlockSpec((L, LANES), lambda p, j: (0, p)),
            pl.BlockSpec((L, LANES), lambda p, j: (0, p)),
        ] + [_ANY] * n_r,
        out_specs=[
            pl.BlockSpec((L, pair_w), lambda p, j: (0, p)),
            pl.BlockSpec((tq, pair_w), lambda p, j: (j, p)),
            pl.BlockSpec((tq, LANES), lambda p, j: (j, p)),
        ] + [_ANY] * n_r,
        out_shape=[
            jax.ShapeDtypeStruct((L, n_pairs * pair_w), F32),
            jax.ShapeDtypeStruct((L, n_pairs * pair_w), F32),
            jax.ShapeDtypeStruct((L, n_pairs * LANES), F32),
        ] + [jax.ShapeDtypeStruct(e.shape, e.dtype) for e in exchange]
        + [jax.ShapeDtypeStruct((N_DEV,) + g.shape, g.dtype) for g in gather],
        scratch_shapes=(_chip_exchange_scratch(n_x) if n_x else []) + (_direct_gather_scratch(n_g) if n_g else []),
        compiler_params=_params("arbitrary", "arbitrary"),
        name=name,
    )(qp, kp, v, do, lse, delta, *exchange, *gather)


def loss_head(h, target, n_real, name):
    L, D = h.shape
    tm = _wide_tile(L)

    def body(h_ref, t_ref, dh_ref, sq_ref):
        i = pl.program_id(0)

        @pl.when(i == 0)
        def _():
            sq_ref[...] = jnp.zeros_like(sq_ref)

        t = i * tm + lax.broadcasted_iota(jnp.int32, (tm, D), 0)
        real = (t >= N_META) & (t < N_META + n_real)
        diff = jnp.where(real, h_ref[...] - t_ref[...], 0.0)
        dh_ref[...] = diff * (1.0 / D)
        sq_ref[...] += jnp.sum(diff * diff, axis=0, keepdims=True)

    return pl.pallas_call(
        body,
        grid=(L // tm,),
        in_specs=[_row_spec(tm, D), _row_spec(tm, D)],
        out_specs=[_row_spec(tm, D), _const_spec((1, D))],
        out_shape=[jax.ShapeDtypeStruct((L, D), F32), jax.ShapeDtypeStruct((1, D), F32)],
        compiler_params=_params("arbitrary"),
        name=name,
    )(h, target)


def _mesh_position():
    x, y, c = lax.axis_index("x"), lax.axis_index("y"), lax.axis_index("c")
    return x, y, c, 4 * x + 2 * y + c


def _flip(x, y, c, k):
    px = 1 - x if k & 4 else x
    py = 1 - y if k & 2 else y
    pc = 1 - c if k & 1 else c
    return (px, py, pc), 4 * px + 2 * py + pc


def _other_chips(x, y):
    return [(1 - x, y), (x, 1 - y), (1 - x, 1 - y)]


def _dev_index(px, py, pc):
    return 4 * px + 2 * py + pc


_ANY = pl.BlockSpec(memory_space=pl.ANY)


def cast_bf16(arrays, name):
    n = len(arrays)
    depth = arrays[0].shape[0]

    def body(*refs):
        for k, src in enumerate(refs[:n]):
            for l in range(depth):
                refs[n + k * depth + l][0] = src[l].astype(BF16)

    return pl.pallas_call(
        body,
        out_shape=[jax.ShapeDtypeStruct((1,) + a.shape[1:], BF16) for a in arrays for _ in range(depth)],
        compiler_params=pltpu.CompilerParams(vmem_limit_bytes=VMEM_LIMIT),
        name=name,
    )(*arrays)


def _gather_steps(srcs, dsts, send_sems, recv_sems, local_sems):
    n = len(srcs)
    x, y, c, me = _mesh_position()
    sibling = (x, y, 1 - c)
    chips = _other_chips(x, y)

    def copy(k, b, block, to, from_src=False):
        dst = dsts[b].at[block]
        return pltpu.make_async_remote_copy(
            src_ref=srcs[b] if from_src else dst, dst_ref=dst, send_sem=send_sems.at[k * n + b],
            recv_sem=recv_sems.at[k * n + b], device_id=to, device_id_type=pl.DeviceIdType.MESH)

    def own():
        return [pltpu.make_async_copy(srcs[b], dsts[b].at[me], local_sems.at[b]) for b in range(n)]

    def first():
        out = []
        for b in range(n):
            out.append(copy(0, b, me, sibling, from_src=True))
            out += [copy(1 + q, b, me, (*chip, c), from_src=True) for q, chip in enumerate(chips)]
        return out

    def passed():
        return [copy(4 + q, b, _dev_index(*chip, c), sibling) for q, chip in enumerate(chips) for b in range(n)]

    def start():
        for cp in own() + first():
            cp.start()

    def forward():
        for q, chip in enumerate(chips):
            for b in range(n):
                copy(1 + q, b, _dev_index(*chip, c), sibling).wait_recv()
        for cp in passed():
            cp.start()

    def finish():
        for b in range(n):
            copy(0, b, _dev_index(x, y, 1 - c), sibling).wait_recv()
            for q, chip in enumerate(chips):
                copy(4 + q, b, _dev_index(*chip, 1 - c), sibling).wait_recv()
        for cp in first() + passed():
            cp.wait_send()
        for cp in own():
            cp.wait()

    return start, forward, finish


def _gather_scratch(n):
    copies = N_DEV - 1
    return [pltpu.SemaphoreType.DMA((copies * n,)), pltpu.SemaphoreType.DMA((copies * n,)),
            pltpu.SemaphoreType.DMA((n,))]


def all_gather(payloads, name):
    n = len(payloads)

    def body(*refs):
        start, forward, finish = _gather_steps(refs[:n], refs[n:2 * n], *refs[2 * n:])
        start()
        forward()
        finish()

    return pl.pallas_call(
        body,
        in_specs=[_ANY] * n,
        out_specs=[_ANY] * n,
        out_shape=[jax.ShapeDtypeStruct((N_DEV,) + p.shape, p.dtype) for p in payloads],
        scratch_shapes=_gather_scratch(n),
        name=name,
    )(*payloads)


def exchange_pair(bigs, small, name):
    n = len(bigs)
    has_small = small is not None
    n_big = N_CHIPS * n
    n_sems = n_big + (N_DEV - 1 if has_small else 0)

    def body(*refs):
        big_refs = refs[:n]
        pos = n + has_small
        sib_refs = refs[pos:pos + n]
        send_sems, recv_sems, local_sem = refs[-3:]
        x, y, c, me = _mesh_position()
        sibling = (x, y, 1 - c)
        copies = []
        for b in range(n):
            for q in range(N_CHIPS):
                copies.append(pltpu.make_async_remote_copy(
                    src_ref=big_refs[b].at[_dev_index(q // 2, q % 2, 1 - c)], dst_ref=sib_refs[b].at[q],
                    send_sem=send_sems.at[b * N_CHIPS + q], recv_sem=recv_sems.at[b * N_CHIPS + q],
                    device_id=sibling, device_id_type=pl.DeviceIdType.MESH))
        waits = list(copies)
        if has_small:
            small_ref, gsmall_ref = refs[n], refs[pos + n]
            own = pltpu.make_async_copy(small_ref.at[me], gsmall_ref.at[me], local_sem.at[0])
            own.start()
            for k in range(1, N_DEV):
                peer, peer_idx = _flip(x, y, c, k)
                s = n_big + k - 1
                copies.append(pltpu.make_async_remote_copy(
                    src_ref=small_ref.at[peer_idx], dst_ref=gsmall_ref.at[me], send_sem=send_sems.at[s],
                    recv_sem=recv_sems.at[s], device_id=peer, device_id_type=pl.DeviceIdType.MESH))
                waits.append(pltpu.make_async_remote_copy(
                    src_ref=small_ref.at[peer_idx], dst_ref=gsmall_ref.at[peer_idx], send_sem=send_sems.at[s],
                    recv_sem=recv_sems.at[s], device_id=peer, device_id_type=pl.DeviceIdType.MESH))
        for cp in copies:
            cp.start()
        for cp in waits:
            cp.wait_recv()
        for cp in copies:
            cp.wait_send()
        if has_small:
            own.wait()

    out_shape = [jax.ShapeDtypeStruct((N_CHIPS,) + b.shape[1:], b.dtype) for b in bigs]
    args = list(bigs)
    if has_small:
        out_shape.append(jax.ShapeDtypeStruct(small.shape, small.dtype))
        args.append(small)
    return pl.pallas_call(
        body,
        in_specs=[_ANY] * len(args),
        out_specs=[_ANY] * len(out_shape),
        out_shape=out_shape,
        scratch_shapes=[pltpu.SemaphoreType.DMA((n_sems,)), pltpu.SemaphoreType.DMA((n_sems,)),
                        pltpu.SemaphoreType.DMA((1,))],
        name=name,
    )(*args)


def pair_sum(bigs, from_sibling, core, name):
    n = len(bigs)

    def body(core_ref, *refs):
        del core_ref
        for mine, sib, out in zip(refs[:n], refs[n:2 * n], refs[2 * n:]):
            out[...] = (mine[...].astype(F32) + sib[...].astype(F32)).astype(out.dtype)

    def slot(shape, picked):
        zeros = (0,) * (len(shape) - 1)
        if picked:
            return pl.BlockSpec((None,) + shape[1:], lambda q, core_ref: (2 * q + core_ref[0],) + zeros)
        return pl.BlockSpec((None,) + shape[1:], lambda q, core_ref: (q,) + zeros)

    grid_spec = pltpu.PrefetchScalarGridSpec(
        num_scalar_prefetch=1,
        grid=(N_CHIPS,),
        in_specs=[slot(b.shape, True) for b in bigs] + [slot(s.shape, False) for s in from_sibling],
        out_specs=[slot(s.shape, False) for s in from_sibling],
    )
    return pl.pallas_call(
        body,
        grid_spec=grid_spec,
        out_shape=[jax.ShapeDtypeStruct(s.shape, s.dtype) for s in from_sibling],
        compiler_params=_params("arbitrary"),
        name=name,
    )(core, *bigs, *from_sibling)


def _chip_exchange_steps(part_refs, got_refs, send_sems, recv_sems, local_sems):
    n = len(part_refs)
    x, y, c, me = _mesh_position()
    mine = 2 * x + y

    def copies(landing):
        out = []
        for q, (px, py) in enumerate(_other_chips(x, y)):
            theirs = 2 * px + py
            for b in range(n):
                out.append(pltpu.make_async_remote_copy(
                    src_ref=part_refs[b].at[theirs], dst_ref=got_refs[b].at[theirs if landing else mine],
                    send_sem=send_sems.at[q * n + b], recv_sem=recv_sems.at[q * n + b],
                    device_id=(px, py, c), device_id_type=pl.DeviceIdType.MESH))
        return out

    def own():
        return [pltpu.make_async_copy(part_refs[b].at[mine], got_refs[b].at[mine], local_sems.at[b])
                for b in range(n)]

    def start():
        for cp in own() + copies(False):
            cp.start()

    def finish():
        for cp in copies(True):
            cp.wait_recv()
        for cp in copies(False):
            cp.wait_send()
        for cp in own():
            cp.wait()

    return start, finish


def _chip_exchange_scratch(n):
    return [pltpu.SemaphoreType.DMA(((N_CHIPS - 1) * n,)), pltpu.SemaphoreType.DMA(((N_CHIPS - 1) * n,)),
            pltpu.SemaphoreType.DMA((n,))]


def _direct_gather_steps(srcs, dsts, send_sems, recv_sems, local_sems):
    n = len(srcs)
    x, y, c, me = _mesh_position()

    def copies(landing):
        out = []
        for k in range(1, N_DEV):
            peer, peer_idx = _flip(x, y, c, k)
            for b in range(n):
                out.append(pltpu.make_async_remote_copy(
                    src_ref=srcs[b], dst_ref=dsts[b].at[peer_idx if landing else me],
                    send_sem=send_sems.at[(k - 1) * n + b], recv_sem=recv_sems.at[(k - 1) * n + b],
                    device_id=peer, device_id_type=pl.DeviceIdType.MESH))
        return out

    def own():
        return [pltpu.make_async_copy(srcs[b], dsts[b].at[me], local_sems.at[b]) for b in range(n)]

    def start():
        for cp in own() + copies(False):
            cp.start()

    def finish():
        for cp in copies(True):
            cp.wait_recv()
        for cp in copies(False):
            cp.wait_send()
        for cp in own():
            cp.wait()

    return start, finish


def _direct_gather_scratch(n):
    return [pltpu.SemaphoreType.DMA(((N_DEV - 1) * n,)), pltpu.SemaphoreType.DMA(((N_DEV - 1) * n,)),
            pltpu.SemaphoreType.DMA((n,))]


def exchange_chips(partials, name):
    n = len(partials)

    def body(*refs):
        start, finish = _chip_exchange_steps(refs[:n], refs[n:2 * n], *refs[2 * n:])
        start()
        finish()

    return pl.pallas_call(
        body,
        in_specs=[_ANY] * n,
        out_specs=[_ANY] * n,
        out_shape=[jax.ShapeDtypeStruct(p.shape, p.dtype) for p in partials],
        scratch_shapes=_chip_exchange_scratch(n),
        name=name,
    )(*partials)


def _adamw_math(g, w, m, v):
    m_new = ADAM_B1 * m + (1.0 - ADAM_B1) * g
    v_new = ADAM_B2 * v + (1.0 - ADAM_B2) * (g * g)
    m_hat = m_new / (1.0 - ADAM_B1 ** ADAM_STEP)
    v_hat = v_new / (1.0 - ADAM_B2 ** ADAM_STEP)
    delta = -ADAM_LR * (m_hat / (jnp.sqrt(v_hat) + ADAM_EPS) + ADAM_WD * w)
    return delta, m_new, v_new


def _chip_total(p_ref):
    g = p_ref[0].astype(F32)
    for q in range(1, N_CHIPS):
        g = g + p_ref[q].astype(F32)
    return g


def chip_sum(parts, name):
    depth = len(parts)

    def body(*refs):
        for l in range(depth):
            refs[depth][l] = _chip_total(refs[l])

    return pl.pallas_call(
        body,
        out_shape=jax.ShapeDtypeStruct((depth,) + parts[0].shape[1:], F32),
        compiler_params=pltpu.CompilerParams(vmem_limit_bytes=VMEM_LIMIT),
        name=name,
    )(*parts)


def adamw_shard(parts, w, m, v, name, grad=None):
    depth = w.shape[0]
    n_in = 1 if grad is not None else depth

    def body(*refs):
        w_ref, m_ref, v_ref, g_ref, d_ref, mo_ref, vo_ref = refs[n_in:]
        for l in range(depth):
            g = refs[0][l] if grad is not None else _chip_total(refs[l])
            delta, m_new, v_new = _adamw_math(g, w_ref[l], m_ref[l], v_ref[l])
            g_ref[l] = g
            d_ref[l] = delta
            mo_ref[l] = m_new
            vo_ref[l] = v_new

    return pl.pallas_call(
        body,
        out_shape=[jax.ShapeDtypeStruct(w.shape, F32)] * 4,
        compiler_params=pltpu.CompilerParams(vmem_limit_bytes=VMEM_LIMIT),
        name=name,
    )(*([grad] if grad is not None else parts), w, m, v)


def adamw_packed(parts, w, m, v, name):
    n_slots = parts.shape[0]

    def body(p_ref, w_ref, m_ref, v_ref, g_ref, d_ref, mo_ref, vo_ref):
        g = p_ref[0]
        for s in range(1, n_slots):
            g = g + p_ref[s]
        delta, m_new, v_new = _adamw_math(g, w_ref[...], m_ref[...], v_ref[...])
        g_ref[...] = g
        d_ref[...] = delta
        mo_ref[...] = m_new
        vo_ref[...] = v_new

    return pl.pallas_call(
        body,
        out_shape=[jax.ShapeDtypeStruct(w.shape, F32)] * 4,
        compiler_params=pltpu.CompilerParams(vmem_limit_bytes=VMEM_LIMIT),
        name=name,
    )(parts, w, m, v)


SHARDED = ("w_in", "w_q_b", "w_kv_b", "w_out", "w_gate", "w_up", "w_down")
TRANSPOSED = ("w_in", "w_q_b", "w_gate", "w_up")
ADAM_TRANSPOSED = ("w_q_b", "w_gate", "w_up")
BEFORE_ATTENTION = ("w_in", "w_q_b", "w_kv_b")
BEFORE_ATTENTION_BWD = ("w_out", "w_gate", "w_up", "w_down")
REPLICATED = ("attn_norm_g", "pool_scale", "q_a_norm_g", "kv_a_norm_g", "q_norm_g", "k_norm_g", "ffn_norm_g")


def _pack_flat(arrays, rows):
    flat = jnp.concatenate([a.reshape(-1) for a in arrays])
    return jnp.pad(flat, (0, rows * LANES - flat.shape[0])).reshape(rows, LANES)


def _unpack_flat(packed, shapes):
    flat, out, at = packed.reshape(-1), [], 0
    for shp in shapes:
        n = 1
        for d in shp:
            n *= d
        out.append(flat[at:at + n].reshape(shp))
        at += n
    return out


def _rope_lane_tables(length):
    inv = 1.0 / (ROPE_THETA ** (jnp.arange(0, QK_ROPE_DIM, 2, dtype=F32) / QK_ROPE_DIM))
    ang = jnp.arange(length, dtype=F32)[:, None] * inv[None, :]
    cos, sin = jnp.cos(ang), jnp.sin(ang)
    ones = jnp.ones((length, QK_NOPE_DIM), F32)
    zeros = jnp.zeros((length, QK_NOPE_DIM), F32)
    tail = HEAD_PAD - QK_HEAD_DIM
    cosf = jnp.concatenate([ones, cos, cos, ones[:, :tail]], axis=1)
    sins = jnp.concatenate([zeros, -sin, sin, zeros[:, :tail]], axis=1)
    return cosf, sins


def _pad_lanes(vec, at, width=HEAD_PAD):
    return jnp.pad(vec, (at, width - at - vec.shape[0])).reshape(1, width)


def kernel(x, meta_tokens, attn_norm_g, w_in, w_pool, pool_scale, q_a_norm_g, w_q_b, kv_a_norm_g, w_kv_b, q_norm_g, k_norm_g, w_out, ffn_norm_g, w_gate, w_up, w_down, loss_target, m_meta_tokens, m_attn_norm_g, m_w_in, m_w_pool, m_pool_scale, m_q_a_norm_g, m_w_q_b, m_kv_a_norm_g, m_w_kv_b, m_q_norm_g, m_k_norm_g, m_w_out, m_ffn_norm_g, m_w_gate, m_w_up, m_w_down, v_meta_tokens, v_attn_norm_g, v_w_in, v_w_pool, v_pool_scale, v_q_a_norm_g, v_w_q_b, v_kv_a_norm_g, v_w_kv_b, v_q_norm_g, v_k_norm_g, v_w_out, v_ffn_norm_g, v_w_gate, v_w_up, v_w_down):
    weights = dict(meta_tokens=meta_tokens, attn_norm_g=attn_norm_g, w_in=w_in, w_pool=w_pool, pool_scale=pool_scale,
                   q_a_norm_g=q_a_norm_g, w_q_b=w_q_b, kv_a_norm_g=kv_a_norm_g, w_kv_b=w_kv_b, q_norm_g=q_norm_g,
                   k_norm_g=k_norm_g, w_out=w_out, ffn_norm_g=ffn_norm_g, w_gate=w_gate, w_up=w_up, w_down=w_down)
    mom1 = dict(meta_tokens=m_meta_tokens, attn_norm_g=m_attn_norm_g, w_in=m_w_in, w_pool=m_w_pool,
                pool_scale=m_pool_scale, q_a_norm_g=m_q_a_norm_g, w_q_b=m_w_q_b, kv_a_norm_g=m_kv_a_norm_g,
                w_kv_b=m_w_kv_b, q_norm_g=m_q_norm_g, k_norm_g=m_k_norm_g, w_out=m_w_out, ffn_norm_g=m_ffn_norm_g,
                w_gate=m_w_gate, w_up=m_w_up, w_down=m_w_down)
    mom2 = dict(meta_tokens=v_meta_tokens, attn_norm_g=v_attn_norm_g, w_in=v_w_in, w_pool=v_w_pool,
                pool_scale=v_pool_scale, q_a_norm_g=v_q_a_norm_g, w_q_b=v_w_q_b, kv_a_norm_g=v_kv_a_norm_g,
                w_kv_b=v_w_kv_b, q_norm_g=v_q_norm_g, k_norm_g=v_k_norm_g, w_out=v_w_out, ffn_norm_g=v_ffn_norm_g,
                w_gate=v_w_gate, w_up=v_w_up, w_down=v_w_down)
    order = ("meta_tokens", "attn_norm_g", "w_in", "w_pool", "pool_scale", "q_a_norm_g", "w_q_b", "kv_a_norm_g",
             "w_kv_b", "q_norm_g", "k_norm_g", "w_out", "ffn_norm_g", "w_gate", "w_up", "w_down")
    depth = w_in.shape[0]
    seq = x.shape[1]
    length = N_META + seq
    lp = -(-length // ROW_TILE) * ROW_TILE
    in_cols = w_in.shape[2]

    local = {n: (jnp.swapaxes(weights[n], 1, 2) if n in TRANSPOSED else weights[n]) for n in SHARDED}
    cast = cast_bf16([local[n] for n in SHARDED], "cast_weights")
    shards = [{n: cast[k * depth + l] for k, n in enumerate(SHARDED)} for l in range(depth)]
    gathered = all_gather([shards[0][n] for n in BEFORE_ATTENTION] + [meta_tokens], "all_gather")
    g8l = [dict(zip(BEFORE_ATTENTION, gathered[:-1]))] + [{} for _ in range(depth - 1)]
    meta_full = jnp.transpose(gathered[-1], (1, 0, 2)).reshape(N_META, D_MODEL)

    s1, s2, s3 = POOL_WIDTH, POOL_WIDTH + Q_LORA_RANK, POOL_WIDTH + Q_LORA_RANK + KV_LORA_RANK
    zpad = lambda n: jnp.zeros((1, n, D_MODEL), BF16)

    def padded_in_proj(w8):
        w_in_t = jnp.swapaxes(w8, 0, 1).reshape(1, N_DEV * in_cols, D_MODEL)
        return jnp.concatenate([w_in_t[:, :s3], zpad(QK_NOPE_DIM), w_in_t[:, s3:],
                                zpad(HEAD_PAD - QK_HEAD_DIM)], axis=1)

    w_in_ps = [None] * depth
    w_pool_b = w_pool.astype(BF16)

    cosf, sins = _rope_lane_tables(lp)
    seg = _segment_matrix()
    row = lambda a, l: a[l].reshape(1, -1)

    h = jnp.concatenate([meta_full, x[0], jnp.zeros((lp - length, D_MODEL), F32)], axis=0)
    target = jnp.pad(loss_target[0], ((N_META, lp - length), (0, 0)))
    saved = []
    for l in range(depth):
        gq = _pad_lanes(q_norm_g[l], 0)
        gkn = _pad_lanes(k_norm_g[l, :QK_NOPE_DIM], 0)
        gkr = _pad_lanes(k_norm_g[l, QK_NOPE_DIM:], QK_NOPE_DIM)
        g8 = g8l[l]
        w_in_ps[l] = padded_in_proj(g8["w_in"])
        a, u, c_q, c_kv, kr = norm_mm(
            h, row(attn_norm_g, l), w_in_ps[l], 0,
            [(0, s1), (s1, Q_LORA_RANK), (s2, KV_LORA_RANK), (s3, HEAD_PAD)], [F32] * 4, "in_proj")
        cat = pool_fwd(u, w_pool_b[l], row(pool_scale, l), "pool_fwd")
        qn, q = norm_mm_heads(c_q, row(q_a_norm_g, l), g8["w_q_b"], 0, "q_proj", transposed=True)
        kvn, kv = norm_mm_heads(c_kv, row(kv_a_norm_g, l), g8["w_kv_b"], 0, "kv_proj")
        qp, kp, v = qk_prep_fwd(q, kv, kr, cosf, sins, seg, gq, gkn, gkr, "qk_prep_fwd")
        riders = [(l, n) for n in SHARDED if n not in g8]
        riders += [(l + 1, n) for n in BEFORE_ATTENTION if l + 1 < depth]
        o, lse, cat, *arrived = attn_fwd(qp, kp, v, cat, "attn_fwd_gather", gather=[shards[k][n] for k, n in riders])
        for (k, n), w8 in zip(riders, arrived):
            g8l[k][n] = w8
        h_mid, g = mm_res(cat, g8["w_out"], h, 0, "out_proj", gamma=row(ffn_norm_g, l))
        gate, up, act = ffn_up(g, g8["w_gate"], g8["w_up"], 0, "ffn_up")
        h_next = mm_res(act, g8["w_down"], h_mid, 0, "ffn_down")[0]
        saved.append(dict(h=h, a=a, u=u, c_q=c_q, c_kv=c_kv, kr=kr, qn=qn, q=q, kvn=kvn, kv=kv, v=v, qp=qp, kp=kp,
                          o=o, lse=lse, cat=cat, h_mid=h_mid, g=g, gate=gate, up=up, act=act,
                          gq=gq, gkn=gkn, gkr=gkr))
        h = h_next

    dh, sq = loss_head(h, target, seq, "loss_head")
    loss = lax.psum(0.5 / D_MODEL * jnp.sum(sq), ("x", "y", "c"))

    core = lax.axis_index("c").astype(jnp.int32).reshape(1)
    small_grads = {n: [None] * depth for n in REPLICATED + ("w_pool",)}
    pending = []
    got = [{} for _ in range(depth)]

    def pair_reduce(l, names, slots, small):
        bigs = [slots[n] for n in names]
        if small is None:
            from_sibling, got_small = exchange_pair(bigs, None, "grad_exchange_pair"), None
        else:
            *from_sibling, got_small = exchange_pair(bigs, small, "grad_exchange_pair_small")
        partial = pair_sum(bigs, list(from_sibling), core, "grad_pair_sum")
        return [(l, n, p) for n, p in zip(names, partial)], got_small

    for l in reversed(range(depth)):
        s = saved[l]
        g8 = g8l[l]
        w_in_p = w_in_ps[l]
        slots = {}
        dgate, dup = ffn_bwd_act(dh, g8["w_down"], 0, s["gate"], s["up"], "ffn_bwd_act")
        slot_of = lambda n, full: full.reshape(g8[n].shape[:1] + g8[n].shape[2:])
        slots["w_down"] = slot_of("w_down", mm_tn([s["act"]], dh, "dw_down", out_dtype=BF16)[0])
        dw_gate_t, dw_up_t = mm_tn([dgate, dup], s["g"], "dw_gate_up", out_dtype=BF16)
        slots["w_gate"] = slot_of("w_gate", dw_gate_t)
        slots["w_up"] = slot_of("w_up", dw_up_t)
        dh_mid, dg_ffn = mm_nt_normbwd([("gathered", dgate, g8["w_gate"]), ("gathered", dup, g8["w_up"])],
                                       s["h_mid"], row(ffn_norm_g, l), dh, 0, "ffn_bwd_in", wide=False)
        small_grads["ffn_norm_g"][l] = dg_ffn[0]
        slots["w_out"] = slot_of("w_out", mm_tn([s["cat"]], dh_mid, "dw_out", out_dtype=BF16)[0])
        pending += pair_reduce(l, BEFORE_ATTENTION_BWD, slots, None)[0]
        dy_pool, do, delta = out_proj_bwd(dh_mid, g8["w_out"], 0, s["o"], "out_proj_bwd")
        du, dw_pool, dscale = pool_bwd(s["u"], dy_pool, w_pool_b[l], row(pool_scale, l), "pool_bwd")
        small_grads["w_pool"][l] = dw_pool
        small_grads["pool_scale"][l] = dscale[0]
        pool_rider = [jnp.stack(small_grads["w_pool"], axis=0).reshape(-1, LANES)] if l == 0 else []
        dqp, dkp, dv, *arrived = attn_bwd(s["qp"], s["kp"], s["v"], do, s["lse"], delta, "attn_bwd_exchange",
                                          exchange=[p for _, _, p in pending], gather=pool_rider)
        if pool_rider:
            got_pool = arrived.pop()
        for (k, n, _), parts in zip(pending, arrived):
            got[k][n] = parts
        dq, dkv, dkr, dgq, dgkn, dgkr = qk_prep_bwd(dqp, dkp, dv, s["q"], s["kv"], s["kr"], cosf, sins, seg,
                                                    s["gq"], s["gkn"], s["gkr"], "qk_prep_bwd")
        small_grads["q_norm_g"][l] = dgq[0, :QK_HEAD_DIM]
        small_grads["k_norm_g"][l] = jnp.concatenate([dgkn[0, :QK_NOPE_DIM], dgkr[0, QK_NOPE_DIM:QK_HEAD_DIM]])
        heads_first = lambda full: jnp.swapaxes(full.reshape(full.shape[0], MLA_HEADS, HEAD_PAD), 0, 1)
        dw_q_t = mm_tn([dq], s["qn"], "dw_q", out_dtype=BF16)[0]
        slots["w_q_b"] = dw_q_t.reshape(MLA_HEADS, HEAD_PAD, -1)[:, :QK_HEAD_DIM]
        dc_q, dg_qa = mm_nt_normbwd([("heads_t", dq, g8["w_q_b"])], s["c_q"], row(q_a_norm_g, l), None, 0,
                                    "q_proj_bwd")
        small_grads["q_a_norm_g"][l] = dg_qa[0]
        slots["w_kv_b"] = heads_first(mm_tn([s["kvn"]], dkv, "dw_kv", out_dtype=BF16)[0])
        dc_kv, dg_kva = mm_nt_normbwd([("heads", dkv, g8["w_kv_b"])], s["c_kv"], row(kv_a_norm_g, l), None, 0,
                                      "kv_proj_bwd")
        small_grads["kv_a_norm_g"][l] = dg_kva[0]
        dw_pool_t, dw_q_t, dw_kv_t, dw_rope_t = mm_tn([du, dc_q, dc_kv, dkr], s["a"], "dw_in", out_dtype=BF16)
        dw_in_t = jnp.concatenate([dw_pool_t, dw_q_t, dw_kv_t, dw_rope_t[QK_NOPE_DIM:QK_HEAD_DIM]], axis=0)
        slots["w_in"] = slot_of("w_in", dw_in_t)
        dh, dg_attn = mm_nt_normbwd(
            [("rows", du, w_in_p, 0), ("rows", dc_q, w_in_p, s1), ("rows", dc_kv, w_in_p, s2),
             ("rows", dkr, w_in_p, s3)],
            s["h"], row(attn_norm_g, l), dh_mid, 0, "in_proj_bwd")
        small_grads["attn_norm_g"][l] = dg_attn[0]

        small_slots = None
        if l == 0:
            rep_shapes = [weights[n].shape for n in REPLICATED]
            rep_count = sum(int(jnp.size(weights[n])) for n in REPLICATED)
            rep_rows = -(-rep_count // (8 * LANES)) * 8
            rep_packed = _pack_flat([jnp.stack(small_grads[n], axis=0) for n in REPLICATED], rep_rows)
            meta_slots = jnp.transpose(dh[:N_META].reshape(N_META, N_DEV, LANES), (1, 0, 2))
            small_slots = jnp.concatenate(
                [meta_slots, jnp.broadcast_to(rep_packed[None], (N_DEV, rep_rows, LANES))], axis=1)
        pending, got_small = pair_reduce(l, [n for n in SHARDED if n not in BEFORE_ATTENTION_BWD], slots,
                                         small_slots)
    for (k, n, _), parts in zip(pending, exchange_chips([p for _, _, p in pending], "grad_exchange_chips")):
        got[k][n] = parts

    grad_x = dh[N_META:length][None]

    per = [{} for _ in range(4)]
    for n in SHARDED:
        parts = [got[l][n] for l in range(depth)]
        if n in ADAM_TRANSPOSED:
            t = lambda a: jnp.swapaxes(a, 1, 2)
            outs = [t(o) for o in adamw_shard(parts, local[n], t(mom1[n]), t(mom2[n]), "adamw_" + n)]
        elif n in TRANSPOSED:
            grad = jnp.swapaxes(chip_sum(parts, "chip_sum_" + n), 1, 2)
            outs = adamw_shard(None, weights[n], mom1[n], mom2[n], "adamw_" + n, grad=grad)
        else:
            outs = adamw_shard(parts, weights[n], mom1[n], mom2[n], "adamw_" + n)
        for k in range(4):
            per[k][n] = outs[k]
    ps = lambda src: jnp.concatenate(
        [src["meta_tokens"], _pack_flat([src[n] for n in REPLICATED], rep_rows)], axis=0)
    small_out = adamw_packed(got_small, ps(weights), ps(mom1), ps(mom2), "adamw_small")
    as_rows = lambda a: a.reshape(-1, LANES)
    pool_out = adamw_packed(got_pool, as_rows(w_pool), as_rows(m_w_pool), as_rows(v_w_pool), "adamw_w_pool")
    for k in range(4):
        per[k]["meta_tokens"] = small_out[k][:N_META]
        per[k].update(zip(REPLICATED, _unpack_flat(small_out[k][N_META:], rep_shapes)))
        per[k]["w_pool"] = pool_out[k].reshape(w_pool.shape)
    return (loss, grad_x, *[per[0][n] for n in order], *[per[1][n] for n in order],
            *[per[2][n] for n in order], *[per[3][n] for n in order])
```

```python
import functools

import jax
import jax.numpy as jnp
from jax import lax
from jax.experimental import pallas as pl
from jax.experimental.pallas import tpu as pltpu

F32 = jnp.float32
BF16 = jnp.bfloat16

D_MODEL = 1024
N_META = 16
POOL_WIDTH = 512
POOL_WINDOWS = (2, 4, 8, 16)
POOL_GROUP_DIM = 128
POOL_HALO = 16
MLA_HEADS = 8
QK_NOPE_DIM = 64
QK_ROPE_DIM = 32
QK_HEAD_DIM = 96
V_HEAD_DIM = 64
HEAD_PAD = 128
Q_LORA_RANK = 384
KV_LORA_RANK = 256
ROPE_THETA = 10000.0
RMS_EPS = 1e-6
ATTN_SCALE = QK_HEAD_DIM ** -0.5
LOG2_E = 1.4426950408889634
SCORE_SCALE = ATTN_SCALE * LOG2_E

ADAM_LR = 0.001
ADAM_B1 = 0.9
ADAM_B2 = 0.999
ADAM_EPS = 1e-08
ADAM_WD = 0.01
ADAM_STEP = 10

N_DEV = 8
N_CHIPS = 4
LANES = 128
ROW_TILE = 384
LONG_TILE = 1056
WIDE_TILE = 1056
VMEM_LIMIT = 56 * 1024 * 1024


def _params(*sem):
    return pltpu.CompilerParams(dimension_semantics=sem, vmem_limit_bytes=VMEM_LIMIT)


def _wide_tile(rows):
    return WIDE_TILE if rows % WIDE_TILE == 0 else ROW_TILE


def _row_spec(tile, width):
    return pl.BlockSpec((tile, width), lambda i: (i, 0))


def _const_spec(shape):
    return pl.BlockSpec(shape, lambda i: tuple(0 for _ in shape))


def _layer_spec(w, layer):
    return pl.BlockSpec((None,) + w.shape[1:], lambda *_: (layer, 0, 0))


def _gathered_spec(w8, layer):
    return pl.BlockSpec((N_DEV, None) + w8.shape[2:], lambda *_: (0, layer, 0, 0))


def _nt(a, b):
    return lax.dot_general(a, b, (((1,), (1,)), ((), ())), preferred_element_type=F32)


def _tn(a, b):
    return lax.dot_general(a, b, (((0,), (0,)), ((), ())), preferred_element_type=F32)


def _nn(a, b):
    return jnp.dot(a, b, preferred_element_type=F32)


def _silu(g):
    return g * (1.0 / (1.0 + jnp.exp(-g)))


def _rms(xf):
    return lax.rsqrt(jnp.mean(xf * xf, axis=-1, keepdims=True) + RMS_EPS)


def norm_mm(x, gamma, wt, layer, splits, dtypes, name):
    L, K = x.shape
    tm = _wide_tile(L)

    def body(x_ref, g_ref, w_ref, a_ref, *z_refs):
        xf = x_ref[...]
        a = ((xf * _rms(xf)) * g_ref[...]).astype(BF16)
        a_ref[...] = a
        z = _nt(a, w_ref[...])
        for (s, n), zr in zip(splits, z_refs):
            zr[...] = z[:, s:s + n].astype(zr.dtype)

    widths = [n for _, n in splits]
    return pl.pallas_call(
        body,
        grid=(L // tm,),
        in_specs=[_row_spec(tm, K), _const_spec((1, K)), _layer_spec(wt, layer)],
        out_specs=[_row_spec(tm, K)] + [_row_spec(tm, n) for n in widths],
        out_shape=[jax.ShapeDtypeStruct((L, K), BF16)]
        + [jax.ShapeDtypeStruct((L, n), dt) for n, dt in zip(widths, dtypes)],
        compiler_params=_params("arbitrary"),
        name=name,
    )(x, gamma, wt)


def norm_mm_heads(x, gamma, w8, layer, name, transposed=False):
    L, K = x.shape
    hw = w8.shape[-2] if transposed else w8.shape[-1]
    tm = _wide_tile(L)

    def body(x_ref, g_ref, w_ref, a_ref, z_ref):
        xf = x_ref[...]
        a = ((xf * _rms(xf)) * g_ref[...]).astype(BF16)
        a_ref[...] = a
        if hw < HEAD_PAD:
            z_ref[...] = jnp.zeros_like(z_ref)
        for j in range(MLA_HEADS):
            z_ref[:, j * HEAD_PAD:j * HEAD_PAD + hw] = _nt(a, w_ref[j]) if transposed else _nn(a, w_ref[j])

    return pl.pallas_call(
        body,
        grid=(L // tm,),
        in_specs=[_row_spec(tm, K), _const_spec((1, K)), _gathered_spec(w8, layer)],
        out_specs=[_row_spec(tm, K), _row_spec(tm, MLA_HEADS * HEAD_PAD)],
        out_shape=[jax.ShapeDtypeStruct((L, K), BF16), jax.ShapeDtypeStruct((L, MLA_HEADS * HEAD_PAD), F32)],
        compiler_params=_params("arbitrary"),
        name=name,
    )(x, gamma, w8)


FF_GROUP = 4


def _ff_spec(w8, layer):
    return pl.BlockSpec((FF_GROUP, None) + w8.shape[2:], lambda j, i: (j, layer, 0, 0))


def ffn_up(g, w_gate8, w_up8, layer, name):
    L, K = g.shape
    fb = w_gate8.shape[-2]
    tm, tf = _wide_tile(L), FF_GROUP * fb

    def body(a_ref, wg_ref, wu_ref, gate_ref, up_ref, act_ref):
        a = a_ref[...]
        gate = _nt(a, wg_ref[...].reshape(tf, K))
        up = _nt(a, wu_ref[...].reshape(tf, K))
        gate_ref[...] = gate.astype(BF16)
        up_ref[...] = up.astype(BF16)
        act_ref[...] = (_silu(gate) * up).astype(BF16)

    tile = pl.BlockSpec((tm, tf), lambda j, i: (i, j))
    F = N_DEV * fb
    return pl.pallas_call(
        body,
        grid=(N_DEV // FF_GROUP, L // tm),
        in_specs=[pl.BlockSpec((tm, K), lambda j, i: (i, 0)), _ff_spec(w_gate8, layer), _ff_spec(w_up8, layer)],
        out_specs=[tile, tile, tile],
        out_shape=[
            jax.ShapeDtypeStruct((L, F), BF16),
            jax.ShapeDtypeStruct((L, F), BF16),
            jax.ShapeDtypeStruct((L, F), BF16),
        ],
        compiler_params=_params("arbitrary", "arbitrary"),
        name=name,
    )(g, w_gate8, w_up8)


def mm_res(a, w8, res, layer, name, gamma=None):
    L = a.shape[0]
    kb, N = w8.shape[-2:]
    tm = _wide_tile(L)
    normed = gamma is not None

    def body(a_ref, w_ref, r_ref, *rest):
        out = r_ref[...] + _nn(a_ref[...], w_ref[...].reshape(N_DEV * kb, N))
        if normed:
            g_ref, o_ref, n_ref = rest
            n_ref[...] = ((out * _rms(out)) * g_ref[...]).astype(BF16)
        else:
            (o_ref,) = rest
        o_ref[...] = out

    in_specs = [_row_spec(tm, N_DEV * kb), _gathered_spec(w8, layer), _row_spec(tm, N)]
    out_specs = [_row_spec(tm, N)]
    out_shape = [jax.ShapeDtypeStruct((L, N), F32)]
    args = [a, w8, res]
    if normed:
        in_specs.append(_const_spec((1, N)))
        out_specs.append(_row_spec(tm, N))
        out_shape.append(jax.ShapeDtypeStruct((L, N), BF16))
        args.append(gamma)
    return pl.pallas_call(
        body,
        grid=(L // tm,),
        in_specs=in_specs,
        out_specs=out_specs,
        out_shape=out_shape,
        compiler_params=_params("arbitrary"),
        name=name,
    )(*args)


def out_proj_bwd(dz, w8, layer, o, name):
    L, N = dz.shape
    kb = w8.shape[-2]
    C = o.shape[1]
    tm = _wide_tile(L)

    def body(dz_ref, w_ref, o_ref, dy_ref, do_ref, dl_ref):
        d = _nt(dz_ref[...].astype(BF16), w_ref[...].reshape(N_DEV * kb, N))
        dy_ref[...] = d[:, :C]
        do_ref[...] = d[:, C:]
        left = lax.broadcasted_iota(jnp.int32, (tm, LANES), 1) < V_HEAD_DIM
        for p in range(C // LANES):
            cols = slice(p * LANES, (p + 1) * LANES)
            prod = d[:, C + p * LANES:C + (p + 1) * LANES] * o_ref[:, cols]
            d0 = jnp.sum(jnp.where(left, prod, 0.0), axis=1, keepdims=True)
            d1 = jnp.sum(jnp.where(left, 0.0, prod), axis=1, keepdims=True)
            dl_ref[:, cols] = jnp.where(left, d0, d1)

    return pl.pallas_call(
        body,
        grid=(L // tm,),
        in_specs=[_row_spec(tm, N), _gathered_spec(w8, layer), _row_spec(tm, C)],
        out_specs=[_row_spec(tm, C)] * 3,
        out_shape=[jax.ShapeDtypeStruct((L, C), F32)] * 3,
        compiler_params=_params("arbitrary"),
        name=name,
    )(dz, w8, o)


def mm_nt_normbwd(terms, x, gamma, dres, layer, name, wide=True):
    L, K = x.shape
    tm = _wide_tile(L) if wide else ROW_TILE
    n_terms = len(terms)
    has_res = dres is not None
    weights = []
    for t in terms:
        if not any(t[2] is u for u in weights):
            weights.append(t[2])
    which = [[t[2] is u for u in weights].index(True) for t in terms]
    n_in = n_terms + len(weights)

    def body(*refs):
        dz_refs = refs[:n_terms]
        w_refs = [refs[n_terms + n] for n in which]
        x_ref, g_ref = refs[n_in], refs[n_in + 1]
        pos = n_in + 2
        r_ref = refs[pos] if has_res else None
        dx_ref, dg_ref = refs[pos + has_res], refs[pos + has_res + 1]
        da = None
        for t, dz_ref, w_ref in zip(terms, dz_refs, w_refs):
            if t[0] == "rows":
                n, at = t[1].shape[1], t[3]
                parts = [_nn(dz_ref[...].astype(BF16), w_ref[at:at + n, :])]
            elif t[0] == "heads":
                hw = t[2].shape[-1]
                parts = [_nt(dz_ref[:, j * HEAD_PAD:j * HEAD_PAD + hw].astype(BF16), w_ref[j])
                         for j in range(MLA_HEADS)]
            elif t[0] == "heads_t":
                hw = t[2].shape[-2]
                parts = [_nn(dz_ref[:, j * HEAD_PAD:j * HEAD_PAD + hw].astype(BF16), w_ref[j])
                         for j in range(MLA_HEADS)]
            else:
                nb = t[2].shape[-2]
                parts = [_nn(dz_ref[...].astype(BF16), w_ref[...].reshape(N_DEV * nb, K))]
            for p in parts:
                da = p if da is None else da + p
        xf = x_ref[...]
        r = _rms(xf)
        xh = xf * r

        @pl.when(pl.program_id(0) == 0)
        def _():
            dg_ref[...] = jnp.zeros_like(dg_ref)

        dg_ref[...] += jnp.sum(da * xh, axis=0, keepdims=True)
        dxh = da * g_ref[...]
        dx = r * (dxh - xh * jnp.mean(dxh * xh, axis=-1, keepdims=True))
        if has_res:
            dx = dx + r_ref[...]
        dx_ref[...] = dx

    in_specs = [_row_spec(tm, t[1].shape[1]) for t in terms]
    for w in weights:
        in_specs.append(_layer_spec(w, layer) if w.ndim == 3 else _gathered_spec(w, layer))
    in_specs += [_row_spec(tm, K), _const_spec((1, K))]
    args = [t[1] for t in terms] + weights + [x, gamma]
    if has_res:
        in_specs.append(_row_spec(tm, K))
        args.append(dres)
    return pl.pallas_call(
        body,
        grid=(L // tm,),
        in_specs=in_specs,
        out_specs=[_row_spec(tm, K), _const_spec((1, K))],
        out_shape=[jax.ShapeDtypeStruct((L, K), F32), jax.ShapeDtypeStruct((1, K), F32)],
        compiler_params=_params("arbitrary"),
        name=name,
    )(*args)


MAX_OUT_ROWS = 1408


def mm_tn(a_list, b, name, out_dtype=F32):
    n = len(a_list)
    L, N = b.shape
    tks = [MAX_OUT_ROWS if (a.shape[1] > MAX_OUT_ROWS and a.shape[1] % MAX_OUT_ROWS == 0) else a.shape[1]
           for a in a_list]
    blocks = a_list[0].shape[1] // tks[0]
    assert all(a.shape[1] // tk == blocks for a, tk in zip(a_list, tks))
    tl = LONG_TILE if L % LONG_TILE == 0 else ROW_TILE
    n_l = L // tl

    def body(*refs):
        a_refs, b_ref = refs[:n], refs[n]
        o_refs, accs = refs[n + 1:2 * n + 1], refs[2 * n + 1:]
        l = pl.program_id(1)
        bt = b_ref[...].astype(BF16)
        for a_ref, o_ref, acc in zip(a_refs, o_refs, accs):
            @pl.when(l == 0)
            def _(acc=acc):
                acc[...] = jnp.zeros_like(acc)

            acc[...] += _tn(a_ref[...].astype(BF16), bt)

            @pl.when(l == n_l - 1)
            def _(acc=acc, o_ref=o_ref):
                o_ref[...] = acc[...].astype(o_ref.dtype)

    return pl.pallas_call(
        body,
        grid=(blocks, n_l),
        in_specs=[pl.BlockSpec((tl, tk), lambda j, l: (l, j)) for tk in tks]
        + [pl.BlockSpec((tl, N), lambda j, l: (l, 0))],
        out_specs=[pl.BlockSpec((tk, N), lambda j, l: (j, 0)) for tk in tks],
        out_shape=[jax.ShapeDtypeStruct((a.shape[1], N), out_dtype) for a in a_list],
        scratch_shapes=[pltpu.VMEM((tk, N), F32) for tk in tks],
        compiler_params=_params("arbitrary", "arbitrary"),
        name=name,
    )(*a_list, b)


def ffn_bwd_act(dh, w_down8, layer, gate, up, name):
    L, K = dh.shape
    fb = w_down8.shape[-2]
    tm, tf = _wide_tile(L), FF_GROUP * fb

    def body(dh_ref, w_ref, gate_ref, up_ref, dgate_ref, dup_ref):
        dact = _nt(dh_ref[...].astype(BF16), w_ref[...].reshape(tf, K))
        g = gate_ref[...].astype(F32)
        sig = 0.5 * jnp.tanh(0.5 * g) + 0.5
        dup_ref[...] = (dact * (g * sig)).astype(BF16)
        dgate_ref[...] = (dact * up_ref[...].astype(F32) * (sig * (1.0 + g * (1.0 - sig)))).astype(BF16)

    tile = pl.BlockSpec((tm, tf), lambda j, i: (i, j))
    F = N_DEV * fb
    return pl.pallas_call(
        body,
        grid=(N_DEV // FF_GROUP, L // tm),
        in_specs=[pl.BlockSpec((tm, K), lambda j, i: (i, 0)), _ff_spec(w_down8, layer), tile, tile],
        out_specs=[tile, tile],
        out_shape=[jax.ShapeDtypeStruct((L, F), BF16), jax.ShapeDtypeStruct((L, F), BF16)],
        compiler_params=_params("arbitrary", "arbitrary"),
        name=name,
    )(dh, w_down8, gate, up)


def _pool_residual(scr, lo, tm, g, w, t):
    cols = slice(g * POOL_GROUP_DIM, (g + 1) * POOL_GROUP_DIM)
    cur = scr[lo:lo + tm, cols]
    s = cur
    for k in range(1, w):
        s = s + scr[lo - k:lo - k + tm, cols]
    cnt = jnp.minimum(t + 1, w).astype(F32)
    return s / cnt - cur


def pool_fwd(u, w_pool, scale, name):
    L, C = u.shape
    tm, halo = _wide_tile(L), POOL_HALO

    def body(u_ref, halo_ref, w_ref, s_ref, y_ref, scr):
        i = pl.program_id(0)
        scr[0:halo, :] = jnp.where(i > 0, halo_ref[...], 0.0)
        scr[halo:halo + tm, :] = u_ref[...]
        t = i * tm + lax.broadcasted_iota(jnp.int32, (tm, POOL_GROUP_DIM), 0)
        for g, w in enumerate(POOL_WINDOWS):
            cols = slice(g * POOL_GROUP_DIM, (g + 1) * POOL_GROUP_DIM)
            p = _pool_residual(scr, halo, tm, g, w, t)
            y = _nn(p.astype(BF16), w_ref[g]) * s_ref[:, cols]
            y_ref[:, cols] = y.astype(y_ref.dtype)

    return pl.pallas_call(
        body,
        grid=(L // tm,),
        in_specs=[
            _row_spec(tm, C),
            pl.BlockSpec((halo, C), lambda i: (jnp.maximum(i * (tm // halo) - 1, 0), 0)),
            _const_spec(w_pool.shape),
            _const_spec((1, C)),
        ],
        out_specs=_row_spec(tm, C),
        out_shape=jax.ShapeDtypeStruct((L, 2 * C), BF16),
        scratch_shapes=[pltpu.VMEM((tm + halo, C), F32)],
        compiler_params=_params("arbitrary"),
        name=name,
    )(u, u, w_pool, scale)


def pool_bwd(u, dy, w_pool, scale, name):
    L, C = u.shape
    tm, halo = _wide_tile(L), POOL_HALO
    n_tiles = L // tm
    last_halo = L // halo - 1

    def body(u_ref, uh_ref, dy_ref, dyh_ref, w_ref, s_ref, du_ref, dw_ref, ds_ref, scr_u, scr_q):
        i = pl.program_id(0)

        @pl.when(i == 0)
        def _():
            dw_ref[...] = jnp.zeros_like(dw_ref)
            ds_ref[...] = jnp.zeros_like(ds_ref)

        scr_u[0:halo, :] = jnp.where(i > 0, uh_ref[...], 0.0)
        scr_u[halo:halo + tm, :] = u_ref[...]
        t = i * tm + lax.broadcasted_iota(jnp.int32, (tm, POOL_GROUP_DIM), 0)
        th = (i + 1) * tm + lax.broadcasted_iota(jnp.int32, (halo, POOL_GROUP_DIM), 0)
        for g, w in enumerate(POOL_WINDOWS):
            cols = slice(g * POOL_GROUP_DIM, (g + 1) * POOL_GROUP_DIM)
            p = _pool_residual(scr_u, halo, tm, g, w, t).astype(BF16)
            wg = w_ref[g]
            sc = s_ref[:, cols]
            dy = dy_ref[:, cols]
            ds_ref[:, cols] += jnp.sum(dy * _nn(p, wg), axis=0, keepdims=True)
            dys = (dy * sc).astype(BF16)
            dw_ref[g] += _tn(p, dys)
            dp = _nt(dys, wg)
            dyh = jnp.where(i < n_tiles - 1, dyh_ref[:, cols], 0.0)
            dph = _nt((dyh * sc).astype(BF16), wg)
            scr_q[0:tm, cols] = dp / jnp.minimum(t + 1, w).astype(F32)
            scr_q[tm:tm + halo, cols] = dph / jnp.minimum(th + 1, w).astype(F32)
            acc = scr_q[0:tm, cols]
            for k in range(1, w):
                acc = acc + scr_q[k:k + tm, cols]
            du_ref[:, cols] = acc - dp

    return pl.pallas_call(
        body,
        grid=(n_tiles,),
        in_specs=[
            _row_spec(tm, C),
            pl.BlockSpec((halo, C), lambda i: (jnp.maximum(i * (tm // halo) - 1, 0), 0)),
            _row_spec(tm, C),
            pl.BlockSpec((halo, C), lambda i: (jnp.minimum((i + 1) * (tm // halo), last_halo), 0)),
            _const_spec(w_pool.shape),
            _const_spec((1, C)),
        ],
        out_specs=[_row_spec(tm, C), _const_spec(w_pool.shape), _const_spec((1, C))],
        out_shape=[
            jax.ShapeDtypeStruct((L, C), F32),
            jax.ShapeDtypeStruct(w_pool.shape, F32),
            jax.ShapeDtypeStruct((1, C), F32),
        ],
        scratch_shapes=[pltpu.VMEM((tm + halo, C), F32), pltpu.VMEM((tm + halo, C), F32)],
        compiler_params=_params("arbitrary"),
        name=name,
    )(u, u, dy, dy, w_pool, scale)


def _head_masks(rows):
    lane = lax.broadcasted_iota(jnp.int32, (rows, HEAD_PAD), 1)
    return lane, lane < QK_NOPE_DIM, (lane >= QK_NOPE_DIM) & (lane < QK_HEAD_DIM)


def _rope_swap(x, lane):
    half = QK_ROPE_DIM // 2
    swapped = jnp.where(lane < QK_NOPE_DIM + half, pltpu.roll(x, HEAD_PAD - half, 1), pltpu.roll(x, half, 1))
    return jnp.where((lane >= QK_NOPE_DIM) & (lane < QK_HEAD_DIM), swapped, 0.0)


def _seg_mean(v, seg_ref):
    hi = v.astype(BF16)
    lo = (v - hi.astype(F32)).astype(BF16)
    seg = seg_ref[...]
    return _nn(hi, seg) + _nn(lo, seg)


def _segment_matrix():
    lane = jnp.arange(HEAD_PAD)
    seg = jnp.where(lane < QK_NOPE_DIM, 0, jnp.where(lane < QK_HEAD_DIM, 1, 2))
    inv = jnp.where(lane < QK_NOPE_DIM, 1.0 / QK_NOPE_DIM, jnp.where(lane < QK_HEAD_DIM, 1.0 / QK_ROPE_DIM, 0.0))
    return jnp.where(seg[:, None] == seg[None, :], inv[None, :], 0.0).astype(BF16)


def qk_prep_fwd(q, kv, kr, cosf, sins, seg, gq, gkn, gkr, name):
    L = q.shape[0]
    tm = _wide_tile(L)
    W = MLA_HEADS * HEAD_PAD

    def body(q_ref, kv_ref, kr_ref, c_ref, s_ref, seg_ref, gq_ref, gkn_ref, gkr_ref, qo_ref, ko_ref, vo_ref):
        lane, m_n, _ = _head_masks(tm)
        cosf_, sins_ = c_ref[...], s_ref[...]
        kr_ = kr_ref[...]
        rk = lax.rsqrt(_seg_mean(kr_ * kr_, seg_ref) + RMS_EPS)
        krn = kr_ * rk * gkr_ref[...]
        krf = krn * cosf_ + _rope_swap(krn, lane) * sins_
        for h in range(MLA_HEADS):
            cols = slice(h * HEAD_PAD, (h + 1) * HEAD_PAD)
            qh = q_ref[:, cols]
            qn = qh * lax.rsqrt(_seg_mean(qh * qh, seg_ref) + RMS_EPS) * gq_ref[...]
            qo_ref[:, cols] = (qn * cosf_ + _rope_swap(qn, lane) * sins_).astype(BF16)
            kh = jnp.where(m_n, kv_ref[:, cols], 0.0)
            rn = lax.rsqrt(_seg_mean(kh * kh, seg_ref) + RMS_EPS)
            ko_ref[:, cols] = (kh * rn * gkn_ref[...] + krf).astype(BF16)
        for p in range(MLA_HEADS // 2):
            even = kv_ref[:, 2 * p * HEAD_PAD:(2 * p + 1) * HEAD_PAD]
            odd = kv_ref[:, (2 * p + 1) * HEAD_PAD:(2 * p + 2) * HEAD_PAD]
            pair = jnp.where(m_n, pltpu.roll(even, V_HEAD_DIM, 1), odd)
            vo_ref[:, p * LANES:(p + 1) * LANES] = pair.astype(BF16)

    vec = _const_spec((1, HEAD_PAD))
    return pl.pallas_call(
        body,
        grid=(L // tm,),
        in_specs=[_row_spec(tm, W), _row_spec(tm, W), _row_spec(tm, HEAD_PAD), _row_spec(tm, HEAD_PAD),
                  _row_spec(tm, HEAD_PAD), _const_spec((HEAD_PAD, HEAD_PAD)), vec, vec, vec],
        out_specs=[_row_spec(tm, W), _row_spec(tm, W), _row_spec(tm, W // 2)],
        out_shape=[jax.ShapeDtypeStruct((L, W), BF16), jax.ShapeDtypeStruct((L, W), BF16),
                   jax.ShapeDtypeStruct((L, W // 2), BF16)],
        compiler_params=_params("arbitrary"),
        name=name,
    )(q, kv, kr, cosf, sins, seg, gq, gkn, gkr)


def qk_prep_bwd(dqo, dko, dv, q, kv, kr, cosf, sins, seg, gq, gkn, gkr, name):
    L = q.shape[0]
    tm = ROW_TILE
    W = MLA_HEADS * HEAD_PAD

    def body(dqo_ref, dko_ref, dv_ref, q_ref, kv_ref, kr_ref, c_ref, s_ref, seg_ref, gq_ref, gkn_ref, gkr_ref,
             dq_ref, dkv_ref, dkr_ref, dgq_ref, dgkn_ref, dgkr_ref):
        @pl.when(pl.program_id(0) == 0)
        def _():
            dgq_ref[...] = jnp.zeros_like(dgq_ref)
            dgkn_ref[...] = jnp.zeros_like(dgkn_ref)
            dgkr_ref[...] = jnp.zeros_like(dgkr_ref)

        lane, m_n, _ = _head_masks(tm)
        cosf_, sins_ = c_ref[...], s_ref[...]
        dgq = jnp.zeros((1, HEAD_PAD), F32)
        dgkn = jnp.zeros((1, HEAD_PAD), F32)
        dkrf = jnp.zeros((tm, HEAD_PAD), F32)
        for h in range(MLA_HEADS):
            cols = slice(h * HEAD_PAD, (h + 1) * HEAD_PAD)
            dy = dqo_ref[:, cols]
            dqn = dy * cosf_ + _rope_swap(dy * sins_, lane)
            qh = q_ref[:, cols]
            rinv = lax.rsqrt(_seg_mean(qh * qh, seg_ref) + RMS_EPS)
            xh = qh * rinv
            dgq = dgq + jnp.sum(dqn * xh, axis=0, keepdims=True)
            dxh = dqn * gq_ref[...]
            dq_ref[:, cols] = rinv * (dxh - xh * _seg_mean(dxh * xh, seg_ref))
            dk = dko_ref[:, cols]
            dkrf = dkrf + dk
            kh = jnp.where(m_n, kv_ref[:, cols], 0.0)
            rn = lax.rsqrt(_seg_mean(kh * kh, seg_ref) + RMS_EPS)
            xk = kh * rn
            dgkn = dgkn + jnp.sum(dk * xk, axis=0, keepdims=True)
            dxk = dk * gkn_ref[...]
            dkn = rn * (dxk - xk * _seg_mean(dxk * xk, seg_ref))
            dvp = dv_ref[:, (h // 2) * LANES:(h // 2 + 1) * LANES]
            dvh = pltpu.roll(dvp, V_HEAD_DIM, 1) if h % 2 == 0 else dvp
            dkv_ref[:, cols] = jnp.where(m_n, dkn, dvh)
        kr_ = kr_ref[...]
        rk = lax.rsqrt(_seg_mean(kr_ * kr_, seg_ref) + RMS_EPS)
        xr = kr_ * rk
        dkrn = dkrf * cosf_ + _rope_swap(dkrf * sins_, lane)
        dgkr_ref[...] += jnp.sum(dkrn * xr, axis=0, keepdims=True)
        dxr = dkrn * gkr_ref[...]
        dkr_ref[...] = rk * (dxr - xr * _seg_mean(dxr * xr, seg_ref))
        dgq_ref[...] += dgq
        dgkn_ref[...] += dgkn

    vec = _const_spec((1, HEAD_PAD))
    return pl.pallas_call(
        body,
        grid=(L // tm,),
        in_specs=[_row_spec(tm, W), _row_spec(tm, W), _row_spec(tm, W // 2), _row_spec(tm, W), _row_spec(tm, W),
                  _row_spec(tm, HEAD_PAD), _row_spec(tm, HEAD_PAD), _row_spec(tm, HEAD_PAD),
                  _const_spec((HEAD_PAD, HEAD_PAD)), vec, vec, vec],
        out_specs=[_row_spec(tm, W), _row_spec(tm, W), _row_spec(tm, HEAD_PAD), vec, vec, vec],
        out_shape=[jax.ShapeDtypeStruct((L, W), F32), jax.ShapeDtypeStruct((L, W), F32),
                   jax.ShapeDtypeStruct((L, HEAD_PAD), F32)] + [jax.ShapeDtypeStruct((1, HEAD_PAD), F32)] * 3,
        compiler_params=_params("arbitrary"),
        name=name,
    )(dqo, dko, dv, q, kv, kr, cosf, sins, seg, gq, gkn, gkr)


def attn_fwd(qp, kp, v, cat, name, gather=()):
    L = qp.shape[0]
    tq = ROW_TILE
    n_i = L // tq
    pair_w = 2 * HEAD_PAD
    n_pairs = MLA_HEADS // 2
    n_g = len(gather)

    def body(q_ref, k_ref, v_ref, cat_in, *rest):
        del cat_in
        o_ref, lse_ref, cat_ref = rest[n_g:n_g + 3]
        i = pl.program_id(1)
        if n_g:
            pair = pl.program_id(0)
            start, forward, finish = _gather_steps(rest[:n_g], rest[n_g + 3:2 * n_g + 3], *rest[2 * n_g + 3:])
            pl.when((pair == 0) & (i == 0))(start)
            pl.when((pair == n_pairs // 2) & (i == 0))(forward)
        left = lax.broadcasted_iota(jnp.int32, (tq, LANES), 1) < V_HEAD_DIM

        def step(t, carry, masked, width=1, first=0):
            rows = pl.ds(pl.multiple_of((first + t * width) * tq, tq), tq * width)
            vv = v_ref[rows, :]
            out = []
            for hh in range(2):
                cols = slice(hh * HEAD_PAD, (hh + 1) * HEAD_PAD)
                m, l, acc = carry[hh]
                s = _nt(q_ref[:, cols], k_ref[rows, cols]) * SCORE_SCALE
                if masked:
                    row = lax.broadcasted_iota(jnp.int32, s.shape, 0)
                    col = lax.broadcasted_iota(jnp.int32, s.shape, 1)
                    s = jnp.where(col <= row + tq * (width - 1), s, -jnp.inf)
                m_new = jnp.maximum(m, jnp.max(s, axis=1, keepdims=True))
                alpha = jnp.exp2(m - m_new)
                p = jnp.exp2(s - m_new)
                l = alpha * l + jnp.sum(p, axis=1, keepdims=True)
                acc = alpha * acc + _nn(p.astype(BF16), vv)
                out.append((m_new, l, acc))
            return tuple(out)

        one = (jnp.full((tq, 1), -jnp.inf, F32), jnp.zeros((tq, 1), F32), jnp.zeros((tq, LANES), F32))

        def with_history(carry):
            doubles = jnp.right_shift(i - 1, 1)
            carry = lax.fori_loop(0, doubles, functools.partial(step, masked=False, width=2), carry)
            carry = lax.fori_loop(2 * doubles, i - 1, functools.partial(step, masked=False), carry)
            return step(0, carry, True, width=2, first=i - 1)

        (m0, l0, a0), (m1, l1, a1) = lax.cond(i >= 1, with_history, lambda carry: step(i, carry, True), (one, one))
        o = jnp.where(left, a0 / l0, a1 / l1)
        o_ref[...] = o
        cat_ref[...] = o.astype(BF16)
        lse_ref[...] = jnp.where(left, m0 + jnp.log2(l0), m1 + jnp.log2(l1))
        if n_g:
            pl.when((pair == n_pairs - 1) & (i == n_i - 1))(finish)

    return pl.pallas_call(
        body,
        grid=(n_pairs, n_i),
        in_specs=[
            pl.BlockSpec((tq, pair_w), lambda p, i: (i, p)),
            pl.BlockSpec((L, pair_w), lambda p, i: (0, p)),
            pl.BlockSpec((L, LANES), lambda p, i: (0, p)),
            _ANY,
        ] + [_ANY] * n_g,
        out_specs=[
            pl.BlockSpec((tq, LANES), lambda p, i: (i, p)),
            pl.BlockSpec((tq, LANES), lambda p, i: (i, p)),
            pl.BlockSpec((tq, LANES), lambda p, i: (i, n_pairs + p)),
        ] + [_ANY] * n_g,
        out_shape=[jax.ShapeDtypeStruct((L, n_pairs * LANES), F32), jax.ShapeDtypeStruct((L, n_pairs * LANES), F32),
                   jax.ShapeDtypeStruct(cat.shape, cat.dtype)]
        + [jax.ShapeDtypeStruct((N_DEV,) + g.shape, g.dtype) for g in gather],
        scratch_shapes=_gather_scratch(n_g) if n_g else [],
        input_output_aliases={3: 2},
        compiler_params=_params("arbitrary", "arbitrary"),
        name=name,
    )(qp, kp, v, cat, *gather)


def attn_bwd(qp, kp, v, do, lse, delta, name, exchange=(), gather=()):
    L = qp.shape[0]
    tq = ROW_TILE
    n_q = L // tq
    pair_w = 2 * HEAD_PAD
    n_pairs = MLA_HEADS // 2
    n_x, n_g = len(exchange), len(gather)
    n_r = n_x + n_g

    def body(q_ref, k_ref, v_ref, do_ref, lse_ref, dl_ref, *rest):
        dq_ref, dk_ref, dv_ref = rest[n_r:n_r + 3]
        riders_in, riders_out, sems = rest[:n_r], rest[n_r + 3:2 * n_r + 3], rest[2 * n_r + 3:]
        j = pl.program_id(1)
        pair = pl.program_id(0)
        steps = []
        if n_x:
            steps.append(_chip_exchange_steps(riders_in[:n_x], riders_out[:n_x], *sems[:3]))
        if n_g:
            steps.append(_direct_gather_steps(riders_in[n_x:], riders_out[n_x:], *sems[-3:]))
        for start, _ in steps:
            pl.when((pair == 0) & (j == 0))(start)

        @pl.when(j == 0)
        def _():
            dq_ref[...] = jnp.zeros_like(dq_ref)

        dk_ref[...] = jnp.zeros_like(dk_ref)
        dv_ref[...] = jnp.zeros_like(dv_ref)

        def step(t, carry, masked, width=1, first=0):
            rows = pl.ds(pl.multiple_of((first + t * width) * tq, tq), tq * width)
            do = do_ref[rows, :]
            left = lax.broadcasted_iota(jnp.int32, do.shape, 1) < V_HEAD_DIM
            vv = v_ref[...]
            dv = None
            for hh in range(2):
                cols = slice(hh * HEAD_PAD, (hh + 1) * HEAD_PAD)
                stat = slice(hh * V_HEAD_DIM, hh * V_HEAD_DIM + 1)
                q = q_ref[rows, cols]
                k = k_ref[:, cols]
                dom = jnp.where(left if hh == 0 else jnp.logical_not(left), do, 0.0).astype(BF16)
                s = _nt(q, k) * SCORE_SCALE
                p = jnp.exp2(s - lse_ref[rows, stat])
                if masked:
                    row = lax.broadcasted_iota(jnp.int32, p.shape, 0)
                    col = lax.broadcasted_iota(jnp.int32, p.shape, 1)
                    p = jnp.where(col <= row, p, 0.0)
                dp = _nt(dom, vv)
                ds = (p * (dp - dl_ref[rows, stat]) * ATTN_SCALE).astype(BF16)
                dq_ref[rows, cols] += _nn(ds, k)
                dk_ref[:, cols] += _tn(ds, q)
                t = _tn(p.astype(BF16), dom)
                dv = t if dv is None else dv + t
            dv_ref[...] += dv
            return carry

        @pl.when(j < n_q - 1)
        def _():
            step(0, 0, True, width=2, first=j)
            doubles = jnp.right_shift(n_q - 2 - j, 1)
            lax.fori_loop(0, doubles, functools.partial(step, masked=False, width=2, first=j + 2), 0)
            lax.fori_loop(j + 2 + 2 * doubles, n_q, functools.partial(step, masked=False), 0)

        @pl.when(j == n_q - 1)
        def _():
            step(j, 0, True)

        for _, finish in steps:
            pl.when((pair == n_pairs - 1) & (j == n_q - 1))(finish)

    return pl.pallas_call(
        body,
        grid=(n_pairs, n_q),
        in_specs=[
            pl.BlockSpec((L, pair_w), lambda p, j: (0, p)),
            pl.BlockSpec((tq, pair_w), lambda p, j: (j, p)),
            pl.BlockSpec((tq, LANES), lambda p, j: (j, p)),
            pl.BlockSpec((L, LANES), lambda p, j: (0, p)),
            pl.BlockSpec((L, LANES), lambda p, j: (0, p)),
            pl.BlockSpec((L, LANES), lambda p, j: (0, p)),
        ] + [_ANY] * n_r,
        out_specs=[
            pl.BlockSpec((L, pair_w), lambda p, j: (0, p)),
            pl.BlockSpec((tq, pair_w), lambda p, j: (j, p)),
            pl.BlockSpec((tq, LANES), lambda p, j: (j, p)),
        ] + [_ANY] * n_r,
        out_shape=[
            jax.ShapeDtypeStruct((L, n_pairs * pair_w), F32),
            jax.ShapeDtypeStruct((L, n_pairs * pair_w), F32),
            jax.ShapeDtypeStruct((L, n_pairs * LANES), F32),
        ] + [jax.ShapeDtypeStruct(e.shape, e.dtype) for e in exchange]
        + [jax.ShapeDtypeStruct((N_DEV,) + g.shape, g.dtype) for g in gather],
        scratch_shapes=(_chip_exchange_scratch(n_x) if n_x else []) + (_direct_gather_scratch(n_g) if n_g else []),
        compiler_params=_params("arbitrary", "arbitrary"),
        name=name,
    )(qp, kp, v, do, lse, delta, *exchange, *gather)


def loss_head(h, target, n_real, name):
    L, D = h.shape
    tm = _wide_tile(L)

    def body(h_ref, t_ref, dh_ref, sq_ref):
        i = pl.program_id(0)

        @pl.when(i == 0)
        def _():
            sq_ref[...] = jnp.zeros_like(sq_ref)

        t = i * tm + lax.broadcasted_iota(jnp.int32, (tm, D), 0)
        real = (t >= N_META) & (t < N_META + n_real)
        diff = jnp.where(real, h_ref[...] - t_ref[...], 0.0)
        dh_ref[...] = diff * (1.0 / D)
        sq_ref[...] += jnp.sum(diff * diff, axis=0, keepdims=True)

    return pl.pallas_call(
        body,
        grid=(L // tm,),
        in_specs=[_row_spec(tm, D), _row_spec(tm, D)],
        out_specs=[_row_spec(tm, D), _const_spec((1, D))],
        out_shape=[jax.ShapeDtypeStruct((L, D), F32), jax.ShapeDtypeStruct((1, D), F32)],
        compiler_params=_params("arbitrary"),
        name=name,
    )(h, target)


def _mesh_position():
    x, y, c = lax.axis_index("x"), lax.axis_index("y"), lax.axis_index("c")
    return x, y, c, 4 * x + 2 * y + c


def _flip(x, y, c, k):
    px = 1 - x if k & 4 else x
    py = 1 - y if k & 2 else y
    pc = 1 - c if k & 1 else c
    return (px, py, pc), 4 * px + 2 * py + pc


def _other_chips(x, y):
    return [(1 - x, y), (x, 1 - y), (1 - x, 1 - y)]


def _dev_index(px, py, pc):
    return 4 * px + 2 * py + pc


_ANY = pl.BlockSpec(memory_space=pl.ANY)


def cast_bf16(arrays, name):
    n = len(arrays)
    depth = arrays[0].shape[0]

    def body(*refs):
        for k, src in enumerate(refs[:n]):
            for l in range(depth):
                refs[n + k * depth + l][0] = src[l].astype(BF16)

    return pl.pallas_call(
        body,
        out_shape=[jax.ShapeDtypeStruct((1,) + a.shape[1:], BF16) for a in arrays for _ in range(depth)],
        compiler_params=pltpu.CompilerParams(vmem_limit_bytes=VMEM_LIMIT),
        name=name,
    )(*arrays)


def _gather_steps(srcs, dsts, send_sems, recv_sems, local_sems):
    n = len(srcs)
    x, y, c, me = _mesh_position()
    sibling = (x, y, 1 - c)
    chips = _other_chips(x, y)

    def copy(k, b, block, to, from_src=False):
        dst = dsts[b].at[block]
        return pltpu.make_async_remote_copy(
            src_ref=srcs[b] if from_src else dst, dst_ref=dst, send_sem=send_sems.at[k * n + b],
            recv_sem=recv_sems.at[k * n + b], device_id=to, device_id_type=pl.DeviceIdType.MESH)

    def own():
        return [pltpu.make_async_copy(srcs[b], dsts[b].at[me], local_sems.at[b]) for b in range(n)]

    def first():
        out = []
        for b in range(n):
            out.append(copy(0, b, me, sibling, from_src=True))
            out += [copy(1 + q, b, me, (*chip, c), from_src=True) for q, chip in enumerate(chips)]
        return out

    def passed():
        return [copy(4 + q, b, _dev_index(*chip, c), sibling) for q, chip in enumerate(chips) for b in range(n)]

    def start():
        for cp in own() + first():
            cp.start()

    def forward():
        for q, chip in enumerate(chips):
            for b in range(n):
                copy(1 + q, b, _dev_index(*chip, c), sibling).wait_recv()
        for cp in passed():
            cp.start()

    def finish():
        for b in range(n):
            copy(0, b, _dev_index(x, y, 1 - c), sibling).wait_recv()
            for q, chip in enumerate(chips):
                copy(4 + q, b, _dev_index(*chip, 1 - c), sibling).wait_recv()
        for cp in first() + passed():
            cp.wait_send()
        for cp in own():
            cp.wait()

    return start, forward, finish


def _gather_scratch(n):
    copies = N_DEV - 1
    return [pltpu.SemaphoreType.DMA((copies * n,)), pltpu.SemaphoreType.DMA((copies * n,)),
            pltpu.SemaphoreType.DMA((n,))]


def all_gather(payloads, name):
    n = len(payloads)

    def body(*refs):
        start, forward, finish = _gather_steps(refs[:n], refs[n:2 * n], *refs[2 * n:])
        start()
        forward()
        finish()

    return pl.pallas_call(
        body,
        in_specs=[_ANY] * n,
        out_specs=[_ANY] * n,
        out_shape=[jax.ShapeDtypeStruct((N_DEV,) + p.shape, p.dtype) for p in payloads],
        scratch_shapes=_gather_scratch(n),
        name=name,
    )(*payloads)


def exchange_pair(bigs, small, name):
    n = len(bigs)
    has_small = small is not None
    n_big = N_CHIPS * n
    n_sems = n_big + (N_DEV - 1 if has_small else 0)

    def body(*refs):
        big_refs = refs[:n]
        pos = n + has_small
        sib_refs = refs[pos:pos + n]
        send_sems, recv_sems, local_sem = refs[-3:]
        x, y, c, me = _mesh_position()
        sibling = (x, y, 1 - c)
        copies = []
        for b in range(n):
            for q in range(N_CHIPS):
                copies.append(pltpu.make_async_remote_copy(
                    src_ref=big_refs[b].at[_dev_index(q // 2, q % 2, 1 - c)], dst_ref=sib_refs[b].at[q],
                    send_sem=send_sems.at[b * N_CHIPS + q], recv_sem=recv_sems.at[b * N_CHIPS + q],
                    device_id=sibling, device_id_type=pl.DeviceIdType.MESH))
        waits = list(copies)
        if has_small:
            small_ref, gsmall_ref = refs[n], refs[pos + n]
            own = pltpu.make_async_copy(small_ref.at[me], gsmall_ref.at[me], local_sem.at[0])
            own.start()
            for k in range(1, N_DEV):
                peer, peer_idx = _flip(x, y, c, k)
                s = n_big + k - 1
                copies.append(pltpu.make_async_remote_copy(
                    src_ref=small_ref.at[peer_idx], dst_ref=gsmall_ref.at[me], send_sem=send_sems.at[s],
                    recv_sem=recv_sems.at[s], device_id=peer, device_id_type=pl.DeviceIdType.MESH))
                waits.append(pltpu.make_async_remote_copy(
                    src_ref=small_ref.at[peer_idx], dst_ref=gsmall_ref.at[peer_idx], send_sem=send_sems.at[s],
                    recv_sem=recv_sems.at[s], device_id=peer, device_id_type=pl.DeviceIdType.MESH))
        for cp in copies:
            cp.start()
        for cp in waits:
            cp.wait_recv()
        for cp in copies:
            cp.wait_send()
        if has_small:
            own.wait()

    out_shape = [jax.ShapeDtypeStruct((N_CHIPS,) + b.shape[1:], b.dtype) for b in bigs]
    args = list(bigs)
    if has_small:
        out_shape.append(jax.ShapeDtypeStruct(small.shape, small.dtype))
        args.append(small)
    return pl.pallas_call(
        body,
        in_specs=[_ANY] * len(args),
        out_specs=[_ANY] * len(out_shape),
        out_shape=out_shape,
        scratch_shapes=[pltpu.SemaphoreType.DMA((n_sems,)), pltpu.SemaphoreType.DMA((n_sems,)),
                        pltpu.SemaphoreType.DMA((1,))],
        name=name,
    )(*args)


def pair_sum(bigs, from_sibling, core, name):
    n = len(bigs)

    def body(core_ref, *refs):
        del core_ref
        for mine, sib, out in zip(refs[:n], refs[n:2 * n], refs[2 * n:]):
            out[...] = (mine[...].astype(F32) + sib[...].astype(F32)).astype(out.dtype)

    def slot(shape, picked):
        zeros = (0,) * (len(shape) - 1)
        if picked:
            return pl.BlockSpec((None,) + shape[1:], lambda q, core_ref: (2 * q + core_ref[0],) + zeros)
        return pl.BlockSpec((None,) + shape[1:], lambda q, core_ref: (q,) + zeros)

    grid_spec = pltpu.PrefetchScalarGridSpec(
        num_scalar_prefetch=1,
        grid=(N_CHIPS,),
        in_specs=[slot(b.shape, True) for b in bigs] + [slot(s.shape, False) for s in from_sibling],
        out_specs=[slot(s.shape, False) for s in from_sibling],
    )
    return pl.pallas_call(
        body,
        grid_spec=grid_spec,
        out_shape=[jax.ShapeDtypeStruct(s.shape, s.dtype) for s in from_sibling],
        compiler_params=_params("arbitrary"),
        name=name,
    )(core, *bigs, *from_sibling)


def _chip_exchange_steps(part_refs, got_refs, send_sems, recv_sems, local_sems):
    n = len(part_refs)
    x, y, c, me = _mesh_position()
    mine = 2 * x + y

    def copies(landing):
        out = []
        for q, (px, py) in enumerate(_other_chips(x, y)):
            theirs = 2 * px + py
            for b in range(n):
                out.append(pltpu.make_async_remote_copy(
                    src_ref=part_refs[b].at[theirs], dst_ref=got_refs[b].at[theirs if landing else mine],
                    send_sem=send_sems.at[q * n + b], recv_sem=recv_sems.at[q * n + b],
                    device_id=(px, py, c), device_id_type=pl.DeviceIdType.MESH))
        return out

    def own():
        return [pltpu.make_async_copy(part_refs[b].at[mine], got_refs[b].at[mine], local_sems.at[b])
                for b in range(n)]

    def start():
        for cp in own() + copies(False):
            cp.start()

    def finish():
        for cp in copies(True):
            cp.wait_recv()
        for cp in copies(False):
            cp.wait_send()
        for cp in own():
            cp.wait()

    return start, finish


def _chip_exchange_scratch(n):
    return [pltpu.SemaphoreType.DMA(((N_CHIPS - 1) * n,)), pltpu.SemaphoreType.DMA(((N_CHIPS - 1) * n,)),
            pltpu.SemaphoreType.DMA((n,))]


def _direct_gather_steps(srcs, dsts, send_sems, recv_sems, local_sems):
    n = len(srcs)
    x, y, c, me = _mesh_position()

    def copies(landing):
        out = []
        for k in range(1, N_DEV):
            peer, peer_idx = _flip(x, y, c, k)
            for b in range(n):
                out.append(pltpu.make_async_remote_copy(
                    src_ref=srcs[b], dst_ref=dsts[b].at[peer_idx if landing else me],
                    send_sem=send_sems.at[(k - 1) * n + b], recv_sem=recv_sems.at[(k - 1) * n + b],
                    device_id=peer, device_id_type=pl.DeviceIdType.MESH))
        return out

    def own():
        return [pltpu.make_async_copy(srcs[b], dsts[b].at[me], local_sems.at[b]) for b in range(n)]

    def start():
        for cp in own() + copies(False):
            cp.start()

    def finish():
        for cp in copies(True):
            cp.wait_recv()
        for cp in copies(False):
            cp.wait_send()
        for cp in own():
            cp.wait()

    return start, finish


def _direct_gather_scratch(n):
    return [pltpu.SemaphoreType.DMA(((N_DEV - 1) * n,)), pltpu.SemaphoreType.DMA(((N_DEV - 1) * n,)),
            pltpu.SemaphoreType.DMA((n,))]


def exchange_chips(partials, name):
    n = len(partials)

    def body(*refs):
        start, finish = _chip_exchange_steps(refs[:n], refs[n:2 * n], *refs[2 * n:])
        start()
        finish()

    return pl.pallas_call(
        body,
        in_specs=[_ANY] * n,
        out_specs=[_ANY] * n,
        out_shape=[jax.ShapeDtypeStruct(p.shape, p.dtype) for p in partials],
        scratch_shapes=_chip_exchange_scratch(n),
        name=name,
    )(*partials)


def _adamw_math(g, w, m, v):
    m_new = ADAM_B1 * m + (1.0 - ADAM_B1) * g
    v_new = ADAM_B2 * v + (1.0 - ADAM_B2) * (g * g)
    m_hat = m_new / (1.0 - ADAM_B1 ** ADAM_STEP)
    v_hat = v_new / (1.0 - ADAM_B2 ** ADAM_STEP)
    delta = -ADAM_LR * (m_hat / (jnp.sqrt(v_hat) + ADAM_EPS) + ADAM_WD * w)
    return delta, m_new, v_new


def _chip_total(p_ref):
    g = p_ref[0].astype(F32)
    for q in range(1, N_CHIPS):
        g = g + p_ref[q].astype(F32)
    return g


def chip_sum(parts, name):
    depth = len(parts)

    def body(*refs):
        for l in range(depth):
            refs[depth][l] = _chip_total(refs[l])

    return pl.pallas_call(
        body,
        out_shape=jax.ShapeDtypeStruct((depth,) + parts[0].shape[1:], F32),
        compiler_params=pltpu.CompilerParams(vmem_limit_bytes=VMEM_LIMIT),
        name=name,
    )(*parts)


def adamw_shard(parts, w, m, v, name, grad=None):
    depth = w.shape[0]
    n_in = 1 if grad is not None else depth

    def body(*refs):
        w_ref, m_ref, v_ref, g_ref, d_ref, mo_ref, vo_ref = refs[n_in:]
        for l in range(depth):
            g = refs[0][l] if grad is not None else _chip_total(refs[l])
            delta, m_new, v_new = _adamw_math(g, w_ref[l], m_ref[l], v_ref[l])
            g_ref[l] = g
            d_ref[l] = delta
            mo_ref[l] = m_new
            vo_ref[l] = v_new

    return pl.pallas_call(
        body,
        out_shape=[jax.ShapeDtypeStruct(w.shape, F32)] * 4,
        compiler_params=pltpu.CompilerParams(vmem_limit_bytes=VMEM_LIMIT),
        name=name,
    )(*([grad] if grad is not None else parts), w, m, v)


def adamw_packed(parts, w, m, v, name):
    n_slots = parts.shape[0]

    def body(p_ref, w_ref, m_ref, v_ref, g_ref, d_ref, mo_ref, vo_ref):
        g = p_ref[0]
        for s in range(1, n_slots):
            g = g + p_ref[s]
        delta, m_new, v_new = _adamw_math(g, w_ref[...], m_ref[...], v_ref[...])
        g_ref[...] = g
        d_ref[...] = delta
        mo_ref[...] = m_new
        vo_ref[...] = v_new

    return pl.pallas_call(
        body,
        out_shape=[jax.ShapeDtypeStruct(w.shape, F32)] * 4,
        compiler_params=pltpu.CompilerParams(vmem_limit_bytes=VMEM_LIMIT),
        name=name,
    )(parts, w, m, v)


SHARDED = ("w_in", "w_q_b", "w_kv_b", "w_out", "w_gate", "w_up", "w_down")
TRANSPOSED = ("w_in", "w_q_b", "w_gate", "w_up")
ADAM_TRANSPOSED = ("w_q_b", "w_gate", "w_up")
BEFORE_ATTENTION = ("w_in", "w_q_b", "w_kv_b")
BEFORE_ATTENTION_BWD = ("w_out", "w_gate", "w_up", "w_down")
REPLICATED = ("attn_norm_g", "pool_scale", "q_a_norm_g", "kv_a_norm_g", "q_norm_g", "k_norm_g", "ffn_norm_g")


def _pack_flat(arrays, rows):
    flat = jnp.concatenate([a.reshape(-1) for a in arrays])
    return jnp.pad(flat, (0, rows * LANES - flat.shape[0])).reshape(rows, LANES)


def _unpack_flat(packed, shapes):
    flat, out, at = packed.reshape(-1), [], 0
    for shp in shapes:
        n = 1
        for d in shp:
            n *= d
        out.append(flat[at:at + n].reshape(shp))
        at += n
    return out


def _rope_lane_tables(length):
    inv = 1.0 / (ROPE_THETA ** (jnp.arange(0, QK_ROPE_DIM, 2, dtype=F32) / QK_ROPE_DIM))
    ang = jnp.arange(length, dtype=F32)[:, None] * inv[None, :]
    cos, sin = jnp.cos(ang), jnp.sin(ang)
    ones = jnp.ones((length, QK_NOPE_DIM), F32)
    zeros = jnp.zeros((length, QK_NOPE_DIM), F32)
    tail = HEAD_PAD - QK_HEAD_DIM
    cosf = jnp.concatenate([ones, cos, cos, ones[:, :tail]], axis=1)
    sins = jnp.concatenate([zeros, -sin, sin, zeros[:, :tail]], axis=1)
    return cosf, sins


def _pad_lanes(vec, at, width=HEAD_PAD):
    return jnp.pad(vec, (at, width - at - vec.shape[0])).reshape(1, width)


def kernel(x, meta_tokens, attn_norm_g, w_in, w_pool, pool_scale, q_a_norm_g, w_q_b, kv_a_norm_g, w_kv_b, q_norm_g, k_norm_g, w_out, ffn_norm_g, w_gate, w_up, w_down, loss_target, m_meta_tokens, m_attn_norm_g, m_w_in, m_w_pool, m_pool_scale, m_q_a_norm_g, m_w_q_b, m_kv_a_norm_g, m_w_kv_b, m_q_norm_g, m_k_norm_g, m_w_out, m_ffn_norm_g, m_w_gate, m_w_up, m_w_down, v_meta_tokens, v_attn_norm_g, v_w_in, v_w_pool, v_pool_scale, v_q_a_norm_g, v_w_q_b, v_kv_a_norm_g, v_w_kv_b, v_q_norm_g, v_k_norm_g, v_w_out, v_ffn_norm_g, v_w_gate, v_w_up, v_w_down):
    weights = dict(meta_tokens=meta_tokens, attn_norm_g=attn_norm_g, w_in=w_in, w_pool=w_pool, pool_scale=pool_scale,
                   q_a_norm_g=q_a_norm_g, w_q_b=w_q_b, kv_a_norm_g=kv_a_norm_g, w_kv_b=w_kv_b, q_norm_g=q_norm_g,
                   k_norm_g=k_norm_g, w_out=w_out, ffn_norm_g=ffn_norm_g, w_gate=w_gate, w_up=w_up, w_down=w_down)
    mom1 = dict(meta_tokens=m_meta_tokens, attn_norm_g=m_attn_norm_g, w_in=m_w_in, w_pool=m_w_pool,
                pool_scale=m_pool_scale, q_a_norm_g=m_q_a_norm_g, w_q_b=m_w_q_b, kv_a_norm_g=m_kv_a_norm_g,
                w_kv_b=m_w_kv_b, q_norm_g=m_q_norm_g, k_norm_g=m_k_norm_g, w_out=m_w_out, ffn_norm_g=m_ffn_norm_g,
                w_gate=m_w_gate, w_up=m_w_up, w_down=m_w_down)
    mom2 = dict(meta_tokens=v_meta_tokens, attn_norm_g=v_attn_norm_g, w_in=v_w_in, w_pool=v_w_pool,
                pool_scale=v_pool_scale, q_a_norm_g=v_q_a_norm_g, w_q_b=v_w_q_b, kv_a_norm_g=v_kv_a_norm_g,
                w_kv_b=v_w_kv_b, q_norm_g=v_q_norm_g, k_norm_g=v_k_norm_g, w_out=v_w_out, ffn_norm_g=v_ffn_norm_g,
                w_gate=v_w_gate, w_up=v_w_up, w_down=v_w_down)
    order = ("meta_tokens", "attn_norm_g", "w_in", "w_pool", "pool_scale", "q_a_norm_g", "w_q_b", "kv_a_norm_g",
             "w_kv_b", "q_norm_g", "k_norm_g", "w_out", "ffn_norm_g", "w_gate", "w_up", "w_down")
    depth = w_in.shape[0]
    seq = x.shape[1]
    length = N_META + seq
    lp = -(-length // ROW_TILE) * ROW_TILE
    in_cols = w_in.shape[2]

    local = {n: (jnp.swapaxes(weights[n], 1, 2) if n in TRANSPOSED else weights[n]) for n in SHARDED}
    cast = cast_bf16([local[n] for n in SHARDED], "cast_weights")
    shards = [{n: cast[k * depth + l] for k, n in enumerate(SHARDED)} for l in range(depth)]
    gathered = all_gather([shards[0][n] for n in BEFORE_ATTENTION] + [meta_tokens], "all_gather")
    g8l = [dict(zip(BEFORE_ATTENTION, gathered[:-1]))] + [{} for _ in range(depth - 1)]
    meta_full = jnp.transpose(gathered[-1], (1, 0, 2)).reshape(N_META, D_MODEL)

    s1, s2, s3 = POOL_WIDTH, POOL_WIDTH + Q_LORA_RANK, POOL_WIDTH + Q_LORA_RANK + KV_LORA_RANK
    zpad = lambda n: jnp.zeros((1, n, D_MODEL), BF16)

    def padded_in_proj(w8):
        w_in_t = jnp.swapaxes(w8, 0, 1).reshape(1, N_DEV * in_cols, D_MODEL)
        return jnp.concatenate([w_in_t[:, :s3], zpad(QK_NOPE_DIM), w_in_t[:, s3:],
                                zpad(HEAD_PAD - QK_HEAD_DIM)], axis=1)

    w_in_ps = [None] * depth
    w_pool_b = w_pool.astype(BF16)

    cosf, sins = _rope_lane_tables(lp)
    seg = _segment_matrix()
    row = lambda a, l: a[l].reshape(1, -1)

    h = jnp.concatenate([meta_full, x[0], jnp.zeros((lp - length, D_MODEL), F32)], axis=0)
    target = jnp.pad(loss_target[0], ((N_META, lp - length), (0, 0)))
    saved = []
    for l in range(depth):
        gq = _pad_lanes(q_norm_g[l], 0)
        gkn = _pad_lanes(k_norm_g[l, :QK_NOPE_DIM], 0)
        gkr = _pad_lanes(k_norm_g[l, QK_NOPE_DIM:], QK_NOPE_DIM)
        g8 = g8l[l]
        w_in_ps[l] = padded_in_proj(g8["w_in"])
        a, u, c_q, c_kv, kr = norm_mm(
            h, row(attn_norm_g, l), w_in_ps[l], 0,
            [(0, s1), (s1, Q_LORA_RANK), (s2, KV_LORA_RANK), (s3, HEAD_PAD)], [F32] * 4, "in_proj")
        cat = pool_fwd(u, w_pool_b[l], row(pool_scale, l), "pool_fwd")
        qn, q = norm_mm_heads(c_q, row(q_a_norm_g, l), g8["w_q_b"], 0, "q_proj", transposed=True)
        kvn, kv = norm_mm_heads(c_kv, row(kv_a_norm_g, l), g8["w_kv_b"], 0, "kv_proj")
        qp, kp, v = qk_prep_fwd(q, kv, kr, cosf, sins, seg, gq, gkn, gkr, "qk_prep_fwd")
        riders = [(l, n) for n in SHARDED if n not in g8]
        riders += [(l + 1, n) for n in BEFORE_ATTENTION if l + 1 < depth]
        o, lse, cat, *arrived = attn_fwd(qp, kp, v, cat, "attn_fwd_gather", gather=[shards[k][n] for k, n in riders])
        for (k, n), w8 in zip(riders, arrived):
            g8l[k][n] = w8
        h_mid, g = mm_res(cat, g8["w_out"], h, 0, "out_proj", gamma=row(ffn_norm_g, l))
        gate, up, act = ffn_up(g, g8["w_gate"], g8["w_up"], 0, "ffn_up")
        h_next = mm_res(act, g8["w_down"], h_mid, 0, "ffn_down")[0]
        saved.append(dict(h=h, a=a, u=u, c_q=c_q, c_kv=c_kv, kr=kr, qn=qn, q=q, kvn=kvn, kv=kv, v=v, qp=qp, kp=kp,
                          o=o, lse=lse, cat=cat, h_mid=h_mid, g=g, gate=gate, up=up, act=act,
                          gq=gq, gkn=gkn, gkr=gkr))
        h = h_next

    dh, sq = loss_head(h, target, seq, "loss_head")
    loss = lax.psum(0.5 / D_MODEL * jnp.sum(sq), ("x", "y", "c"))

    core = lax.axis_index("c").astype(jnp.int32).reshape(1)
    small_grads = {n: [None] * depth for n in REPLICATED + ("w_pool",)}
    pending = []
    got = [{} for _ in range(depth)]

    def pair_reduce(l, names, slots, small):
        bigs = [slots[n] for n in names]
        if small is None:
            from_sibling, got_small = exchange_pair(bigs, None, "grad_exchange_pair"), None
        else:
            *from_sibling, got_small = exchange_pair(bigs, small, "grad_exchange_pair_small")
        partial = pair_sum(bigs, list(from_sibling), core, "grad_pair_sum")
        return [(l, n, p) for n, p in zip(names, partial)], got_small

    for l in reversed(range(depth)):
        s = saved[l]
        g8 = g8l[l]
        w_in_p = w_in_ps[l]
        slots = {}
        dgate, dup = ffn_bwd_act(dh, g8["w_down"], 0, s["gate"], s["up"], "ffn_bwd_act")
        slot_of = lambda n, full: full.reshape(g8[n].shape[:1] + g8[n].shape[2:])
        slots["w_down"] = slot_of("w_down", mm_tn([s["act"]], dh, "dw_down", out_dtype=BF16)[0])
        dw_gate_t, dw_up_t = mm_tn([dgate, dup], s["g"], "dw_gate_up", out_dtype=BF16)
        slots["w_gate"] = slot_of("w_gate", dw_gate_t)
        slots["w_up"] = slot_of("w_up", dw_up_t)
        dh_mid, dg_ffn = mm_nt_normbwd([("gathered", dgate, g8["w_gate"]), ("gathered", dup, g8["w_up"])],
                                       s["h_mid"], row(ffn_norm_g, l), dh, 0, "ffn_bwd_in", wide=False)
        small_grads["ffn_norm_g"][l] = dg_ffn[0]
        slots["w_out"] = slot_of("w_out", mm_tn([s["cat"]], dh_mid, "dw_out", out_dtype=BF16)[0])
        pending += pair_reduce(l, BEFORE_ATTENTION_BWD, slots, None)[0]
        dy_pool, do, delta = out_proj_bwd(dh_mid, g8["w_out"], 0, s["o"], "out_proj_bwd")
        du, dw_pool, dscale = pool_bwd(s["u"], dy_pool, w_pool_b[l], row(pool_scale, l), "pool_bwd")
        small_grads["w_pool"][l] = dw_pool
        small_grads["pool_scale"][l] = dscale[0]
        pool_rider = [jnp.stack(small_grads["w_pool"], axis=0).reshape(-1, LANES)] if l == 0 else []
        dqp, dkp, dv, *arrived = attn_bwd(s["qp"], s["kp"], s["v"], do, s["lse"], delta, "attn_bwd_exchange",
                                          exchange=[p for _, _, p in pending], gather=pool_rider)
        if pool_rider:
            got_pool = arrived.pop()
        for (k, n, _), parts in zip(pending, arrived):
            got[k][n] = parts
        dq, dkv, dkr, dgq, dgkn, dgkr = qk_prep_bwd(dqp, dkp, dv, s["q"], s["kv"], s["kr"], cosf, sins, seg,
                                                    s["gq"], s["gkn"], s["gkr"], "qk_prep_bwd")
        small_grads["q_norm_g"][l] = dgq[0, :QK_HEAD_DIM]
        small_grads["k_norm_g"][l] = jnp.concatenate([dgkn[0, :QK_NOPE_DIM], dgkr[0, QK_NOPE_DIM:QK_HEAD_DIM]])
        heads_first = lambda full: jnp.swapaxes(full.reshape(full.shape[0], MLA_HEADS, HEAD_PAD), 0, 1)
        dw_q_t = mm_tn([dq], s["qn"], "dw_q", out_dtype=BF16)[0]
        slots["w_q_b"] = dw_q_t.reshape(MLA_HEADS, HEAD_PAD, -1)[:, :QK_HEAD_DIM]
        dc_q, dg_qa = mm_nt_normbwd([("heads_t", dq, g8["w_q_b"])], s["c_q"], row(q_a_norm_g, l), None, 0,
                                    "q_proj_bwd")
        small_grads["q_a_norm_g"][l] = dg_qa[0]
        slots["w_kv_b"] = heads_first(mm_tn([s["kvn"]], dkv, "dw_kv", out_dtype=BF16)[0])
        dc_kv, dg_kva = mm_nt_normbwd([("heads", dkv, g8["w_kv_b"])], s["c_kv"], row(kv_a_norm_g, l), None, 0,
                                      "kv_proj_bwd")
        small_grads["kv_a_norm_g"][l] = dg_kva[0]
        dw_pool_t, dw_q_t, dw_kv_t, dw_rope_t = mm_tn([du, dc_q, dc_kv, dkr], s["a"], "dw_in", out_dtype=BF16)
        dw_in_t = jnp.concatenate([dw_pool_t, dw_q_t, dw_kv_t, dw_rope_t[QK_NOPE_DIM:QK_HEAD_DIM]], axis=0)
        slots["w_in"] = slot_of("w_in", dw_in_t)
        dh, dg_attn = mm_nt_normbwd(
            [("rows", du, w_in_p, 0), ("rows", dc_q, w_in_p, s1), ("rows", dc_kv, w_in_p, s2),
             ("rows", dkr, w_in_p, s3)],
            s["h"], row(attn_norm_g, l), dh_mid, 0, "in_proj_bwd")
        small_grads["attn_norm_g"][l] = dg_attn[0]

        small_slots = None
        if l == 0:
            rep_shapes = [weights[n].shape for n in REPLICATED]
            rep_count = sum(int(jnp.size(weights[n])) for n in REPLICATED)
            rep_rows = -(-rep_count // (8 * LANES)) * 8
            rep_packed = _pack_flat([jnp.stack(small_grads[n], axis=0) for n in REPLICATED], rep_rows)
            meta_slots = jnp.transpose(dh[:N_META].reshape(N_META, N_DEV, LANES), (1, 0, 2))
            small_slots = jnp.concatenate(
                [meta_slots, jnp.broadcast_to(rep_packed[None], (N_DEV, rep_rows, LANES))], axis=1)
        pending, got_small = pair_reduce(l, [n for n in SHARDED if n not in BEFORE_ATTENTION_BWD], slots,
                                         small_slots)
    for (k, n, _), parts in zip(pending, exchange_chips([p for _, _, p in pending], "grad_exchange_chips")):
        got[k][n] = parts

    grad_x = dh[N_META:length][None]

    per = [{} for _ in range(4)]
    for n in SHARDED:
        parts = [got[l][n] for l in range(depth)]
        if n in ADAM_TRANSPOSED:
            t = lambda a: jnp.swapaxes(a, 1, 2)
            outs = [t(o) for o in adamw_shard(parts, local[n], t(mom1[n]), t(mom2[n]), "adamw_" + n)]
        elif n in TRANSPOSED:
            grad = jnp.swapaxes(chip_sum(parts, "chip_sum_" + n), 1, 2)
            outs = adamw_shard(None, weights[n], mom1[n], mom2[n], "adamw_" + n, grad=grad)
        else:
            outs = adamw_shard(parts, weights[n], mom1[n], mom2[n], "adamw_" + n)
        for k in range(4):
            per[k][n] = outs[k]
    ps = lambda src: jnp.concatenate(
        [src["meta_tokens"], _pack_flat([src[n] for n in REPLICATED], rep_rows)], axis=0)
    small_out = adamw_packed(got_small, ps(weights), ps(mom1), ps(mom2), "adamw_small")
    as_rows = lambda a: a.reshape(-1, LANES)
    pool_out = adamw_packed(got_pool, as_rows(w_pool), as_rows(m_w_pool), as_rows(v_w_pool), "adamw_w_pool")
    for k in range(4):
        per[k]["meta_tokens"] = small_out[k][:N_META]
        per[k].update(zip(REPLICATED, _unpack_flat(small_out[k][N_META:], rep_shapes)))
        per[k]["w_pool"] = pool_out[k].reshape(w_pool.shape)
    return (loss, grad_x, *[per[0][n] for n in order], *[per[1][n] for n in order],
            *[per[2][n] for n in order], *[per[3][n] for n in order])
```

```python
import functools

import jax
import jax.numpy as jnp
from jax import lax
from jax.experimental import pallas as pl
from jax.experimental.pallas import tpu as pltpu

F32 = jnp.float32
BF16 = jnp.bfloat16

D_MODEL = 1024
N_META = 16
POOL_WIDTH = 512
POOL_WINDOWS = (2, 4, 8, 16)
POOL_GROUP_DIM = 128
POOL_HALO = 16
MLA_HEADS = 8
QK_NOPE_DIM = 64
QK_ROPE_DIM = 32
QK_HEAD_DIM = 96
V_HEAD_DIM = 64
HEAD_PAD = 128
Q_LORA_RANK = 384
KV_LORA_RANK = 256
ROPE_THETA = 10000.0
RMS_EPS = 1e-6
ATTN_SCALE = QK_HEAD_DIM ** -0.5
LOG2_E = 1.4426950408889634
SCORE_SCALE = ATTN_SCALE * LOG2_E

ADAM_LR = 0.001
ADAM_B1 = 0.9
ADAM_B2 = 0.999
ADAM_EPS = 1e-08
ADAM_WD = 0.01
ADAM_STEP = 10

N_DEV = 8
N_CHIPS = 4
LANES = 128
ROW_TILE = 384
LONG_TILE = 1056
WIDE_TILE = 1056
VMEM_LIMIT = 56 * 1024 * 1024


def _params(*sem):
    return pltpu.CompilerParams(dimension_semantics=sem, vmem_limit_bytes=VMEM_LIMIT)


def _wide_tile(rows):
    return WIDE_TILE if rows % WIDE_TILE == 0 else ROW_TILE


def _row_spec(tile, width):
    return pl.BlockSpec((tile, width), lambda i: (i, 0))


def _const_spec(shape):
    return pl.BlockSpec(shape, lambda i: tuple(0 for _ in shape))


def _layer_spec(w, layer):
    return pl.BlockSpec((None,) + w.shape[1:], lambda *_: (layer, 0, 0))


def _gathered_spec(w8, layer):
    return pl.BlockSpec((N_DEV, None) + w8.shape[2:], lambda *_: (0, layer, 0, 0))


def _nt(a, b):
    return lax.dot_general(a, b, (((1,), (1,)), ((), ())), preferred_element_type=F32)


def _tn(a, b):
    return lax.dot_general(a, b, (((0,), (0,)), ((), ())), preferred_element_type=F32)


def _nn(a, b):
    return jnp.dot(a, b, preferred_element_type=F32)


def _silu(g):
    return g * (1.0 / (1.0 + jnp.exp(-g)))


def _rms(xf):
    return lax.rsqrt(jnp.mean(xf * xf, axis=-1, keepdims=True) + RMS_EPS)


def norm_mm(x, gamma, wt, layer, splits, dtypes, name):
    L, K = x.shape
    tm = _wide_tile(L)

    def body(x_ref, g_ref, w_ref, a_ref, *z_refs):
        xf = x_ref[...]
        a = ((xf * _rms(xf)) * g_ref[...]).astype(BF16)
        a_ref[...] = a
        z = _nt(a, w_ref[...])
        for (s, n), zr in zip(splits, z_refs):
            zr[...] = z[:, s:s + n].astype(zr.dtype)

    widths = [n for _, n in splits]
    return pl.pallas_call(
        body,
        grid=(L // tm,),
        in_specs=[_row_spec(tm, K), _const_spec((1, K)), _layer_spec(wt, layer)],
        out_specs=[_row_spec(tm, K)] + [_row_spec(tm, n) for n in widths],
        out_shape=[jax.ShapeDtypeStruct((L, K), BF16)]
        + [jax.ShapeDtypeStruct((L, n), dt) for n, dt in zip(widths, dtypes)],
        compiler_params=_params("arbitrary"),
        name=name,
    )(x, gamma, wt)


def norm_mm_heads(x, gamma, w8, layer, name, transposed=False):
    L, K = x.shape
    hw = w8.shape[-2] if transposed else w8.shape[-1]
    tm = _wide_tile(L)

    def body(x_ref, g_ref, w_ref, a_ref, z_ref):
        xf = x_ref[...]
        a = ((xf * _rms(xf)) * g_ref[...]).astype(BF16)
        a_ref[...] = a
        if hw < HEAD_PAD:
            z_ref[...] = jnp.zeros_like(z_ref)
        for j in range(MLA_HEADS):
            z_ref[:, j * HEAD_PAD:j * HEAD_PAD + hw] = _nt(a, w_ref[j]) if transposed else _nn(a, w_ref[j])

    return pl.pallas_call(
        body,
        grid=(L // tm,),
        in_specs=[_row_spec(tm, K), _const_spec((1, K)), _gathered_spec(w8, layer)],
        out_specs=[_row_spec(tm, K), _row_spec(tm, MLA_HEADS * HEAD_PAD)],
        out_shape=[jax.ShapeDtypeStruct((L, K), BF16), jax.ShapeDtypeStruct((L, MLA_HEADS * HEAD_PAD), F32)],
        compiler_params=_params("arbitrary"),
        name=name,
    )(x, gamma, w8)


FF_GROUP = 4


def _ff_spec(w8, layer):
    return pl.BlockSpec((FF_GROUP, None) + w8.shape[2:], lambda j, i: (j, layer, 0, 0))


def ffn_up(g, w_gate8, w_up8, layer, name):
    L, K = g.shape
    fb = w_gate8.shape[-2]
    tm, tf = _wide_tile(L), FF_GROUP * fb

    def body(a_ref, wg_ref, wu_ref, gate_ref, up_ref, act_ref):
        a = a_ref[...]
        gate = _nt(a, wg_ref[...].reshape(tf, K))
        up = _nt(a, wu_ref[...].reshape(tf, K))
        gate_ref[...] = gate.astype(BF16)
        up_ref[...] = up.astype(BF16)
        act_ref[...] = (_silu(gate) * up).astype(BF16)

    tile = pl.BlockSpec((tm, tf), lambda j, i: (i, j))
    F = N_DEV * fb
    return pl.pallas_call(
        body,
        grid=(N_DEV // FF_GROUP, L // tm),
        in_specs=[pl.BlockSpec((tm, K), lambda j, i: (i, 0)), _ff_spec(w_gate8, layer), _ff_spec(w_up8, layer)],
        out_specs=[tile, tile, tile],
        out_shape=[
            jax.ShapeDtypeStruct((L, F), BF16),
            jax.ShapeDtypeStruct((L, F), BF16),
            jax.ShapeDtypeStruct((L, F), BF16),
        ],
        compiler_params=_params("arbitrary", "arbitrary"),
        name=name,
    )(g, w_gate8, w_up8)


def mm_res(a, w8, res, layer, name, gamma=None):
    L = a.shape[0]
    kb, N = w8.shape[-2:]
    tm = _wide_tile(L)
    normed = gamma is not None

    def body(a_ref, w_ref, r_ref, *rest):
        out = r_ref[...] + _nn(a_ref[...], w_ref[...].reshape(N_DEV * kb, N))
        if normed:
            g_ref, o_ref, n_ref = rest
            n_ref[...] = ((out * _rms(out)) * g_ref[...]).astype(BF16)
        else:
            (o_ref,) = rest
        o_ref[...] = out

    in_specs = [_row_spec(tm, N_DEV * kb), _gathered_spec(w8, layer), _row_spec(tm, N)]
    out_specs = [_row_spec(tm, N)]
    out_shape = [jax.ShapeDtypeStruct((L, N), F32)]
    args = [a, w8, res]
    if normed:
        in_specs.append(_const_spec((1, N)))
        out_specs.append(_row_spec(tm, N))
        out_shape.append(jax.ShapeDtypeStruct((L, N), BF16))
        args.append(gamma)
    return pl.pallas_call(
        body,
        grid=(L // tm,),
        in_specs=in_specs,
        out_specs=out_specs,
        out_shape=out_shape,
        compiler_params=_params("arbitrary"),
        name=name,
    )(*args)


def out_proj_bwd(dz, w8, layer, o, name, pair=()):
    L, N = dz.shape
    kb = w8.shape[-2]
    C = o.shape[1]
    tm = _wide_tile(L)
    n_p = len(pair)
    n_steps = L // tm

    def body(dz_ref, w_ref, o_ref, *rest):
        dy_ref, do_ref, dl_ref = rest[n_p:n_p + 3]
        if n_p:
            start, finish = _pair_steps(rest[:n_p], rest[n_p + 3:2 * n_p + 3], *rest[2 * n_p + 3:])
            pl.when(pl.program_id(0) == 0)(start)
        d = _nt(dz_ref[...].astype(BF16), w_ref[...].reshape(N_DEV * kb, N))
        dy_ref[...] = d[:, :C]
        do_ref[...] = d[:, C:]
        left = lax.broadcasted_iota(jnp.int32, (tm, LANES), 1) < V_HEAD_DIM
        for p in range(C // LANES):
            cols = slice(p * LANES, (p + 1) * LANES)
            prod = d[:, C + p * LANES:C + (p + 1) * LANES] * o_ref[:, cols]
            d0 = jnp.sum(jnp.where(left, prod, 0.0), axis=1, keepdims=True)
            d1 = jnp.sum(jnp.where(left, 0.0, prod), axis=1, keepdims=True)
            dl_ref[:, cols] = jnp.where(left, d0, d1)
        if n_p:
            pl.when(pl.program_id(0) == n_steps - 1)(finish)

    return pl.pallas_call(
        body,
        grid=(n_steps,),
        in_specs=[_row_spec(tm, N), _gathered_spec(w8, layer), _row_spec(tm, C)] + [_ANY] * n_p,
        out_specs=[_row_spec(tm, C)] * 3 + [_ANY] * n_p,
        out_shape=[jax.ShapeDtypeStruct((L, C), F32)] * 3
        + [jax.ShapeDtypeStruct((N_CHIPS,) + p.shape[1:], p.dtype) for p in pair],
        scratch_shapes=[pltpu.SemaphoreType.DMA((N_CHIPS * n_p,))] * 2 if n_p else [],
        compiler_params=_params("arbitrary"),
        name=name,
    )(dz, w8, o, *pair)


def mm_nt_normbwd(terms, x, gamma, dres, layer, name, wide=True):
    L, K = x.shape
    tm = _wide_tile(L) if wide else ROW_TILE
    n_terms = len(terms)
    has_res = dres is not None
    weights = []
    for t in terms:
        if not any(t[2] is u for u in weights):
            weights.append(t[2])
    which = [[t[2] is u for u in weights].index(True) for t in terms]
    n_in = n_terms + len(weights)

    def body(*refs):
        dz_refs = refs[:n_terms]
        w_refs = [refs[n_terms + n] for n in which]
        x_ref, g_ref = refs[n_in], refs[n_in + 1]
        pos = n_in + 2
        r_ref = refs[pos] if has_res else None
        dx_ref, dg_ref = refs[pos + has_res], refs[pos + has_res + 1]
        da = None
        for t, dz_ref, w_ref in zip(terms, dz_refs, w_refs):
            if t[0] == "rows":
                n, at = t[1].shape[1], t[3]
                parts = [_nn(dz_ref[...].astype(BF16), w_ref[at:at + n, :])]
            elif t[0] == "heads":
                hw = t[2].shape[-1]
                parts = [_nt(dz_ref[:, j * HEAD_PAD:j * HEAD_PAD + hw].astype(BF16), w_ref[j])
                         for j in range(MLA_HEADS)]
            elif t[0] == "heads_t":
                hw = t[2].shape[-2]
                parts = [_nn(dz_ref[:, j * HEAD_PAD:j * HEAD_PAD + hw].astype(BF16), w_ref[j])
                         for j in range(MLA_HEADS)]
            else:
                nb = t[2].shape[-2]
                parts = [_nn(dz_ref[...].astype(BF16), w_ref[...].reshape(N_DEV * nb, K))]
            for p in parts:
                da = p if da is None else da + p
        xf = x_ref[...]
        r = _rms(xf)
        xh = xf * r

        @pl.when(pl.program_id(0) == 0)
        def _():
            dg_ref[...] = jnp.zeros_like(dg_ref)

        dg_ref[...] += jnp.sum(da * xh, axis=0, keepdims=True)
        dxh = da * g_ref[...]
        dx = r * (dxh - xh * jnp.mean(dxh * xh, axis=-1, keepdims=True))
        if has_res:
            dx = dx + r_ref[...]
        dx_ref[...] = dx

    in_specs = [_row_spec(tm, t[1].shape[1]) for t in terms]
    for w in weights:
        in_specs.append(_layer_spec(w, layer) if w.ndim == 3 else _gathered_spec(w, layer))
    in_specs += [_row_spec(tm, K), _const_spec((1, K))]
    args = [t[1] for t in terms] + weights + [x, gamma]
    if has_res:
        in_specs.append(_row_spec(tm, K))
        args.append(dres)
    return pl.pallas_call(
        body,
        grid=(L // tm,),
        in_specs=in_specs,
        out_specs=[_row_spec(tm, K), _const_spec((1, K))],
        out_shape=[jax.ShapeDtypeStruct((L, K), F32), jax.ShapeDtypeStruct((1, K), F32)],
        compiler_params=_params("arbitrary"),
        name=name,
    )(*args)


MAX_OUT_ROWS = 1408


def mm_tn(a_list, b, name, out_dtype=F32):
    n = len(a_list)
    L, N = b.shape
    tks = [MAX_OUT_ROWS if (a.shape[1] > MAX_OUT_ROWS and a.shape[1] % MAX_OUT_ROWS == 0) else a.shape[1]
           for a in a_list]
    blocks = a_list[0].shape[1] // tks[0]
    assert all(a.shape[1] // tk == blocks for a, tk in zip(a_list, tks))
    tl = LONG_TILE if L % LONG_TILE == 0 else ROW_TILE
    n_l = L // tl

    def body(*refs):
        a_refs, b_ref = refs[:n], refs[n]
        o_refs, accs = refs[n + 1:2 * n + 1], refs[2 * n + 1:]
        l = pl.program_id(1)
        bt = b_ref[...].astype(BF16)
        for a_ref, o_ref, acc in zip(a_refs, o_refs, accs):
            @pl.when(l == 0)
            def _(acc=acc):
                acc[...] = jnp.zeros_like(acc)

            acc[...] += _tn(a_ref[...].astype(BF16), bt)

            @pl.when(l == n_l - 1)
            def _(acc=acc, o_ref=o_ref):
                o_ref[...] = acc[...].astype(o_ref.dtype)

    return pl.pallas_call(
        body,
        grid=(blocks, n_l),
        in_specs=[pl.BlockSpec((tl, tk), lambda j, l: (l, j)) for tk in tks]
        + [pl.BlockSpec((tl, N), lambda j, l: (l, 0))],
        out_specs=[pl.BlockSpec((tk, N), lambda j, l: (j, 0)) for tk in tks],
        out_shape=[jax.ShapeDtypeStruct((a.shape[1], N), out_dtype) for a in a_list],
        scratch_shapes=[pltpu.VMEM((tk, N), F32) for tk in tks],
        compiler_params=_params("arbitrary", "arbitrary"),
        name=name,
    )(*a_list, b)


def ffn_bwd_act(dh, w_down8, layer, gate, up, name):
    L, K = dh.shape
    fb = w_down8.shape[-2]
    tm, tf = _wide_tile(L), FF_GROUP * fb

    def body(dh_ref, w_ref, gate_ref, up_ref, dgate_ref, dup_ref):
        dact = _nt(dh_ref[...].astype(BF16), w_ref[...].reshape(tf, K))
        g = gate_ref[...].astype(F32)
        sig = 0.5 * jnp.tanh(0.5 * g) + 0.5
        dup_ref[...] = (dact * (g * sig)).astype(BF16)
        dgate_ref[...] = (dact * up_ref[...].astype(F32) * (sig * (1.0 + g * (1.0 - sig)))).astype(BF16)

    tile = pl.BlockSpec((tm, tf), lambda j, i: (i, j))
    F = N_DEV * fb
    return pl.pallas_call(
        body,
        grid=(N_DEV // FF_GROUP, L // tm),
        in_specs=[pl.BlockSpec((tm, K), lambda j, i: (i, 0)), _ff_spec(w_down8, layer), tile, tile],
        out_specs=[tile, tile],
        out_shape=[jax.ShapeDtypeStruct((L, F), BF16), jax.ShapeDtypeStruct((L, F), BF16)],
        compiler_params=_params("arbitrary", "arbitrary"),
        name=name,
    )(dh, w_down8, gate, up)


def _pool_residual(scr, lo, tm, g, w, t):
    cols = slice(g * POOL_GROUP_DIM, (g + 1) * POOL_GROUP_DIM)
    cur = scr[lo:lo + tm, cols]
    s = cur
    for k in range(1, w):
        s = s + scr[lo - k:lo - k + tm, cols]
    cnt = jnp.minimum(t + 1, w).astype(F32)
    return s / cnt - cur


def pool_fwd(u, w_pool, scale, name):
    L, C = u.shape
    tm, halo = _wide_tile(L), POOL_HALO

    def body(u_ref, halo_ref, w_ref, s_ref, y_ref, scr):
        i = pl.program_id(0)
        scr[0:halo, :] = jnp.where(i > 0, halo_ref[...], 0.0)
        scr[halo:halo + tm, :] = u_ref[...]
        t = i * tm + lax.broadcasted_iota(jnp.int32, (tm, POOL_GROUP_DIM), 0)
        for g, w in enumerate(POOL_WINDOWS):
            cols = slice(g * POOL_GROUP_DIM, (g + 1) * POOL_GROUP_DIM)
            p = _pool_residual(scr, halo, tm, g, w, t)
            y = _nn(p.astype(BF16), w_ref[g]) * s_ref[:, cols]
            y_ref[:, cols] = y.astype(y_ref.dtype)

    return pl.pallas_call(
        body,
        grid=(L // tm,),
        in_specs=[
            _row_spec(tm, C),
            pl.BlockSpec((halo, C), lambda i: (jnp.maximum(i * (tm // halo) - 1, 0), 0)),
            _const_spec(w_pool.shape),
            _const_spec((1, C)),
        ],
        out_specs=_row_spec(tm, C),
        out_shape=jax.ShapeDtypeStruct((L, 2 * C), BF16),
        scratch_shapes=[pltpu.VMEM((tm + halo, C), F32)],
        compiler_params=_params("arbitrary"),
        name=name,
    )(u, u, w_pool, scale)


def pool_bwd(u, dy, w_pool, scale, name):
    L, C = u.shape
    tm, halo = _wide_tile(L), POOL_HALO
    n_tiles = L // tm
    last_halo = L // halo - 1

    def body(u_ref, uh_ref, dy_ref, dyh_ref, w_ref, s_ref, du_ref, dw_ref, ds_ref, scr_u, scr_q):
        i = pl.program_id(0)

        @pl.when(i == 0)
        def _():
            dw_ref[...] = jnp.zeros_like(dw_ref)
            ds_ref[...] = jnp.zeros_like(ds_ref)

        scr_u[0:halo, :] = jnp.where(i > 0, uh_ref[...], 0.0)
        scr_u[halo:halo + tm, :] = u_ref[...]
        t = i * tm + lax.broadcasted_iota(jnp.int32, (tm, POOL_GROUP_DIM), 0)
        th = (i + 1) * tm + lax.broadcasted_iota(jnp.int32, (halo, POOL_GROUP_DIM), 0)
        for g, w in enumerate(POOL_WINDOWS):
            cols = slice(g * POOL_GROUP_DIM, (g + 1) * POOL_GROUP_DIM)
            p = _pool_residual(scr_u, halo, tm, g, w, t).astype(BF16)
            wg = w_ref[g]
            sc = s_ref[:, cols]
            dy = dy_ref[:, cols]
            ds_ref[:, cols] += jnp.sum(dy * _nn(p, wg), axis=0, keepdims=True)
            dys = (dy * sc).astype(BF16)
            dw_ref[g] += _tn(p, dys)
            dp = _nt(dys, wg)
            dyh = jnp.where(i < n_tiles - 1, dyh_ref[:, cols], 0.0)
            dph = _nt((dyh * sc).astype(BF16), wg)
            scr_q[0:tm, cols] = dp / jnp.minimum(t + 1, w).astype(F32)
            scr_q[tm:tm + halo, cols] = dph / jnp.minimum(th + 1, w).astype(F32)
            acc = scr_q[0:tm, cols]
            for k in range(1, w):
                acc = acc + scr_q[k:k + tm, cols]
            du_ref[:, cols] = acc - dp

    return pl.pallas_call(
        body,
        grid=(n_tiles,),
        in_specs=[
            _row_spec(tm, C),
            pl.BlockSpec((halo, C), lambda i: (jnp.maximum(i * (tm // halo) - 1, 0), 0)),
            _row_spec(tm, C),
            pl.BlockSpec((halo, C), lambda i: (jnp.minimum((i + 1) * (tm // halo), last_halo), 0)),
            _const_spec(w_pool.shape),
            _const_spec((1, C)),
        ],
        out_specs=[_row_spec(tm, C), _const_spec(w_pool.shape), _const_spec((1, C))],
        out_shape=[
            jax.ShapeDtypeStruct((L, C), F32),
            jax.ShapeDtypeStruct(w_pool.shape, F32),
            jax.ShapeDtypeStruct((1, C), F32),
        ],
        scratch_shapes=[pltpu.VMEM((tm + halo, C), F32), pltpu.VMEM((tm + halo, C), F32)],
        compiler_params=_params("arbitrary"),
        name=name,
    )(u, u, dy, dy, w_pool, scale)


def _head_masks(rows):
    lane = lax.broadcasted_iota(jnp.int32, (rows, HEAD_PAD), 1)
    return lane, lane < QK_NOPE_DIM, (lane >= QK_NOPE_DIM) & (lane < QK_HEAD_DIM)


def _rope_swap(x, lane):
    half = QK_ROPE_DIM // 2
    swapped = jnp.where(lane < QK_NOPE_DIM + half, pltpu.roll(x, HEAD_PAD - half, 1), pltpu.roll(x, half, 1))
    return jnp.where((lane >= QK_NOPE_DIM) & (lane < QK_HEAD_DIM), swapped, 0.0)


def _seg_mean(v, seg_ref):
    hi = v.astype(BF16)
    lo = (v - hi.astype(F32)).astype(BF16)
    seg = seg_ref[...]
    return _nn(hi, seg) + _nn(lo, seg)


def _segment_matrix():
    lane = jnp.arange(HEAD_PAD)
    seg = jnp.where(lane < QK_NOPE_DIM, 0, jnp.where(lane < QK_HEAD_DIM, 1, 2))
    inv = jnp.where(lane < QK_NOPE_DIM, 1.0 / QK_NOPE_DIM, jnp.where(lane < QK_HEAD_DIM, 1.0 / QK_ROPE_DIM, 0.0))
    return jnp.where(seg[:, None] == seg[None, :], inv[None, :], 0.0).astype(BF16)


def qk_prep_fwd(q, kv, kr, cosf, sins, seg, gq, gkn, gkr, name):
    L = q.shape[0]
    tm = _wide_tile(L)
    W = MLA_HEADS * HEAD_PAD

    def body(q_ref, kv_ref, kr_ref, c_ref, s_ref, seg_ref, gq_ref, gkn_ref, gkr_ref, qo_ref, ko_ref, vo_ref):
        lane, m_n, _ = _head_masks(tm)
        cosf_, sins_ = c_ref[...], s_ref[...]
        kr_ = kr_ref[...]
        rk = lax.rsqrt(_seg_mean(kr_ * kr_, seg_ref) + RMS_EPS)
        krn = kr_ * rk * gkr_ref[...]
        krf = krn * cosf_ + _rope_swap(krn, lane) * sins_
        for h in range(MLA_HEADS):
            cols = slice(h * HEAD_PAD, (h + 1) * HEAD_PAD)
            qh = q_ref[:, cols]
            qn = qh * lax.rsqrt(_seg_mean(qh * qh, seg_ref) + RMS_EPS) * gq_ref[...]
            qo_ref[:, cols] = (qn * cosf_ + _rope_swap(qn, lane) * sins_).astype(BF16)
            kh = jnp.where(m_n, kv_ref[:, cols], 0.0)
            rn = lax.rsqrt(_seg_mean(kh * kh, seg_ref) + RMS_EPS)
            ko_ref[:, cols] = (kh * rn * gkn_ref[...] + krf).astype(BF16)
        for p in range(MLA_HEADS // 2):
            even = kv_ref[:, 2 * p * HEAD_PAD:(2 * p + 1) * HEAD_PAD]
            odd = kv_ref[:, (2 * p + 1) * HEAD_PAD:(2 * p + 2) * HEAD_PAD]
            pair = jnp.where(m_n, pltpu.roll(even, V_HEAD_DIM, 1), odd)
            vo_ref[:, p * LANES:(p + 1) * LANES] = pair.astype(BF16)

    vec = _const_spec((1, HEAD_PAD))
    return pl.pallas_call(
        body,
        grid=(L // tm,),
        in_specs=[_row_spec(tm, W), _row_spec(tm, W), _row_spec(tm, HEAD_PAD), _row_spec(tm, HEAD_PAD),
                  _row_spec(tm, HEAD_PAD), _const_spec((HEAD_PAD, HEAD_PAD)), vec, vec, vec],
        out_specs=[_row_spec(tm, W), _row_spec(tm, W), _row_spec(tm, W // 2)],
        out_shape=[jax.ShapeDtypeStruct((L, W), BF16), jax.ShapeDtypeStruct((L, W), BF16),
                   jax.ShapeDtypeStruct((L, W // 2), BF16)],
        compiler_params=_params("arbitrary"),
        name=name,
    )(q, kv, kr, cosf, sins, seg, gq, gkn, gkr)


def qk_prep_bwd(dqo, dko, dv, q, kv, kr, cosf, sins, seg, gq, gkn, gkr, name):
    L = q.shape[0]
    tm = ROW_TILE
    W = MLA_HEADS * HEAD_PAD

    def body(dqo_ref, dko_ref, dv_ref, q_ref, kv_ref, kr_ref, c_ref, s_ref, seg_ref, gq_ref, gkn_ref, gkr_ref,
             dq_ref, dkv_ref, dkr_ref, dgq_ref, dgkn_ref, dgkr_ref):
        @pl.when(pl.program_id(0) == 0)
        def _():
            dgq_ref[...] = jnp.zeros_like(dgq_ref)
            dgkn_ref[...] = jnp.zeros_like(dgkn_ref)
            dgkr_ref[...] = jnp.zeros_like(dgkr_ref)

        lane, m_n, _ = _head_masks(tm)
        cosf_, sins_ = c_ref[...], s_ref[...]
        dgq = jnp.zeros((1, HEAD_PAD), F32)
        dgkn = jnp.zeros((1, HEAD_PAD), F32)
        dkrf = jnp.zeros((tm, HEAD_PAD), F32)
        for h in range(MLA_HEADS):
            cols = slice(h * HEAD_PAD, (h + 1) * HEAD_PAD)
            dy = dqo_ref[:, cols]
            dqn = dy * cosf_ + _rope_swap(dy * sins_, lane)
            qh = q_ref[:, cols]
            rinv = lax.rsqrt(_seg_mean(qh * qh, seg_ref) + RMS_EPS)
            xh = qh * rinv
            dgq = dgq + jnp.sum(dqn * xh, axis=0, keepdims=True)
            dxh = dqn * gq_ref[...]
            dq_ref[:, cols] = rinv * (dxh - xh * _seg_mean(dxh * xh, seg_ref))
            dk = dko_ref[:, cols]
            dkrf = dkrf + dk
            kh = jnp.where(m_n, kv_ref[:, cols], 0.0)
            rn = lax.rsqrt(_seg_mean(kh * kh, seg_ref) + RMS_EPS)
            xk = kh * rn
            dgkn = dgkn + jnp.sum(dk * xk, axis=0, keepdims=True)
            dxk = dk * gkn_ref[...]
            dkn = rn * (dxk - xk * _seg_mean(dxk * xk, seg_ref))
            dvp = dv_ref[:, (h // 2) * LANES:(h // 2 + 1) * LANES]
            dvh = pltpu.roll(dvp, V_HEAD_DIM, 1) if h % 2 == 0 else dvp
            dkv_ref[:, cols] = jnp.where(m_n, dkn, dvh)
        kr_ = kr_ref[...]
        rk = lax.rsqrt(_seg_mean(kr_ * kr_, seg_ref) + RMS_EPS)
        xr = kr_ * rk
        dkrn = dkrf * cosf_ + _rope_swap(dkrf * sins_, lane)
        dgkr_ref[...] += jnp.sum(dkrn * xr, axis=0, keepdims=True)
        dxr = dkrn * gkr_ref[...]
        dkr_ref[...] = rk * (dxr - xr * _seg_mean(dxr * xr, seg_ref))
        dgq_ref[...] += dgq
        dgkn_ref[...] += dgkn

    vec = _const_spec((1, HEAD_PAD))
    return pl.pallas_call(
        body,
        grid=(L // tm,),
        in_specs=[_row_spec(tm, W), _row_spec(tm, W), _row_spec(tm, W // 2), _row_spec(tm, W), _row_spec(tm, W),
                  _row_spec(tm, HEAD_PAD), _row_spec(tm, HEAD_PAD), _row_spec(tm, HEAD_PAD),
                  _const_spec((HEAD_PAD, HEAD_PAD)), vec, vec, vec],
        out_specs=[_row_spec(tm, W), _row_spec(tm, W), _row_spec(tm, HEAD_PAD), vec, vec, vec],
        out_shape=[jax.ShapeDtypeStruct((L, W), F32), jax.ShapeDtypeStruct((L, W), F32),
                   jax.ShapeDtypeStruct((L, HEAD_PAD), F32)] + [jax.ShapeDtypeStruct((1, HEAD_PAD), F32)] * 3,
        compiler_params=_params("arbitrary"),
        name=name,
    )(dqo, dko, dv, q, kv, kr, cosf, sins, seg, gq, gkn, gkr)


def attn_fwd(qp, kp, v, cat, name, gather=()):
    L = qp.shape[0]
    tq = ROW_TILE
    n_i = L // tq
    pair_w = 2 * HEAD_PAD
    n_pairs = MLA_HEADS // 2
    n_g = len(gather)

    def body(q_ref, k_ref, v_ref, cat_in, *rest):
        del cat_in
        o_ref, lse_ref, cat_ref = rest[n_g:n_g + 3]
        i = pl.program_id(1)
        if n_g:
            pair = pl.program_id(0)
            start, forward, finish = _gather_steps(rest[:n_g], rest[n_g + 3:2 * n_g + 3], *rest[2 * n_g + 3:])
            pl.when((pair == 0) & (i == 0))(start)
            pl.when((pair == n_pairs // 2) & (i == 0))(forward)
        left = lax.broadcasted_iota(jnp.int32, (tq, LANES), 1) < V_HEAD_DIM

        def step(t, carry, masked, width=1):
            rows = pl.ds(pl.multiple_of(t * (tq * width), tq), tq * width)
            vv = v_ref[rows, :]
            out = []
            for hh in range(2):
                cols = slice(hh * HEAD_PAD, (hh + 1) * HEAD_PAD)
                m, l, acc = carry[hh]
                s = _nt(q_ref[:, cols], k_ref[rows, cols]) * SCORE_SCALE
                if masked:
                    row = lax.broadcasted_iota(jnp.int32, s.shape, 0)
                    col = lax.broadcasted_iota(jnp.int32, s.shape, 1)
                    s = jnp.where(col <= row, s, -jnp.inf)
                m_new = jnp.maximum(m, jnp.max(s, axis=1, keepdims=True))
                alpha = jnp.exp2(m - m_new)
                p = jnp.exp2(s - m_new)
                l = alpha * l + jnp.sum(p, axis=1, keepdims=True)
                acc = alpha * acc + _nn(p.astype(BF16), vv)
                out.append((m_new, l, acc))
            return tuple(out)

        one = (jnp.full((tq, 1), -jnp.inf, F32), jnp.zeros((tq, 1), F32), jnp.zeros((tq, LANES), F32))
        doubles = jnp.right_shift(i, 1)
        carry = lax.fori_loop(0, doubles, functools.partial(step, masked=False, width=2), (one, one))
        carry = lax.fori_loop(2 * doubles, i, functools.partial(step, masked=False), carry)
        (m0, l0, a0), (m1, l1, a1) = step(i, carry, True)
        o = jnp.where(left, a0 / l0, a1 / l1)
        o_ref[...] = o
        cat_ref[...] = o.astype(BF16)
        lse_ref[...] = jnp.where(left, m0 + jnp.log2(l0), m1 + jnp.log2(l1))
        if n_g:
            pl.when((pair == n_pairs - 1) & (i == n_i - 1))(finish)

    return pl.pallas_call(
        body,
        grid=(n_pairs, n_i),
        in_specs=[
            pl.BlockSpec((tq, pair_w), lambda p, i: (i, p)),
            pl.BlockSpec((L, pair_w), lambda p, i: (0, p)),
            pl.BlockSpec((L, LANES), lambda p, i: (0, p)),
            _ANY,
        ] + [_ANY] * n_g,
        out_specs=[
            pl.BlockSpec((tq, LANES), lambda p, i: (i, p)),
            pl.BlockSpec((tq, LANES), lambda p, i: (i, p)),
            pl.BlockSpec((tq, LANES), lambda p, i: (i, n_pairs + p)),
        ] + [_ANY] * n_g,
        out_shape=[jax.ShapeDtypeStruct((L, n_pairs * LANES), F32), jax.ShapeDtypeStruct((L, n_pairs * LANES), F32),
                   jax.ShapeDtypeStruct(cat.shape, cat.dtype)]
        + [jax.ShapeDtypeStruct((N_DEV,) + g.shape, g.dtype) for g in gather],
        scratch_shapes=_gather_scratch(n_g) if n_g else [],
        input_output_aliases={3: 2},
        compiler_params=_params("arbitrary", "arbitrary"),
        name=name,
    )(qp, kp, v, cat, *gather)


def attn_bwd(qp, kp, v, do, lse, delta, name, exchange=(), gather=()):
    L = qp.shape[0]
    tq = ROW_TILE
    n_q = L // tq
    pair_w = 2 * HEAD_PAD
    n_pairs = MLA_HEADS // 2
    n_x, n_g = len(exchange), len(gather)
    n_r = n_x + n_g

    def body(q_ref, k_ref, v_ref, do_ref, lse_ref, dl_ref, *rest):
        dq_ref, dk_ref, dv_ref = rest[n_r:n_r + 3]
        riders_in, riders_out, sems = rest[:n_r], rest[n_r + 3:2 * n_r + 3], rest[2 * n_r + 3:]
        j = pl.program_id(1)
        pair = pl.program_id(0)
        steps = []
        if n_x:
            steps.append(_chip_exchange_steps(riders_in[:n_x], riders_out[:n_x], *sems[:3]))
        if n_g:
            steps.append(_direct_gather_steps(riders_in[n_x:], riders_out[n_x:], *sems[-3:]))
        for start, _ in steps:
            pl.when((pair == 0) & (j == 0))(start)

        @pl.when(j == 0)
        def _():
            dq_ref[...] = jnp.zeros_like(dq_ref)

        dk_ref[...] = jnp.zeros_like(dk_ref)
        dv_ref[...] = jnp.zeros_like(dv_ref)

        def step(t, carry, masked, width=1, first=0):
            rows = pl.ds(pl.multiple_of((first + t * width) * tq, tq), tq * width)
            do = do_ref[rows, :]
            left = lax.broadcasted_iota(jnp.int32, do.shape, 1) < V_HEAD_DIM
            vv = v_ref[...]
            dv = None
            for hh in range(2):
                cols = slice(hh * HEAD_PAD, (hh + 1) * HEAD_PAD)
                stat = slice(hh * V_HEAD_DIM, hh * V_HEAD_DIM + 1)
                q = q_ref[rows, cols]
                k = k_ref[:, cols]
                dom = jnp.where(left if hh == 0 else jnp.logical_not(left), do, 0.0).astype(BF16)
                s = _nt(q, k) * SCORE_SCALE
                p = jnp.exp2(s - lse_ref[rows, stat])
                if masked:
                    row = lax.broadcasted_iota(jnp.int32, p.shape, 0)
                    col = lax.broadcasted_iota(jnp.int32, p.shape, 1)
                    p = jnp.where(col <= row, p, 0.0)
                dp = _nt(dom, vv)
                ds = (p * (dp - dl_ref[rows, stat]) * ATTN_SCALE).astype(BF16)
                dq_ref[rows, cols] += _nn(ds, k)
                dk_ref[:, cols] += _tn(ds, q)
                t = _tn(p.astype(BF16), dom)
                dv = t if dv is None else dv + t
            dv_ref[...] += dv
            return carry

        @pl.when(j < n_q - 1)
        def _():
            step(0, 0, True, width=2, first=j)
            doubles = jnp.right_shift(n_q - 2 - j, 1)
            lax.fori_loop(0, doubles, functools.partial(step, masked=False, width=2, first=j + 2), 0)
            lax.fori_loop(j + 2 + 2 * doubles, n_q, functools.partial(step, masked=False), 0)

        @pl.when(j == n_q - 1)
        def _():
            step(j, 0, True)

        for _, finish in steps:
            pl.when((pair == n_pairs - 1) & (j == n_q - 1))(finish)

    return pl.pallas_call(
        body,
        grid=(n_pairs, n_q),
        in_specs=[
            pl.BlockSpec((L, pair_w), lambda p, j: (0, p)),
            pl.BlockSpec((tq, pair_w), lambda p, j: (j, p)),
            pl.BlockSpec((tq, LANES), lambda p, j: (j, p)),
            pl.BlockSpec((L, LANES), lambda p, j: (0, p)),
            pl.BlockSpec((L, LANES), lambda p, j: (0, p)),
            pl.BlockSpec((L, LANES), lambda p, j: (0, p)),
        ] + [_ANY] * n_r,
        out_specs=[
            pl.BlockSpec((L, pair_w), lambda p, j: (0, p)),
            pl.BlockSpec((tq, pair_w), lambda p, j: (j, p)),
            pl.BlockSpec((tq, LANES), lambda p, j: (j, p)),
        ] + [_ANY] * n_r,
        out_shape=[
            jax.ShapeDtypeStruct((L, n_pairs * pair_w), F32),
            jax.ShapeDtypeStruct((L, n_pairs * pair_w), F32),
            jax.ShapeDtypeStruct((L, n_pairs * LANES), F32),
        ] + [jax.ShapeDtypeStruct(e.shape, e.dtype) for e in exchange]
        + [jax.ShapeDtypeStruct((N_DEV,) + g.shape, g.dtype) for g in gather],
        scratch_shapes=(_chip_exchange_scratch(n_x) if n_x else []) + (_direct_gather_scratch(n_g) if n_g else []),
        compiler_params=_params("arbitrary", "arbitrary"),
        name=name,
    )(qp, kp, v, do, lse, delta, *exchange, *gather)


def loss_head(h, target, n_real, name):
    L, D = h.shape
    tm = _wide_tile(L)

    def body(h_ref, t_ref, dh_ref, sq_ref):
        i = pl.program_id(0)

        @pl.when(i == 0)
        def _():
            sq_ref[...] = jnp.zeros_like(sq_ref)

        t = i * tm + lax.broadcasted_iota(jnp.int32, (tm, D), 0)
        real = (t >= N_META) & (t < N_META + n_real)
        diff = jnp.where(real, h_ref[...] - t_ref[...], 0.0)
        dh_ref[...] = diff * (1.0 / D)
        sq_ref[...] += jnp.sum(diff * diff, axis=0, keepdims=True)

    return pl.pallas_call(
        body,
        grid=(L // tm,),
        in_specs=[_row_spec(tm, D), _row_spec(tm, D)],
        out_specs=[_row_spec(tm, D), _const_spec((1, D))],
        out_shape=[jax.ShapeDtypeStruct((L, D), F32), jax.ShapeDtypeStruct((1, D), F32)],
        compiler_params=_params("arbitrary"),
        name=name,
    )(h, target)


def _mesh_position():
    x, y, c = lax.axis_index("x"), lax.axis_index("y"), lax.axis_index("c")
    return x, y, c, 4 * x + 2 * y + c


def _flip(x, y, c, k):
    px = 1 - x if k & 4 else x
    py = 1 - y if k & 2 else y
    pc = 1 - c if k & 1 else c
    return (px, py, pc), 4 * px + 2 * py + pc


def _other_chips(x, y):
    return [(1 - x, y), (x, 1 - y), (1 - x, 1 - y)]


def _dev_index(px, py, pc):
    return 4 * px + 2 * py + pc


_ANY = pl.BlockSpec(memory_space=pl.ANY)


def cast_bf16(arrays, name):
    n = len(arrays)
    depth = arrays[0].shape[0]

    def body(*refs):
        for k, src in enumerate(refs[:n]):
            for l in range(depth):
                refs[n + k * depth + l][0] = src[l].astype(BF16)

    return pl.pallas_call(
        body,
        out_shape=[jax.ShapeDtypeStruct((1,) + a.shape[1:], BF16) for a in arrays for _ in range(depth)],
        compiler_params=pltpu.CompilerParams(vmem_limit_bytes=VMEM_LIMIT),
        name=name,
    )(*arrays)


def _gather_steps(srcs, dsts, send_sems, recv_sems, local_sems):
    n = len(srcs)
    x, y, c, me = _mesh_position()
    sibling = (x, y, 1 - c)
    chips = _other_chips(x, y)

    def copy(k, b, block, to, from_src=False):
        dst = dsts[b].at[block]
        return pltpu.make_async_remote_copy(
            src_ref=srcs[b] if from_src else dst, dst_ref=dst, send_sem=send_sems.at[k * n + b],
            recv_sem=recv_sems.at[k * n + b], device_id=to, device_id_type=pl.DeviceIdType.MESH)

    def own():
        return [pltpu.make_async_copy(srcs[b], dsts[b].at[me], local_sems.at[b]) for b in range(n)]

    def first():
        out = []
        for b in range(n):
            out.append(copy(0, b, me, sibling, from_src=True))
            out += [copy(1 + q, b, me, (*chip, c), from_src=True) for q, chip in enumerate(chips)]
        return out

    def passed():
        return [copy(4 + q, b, _dev_index(*chip, c), sibling) for q, chip in enumerate(chips) for b in range(n)]

    def start():
        for cp in own() + first():
            cp.start()

    def forward():
        for q, chip in enumerate(chips):
            for b in range(n):
                copy(1 + q, b, _dev_index(*chip, c), sibling).wait_recv()
        for cp in passed():
            cp.start()

    def finish():
        for b in range(n):
            copy(0, b, _dev_index(x, y, 1 - c), sibling).wait_recv()
            for q, chip in enumerate(chips):
                copy(4 + q, b, _dev_index(*chip, 1 - c), sibling).wait_recv()
        for cp in first() + passed():
            cp.wait_send()
        for cp in own():
            cp.wait()

    return start, forward, finish


def _gather_scratch(n):
    copies = N_DEV - 1
    return [pltpu.SemaphoreType.DMA((copies * n,)), pltpu.SemaphoreType.DMA((copies * n,)),
            pltpu.SemaphoreType.DMA((n,))]


def all_gather(payloads, name):
    n = len(payloads)

    def body(*refs):
        start, forward, finish = _gather_steps(refs[:n], refs[n:2 * n], *refs[2 * n:])
        start()
        forward()
        finish()

    return pl.pallas_call(
        body,
        in_specs=[_ANY] * n,
        out_specs=[_ANY] * n,
        out_shape=[jax.ShapeDtypeStruct((N_DEV,) + p.shape, p.dtype) for p in payloads],
        scratch_shapes=_gather_scratch(n),
        name=name,
    )(*payloads)


def _pair_steps(big_refs, sib_refs, send_sems, recv_sems):
    n = len(big_refs)
    x, y, c, me = _mesh_position()

    def copies():
        return [pltpu.make_async_remote_copy(
            src_ref=big_refs[b].at[_dev_index(q // 2, q % 2, 1 - c)], dst_ref=sib_refs[b].at[q],
            send_sem=send_sems.at[b * N_CHIPS + q], recv_sem=recv_sems.at[b * N_CHIPS + q],
            device_id=(x, y, 1 - c), device_id_type=pl.DeviceIdType.MESH) for b in range(n) for q in range(N_CHIPS)]

    def start():
        for cp in copies():
            cp.start()

    def finish():
        for cp in copies():
            cp.wait_recv()
        for cp in copies():
            cp.wait_send()

    return start, finish


def exchange_pair(bigs, small, name):
    n = len(bigs)
    has_small = small is not None
    n_big = N_CHIPS * n
    n_sems = n_big + (N_DEV - 1 if has_small else 0)

    def body(*refs):
        big_refs = refs[:n]
        pos = n + has_small
        sib_refs = refs[pos:pos + n]
        send_sems, recv_sems, local_sem = refs[-3:]
        x, y, c, me = _mesh_position()
        sibling = (x, y, 1 - c)
        copies = []
        for b in range(n):
            for q in range(N_CHIPS):
                copies.append(pltpu.make_async_remote_copy(
                    src_ref=big_refs[b].at[_dev_index(q // 2, q % 2, 1 - c)], dst_ref=sib_refs[b].at[q],
                    send_sem=send_sems.at[b * N_CHIPS + q], recv_sem=recv_sems.at[b * N_CHIPS + q],
                    device_id=sibling, device_id_type=pl.DeviceIdType.MESH))
        waits = list(copies)
        if has_small:
            small_ref, gsmall_ref = refs[n], refs[pos + n]
            own = pltpu.make_async_copy(small_ref.at[me], gsmall_ref.at[me], local_sem.at[0])
            own.start()
            for k in range(1, N_DEV):
                peer, peer_idx = _flip(x, y, c, k)
                s = n_big + k - 1
                copies.append(pltpu.make_async_remote_copy(
                    src_ref=small_ref.at[peer_idx], dst_ref=gsmall_ref.at[me], send_sem=send_sems.at[s],
                    recv_sem=recv_sems.at[s], device_id=peer, device_id_type=pl.DeviceIdType.MESH))
                waits.append(pltpu.make_async_remote_copy(
                    src_ref=small_ref.at[peer_idx], dst_ref=gsmall_ref.at[peer_idx], send_sem=send_sems.at[s],
                    recv_sem=recv_sems.at[s], device_id=peer, device_id_type=pl.DeviceIdType.MESH))
        for cp in copies:
            cp.start()
        for cp in waits:
            cp.wait_recv()
        for cp in copies:
            cp.wait_send()
        if has_small:
            own.wait()

    out_shape = [jax.ShapeDtypeStruct((N_CHIPS,) + b.shape[1:], b.dtype) for b in bigs]
    args = list(bigs)
    if has_small:
        out_shape.append(jax.ShapeDtypeStruct(small.shape, small.dtype))
        args.append(small)
    return pl.pallas_call(
        body,
        in_specs=[_ANY] * len(args),
        out_specs=[_ANY] * len(out_shape),
        out_shape=out_shape,
        scratch_shapes=[pltpu.SemaphoreType.DMA((n_sems,)), pltpu.SemaphoreType.DMA((n_sems,)),
                        pltpu.SemaphoreType.DMA((1,))],
        name=name,
    )(*args)


def pair_sum(bigs, from_sibling, core, name):
    n = len(bigs)

    def body(core_ref, *refs):
        del core_ref
        for mine, sib, out in zip(refs[:n], refs[n:2 * n], refs[2 * n:]):
            out[...] = (mine[...].astype(F32) + sib[...].astype(F32)).astype(out.dtype)

    def slot(shape, picked):
        zeros = (0,) * (len(shape) - 1)
        if picked:
            return pl.BlockSpec((None,) + shape[1:], lambda q, core_ref: (2 * q + core_ref[0],) + zeros)
        return pl.BlockSpec((None,) + shape[1:], lambda q, core_ref: (q,) + zeros)

    grid_spec = pltpu.PrefetchScalarGridSpec(
        num_scalar_prefetch=1,
        grid=(N_CHIPS,),
        in_specs=[slot(b.shape, True) for b in bigs] + [slot(s.shape, False) for s in from_sibling],
        out_specs=[slot(s.shape, False) for s in from_sibling],
    )
    return pl.pallas_call(
        body,
        grid_spec=grid_spec,
        out_shape=[jax.ShapeDtypeStruct(s.shape, s.dtype) for s in from_sibling],
        compiler_params=_params("arbitrary"),
        name=name,
    )(core, *bigs, *from_sibling)


def _chip_exchange_steps(part_refs, got_refs, send_sems, recv_sems, local_sems):
    n = len(part_refs)
    x, y, c, me = _mesh_position()
    mine = 2 * x + y

    def copies(landing):
        out = []
        for q, (px, py) in enumerate(_other_chips(x, y)):
            theirs = 2 * px + py
            for b in range(n):
                out.append(pltpu.make_async_remote_copy(
                    src_ref=part_refs[b].at[theirs], dst_ref=got_refs[b].at[theirs if landing else mine],
                    send_sem=send_sems.at[q * n + b], recv_sem=recv_sems.at[q * n + b],
                    device_id=(px, py, c), device_id_type=pl.DeviceIdType.MESH))
        return out

    def own():
        return [pltpu.make_async_copy(part_refs[b].at[mine], got_refs[b].at[mine], local_sems.at[b])
                for b in range(n)]

    def start():
        for cp in own() + copies(False):
            cp.start()

    def finish():
        for cp in copies(True):
            cp.wait_recv()
        for cp in copies(False):
            cp.wait_send()
        for cp in own():
            cp.wait()

    return start, finish


def _chip_exchange_scratch(n):
    return [pltpu.SemaphoreType.DMA(((N_CHIPS - 1) * n,)), pltpu.SemaphoreType.DMA(((N_CHIPS - 1) * n,)),
            pltpu.SemaphoreType.DMA((n,))]


def _direct_gather_steps(srcs, dsts, send_sems, recv_sems, local_sems):
    n = len(srcs)
    x, y, c, me = _mesh_position()

    def copies(landing):
        out = []
        for k in range(1, N_DEV):
            peer, peer_idx = _flip(x, y, c, k)
            for b in range(n):
                out.append(pltpu.make_async_remote_copy(
                    src_ref=srcs[b], dst_ref=dsts[b].at[peer_idx if landing else me],
                    send_sem=send_sems.at[(k - 1) * n + b], recv_sem=recv_sems.at[(k - 1) * n + b],
                    device_id=peer, device_id_type=pl.DeviceIdType.MESH))
        return out

    def own():
        return [pltpu.make_async_copy(srcs[b], dsts[b].at[me], local_sems.at[b]) for b in range(n)]

    def start():
        for cp in own() + copies(False):
            cp.start()

    def finish():
        for cp in copies(True):
            cp.wait_recv()
        for cp in copies(False):
            cp.wait_send()
        for cp in own():
            cp.wait()

    return start, finish


def _direct_gather_scratch(n):
    return [pltpu.SemaphoreType.DMA(((N_DEV - 1) * n,)), pltpu.SemaphoreType.DMA(((N_DEV - 1) * n,)),
            pltpu.SemaphoreType.DMA((n,))]


def exchange_chips(partials, name):
    n = len(partials)

    def body(*refs):
        start, finish = _chip_exchange_steps(refs[:n], refs[n:2 * n], *refs[2 * n:])
        start()
        finish()

    return pl.pallas_call(
        body,
        in_specs=[_ANY] * n,
        out_specs=[_ANY] * n,
        out_shape=[jax.ShapeDtypeStruct(p.shape, p.dtype) for p in partials],
        scratch_shapes=_chip_exchange_scratch(n),
        name=name,
    )(*partials)


def _adamw_math(g, w, m, v):
    m_new = ADAM_B1 * m + (1.0 - ADAM_B1) * g
    v_new = ADAM_B2 * v + (1.0 - ADAM_B2) * (g * g)
    m_hat = m_new / (1.0 - ADAM_B1 ** ADAM_STEP)
    v_hat = v_new / (1.0 - ADAM_B2 ** ADAM_STEP)
    delta = -ADAM_LR * (m_hat / (jnp.sqrt(v_hat) + ADAM_EPS) + ADAM_WD * w)
    return delta, m_new, v_new


def _chip_total(p_ref):
    g = p_ref[0].astype(F32)
    for q in range(1, N_CHIPS):
        g = g + p_ref[q].astype(F32)
    return g


def chip_sum(parts, name):
    depth = len(parts)

    def body(*refs):
        for l in range(depth):
            refs[depth][l] = _chip_total(refs[l])

    return pl.pallas_call(
        body,
        out_shape=jax.ShapeDtypeStruct((depth,) + parts[0].shape[1:], F32),
        compiler_params=pltpu.CompilerParams(vmem_limit_bytes=VMEM_LIMIT),
        name=name,
    )(*parts)


def adamw_shard(parts, w, m, v, name, grad=None):
    depth = w.shape[0]
    n_in = 1 if grad is not None else depth

    def body(*refs):
        w_ref, m_ref, v_ref, g_ref, d_ref, mo_ref, vo_ref = refs[n_in:]
        for l in range(depth):
            g = refs[0][l] if grad is not None else _chip_total(refs[l])
            delta, m_new, v_new = _adamw_math(g, w_ref[l], m_ref[l], v_ref[l])
            g_ref[l] = g
            d_ref[l] = delta
            mo_ref[l] = m_new
            vo_ref[l] = v_new

    return pl.pallas_call(
        body,
        out_shape=[jax.ShapeDtypeStruct(w.shape, F32)] * 4,
        compiler_params=pltpu.CompilerParams(vmem_limit_bytes=VMEM_LIMIT),
        name=name,
    )(*([grad] if grad is not None else parts), w, m, v)


def adamw_packed(parts, w, m, v, name):
    n_slots = parts.shape[0]

    def body(p_ref, w_ref, m_ref, v_ref, g_ref, d_ref, mo_ref, vo_ref):
        g = p_ref[0]
        for s in range(1, n_slots):
            g = g + p_ref[s]
        delta, m_new, v_new = _adamw_math(g, w_ref[...], m_ref[...], v_ref[...])
        g_ref[...] = g
        d_ref[...] = delta
        mo_ref[...] = m_new
        vo_ref[...] = v_new

    return pl.pallas_call(
        body,
        out_shape=[jax.ShapeDtypeStruct(w.shape, F32)] * 4,
        compiler_params=pltpu.CompilerParams(vmem_limit_bytes=VMEM_LIMIT),
        name=name,
    )(parts, w, m, v)


SHARDED = ("w_in", "w_q_b", "w_kv_b", "w_out", "w_gate", "w_up", "w_down")
TRANSPOSED = ("w_in", "w_q_b", "w_gate", "w_up")
ADAM_TRANSPOSED = ("w_q_b", "w_gate", "w_up")
BEFORE_ATTENTION = ("w_in", "w_q_b", "w_kv_b")
BEFORE_ATTENTION_BWD = ("w_out", "w_gate", "w_up", "w_down")
REPLICATED = ("attn_norm_g", "pool_scale", "q_a_norm_g", "kv_a_norm_g", "q_norm_g", "k_norm_g", "ffn_norm_g")


def _pack_flat(arrays, rows):
    flat = jnp.concatenate([a.reshape(-1) for a in arrays])
    return jnp.pad(flat, (0, rows * LANES - flat.shape[0])).reshape(rows, LANES)


def _unpack_flat(packed, shapes):
    flat, out, at = packed.reshape(-1), [], 0
    for shp in shapes:
        n = 1
        for d in shp:
            n *= d
        out.append(flat[at:at + n].reshape(shp))
        at += n
    return out


def _rope_lane_tables(length):
    inv = 1.0 / (ROPE_THETA ** (jnp.arange(0, QK_ROPE_DIM, 2, dtype=F32) / QK_ROPE_DIM))
    ang = jnp.arange(length, dtype=F32)[:, None] * inv[None, :]
    cos, sin = jnp.cos(ang), jnp.sin(ang)
    ones = jnp.ones((length, QK_NOPE_DIM), F32)
    zeros = jnp.zeros((length, QK_NOPE_DIM), F32)
    tail = HEAD_PAD - QK_HEAD_DIM
    cosf = jnp.concatenate([ones, cos, cos, ones[:, :tail]], axis=1)
    sins = jnp.concatenate([zeros, -sin, sin, zeros[:, :tail]], axis=1)
    return cosf, sins


def _pad_lanes(vec, at, width=HEAD_PAD):
    return jnp.pad(vec, (at, width - at - vec.shape[0])).reshape(1, width)


def kernel(x, meta_tokens, attn_norm_g, w_in, w_pool, pool_scale, q_a_norm_g, w_q_b, kv_a_norm_g, w_kv_b, q_norm_g, k_norm_g, w_out, ffn_norm_g, w_gate, w_up, w_down, loss_target, m_meta_tokens, m_attn_norm_g, m_w_in, m_w_pool, m_pool_scale, m_q_a_norm_g, m_w_q_b, m_kv_a_norm_g, m_w_kv_b, m_q_norm_g, m_k_norm_g, m_w_out, m_ffn_norm_g, m_w_gate, m_w_up, m_w_down, v_meta_tokens, v_attn_norm_g, v_w_in, v_w_pool, v_pool_scale, v_q_a_norm_g, v_w_q_b, v_kv_a_norm_g, v_w_kv_b, v_q_norm_g, v_k_norm_g, v_w_out, v_ffn_norm_g, v_w_gate, v_w_up, v_w_down):
    weights = dict(meta_tokens=meta_tokens, attn_norm_g=attn_norm_g, w_in=w_in, w_pool=w_pool, pool_scale=pool_scale,
                   q_a_norm_g=q_a_norm_g, w_q_b=w_q_b, kv_a_norm_g=kv_a_norm_g, w_kv_b=w_kv_b, q_norm_g=q_norm_g,
                   k_norm_g=k_norm_g, w_out=w_out, ffn_norm_g=ffn_norm_g, w_gate=w_gate, w_up=w_up, w_down=w_down)
    mom1 = dict(meta_tokens=m_meta_tokens, attn_norm_g=m_attn_norm_g, w_in=m_w_in, w_pool=m_w_pool,
                pool_scale=m_pool_scale, q_a_norm_g=m_q_a_norm_g, w_q_b=m_w_q_b, kv_a_norm_g=m_kv_a_norm_g,
                w_kv_b=m_w_kv_b, q_norm_g=m_q_norm_g, k_norm_g=m_k_norm_g, w_out=m_w_out, ffn_norm_g=m_ffn_norm_g,
                w_gate=m_w_gate, w_up=m_w_up, w_down=m_w_down)
    mom2 = dict(meta_tokens=v_meta_tokens, attn_norm_g=v_attn_norm_g, w_in=v_w_in, w_pool=v_w_pool,
                pool_scale=v_pool_scale, q_a_norm_g=v_q_a_norm_g, w_q_b=v_w_q_b, kv_a_norm_g=v_kv_a_norm_g,
                w_kv_b=v_w_kv_b, q_norm_g=v_q_norm_g, k_norm_g=v_k_norm_g, w_out=v_w_out, ffn_norm_g=v_ffn_norm_g,
                w_gate=v_w_gate, w_up=v_w_up, w_down=v_w_down)
    order = ("meta_tokens", "attn_norm_g", "w_in", "w_pool", "pool_scale", "q_a_norm_g", "w_q_b", "kv_a_norm_g",
             "w_kv_b", "q_norm_g", "k_norm_g", "w_out", "ffn_norm_g", "w_gate", "w_up", "w_down")
    depth = w_in.shape[0]
    seq = x.shape[1]
    length = N_META + seq
    lp = -(-length // ROW_TILE) * ROW_TILE
    in_cols = w_in.shape[2]

    local = {n: (jnp.swapaxes(weights[n], 1, 2) if n in TRANSPOSED else weights[n]) for n in SHARDED}
    cast = cast_bf16([local[n] for n in SHARDED], "cast_weights")
    shards = [{n: cast[k * depth + l] for k, n in enumerate(SHARDED)} for l in range(depth)]
    gathered = all_gather([shards[0][n] for n in BEFORE_ATTENTION] + [meta_tokens], "all_gather")
    g8l = [dict(zip(BEFORE_ATTENTION, gathered[:-1]))] + [{} for _ in range(depth - 1)]
    meta_full = jnp.transpose(gathered[-1], (1, 0, 2)).reshape(N_META, D_MODEL)

    s1, s2, s3 = POOL_WIDTH, POOL_WIDTH + Q_LORA_RANK, POOL_WIDTH + Q_LORA_RANK + KV_LORA_RANK
    zpad = lambda n: jnp.zeros((1, n, D_MODEL), BF16)

    def padded_in_proj(w8):
        w_in_t = jnp.swapaxes(w8, 0, 1).reshape(1, N_DEV * in_cols, D_MODEL)
        return jnp.concatenate([w_in_t[:, :s3], zpad(QK_NOPE_DIM), w_in_t[:, s3:],
                                zpad(HEAD_PAD - QK_HEAD_DIM)], axis=1)

    w_in_ps = [None] * depth
    w_pool_b = w_pool.astype(BF16)

    cosf, sins = _rope_lane_tables(lp)
    seg = _segment_matrix()
    row = lambda a, l: a[l].reshape(1, -1)

    h = jnp.concatenate([meta_full, x[0], jnp.zeros((lp - length, D_MODEL), F32)], axis=0)
    target = jnp.pad(loss_target[0], ((N_META, lp - length), (0, 0)))
    saved = []
    for l in range(depth):
        gq = _pad_lanes(q_norm_g[l], 0)
        gkn = _pad_lanes(k_norm_g[l, :QK_NOPE_DIM], 0)
        gkr = _pad_lanes(k_norm_g[l, QK_NOPE_DIM:], QK_NOPE_DIM)
        g8 = g8l[l]
        w_in_ps[l] = padded_in_proj(g8["w_in"])
        a, u, c_q, c_kv, kr = norm_mm(
            h, row(attn_norm_g, l), w_in_ps[l], 0,
            [(0, s1), (s1, Q_LORA_RANK), (s2, KV_LORA_RANK), (s3, HEAD_PAD)], [F32] * 4, "in_proj")
        cat = pool_fwd(u, w_pool_b[l], row(pool_scale, l), "pool_fwd")
        qn, q = norm_mm_heads(c_q, row(q_a_norm_g, l), g8["w_q_b"], 0, "q_proj", transposed=True)
        kvn, kv = norm_mm_heads(c_kv, row(kv_a_norm_g, l), g8["w_kv_b"], 0, "kv_proj")
        qp, kp, v = qk_prep_fwd(q, kv, kr, cosf, sins, seg, gq, gkn, gkr, "qk_prep_fwd")
        riders = [(l, n) for n in SHARDED if n not in g8]
        riders += [(l + 1, n) for n in BEFORE_ATTENTION if l + 1 < depth]
        o, lse, cat, *arrived = attn_fwd(qp, kp, v, cat, "attn_fwd_gather", gather=[shards[k][n] for k, n in riders])
        for (k, n), w8 in zip(riders, arrived):
            g8l[k][n] = w8
        h_mid, g = mm_res(cat, g8["w_out"], h, 0, "out_proj", gamma=row(ffn_norm_g, l))
        gate, up, act = ffn_up(g, g8["w_gate"], g8["w_up"], 0, "ffn_up")
        h_next = mm_res(act, g8["w_down"], h_mid, 0, "ffn_down")[0]
        saved.append(dict(h=h, a=a, u=u, c_q=c_q, c_kv=c_kv, kr=kr, qn=qn, q=q, kvn=kvn, kv=kv, v=v, qp=qp, kp=kp,
                          o=o, lse=lse, cat=cat, h_mid=h_mid, g=g, gate=gate, up=up, act=act,
                          gq=gq, gkn=gkn, gkr=gkr))
        h = h_next

    dh, sq = loss_head(h, target, seq, "loss_head")
    loss = lax.psum(0.5 / D_MODEL * jnp.sum(sq), ("x", "y", "c"))

    core = lax.axis_index("c").astype(jnp.int32).reshape(1)
    small_grads = {n: [None] * depth for n in REPLICATED + ("w_pool",)}
    pending = []
    got = [{} for _ in range(depth)]

    def pair_reduce(l, names, slots, small):
        bigs = [slots[n] for n in names]
        if small is None:
            from_sibling, got_small = exchange_pair(bigs, None, "grad_exchange_pair"), None
        else:
            *from_sibling, got_small = exchange_pair(bigs, small, "grad_exchange_pair_small")
        partial = pair_sum(bigs, list(from_sibling), core, "grad_pair_sum")
        return [(l, n, p) for n, p in zip(names, partial)], got_small

    for l in reversed(range(depth)):
        s = saved[l]
        g8 = g8l[l]
        w_in_p = w_in_ps[l]
        slots = {}
        dgate, dup = ffn_bwd_act(dh, g8["w_down"], 0, s["gate"], s["up"], "ffn_bwd_act")
        slot_of = lambda n, full: full.reshape(g8[n].shape[:1] + g8[n].shape[2:])
        slots["w_down"] = slot_of("w_down", mm_tn([s["act"]], dh, "dw_down", out_dtype=BF16)[0])
        dw_gate_t, dw_up_t = mm_tn([dgate, dup], s["g"], "dw_gate_up", out_dtype=BF16)
        slots["w_gate"] = slot_of("w_gate", dw_gate_t)
        slots["w_up"] = slot_of("w_up", dw_up_t)
        dh_mid, dg_ffn = mm_nt_normbwd([("gathered", dgate, g8["w_gate"]), ("gathered", dup, g8["w_up"])],
                                       s["h_mid"], row(ffn_norm_g, l), dh, 0, "ffn_bwd_in", wide=False)
        small_grads["ffn_norm_g"][l] = dg_ffn[0]
        slots["w_out"] = slot_of("w_out", mm_tn([s["cat"]], dh_mid, "dw_out", out_dtype=BF16)[0])
        early = [slots[n] for n in BEFORE_ATTENTION_BWD]
        dy_pool, do, delta, *from_sibling = out_proj_bwd(dh_mid, g8["w_out"], 0, s["o"], "out_proj_bwd", pair=early)
        partial = pair_sum(early, from_sibling, core, "grad_pair_sum")
        pending += [(l, n, p) for n, p in zip(BEFORE_ATTENTION_BWD, partial)]
        du, dw_pool, dscale = pool_bwd(s["u"], dy_pool, w_pool_b[l], row(pool_scale, l), "pool_bwd")
        small_grads["w_pool"][l] = dw_pool
        small_grads["pool_scale"][l] = dscale[0]
        pool_rider = [jnp.stack(small_grads["w_pool"], axis=0).reshape(-1, LANES)] if l == 0 else []
        dqp, dkp, dv, *arrived = attn_bwd(s["qp"], s["kp"], s["v"], do, s["lse"], delta, "attn_bwd_exchange",
                                          exchange=[p for _, _, p in pending], gather=pool_rider)
        if pool_rider:
            got_pool = arrived.pop()
        for (k, n, _), parts in zip(pending, arrived):
            got[k][n] = parts
        dq, dkv, dkr, dgq, dgkn, dgkr = qk_prep_bwd(dqp, dkp, dv, s["q"], s["kv"], s["kr"], cosf, sins, seg,
                                                    s["gq"], s["gkn"], s["gkr"], "qk_prep_bwd")
        small_grads["q_norm_g"][l] = dgq[0, :QK_HEAD_DIM]
        small_grads["k_norm_g"][l] = jnp.concatenate([dgkn[0, :QK_NOPE_DIM], dgkr[0, QK_NOPE_DIM:QK_HEAD_DIM]])
        heads_first = lambda full: jnp.swapaxes(full.reshape(full.shape[0], MLA_HEADS, HEAD_PAD), 0, 1)
        dw_q_t = mm_tn([dq], s["qn"], "dw_q", out_dtype=BF16)[0]
        slots["w_q_b"] = dw_q_t.reshape(MLA_HEADS, HEAD_PAD, -1)[:, :QK_HEAD_DIM]
        dc_q, dg_qa = mm_nt_normbwd([("heads_t", dq, g8["w_q_b"])], s["c_q"], row(q_a_norm_g, l), None, 0,
                                    "q_proj_bwd")
        small_grads["q_a_norm_g"][l] = dg_qa[0]
        slots["w_kv_b"] = heads_first(mm_tn([s["kvn"]], dkv, "dw_kv", out_dtype=BF16)[0])
        dc_kv, dg_kva = mm_nt_normbwd([("heads", dkv, g8["w_kv_b"])], s["c_kv"], row(kv_a_norm_g, l), None, 0,
                                      "kv_proj_bwd")
        small_grads["kv_a_norm_g"][l] = dg_kva[0]
        dw_pool_t, dw_q_t, dw_kv_t, dw_rope_t = mm_tn([du, dc_q, dc_kv, dkr], s["a"], "dw_in", out_dtype=BF16)
        dw_in_t = jnp.concatenate([dw_pool_t, dw_q_t, dw_kv_t, dw_rope_t[QK_NOPE_DIM:QK_HEAD_DIM]], axis=0)
        slots["w_in"] = slot_of("w_in", dw_in_t)
        dh, dg_attn = mm_nt_normbwd(
            [("rows", du, w_in_p, 0), ("rows", dc_q, w_in_p, s1), ("rows", dc_kv, w_in_p, s2),
             ("rows", dkr, w_in_p, s3)],
            s["h"], row(attn_norm_g, l), dh_mid, 0, "in_proj_bwd")
        small_grads["attn_norm_g"][l] = dg_attn[0]

        small_slots = None
        if l == 0:
            rep_shapes = [weights[n].shape for n in REPLICATED]
            rep_count = sum(int(jnp.size(weights[n])) for n in REPLICATED)
            rep_rows = -(-rep_count // (8 * LANES)) * 8
            rep_packed = _pack_flat([jnp.stack(small_grads[n], axis=0) for n in REPLICATED], rep_rows)
            meta_slots = jnp.transpose(dh[:N_META].reshape(N_META, N_DEV, LANES), (1, 0, 2))
            small_slots = jnp.concatenate(
                [meta_slots, jnp.broadcast_to(rep_packed[None], (N_DEV, rep_rows, LANES))], axis=1)
        pending, got_small = pair_reduce(l, [n for n in SHARDED if n not in BEFORE_ATTENTION_BWD], slots,
                                         small_slots)
    for (k, n, _), parts in zip(pending, exchange_chips([p for _, _, p in pending], "grad_exchange_chips")):
        got[k][n] = parts

    grad_x = dh[N_META:length][None]

    per = [{} for _ in range(4)]
    for n in SHARDED:
        parts = [got[l][n] for l in range(depth)]
        if n in ADAM_TRANSPOSED:
            t = lambda a: jnp.swapaxes(a, 1, 2)
            outs = [t(o) for o in adamw_shard(parts, local[n], t(mom1[n]), t(mom2[n]), "adamw_" + n)]
        elif n in TRANSPOSED:
            grad = jnp.swapaxes(chip_sum(parts, "chip_sum_" + n), 1, 2)
            outs = adamw_shard(None, weights[n], mom1[n], mom2[n], "adamw_" + n, grad=grad)
        else:
            outs = adamw_shard(parts, weights[n], mom1[n], mom2[n], "adamw_" + n)
        for k in range(4):
            per[k][n] = outs[k]
    ps = lambda src: jnp.concatenate(
        [src["meta_tokens"], _pack_flat([src[n] for n in REPLICATED], rep_rows)], axis=0)
    small_out = adamw_packed(got_small, ps(weights), ps(mom1), ps(mom2), "adamw_small")
    as_rows = lambda a: a.reshape(-1, LANES)
    pool_out = adamw_packed(got_pool, as_rows(w_pool), as_rows(m_w_pool), as_rows(v_w_pool), "adamw_w_pool")
    for k in range(4):
        per[k]["meta_tokens"] = small_out[k][:N_META]
        per[k].update(zip(REPLICATED, _unpack_flat(small_out[k][N_META:], rep_shapes)))
        per[k]["w_pool"] = pool_out[k].reshape(w_pool.shape)
    return (loss, grad_x, *[per[0][n] for n in order], *[per[1][n] for n in order],
            *[per[2][n] for n in order], *[per[3][n] for n in order])
```

```python
import functools

import jax
import jax.numpy as jnp
from jax import lax
from jax.experimental import pallas as pl
from jax.experimental.pallas import tpu as pltpu

F32 = jnp.float32
BF16 = jnp.bfloat16

D_MODEL = 1024
N_META = 16
POOL_WIDTH = 512
POOL_WINDOWS = (2, 4, 8, 16)
POOL_GROUP_DIM = 128
POOL_HALO = 16
MLA_HEADS = 8
QK_NOPE_DIM = 64
QK_ROPE_DIM = 32
QK_HEAD_DIM = 96
V_HEAD_DIM = 64
HEAD_PAD = 128
Q_LORA_RANK = 384
KV_LORA_RANK = 256
ROPE_THETA = 10000.0
RMS_EPS = 1e-6
ATTN_SCALE = QK_HEAD_DIM ** -0.5
LOG2_E = 1.4426950408889634
SCORE_SCALE = ATTN_SCALE * LOG2_E

ADAM_LR = 0.001
ADAM_B1 = 0.9
ADAM_B2 = 0.999
ADAM_EPS = 1e-08
ADAM_WD = 0.01
ADAM_STEP = 10

N_DEV = 8
N_CHIPS = 4
LANES = 128
ROW_TILE = 384
LONG_TILE = 1056
WIDE_TILE = 1056
VMEM_LIMIT = 56 * 1024 * 1024


def _params(*sem):
    return pltpu.CompilerParams(dimension_semantics=sem, vmem_limit_bytes=VMEM_LIMIT)


def _wide_tile(rows):
    return WIDE_TILE if rows % WIDE_TILE == 0 else ROW_TILE


def _row_spec(tile, width):
    return pl.BlockSpec((tile, width), lambda i: (i, 0))


def _const_spec(shape):
    return pl.BlockSpec(shape, lambda i: tuple(0 for _ in shape))


def _layer_spec(w, layer):
    return pl.BlockSpec((None,) + w.shape[1:], lambda *_: (layer, 0, 0))


def _gathered_spec(w8, layer):
    return pl.BlockSpec((N_DEV, None) + w8.shape[2:], lambda *_: (0, layer, 0, 0))


def _nt(a, b):
    return lax.dot_general(a, b, (((1,), (1,)), ((), ())), preferred_element_type=F32)


def _tn(a, b):
    return lax.dot_general(a, b, (((0,), (0,)), ((), ())), preferred_element_type=F32)


def _nn(a, b):
    return jnp.dot(a, b, preferred_element_type=F32)


def _silu(g):
    return g * (1.0 / (1.0 + jnp.exp(-g)))


def _rms(xf):
    return lax.rsqrt(jnp.mean(xf * xf, axis=-1, keepdims=True) + RMS_EPS)


def norm_mm(x, gamma, wt, layer, splits, dtypes, name):
    L, K = x.shape
    tm = _wide_tile(L)

    def body(x_ref, g_ref, w_ref, a_ref, *z_refs):
        xf = x_ref[...]
        a = ((xf * _rms(xf)) * g_ref[...]).astype(BF16)
        a_ref[...] = a
        z = _nt(a, w_ref[...])
        for (s, n), zr in zip(splits, z_refs):
            zr[...] = z[:, s:s + n].astype(zr.dtype)

    widths = [n for _, n in splits]
    return pl.pallas_call(
        body,
        grid=(L // tm,),
        in_specs=[_row_spec(tm, K), _const_spec((1, K)), _layer_spec(wt, layer)],
        out_specs=[_row_spec(tm, K)] + [_row_spec(tm, n) for n in widths],
        out_shape=[jax.ShapeDtypeStruct((L, K), BF16)]
        + [jax.ShapeDtypeStruct((L, n), dt) for n, dt in zip(widths, dtypes)],
        compiler_params=_params("arbitrary"),
        name=name,
    )(x, gamma, wt)


def norm_mm_heads(x, gamma, w8, layer, name, transposed=False):
    L, K = x.shape
    hw = w8.shape[-2] if transposed else w8.shape[-1]
    tm = _wide_tile(L)

    def body(x_ref, g_ref, w_ref, a_ref, z_ref):
        xf = x_ref[...]
        a = ((xf * _rms(xf)) * g_ref[...]).astype(BF16)
        a_ref[...] = a
        if hw < HEAD_PAD:
            z_ref[...] = jnp.zeros_like(z_ref)
        for j in range(MLA_HEADS):
            z_ref[:, j * HEAD_PAD:j * HEAD_PAD + hw] = _nt(a, w_ref[j]) if transposed else _nn(a, w_ref[j])

    return pl.pallas_call(
        body,
        grid=(L // tm,),
        in_specs=[_row_spec(tm, K), _const_spec((1, K)), _gathered_spec(w8, layer)],
        out_specs=[_row_spec(tm, K), _row_spec(tm, MLA_HEADS * HEAD_PAD)],
        out_shape=[jax.ShapeDtypeStruct((L, K), BF16), jax.ShapeDtypeStruct((L, MLA_HEADS * HEAD_PAD), F32)],
        compiler_params=_params("arbitrary"),
        name=name,
    )(x, gamma, w8)


FF_GROUP = 4


def _ff_spec(w8, layer):
    return pl.BlockSpec((FF_GROUP, None) + w8.shape[2:], lambda j, i: (j, layer, 0, 0))


def ffn_up(g, w_gate8, w_up8, layer, name):
    L, K = g.shape
    fb = w_gate8.shape[-2]
    tm, tf = _wide_tile(L), FF_GROUP * fb

    def body(a_ref, wg_ref, wu_ref, gate_ref, up_ref, act_ref):
        a = a_ref[...]
        gate = _nt(a, wg_ref[...].reshape(tf, K))
        up = _nt(a, wu_ref[...].reshape(tf, K))
        gate_ref[...] = gate.astype(BF16)
        up_ref[...] = up.astype(BF16)
        act_ref[...] = (_silu(gate) * up).astype(BF16)

    tile = pl.BlockSpec((tm, tf), lambda j, i: (i, j))
    F = N_DEV * fb
    return pl.pallas_call(
        body,
        grid=(N_DEV // FF_GROUP, L // tm),
        in_specs=[pl.BlockSpec((tm, K), lambda j, i: (i, 0)), _ff_spec(w_gate8, layer), _ff_spec(w_up8, layer)],
        out_specs=[tile, tile, tile],
        out_shape=[
            jax.ShapeDtypeStruct((L, F), BF16),
            jax.ShapeDtypeStruct((L, F), BF16),
            jax.ShapeDtypeStruct((L, F), BF16),
        ],
        compiler_params=_params("arbitrary", "arbitrary"),
        name=name,
    )(g, w_gate8, w_up8)


def mm_res(a, w8, res, layer, name, gamma=None):
    L = a.shape[0]
    kb, N = w8.shape[-2:]
    tm = _wide_tile(L)
    normed = gamma is not None

    def body(a_ref, w_ref, r_ref, *rest):
        out = r_ref[...] + _nn(a_ref[...], w_ref[...].reshape(N_DEV * kb, N))
        if normed:
            g_ref, o_ref, n_ref = rest
            n_ref[...] = ((out * _rms(out)) * g_ref[...]).astype(BF16)
        else:
            (o_ref,) = rest
        o_ref[...] = out

    in_specs = [_row_spec(tm, N_DEV * kb), _gathered_spec(w8, layer), _row_spec(tm, N)]
    out_specs = [_row_spec(tm, N)]
    out_shape = [jax.ShapeDtypeStruct((L, N), F32)]
    args = [a, w8, res]
    if normed:
        in_specs.append(_const_spec((1, N)))
        out_specs.append(_row_spec(tm, N))
        out_shape.append(jax.ShapeDtypeStruct((L, N), BF16))
        args.append(gamma)
    return pl.pallas_call(
        body,
        grid=(L // tm,),
        in_specs=in_specs,
        out_specs=out_specs,
        out_shape=out_shape,
        compiler_params=_params("arbitrary"),
        name=name,
    )(*args)


def out_proj_bwd(dz, w8, layer, o, name, pair=()):
    L, N = dz.shape
    kb = w8.shape[-2]
    C = o.shape[1]
    tm = _wide_tile(L)
    n_p = len(pair)
    n_steps = L // tm

    def body(dz_ref, w_ref, o_ref, *rest):
        dy_ref, do_ref, dl_ref = rest[n_p:n_p + 3]
        if n_p:
            start, finish = _pair_steps(rest[:n_p], rest[n_p + 3:2 * n_p + 3], *rest[2 * n_p + 3:])
            pl.when(pl.program_id(0) == 0)(start)
        d = _nt(dz_ref[...].astype(BF16), w_ref[...].reshape(N_DEV * kb, N))
        dy_ref[...] = d[:, :C]
        do_ref[...] = d[:, C:]
        left = lax.broadcasted_iota(jnp.int32, (tm, LANES), 1) < V_HEAD_DIM
        for p in range(C // LANES):
            cols = slice(p * LANES, (p + 1) * LANES)
            prod = d[:, C + p * LANES:C + (p + 1) * LANES] * o_ref[:, cols]
            d0 = jnp.sum(jnp.where(left, prod, 0.0), axis=1, keepdims=True)
            d1 = jnp.sum(jnp.where(left, 0.0, prod), axis=1, keepdims=True)
            dl_ref[:, cols] = jnp.where(left, d0, d1)
        if n_p:
            pl.when(pl.program_id(0) == n_steps - 1)(finish)

    return pl.pallas_call(
        body,
        grid=(n_steps,),
        in_specs=[_row_spec(tm, N), _gathered_spec(w8, layer), _row_spec(tm, C)] + [_ANY] * n_p,
        out_specs=[_row_spec(tm, C)] * 3 + [_ANY] * n_p,
        out_shape=[jax.ShapeDtypeStruct((L, C), F32)] * 3
        + [jax.ShapeDtypeStruct((N_CHIPS,) + p.shape[1:], p.dtype) for p in pair],
        scratch_shapes=[pltpu.SemaphoreType.DMA((N_CHIPS * n_p,))] * 2 if n_p else [],
        compiler_params=_params("arbitrary"),
        name=name,
    )(dz, w8, o, *pair)


def mm_nt_normbwd(terms, x, gamma, dres, layer, name, wide=True):
    L, K = x.shape
    tm = _wide_tile(L) if wide else ROW_TILE
    n_terms = len(terms)
    has_res = dres is not None
    weights = []
    for t in terms:
        if not any(t[2] is u for u in weights):
            weights.append(t[2])
    which = [[t[2] is u for u in weights].index(True) for t in terms]
    n_in = n_terms + len(weights)

    def body(*refs):
        dz_refs = refs[:n_terms]
        w_refs = [refs[n_terms + n] for n in which]
        x_ref, g_ref = refs[n_in], refs[n_in + 1]
        pos = n_in + 2
        r_ref = refs[pos] if has_res else None
        dx_ref, dg_ref = refs[pos + has_res], refs[pos + has_res + 1]
        da = None
        for t, dz_ref, w_ref in zip(terms, dz_refs, w_refs):
            if t[0] == "rows":
                n, at = t[1].shape[1], t[3]
                parts = [_nn(dz_ref[...].astype(BF16), w_ref[at:at + n, :])]
            elif t[0] == "heads":
                hw = t[2].shape[-1]
                parts = [_nt(dz_ref[:, j * HEAD_PAD:j * HEAD_PAD + hw].astype(BF16), w_ref[j])
                         for j in range(MLA_HEADS)]
            elif t[0] == "heads_t":
                hw = t[2].shape[-2]
                parts = [_nn(dz_ref[:, j * HEAD_PAD:j * HEAD_PAD + hw].astype(BF16), w_ref[j])
                         for j in range(MLA_HEADS)]
            else:
                nb = t[2].shape[-2]
                parts = [_nn(dz_ref[...].astype(BF16), w_ref[...].reshape(N_DEV * nb, K))]
            for p in parts:
                da = p if da is None else da + p
        xf = x_ref[...]
        r = _rms(xf)
        xh = xf * r

        @pl.when(pl.program_id(0) == 0)
        def _():
            dg_ref[...] = jnp.zeros_like(dg_ref)

        dg_ref[...] += jnp.sum(da * xh, axis=0, keepdims=True)
        dxh = da * g_ref[...]
        dx = r * (dxh - xh * jnp.mean(dxh * xh, axis=-1, keepdims=True))
        if has_res:
            dx = dx + r_ref[...]
        dx_ref[...] = dx

    in_specs = [_row_spec(tm, t[1].shape[1]) for t in terms]
    for w in weights:
        in_specs.append(_layer_spec(w, layer) if w.ndim == 3 else _gathered_spec(w, layer))
    in_specs += [_row_spec(tm, K), _const_spec((1, K))]
    args = [t[1] for t in terms] + weights + [x, gamma]
    if has_res:
        in_specs.append(_row_spec(tm, K))
        args.append(dres)
    return pl.pallas_call(
        body,
        grid=(L // tm,),
        in_specs=in_specs,
        out_specs=[_row_spec(tm, K), _const_spec((1, K))],
        out_shape=[jax.ShapeDtypeStruct((L, K), F32), jax.ShapeDtypeStruct((1, K), F32)],
        compiler_params=_params("arbitrary"),
        name=name,
    )(*args)


MAX_OUT_ROWS = 1408


def mm_tn(a_list, b, name, out_dtype=F32):
    n = len(a_list)
    L, N = b.shape
    tks = [MAX_OUT_ROWS if (a.shape[1] > MAX_OUT_ROWS and a.shape[1] % MAX_OUT_ROWS == 0) else a.shape[1]
           for a in a_list]
    blocks = a_list[0].shape[1] // tks[0]
    assert all(a.shape[1] // tk == blocks for a, tk in zip(a_list, tks))
    tl = LONG_TILE if L % LONG_TILE == 0 else ROW_TILE
    n_l = L // tl

    def body(*refs):
        a_refs, b_ref = refs[:n], refs[n]
        o_refs, accs = refs[n + 1:2 * n + 1], refs[2 * n + 1:]
        l = pl.program_id(1)
        bt = b_ref[...].astype(BF16)
        for a_ref, o_ref, acc in zip(a_refs, o_refs, accs):
            @pl.when(l == 0)
            def _(acc=acc):
                acc[...] = jnp.zeros_like(acc)

            acc[...] += _tn(a_ref[...].astype(BF16), bt)

            @pl.when(l == n_l - 1)
            def _(acc=acc, o_ref=o_ref):
                o_ref[...] = acc[...].astype(o_ref.dtype)

    return pl.pallas_call(
        body,
        grid=(blocks, n_l),
        in_specs=[pl.BlockSpec((tl, tk), lambda j, l: (l, j)) for tk in tks]
        + [pl.BlockSpec((tl, N), lambda j, l: (l, 0))],
        out_specs=[pl.BlockSpec((tk, N), lambda j, l: (j, 0)) for tk in tks],
        out_shape=[jax.ShapeDtypeStruct((a.shape[1], N), out_dtype) for a in a_list],
        scratch_shapes=[pltpu.VMEM((tk, N), F32) for tk in tks],
        compiler_params=_params("arbitrary", "arbitrary"),
        name=name,
    )(*a_list, b)


def ffn_bwd_act(dh, w_down8, layer, gate, up, name):
    L, K = dh.shape
    fb = w_down8.shape[-2]
    tm, tf = _wide_tile(L), FF_GROUP * fb

    def body(dh_ref, w_ref, gate_ref, up_ref, dgate_ref, dup_ref):
        dact = _nt(dh_ref[...].astype(BF16), w_ref[...].reshape(tf, K))
        g = gate_ref[...].astype(F32)
        sig = 0.5 * jnp.tanh(0.5 * g) + 0.5
        dup_ref[...] = (dact * (g * sig)).astype(BF16)
        dgate_ref[...] = (dact * up_ref[...].astype(F32) * (sig * (1.0 + g * (1.0 - sig)))).astype(BF16)

    tile = pl.BlockSpec((tm, tf), lambda j, i: (i, j))
    F = N_DEV * fb
    return pl.pallas_call(
        body,
        grid=(N_DEV // FF_GROUP, L // tm),
        in_specs=[pl.BlockSpec((tm, K), lambda j, i: (i, 0)), _ff_spec(w_down8, layer), tile, tile],
        out_specs=[tile, tile],
        out_shape=[jax.ShapeDtypeStruct((L, F), BF16), jax.ShapeDtypeStruct((L, F), BF16)],
        compiler_params=_params("arbitrary", "arbitrary"),
        name=name,
    )(dh, w_down8, gate, up)


def _pool_residual(scr, lo, tm, g, w, t):
    cols = slice(g * POOL_GROUP_DIM, (g + 1) * POOL_GROUP_DIM)
    cur = scr[lo:lo + tm, cols]
    s = cur
    for k in range(1, w):
        s = s + scr[lo - k:lo - k + tm, cols]
    cnt = jnp.minimum(t + 1, w).astype(F32)
    return s / cnt - cur


def pool_fwd(u, w_pool, scale, name):
    L, C = u.shape
    tm, halo = _wide_tile(L), POOL_HALO

    def body(u_ref, halo_ref, w_ref, s_ref, y_ref, scr):
        i = pl.program_id(0)
        scr[0:halo, :] = jnp.where(i > 0, halo_ref[...], 0.0)
        scr[halo:halo + tm, :] = u_ref[...]
        t = i * tm + lax.broadcasted_iota(jnp.int32, (tm, POOL_GROUP_DIM), 0)
        for g, w in enumerate(POOL_WINDOWS):
            cols = slice(g * POOL_GROUP_DIM, (g + 1) * POOL_GROUP_DIM)
            p = _pool_residual(scr, halo, tm, g, w, t)
            y = _nn(p.astype(BF16), w_ref[g]) * s_ref[:, cols]
            y_ref[:, cols] = y.astype(y_ref.dtype)

    return pl.pallas_call(
        body,
        grid=(L // tm,),
        in_specs=[
            _row_spec(tm, C),
            pl.BlockSpec((halo, C), lambda i: (jnp.maximum(i * (tm // halo) - 1, 0), 0)),
            _const_spec(w_pool.shape),
            _const_spec((1, C)),
        ],
        out_specs=_row_spec(tm, C),
        out_shape=jax.ShapeDtypeStruct((L, 2 * C), BF16),
        scratch_shapes=[pltpu.VMEM((tm + halo, C), F32)],
        compiler_params=_params("arbitrary"),
        name=name,
    )(u, u, w_pool, scale)


def pool_bwd(u, dy, w_pool, scale, name):
    L, C = u.shape
    tm, halo = _wide_tile(L), POOL_HALO
    n_tiles = L // tm
    last_halo = L // halo - 1

    def body(u_ref, uh_ref, dy_ref, dyh_ref, w_ref, s_ref, du_ref, dw_ref, ds_ref, scr_u, scr_q):
        i = pl.program_id(0)

        @pl.when(i == 0)
        def _():
            dw_ref[...] = jnp.zeros_like(dw_ref)
            ds_ref[...] = jnp.zeros_like(ds_ref)

        scr_u[0:halo, :] = jnp.where(i > 0, uh_ref[...], 0.0)
        scr_u[halo:halo + tm, :] = u_ref[...]
        t = i * tm + lax.broadcasted_iota(jnp.int32, (tm, POOL_GROUP_DIM), 0)
        th = (i + 1) * tm + lax.broadcasted_iota(jnp.int32, (halo, POOL_GROUP_DIM), 0)
        for g, w in enumerate(POOL_WINDOWS):
            cols = slice(g * POOL_GROUP_DIM, (g + 1) * POOL_GROUP_DIM)
            p = _pool_residual(scr_u, halo, tm, g, w, t).astype(BF16)
            wg = w_ref[g]
            sc = s_ref[:, cols]
            dy = dy_ref[:, cols]
            ds_ref[:, cols] += jnp.sum(dy * _nn(p, wg), axis=0, keepdims=True)
            dys = (dy * sc).astype(BF16)
            dw_ref[g] += _tn(p, dys)
            dp = _nt(dys, wg)
            dyh = jnp.where(i < n_tiles - 1, dyh_ref[:, cols], 0.0)
            dph = _nt((dyh * sc).astype(BF16), wg)
            scr_q[0:tm, cols] = dp / jnp.minimum(t + 1, w).astype(F32)
            scr_q[tm:tm + halo, cols] = dph / jnp.minimum(th + 1, w).astype(F32)
            acc = scr_q[0:tm, cols]
            for k in range(1, w):
                acc = acc + scr_q[k:k + tm, cols]
            du_ref[:, cols] = acc - dp

    return pl.pallas_call(
        body,
        grid=(n_tiles,),
        in_specs=[
            _row_spec(tm, C),
            pl.BlockSpec((halo, C), lambda i: (jnp.maximum(i * (tm // halo) - 1, 0), 0)),
            _row_spec(tm, C),
            pl.BlockSpec((halo, C), lambda i: (jnp.minimum((i + 1) * (tm // halo), last_halo), 0)),
            _const_spec(w_pool.shape),
            _const_spec((1, C)),
        ],
        out_specs=[_row_spec(tm, C), _const_spec(w_pool.shape), _const_spec((1, C))],
        out_shape=[
            jax.ShapeDtypeStruct((L, C), F32),
            jax.ShapeDtypeStruct(w_pool.shape, F32),
            jax.ShapeDtypeStruct((1, C), F32),
        ],
        scratch_shapes=[pltpu.VMEM((tm + halo, C), F32), pltpu.VMEM((tm + halo, C), F32)],
        compiler_params=_params("arbitrary"),
        name=name,
    )(u, u, dy, dy, w_pool, scale)


def _head_masks(rows):
    lane = lax.broadcasted_iota(jnp.int32, (rows, HEAD_PAD), 1)
    return lane, lane < QK_NOPE_DIM, (lane >= QK_NOPE_DIM) & (lane < QK_HEAD_DIM)


def _rope_swap(x, lane):
    half = QK_ROPE_DIM // 2
    swapped = jnp.where(lane < QK_NOPE_DIM + half, pltpu.roll(x, HEAD_PAD - half, 1), pltpu.roll(x, half, 1))
    return jnp.where((lane >= QK_NOPE_DIM) & (lane < QK_HEAD_DIM), swapped, 0.0)


def _seg_mean(v, seg_ref):
    hi = v.astype(BF16)
    lo = (v - hi.astype(F32)).astype(BF16)
    seg = seg_ref[...]
    return _nn(hi, seg) + _nn(lo, seg)


def _segment_matrix():
    lane = jnp.arange(HEAD_PAD)
    seg = jnp.where(lane < QK_NOPE_DIM, 0, jnp.where(lane < QK_HEAD_DIM, 1, 2))
    inv = jnp.where(lane < QK_NOPE_DIM, 1.0 / QK_NOPE_DIM, jnp.where(lane < QK_HEAD_DIM, 1.0 / QK_ROPE_DIM, 0.0))
    return jnp.where(seg[:, None] == seg[None, :], inv[None, :], 0.0).astype(BF16)


def qk_prep_fwd(q, kv, kr, cosf, sins, seg, gq, gkn, gkr, name):
    L = q.shape[0]
    tm = _wide_tile(L)
    W = MLA_HEADS * HEAD_PAD

    def body(q_ref, kv_ref, kr_ref, c_ref, s_ref, seg_ref, gq_ref, gkn_ref, gkr_ref, qo_ref, ko_ref, vo_ref):
        lane, m_n, _ = _head_masks(tm)
        cosf_, sins_ = c_ref[...], s_ref[...]
        kr_ = kr_ref[...]
        rk = lax.rsqrt(_seg_mean(kr_ * kr_, seg_ref) + RMS_EPS)
        krn = kr_ * rk * gkr_ref[...]
        krf = krn * cosf_ + _rope_swap(krn, lane) * sins_
        for h in range(MLA_HEADS):
            cols = slice(h * HEAD_PAD, (h + 1) * HEAD_PAD)
            qh = q_ref[:, cols]
            qn = qh * lax.rsqrt(_seg_mean(qh * qh, seg_ref) + RMS_EPS) * gq_ref[...]
            qo_ref[:, cols] = (qn * cosf_ + _rope_swap(qn, lane) * sins_).astype(BF16)
            kh = jnp.where(m_n, kv_ref[:, cols], 0.0)
            rn = lax.rsqrt(_seg_mean(kh * kh, seg_ref) + RMS_EPS)
            ko_ref[:, cols] = (kh * rn * gkn_ref[...] + krf).astype(BF16)
        for p in range(MLA_HEADS // 2):
            even = kv_ref[:, 2 * p * HEAD_PAD:(2 * p + 1) * HEAD_PAD]
            odd = kv_ref[:, (2 * p + 1) * HEAD_PAD:(2 * p + 2) * HEAD_PAD]
            pair = jnp.where(m_n, pltpu.roll(even, V_HEAD_DIM, 1), odd)
            vo_ref[:, p * LANES:(p + 1) * LANES] = pair.astype(BF16)

    vec = _const_spec((1, HEAD_PAD))
    return pl.pallas_call(
        body,
        grid=(L // tm,),
        in_specs=[_row_spec(tm, W), _row_spec(tm, W), _row_spec(tm, HEAD_PAD), _row_spec(tm, HEAD_PAD),
                  _row_spec(tm, HEAD_PAD), _const_spec((HEAD_PAD, HEAD_PAD)), vec, vec, vec],
        out_specs=[_row_spec(tm, W), _row_spec(tm, W), _row_spec(tm, W // 2)],
        out_shape=[jax.ShapeDtypeStruct((L, W), BF16), jax.ShapeDtypeStruct((L, W), BF16),
                   jax.ShapeDtypeStruct((L, W // 2), BF16)],
        compiler_params=_params("arbitrary"),
        name=name,
    )(q, kv, kr, cosf, sins, seg, gq, gkn, gkr)


def qk_prep_bwd(dqo, dko, dv, q, kv, kr, cosf, sins, seg, gq, gkn, gkr, name):
    L = q.shape[0]
    tm = ROW_TILE
    W = MLA_HEADS * HEAD_PAD

    def body(dqo_ref, dko_ref, dv_ref, q_ref, kv_ref, kr_ref, c_ref, s_ref, seg_ref, gq_ref, gkn_ref, gkr_ref,
             dq_ref, dkv_ref, dkr_ref, dgq_ref, dgkn_ref, dgkr_ref):
        @pl.when(pl.program_id(0) == 0)
        def _():
            dgq_ref[...] = jnp.zeros_like(dgq_ref)
            dgkn_ref[...] = jnp.zeros_like(dgkn_ref)
            dgkr_ref[...] = jnp.zeros_like(dgkr_ref)

        lane, m_n, _ = _head_masks(tm)
        cosf_, sins_ = c_ref[...], s_ref[...]
        dgq = jnp.zeros((1, HEAD_PAD), F32)
        dgkn = jnp.zeros((1, HEAD_PAD), F32)
        dkrf = jnp.zeros((tm, HEAD_PAD), F32)
        for h in range(MLA_HEADS):
            cols = slice(h * HEAD_PAD, (h + 1) * HEAD_PAD)
            dy = dqo_ref[:, cols]
            dqn = dy * cosf_ + _rope_swap(dy * sins_, lane)
            qh = q_ref[:, cols]
            rinv = lax.rsqrt(_seg_mean(qh * qh, seg_ref) + RMS_EPS)
            xh = qh * rinv
            dgq = dgq + jnp.sum(dqn * xh, axis=0, keepdims=True)
            dxh = dqn * gq_ref[...]
            dq_ref[:, cols] = rinv * (dxh - xh * _seg_mean(dxh * xh, seg_ref))
            dk = dko_ref[:, cols]
            dkrf = dkrf + dk
            kh = jnp.where(m_n, kv_ref[:, cols], 0.0)
            rn = lax.rsqrt(_seg_mean(kh * kh, seg_ref) + RMS_EPS)
            xk = kh * rn
            dgkn = dgkn + jnp.sum(dk * xk, axis=0, keepdims=True)
            dxk = dk * gkn_ref[...]
            dkn = rn * (dxk - xk * _seg_mean(dxk * xk, seg_ref))
            dvp = dv_ref[:, (h // 2) * LANES:(h // 2 + 1) * LANES]
            dvh = pltpu.roll(dvp, V_HEAD_DIM, 1) if h % 2 == 0 else dvp
            dkv_ref[:, cols] = jnp.where(m_n, dkn, dvh)
        kr_ = kr_ref[...]
        rk = lax.rsqrt(_seg_mean(kr_ * kr_, seg_ref) + RMS_EPS)
        xr = kr_ * rk
        dkrn = dkrf * cosf_ + _rope_swap(dkrf * sins_, lane)
        dgkr_ref[...] += jnp.sum(dkrn * xr, axis=0, keepdims=True)
        dxr = dkrn * gkr_ref[...]
        dkr_ref[...] = rk * (dxr - xr * _seg_mean(dxr * xr, seg_ref))
        dgq_ref[...] += dgq
        dgkn_ref[...] += dgkn

    vec = _const_spec((1, HEAD_PAD))
    return pl.pallas_call(
        body,
        grid=(L // tm,),
        in_specs=[_row_spec(tm, W), _row_spec(tm, W), _row_spec(tm, W // 2), _row_spec(tm, W), _row_spec(tm, W),
                  _row_spec(tm, HEAD_PAD), _row_spec(tm, HEAD_PAD), _row_spec(tm, HEAD_PAD),
                  _const_spec((HEAD_PAD, HEAD_PAD)), vec, vec, vec],
        out_specs=[_row_spec(tm, W), _row_spec(tm, W), _row_spec(tm, HEAD_PAD), vec, vec, vec],
        out_shape=[jax.ShapeDtypeStruct((L, W), F32), jax.ShapeDtypeStruct((L, W), F32),
                   jax.ShapeDtypeStruct((L, HEAD_PAD), F32)] + [jax.ShapeDtypeStruct((1, HEAD_PAD), F32)] * 3,
        compiler_params=_params("arbitrary"),
        name=name,
    )(dqo, dko, dv, q, kv, kr, cosf, sins, seg, gq, gkn, gkr)


def attn_fwd(qp, kp, v, cat, name, gather=()):
    L = qp.shape[0]
    tq = ROW_TILE
    n_i = L // tq
    pair_w = 2 * HEAD_PAD
    n_pairs = MLA_HEADS // 2
    n_g = len(gather)

    def body(q_ref, k_ref, v_ref, cat_in, *rest):
        del cat_in
        o_ref, lse_ref, cat_ref = rest[n_g:n_g + 3]
        i = pl.program_id(1)
        if n_g:
            pair = pl.program_id(0)
            start, forward, finish = _gather_steps(rest[:n_g], rest[n_g + 3:2 * n_g + 3], *rest[2 * n_g + 3:])
            pl.when((pair == 0) & (i == 0))(start)
            pl.when((pair == n_pairs // 2) & (i == 0))(forward)
        left = lax.broadcasted_iota(jnp.int32, (tq, LANES), 1) < V_HEAD_DIM

        def step(t, carry, masked, width=1):
            rows = pl.ds(pl.multiple_of(t * (tq * width), tq), tq * width)
            vv = v_ref[rows, :]
            out = []
            for hh in range(2):
                cols = slice(hh * HEAD_PAD, (hh + 1) * HEAD_PAD)
                m, l, acc = carry[hh]
                s = _nt(q_ref[:, cols], k_ref[rows, cols]) * SCORE_SCALE
                if masked:
                    row = lax.broadcasted_iota(jnp.int32, s.shape, 0)
                    col = lax.broadcasted_iota(jnp.int32, s.shape, 1)
                    s = jnp.where(col <= row, s, -jnp.inf)
                m_new = jnp.maximum(m, jnp.max(s, axis=1, keepdims=True))
                alpha = jnp.exp2(m - m_new)
                p = jnp.exp2(s - m_new)
                l = alpha * l + jnp.sum(p, axis=1, keepdims=True)
                acc = alpha * acc + _nn(p.astype(BF16), vv)
                out.append((m_new, l, acc))
            return tuple(out)

        one = (jnp.full((tq, 1), -jnp.inf, F32), jnp.zeros((tq, 1), F32), jnp.zeros((tq, LANES), F32))
        doubles = jnp.right_shift(i, 1)
        carry = lax.fori_loop(0, doubles, functools.partial(step, masked=False, width=2), (one, one))
        carry = lax.fori_loop(2 * doubles, i, functools.partial(step, masked=False), carry)
        (m0, l0, a0), (m1, l1, a1) = step(i, carry, True)
        o = jnp.where(left, a0 / l0, a1 / l1)
        o_ref[...] = o
        cat_ref[...] = o.astype(BF16)
        lse_ref[...] = jnp.where(left, m0 + jnp.log2(l0), m1 + jnp.log2(l1))
        if n_g:
            pl.when((pair == n_pairs - 1) & (i == n_i - 1))(finish)

    return pl.pallas_call(
        body,
        grid=(n_pairs, n_i),
        in_specs=[
            pl.BlockSpec((tq, pair_w), lambda p, i: (i, p)),
            pl.BlockSpec((L, pair_w), lambda p, i: (0, p)),
            pl.BlockSpec((L, LANES), lambda p, i: (0, p)),
            _ANY,
        ] + [_ANY] * n_g,
        out_specs=[
            pl.BlockSpec((tq, LANES), lambda p, i: (i, p)),
            pl.BlockSpec((tq, LANES), lambda p, i: (i, p)),
            pl.BlockSpec((tq, LANES), lambda p, i: (i, n_pairs + p)),
        ] + [_ANY] * n_g,
        out_shape=[jax.ShapeDtypeStruct((L, n_pairs * LANES), F32), jax.ShapeDtypeStruct((L, n_pairs * LANES), F32),
                   jax.ShapeDtypeStruct(cat.shape, cat.dtype)]
        + [jax.ShapeDtypeStruct((N_DEV,) + g.shape, g.dtype) for g in gather],
        scratch_shapes=_gather_scratch(n_g) if n_g else [],
        input_output_aliases={3: 2},
        compiler_params=_params("arbitrary", "arbitrary"),
        name=name,
    )(qp, kp, v, cat, *gather)


def attn_bwd(qp, kp, v, do, lse, delta, name, exchange=(), gather=()):
    L = qp.shape[0]
    tq = ROW_TILE
    n_q = L // tq
    pair_w = 2 * HEAD_PAD
    n_pairs = MLA_HEADS // 2
    n_x, n_g = len(exchange), len(gather)
    n_r = n_x + n_g

    def body(q_ref, k_ref, v_ref, do_ref, lse_ref, dl_ref, *rest):
        dq_ref, dk_ref, dv_ref = rest[n_r:n_r + 3]
        riders_in, riders_out, sems = rest[:n_r], rest[n_r + 3:2 * n_r + 3], rest[2 * n_r + 3:]
        j = pl.program_id(1)
        pair = pl.program_id(0)
        steps = []
        if n_x:
            steps.append(_chip_exchange_steps(riders_in[:n_x], riders_out[:n_x], *sems[:3]))
        if n_g:
            steps.append(_direct_gather_steps(riders_in[n_x:], riders_out[n_x:], *sems[-3:]))
        for start, _ in steps:
            pl.when((pair == 0) & (j == 0))(start)

        @pl.when(j == 0)
        def _():
            dq_ref[...] = jnp.zeros_like(dq_ref)

        dk_ref[...] = jnp.zeros_like(dk_ref)
        dv_ref[...] = jnp.zeros_like(dv_ref)

        def step(t, carry, masked, width=1, first=0):
            rows = pl.ds(pl.multiple_of((first + t * width) * tq, tq), tq * width)
            do = do_ref[rows, :]
            left = lax.broadcasted_iota(jnp.int32, do.shape, 1) < V_HEAD_DIM
            vv = v_ref[...]
            dv = None
            for hh in range(2):
                cols = slice(hh * HEAD_PAD, (hh + 1) * HEAD_PAD)
                stat = slice(hh * V_HEAD_DIM, hh * V_HEAD_DIM + 1)
                q = q_ref[rows, cols]
                k = k_ref[:, cols]
                dom = jnp.where(left if hh == 0 else jnp.logical_not(left), do, 0.0).astype(BF16)
                s = _nt(q, k) * SCORE_SCALE
                p = jnp.exp2(s - lse_ref[rows, stat])
                if masked:
                    row = lax.broadcasted_iota(jnp.int32, p.shape, 0)
                    col = lax.broadcasted_iota(jnp.int32, p.shape, 1)
                    p = jnp.where(col <= row, p, 0.0)
                dp = _nt(dom, vv)
                ds = (p * (dp - dl_ref[rows, stat]) * ATTN_SCALE).astype(BF16)
                dq_ref[rows, cols] += _nn(ds, k)
                dk_ref[:, cols] += _tn(ds, q)
                t = _tn(p.astype(BF16), dom)
                dv = t if dv is None else dv + t
            dv_ref[...] += dv
            return carry

        @pl.when(j < n_q - 1)
        def _():
            step(0, 0, True, width=2, first=j)
            doubles = jnp.right_shift(n_q - 2 - j, 1)
            lax.fori_loop(0, doubles, functools.partial(step, masked=False, width=2, first=j + 2), 0)
            lax.fori_loop(j + 2 + 2 * doubles, n_q, functools.partial(step, masked=False), 0)

        @pl.when(j == n_q - 1)
        def _():
            step(j, 0, True)

        for _, finish in steps:
            pl.when((pair == n_pairs - 1) & (j == n_q - 1))(finish)

    return pl.pallas_call(
        body,
        grid=(n_pairs, n_q),
        in_specs=[
            pl.BlockSpec((L, pair_w), lambda p, j: (0, p)),
            pl.BlockSpec((tq, pair_w), lambda p, j: (j, p)),
            pl.BlockSpec((tq, LANES), lambda p, j: (j, p)),
            pl.BlockSpec((L, LANES), lambda p, j: (0, p)),
            pl.BlockSpec((L, LANES), lambda p, j: (0, p)),
            pl.BlockSpec((L, LANES), lambda p, j: (0, p)),
        ] + [_ANY] * n_r,
        out_specs=[
            pl.BlockSpec((L, pair_w), lambda p, j: (0, p)),
            pl.BlockSpec((tq, pair_w), lambda p, j: (j, p)),
            pl.BlockSpec((tq, LANES), lambda p, j: (j, p)),
        ] + [_ANY] * n_r,
        out_shape=[
            jax.ShapeDtypeStruct((L, n_pairs * pair_w), F32),
            jax.ShapeDtypeStruct((L, n_pairs * pair_w), F32),
            jax.ShapeDtypeStruct((L, n_pairs * LANES), F32),
        ] + [jax.ShapeDtypeStruct(e.shape, e.dtype) for e in exchange]
        + [jax.ShapeDtypeStruct((N_DEV,) + g.shape, g.dtype) for g in gather],
        scratch_shapes=(_chip_exchange_scratch(n_x) if n_x else []) + (_direct_gather_scratch(n_g) if n_g else []),
        compiler_params=_params("arbitrary", "arbitrary"),
        name=name,
    )(qp, kp, v, do, lse, delta, *exchange, *gather)


def loss_head(h, target, n_real, name):
    L, D = h.shape
    tm = _wide_tile(L)

    def body(h_ref, t_ref, dh_ref, sq_ref):
        i = pl.program_id(0)

        @pl.when(i == 0)
        def _():
            sq_ref[...] = jnp.zeros_like(sq_ref)

        t = i * tm + lax.broadcasted_iota(jnp.int32, (tm, D), 0)
        real = (t >= N_META) & (t < N_META + n_real)
        diff = jnp.where(real, h_ref[...] - t_ref[...], 0.0)
        dh_ref[...] = diff * (1.0 / D)
        sq_ref[...] += jnp.sum(diff * diff, axis=0, keepdims=True)

    return pl.pallas_call(
        body,
        grid=(L // tm,),
        in_specs=[_row_spec(tm, D), _row_spec(tm, D)],
        out_specs=[_row_spec(tm, D), _const_spec((1, D))],
        out_shape=[jax.ShapeDtypeStruct((L, D), F32), jax.ShapeDtypeStruct((1, D), F32)],
        compiler_params=_params("arbitrary"),
        name=name,
    )(h, target)


def _mesh_position():
    x, y, c = lax.axis_index("x"), lax.axis_index("y"), lax.axis_index("c")
    return x, y, c, 4 * x + 2 * y + c


def _flip(x, y, c, k):
    px = 1 - x if k & 4 else x
    py = 1 - y if k & 2 else y
    pc = 1 - c if k & 1 else c
    return (px, py, pc), 4 * px + 2 * py + pc


def _other_chips(x, y):
    return [(1 - x, y), (x, 1 - y), (1 - x, 1 - y)]


def _dev_index(px, py, pc):
    return 4 * px + 2 * py + pc


_ANY = pl.BlockSpec(memory_space=pl.ANY)


def cast_bf16(arrays, name):
    n = len(arrays)
    depth = arrays[0].shape[0]

    def body(*refs):
        for k, src in enumerate(refs[:n]):
            for l in range(depth):
                refs[n + k * depth + l][0] = src[l].astype(BF16)

    return pl.pallas_call(
        body,
        out_shape=[jax.ShapeDtypeStruct((1,) + a.shape[1:], BF16) for a in arrays for _ in range(depth)],
        compiler_params=pltpu.CompilerParams(vmem_limit_bytes=VMEM_LIMIT),
        name=name,
    )(*arrays)


def _gather_steps(srcs, dsts, send_sems, recv_sems, local_sems):
    n = len(srcs)
    x, y, c, me = _mesh_position()
    sibling = (x, y, 1 - c)
    chips = _other_chips(x, y)

    def copy(k, b, block, to, from_src=False):
        dst = dsts[b].at[block]
        return pltpu.make_async_remote_copy(
            src_ref=srcs[b] if from_src else dst, dst_ref=dst, send_sem=send_sems.at[k * n + b],
            recv_sem=recv_sems.at[k * n + b], device_id=to, device_id_type=pl.DeviceIdType.MESH)

    def own():
        return [pltpu.make_async_copy(srcs[b], dsts[b].at[me], local_sems.at[b]) for b in range(n)]

    def first():
        out = []
        for b in range(n):
            out.append(copy(0, b, me, sibling, from_src=True))
            out += [copy(1 + q, b, me, (*chip, c), from_src=True) for q, chip in enumerate(chips)]
        return out

    def passed():
        return [copy(4 + q, b, _dev_index(*chip, c), sibling) for q, chip in enumerate(chips) for b in range(n)]

    def start():
        for cp in own() + first():
            cp.start()

    def forward():
        for q, chip in enumerate(chips):
            for b in range(n):
                copy(1 + q, b, _dev_index(*chip, c), sibling).wait_recv()
        for cp in passed():
            cp.start()

    def finish():
        for b in range(n):
            copy(0, b, _dev_index(x, y, 1 - c), sibling).wait_recv()
            for q, chip in enumerate(chips):
                copy(4 + q, b, _dev_index(*chip, 1 - c), sibling).wait_recv()
        for cp in first() + passed():
            cp.wait_send()
        for cp in own():
            cp.wait()

    return start, forward, finish


def _gather_scratch(n):
    copies = N_DEV - 1
    return [pltpu.SemaphoreType.DMA((copies * n,)), pltpu.SemaphoreType.DMA((copies * n,)),
            pltpu.SemaphoreType.DMA((n,))]


def all_gather(payloads, name):
    n = len(payloads)

    def body(*refs):
        start, forward, finish = _gather_steps(refs[:n], refs[n:2 * n], *refs[2 * n:])
        start()
        forward()
        finish()

    return pl.pallas_call(
        body,
        in_specs=[_ANY] * n,
        out_specs=[_ANY] * n,
        out_shape=[jax.ShapeDtypeStruct((N_DEV,) + p.shape, p.dtype) for p in payloads],
        scratch_shapes=_gather_scratch(n),
        name=name,
    )(*payloads)


def _pair_steps(big_refs, sib_refs, send_sems, recv_sems):
    n = len(big_refs)
    x, y, c, me = _mesh_position()

    def copies():
        return [pltpu.make_async_remote_copy(
            src_ref=big_refs[b].at[_dev_index(q // 2, q % 2, 1 - c)], dst_ref=sib_refs[b].at[q],
            send_sem=send_sems.at[b * N_CHIPS + q], recv_sem=recv_sems.at[b * N_CHIPS + q],
            device_id=(x, y, 1 - c), device_id_type=pl.DeviceIdType.MESH) for b in range(n) for q in range(N_CHIPS)]

    def start():
        for cp in copies():
            cp.start()

    def finish():
        for cp in copies():
            cp.wait_recv()
        for cp in copies():
            cp.wait_send()

    return start, finish


def exchange_pair(bigs, small, name):
    n = len(bigs)
    has_small = small is not None
    n_big = N_CHIPS * n
    n_sems = n_big + (N_DEV - 1 if has_small else 0)

    def body(*refs):
        big_refs = refs[:n]
        pos = n + has_small
        sib_refs = refs[pos:pos + n]
        send_sems, recv_sems, local_sem = refs[-3:]
        x, y, c, me = _mesh_position()
        sibling = (x, y, 1 - c)
        copies = []
        for b in range(n):
            for q in range(N_CHIPS):
                copies.append(pltpu.make_async_remote_copy(
                    src_ref=big_refs[b].at[_dev_index(q // 2, q % 2, 1 - c)], dst_ref=sib_refs[b].at[q],
                    send_sem=send_sems.at[b * N_CHIPS + q], recv_sem=recv_sems.at[b * N_CHIPS + q],
                    device_id=sibling, device_id_type=pl.DeviceIdType.MESH))
        waits = list(copies)
        if has_small:
            small_ref, gsmall_ref = refs[n], refs[pos + n]
            own = pltpu.make_async_copy(small_ref.at[me], gsmall_ref.at[me], local_sem.at[0])
            own.start()
            for k in range(1, N_DEV):
                peer, peer_idx = _flip(x, y, c, k)
                s = n_big + k - 1
                copies.append(pltpu.make_async_remote_copy(
                    src_ref=small_ref.at[peer_idx], dst_ref=gsmall_ref.at[me], send_sem=send_sems.at[s],
                    recv_sem=recv_sems.at[s], device_id=peer, device_id_type=pl.DeviceIdType.MESH))
                waits.append(pltpu.make_async_remote_copy(
                    src_ref=small_ref.at[peer_idx], dst_ref=gsmall_ref.at[peer_idx], send_sem=send_sems.at[s],
                    recv_sem=recv_sems.at[s], device_id=peer, device_id_type=pl.DeviceIdType.MESH))
        for cp in copies:
            cp.start()
        for cp in waits:
            cp.wait_recv()
        for cp in copies:
            cp.wait_send()
        if has_small:
            own.wait()

    out_shape = [jax.ShapeDtypeStruct((N_CHIPS,) + b.shape[1:], b.dtype) for b in bigs]
    args = list(bigs)
    if has_small:
        out_shape.append(jax.ShapeDtypeStruct(small.shape, small.dtype))
        args.append(small)
    return pl.pallas_call(
        body,
        in_specs=[_ANY] * len(args),
        out_specs=[_ANY] * len(out_shape),
        out_shape=out_shape,
        scratch_shapes=[pltpu.SemaphoreType.DMA((n_sems,)), pltpu.SemaphoreType.DMA((n_sems,)),
                        pltpu.SemaphoreType.DMA((1,))],
        name=name,
    )(*args)


def pair_sum(bigs, from_sibling, core, name):
    n = len(bigs)

    def body(core_ref, *refs):
        del core_ref
        for mine, sib, out in zip(refs[:n], refs[n:2 * n], refs[2 * n:]):
            out[...] = (mine[...].astype(F32) + sib[...].astype(F32)).astype(out.dtype)

    def slot(shape, picked):
        zeros = (0,) * (len(shape) - 1)
        if picked:
            return pl.BlockSpec((None,) + shape[1:], lambda q, core_ref: (2 * q + core_ref[0],) + zeros)
        return pl.BlockSpec((None,) + shape[1:], lambda q, core_ref: (q,) + zeros)

    grid_spec = pltpu.PrefetchScalarGridSpec(
        num_scalar_prefetch=1,
        grid=(N_CHIPS,),
        in_specs=[slot(b.shape, True) for b in bigs] + [slot(s.shape, False) for s in from_sibling],
        out_specs=[slot(s.shape, False) for s in from_sibling],
    )
    return pl.pallas_call(
        body,
        grid_spec=grid_spec,
        out_shape=[jax.ShapeDtypeStruct(s.shape, s.dtype) for s in from_sibling],
        compiler_params=_params("arbitrary"),
        name=name,
    )(core, *bigs, *from_sibling)


def _chip_exchange_steps(part_refs, got_refs, send_sems, recv_sems, local_sems):
    n = len(part_refs)
    x, y, c, me = _mesh_position()
    mine = 2 * x + y

    def copies(landing):
        out = []
        for q, (px, py) in enumerate(_other_chips(x, y)):
            theirs = 2 * px + py
            for b in range(n):
                out.append(pltpu.make_async_remote_copy(
                    src_ref=part_refs[b].at[theirs], dst_ref=got_refs[b].at[theirs if landing else mine],
                    send_sem=send_sems.at[q * n + b], recv_sem=recv_sems.at[q * n + b],
                    device_id=(px, py, c), device_id_type=pl.DeviceIdType.MESH))
        return out

    def own():
        return [pltpu.make_async_copy(part_refs[b].at[mine], got_refs[b].at[mine], local_sems.at[b])
                for b in range(n)]

    def start():
        for cp in own() + copies(False):
            cp.start()

    def finish():
        for cp in copies(True):
            cp.wait_recv()
        for cp in copies(False):
            cp.wait_send()
        for cp in own():
            cp.wait()

    return start, finish


def _chip_exchange_scratch(n):
    return [pltpu.SemaphoreType.DMA(((N_CHIPS - 1) * n,)), pltpu.SemaphoreType.DMA(((N_CHIPS - 1) * n,)),
            pltpu.SemaphoreType.DMA((n,))]


def _direct_gather_steps(srcs, dsts, send_sems, recv_sems, local_sems):
    n = len(srcs)
    x, y, c, me = _mesh_position()

    def copies(landing):
        out = []
        for k in range(1, N_DEV):
            peer, peer_idx = _flip(x, y, c, k)
            for b in range(n):
                out.append(pltpu.make_async_remote_copy(
                    src_ref=srcs[b], dst_ref=dsts[b].at[peer_idx if landing else me],
                    send_sem=send_sems.at[(k - 1) * n + b], recv_sem=recv_sems.at[(k - 1) * n + b],
                    device_id=peer, device_id_type=pl.DeviceIdType.MESH))
        return out

    def own():
        return [pltpu.make_async_copy(srcs[b], dsts[b].at[me], local_sems.at[b]) for b in range(n)]

    def start():
        for cp in own() + copies(False):
            cp.start()

    def finish():
        for cp in copies(True):
            cp.wait_recv()
        for cp in copies(False):
            cp.wait_send()
        for cp in own():
            cp.wait()

    return start, finish


def _direct_gather_scratch(n):
    return [pltpu.SemaphoreType.DMA(((N_DEV - 1) * n,)), pltpu.SemaphoreType.DMA(((N_DEV - 1) * n,)),
            pltpu.SemaphoreType.DMA((n,))]


def exchange_chips(partials, name):
    n = len(partials)

    def body(*refs):
        start, finish = _chip_exchange_steps(refs[:n], refs[n:2 * n], *refs[2 * n:])
        start()
        finish()

    return pl.pallas_call(
        body,
        in_specs=[_ANY] * n,
        out_specs=[_ANY] * n,
        out_shape=[jax.ShapeDtypeStruct(p.shape, p.dtype) for p in partials],
        scratch_shapes=_chip_exchange_scratch(n),
        name=name,
    )(*partials)


def _adamw_math(g, w, m, v):
    m_new = ADAM_B1 * m + (1.0 - ADAM_B1) * g
    v_new = ADAM_B2 * v + (1.0 - ADAM_B2) * (g * g)
    m_hat = m_new / (1.0 - ADAM_B1 ** ADAM_STEP)
    v_hat = v_new / (1.0 - ADAM_B2 ** ADAM_STEP)
    delta = -ADAM_LR * (m_hat / (jnp.sqrt(v_hat) + ADAM_EPS) + ADAM_WD * w)
    return delta, m_new, v_new


def _chip_total(p_ref):
    g = p_ref[0].astype(F32)
    for q in range(1, N_CHIPS):
        g = g + p_ref[q].astype(F32)
    return g


def chip_sum(parts, name):
    depth = len(parts)

    def body(*refs):
        for l in range(depth):
            refs[depth][l] = _chip_total(refs[l])

    return pl.pallas_call(
        body,
        out_shape=jax.ShapeDtypeStruct((depth,) + parts[0].shape[1:], F32),
        compiler_params=pltpu.CompilerParams(vmem_limit_bytes=VMEM_LIMIT),
        name=name,
    )(*parts)


def adamw_shard(parts, w, m, v, name, grad=None):
    depth = w.shape[0]
    n_in = 1 if grad is not None else depth

    def body(*refs):
        w_ref, m_ref, v_ref, g_ref, d_ref, mo_ref, vo_ref = refs[n_in:]

        def update(g):
            delta, m_new, v_new = _adamw_math(g, w_ref[...], m_ref[...], v_ref[...])
            g_ref[...] = g
            d_ref[...] = delta
            mo_ref[...] = m_new
            vo_ref[...] = v_new

        if grad is not None:
            update(refs[0][...])
        else:
            for l in range(depth):
                pl.when(pl.program_id(0) == l)(functools.partial(lambda k: update(_chip_total(refs[k])), l))

    layer = pl.BlockSpec((None,) + w.shape[1:], lambda l: (l, 0, 0))
    whole = [pl.BlockSpec(p.shape, lambda l: (0, 0, 0)) for p in (parts or [])]
    return pl.pallas_call(
        body,
        grid=(depth,),
        in_specs=([layer] if grad is not None else whole) + [layer] * 3,
        out_specs=[layer] * 4,
        out_shape=[jax.ShapeDtypeStruct(w.shape, F32)] * 4,
        compiler_params=_params("arbitrary"),
        name=name,
    )(*([grad] if grad is not None else parts), w, m, v)


def adamw_packed(parts, w, m, v, name):
    n_slots = parts.shape[0]

    def body(p_ref, w_ref, m_ref, v_ref, g_ref, d_ref, mo_ref, vo_ref):
        g = p_ref[0]
        for s in range(1, n_slots):
            g = g + p_ref[s]
        delta, m_new, v_new = _adamw_math(g, w_ref[...], m_ref[...], v_ref[...])
        g_ref[...] = g
        d_ref[...] = delta
        mo_ref[...] = m_new
        vo_ref[...] = v_new

    return pl.pallas_call(
        body,
        out_shape=[jax.ShapeDtypeStruct(w.shape, F32)] * 4,
        compiler_params=pltpu.CompilerParams(vmem_limit_bytes=VMEM_LIMIT),
        name=name,
    )(parts, w, m, v)


SHARDED = ("w_in", "w_q_b", "w_kv_b", "w_out", "w_gate", "w_up", "w_down")
TRANSPOSED = ("w_in", "w_q_b", "w_gate", "w_up")
ADAM_TRANSPOSED = ("w_q_b", "w_gate", "w_up")
BEFORE_ATTENTION = ("w_in", "w_q_b", "w_kv_b")
BEFORE_ATTENTION_BWD = ("w_out", "w_gate", "w_up", "w_down")
REPLICATED = ("attn_norm_g", "pool_scale", "q_a_norm_g", "kv_a_norm_g", "q_norm_g", "k_norm_g", "ffn_norm_g")


def _pack_flat(arrays, rows):
    flat = jnp.concatenate([a.reshape(-1) for a in arrays])
    return jnp.pad(flat, (0, rows * LANES - flat.shape[0])).reshape(rows, LANES)


def _unpack_flat(packed, shapes):
    flat, out, at = packed.reshape(-1), [], 0
    for shp in shapes:
        n = 1
        for d in shp:
            n *= d
        out.append(flat[at:at + n].reshape(shp))
        at += n
    return out


def _rope_lane_tables(length):
    inv = 1.0 / (ROPE_THETA ** (jnp.arange(0, QK_ROPE_DIM, 2, dtype=F32) / QK_ROPE_DIM))
    ang = jnp.arange(length, dtype=F32)[:, None] * inv[None, :]
    cos, sin = jnp.cos(ang), jnp.sin(ang)
    ones = jnp.ones((length, QK_NOPE_DIM), F32)
    zeros = jnp.zeros((length, QK_NOPE_DIM), F32)
    tail = HEAD_PAD - QK_HEAD_DIM
    cosf = jnp.concatenate([ones, cos, cos, ones[:, :tail]], axis=1)
    sins = jnp.concatenate([zeros, -sin, sin, zeros[:, :tail]], axis=1)
    return cosf, sins


def _pad_lanes(vec, at, width=HEAD_PAD):
    return jnp.pad(vec, (at, width - at - vec.shape[0])).reshape(1, width)


def kernel(x, meta_tokens, attn_norm_g, w_in, w_pool, pool_scale, q_a_norm_g, w_q_b, kv_a_norm_g, w_kv_b, q_norm_g, k_norm_g, w_out, ffn_norm_g, w_gate, w_up, w_down, loss_target, m_meta_tokens, m_attn_norm_g, m_w_in, m_w_pool, m_pool_scale, m_q_a_norm_g, m_w_q_b, m_kv_a_norm_g, m_w_kv_b, m_q_norm_g, m_k_norm_g, m_w_out, m_ffn_norm_g, m_w_gate, m_w_up, m_w_down, v_meta_tokens, v_attn_norm_g, v_w_in, v_w_pool, v_pool_scale, v_q_a_norm_g, v_w_q_b, v_kv_a_norm_g, v_w_kv_b, v_q_norm_g, v_k_norm_g, v_w_out, v_ffn_norm_g, v_w_gate, v_w_up, v_w_down):
    weights = dict(meta_tokens=meta_tokens, attn_norm_g=attn_norm_g, w_in=w_in, w_pool=w_pool, pool_scale=pool_scale,
                   q_a_norm_g=q_a_norm_g, w_q_b=w_q_b, kv_a_norm_g=kv_a_norm_g, w_kv_b=w_kv_b, q_norm_g=q_norm_g,
                   k_norm_g=k_norm_g, w_out=w_out, ffn_norm_g=ffn_norm_g, w_gate=w_gate, w_up=w_up, w_down=w_down)
    mom1 = dict(meta_tokens=m_meta_tokens, attn_norm_g=m_attn_norm_g, w_in=m_w_in, w_pool=m_w_pool,
                pool_scale=m_pool_scale, q_a_norm_g=m_q_a_norm_g, w_q_b=m_w_q_b, kv_a_norm_g=m_kv_a_norm_g,
                w_kv_b=m_w_kv_b, q_norm_g=m_q_norm_g, k_norm_g=m_k_norm_g, w_out=m_w_out, ffn_norm_g=m_ffn_norm_g,
                w_gate=m_w_gate, w_up=m_w_up, w_down=m_w_down)
    mom2 = dict(meta_tokens=v_meta_tokens, attn_norm_g=v_attn_norm_g, w_in=v_w_in, w_pool=v_w_pool,
                pool_scale=v_pool_scale, q_a_norm_g=v_q_a_norm_g, w_q_b=v_w_q_b, kv_a_norm_g=v_kv_a_norm_g,
                w_kv_b=v_w_kv_b, q_norm_g=v_q_norm_g, k_norm_g=v_k_norm_g, w_out=v_w_out, ffn_norm_g=v_ffn_norm_g,
                w_gate=v_w_gate, w_up=v_w_up, w_down=v_w_down)
    order = ("meta_tokens", "attn_norm_g", "w_in", "w_pool", "pool_scale", "q_a_norm_g", "w_q_b", "kv_a_norm_g",
             "w_kv_b", "q_norm_g", "k_norm_g", "w_out", "ffn_norm_g", "w_gate", "w_up", "w_down")
    depth = w_in.shape[0]
    seq = x.shape[1]
    length = N_META + seq
    lp = -(-length // ROW_TILE) * ROW_TILE
    in_cols = w_in.shape[2]

    local = {n: (jnp.swapaxes(weights[n], 1, 2) if n in TRANSPOSED else weights[n]) for n in SHARDED}
    cast = cast_bf16([local[n] for n in SHARDED], "cast_weights")
    shards = [{n: cast[k * depth + l] for k, n in enumerate(SHARDED)} for l in range(depth)]
    gathered = all_gather([shards[0][n] for n in BEFORE_ATTENTION] + [meta_tokens], "all_gather")
    g8l = [dict(zip(BEFORE_ATTENTION, gathered[:-1]))] + [{} for _ in range(depth - 1)]
    meta_full = jnp.transpose(gathered[-1], (1, 0, 2)).reshape(N_META, D_MODEL)

    s1, s2, s3 = POOL_WIDTH, POOL_WIDTH + Q_LORA_RANK, POOL_WIDTH + Q_LORA_RANK + KV_LORA_RANK
    zpad = lambda n: jnp.zeros((1, n, D_MODEL), BF16)

    def padded_in_proj(w8):
        w_in_t = jnp.swapaxes(w8, 0, 1).reshape(1, N_DEV * in_cols, D_MODEL)
        return jnp.concatenate([w_in_t[:, :s3], zpad(QK_NOPE_DIM), w_in_t[:, s3:],
                                zpad(HEAD_PAD - QK_HEAD_DIM)], axis=1)

    w_in_ps = [None] * depth
    w_pool_b = w_pool.astype(BF16)

    cosf, sins = _rope_lane_tables(lp)
    seg = _segment_matrix()
    row = lambda a, l: a[l].reshape(1, -1)

    h = jnp.concatenate([meta_full, x[0], jnp.zeros((lp - length, D_MODEL), F32)], axis=0)
    target = jnp.pad(loss_target[0], ((N_META, lp - length), (0, 0)))
    saved = []
    for l in range(depth):
        gq = _pad_lanes(q_norm_g[l], 0)
        gkn = _pad_lanes(k_norm_g[l, :QK_NOPE_DIM], 0)
        gkr = _pad_lanes(k_norm_g[l, QK_NOPE_DIM:], QK_NOPE_DIM)
        g8 = g8l[l]
        w_in_ps[l] = padded_in_proj(g8["w_in"])
        a, u, c_q, c_kv, kr = norm_mm(
            h, row(attn_norm_g, l), w_in_ps[l], 0,
            [(0, s1), (s1, Q_LORA_RANK), (s2, KV_LORA_RANK), (s3, HEAD_PAD)], [F32] * 4, "in_proj")
        cat = pool_fwd(u, w_pool_b[l], row(pool_scale, l), "pool_fwd")
        qn, q = norm_mm_heads(c_q, row(q_a_norm_g, l), g8["w_q_b"], 0, "q_proj", transposed=True)
        kvn, kv = norm_mm_heads(c_kv, row(kv_a_norm_g, l), g8["w_kv_b"], 0, "kv_proj")
        qp, kp, v = qk_prep_fwd(q, kv, kr, cosf, sins, seg, gq, gkn, gkr, "qk_prep_fwd")
        riders = [(l, n) for n in SHARDED if n not in g8]
        riders += [(l + 1, n) for n in BEFORE_ATTENTION if l + 1 < depth]
        o, lse, cat, *arrived = attn_fwd(qp, kp, v, cat, "attn_fwd_gather", gather=[shards[k][n] for k, n in riders])
        for (k, n), w8 in zip(riders, arrived):
            g8l[k][n] = w8
        h_mid, g = mm_res(cat, g8["w_out"], h, 0, "out_proj", gamma=row(ffn_norm_g, l))
        gate, up, act = ffn_up(g, g8["w_gate"], g8["w_up"], 0, "ffn_up")
        h_next = mm_res(act, g8["w_down"], h_mid, 0, "ffn_down")[0]
        saved.append(dict(h=h, a=a, u=u, c_q=c_q, c_kv=c_kv, kr=kr, qn=qn, q=q, kvn=kvn, kv=kv, v=v, qp=qp, kp=kp,
                          o=o, lse=lse, cat=cat, h_mid=h_mid, g=g, gate=gate, up=up, act=act,
                          gq=gq, gkn=gkn, gkr=gkr))
        h = h_next

    dh, sq = loss_head(h, target, seq, "loss_head")
    loss = lax.psum(0.5 / D_MODEL * jnp.sum(sq), ("x", "y", "c"))

    core = lax.axis_index("c").astype(jnp.int32).reshape(1)
    small_grads = {n: [None] * depth for n in REPLICATED + ("w_pool",)}
    pending = []
    got = [{} for _ in range(depth)]

    def pair_reduce(l, names, slots, small):
        bigs = [slots[n] for n in names]
        if small is None:
            from_sibling, got_small = exchange_pair(bigs, None, "grad_exchange_pair"), None
        else:
            *from_sibling, got_small = exchange_pair(bigs, small, "grad_exchange_pair_small")
        partial = pair_sum(bigs, list(from_sibling), core, "grad_pair_sum")
        return [(l, n, p) for n, p in zip(names, partial)], got_small

    for l in reversed(range(depth)):
        s = saved[l]
        g8 = g8l[l]
        w_in_p = w_in_ps[l]
        slots = {}
        dgate, dup = ffn_bwd_act(dh, g8["w_down"], 0, s["gate"], s["up"], "ffn_bwd_act")
        slot_of = lambda n, full: full.reshape(g8[n].shape[:1] + g8[n].shape[2:])
        slots["w_down"] = slot_of("w_down", mm_tn([s["act"]], dh, "dw_down", out_dtype=BF16)[0])
        dw_gate_t, dw_up_t = mm_tn([dgate, dup], s["g"], "dw_gate_up", out_dtype=BF16)
        slots["w_gate"] = slot_of("w_gate", dw_gate_t)
        slots["w_up"] = slot_of("w_up", dw_up_t)
        dh_mid, dg_ffn = mm_nt_normbwd([("gathered", dgate, g8["w_gate"]), ("gathered", dup, g8["w_up"])],
                                       s["h_mid"], row(ffn_norm_g, l), dh, 0, "ffn_bwd_in", wide=False)
        small_grads["ffn_norm_g"][l] = dg_ffn[0]
        slots["w_out"] = slot_of("w_out", mm_tn([s["cat"]], dh_mid, "dw_out", out_dtype=BF16)[0])
        early = [slots[n] for n in BEFORE_ATTENTION_BWD]
        dy_pool, do, delta, *from_sibling = out_proj_bwd(dh_mid, g8["w_out"], 0, s["o"], "out_proj_bwd", pair=early)
        partial = pair_sum(early, from_sibling, core, "grad_pair_sum")
        pending += [(l, n, p) for n, p in zip(BEFORE_ATTENTION_BWD, partial)]
        du, dw_pool, dscale = pool_bwd(s["u"], dy_pool, w_pool_b[l], row(pool_scale, l), "pool_bwd")
        small_grads["w_pool"][l] = dw_pool
        small_grads["pool_scale"][l] = dscale[0]
        pool_rider = [jnp.stack(small_grads["w_pool"], axis=0).reshape(-1, LANES)] if l == 0 else []
        dqp, dkp, dv, *arrived = attn_bwd(s["qp"], s["kp"], s["v"], do, s["lse"], delta, "attn_bwd_exchange",
                                          exchange=[p for _, _, p in pending], gather=pool_rider)
        if pool_rider:
            got_pool = arrived.pop()
        for (k, n, _), parts in zip(pending, arrived):
            got[k][n] = parts
        dq, dkv, dkr, dgq, dgkn, dgkr = qk_prep_bwd(dqp, dkp, dv, s["q"], s["kv"], s["kr"], cosf, sins, seg,
                                                    s["gq"], s["gkn"], s["gkr"], "qk_prep_bwd")
        small_grads["q_norm_g"][l] = dgq[0, :QK_HEAD_DIM]
        small_grads["k_norm_g"][l] = jnp.concatenate([dgkn[0, :QK_NOPE_DIM], dgkr[0, QK_NOPE_DIM:QK_HEAD_DIM]])
        heads_first = lambda full: jnp.swapaxes(full.reshape(full.shape[0], MLA_HEADS, HEAD_PAD), 0, 1)
        dw_q_t = mm_tn([dq], s["qn"], "dw_q", out_dtype=BF16)[0]
        slots["w_q_b"] = dw_q_t.reshape(MLA_HEADS, HEAD_PAD, -1)[:, :QK_HEAD_DIM]
        dc_q, dg_qa = mm_nt_normbwd([("heads_t", dq, g8["w_q_b"])], s["c_q"], row(q_a_norm_g, l), None, 0,
                                    "q_proj_bwd")
        small_grads["q_a_norm_g"][l] = dg_qa[0]
        slots["w_kv_b"] = heads_first(mm_tn([s["kvn"]], dkv, "dw_kv", out_dtype=BF16)[0])
        dc_kv, dg_kva = mm_nt_normbwd([("heads", dkv, g8["w_kv_b"])], s["c_kv"], row(kv_a_norm_g, l), None, 0,
                                      "kv_proj_bwd")
        small_grads["kv_a_norm_g"][l] = dg_kva[0]
        dw_pool_t, dw_q_t, dw_kv_t, dw_rope_t = mm_tn([du, dc_q, dc_kv, dkr], s["a"], "dw_in", out_dtype=BF16)
        dw_in_t = jnp.concatenate([dw_pool_t, dw_q_t, dw_kv_t, dw_rope_t[QK_NOPE_DIM:QK_HEAD_DIM]], axis=0)
        slots["w_in"] = slot_of("w_in", dw_in_t)
        dh, dg_attn = mm_nt_normbwd(
            [("rows", du, w_in_p, 0), ("rows", dc_q, w_in_p, s1), ("rows", dc_kv, w_in_p, s2),
             ("rows", dkr, w_in_p, s3)],
            s["h"], row(attn_norm_g, l), dh_mid, 0, "in_proj_bwd")
        small_grads["attn_norm_g"][l] = dg_attn[0]

        small_slots = None
        if l == 0:
            rep_shapes = [weights[n].shape for n in REPLICATED]
            rep_count = sum(int(jnp.size(weights[n])) for n in REPLICATED)
            rep_rows = -(-rep_count // (8 * LANES)) * 8
            rep_packed = _pack_flat([jnp.stack(small_grads[n], axis=0) for n in REPLICATED], rep_rows)
            meta_slots = jnp.transpose(dh[:N_META].reshape(N_META, N_DEV, LANES), (1, 0, 2))
            small_slots = jnp.concatenate(
                [meta_slots, jnp.broadcast_to(rep_packed[None], (N_DEV, rep_rows, LANES))], axis=1)
        pending, got_small = pair_reduce(l, [n for n in SHARDED if n not in BEFORE_ATTENTION_BWD], slots,
                                         small_slots)
    for (k, n, _), parts in zip(pending, exchange_chips([p for _, _, p in pending], "grad_exchange_chips")):
        got[k][n] = parts

    grad_x = dh[N_META:length][None]

    per = [{} for _ in range(4)]
    for n in SHARDED:
        parts = [got[l][n] for l in range(depth)]
        if n in ADAM_TRANSPOSED:
            t = lambda a: jnp.swapaxes(a, 1, 2)
            outs = [t(o) for o in adamw_shard(parts, local[n], t(mom1[n]), t(mom2[n]), "adamw_" + n)]
        elif n in TRANSPOSED:
            grad = jnp.swapaxes(chip_sum(parts, "chip_sum_" + n), 1, 2)
            outs = adamw_shard(None, weights[n], mom1[n], mom2[n], "adamw_" + n, grad=grad)
        else:
            outs = adamw_shard(parts, weights[n], mom1[n], mom2[n], "adamw_" + n)
        for k in range(4):
            per[k][n] = outs[k]
    ps = lambda src: jnp.concatenate(
        [src["meta_tokens"], _pack_flat([src[n] for n in REPLICATED], rep_rows)], axis=0)
    small_out = adamw_packed(got_small, ps(weights), ps(mom1), ps(mom2), "adamw_small")
    as_rows = lambda a: a.reshape(-1, LANES)
    pool_out = adamw_packed(got_pool, as_rows(w_pool), as_rows(m_w_pool), as_rows(v_w_pool), "adamw_w_pool")
    for k in range(4):
        per[k]["meta_tokens"] = small_out[k][:N_META]
        per[k].update(zip(REPLICATED, _unpack_flat(small_out[k][N_META:], rep_shapes)))
        per[k]["w_pool"] = pool_out[k].reshape(w_pool.shape)
    return (loss, grad_x, *[per[0][n] for n in order], *[per[1][n] for n in order],
            *[per[2][n] for n in order], *[per[3][n] for n in order])
```

```python
import functools

import jax
import jax.numpy as jnp
from jax import lax
from jax.experimental import pallas as pl
from jax.experimental.pallas import tpu as pltpu

F32 = jnp.float32
BF16 = jnp.bfloat16

D_MODEL = 1024
N_META = 16
POOL_WIDTH = 512
POOL_WINDOWS = (2, 4, 8, 16)
POOL_GROUP_DIM = 128
POOL_HALO = 16
MLA_HEADS = 8
QK_NOPE_DIM = 64
QK_ROPE_DIM = 32
QK_HEAD_DIM = 96
V_HEAD_DIM = 64
HEAD_PAD = 128
Q_LORA_RANK = 384
KV_LORA_RANK = 256
ROPE_THETA = 10000.0
RMS_EPS = 1e-6
ATTN_SCALE = QK_HEAD_DIM ** -0.5
LOG2_E = 1.4426950408889634
SCORE_SCALE = ATTN_SCALE * LOG2_E

ADAM_LR = 0.001
ADAM_B1 = 0.9
ADAM_B2 = 0.999
ADAM_EPS = 1e-08
ADAM_WD = 0.01
ADAM_STEP = 10

N_DEV = 8
N_CHIPS = 4
LANES = 128
ROW_TILE = 384
LONG_TILE = 1056
WIDE_TILE = 1056
VMEM_LIMIT = 56 * 1024 * 1024


def _params(*sem):
    return pltpu.CompilerParams(dimension_semantics=sem, vmem_limit_bytes=VMEM_LIMIT)


def _wide_tile(rows):
    return WIDE_TILE if rows % WIDE_TILE == 0 else ROW_TILE


def _row_spec(tile, width):
    return pl.BlockSpec((tile, width), lambda i: (i, 0))


def _const_spec(shape):
    return pl.BlockSpec(shape, lambda i: tuple(0 for _ in shape))


def _layer_spec(w, layer):
    return pl.BlockSpec((None,) + w.shape[1:], lambda *_: (layer, 0, 0))


def _gathered_spec(w8, layer):
    return pl.BlockSpec((N_DEV, None) + w8.shape[2:], lambda *_: (0, layer, 0, 0))


def _nt(a, b):
    return lax.dot_general(a, b, (((1,), (1,)), ((), ())), preferred_element_type=F32)


def _tn(a, b):
    return lax.dot_general(a, b, (((0,), (0,)), ((), ())), preferred_element_type=F32)


def _nn(a, b):
    return jnp.dot(a, b, preferred_element_type=F32)


def _silu(g):
    return g * (1.0 / (1.0 + jnp.exp(-g)))


def _rms(xf):
    return lax.rsqrt(jnp.mean(xf * xf, axis=-1, keepdims=True) + RMS_EPS)


def norm_mm(x, gamma, wt, layer, splits, dtypes, name):
    L, K = x.shape
    tm = _wide_tile(L)

    def body(x_ref, g_ref, w_ref, a_ref, *z_refs):
        xf = x_ref[...]
        a = ((xf * _rms(xf)) * g_ref[...]).astype(BF16)
        a_ref[...] = a
        z = _nt(a, w_ref[...])
        for (s, n), zr in zip(splits, z_refs):
            zr[...] = z[:, s:s + n].astype(zr.dtype)

    widths = [n for _, n in splits]
    return pl.pallas_call(
        body,
        grid=(L // tm,),
        in_specs=[_row_spec(tm, K), _const_spec((1, K)), _layer_spec(wt, layer)],
        out_specs=[_row_spec(tm, K)] + [_row_spec(tm, n) for n in widths],
        out_shape=[jax.ShapeDtypeStruct((L, K), BF16)]
        + [jax.ShapeDtypeStruct((L, n), dt) for n, dt in zip(widths, dtypes)],
        compiler_params=_params("arbitrary"),
        name=name,
    )(x, gamma, wt)


def norm_mm_heads(x, gamma, w8, layer, name, transposed=False):
    L, K = x.shape
    hw = w8.shape[-2] if transposed else w8.shape[-1]
    tm = _wide_tile(L)

    def body(x_ref, g_ref, w_ref, a_ref, z_ref):
        xf = x_ref[...]
        a = ((xf * _rms(xf)) * g_ref[...]).astype(BF16)
        a_ref[...] = a
        if hw < HEAD_PAD:
            z_ref[...] = jnp.zeros_like(z_ref)
        for j in range(MLA_HEADS):
            z_ref[:, j * HEAD_PAD:j * HEAD_PAD + hw] = _nt(a, w_ref[j]) if transposed else _nn(a, w_ref[j])

    return pl.pallas_call(
        body,
        grid=(L // tm,),
        in_specs=[_row_spec(tm, K), _const_spec((1, K)), _gathered_spec(w8, layer)],
        out_specs=[_row_spec(tm, K), _row_spec(tm, MLA_HEADS * HEAD_PAD)],
        out_shape=[jax.ShapeDtypeStruct((L, K), BF16), jax.ShapeDtypeStruct((L, MLA_HEADS * HEAD_PAD), F32)],
        compiler_params=_params("arbitrary"),
        name=name,
    )(x, gamma, w8)


FF_GROUP = 4


def _ff_spec(w8, layer):
    return pl.BlockSpec((FF_GROUP, None) + w8.shape[2:], lambda j, i: (j, layer, 0, 0))


def ffn_up(g, w_gate8, w_up8, layer, name):
    L, K = g.shape
    fb = w_gate8.shape[-2]
    tm, tf = _wide_tile(L), FF_GROUP * fb

    def body(a_ref, wg_ref, wu_ref, gate_ref, up_ref, act_ref):
        a = a_ref[...]
        gate = _nt(a, wg_ref[...].reshape(tf, K))
        up = _nt(a, wu_ref[...].reshape(tf, K))
        gate_ref[...] = gate.astype(BF16)
        up_ref[...] = up.astype(BF16)
        act_ref[...] = (_silu(gate) * up).astype(BF16)

    tile = pl.BlockSpec((tm, tf), lambda j, i: (i, j))
    F = N_DEV * fb
    return pl.pallas_call(
        body,
        grid=(N_DEV // FF_GROUP, L // tm),
        in_specs=[pl.BlockSpec((tm, K), lambda j, i: (i, 0)), _ff_spec(w_gate8, layer), _ff_spec(w_up8, layer)],
        out_specs=[tile, tile, tile],
        out_shape=[
            jax.ShapeDtypeStruct((L, F), BF16),
            jax.ShapeDtypeStruct((L, F), BF16),
            jax.ShapeDtypeStruct((L, F), BF16),
        ],
        compiler_params=_params("arbitrary", "arbitrary"),
        name=name,
    )(g, w_gate8, w_up8)


def mm_res(a, w8, res, layer, name, gamma=None):
    L = a.shape[0]
    kb, N = w8.shape[-2:]
    tm = _wide_tile(L)
    normed = gamma is not None

    def body(a_ref, w_ref, r_ref, *rest):
        out = r_ref[...] + _nn(a_ref[...], w_ref[...].reshape(N_DEV * kb, N))
        if normed:
            g_ref, o_ref, n_ref = rest
            n_ref[...] = ((out * _rms(out)) * g_ref[...]).astype(BF16)
        else:
            (o_ref,) = rest
        o_ref[...] = out

    in_specs = [_row_spec(tm, N_DEV * kb), _gathered_spec(w8, layer), _row_spec(tm, N)]
    out_specs = [_row_spec(tm, N)]
    out_shape = [jax.ShapeDtypeStruct((L, N), F32)]
    args = [a, w8, res]
    if normed:
        in_specs.append(_const_spec((1, N)))
        out_specs.append(_row_spec(tm, N))
        out_shape.append(jax.ShapeDtypeStruct((L, N), BF16))
        args.append(gamma)
    return pl.pallas_call(
        body,
        grid=(L // tm,),
        in_specs=in_specs,
        out_specs=out_specs,
        out_shape=out_shape,
        compiler_params=_params("arbitrary"),
        name=name,
    )(*args)


def out_proj_bwd(dz, w8, layer, o, name, pair=()):
    L, N = dz.shape
    kb = w8.shape[-2]
    C = o.shape[1]
    tm = _wide_tile(L)
    n_p = len(pair)
    n_steps = L // tm

    def body(dz_ref, w_ref, o_ref, *rest):
        dy_ref, do_ref, dl_ref = rest[n_p:n_p + 3]
        if n_p:
            start, finish = _pair_steps(rest[:n_p], rest[n_p + 3:2 * n_p + 3], *rest[2 * n_p + 3:])
            pl.when(pl.program_id(0) == 0)(start)
        d = _nt(dz_ref[...].astype(BF16), w_ref[...].reshape(N_DEV * kb, N))
        dy_ref[...] = d[:, :C]
        do_ref[...] = d[:, C:]
        left = lax.broadcasted_iota(jnp.int32, (tm, LANES), 1) < V_HEAD_DIM
        for p in range(C // LANES):
            cols = slice(p * LANES, (p + 1) * LANES)
            prod = d[:, C + p * LANES:C + (p + 1) * LANES] * o_ref[:, cols]
            d0 = jnp.sum(jnp.where(left, prod, 0.0), axis=1, keepdims=True)
            d1 = jnp.sum(jnp.where(left, 0.0, prod), axis=1, keepdims=True)
            dl_ref[:, cols] = jnp.where(left, d0, d1)
        if n_p:
            pl.when(pl.program_id(0) == n_steps - 1)(finish)

    return pl.pallas_call(
        body,
        grid=(n_steps,),
        in_specs=[_row_spec(tm, N), _gathered_spec(w8, layer), _row_spec(tm, C)] + [_ANY] * n_p,
        out_specs=[_row_spec(tm, C)] * 3 + [_ANY] * n_p,
        out_shape=[jax.ShapeDtypeStruct((L, C), F32)] * 3
        + [jax.ShapeDtypeStruct((N_CHIPS,) + p.shape[1:], p.dtype) for p in pair],
        scratch_shapes=[pltpu.SemaphoreType.DMA((N_CHIPS * n_p,))] * 2 if n_p else [],
        compiler_params=_params("arbitrary"),
        name=name,
    )(dz, w8, o, *pair)


def mm_nt_normbwd(terms, x, gamma, dres, layer, name, wide=True, pair=()):
    L, K = x.shape
    tm = _wide_tile(L) if wide else ROW_TILE
    n_terms = len(terms)
    has_res = dres is not None
    weights = []
    for t in terms:
        if not any(t[2] is u for u in weights):
            weights.append(t[2])
    which = [[t[2] is u for u in weights].index(True) for t in terms]
    n_in = n_terms + len(weights)
    n_p = len(pair)
    n_steps = L // tm

    def body(*refs):
        dz_refs = refs[:n_terms]
        w_refs = [refs[n_terms + n] for n in which]
        x_ref, g_ref = refs[n_in], refs[n_in + 1]
        pos = n_in + 2
        r_ref = refs[pos] if has_res else None
        pos += has_res
        dx_ref, dg_ref = refs[pos + n_p], refs[pos + n_p + 1]
        if n_p:
            start, finish = _pair_steps(refs[pos:pos + n_p], refs[pos + n_p + 2:pos + 2 * n_p + 2],
                                        *refs[pos + 2 * n_p + 2:])
            pl.when(pl.program_id(0) == 0)(start)
        da = None
        for t, dz_ref, w_ref in zip(terms, dz_refs, w_refs):
            if t[0] == "rows":
                n, at = t[1].shape[1], t[3]
                parts = [_nn(dz_ref[...].astype(BF16), w_ref[at:at + n, :])]
            elif t[0] == "heads":
                hw = t[2].shape[-1]
                parts = [_nt(dz_ref[:, j * HEAD_PAD:j * HEAD_PAD + hw].astype(BF16), w_ref[j])
                         for j in range(MLA_HEADS)]
            elif t[0] == "heads_t":
                hw = t[2].shape[-2]
                parts = [_nn(dz_ref[:, j * HEAD_PAD:j * HEAD_PAD + hw].astype(BF16), w_ref[j])
                         for j in range(MLA_HEADS)]
            else:
                nb = t[2].shape[-2]
                parts = [_nn(dz_ref[...].astype(BF16), w_ref[...].reshape(N_DEV * nb, K))]
            for p in parts:
                da = p if da is None else da + p
        xf = x_ref[...]
        r = _rms(xf)
        xh = xf * r

        @pl.when(pl.program_id(0) == 0)
        def _():
            dg_ref[...] = jnp.zeros_like(dg_ref)

        dg_ref[...] += jnp.sum(da * xh, axis=0, keepdims=True)
        dxh = da * g_ref[...]
        dx = r * (dxh - xh * jnp.mean(dxh * xh, axis=-1, keepdims=True))
        if has_res:
            dx = dx + r_ref[...]
        dx_ref[...] = dx
        if n_p:
            pl.when(pl.program_id(0) == n_steps - 1)(finish)

    in_specs = [_row_spec(tm, t[1].shape[1]) for t in terms]
    for w in weights:
        in_specs.append(_layer_spec(w, layer) if w.ndim == 3 else _gathered_spec(w, layer))
    in_specs += [_row_spec(tm, K), _const_spec((1, K))]
    args = [t[1] for t in terms] + weights + [x, gamma]
    if has_res:
        in_specs.append(_row_spec(tm, K))
        args.append(dres)
    return pl.pallas_call(
        body,
        grid=(n_steps,),
        in_specs=in_specs + [_ANY] * n_p,
        out_specs=[_row_spec(tm, K), _const_spec((1, K))] + [_ANY] * n_p,
        out_shape=[jax.ShapeDtypeStruct((L, K), F32), jax.ShapeDtypeStruct((1, K), F32)]
        + [jax.ShapeDtypeStruct((N_CHIPS,) + p.shape[1:], p.dtype) for p in pair],
        scratch_shapes=[pltpu.SemaphoreType.DMA((N_CHIPS * n_p,))] * 2 if n_p else [],
        compiler_params=_params("arbitrary"),
        name=name,
    )(*args, *pair)


MAX_OUT_ROWS = 1408


def mm_tn(a_list, b, name, out_dtype=F32):
    n = len(a_list)
    L, N = b.shape
    tks = [MAX_OUT_ROWS if (a.shape[1] > MAX_OUT_ROWS and a.shape[1] % MAX_OUT_ROWS == 0) else a.shape[1]
           for a in a_list]
    blocks = a_list[0].shape[1] // tks[0]
    assert all(a.shape[1] // tk == blocks for a, tk in zip(a_list, tks))
    tl = LONG_TILE if L % LONG_TILE == 0 else ROW_TILE
    n_l = L // tl

    def body(*refs):
        a_refs, b_ref = refs[:n], refs[n]
        o_refs, accs = refs[n + 1:2 * n + 1], refs[2 * n + 1:]
        l = pl.program_id(1)
        bt = b_ref[...].astype(BF16)
        for a_ref, o_ref, acc in zip(a_refs, o_refs, accs):
            @pl.when(l == 0)
            def _(acc=acc):
                acc[...] = jnp.zeros_like(acc)

            acc[...] += _tn(a_ref[...].astype(BF16), bt)

            @pl.when(l == n_l - 1)
            def _(acc=acc, o_ref=o_ref):
                o_ref[...] = acc[...].astype(o_ref.dtype)

    return pl.pallas_call(
        body,
        grid=(blocks, n_l),
        in_specs=[pl.BlockSpec((tl, tk), lambda j, l: (l, j)) for tk in tks]
        + [pl.BlockSpec((tl, N), lambda j, l: (l, 0))],
        out_specs=[pl.BlockSpec((tk, N), lambda j, l: (j, 0)) for tk in tks],
        out_shape=[jax.ShapeDtypeStruct((a.shape[1], N), out_dtype) for a in a_list],
        scratch_shapes=[pltpu.VMEM((tk, N), F32) for tk in tks],
        compiler_params=_params("arbitrary", "arbitrary"),
        name=name,
    )(*a_list, b)


def ffn_bwd_act(dh, w_down8, layer, gate, up, name):
    L, K = dh.shape
    fb = w_down8.shape[-2]
    tm, tf = _wide_tile(L), FF_GROUP * fb

    def body(dh_ref, w_ref, gate_ref, up_ref, dgate_ref, dup_ref):
        dact = _nt(dh_ref[...].astype(BF16), w_ref[...].reshape(tf, K))
        g = gate_ref[...].astype(F32)
        sig = 0.5 * jnp.tanh(0.5 * g) + 0.5
        dup_ref[...] = (dact * (g * sig)).astype(BF16)
        dgate_ref[...] = (dact * up_ref[...].astype(F32) * (sig * (1.0 + g * (1.0 - sig)))).astype(BF16)

    tile = pl.BlockSpec((tm, tf), lambda j, i: (i, j))
    F = N_DEV * fb
    return pl.pallas_call(
        body,
        grid=(N_DEV // FF_GROUP, L // tm),
        in_specs=[pl.BlockSpec((tm, K), lambda j, i: (i, 0)), _ff_spec(w_down8, layer), tile, tile],
        out_specs=[tile, tile],
        out_shape=[jax.ShapeDtypeStruct((L, F), BF16), jax.ShapeDtypeStruct((L, F), BF16)],
        compiler_params=_params("arbitrary", "arbitrary"),
        name=name,
    )(dh, w_down8, gate, up)


def _pool_residual(scr, lo, tm, g, w, t):
    cols = slice(g * POOL_GROUP_DIM, (g + 1) * POOL_GROUP_DIM)
    cur = scr[lo:lo + tm, cols]
    s = cur
    for k in range(1, w):
        s = s + scr[lo - k:lo - k + tm, cols]
    cnt = jnp.minimum(t + 1, w).astype(F32)
    return s / cnt - cur


def pool_fwd(u, w_pool, scale, name):
    L, C = u.shape
    tm, halo = _wide_tile(L), POOL_HALO

    def body(u_ref, halo_ref, w_ref, s_ref, y_ref, scr):
        i = pl.program_id(0)
        scr[0:halo, :] = jnp.where(i > 0, halo_ref[...], 0.0)
        scr[halo:halo + tm, :] = u_ref[...]
        t = i * tm + lax.broadcasted_iota(jnp.int32, (tm, POOL_GROUP_DIM), 0)
        for g, w in enumerate(POOL_WINDOWS):
            cols = slice(g * POOL_GROUP_DIM, (g + 1) * POOL_GROUP_DIM)
            p = _pool_residual(scr, halo, tm, g, w, t)
            y = _nn(p.astype(BF16), w_ref[g]) * s_ref[:, cols]
            y_ref[:, cols] = y.astype(y_ref.dtype)

    return pl.pallas_call(
        body,
        grid=(L // tm,),
        in_specs=[
            _row_spec(tm, C),
            pl.BlockSpec((halo, C), lambda i: (jnp.maximum(i * (tm // halo) - 1, 0), 0)),
            _const_spec(w_pool.shape),
            _const_spec((1, C)),
        ],
        out_specs=_row_spec(tm, C),
        out_shape=jax.ShapeDtypeStruct((L, 2 * C), BF16),
        scratch_shapes=[pltpu.VMEM((tm + halo, C), F32)],
        compiler_params=_params("arbitrary"),
        name=name,
    )(u, u, w_pool, scale)


def pool_bwd(u, dy, w_pool, scale, name):
    L, C = u.shape
    tm, halo = _wide_tile(L), POOL_HALO
    n_tiles = L // tm
    last_halo = L // halo - 1

    def body(u_ref, uh_ref, dy_ref, dyh_ref, w_ref, s_ref, du_ref, dw_ref, ds_ref, scr_u, scr_q):
        i = pl.program_id(0)

        @pl.when(i == 0)
        def _():
            dw_ref[...] = jnp.zeros_like(dw_ref)
            ds_ref[...] = jnp.zeros_like(ds_ref)

        scr_u[0:halo, :] = jnp.where(i > 0, uh_ref[...], 0.0)
        scr_u[halo:halo + tm, :] = u_ref[...]
        t = i * tm + lax.broadcasted_iota(jnp.int32, (tm, POOL_GROUP_DIM), 0)
        th = (i + 1) * tm + lax.broadcasted_iota(jnp.int32, (halo, POOL_GROUP_DIM), 0)
        for g, w in enumerate(POOL_WINDOWS):
            cols = slice(g * POOL_GROUP_DIM, (g + 1) * POOL_GROUP_DIM)
            p = _pool_residual(scr_u, halo, tm, g, w, t).astype(BF16)
            wg = w_ref[g]
            sc = s_ref[:, cols]
            dy = dy_ref[:, cols]
            ds_ref[:, cols] += jnp.sum(dy * _nn(p, wg), axis=0, keepdims=True)
            dys = (dy * sc).astype(BF16)
            dw_ref[g] += _tn(p, dys)
            dp = _nt(dys, wg)
            dyh = jnp.where(i < n_tiles - 1, dyh_ref[:, cols], 0.0)
            dph = _nt((dyh * sc).astype(BF16), wg)
            scr_q[0:tm, cols] = dp / jnp.minimum(t + 1, w).astype(F32)
            scr_q[tm:tm + halo, cols] = dph / jnp.minimum(th + 1, w).astype(F32)
            acc = scr_q[0:tm, cols]
            for k in range(1, w):
                acc = acc + scr_q[k:k + tm, cols]
            du_ref[:, cols] = acc - dp

    return pl.pallas_call(
        body,
        grid=(n_tiles,),
        in_specs=[
            _row_spec(tm, C),
            pl.BlockSpec((halo, C), lambda i: (jnp.maximum(i * (tm // halo) - 1, 0), 0)),
            _row_spec(tm, C),
            pl.BlockSpec((halo, C), lambda i: (jnp.minimum((i + 1) * (tm // halo), last_halo), 0)),
            _const_spec(w_pool.shape),
            _const_spec((1, C)),
        ],
        out_specs=[_row_spec(tm, C), _const_spec(w_pool.shape), _const_spec((1, C))],
        out_shape=[
            jax.ShapeDtypeStruct((L, C), F32),
            jax.ShapeDtypeStruct(w_pool.shape, F32),
            jax.ShapeDtypeStruct((1, C), F32),
        ],
        scratch_shapes=[pltpu.VMEM((tm + halo, C), F32), pltpu.VMEM((tm + halo, C), F32)],
        compiler_params=_params("arbitrary"),
        name=name,
    )(u, u, dy, dy, w_pool, scale)


def _head_masks(rows):
    lane = lax.broadcasted_iota(jnp.int32, (rows, HEAD_PAD), 1)
    return lane, lane < QK_NOPE_DIM, (lane >= QK_NOPE_DIM) & (lane < QK_HEAD_DIM)


def _rope_swap(x, lane):
    half = QK_ROPE_DIM // 2
    swapped = jnp.where(lane < QK_NOPE_DIM + half, pltpu.roll(x, HEAD_PAD - half, 1), pltpu.roll(x, half, 1))
    return jnp.where((lane >= QK_NOPE_DIM) & (lane < QK_HEAD_DIM), swapped, 0.0)


def _seg_mean(v, seg_ref):
    hi = v.astype(BF16)
    lo = (v - hi.astype(F32)).astype(BF16)
    seg = seg_ref[...]
    return _nn(hi, seg) + _nn(lo, seg)


def _segment_matrix():
    lane = jnp.arange(HEAD_PAD)
    seg = jnp.where(lane < QK_NOPE_DIM, 0, jnp.where(lane < QK_HEAD_DIM, 1, 2))
    inv = jnp.where(lane < QK_NOPE_DIM, 1.0 / QK_NOPE_DIM, jnp.where(lane < QK_HEAD_DIM, 1.0 / QK_ROPE_DIM, 0.0))
    return jnp.where(seg[:, None] == seg[None, :], inv[None, :], 0.0).astype(BF16)


def qk_prep_fwd(q, kv, kr, cosf, sins, seg, gq, gkn, gkr, name):
    L = q.shape[0]
    tm = _wide_tile(L)
    W = MLA_HEADS * HEAD_PAD

    def body(q_ref, kv_ref, kr_ref, c_ref, s_ref, seg_ref, gq_ref, gkn_ref, gkr_ref, qo_ref, ko_ref, vo_ref):
        lane, m_n, _ = _head_masks(tm)
        cosf_, sins_ = c_ref[...], s_ref[...]
        kr_ = kr_ref[...]
        rk = lax.rsqrt(_seg_mean(kr_ * kr_, seg_ref) + RMS_EPS)
        krn = kr_ * rk * gkr_ref[...]
        krf = krn * cosf_ + _rope_swap(krn, lane) * sins_
        for h in range(MLA_HEADS):
            cols = slice(h * HEAD_PAD, (h + 1) * HEAD_PAD)
            qh = q_ref[:, cols]
            qn = qh * lax.rsqrt(_seg_mean(qh * qh, seg_ref) + RMS_EPS) * gq_ref[...]
            qo_ref[:, cols] = (qn * cosf_ + _rope_swap(qn, lane) * sins_).astype(BF16)
            kh = jnp.where(m_n, kv_ref[:, cols], 0.0)
            rn = lax.rsqrt(_seg_mean(kh * kh, seg_ref) + RMS_EPS)
            ko_ref[:, cols] = (kh * rn * gkn_ref[...] + krf).astype(BF16)
        for p in range(MLA_HEADS // 2):
            even = kv_ref[:, 2 * p * HEAD_PAD:(2 * p + 1) * HEAD_PAD]
            odd = kv_ref[:, (2 * p + 1) * HEAD_PAD:(2 * p + 2) * HEAD_PAD]
            pair = jnp.where(m_n, pltpu.roll(even, V_HEAD_DIM, 1), odd)
            vo_ref[:, p * LANES:(p + 1) * LANES] = pair.astype(BF16)

    vec = _const_spec((1, HEAD_PAD))
    return pl.pallas_call(
        body,
        grid=(L // tm,),
        in_specs=[_row_spec(tm, W), _row_spec(tm, W), _row_spec(tm, HEAD_PAD), _row_spec(tm, HEAD_PAD),
                  _row_spec(tm, HEAD_PAD), _const_spec((HEAD_PAD, HEAD_PAD)), vec, vec, vec],
        out_specs=[_row_spec(tm, W), _row_spec(tm, W), _row_spec(tm, W // 2)],
        out_shape=[jax.ShapeDtypeStruct((L, W), BF16), jax.ShapeDtypeStruct((L, W), BF16),
                   jax.ShapeDtypeStruct((L, W // 2), BF16)],
        compiler_params=_params("arbitrary"),
        name=name,
    )(q, kv, kr, cosf, sins, seg, gq, gkn, gkr)


def qk_prep_bwd(dqo, dko, dv, q, kv, kr, cosf, sins, seg, gq, gkn, gkr, name):
    L = q.shape[0]
    tm = ROW_TILE
    W = MLA_HEADS * HEAD_PAD

    def body(dqo_ref, dko_ref, dv_ref, q_ref, kv_ref, kr_ref, c_ref, s_ref, seg_ref, gq_ref, gkn_ref, gkr_ref,
             dq_ref, dkv_ref, dkr_ref, dgq_ref, dgkn_ref, dgkr_ref):
        @pl.when(pl.program_id(0) == 0)
        def _():
            dgq_ref[...] = jnp.zeros_like(dgq_ref)
            dgkn_ref[...] = jnp.zeros_like(dgkn_ref)
            dgkr_ref[...] = jnp.zeros_like(dgkr_ref)

        lane, m_n, _ = _head_masks(tm)
        cosf_, sins_ = c_ref[...], s_ref[...]
        dgq = jnp.zeros((1, HEAD_PAD), F32)
        dgkn = jnp.zeros((1, HEAD_PAD), F32)
        dkrf = jnp.zeros((tm, HEAD_PAD), F32)
        for h in range(MLA_HEADS):
            cols = slice(h * HEAD_PAD, (h + 1) * HEAD_PAD)
            dy = dqo_ref[:, cols]
            dqn = dy * cosf_ + _rope_swap(dy * sins_, lane)
            qh = q_ref[:, cols]
            rinv = lax.rsqrt(_seg_mean(qh * qh, seg_ref) + RMS_EPS)
            xh = qh * rinv
            dgq = dgq + jnp.sum(dqn * xh, axis=0, keepdims=True)
            dxh = dqn * gq_ref[...]
            dq_ref[:, cols] = rinv * (dxh - xh * _seg_mean(dxh * xh, seg_ref))
            dk = dko_ref[:, cols]
            dkrf = dkrf + dk
            kh = jnp.where(m_n, kv_ref[:, cols], 0.0)
            rn = lax.rsqrt(_seg_mean(kh * kh, seg_ref) + RMS_EPS)
            xk = kh * rn
            dgkn = dgkn + jnp.sum(dk * xk, axis=0, keepdims=True)
            dxk = dk * gkn_ref[...]
            dkn = rn * (dxk - xk * _seg_mean(dxk * xk, seg_ref))
            dvp = dv_ref[:, (h // 2) * LANES:(h // 2 + 1) * LANES]
            dvh = pltpu.roll(dvp, V_HEAD_DIM, 1) if h % 2 == 0 else dvp
            dkv_ref[:, cols] = jnp.where(m_n, dkn, dvh)
        kr_ = kr_ref[...]
        rk = lax.rsqrt(_seg_mean(kr_ * kr_, seg_ref) + RMS_EPS)
        xr = kr_ * rk
        dkrn = dkrf * cosf_ + _rope_swap(dkrf * sins_, lane)
        dgkr_ref[...] += jnp.sum(dkrn * xr, axis=0, keepdims=True)
        dxr = dkrn * gkr_ref[...]
        dkr_ref[...] = rk * (dxr - xr * _seg_mean(dxr * xr, seg_ref))
        dgq_ref[...] += dgq
        dgkn_ref[...] += dgkn

    vec = _const_spec((1, HEAD_PAD))
    return pl.pallas_call(
        body,
        grid=(L // tm,),
        in_specs=[_row_spec(tm, W), _row_spec(tm, W), _row_spec(tm, W // 2), _row_spec(tm, W), _row_spec(tm, W),
                  _row_spec(tm, HEAD_PAD), _row_spec(tm, HEAD_PAD), _row_spec(tm, HEAD_PAD),
                  _const_spec((HEAD_PAD, HEAD_PAD)), vec, vec, vec],
        out_specs=[_row_spec(tm, W), _row_spec(tm, W), _row_spec(tm, HEAD_PAD), vec, vec, vec],
        out_shape=[jax.ShapeDtypeStruct((L, W), F32), jax.ShapeDtypeStruct((L, W), F32),
                   jax.ShapeDtypeStruct((L, HEAD_PAD), F32)] + [jax.ShapeDtypeStruct((1, HEAD_PAD), F32)] * 3,
        compiler_params=_params("arbitrary"),
        name=name,
    )(dqo, dko, dv, q, kv, kr, cosf, sins, seg, gq, gkn, gkr)


def attn_fwd(qp, kp, v, cat, name, gather=()):
    L = qp.shape[0]
    tq = ROW_TILE
    n_i = L // tq
    pair_w = 2 * HEAD_PAD
    n_pairs = MLA_HEADS // 2
    n_g = len(gather)

    def body(q_ref, k_ref, v_ref, cat_in, *rest):
        del cat_in
        o_ref, lse_ref, cat_ref = rest[n_g:n_g + 3]
        i = pl.program_id(1)
        if n_g:
            pair = pl.program_id(0)
            start, forward, finish = _gather_steps(rest[:n_g], rest[n_g + 3:2 * n_g + 3], *rest[2 * n_g + 3:])
            pl.when((pair == 0) & (i == 0))(start)
            pl.when((pair == n_pairs // 2) & (i == 0))(forward)
        left = lax.broadcasted_iota(jnp.int32, (tq, LANES), 1) < V_HEAD_DIM

        def step(t, carry, masked, width=1):
            rows = pl.ds(pl.multiple_of(t * (tq * width), tq), tq * width)
            vv = v_ref[rows, :]
            out = []
            for hh in range(2):
                cols = slice(hh * HEAD_PAD, (hh + 1) * HEAD_PAD)
                m, l, acc = carry[hh]
                s = _nt(q_ref[:, cols], k_ref[rows, cols]) * SCORE_SCALE
                if masked:
                    row = lax.broadcasted_iota(jnp.int32, s.shape, 0)
                    col = lax.broadcasted_iota(jnp.int32, s.shape, 1)
                    s = jnp.where(col <= row, s, -jnp.inf)
                m_new = jnp.maximum(m, jnp.max(s, axis=1, keepdims=True))
                alpha = jnp.exp2(m - m_new)
                p = jnp.exp2(s - m_new)
                l = alpha * l + jnp.sum(p, axis=1, keepdims=True)
                acc = alpha * acc + _nn(p.astype(BF16), vv)
                out.append((m_new, l, acc))
            return tuple(out)

        one = (jnp.full((tq, 1), -jnp.inf, F32), jnp.zeros((tq, 1), F32), jnp.zeros((tq, LANES), F32))
        doubles = jnp.right_shift(i, 1)
        carry = lax.fori_loop(0, doubles, functools.partial(step, masked=False, width=2), (one, one))
        carry = lax.fori_loop(2 * doubles, i, functools.partial(step, masked=False), carry)
        (m0, l0, a0), (m1, l1, a1) = step(i, carry, True)
        o = jnp.where(left, a0 / l0, a1 / l1)
        o_ref[...] = o
        cat_ref[...] = o.astype(BF16)
        lse_ref[...] = jnp.where(left, m0 + jnp.log2(l0), m1 + jnp.log2(l1))
        if n_g:
            pl.when((pair == n_pairs - 1) & (i == n_i - 1))(finish)

    return pl.pallas_call(
        body,
        grid=(n_pairs, n_i),
        in_specs=[
            pl.BlockSpec((tq, pair_w), lambda p, i: (i, p)),
            pl.BlockSpec((L, pair_w), lambda p, i: (0, p)),
            pl.BlockSpec((L, LANES), lambda p, i: (0, p)),
            _ANY,
        ] + [_ANY] * n_g,
        out_specs=[
            pl.BlockSpec((tq, LANES), lambda p, i: (i, p)),
            pl.BlockSpec((tq, LANES), lambda p, i: (i, p)),
            pl.BlockSpec((tq, LANES), lambda p, i: (i, n_pairs + p)),
        ] + [_ANY] * n_g,
        out_shape=[jax.ShapeDtypeStruct((L, n_pairs * LANES), F32), jax.ShapeDtypeStruct((L, n_pairs * LANES), F32),
                   jax.ShapeDtypeStruct(cat.shape, cat.dtype)]
        + [jax.ShapeDtypeStruct((N_DEV,) + g.shape, g.dtype) for g in gather],
        scratch_shapes=_gather_scratch(n_g) if n_g else [],
        input_output_aliases={3: 2},
        compiler_params=_params("arbitrary", "arbitrary"),
        name=name,
    )(qp, kp, v, cat, *gather)


def attn_bwd(qp, kp, v, do, lse, delta, name, exchange=(), gather=()):
    L = qp.shape[0]
    tq = ROW_TILE
    n_q = L // tq
    pair_w = 2 * HEAD_PAD
    n_pairs = MLA_HEADS // 2
    n_x, n_g = len(exchange), len(gather)
    n_r = n_x + n_g

    def body(q_ref, k_ref, v_ref, do_ref, lse_ref, dl_ref, *rest):
        dq_ref, dk_ref, dv_ref = rest[n_r:n_r + 3]
        riders_in, riders_out, sems = rest[:n_r], rest[n_r + 3:2 * n_r + 3], rest[2 * n_r + 3:]
        j = pl.program_id(1)
        pair = pl.program_id(0)
        steps = []
        if n_x:
            steps.append(_chip_exchange_steps(riders_in[:n_x], riders_out[:n_x], *sems[:3]))
        if n_g:
            steps.append(_direct_gather_steps(riders_in[n_x:], riders_out[n_x:], *sems[-3:]))
        for start, _ in steps:
            pl.when((pair == 0) & (j == 0))(start)

        @pl.when(j == 0)
        def _():
            dq_ref[...] = jnp.zeros_like(dq_ref)

        dk_ref[...] = jnp.zeros_like(dk_ref)
        dv_ref[...] = jnp.zeros_like(dv_ref)

        def step(t, carry, masked, width=1, first=0):
            rows = pl.ds(pl.multiple_of((first + t * width) * tq, tq), tq * width)
            do = do_ref[rows, :]
            left = lax.broadcasted_iota(jnp.int32, do.shape, 1) < V_HEAD_DIM
            vv = v_ref[...]
            dv = None
            for hh in range(2):
                cols = slice(hh * HEAD_PAD, (hh + 1) * HEAD_PAD)
                stat = slice(hh * V_HEAD_DIM, hh * V_HEAD_DIM + 1)
                q = q_ref[rows, cols]
                k = k_ref[:, cols]
                dom = jnp.where(left if hh == 0 else jnp.logical_not(left), do, 0.0).astype(BF16)
                s = _nt(q, k) * SCORE_SCALE
                p = jnp.exp2(s - lse_ref[rows, stat])
                if masked:
                    row = lax.broadcasted_iota(jnp.int32, p.shape, 0)
                    col = lax.broadcasted_iota(jnp.int32, p.shape, 1)
                    p = jnp.where(col <= row, p, 0.0)
                dp = _nt(dom, vv)
                ds = (p * (dp - dl_ref[rows, stat]) * ATTN_SCALE).astype(BF16)
                dq_ref[rows, cols] += _nn(ds, k)
                dk_ref[:, cols] += _tn(ds, q)
                t = _tn(p.astype(BF16), dom)
                dv = t if dv is None else dv + t
            dv_ref[...] += dv
            return carry

        @pl.when(j < n_q - 1)
        def _():
            step(0, 0, True, width=2, first=j)
            doubles = jnp.right_shift(n_q - 2 - j, 1)
            lax.fori_loop(0, doubles, functools.partial(step, masked=False, width=2, first=j + 2), 0)
            lax.fori_loop(j + 2 + 2 * doubles, n_q, functools.partial(step, masked=False), 0)

        @pl.when(j == n_q - 1)
        def _():
            step(j, 0, True)

        for _, finish in steps:
            pl.when((pair == n_pairs - 1) & (j == n_q - 1))(finish)

    return pl.pallas_call(
        body,
        grid=(n_pairs, n_q),
        in_specs=[
            pl.BlockSpec((L, pair_w), lambda p, j: (0, p)),
            pl.BlockSpec((tq, pair_w), lambda p, j: (j, p)),
            pl.BlockSpec((tq, LANES), lambda p, j: (j, p)),
            pl.BlockSpec((L, LANES), lambda p, j: (0, p)),
            pl.BlockSpec((L, LANES), lambda p, j: (0, p)),
            pl.BlockSpec((L, LANES), lambda p, j: (0, p)),
        ] + [_ANY] * n_r,
        out_specs=[
            pl.BlockSpec((L, pair_w), lambda p, j: (0, p)),
            pl.BlockSpec((tq, pair_w), lambda p, j: (j, p)),
            pl.BlockSpec((tq, LANES), lambda p, j: (j, p)),
        ] + [_ANY] * n_r,
        out_shape=[
            jax.ShapeDtypeStruct((L, n_pairs * pair_w), F32),
            jax.ShapeDtypeStruct((L, n_pairs * pair_w), F32),
            jax.ShapeDtypeStruct((L, n_pairs * LANES), F32),
        ] + [jax.ShapeDtypeStruct(e.shape, e.dtype) for e in exchange]
        + [jax.ShapeDtypeStruct((N_DEV,) + g.shape, g.dtype) for g in gather],
        scratch_shapes=(_chip_exchange_scratch(n_x) if n_x else []) + (_direct_gather_scratch(n_g) if n_g else []),
        compiler_params=_params("arbitrary", "arbitrary"),
        name=name,
    )(qp, kp, v, do, lse, delta, *exchange, *gather)


def loss_head(h, target, n_real, name):
    L, D = h.shape
    tm = _wide_tile(L)

    def body(h_ref, t_ref, dh_ref, sq_ref):
        i = pl.program_id(0)

        @pl.when(i == 0)
        def _():
            sq_ref[...] = jnp.zeros_like(sq_ref)

        t = i * tm + lax.broadcasted_iota(jnp.int32, (tm, D), 0)
        real = (t >= N_META) & (t < N_META + n_real)
        diff = jnp.where(real, h_ref[...] - t_ref[...], 0.0)
        dh_ref[...] = diff * (1.0 / D)
        sq_ref[...] += jnp.sum(diff * diff, axis=0, keepdims=True)

    return pl.pallas_call(
        body,
        grid=(L // tm,),
        in_specs=[_row_spec(tm, D), _row_spec(tm, D)],
        out_specs=[_row_spec(tm, D), _const_spec((1, D))],
        out_shape=[jax.ShapeDtypeStruct((L, D), F32), jax.ShapeDtypeStruct((1, D), F32)],
        compiler_params=_params("arbitrary"),
        name=name,
    )(h, target)


def _mesh_position():
    x, y, c = lax.axis_index("x"), lax.axis_index("y"), lax.axis_index("c")
    return x, y, c, 4 * x + 2 * y + c


def _flip(x, y, c, k):
    px = 1 - x if k & 4 else x
    py = 1 - y if k & 2 else y
    pc = 1 - c if k & 1 else c
    return (px, py, pc), 4 * px + 2 * py + pc


def _other_chips(x, y):
    return [(1 - x, y), (x, 1 - y), (1 - x, 1 - y)]


def _dev_index(px, py, pc):
    return 4 * px + 2 * py + pc


_ANY = pl.BlockSpec(memory_space=pl.ANY)


def cast_bf16(arrays, name):
    n = len(arrays)
    depth = arrays[0].shape[0]

    def body(*refs):
        for k, src in enumerate(refs[:n]):
            for l in range(depth):
                refs[n + k * depth + l][0] = src[l].astype(BF16)

    return pl.pallas_call(
        body,
        out_shape=[jax.ShapeDtypeStruct((1,) + a.shape[1:], BF16) for a in arrays for _ in range(depth)],
        compiler_params=pltpu.CompilerParams(vmem_limit_bytes=VMEM_LIMIT),
        name=name,
    )(*arrays)


def _gather_steps(srcs, dsts, send_sems, recv_sems, local_sems):
    n = len(srcs)
    x, y, c, me = _mesh_position()
    sibling = (x, y, 1 - c)
    chips = _other_chips(x, y)

    def copy(k, b, block, to, from_src=False):
        dst = dsts[b].at[block]
        return pltpu.make_async_remote_copy(
            src_ref=srcs[b] if from_src else dst, dst_ref=dst, send_sem=send_sems.at[k * n + b],
            recv_sem=recv_sems.at[k * n + b], device_id=to, device_id_type=pl.DeviceIdType.MESH)

    def own():
        return [pltpu.make_async_copy(srcs[b], dsts[b].at[me], local_sems.at[b]) for b in range(n)]

    def first():
        out = []
        for b in range(n):
            out.append(copy(0, b, me, sibling, from_src=True))
            out += [copy(1 + q, b, me, (*chip, c), from_src=True) for q, chip in enumerate(chips)]
        return out

    def passed():
        return [copy(4 + q, b, _dev_index(*chip, c), sibling) for q, chip in enumerate(chips) for b in range(n)]

    def start():
        for cp in own() + first():
            cp.start()

    def forward():
        for q, chip in enumerate(chips):
            for b in range(n):
                copy(1 + q, b, _dev_index(*chip, c), sibling).wait_recv()
        for cp in passed():
            cp.start()

    def finish():
        for b in range(n):
            copy(0, b, _dev_index(x, y, 1 - c), sibling).wait_recv()
            for q, chip in enumerate(chips):
                copy(4 + q, b, _dev_index(*chip, 1 - c), sibling).wait_recv()
        for cp in first() + passed():
            cp.wait_send()
        for cp in own():
            cp.wait()

    return start, forward, finish


def _gather_scratch(n):
    copies = N_DEV - 1
    return [pltpu.SemaphoreType.DMA((copies * n,)), pltpu.SemaphoreType.DMA((copies * n,)),
            pltpu.SemaphoreType.DMA((n,))]


def all_gather(payloads, name):
    n = len(payloads)

    def body(*refs):
        start, forward, finish = _gather_steps(refs[:n], refs[n:2 * n], *refs[2 * n:])
        start()
        forward()
        finish()

    return pl.pallas_call(
        body,
        in_specs=[_ANY] * n,
        out_specs=[_ANY] * n,
        out_shape=[jax.ShapeDtypeStruct((N_DEV,) + p.shape, p.dtype) for p in payloads],
        scratch_shapes=_gather_scratch(n),
        name=name,
    )(*payloads)


def _pair_steps(big_refs, sib_refs, send_sems, recv_sems):
    n = len(big_refs)
    x, y, c, me = _mesh_position()

    def copies():
        return [pltpu.make_async_remote_copy(
            src_ref=big_refs[b].at[_dev_index(q // 2, q % 2, 1 - c)], dst_ref=sib_refs[b].at[q],
            send_sem=send_sems.at[b * N_CHIPS + q], recv_sem=recv_sems.at[b * N_CHIPS + q],
            device_id=(x, y, 1 - c), device_id_type=pl.DeviceIdType.MESH) for b in range(n) for q in range(N_CHIPS)]

    def start():
        for cp in copies():
            cp.start()

    def finish():
        for cp in copies():
            cp.wait_recv()
        for cp in copies():
            cp.wait_send()

    return start, finish


def exchange_pair(bigs, small, name):
    n = len(bigs)
    has_small = small is not None
    n_big = N_CHIPS * n
    n_sems = n_big + (N_DEV - 1 if has_small else 0)

    def body(*refs):
        big_refs = refs[:n]
        pos = n + has_small
        sib_refs = refs[pos:pos + n]
        send_sems, recv_sems, local_sem = refs[-3:]
        x, y, c, me = _mesh_position()
        sibling = (x, y, 1 - c)
        copies = []
        for b in range(n):
            for q in range(N_CHIPS):
                copies.append(pltpu.make_async_remote_copy(
                    src_ref=big_refs[b].at[_dev_index(q // 2, q % 2, 1 - c)], dst_ref=sib_refs[b].at[q],
                    send_sem=send_sems.at[b * N_CHIPS + q], recv_sem=recv_sems.at[b * N_CHIPS + q],
                    device_id=sibling, device_id_type=pl.DeviceIdType.MESH))
        waits = list(copies)
        if has_small:
            small_ref, gsmall_ref = refs[n], refs[pos + n]
            own = pltpu.make_async_copy(small_ref.at[me], gsmall_ref.at[me], local_sem.at[0])
            own.start()
            for k in range(1, N_DEV):
                peer, peer_idx = _flip(x, y, c, k)
                s = n_big + k - 1
                copies.append(pltpu.make_async_remote_copy(
                    src_ref=small_ref.at[peer_idx], dst_ref=gsmall_ref.at[me], send_sem=send_sems.at[s],
                    recv_sem=recv_sems.at[s], device_id=peer, device_id_type=pl.DeviceIdType.MESH))
                waits.append(pltpu.make_async_remote_copy(
                    src_ref=small_ref.at[peer_idx], dst_ref=gsmall_ref.at[peer_idx], send_sem=send_sems.at[s],
                    recv_sem=recv_sems.at[s], device_id=peer, device_id_type=pl.DeviceIdType.MESH))
        for cp in copies:
            cp.start()
        for cp in waits:
            cp.wait_recv()
        for cp in copies:
            cp.wait_send()
        if has_small:
            own.wait()

    out_shape = [jax.ShapeDtypeStruct((N_CHIPS,) + b.shape[1:], b.dtype) for b in bigs]
    args = list(bigs)
    if has_small:
        out_shape.append(jax.ShapeDtypeStruct(small.shape, small.dtype))
        args.append(small)
    return pl.pallas_call(
        body,
        in_specs=[_ANY] * len(args),
        out_specs=[_ANY] * len(out_shape),
        out_shape=out_shape,
        scratch_shapes=[pltpu.SemaphoreType.DMA((n_sems,)), pltpu.SemaphoreType.DMA((n_sems,)),
                        pltpu.SemaphoreType.DMA((1,))],
        name=name,
    )(*args)


def pair_sum(bigs, from_sibling, core, name):
    n = len(bigs)

    def body(core_ref, *refs):
        del core_ref
        for mine, sib, out in zip(refs[:n], refs[n:2 * n], refs[2 * n:]):
            out[...] = (mine[...].astype(F32) + sib[...].astype(F32)).astype(out.dtype)

    def slot(shape, picked):
        zeros = (0,) * (len(shape) - 1)
        if picked:
            return pl.BlockSpec((None,) + shape[1:], lambda q, core_ref: (2 * q + core_ref[0],) + zeros)
        return pl.BlockSpec((None,) + shape[1:], lambda q, core_ref: (q,) + zeros)

    grid_spec = pltpu.PrefetchScalarGridSpec(
        num_scalar_prefetch=1,
        grid=(N_CHIPS,),
        in_specs=[slot(b.shape, True) for b in bigs] + [slot(s.shape, False) for s in from_sibling],
        out_specs=[slot(s.shape, False) for s in from_sibling],
    )
    return pl.pallas_call(
        body,
        grid_spec=grid_spec,
        out_shape=[jax.ShapeDtypeStruct(s.shape, s.dtype) for s in from_sibling],
        compiler_params=_params("arbitrary"),
        name=name,
    )(core, *bigs, *from_sibling)


def _chip_exchange_steps(part_refs, got_refs, send_sems, recv_sems, local_sems):
    n = len(part_refs)
    x, y, c, me = _mesh_position()
    mine = 2 * x + y

    def copies(landing):
        out = []
        for q, (px, py) in enumerate(_other_chips(x, y)):
            theirs = 2 * px + py
            for b in range(n):
                out.append(pltpu.make_async_remote_copy(
                    src_ref=part_refs[b].at[theirs], dst_ref=got_refs[b].at[theirs if landing else mine],
                    send_sem=send_sems.at[q * n + b], recv_sem=recv_sems.at[q * n + b],
                    device_id=(px, py, c), device_id_type=pl.DeviceIdType.MESH))
        return out

    def own():
        return [pltpu.make_async_copy(part_refs[b].at[mine], got_refs[b].at[mine], local_sems.at[b])
                for b in range(n)]

    def start():
        for cp in own() + copies(False):
            cp.start()

    def finish():
        for cp in copies(True):
            cp.wait_recv()
        for cp in copies(False):
            cp.wait_send()
        for cp in own():
            cp.wait()

    return start, finish


def _chip_exchange_scratch(n):
    return [pltpu.SemaphoreType.DMA(((N_CHIPS - 1) * n,)), pltpu.SemaphoreType.DMA(((N_CHIPS - 1) * n,)),
            pltpu.SemaphoreType.DMA((n,))]


def _direct_gather_steps(srcs, dsts, send_sems, recv_sems, local_sems):
    n = len(srcs)
    x, y, c, me = _mesh_position()

    def copies(landing):
        out = []
        for k in range(1, N_DEV):
            peer, peer_idx = _flip(x, y, c, k)
            for b in range(n):
                out.append(pltpu.make_async_remote_copy(
                    src_ref=srcs[b], dst_ref=dsts[b].at[peer_idx if landing else me],
                    send_sem=send_sems.at[(k - 1) * n + b], recv_sem=recv_sems.at[(k - 1) * n + b],
                    device_id=peer, device_id_type=pl.DeviceIdType.MESH))
        return out

    def own():
        return [pltpu.make_async_copy(srcs[b], dsts[b].at[me], local_sems.at[b]) for b in range(n)]

    def start():
        for cp in own() + copies(False):
            cp.start()

    def finish():
        for cp in copies(True):
            cp.wait_recv()
        for cp in copies(False):
            cp.wait_send()
        for cp in own():
            cp.wait()

    return start, finish


def _direct_gather_scratch(n):
    return [pltpu.SemaphoreType.DMA(((N_DEV - 1) * n,)), pltpu.SemaphoreType.DMA(((N_DEV - 1) * n,)),
            pltpu.SemaphoreType.DMA((n,))]


def exchange_chips(partials, name):
    n = len(partials)

    def body(*refs):
        start, finish = _chip_exchange_steps(refs[:n], refs[n:2 * n], *refs[2 * n:])
        start()
        finish()

    return pl.pallas_call(
        body,
        in_specs=[_ANY] * n,
        out_specs=[_ANY] * n,
        out_shape=[jax.ShapeDtypeStruct(p.shape, p.dtype) for p in partials],
        scratch_shapes=_chip_exchange_scratch(n),
        name=name,
    )(*partials)


def _adamw_math(g, w, m, v):
    m_new = ADAM_B1 * m + (1.0 - ADAM_B1) * g
    v_new = ADAM_B2 * v + (1.0 - ADAM_B2) * (g * g)
    m_hat = m_new / (1.0 - ADAM_B1 ** ADAM_STEP)
    v_hat = v_new / (1.0 - ADAM_B2 ** ADAM_STEP)
    delta = -ADAM_LR * (m_hat / (jnp.sqrt(v_hat) + ADAM_EPS) + ADAM_WD * w)
    return delta, m_new, v_new


def _chip_total(p_ref):
    g = p_ref[0].astype(F32)
    for q in range(1, N_CHIPS):
        g = g + p_ref[q].astype(F32)
    return g


def chip_sum(parts, name):
    depth = len(parts)

    def body(*refs):
        for l in range(depth):
            refs[depth][l] = _chip_total(refs[l])

    return pl.pallas_call(
        body,
        out_shape=jax.ShapeDtypeStruct((depth,) + parts[0].shape[1:], F32),
        compiler_params=pltpu.CompilerParams(vmem_limit_bytes=VMEM_LIMIT),
        name=name,
    )(*parts)


def adamw_shard(parts, w, m, v, name, grad=None):
    depth = w.shape[0]
    n_in = 1 if grad is not None else depth

    def body(*refs):
        w_ref, m_ref, v_ref, g_ref, d_ref, mo_ref, vo_ref = refs[n_in:]

        def update(g):
            delta, m_new, v_new = _adamw_math(g, w_ref[...], m_ref[...], v_ref[...])
            g_ref[...] = g
            d_ref[...] = delta
            mo_ref[...] = m_new
            vo_ref[...] = v_new

        if grad is not None:
            update(refs[0][...])
        else:
            for l in range(depth):
                pl.when(pl.program_id(0) == l)(functools.partial(lambda k: update(_chip_total(refs[k])), l))

    layer = pl.BlockSpec((None,) + w.shape[1:], lambda l: (l, 0, 0))
    whole = [pl.BlockSpec(p.shape, lambda l: (0, 0, 0)) for p in (parts or [])]
    return pl.pallas_call(
        body,
        grid=(depth,),
        in_specs=([layer] if grad is not None else whole) + [layer] * 3,
        out_specs=[layer] * 4,
        out_shape=[jax.ShapeDtypeStruct(w.shape, F32)] * 4,
        compiler_params=_params("arbitrary"),
        name=name,
    )(*([grad] if grad is not None else parts), w, m, v)


def adamw_packed(parts, w, m, v, name):
    n_slots = parts.shape[0]

    def body(p_ref, w_ref, m_ref, v_ref, g_ref, d_ref, mo_ref, vo_ref):
        g = p_ref[0]
        for s in range(1, n_slots):
            g = g + p_ref[s]
        delta, m_new, v_new = _adamw_math(g, w_ref[...], m_ref[...], v_ref[...])
        g_ref[...] = g
        d_ref[...] = delta
        mo_ref[...] = m_new
        vo_ref[...] = v_new

    return pl.pallas_call(
        body,
        out_shape=[jax.ShapeDtypeStruct(w.shape, F32)] * 4,
        compiler_params=pltpu.CompilerParams(vmem_limit_bytes=VMEM_LIMIT),
        name=name,
    )(parts, w, m, v)


SHARDED = ("w_in", "w_q_b", "w_kv_b", "w_out", "w_gate", "w_up", "w_down")
TRANSPOSED = ("w_in", "w_q_b", "w_gate", "w_up")
ADAM_TRANSPOSED = ("w_q_b", "w_gate", "w_up")
BEFORE_ATTENTION = ("w_in", "w_q_b", "w_kv_b")
BEFORE_ATTENTION_BWD = ("w_out", "w_gate", "w_up", "w_down")
REPLICATED = ("attn_norm_g", "pool_scale", "q_a_norm_g", "kv_a_norm_g", "q_norm_g", "k_norm_g", "ffn_norm_g")


def _pack_flat(arrays, rows):
    flat = jnp.concatenate([a.reshape(-1) for a in arrays])
    return jnp.pad(flat, (0, rows * LANES - flat.shape[0])).reshape(rows, LANES)


def _unpack_flat(packed, shapes):
    flat, out, at = packed.reshape(-1), [], 0
    for shp in shapes:
        n = 1
        for d in shp:
            n *= d
        out.append(flat[at:at + n].reshape(shp))
        at += n
    return out


def _rope_lane_tables(length):
    inv = 1.0 / (ROPE_THETA ** (jnp.arange(0, QK_ROPE_DIM, 2, dtype=F32) / QK_ROPE_DIM))
    ang = jnp.arange(length, dtype=F32)[:, None] * inv[None, :]
    cos, sin = jnp.cos(ang), jnp.sin(ang)
    ones = jnp.ones((length, QK_NOPE_DIM), F32)
    zeros = jnp.zeros((length, QK_NOPE_DIM), F32)
    tail = HEAD_PAD - QK_HEAD_DIM
    cosf = jnp.concatenate([ones, cos, cos, ones[:, :tail]], axis=1)
    sins = jnp.concatenate([zeros, -sin, sin, zeros[:, :tail]], axis=1)
    return cosf, sins


def _pad_lanes(vec, at, width=HEAD_PAD):
    return jnp.pad(vec, (at, width - at - vec.shape[0])).reshape(1, width)


def kernel(x, meta_tokens, attn_norm_g, w_in, w_pool, pool_scale, q_a_norm_g, w_q_b, kv_a_norm_g, w_kv_b, q_norm_g, k_norm_g, w_out, ffn_norm_g, w_gate, w_up, w_down, loss_target, m_meta_tokens, m_attn_norm_g, m_w_in, m_w_pool, m_pool_scale, m_q_a_norm_g, m_w_q_b, m_kv_a_norm_g, m_w_kv_b, m_q_norm_g, m_k_norm_g, m_w_out, m_ffn_norm_g, m_w_gate, m_w_up, m_w_down, v_meta_tokens, v_attn_norm_g, v_w_in, v_w_pool, v_pool_scale, v_q_a_norm_g, v_w_q_b, v_kv_a_norm_g, v_w_kv_b, v_q_norm_g, v_k_norm_g, v_w_out, v_ffn_norm_g, v_w_gate, v_w_up, v_w_down):
    weights = dict(meta_tokens=meta_tokens, attn_norm_g=attn_norm_g, w_in=w_in, w_pool=w_pool, pool_scale=pool_scale,
                   q_a_norm_g=q_a_norm_g, w_q_b=w_q_b, kv_a_norm_g=kv_a_norm_g, w_kv_b=w_kv_b, q_norm_g=q_norm_g,
                   k_norm_g=k_norm_g, w_out=w_out, ffn_norm_g=ffn_norm_g, w_gate=w_gate, w_up=w_up, w_down=w_down)
    mom1 = dict(meta_tokens=m_meta_tokens, attn_norm_g=m_attn_norm_g, w_in=m_w_in, w_pool=m_w_pool,
                pool_scale=m_pool_scale, q_a_norm_g=m_q_a_norm_g, w_q_b=m_w_q_b, kv_a_norm_g=m_kv_a_norm_g,
                w_kv_b=m_w_kv_b, q_norm_g=m_q_norm_g, k_norm_g=m_k_norm_g, w_out=m_w_out, ffn_norm_g=m_ffn_norm_g,
                w_gate=m_w_gate, w_up=m_w_up, w_down=m_w_down)
    mom2 = dict(meta_tokens=v_meta_tokens, attn_norm_g=v_attn_norm_g, w_in=v_w_in, w_pool=v_w_pool,
                pool_scale=v_pool_scale, q_a_norm_g=v_q_a_norm_g, w_q_b=v_w_q_b, kv_a_norm_g=v_kv_a_norm_g,
                w_kv_b=v_w_kv_b, q_norm_g=v_q_norm_g, k_norm_g=v_k_norm_g, w_out=v_w_out, ffn_norm_g=v_ffn_norm_g,
                w_gate=v_w_gate, w_up=v_w_up, w_down=v_w_down)
    order = ("meta_tokens", "attn_norm_g", "w_in", "w_pool", "pool_scale", "q_a_norm_g", "w_q_b", "kv_a_norm_g",
             "w_kv_b", "q_norm_g", "k_norm_g", "w_out", "ffn_norm_g", "w_gate", "w_up", "w_down")
    depth = w_in.shape[0]
    seq = x.shape[1]
    length = N_META + seq
    lp = -(-length // ROW_TILE) * ROW_TILE
    in_cols = w_in.shape[2]

    local = {n: (jnp.swapaxes(weights[n], 1, 2) if n in TRANSPOSED else weights[n]) for n in SHARDED}
    cast = cast_bf16([local[n] for n in SHARDED], "cast_weights")
    shards = [{n: cast[k * depth + l] for k, n in enumerate(SHARDED)} for l in range(depth)]
    gathered = all_gather([shards[0][n] for n in BEFORE_ATTENTION] + [meta_tokens], "all_gather")
    g8l = [dict(zip(BEFORE_ATTENTION, gathered[:-1]))] + [{} for _ in range(depth - 1)]
    meta_full = jnp.transpose(gathered[-1], (1, 0, 2)).reshape(N_META, D_MODEL)

    s1, s2, s3 = POOL_WIDTH, POOL_WIDTH + Q_LORA_RANK, POOL_WIDTH + Q_LORA_RANK + KV_LORA_RANK
    zpad = lambda n: jnp.zeros((1, n, D_MODEL), BF16)

    def padded_in_proj(w8):
        w_in_t = jnp.swapaxes(w8, 0, 1).reshape(1, N_DEV * in_cols, D_MODEL)
        return jnp.concatenate([w_in_t[:, :s3], zpad(QK_NOPE_DIM), w_in_t[:, s3:],
                                zpad(HEAD_PAD - QK_HEAD_DIM)], axis=1)

    w_in_ps = [None] * depth
    w_pool_b = w_pool.astype(BF16)

    cosf, sins = _rope_lane_tables(lp)
    seg = _segment_matrix()
    row = lambda a, l: a[l].reshape(1, -1)

    h = jnp.concatenate([meta_full, x[0], jnp.zeros((lp - length, D_MODEL), F32)], axis=0)
    target = jnp.pad(loss_target[0], ((N_META, lp - length), (0, 0)))
    saved = []
    for l in range(depth):
        gq = _pad_lanes(q_norm_g[l], 0)
        gkn = _pad_lanes(k_norm_g[l, :QK_NOPE_DIM], 0)
        gkr = _pad_lanes(k_norm_g[l, QK_NOPE_DIM:], QK_NOPE_DIM)
        g8 = g8l[l]
        w_in_ps[l] = padded_in_proj(g8["w_in"])
        a, u, c_q, c_kv, kr = norm_mm(
            h, row(attn_norm_g, l), w_in_ps[l], 0,
            [(0, s1), (s1, Q_LORA_RANK), (s2, KV_LORA_RANK), (s3, HEAD_PAD)], [F32] * 4, "in_proj")
        cat = pool_fwd(u, w_pool_b[l], row(pool_scale, l), "pool_fwd")
        qn, q = norm_mm_heads(c_q, row(q_a_norm_g, l), g8["w_q_b"], 0, "q_proj", transposed=True)
        kvn, kv = norm_mm_heads(c_kv, row(kv_a_norm_g, l), g8["w_kv_b"], 0, "kv_proj")
        qp, kp, v = qk_prep_fwd(q, kv, kr, cosf, sins, seg, gq, gkn, gkr, "qk_prep_fwd")
        riders = [(l, n) for n in SHARDED if n not in g8]
        riders += [(l + 1, n) for n in BEFORE_ATTENTION if l + 1 < depth]
        o, lse, cat, *arrived = attn_fwd(qp, kp, v, cat, "attn_fwd_gather", gather=[shards[k][n] for k, n in riders])
        for (k, n), w8 in zip(riders, arrived):
            g8l[k][n] = w8
        h_mid, g = mm_res(cat, g8["w_out"], h, 0, "out_proj", gamma=row(ffn_norm_g, l))
        gate, up, act = ffn_up(g, g8["w_gate"], g8["w_up"], 0, "ffn_up")
        h_next = mm_res(act, g8["w_down"], h_mid, 0, "ffn_down")[0]
        saved.append(dict(h=h, a=a, u=u, c_q=c_q, c_kv=c_kv, kr=kr, qn=qn, q=q, kvn=kvn, kv=kv, v=v, qp=qp, kp=kp,
                          o=o, lse=lse, cat=cat, h_mid=h_mid, g=g, gate=gate, up=up, act=act,
                          gq=gq, gkn=gkn, gkr=gkr))
        h = h_next

    dh, sq = loss_head(h, target, seq, "loss_head")
    loss = lax.psum(0.5 / D_MODEL * jnp.sum(sq), ("x", "y", "c"))

    core = lax.axis_index("c").astype(jnp.int32).reshape(1)
    small_grads = {n: [None] * depth for n in REPLICATED + ("w_pool",)}
    pending = []
    got = [{} for _ in range(depth)]

    def pair_reduce(l, names, slots, small):
        bigs = [slots[n] for n in names]
        if small is None:
            from_sibling, got_small = exchange_pair(bigs, None, "grad_exchange_pair"), None
        else:
            *from_sibling, got_small = exchange_pair(bigs, small, "grad_exchange_pair_small")
        partial = pair_sum(bigs, list(from_sibling), core, "grad_pair_sum")
        return [(l, n, p) for n, p in zip(names, partial)], got_small

    for l in reversed(range(depth)):
        s = saved[l]
        g8 = g8l[l]
        w_in_p = w_in_ps[l]
        slots = {}
        dgate, dup = ffn_bwd_act(dh, g8["w_down"], 0, s["gate"], s["up"], "ffn_bwd_act")
        slot_of = lambda n, full: full.reshape(g8[n].shape[:1] + g8[n].shape[2:])
        slots["w_down"] = slot_of("w_down", mm_tn([s["act"]], dh, "dw_down", out_dtype=BF16)[0])
        dw_gate_t, dw_up_t = mm_tn([dgate, dup], s["g"], "dw_gate_up", out_dtype=BF16)
        slots["w_gate"] = slot_of("w_gate", dw_gate_t)
        slots["w_up"] = slot_of("w_up", dw_up_t)
        ffn_slots = [slots[n] for n in BEFORE_ATTENTION_BWD[1:]]
        dh_mid, dg_ffn, *from_sibling = mm_nt_normbwd(
            [("gathered", dgate, g8["w_gate"]), ("gathered", dup, g8["w_up"])],
            s["h_mid"], row(ffn_norm_g, l), dh, 0, "ffn_bwd_in", wide=False, pair=ffn_slots)
        small_grads["ffn_norm_g"][l] = dg_ffn[0]
        slots["w_out"] = slot_of("w_out", mm_tn([s["cat"]], dh_mid, "dw_out", out_dtype=BF16)[0])
        early = [slots[n] for n in BEFORE_ATTENTION_BWD]
        dy_pool, do, delta, out_sibling = out_proj_bwd(dh_mid, g8["w_out"], 0, s["o"], "out_proj_bwd",
                                                       pair=early[:1])
        partial = pair_sum(early, [out_sibling] + from_sibling, core, "grad_pair_sum")
        pending += [(l, n, p) for n, p in zip(BEFORE_ATTENTION_BWD, partial)]
        du, dw_pool, dscale = pool_bwd(s["u"], dy_pool, w_pool_b[l], row(pool_scale, l), "pool_bwd")
        small_grads["w_pool"][l] = dw_pool
        small_grads["pool_scale"][l] = dscale[0]
        pool_rider = [jnp.stack(small_grads["w_pool"], axis=0).reshape(-1, LANES)] if l == 0 else []
        dqp, dkp, dv, *arrived = attn_bwd(s["qp"], s["kp"], s["v"], do, s["lse"], delta, "attn_bwd_exchange",
                                          exchange=[p for _, _, p in pending], gather=pool_rider)
        if pool_rider:
            got_pool = arrived.pop()
        for (k, n, _), parts in zip(pending, arrived):
            got[k][n] = parts
        dq, dkv, dkr, dgq, dgkn, dgkr = qk_prep_bwd(dqp, dkp, dv, s["q"], s["kv"], s["kr"], cosf, sins, seg,
                                                    s["gq"], s["gkn"], s["gkr"], "qk_prep_bwd")
        small_grads["q_norm_g"][l] = dgq[0, :QK_HEAD_DIM]
        small_grads["k_norm_g"][l] = jnp.concatenate([dgkn[0, :QK_NOPE_DIM], dgkr[0, QK_NOPE_DIM:QK_HEAD_DIM]])
        heads_first = lambda full: jnp.swapaxes(full.reshape(full.shape[0], MLA_HEADS, HEAD_PAD), 0, 1)
        dw_q_t = mm_tn([dq], s["qn"], "dw_q", out_dtype=BF16)[0]
        slots["w_q_b"] = dw_q_t.reshape(MLA_HEADS, HEAD_PAD, -1)[:, :QK_HEAD_DIM]
        dc_q, dg_qa = mm_nt_normbwd([("heads_t", dq, g8["w_q_b"])], s["c_q"], row(q_a_norm_g, l), None, 0,
                                    "q_proj_bwd")
        small_grads["q_a_norm_g"][l] = dg_qa[0]
        slots["w_kv_b"] = heads_first(mm_tn([s["kvn"]], dkv, "dw_kv", out_dtype=BF16)[0])
        dc_kv, dg_kva = mm_nt_normbwd([("heads", dkv, g8["w_kv_b"])], s["c_kv"], row(kv_a_norm_g, l), None, 0,
                                      "kv_proj_bwd")
        small_grads["kv_a_norm_g"][l] = dg_kva[0]
        dw_pool_t, dw_q_t, dw_kv_t, dw_rope_t = mm_tn([du, dc_q, dc_kv, dkr], s["a"], "dw_in", out_dtype=BF16)
        dw_in_t = jnp.concatenate([dw_pool_t, dw_q_t, dw_kv_t, dw_rope_t[QK_NOPE_DIM:QK_HEAD_DIM]], axis=0)
        slots["w_in"] = slot_of("w_in", dw_in_t)
        dh, dg_attn = mm_nt_normbwd(
            [("rows", du, w_in_p, 0), ("rows", dc_q, w_in_p, s1), ("rows", dc_kv, w_in_p, s2),
             ("rows", dkr, w_in_p, s3)],
            s["h"], row(attn_norm_g, l), dh_mid, 0, "in_proj_bwd")
        small_grads["attn_norm_g"][l] = dg_attn[0]

        small_slots = None
        if l == 0:
            rep_shapes = [weights[n].shape for n in REPLICATED]
            rep_count = sum(int(jnp.size(weights[n])) for n in REPLICATED)
            rep_rows = -(-rep_count // (8 * LANES)) * 8
            rep_packed = _pack_flat([jnp.stack(small_grads[n], axis=0) for n in REPLICATED], rep_rows)
            meta_slots = jnp.transpose(dh[:N_META].reshape(N_META, N_DEV, LANES), (1, 0, 2))
            small_slots = jnp.concatenate(
                [meta_slots, jnp.broadcast_to(rep_packed[None], (N_DEV, rep_rows, LANES))], axis=1)
        pending, got_small = pair_reduce(l, [n for n in SHARDED if n not in BEFORE_ATTENTION_BWD], slots,
                                         small_slots)
    for (k, n, _), parts in zip(pending, exchange_chips([p for _, _, p in pending], "grad_exchange_chips")):
        got[k][n] = parts

    grad_x = dh[N_META:length][None]

    per = [{} for _ in range(4)]
    for n in SHARDED:
        parts = [got[l][n] for l in range(depth)]
        if n in ADAM_TRANSPOSED:
            t = lambda a: jnp.swapaxes(a, 1, 2)
            outs = [t(o) for o in adamw_shard(parts, local[n], t(mom1[n]), t(mom2[n]), "adamw_" + n)]
        elif n in TRANSPOSED:
            grad = jnp.swapaxes(chip_sum(parts, "chip_sum_" + n), 1, 2)
            outs = adamw_shard(None, weights[n], mom1[n], mom2[n], "adamw_" + n, grad=grad)
        else:
            outs = adamw_shard(parts, weights[n], mom1[n], mom2[n], "adamw_" + n)
        for k in range(4):
            per[k][n] = outs[k]
    ps = lambda src: jnp.concatenate(
        [src["meta_tokens"], _pack_flat([src[n] for n in REPLICATED], rep_rows)], axis=0)
    small_out = adamw_packed(got_small, ps(weights), ps(mom1), ps(mom2), "adamw_small")
    as_rows = lambda a: a.reshape(-1, LANES)
    pool_out = adamw_packed(got_pool, as_rows(w_pool), as_rows(m_w_pool), as_rows(v_w_pool), "adamw_w_pool")
    for k in range(4):
        per[k]["meta_tokens"] = small_out[k][:N_META]
        per[k].update(zip(REPLICATED, _unpack_flat(small_out[k][N_META:], rep_shapes)))
        per[k]["w_pool"] = pool_out[k].reshape(w_pool.shape)
    return (loss, grad_x, *[per[0][n] for n in order], *[per[1][n] for n in order],
            *[per[2][n] for n in order], *[per[3][n] for n in order])
```

```python
import functools

import jax
import jax.numpy as jnp
from jax import lax
from jax.experimental import pallas as pl
from jax.experimental.pallas import tpu as pltpu

F32 = jnp.float32
BF16 = jnp.bfloat16

D_MODEL = 1024
N_META = 16
POOL_WIDTH = 512
POOL_WINDOWS = (2, 4, 8, 16)
POOL_GROUP_DIM = 128
POOL_HALO = 16
MLA_HEADS = 8
QK_NOPE_DIM = 64
QK_ROPE_DIM = 32
QK_HEAD_DIM = 96
V_HEAD_DIM = 64
HEAD_PAD = 128
Q_LORA_RANK = 384
KV_LORA_RANK = 256
ROPE_THETA = 10000.0
RMS_EPS = 1e-6
ATTN_SCALE = QK_HEAD_DIM ** -0.5
LOG2_E = 1.4426950408889634
SCORE_SCALE = ATTN_SCALE * LOG2_E
LN_2 = 1.0 / LOG2_E

ADAM_LR = 0.001
ADAM_B1 = 0.9
ADAM_B2 = 0.999
ADAM_EPS = 1e-08
ADAM_WD = 0.01
ADAM_STEP = 10

N_DEV = 8
N_CHIPS = 4
LANES = 128
ROW_TILE = 384
LONG_TILE = 1056
WIDE_TILE = 1056
VMEM_LIMIT = 56 * 1024 * 1024


def _params(*sem):
    return pltpu.CompilerParams(dimension_semantics=sem, vmem_limit_bytes=VMEM_LIMIT)


def _wide_tile(rows):
    return WIDE_TILE if rows % WIDE_TILE == 0 else ROW_TILE


def _row_spec(tile, width):
    return pl.BlockSpec((tile, width), lambda i: (i, 0))


def _const_spec(shape):
    return pl.BlockSpec(shape, lambda i: tuple(0 for _ in shape))


def _layer_spec(w, layer):
    return pl.BlockSpec((None,) + w.shape[1:], lambda *_: (layer, 0, 0))


def _gathered_spec(w8, layer):
    return pl.BlockSpec((N_DEV, None) + w8.shape[2:], lambda *_: (0, layer, 0, 0))


def _nt(a, b):
    return lax.dot_general(a, b, (((1,), (1,)), ((), ())), preferred_element_type=F32)


def _tn(a, b):
    return lax.dot_general(a, b, (((0,), (0,)), ((), ())), preferred_element_type=F32)


def _nn(a, b):
    return jnp.dot(a, b, preferred_element_type=F32)


def _silu(g):
    return g * (1.0 / (1.0 + jnp.exp(-g)))


def _rms(xf):
    return lax.rsqrt(jnp.mean(xf * xf, axis=-1, keepdims=True) + RMS_EPS)


def norm_mm(x, gamma, wt, layer, splits, dtypes, name):
    L, K = x.shape
    tm = _wide_tile(L)

    def body(x_ref, g_ref, w_ref, a_ref, *z_refs):
        xf = x_ref[...]
        a = ((xf * _rms(xf)) * g_ref[...]).astype(BF16)
        a_ref[...] = a
        z = _nt(a, w_ref[...])
        for (s, n), zr in zip(splits, z_refs):
            zr[...] = z[:, s:s + n].astype(zr.dtype)

    widths = [n for _, n in splits]
    return pl.pallas_call(
        body,
        grid=(L // tm,),
        in_specs=[_row_spec(tm, K), _const_spec((1, K)), _layer_spec(wt, layer)],
        out_specs=[_row_spec(tm, K)] + [_row_spec(tm, n) for n in widths],
        out_shape=[jax.ShapeDtypeStruct((L, K), BF16)]
        + [jax.ShapeDtypeStruct((L, n), dt) for n, dt in zip(widths, dtypes)],
        compiler_params=_params("arbitrary"),
        name=name,
    )(x, gamma, wt)


def norm_mm_heads(x, gamma, w8, layer, name, transposed=False):
    L, K = x.shape
    hw = w8.shape[-2] if transposed else w8.shape[-1]
    tm = _wide_tile(L)

    def body(x_ref, g_ref, w_ref, a_ref, z_ref):
        xf = x_ref[...]
        a = ((xf * _rms(xf)) * g_ref[...]).astype(BF16)
        a_ref[...] = a
        if hw < HEAD_PAD:
            z_ref[...] = jnp.zeros_like(z_ref)
        for j in range(MLA_HEADS):
            z_ref[:, j * HEAD_PAD:j * HEAD_PAD + hw] = _nt(a, w_ref[j]) if transposed else _nn(a, w_ref[j])

    return pl.pallas_call(
        body,
        grid=(L // tm,),
        in_specs=[_row_spec(tm, K), _const_spec((1, K)), _gathered_spec(w8, layer)],
        out_specs=[_row_spec(tm, K), _row_spec(tm, MLA_HEADS * HEAD_PAD)],
        out_shape=[jax.ShapeDtypeStruct((L, K), BF16), jax.ShapeDtypeStruct((L, MLA_HEADS * HEAD_PAD), F32)],
        compiler_params=_params("arbitrary"),
        name=name,
    )(x, gamma, w8)


FF_GROUP = 4


def _ff_spec(w8, layer):
    return pl.BlockSpec((FF_GROUP, None) + w8.shape[2:], lambda j, i: (j, layer, 0, 0))


def ffn_up(g, w_gate8, w_up8, layer, name):
    L, K = g.shape
    fb = w_gate8.shape[-2]
    tm, tf = _wide_tile(L), FF_GROUP * fb

    def body(a_ref, wg_ref, wu_ref, gate_ref, up_ref, act_ref):
        a = a_ref[...]
        gate = _nt(a, wg_ref[...].reshape(tf, K))
        up = _nt(a, wu_ref[...].reshape(tf, K))
        gate_ref[...] = gate.astype(BF16)
        up_ref[...] = up.astype(BF16)
        act_ref[...] = (_silu(gate) * up).astype(BF16)

    tile = pl.BlockSpec((tm, tf), lambda j, i: (i, j))
    F = N_DEV * fb
    return pl.pallas_call(
        body,
        grid=(N_DEV // FF_GROUP, L // tm),
        in_specs=[pl.BlockSpec((tm, K), lambda j, i: (i, 0)), _ff_spec(w_gate8, layer), _ff_spec(w_up8, layer)],
        out_specs=[tile, tile, tile],
        out_shape=[
            jax.ShapeDtypeStruct((L, F), BF16),
            jax.ShapeDtypeStruct((L, F), BF16),
            jax.ShapeDtypeStruct((L, F), BF16),
        ],
        compiler_params=_params("arbitrary", "arbitrary"),
        name=name,
    )(g, w_gate8, w_up8)


def mm_res(a, w8, res, layer, name, gamma=None):
    L = a.shape[0]
    kb, N = w8.shape[-2:]
    tm = _wide_tile(L)
    normed = gamma is not None

    def body(a_ref, w_ref, r_ref, *rest):
        out = r_ref[...] + _nn(a_ref[...], w_ref[...].reshape(N_DEV * kb, N))
        if normed:
            g_ref, o_ref, n_ref = rest
            n_ref[...] = ((out * _rms(out)) * g_ref[...]).astype(BF16)
        else:
            (o_ref,) = rest
        o_ref[...] = out

    in_specs = [_row_spec(tm, N_DEV * kb), _gathered_spec(w8, layer), _row_spec(tm, N)]
    out_specs = [_row_spec(tm, N)]
    out_shape = [jax.ShapeDtypeStruct((L, N), F32)]
    args = [a, w8, res]
    if normed:
        in_specs.append(_const_spec((1, N)))
        out_specs.append(_row_spec(tm, N))
        out_shape.append(jax.ShapeDtypeStruct((L, N), BF16))
        args.append(gamma)
    return pl.pallas_call(
        body,
        grid=(L // tm,),
        in_specs=in_specs,
        out_specs=out_specs,
        out_shape=out_shape,
        compiler_params=_params("arbitrary"),
        name=name,
    )(*args)


def out_proj_bwd(dz, w8, layer, o, name, pair=()):
    L, N = dz.shape
    kb = w8.shape[-2]
    C = o.shape[1]
    tm = _wide_tile(L)
    n_p = len(pair)
    n_steps = L // tm

    def body(dz_ref, w_ref, o_ref, *rest):
        dy_ref, do_ref, dl_ref = rest[n_p:n_p + 3]
        if n_p:
            start, finish = _pair_steps(rest[:n_p], rest[n_p + 3:2 * n_p + 3], *rest[2 * n_p + 3:])
            pl.when(pl.program_id(0) == 0)(start)
        d = _nt(dz_ref[...].astype(BF16), w_ref[...].reshape(N_DEV * kb, N))
        dy_ref[...] = d[:, :C]
        do_ref[...] = d[:, C:]
        left = lax.broadcasted_iota(jnp.int32, (tm, LANES), 1) < V_HEAD_DIM
        for p in range(C // LANES):
            cols = slice(p * LANES, (p + 1) * LANES)
            prod = d[:, C + p * LANES:C + (p + 1) * LANES] * o_ref[:, cols]
            d0 = jnp.sum(jnp.where(left, prod, 0.0), axis=1, keepdims=True)
            d1 = jnp.sum(jnp.where(left, 0.0, prod), axis=1, keepdims=True)
            dl_ref[:, cols] = jnp.where(left, d0, d1)
        if n_p:
            pl.when(pl.program_id(0) == n_steps - 1)(finish)

    return pl.pallas_call(
        body,
        grid=(n_steps,),
        in_specs=[_row_spec(tm, N), _gathered_spec(w8, layer), _row_spec(tm, C)] + [_ANY] * n_p,
        out_specs=[_row_spec(tm, C)] * 3 + [_ANY] * n_p,
        out_shape=[jax.ShapeDtypeStruct((L, C), F32)] * 3
        + [jax.ShapeDtypeStruct((N_CHIPS,) + p.shape[1:], p.dtype) for p in pair],
        scratch_shapes=[pltpu.SemaphoreType.DMA((N_CHIPS * n_p,))] * 2 if n_p else [],
        compiler_params=_params("arbitrary"),
        name=name,
    )(dz, w8, o, *pair)


def mm_nt_normbwd(terms, x, gamma, dres, layer, name, wide=True):
    L, K = x.shape
    tm = _wide_tile(L) if wide else ROW_TILE
    n_terms = len(terms)
    has_res = dres is not None
    weights = []
    for t in terms:
        if not any(t[2] is u for u in weights):
            weights.append(t[2])
    which = [[t[2] is u for u in weights].index(True) for t in terms]
    n_in = n_terms + len(weights)

    def body(*refs):
        dz_refs = refs[:n_terms]
        w_refs = [refs[n_terms + n] for n in which]
        x_ref, g_ref = refs[n_in], refs[n_in + 1]
        pos = n_in + 2
        r_ref = refs[pos] if has_res else None
        dx_ref, dg_ref = refs[pos + has_res], refs[pos + has_res + 1]
        da = None
        for t, dz_ref, w_ref in zip(terms, dz_refs, w_refs):
            if t[0] == "rows":
                n, at = t[1].shape[1], t[3]
                parts = [_nn(dz_ref[...].astype(BF16), w_ref[at:at + n, :])]
            elif t[0] == "heads":
                hw = t[2].shape[-1]
                parts = [_nt(dz_ref[:, j * HEAD_PAD:j * HEAD_PAD + hw].astype(BF16), w_ref[j])
                         for j in range(MLA_HEADS)]
            elif t[0] == "heads_t":
                hw = t[2].shape[-2]
                parts = [_nn(dz_ref[:, j * HEAD_PAD:j * HEAD_PAD + hw].astype(BF16), w_ref[j])
                         for j in range(MLA_HEADS)]
            else:
                nb = t[2].shape[-2]
                parts = [_nn(dz_ref[...].astype(BF16), w_ref[...].reshape(N_DEV * nb, K))]
            for p in parts:
                da = p if da is None else da + p
        xf = x_ref[...]
        r = _rms(xf)
        xh = xf * r

        @pl.when(pl.program_id(0) == 0)
        def _():
            dg_ref[...] = jnp.zeros_like(dg_ref)

        dg_ref[...] += jnp.sum(da * xh, axis=0, keepdims=True)
        dxh = da * g_ref[...]
        dx = r * (dxh - xh * jnp.mean(dxh * xh, axis=-1, keepdims=True))
        if has_res:
            dx = dx + r_ref[...]
        dx_ref[...] = dx

    in_specs = [_row_spec(tm, t[1].shape[1]) for t in terms]
    for w in weights:
        in_specs.append(_layer_spec(w, layer) if w.ndim == 3 else _gathered_spec(w, layer))
    in_specs += [_row_spec(tm, K), _const_spec((1, K))]
    args = [t[1] for t in terms] + weights + [x, gamma]
    if has_res:
        in_specs.append(_row_spec(tm, K))
        args.append(dres)
    return pl.pallas_call(
        body,
        grid=(L // tm,),
        in_specs=in_specs,
        out_specs=[_row_spec(tm, K), _const_spec((1, K))],
        out_shape=[jax.ShapeDtypeStruct((L, K), F32), jax.ShapeDtypeStruct((1, K), F32)],
        compiler_params=_params("arbitrary"),
        name=name,
    )(*args)


MAX_OUT_ROWS = 1408


def mm_tn(a_list, b, name, out_dtype=F32):
    n = len(a_list)
    L, N = b.shape
    tks = [MAX_OUT_ROWS if (a.shape[1] > MAX_OUT_ROWS and a.shape[1] % MAX_OUT_ROWS == 0) else a.shape[1]
           for a in a_list]
    blocks = a_list[0].shape[1] // tks[0]
    assert all(a.shape[1] // tk == blocks for a, tk in zip(a_list, tks))
    tl = LONG_TILE if L % LONG_TILE == 0 else ROW_TILE
    n_l = L // tl

    def body(*refs):
        a_refs, b_ref = refs[:n], refs[n]
        o_refs, accs = refs[n + 1:2 * n + 1], refs[2 * n + 1:]
        l = pl.program_id(1)
        bt = b_ref[...].astype(BF16)
        for a_ref, o_ref, acc in zip(a_refs, o_refs, accs):
            @pl.when(l == 0)
            def _(acc=acc):
                acc[...] = jnp.zeros_like(acc)

            acc[...] += _tn(a_ref[...].astype(BF16), bt)

            @pl.when(l == n_l - 1)
            def _(acc=acc, o_ref=o_ref):
                o_ref[...] = acc[...].astype(o_ref.dtype)

    return pl.pallas_call(
        body,
        grid=(blocks, n_l),
        in_specs=[pl.BlockSpec((tl, tk), lambda j, l: (l, j)) for tk in tks]
        + [pl.BlockSpec((tl, N), lambda j, l: (l, 0))],
        out_specs=[pl.BlockSpec((tk, N), lambda j, l: (j, 0)) for tk in tks],
        out_shape=[jax.ShapeDtypeStruct((a.shape[1], N), out_dtype) for a in a_list],
        scratch_shapes=[pltpu.VMEM((tk, N), F32) for tk in tks],
        compiler_params=_params("arbitrary", "arbitrary"),
        name=name,
    )(*a_list, b)


def ffn_bwd_act(dh, w_down8, layer, gate, up, name):
    L, K = dh.shape
    fb = w_down8.shape[-2]
    tm, tf = _wide_tile(L), FF_GROUP * fb

    def body(dh_ref, w_ref, gate_ref, up_ref, dgate_ref, dup_ref):
        dact = _nt(dh_ref[...].astype(BF16), w_ref[...].reshape(tf, K))
        g = gate_ref[...].astype(F32)
        sig = 0.5 * jnp.tanh(0.5 * g) + 0.5
        dup_ref[...] = (dact * (g * sig)).astype(BF16)
        dgate_ref[...] = (dact * up_ref[...].astype(F32) * (sig * (1.0 + g * (1.0 - sig)))).astype(BF16)

    tile = pl.BlockSpec((tm, tf), lambda j, i: (i, j))
    F = N_DEV * fb
    return pl.pallas_call(
        body,
        grid=(N_DEV // FF_GROUP, L // tm),
        in_specs=[pl.BlockSpec((tm, K), lambda j, i: (i, 0)), _ff_spec(w_down8, layer), tile, tile],
        out_specs=[tile, tile],
        out_shape=[jax.ShapeDtypeStruct((L, F), BF16), jax.ShapeDtypeStruct((L, F), BF16)],
        compiler_params=_params("arbitrary", "arbitrary"),
        name=name,
    )(dh, w_down8, gate, up)


def _pool_residual(scr, lo, tm, g, w, t):
    cols = slice(g * POOL_GROUP_DIM, (g + 1) * POOL_GROUP_DIM)
    cur = scr[lo:lo + tm, cols]
    s = cur
    for k in range(1, w):
        s = s + scr[lo - k:lo - k + tm, cols]
    cnt = jnp.minimum(t + 1, w).astype(F32)
    return s / cnt - cur


def pool_fwd(u, w_pool, scale, name):
    L, C = u.shape
    tm, halo = _wide_tile(L), POOL_HALO

    def body(u_ref, halo_ref, w_ref, s_ref, y_ref, scr):
        i = pl.program_id(0)
        scr[0:halo, :] = jnp.where(i > 0, halo_ref[...], 0.0)
        scr[halo:halo + tm, :] = u_ref[...]
        t = i * tm + lax.broadcasted_iota(jnp.int32, (tm, POOL_GROUP_DIM), 0)
        for g, w in enumerate(POOL_WINDOWS):
            cols = slice(g * POOL_GROUP_DIM, (g + 1) * POOL_GROUP_DIM)
            p = _pool_residual(scr, halo, tm, g, w, t)
            y = _nn(p.astype(BF16), w_ref[g]) * s_ref[:, cols]
            y_ref[:, cols] = y.astype(y_ref.dtype)

    return pl.pallas_call(
        body,
        grid=(L // tm,),
        in_specs=[
            _row_spec(tm, C),
            pl.BlockSpec((halo, C), lambda i: (jnp.maximum(i * (tm // halo) - 1, 0), 0)),
            _const_spec(w_pool.shape),
            _const_spec((1, C)),
        ],
        out_specs=_row_spec(tm, C),
        out_shape=jax.ShapeDtypeStruct((L, 2 * C), BF16),
        scratch_shapes=[pltpu.VMEM((tm + halo, C), F32)],
        compiler_params=_params("arbitrary"),
        name=name,
    )(u, u, w_pool, scale)


def pool_bwd(u, dy, w_pool, scale, name):
    L, C = u.shape
    tm, halo = _wide_tile(L), POOL_HALO
    n_tiles = L // tm
    last_halo = L // halo - 1

    def body(u_ref, uh_ref, dy_ref, dyh_ref, w_ref, s_ref, du_ref, dw_ref, ds_ref, scr_u, scr_q):
        i = pl.program_id(0)

        @pl.when(i == 0)
        def _():
            dw_ref[...] = jnp.zeros_like(dw_ref)
            ds_ref[...] = jnp.zeros_like(ds_ref)

        scr_u[0:halo, :] = jnp.where(i > 0, uh_ref[...], 0.0)
        scr_u[halo:halo + tm, :] = u_ref[...]
        t = i * tm + lax.broadcasted_iota(jnp.int32, (tm, POOL_GROUP_DIM), 0)
        th = (i + 1) * tm + lax.broadcasted_iota(jnp.int32, (halo, POOL_GROUP_DIM), 0)
        for g, w in enumerate(POOL_WINDOWS):
            cols = slice(g * POOL_GROUP_DIM, (g + 1) * POOL_GROUP_DIM)
            p = _pool_residual(scr_u, halo, tm, g, w, t).astype(BF16)
            wg = w_ref[g]
            sc = s_ref[:, cols]
            dy = dy_ref[:, cols]
            ds_ref[:, cols] += jnp.sum(dy * _nn(p, wg), axis=0, keepdims=True)
            dys = (dy * sc).astype(BF16)
            dw_ref[g] += _tn(p, dys)
            dp = _nt(dys, wg)
            dyh = jnp.where(i < n_tiles - 1, dyh_ref[:, cols], 0.0)
            dph = _nt((dyh * sc).astype(BF16), wg)
            scr_q[0:tm, cols] = dp / jnp.minimum(t + 1, w).astype(F32)
            scr_q[tm:tm + halo, cols] = dph / jnp.minimum(th + 1, w).astype(F32)
            acc = scr_q[0:tm, cols]
            for k in range(1, w):
                acc = acc + scr_q[k:k + tm, cols]
            du_ref[:, cols] = acc - dp

    return pl.pallas_call(
        body,
        grid=(n_tiles,),
        in_specs=[
            _row_spec(tm, C),
            pl.BlockSpec((halo, C), lambda i: (jnp.maximum(i * (tm // halo) - 1, 0), 0)),
            _row_spec(tm, C),
            pl.BlockSpec((halo, C), lambda i: (jnp.minimum((i + 1) * (tm // halo), last_halo), 0)),
            _const_spec(w_pool.shape),
            _const_spec((1, C)),
        ],
        out_specs=[_row_spec(tm, C), _const_spec(w_pool.shape), _const_spec((1, C))],
        out_shape=[
            jax.ShapeDtypeStruct((L, C), F32),
            jax.ShapeDtypeStruct(w_pool.shape, F32),
            jax.ShapeDtypeStruct((1, C), F32),
        ],
        scratch_shapes=[pltpu.VMEM((tm + halo, C), F32), pltpu.VMEM((tm + halo, C), F32)],
        compiler_params=_params("arbitrary"),
        name=name,
    )(u, u, dy, dy, w_pool, scale)


def _head_masks(rows):
    lane = lax.broadcasted_iota(jnp.int32, (rows, HEAD_PAD), 1)
    return lane, lane < QK_NOPE_DIM, (lane >= QK_NOPE_DIM) & (lane < QK_HEAD_DIM)


def _rope_swap(x, lane):
    half = QK_ROPE_DIM // 2
    swapped = jnp.where(lane < QK_NOPE_DIM + half, pltpu.roll(x, HEAD_PAD - half, 1), pltpu.roll(x, half, 1))
    return jnp.where((lane >= QK_NOPE_DIM) & (lane < QK_HEAD_DIM), swapped, 0.0)


def _seg_mean(v, seg_ref):
    hi = v.astype(BF16)
    lo = (v - hi.astype(F32)).astype(BF16)
    seg = seg_ref[...]
    return _nn(hi, seg) + _nn(lo, seg)


def _segment_matrix():
    lane = jnp.arange(HEAD_PAD)
    seg = jnp.where(lane < QK_NOPE_DIM, 0, jnp.where(lane < QK_HEAD_DIM, 1, 2))
    inv = jnp.where(lane < QK_NOPE_DIM, 1.0 / QK_NOPE_DIM, jnp.where(lane < QK_HEAD_DIM, 1.0 / QK_ROPE_DIM, 0.0))
    return jnp.where(seg[:, None] == seg[None, :], inv[None, :], 0.0).astype(BF16)


def qk_prep_fwd(q, kv, kr, cosf, sins, seg, gq, gkn, gkr, name):
    L = q.shape[0]
    tm = _wide_tile(L)
    W = MLA_HEADS * HEAD_PAD

    def body(q_ref, kv_ref, kr_ref, c_ref, s_ref, seg_ref, gq_ref, gkn_ref, gkr_ref, qo_ref, ko_ref, vo_ref):
        lane, m_n, _ = _head_masks(tm)
        cosf_, sins_ = c_ref[...], s_ref[...]
        kr_ = kr_ref[...]
        rk = lax.rsqrt(_seg_mean(kr_ * kr_, seg_ref) + RMS_EPS)
        krn = kr_ * rk * gkr_ref[...]
        krf = krn * cosf_ + _rope_swap(krn, lane) * sins_
        for h in range(MLA_HEADS):
            cols = slice(h * HEAD_PAD, (h + 1) * HEAD_PAD)
            qh = q_ref[:, cols]
            qn = qh * lax.rsqrt(_seg_mean(qh * qh, seg_ref) + RMS_EPS) * gq_ref[...]
            qo_ref[:, cols] = (qn * cosf_ + _rope_swap(qn, lane) * sins_).astype(BF16)
            kh = jnp.where(m_n, kv_ref[:, cols], 0.0)
            rn = lax.rsqrt(_seg_mean(kh * kh, seg_ref) + RMS_EPS)
            ko_ref[:, cols] = (kh * rn * gkn_ref[...] + krf).astype(BF16)
        for p in range(MLA_HEADS // 2):
            even = kv_ref[:, 2 * p * HEAD_PAD:(2 * p + 1) * HEAD_PAD]
            odd = kv_ref[:, (2 * p + 1) * HEAD_PAD:(2 * p + 2) * HEAD_PAD]
            pair = jnp.where(m_n, pltpu.roll(even, V_HEAD_DIM, 1), odd)
            vo_ref[:, p * LANES:(p + 1) * LANES] = pair.astype(BF16)

    vec = _const_spec((1, HEAD_PAD))
    return pl.pallas_call(
        body,
        grid=(L // tm,),
        in_specs=[_row_spec(tm, W), _row_spec(tm, W), _row_spec(tm, HEAD_PAD), _row_spec(tm, HEAD_PAD),
                  _row_spec(tm, HEAD_PAD), _const_spec((HEAD_PAD, HEAD_PAD)), vec, vec, vec],
        out_specs=[_row_spec(tm, W), _row_spec(tm, W), _row_spec(tm, W // 2)],
        out_shape=[jax.ShapeDtypeStruct((L, W), BF16), jax.ShapeDtypeStruct((L, W), BF16),
                   jax.ShapeDtypeStruct((L, W // 2), BF16)],
        compiler_params=_params("arbitrary"),
        name=name,
    )(q, kv, kr, cosf, sins, seg, gq, gkn, gkr)


def qk_prep_bwd(dqo, dko, dv, q, kv, kr, cosf, sins, seg, gq, gkn, gkr, name):
    L = q.shape[0]
    tm = ROW_TILE
    W = MLA_HEADS * HEAD_PAD

    def body(dqo_ref, dko_ref, dv_ref, q_ref, kv_ref, kr_ref, c_ref, s_ref, seg_ref, gq_ref, gkn_ref, gkr_ref,
             dq_ref, dkv_ref, dkr_ref, dgq_ref, dgkn_ref, dgkr_ref):
        @pl.when(pl.program_id(0) == 0)
        def _():
            dgq_ref[...] = jnp.zeros_like(dgq_ref)
            dgkn_ref[...] = jnp.zeros_like(dgkn_ref)
            dgkr_ref[...] = jnp.zeros_like(dgkr_ref)

        lane, m_n, _ = _head_masks(tm)
        cosf_, sins_ = c_ref[...], s_ref[...]
        dgq = jnp.zeros((1, HEAD_PAD), F32)
        dgkn = jnp.zeros((1, HEAD_PAD), F32)
        dkrf = jnp.zeros((tm, HEAD_PAD), F32)
        for h in range(MLA_HEADS):
            cols = slice(h * HEAD_PAD, (h + 1) * HEAD_PAD)
            dy = dqo_ref[:, cols]
            dqn = dy * cosf_ + _rope_swap(dy * sins_, lane)
            qh = q_ref[:, cols]
            rinv = lax.rsqrt(_seg_mean(qh * qh, seg_ref) + RMS_EPS)
            xh = qh * rinv
            dgq = dgq + jnp.sum(dqn * xh, axis=0, keepdims=True)
            dxh = dqn * gq_ref[...]
            dq_ref[:, cols] = rinv * (dxh - xh * _seg_mean(dxh * xh, seg_ref))
            dk = dko_ref[:, cols]
            dkrf = dkrf + dk
            kh = jnp.where(m_n, kv_ref[:, cols], 0.0)
            rn = lax.rsqrt(_seg_mean(kh * kh, seg_ref) + RMS_EPS)
            xk = kh * rn
            dgkn = dgkn + jnp.sum(dk * xk, axis=0, keepdims=True)
            dxk = dk * gkn_ref[...]
            dkn = rn * (dxk - xk * _seg_mean(dxk * xk, seg_ref))
            dvp = dv_ref[:, (h // 2) * LANES:(h // 2 + 1) * LANES]
            dvh = pltpu.roll(dvp, V_HEAD_DIM, 1) if h % 2 == 0 else dvp
            dkv_ref[:, cols] = jnp.where(m_n, dkn, dvh)
        kr_ = kr_ref[...]
        rk = lax.rsqrt(_seg_mean(kr_ * kr_, seg_ref) + RMS_EPS)
        xr = kr_ * rk
        dkrn = dkrf * cosf_ + _rope_swap(dkrf * sins_, lane)
        dgkr_ref[...] += jnp.sum(dkrn * xr, axis=0, keepdims=True)
        dxr = dkrn * gkr_ref[...]
        dkr_ref[...] = rk * (dxr - xr * _seg_mean(dxr * xr, seg_ref))
        dgq_ref[...] += dgq
        dgkn_ref[...] += dgkn

    vec = _const_spec((1, HEAD_PAD))
    return pl.pallas_call(
        body,
        grid=(L // tm,),
        in_specs=[_row_spec(tm, W), _row_spec(tm, W), _row_spec(tm, W // 2), _row_spec(tm, W), _row_spec(tm, W),
                  _row_spec(tm, HEAD_PAD), _row_spec(tm, HEAD_PAD), _row_spec(tm, HEAD_PAD),
                  _const_spec((HEAD_PAD, HEAD_PAD)), vec, vec, vec],
        out_specs=[_row_spec(tm, W), _row_spec(tm, W), _row_spec(tm, HEAD_PAD), vec, vec, vec],
        out_shape=[jax.ShapeDtypeStruct((L, W), F32), jax.ShapeDtypeStruct((L, W), F32),
                   jax.ShapeDtypeStruct((L, HEAD_PAD), F32)] + [jax.ShapeDtypeStruct((1, HEAD_PAD), F32)] * 3,
        compiler_params=_params("arbitrary"),
        name=name,
    )(dqo, dko, dv, q, kv, kr, cosf, sins, seg, gq, gkn, gkr)


def attn_fwd(qp, kp, v, cat, name, gather=()):
    L = qp.shape[0]
    tq = ROW_TILE
    n_i = L // tq
    pair_w = 2 * HEAD_PAD
    n_pairs = MLA_HEADS // 2
    n_g = len(gather)

    def body(q_ref, k_ref, v_ref, cat_in, *rest):
        del cat_in
        o_ref, lse_ref, cat_ref = rest[n_g:n_g + 3]
        i = pl.program_id(1)
        if n_g:
            pair = pl.program_id(0)
            start, forward, finish = _gather_steps(rest[:n_g], rest[n_g + 3:2 * n_g + 3], *rest[2 * n_g + 3:])
            pl.when((pair == 0) & (i == 0))(start)
            pl.when((pair == n_pairs // 2) & (i == 0))(forward)
        left = lax.broadcasted_iota(jnp.int32, (tq, LANES), 1) < V_HEAD_DIM

        def step(t, carry, masked, width=1):
            rows = pl.ds(pl.multiple_of(t * (tq * width), tq), tq * width)
            vv = v_ref[rows, :]
            out = []
            for hh in range(2):
                cols = slice(hh * HEAD_PAD, (hh + 1) * HEAD_PAD)
                m, l, acc = carry[hh]
                s = _nt(q_ref[:, cols], k_ref[rows, cols])
                if masked:
                    row = lax.broadcasted_iota(jnp.int32, s.shape, 0)
                    col = lax.broadcasted_iota(jnp.int32, s.shape, 1)
                    s = jnp.where(col <= row, s, -jnp.inf)
                m_new = jnp.maximum(m, jnp.max(s, axis=1, keepdims=True))
                alpha = jnp.exp2(m - m_new)
                p = jnp.exp2(s - m_new)
                l = alpha * l + jnp.sum(p, axis=1, keepdims=True)
                acc = alpha * acc + _nn(p.astype(BF16), vv)
                out.append((m_new, l, acc))
            return tuple(out)

        one = (jnp.full((tq, 1), -jnp.inf, F32), jnp.zeros((tq, 1), F32), jnp.zeros((tq, LANES), F32))
        doubles = jnp.right_shift(i, 1)
        carry = lax.fori_loop(0, doubles, functools.partial(step, masked=False, width=2), (one, one))
        carry = lax.fori_loop(2 * doubles, i, functools.partial(step, masked=False), carry)
        (m0, l0, a0), (m1, l1, a1) = step(i, carry, True)
        o = jnp.where(left, a0 / l0, a1 / l1)
        o_ref[...] = o
        cat_ref[...] = o.astype(BF16)
        lse_ref[...] = jnp.where(left, m0 + jnp.log2(l0), m1 + jnp.log2(l1))
        if n_g:
            pl.when((pair == n_pairs - 1) & (i == n_i - 1))(finish)

    return pl.pallas_call(
        body,
        grid=(n_pairs, n_i),
        in_specs=[
            pl.BlockSpec((tq, pair_w), lambda p, i: (i, p)),
            pl.BlockSpec((L, pair_w), lambda p, i: (0, p)),
            pl.BlockSpec((L, LANES), lambda p, i: (0, p)),
            _ANY,
        ] + [_ANY] * n_g,
        out_specs=[
            pl.BlockSpec((tq, LANES), lambda p, i: (i, p)),
            pl.BlockSpec((tq, LANES), lambda p, i: (i, p)),
            pl.BlockSpec((tq, LANES), lambda p, i: (i, n_pairs + p)),
        ] + [_ANY] * n_g,
        out_shape=[jax.ShapeDtypeStruct((L, n_pairs * LANES), F32), jax.ShapeDtypeStruct((L, n_pairs * LANES), F32),
                   jax.ShapeDtypeStruct(cat.shape, cat.dtype)]
        + [jax.ShapeDtypeStruct((N_DEV,) + g.shape, g.dtype) for g in gather],
        scratch_shapes=_gather_scratch(n_g) if n_g else [],
        input_output_aliases={3: 2},
        compiler_params=_params("arbitrary", "arbitrary"),
        name=name,
    )(qp, kp, v, cat, *gather)


def attn_bwd(qp, kp, v, do, lse, delta, name, exchange=(), gather=()):
    L = qp.shape[0]
    tq = ROW_TILE
    n_q = L // tq
    pair_w = 2 * HEAD_PAD
    n_pairs = MLA_HEADS // 2
    n_x, n_g = len(exchange), len(gather)
    n_r = n_x + n_g

    def body(q_ref, k_ref, v_ref, do_ref, lse_ref, dl_ref, *rest):
        dq_ref, dk_ref, dv_ref = rest[n_r:n_r + 3]
        riders_in, riders_out, sems = rest[:n_r], rest[n_r + 3:2 * n_r + 3], rest[2 * n_r + 3:]
        j = pl.program_id(1)
        pair = pl.program_id(0)
        steps = []
        if n_x:
            steps.append(_chip_exchange_steps(riders_in[:n_x], riders_out[:n_x], *sems[:3]))
        if n_g:
            steps.append(_direct_gather_steps(riders_in[n_x:], riders_out[n_x:], *sems[-3:]))
        for start, _ in steps:
            pl.when((pair == 0) & (j == 0))(start)

        @pl.when(j == 0)
        def _():
            dq_ref[...] = jnp.zeros_like(dq_ref)

        dk_ref[...] = jnp.zeros_like(dk_ref)
        dv_ref[...] = jnp.zeros_like(dv_ref)

        def step(t, carry, masked, width=1, first=0):
            rows = pl.ds(pl.multiple_of((first + t * width) * tq, tq), tq * width)
            do = do_ref[rows, :]
            left = lax.broadcasted_iota(jnp.int32, do.shape, 1) < V_HEAD_DIM
            vv = v_ref[...]
            dv = None
            for hh in range(2):
                cols = slice(hh * HEAD_PAD, (hh + 1) * HEAD_PAD)
                stat = slice(hh * V_HEAD_DIM, hh * V_HEAD_DIM + 1)
                q = q_ref[rows, cols]
                k = k_ref[:, cols]
                dom = jnp.where(left if hh == 0 else jnp.logical_not(left), do, 0.0).astype(BF16)
                s = _nt(q, k)
                p = jnp.exp2(s - lse_ref[rows, stat])
                if masked:
                    row = lax.broadcasted_iota(jnp.int32, p.shape, 0)
                    col = lax.broadcasted_iota(jnp.int32, p.shape, 1)
                    p = jnp.where(col <= row, p, 0.0)
                dp = _nt(dom, vv)
                ds = (p * (dp - dl_ref[rows, stat])).astype(BF16)
                dq_ref[rows, cols] += _nn(ds, k)
                dk_ref[:, cols] += _tn(ds, q)
                t = _tn(p.astype(BF16), dom)
                dv = t if dv is None else dv + t
            dv_ref[...] += dv
            return carry

        @pl.when(j < n_q - 1)
        def _():
            step(0, 0, True, width=2, first=j)
            doubles = jnp.right_shift(n_q - 2 - j, 1)
            lax.fori_loop(0, doubles, functools.partial(step, masked=False, width=2, first=j + 2), 0)
            lax.fori_loop(j + 2 + 2 * doubles, n_q, functools.partial(step, masked=False), 0)

        @pl.when(j == n_q - 1)
        def _():
            step(j, 0, True)

        dk_ref[...] *= LN_2

        @pl.when(j == n_q - 1)
        def _():
            dq_ref[...] *= LN_2

        for _, finish in steps:
            pl.when((pair == n_pairs - 1) & (j == n_q - 1))(finish)

    return pl.pallas_call(
        body,
        grid=(n_pairs, n_q),
        in_specs=[
            pl.BlockSpec((L, pair_w), lambda p, j: (0, p)),
            pl.BlockSpec((tq, pair_w), lambda p, j: (j, p)),
            pl.BlockSpec((tq, LANES), lambda p, j: (j, p)),
            pl.BlockSpec((L, LANES), lambda p, j: (0, p)),
            pl.BlockSpec((L, LANES), lambda p, j: (0, p)),
            pl.BlockSpec((L, LANES), lambda p, j: (0, p)),
        ] + [_ANY] * n_r,
        out_specs=[
            pl.BlockSpec((L, pair_w), lambda p, j: (0, p)),
            pl.BlockSpec((tq, pair_w), lambda p, j: (j, p)),
            pl.BlockSpec((tq, LANES), lambda p, j: (j, p)),
        ] + [_ANY] * n_r,
        out_shape=[
            jax.ShapeDtypeStruct((L, n_pairs * pair_w), F32),
            jax.ShapeDtypeStruct((L, n_pairs * pair_w), F32),
            jax.ShapeDtypeStruct((L, n_pairs * LANES), F32),
        ] + [jax.ShapeDtypeStruct(e.shape, e.dtype) for e in exchange]
        + [jax.ShapeDtypeStruct((N_DEV,) + g.shape, g.dtype) for g in gather],
        scratch_shapes=(_chip_exchange_scratch(n_x) if n_x else []) + (_direct_gather_scratch(n_g) if n_g else []),
        compiler_params=_params("arbitrary", "arbitrary"),
        name=name,
    )(qp, kp, v, do, lse, delta, *exchange, *gather)


def loss_head(h, target, n_real, name):
    L, D = h.shape
    tm = _wide_tile(L)

    def body(h_ref, t_ref, dh_ref, sq_ref):
        i = pl.program_id(0)

        @pl.when(i == 0)
        def _():
            sq_ref[...] = jnp.zeros_like(sq_ref)

        t = i * tm + lax.broadcasted_iota(jnp.int32, (tm, D), 0)
        real = (t >= N_META) & (t < N_META + n_real)
        diff = jnp.where(real, h_ref[...] - t_ref[...], 0.0)
        dh_ref[...] = diff * (1.0 / D)
        sq_ref[...] += jnp.sum(diff * diff, axis=0, keepdims=True)

    return pl.pallas_call(
        body,
        grid=(L // tm,),
        in_specs=[_row_spec(tm, D), _row_spec(tm, D)],
        out_specs=[_row_spec(tm, D), _const_spec((1, D))],
        out_shape=[jax.ShapeDtypeStruct((L, D), F32), jax.ShapeDtypeStruct((1, D), F32)],
        compiler_params=_params("arbitrary"),
        name=name,
    )(h, target)


def _mesh_position():
    x, y, c = lax.axis_index("x"), lax.axis_index("y"), lax.axis_index("c")
    return x, y, c, 4 * x + 2 * y + c


def _flip(x, y, c, k):
    px = 1 - x if k & 4 else x
    py = 1 - y if k & 2 else y
    pc = 1 - c if k & 1 else c
    return (px, py, pc), 4 * px + 2 * py + pc


def _other_chips(x, y):
    return [(1 - x, y), (x, 1 - y), (1 - x, 1 - y)]


def _dev_index(px, py, pc):
    return 4 * px + 2 * py + pc


_ANY = pl.BlockSpec(memory_space=pl.ANY)


def cast_bf16(arrays, name):
    n = len(arrays)
    depth = arrays[0].shape[0]

    def body(*refs):
        for k, src in enumerate(refs[:n]):
            for l in range(depth):
                refs[n + k * depth + l][0] = src[l].astype(BF16)

    return pl.pallas_call(
        body,
        out_shape=[jax.ShapeDtypeStruct((1,) + a.shape[1:], BF16) for a in arrays for _ in range(depth)],
        compiler_params=pltpu.CompilerParams(vmem_limit_bytes=VMEM_LIMIT),
        name=name,
    )(*arrays)


def _gather_steps(srcs, dsts, send_sems, recv_sems, local_sems):
    n = len(srcs)
    x, y, c, me = _mesh_position()
    sibling = (x, y, 1 - c)
    chips = _other_chips(x, y)

    def copy(k, b, block, to, from_src=False):
        dst = dsts[b].at[block]
        return pltpu.make_async_remote_copy(
            src_ref=srcs[b] if from_src else dst, dst_ref=dst, send_sem=send_sems.at[k * n + b],
            recv_sem=recv_sems.at[k * n + b], device_id=to, device_id_type=pl.DeviceIdType.MESH)

    def own():
        return [pltpu.make_async_copy(srcs[b], dsts[b].at[me], local_sems.at[b]) for b in range(n)]

    def first():
        out = []
        for b in range(n):
            out.append(copy(0, b, me, sibling, from_src=True))
            out += [copy(1 + q, b, me, (*chip, c), from_src=True) for q, chip in enumerate(chips)]
        return out

    def passed():
        return [copy(4 + q, b, _dev_index(*chip, c), sibling) for q, chip in enumerate(chips) for b in range(n)]

    def start():
        for cp in own() + first():
            cp.start()

    def forward():
        for q, chip in enumerate(chips):
            for b in range(n):
                copy(1 + q, b, _dev_index(*chip, c), sibling).wait_recv()
        for cp in passed():
            cp.start()

    def finish():
        for b in range(n):
            copy(0, b, _dev_index(x, y, 1 - c), sibling).wait_recv()
            for q, chip in enumerate(chips):
                copy(4 + q, b, _dev_index(*chip, 1 - c), sibling).wait_recv()
        for cp in first() + passed():
            cp.wait_send()
        for cp in own():
            cp.wait()

    return start, forward, finish


def _gather_scratch(n):
    copies = N_DEV - 1
    return [pltpu.SemaphoreType.DMA((copies * n,)), pltpu.SemaphoreType.DMA((copies * n,)),
            pltpu.SemaphoreType.DMA((n,))]


def all_gather(payloads, name):
    n = len(payloads)

    def body(*refs):
        start, forward, finish = _gather_steps(refs[:n], refs[n:2 * n], *refs[2 * n:])
        start()
        forward()
        finish()

    return pl.pallas_call(
        body,
        in_specs=[_ANY] * n,
        out_specs=[_ANY] * n,
        out_shape=[jax.ShapeDtypeStruct((N_DEV,) + p.shape, p.dtype) for p in payloads],
        scratch_shapes=_gather_scratch(n),
        name=name,
    )(*payloads)


def _pair_steps(big_refs, sib_refs, send_sems, recv_sems):
    n = len(big_refs)
    x, y, c, me = _mesh_position()

    def copies():
        return [pltpu.make_async_remote_copy(
            src_ref=big_refs[b].at[_dev_index(q // 2, q % 2, 1 - c)], dst_ref=sib_refs[b].at[q],
            send_sem=send_sems.at[b * N_CHIPS + q], recv_sem=recv_sems.at[b * N_CHIPS + q],
            device_id=(x, y, 1 - c), device_id_type=pl.DeviceIdType.MESH) for b in range(n) for q in range(N_CHIPS)]

    def start():
        for cp in copies():
            cp.start()

    def finish():
        for cp in copies():
            cp.wait_recv()
        for cp in copies():
            cp.wait_send()

    return start, finish


def exchange_pair(bigs, small, name):
    n = len(bigs)
    has_small = small is not None
    n_big = N_CHIPS * n
    n_sems = n_big + (N_DEV - 1 if has_small else 0)

    def body(*refs):
        big_refs = refs[:n]
        pos = n + has_small
        sib_refs = refs[pos:pos + n]
        send_sems, recv_sems, local_sem = refs[-3:]
        x, y, c, me = _mesh_position()
        sibling = (x, y, 1 - c)
        copies = []
        for b in range(n):
            for q in range(N_CHIPS):
                copies.append(pltpu.make_async_remote_copy(
                    src_ref=big_refs[b].at[_dev_index(q // 2, q % 2, 1 - c)], dst_ref=sib_refs[b].at[q],
                    send_sem=send_sems.at[b * N_CHIPS + q], recv_sem=recv_sems.at[b * N_CHIPS + q],
                    device_id=sibling, device_id_type=pl.DeviceIdType.MESH))
        waits = list(copies)
        if has_small:
            small_ref, gsmall_ref = refs[n], refs[pos + n]
            own = pltpu.make_async_copy(small_ref.at[me], gsmall_ref.at[me], local_sem.at[0])
            own.start()
            for k in range(1, N_DEV):
                peer, peer_idx = _flip(x, y, c, k)
                s = n_big + k - 1
                copies.append(pltpu.make_async_remote_copy(
                    src_ref=small_ref.at[peer_idx], dst_ref=gsmall_ref.at[me], send_sem=send_sems.at[s],
                    recv_sem=recv_sems.at[s], device_id=peer, device_id_type=pl.DeviceIdType.MESH))
                waits.append(pltpu.make_async_remote_copy(
                    src_ref=small_ref.at[peer_idx], dst_ref=gsmall_ref.at[peer_idx], send_sem=send_sems.at[s],
                    recv_sem=recv_sems.at[s], device_id=peer, device_id_type=pl.DeviceIdType.MESH))
        for cp in copies:
            cp.start()
        for cp in waits:
            cp.wait_recv()
        for cp in copies:
            cp.wait_send()
        if has_small:
            own.wait()

    out_shape = [jax.ShapeDtypeStruct((N_CHIPS,) + b.shape[1:], b.dtype) for b in bigs]
    args = list(bigs)
    if has_small:
        out_shape.append(jax.ShapeDtypeStruct(small.shape, small.dtype))
        args.append(small)
    return pl.pallas_call(
        body,
        in_specs=[_ANY] * len(args),
        out_specs=[_ANY] * len(out_shape),
        out_shape=out_shape,
        scratch_shapes=[pltpu.SemaphoreType.DMA((n_sems,)), pltpu.SemaphoreType.DMA((n_sems,)),
                        pltpu.SemaphoreType.DMA((1,))],
        name=name,
    )(*args)


def pair_sum(bigs, from_sibling, core, name):
    n = len(bigs)

    def body(core_ref, *refs):
        del core_ref
        for mine, sib, out in zip(refs[:n], refs[n:2 * n], refs[2 * n:]):
            out[...] = (mine[...].astype(F32) + sib[...].astype(F32)).astype(out.dtype)

    def slot(shape, picked):
        zeros = (0,) * (len(shape) - 1)
        if picked:
            return pl.BlockSpec((None,) + shape[1:], lambda q, core_ref: (2 * q + core_ref[0],) + zeros)
        return pl.BlockSpec((None,) + shape[1:], lambda q, core_ref: (q,) + zeros)

    grid_spec = pltpu.PrefetchScalarGridSpec(
        num_scalar_prefetch=1,
        grid=(N_CHIPS,),
        in_specs=[slot(b.shape, True) for b in bigs] + [slot(s.shape, False) for s in from_sibling],
        out_specs=[slot(s.shape, False) for s in from_sibling],
    )
    return pl.pallas_call(
        body,
        grid_spec=grid_spec,
        out_shape=[jax.ShapeDtypeStruct(s.shape, s.dtype) for s in from_sibling],
        compiler_params=_params("arbitrary"),
        name=name,
    )(core, *bigs, *from_sibling)


def _chip_exchange_steps(part_refs, got_refs, send_sems, recv_sems, local_sems):
    n = len(part_refs)
    x, y, c, me = _mesh_position()
    mine = 2 * x + y

    def copies(landing):
        out = []
        for q, (px, py) in enumerate(_other_chips(x, y)):
            theirs = 2 * px + py
            for b in range(n):
                out.append(pltpu.make_async_remote_copy(
                    src_ref=part_refs[b].at[theirs], dst_ref=got_refs[b].at[theirs if landing else mine],
                    send_sem=send_sems.at[q * n + b], recv_sem=recv_sems.at[q * n + b],
                    device_id=(px, py, c), device_id_type=pl.DeviceIdType.MESH))
        return out

    def own():
        return [pltpu.make_async_copy(part_refs[b].at[mine], got_refs[b].at[mine], local_sems.at[b])
                for b in range(n)]

    def start():
        for cp in own() + copies(False):
            cp.start()

    def finish():
        for cp in copies(True):
            cp.wait_recv()
        for cp in copies(False):
            cp.wait_send()
        for cp in own():
            cp.wait()

    return start, finish


def _chip_exchange_scratch(n):
    return [pltpu.SemaphoreType.DMA(((N_CHIPS - 1) * n,)), pltpu.SemaphoreType.DMA(((N_CHIPS - 1) * n,)),
            pltpu.SemaphoreType.DMA((n,))]


def _direct_gather_steps(srcs, dsts, send_sems, recv_sems, local_sems):
    n = len(srcs)
    x, y, c, me = _mesh_position()

    def copies(landing):
        out = []
        for k in range(1, N_DEV):
            peer, peer_idx = _flip(x, y, c, k)
            for b in range(n):
                out.append(pltpu.make_async_remote_copy(
                    src_ref=srcs[b], dst_ref=dsts[b].at[peer_idx if landing else me],
                    send_sem=send_sems.at[(k - 1) * n + b], recv_sem=recv_sems.at[(k - 1) * n + b],
                    device_id=peer, device_id_type=pl.DeviceIdType.MESH))
        return out

    def own():
        return [pltpu.make_async_copy(srcs[b], dsts[b].at[me], local_sems.at[b]) for b in range(n)]

    def start():
        for cp in own() + copies(False):
            cp.start()

    def finish():
        for cp in copies(True):
            cp.wait_recv()
        for cp in copies(False):
            cp.wait_send()
        for cp in own():
            cp.wait()

    return start, finish


def _direct_gather_scratch(n):
    return [pltpu.SemaphoreType.DMA(((N_DEV - 1) * n,)), pltpu.SemaphoreType.DMA(((N_DEV - 1) * n,)),
            pltpu.SemaphoreType.DMA((n,))]


def exchange_chips(partials, name):
    n = len(partials)

    def body(*refs):
        start, finish = _chip_exchange_steps(refs[:n], refs[n:2 * n], *refs[2 * n:])
        start()
        finish()

    return pl.pallas_call(
        body,
        in_specs=[_ANY] * n,
        out_specs=[_ANY] * n,
        out_shape=[jax.ShapeDtypeStruct(p.shape, p.dtype) for p in partials],
        scratch_shapes=_chip_exchange_scratch(n),
        name=name,
    )(*partials)


def _adamw_math(g, w, m, v):
    m_new = ADAM_B1 * m + (1.0 - ADAM_B1) * g
    v_new = ADAM_B2 * v + (1.0 - ADAM_B2) * (g * g)
    m_hat = m_new / (1.0 - ADAM_B1 ** ADAM_STEP)
    v_hat = v_new / (1.0 - ADAM_B2 ** ADAM_STEP)
    delta = -ADAM_LR * (m_hat / (jnp.sqrt(v_hat) + ADAM_EPS) + ADAM_WD * w)
    return delta, m_new, v_new


def _chip_total(p_ref):
    g = p_ref[0].astype(F32)
    for q in range(1, N_CHIPS):
        g = g + p_ref[q].astype(F32)
    return g


def chip_sum(parts, name):
    depth = len(parts)

    def body(*refs):
        for l in range(depth):
            refs[depth][l] = _chip_total(refs[l])

    return pl.pallas_call(
        body,
        out_shape=jax.ShapeDtypeStruct((depth,) + parts[0].shape[1:], F32),
        compiler_params=pltpu.CompilerParams(vmem_limit_bytes=VMEM_LIMIT),
        name=name,
    )(*parts)


def adamw_shard(parts, w, m, v, name, grad=None):
    depth = w.shape[0]
    n_in = 1 if grad is not None else depth

    def body(*refs):
        w_ref, m_ref, v_ref, g_ref, d_ref, mo_ref, vo_ref = refs[n_in:]

        def update(g):
            delta, m_new, v_new = _adamw_math(g, w_ref[...], m_ref[...], v_ref[...])
            g_ref[...] = g
            d_ref[...] = delta
            mo_ref[...] = m_new
            vo_ref[...] = v_new

        if grad is not None:
            update(refs[0][...])
        else:
            for l in range(depth):
                pl.when(pl.program_id(0) == l)(functools.partial(lambda k: update(_chip_total(refs[k])), l))

    layer = pl.BlockSpec((None,) + w.shape[1:], lambda l: (l, 0, 0))
    whole = [pl.BlockSpec(p.shape, lambda l: (0, 0, 0)) for p in (parts or [])]
    return pl.pallas_call(
        body,
        grid=(depth,),
        in_specs=([layer] if grad is not None else whole) + [layer] * 3,
        out_specs=[layer] * 4,
        out_shape=[jax.ShapeDtypeStruct(w.shape, F32)] * 4,
        compiler_params=_params("arbitrary"),
        name=name,
    )(*([grad] if grad is not None else parts), w, m, v)


def adamw_packed(parts, w, m, v, name):
    n_slots = parts.shape[0]

    def body(p_ref, w_ref, m_ref, v_ref, g_ref, d_ref, mo_ref, vo_ref):
        g = p_ref[0]
        for s in range(1, n_slots):
            g = g + p_ref[s]
        delta, m_new, v_new = _adamw_math(g, w_ref[...], m_ref[...], v_ref[...])
        g_ref[...] = g
        d_ref[...] = delta
        mo_ref[...] = m_new
        vo_ref[...] = v_new

    return pl.pallas_call(
        body,
        out_shape=[jax.ShapeDtypeStruct(w.shape, F32)] * 4,
        compiler_params=pltpu.CompilerParams(vmem_limit_bytes=VMEM_LIMIT),
        name=name,
    )(parts, w, m, v)


SHARDED = ("w_in", "w_q_b", "w_kv_b", "w_out", "w_gate", "w_up", "w_down")
TRANSPOSED = ("w_in", "w_q_b", "w_gate", "w_up")
ADAM_TRANSPOSED = ("w_q_b", "w_gate", "w_up")
BEFORE_ATTENTION = ("w_in", "w_q_b", "w_kv_b")
BEFORE_ATTENTION_BWD = ("w_out", "w_gate", "w_up", "w_down")
REPLICATED = ("attn_norm_g", "pool_scale", "q_a_norm_g", "kv_a_norm_g", "q_norm_g", "k_norm_g", "ffn_norm_g")


def _pack_flat(arrays, rows):
    flat = jnp.concatenate([a.reshape(-1) for a in arrays])
    return jnp.pad(flat, (0, rows * LANES - flat.shape[0])).reshape(rows, LANES)


def _unpack_flat(packed, shapes):
    flat, out, at = packed.reshape(-1), [], 0
    for shp in shapes:
        n = 1
        for d in shp:
            n *= d
        out.append(flat[at:at + n].reshape(shp))
        at += n
    return out


def _rope_lane_tables(length):
    inv = 1.0 / (ROPE_THETA ** (jnp.arange(0, QK_ROPE_DIM, 2, dtype=F32) / QK_ROPE_DIM))
    ang = jnp.arange(length, dtype=F32)[:, None] * inv[None, :]
    cos, sin = jnp.cos(ang), jnp.sin(ang)
    ones = jnp.ones((length, QK_NOPE_DIM), F32)
    zeros = jnp.zeros((length, QK_NOPE_DIM), F32)
    tail = HEAD_PAD - QK_HEAD_DIM
    cosf = jnp.concatenate([ones, cos, cos, ones[:, :tail]], axis=1)
    sins = jnp.concatenate([zeros, -sin, sin, zeros[:, :tail]], axis=1)
    return cosf, sins


def _pad_lanes(vec, at, width=HEAD_PAD):
    return jnp.pad(vec, (at, width - at - vec.shape[0])).reshape(1, width)


def kernel(x, meta_tokens, attn_norm_g, w_in, w_pool, pool_scale, q_a_norm_g, w_q_b, kv_a_norm_g, w_kv_b, q_norm_g, k_norm_g, w_out, ffn_norm_g, w_gate, w_up, w_down, loss_target, m_meta_tokens, m_attn_norm_g, m_w_in, m_w_pool, m_pool_scale, m_q_a_norm_g, m_w_q_b, m_kv_a_norm_g, m_w_kv_b, m_q_norm_g, m_k_norm_g, m_w_out, m_ffn_norm_g, m_w_gate, m_w_up, m_w_down, v_meta_tokens, v_attn_norm_g, v_w_in, v_w_pool, v_pool_scale, v_q_a_norm_g, v_w_q_b, v_kv_a_norm_g, v_w_kv_b, v_q_norm_g, v_k_norm_g, v_w_out, v_ffn_norm_g, v_w_gate, v_w_up, v_w_down):
    weights = dict(meta_tokens=meta_tokens, attn_norm_g=attn_norm_g, w_in=w_in, w_pool=w_pool, pool_scale=pool_scale,
                   q_a_norm_g=q_a_norm_g, w_q_b=w_q_b, kv_a_norm_g=kv_a_norm_g, w_kv_b=w_kv_b, q_norm_g=q_norm_g,
                   k_norm_g=k_norm_g, w_out=w_out, ffn_norm_g=ffn_norm_g, w_gate=w_gate, w_up=w_up, w_down=w_down)
    mom1 = dict(meta_tokens=m_meta_tokens, attn_norm_g=m_attn_norm_g, w_in=m_w_in, w_pool=m_w_pool,
                pool_scale=m_pool_scale, q_a_norm_g=m_q_a_norm_g, w_q_b=m_w_q_b, kv_a_norm_g=m_kv_a_norm_g,
                w_kv_b=m_w_kv_b, q_norm_g=m_q_norm_g, k_norm_g=m_k_norm_g, w_out=m_w_out, ffn_norm_g=m_ffn_norm_g,
                w_gate=m_w_gate, w_up=m_w_up, w_down=m_w_down)
    mom2 = dict(meta_tokens=v_meta_tokens, attn_norm_g=v_attn_norm_g, w_in=v_w_in, w_pool=v_w_pool,
                pool_scale=v_pool_scale, q_a_norm_g=v_q_a_norm_g, w_q_b=v_w_q_b, kv_a_norm_g=v_kv_a_norm_g,
                w_kv_b=v_w_kv_b, q_norm_g=v_q_norm_g, k_norm_g=v_k_norm_g, w_out=v_w_out, ffn_norm_g=v_ffn_norm_g,
                w_gate=v_w_gate, w_up=v_w_up, w_down=v_w_down)
    order = ("meta_tokens", "attn_norm_g", "w_in", "w_pool", "pool_scale", "q_a_norm_g", "w_q_b", "kv_a_norm_g",
             "w_kv_b", "q_norm_g", "k_norm_g", "w_out", "ffn_norm_g", "w_gate", "w_up", "w_down")
    depth = w_in.shape[0]
    seq = x.shape[1]
    length = N_META + seq
    lp = -(-length // ROW_TILE) * ROW_TILE
    in_cols = w_in.shape[2]

    local = {n: (jnp.swapaxes(weights[n], 1, 2) if n in TRANSPOSED else weights[n]) for n in SHARDED}
    cast = cast_bf16([local[n] for n in SHARDED], "cast_weights")
    shards = [{n: cast[k * depth + l] for k, n in enumerate(SHARDED)} for l in range(depth)]
    gathered = all_gather([shards[0][n] for n in BEFORE_ATTENTION] + [meta_tokens], "all_gather")
    g8l = [dict(zip(BEFORE_ATTENTION, gathered[:-1]))] + [{} for _ in range(depth - 1)]
    meta_full = jnp.transpose(gathered[-1], (1, 0, 2)).reshape(N_META, D_MODEL)

    s1, s2, s3 = POOL_WIDTH, POOL_WIDTH + Q_LORA_RANK, POOL_WIDTH + Q_LORA_RANK + KV_LORA_RANK
    zpad = lambda n: jnp.zeros((1, n, D_MODEL), BF16)

    def padded_in_proj(w8):
        w_in_t = jnp.swapaxes(w8, 0, 1).reshape(1, N_DEV * in_cols, D_MODEL)
        return jnp.concatenate([w_in_t[:, :s3], zpad(QK_NOPE_DIM), w_in_t[:, s3:],
                                zpad(HEAD_PAD - QK_HEAD_DIM)], axis=1)

    w_in_ps = [None] * depth
    w_pool_b = w_pool.astype(BF16)

    cosf, sins = _rope_lane_tables(lp)
    seg = _segment_matrix()
    row = lambda a, l: a[l].reshape(1, -1)

    h = jnp.concatenate([meta_full, x[0], jnp.zeros((lp - length, D_MODEL), F32)], axis=0)
    target = jnp.pad(loss_target[0], ((N_META, lp - length), (0, 0)))
    saved = []
    for l in range(depth):
        gq = _pad_lanes(q_norm_g[l] * SCORE_SCALE, 0)
        gkn = _pad_lanes(k_norm_g[l, :QK_NOPE_DIM], 0)
        gkr = _pad_lanes(k_norm_g[l, QK_NOPE_DIM:], QK_NOPE_DIM)
        g8 = g8l[l]
        w_in_ps[l] = padded_in_proj(g8["w_in"])
        a, u, c_q, c_kv, kr = norm_mm(
            h, row(attn_norm_g, l), w_in_ps[l], 0,
            [(0, s1), (s1, Q_LORA_RANK), (s2, KV_LORA_RANK), (s3, HEAD_PAD)], [F32] * 4, "in_proj")
        cat = pool_fwd(u, w_pool_b[l], row(pool_scale, l), "pool_fwd")
        qn, q = norm_mm_heads(c_q, row(q_a_norm_g, l), g8["w_q_b"], 0, "q_proj", transposed=True)
        kvn, kv = norm_mm_heads(c_kv, row(kv_a_norm_g, l), g8["w_kv_b"], 0, "kv_proj")
        qp, kp, v = qk_prep_fwd(q, kv, kr, cosf, sins, seg, gq, gkn, gkr, "qk_prep_fwd")
        riders = [(l, n) for n in SHARDED if n not in g8]
        riders += [(l + 1, n) for n in BEFORE_ATTENTION if l + 1 < depth]
        o, lse, cat, *arrived = attn_fwd(qp, kp, v, cat, "attn_fwd_gather", gather=[shards[k][n] for k, n in riders])
        for (k, n), w8 in zip(riders, arrived):
            g8l[k][n] = w8
        h_mid, g = mm_res(cat, g8["w_out"], h, 0, "out_proj", gamma=row(ffn_norm_g, l))
        gate, up, act = ffn_up(g, g8["w_gate"], g8["w_up"], 0, "ffn_up")
        h_next = mm_res(act, g8["w_down"], h_mid, 0, "ffn_down")[0]
        saved.append(dict(h=h, a=a, u=u, c_q=c_q, c_kv=c_kv, kr=kr, qn=qn, q=q, kvn=kvn, kv=kv, v=v, qp=qp, kp=kp,
                          o=o, lse=lse, cat=cat, h_mid=h_mid, g=g, gate=gate, up=up, act=act,
                          gq=gq, gkn=gkn, gkr=gkr))
        h = h_next

    dh, sq = loss_head(h, target, seq, "loss_head")
    loss = lax.psum(0.5 / D_MODEL * jnp.sum(sq), ("x", "y", "c"))

    core = lax.axis_index("c").astype(jnp.int32).reshape(1)
    small_grads = {n: [None] * depth for n in REPLICATED + ("w_pool",)}
    pending = []
    got = [{} for _ in range(depth)]

    def pair_reduce(l, names, slots, small):
        bigs = [slots[n] for n in names]
        if small is None:
            from_sibling, got_small = exchange_pair(bigs, None, "grad_exchange_pair"), None
        else:
            *from_sibling, got_small = exchange_pair(bigs, small, "grad_exchange_pair_small")
        partial = pair_sum(bigs, list(from_sibling), core, "grad_pair_sum")
        return [(l, n, p) for n, p in zip(names, partial)], got_small

    for l in reversed(range(depth)):
        s = saved[l]
        g8 = g8l[l]
        w_in_p = w_in_ps[l]
        slots = {}
        dgate, dup = ffn_bwd_act(dh, g8["w_down"], 0, s["gate"], s["up"], "ffn_bwd_act")
        slot_of = lambda n, full: full.reshape(g8[n].shape[:1] + g8[n].shape[2:])
        slots["w_down"] = slot_of("w_down", mm_tn([s["act"]], dh, "dw_down", out_dtype=BF16)[0])
        dw_gate_t, dw_up_t = mm_tn([dgate, dup], s["g"], "dw_gate_up", out_dtype=BF16)
        slots["w_gate"] = slot_of("w_gate", dw_gate_t)
        slots["w_up"] = slot_of("w_up", dw_up_t)
        dh_mid, dg_ffn = mm_nt_normbwd([("gathered", dgate, g8["w_gate"]), ("gathered", dup, g8["w_up"])],
                                       s["h_mid"], row(ffn_norm_g, l), dh, 0, "ffn_bwd_in", wide=False)
        small_grads["ffn_norm_g"][l] = dg_ffn[0]
        slots["w_out"] = slot_of("w_out", mm_tn([s["cat"]], dh_mid, "dw_out", out_dtype=BF16)[0])
        early = [slots[n] for n in BEFORE_ATTENTION_BWD]
        dy_pool, do, delta, *from_sibling = out_proj_bwd(dh_mid, g8["w_out"], 0, s["o"], "out_proj_bwd", pair=early)
        partial = pair_sum(early, from_sibling, core, "grad_pair_sum")
        pending += [(l, n, p) for n, p in zip(BEFORE_ATTENTION_BWD, partial)]
        du, dw_pool, dscale = pool_bwd(s["u"], dy_pool, w_pool_b[l], row(pool_scale, l), "pool_bwd")
        small_grads["w_pool"][l] = dw_pool
        small_grads["pool_scale"][l] = dscale[0]
        pool_rider = [jnp.stack(small_grads["w_pool"], axis=0).reshape(-1, LANES)] if l == 0 else []
        dqp, dkp, dv, *arrived = attn_bwd(s["qp"], s["kp"], s["v"], do, s["lse"], delta, "attn_bwd_exchange",
                                          exchange=[p for _, _, p in pending], gather=pool_rider)
        if pool_rider:
            got_pool = arrived.pop()
        for (k, n, _), parts in zip(pending, arrived):
            got[k][n] = parts
        dq, dkv, dkr, dgq, dgkn, dgkr = qk_prep_bwd(dqp, dkp, dv, s["q"], s["kv"], s["kr"], cosf, sins, seg,
                                                    s["gq"], s["gkn"], s["gkr"], "qk_prep_bwd")
        small_grads["q_norm_g"][l] = dgq[0, :QK_HEAD_DIM] * SCORE_SCALE
        small_grads["k_norm_g"][l] = jnp.concatenate([dgkn[0, :QK_NOPE_DIM], dgkr[0, QK_NOPE_DIM:QK_HEAD_DIM]])
        heads_first = lambda full: jnp.swapaxes(full.reshape(full.shape[0], MLA_HEADS, HEAD_PAD), 0, 1)
        dw_q_t = mm_tn([dq], s["qn"], "dw_q", out_dtype=BF16)[0]
        slots["w_q_b"] = dw_q_t.reshape(MLA_HEADS, HEAD_PAD, -1)[:, :QK_HEAD_DIM]
        dc_q, dg_qa = mm_nt_normbwd([("heads_t", dq, g8["w_q_b"])], s["c_q"], row(q_a_norm_g, l), None, 0,
                                    "q_proj_bwd")
        small_grads["q_a_norm_g"][l] = dg_qa[0]
        slots["w_kv_b"] = heads_first(mm_tn([s["kvn"]], dkv, "dw_kv", out_dtype=BF16)[0])
        dc_kv, dg_kva = mm_nt_normbwd([("heads", dkv, g8["w_kv_b"])], s["c_kv"], row(kv_a_norm_g, l), None, 0,
                                      "kv_proj_bwd")
        small_grads["kv_a_norm_g"][l] = dg_kva[0]
        dw_pool_t, dw_q_t, dw_kv_t, dw_rope_t = mm_tn([du, dc_q, dc_kv, dkr], s["a"], "dw_in", out_dtype=BF16)
        dw_in_t = jnp.concatenate([dw_pool_t, dw_q_t, dw_kv_t, dw_rope_t[QK_NOPE_DIM:QK_HEAD_DIM]], axis=0)
        slots["w_in"] = slot_of("w_in", dw_in_t)
        dh, dg_attn = mm_nt_normbwd(
            [("rows", du, w_in_p, 0), ("rows", dc_q, w_in_p, s1), ("rows", dc_kv, w_in_p, s2),
             ("rows", dkr, w_in_p, s3)],
            s["h"], row(attn_norm_g, l), dh_mid, 0, "in_proj_bwd")
        small_grads["attn_norm_g"][l] = dg_attn[0]

        small_slots = None
        if l == 0:
            rep_shapes = [weights[n].shape for n in REPLICATED]
            rep_count = sum(int(jnp.size(weights[n])) for n in REPLICATED)
            rep_rows = -(-rep_count // (8 * LANES)) * 8
            rep_packed = _pack_flat([jnp.stack(small_grads[n], axis=0) for n in REPLICATED], rep_rows)
            meta_slots = jnp.transpose(dh[:N_META].reshape(N_META, N_DEV, LANES), (1, 0, 2))
            small_slots = jnp.concatenate(
                [meta_slots, jnp.broadcast_to(rep_packed[None], (N_DEV, rep_rows, LANES))], axis=1)
        pending, got_small = pair_reduce(l, [n for n in SHARDED if n not in BEFORE_ATTENTION_BWD], slots,
                                         small_slots)
    for (k, n, _), parts in zip(pending, exchange_chips([p for _, _, p in pending], "grad_exchange_chips")):
        got[k][n] = parts

    grad_x = dh[N_META:length][None]

    per = [{} for _ in range(4)]
    for n in SHARDED:
        parts = [got[l][n] for l in range(depth)]
        if n in ADAM_TRANSPOSED:
            t = lambda a: jnp.swapaxes(a, 1, 2)
            outs = [t(o) for o in adamw_shard(parts, local[n], t(mom1[n]), t(mom2[n]), "adamw_" + n)]
        elif n in TRANSPOSED:
            grad = jnp.swapaxes(chip_sum(parts, "chip_sum_" + n), 1, 2)
            outs = adamw_shard(None, weights[n], mom1[n], mom2[n], "adamw_" + n, grad=grad)
        else:
            outs = adamw_shard(parts, weights[n], mom1[n], mom2[n], "adamw_" + n)
        for k in range(4):
            per[k][n] = outs[k]
    ps = lambda src: jnp.concatenate(
        [src["meta_tokens"], _pack_flat([src[n] for n in REPLICATED], rep_rows)], axis=0)
    small_out = adamw_packed(got_small, ps(weights), ps(mom1), ps(mom2), "adamw_small")
    as_rows = lambda a: a.reshape(-1, LANES)
    pool_out = adamw_packed(got_pool, as_rows(w_pool), as_rows(m_w_pool), as_rows(v_w_pool), "adamw_w_pool")
    for k in range(4):
        per[k]["meta_tokens"] = small_out[k][:N_META]
        per[k].update(zip(REPLICATED, _unpack_flat(small_out[k][N_META:], rep_shapes)))
        per[k]["w_pool"] = pool_out[k].reshape(w_pool.shape)
    return (loss, grad_x, *[per[0][n] for n in order], *[per[1][n] for n in order],
            *[per[2][n] for n in order], *[per[3][n] for n in order])
```
